```python
import math
import jax, jax.numpy as jnp
from jax import lax
import numpy as np

D_MODEL = 1024
BATCH = 4
SEQ = 4096
DEPTH = 4

CTX_LEN = 256
GRID_W = 64
NORM_EPS = 1e-6

ML_HEADS = 4
ML_DK = 256
ML_DV = 256
ML_W = ML_HEADS * ML_DV
ML_CHUNK = 128

DA_HEADS = 8
DA_DH = 64
DA_DV = 2 * DA_DH
DA_QK_W = DA_HEADS * 2 * DA_DH
DA_V_W = DA_HEADS * DA_DV
Q_BLOCK = 128
ROPE_BASE = 10000.0

FN_GROUPS = 4
FN_GC = 256
FN_W = FN_GROUPS * FN_GC

N_BRANCH = 3
D_FF = -(-8 * D_MODEL // (3 * 256)) * 256

IN_SPLITS = (ML_DK * ML_HEADS, ML_DK * ML_HEADS, ML_W, ML_W, 4 * ML_HEADS,
             DA_QK_W, DA_QK_W, DA_V_W, FN_W, N_BRANCH * D_MODEL)
D_IN = sum(IN_SPLITS)

kernel_name = 'hybrid_mlstm_diffattn_fnet_dit'


def _rmsnorm(x, g):
    xf = x.astype(jnp.float32)
    y = xf * lax.rsqrt(jnp.mean(xf * xf, axis=-1, keepdims=True) + NORM_EPS)
    return (y * g.astype(jnp.float32)).astype(x.dtype)


def _modulate(h, shift, scale):
    return h * (1 + scale) + shift


def _split_cols(u):
    idx, acc = [], 0
    for s in IN_SPLITS[:-1]:
        acc += s
        idx.append(acc)
    return jnp.split(u, idx, axis=-1)


def _axial_rope(rows):
    n_freq = DA_DH // 4
    inv = ROPE_BASE ** (-jnp.arange(n_freq, dtype=jnp.float32) / n_freq)
    r = jnp.repeat(jnp.arange(rows, dtype=jnp.float32), GRID_W)
    col = jnp.tile(jnp.arange(GRID_W, dtype=jnp.float32), rows)
    ang = jnp.concatenate([r[:, None] * inv, col[:, None] * inv], axis=-1)
    return jnp.cos(ang), jnp.sin(ang)


def _rope(x, cos, sin):
    shp = (cos.shape[0],) + (1,) * (x.ndim - 3) + (cos.shape[1],)
    c = cos.reshape(shp).astype(x.dtype)
    s = sin.reshape(shp).astype(x.dtype)
    x1, x2 = jnp.split(x, 2, axis=-1)
    return jnp.concatenate([x1 * c - x2 * s, x1 * s + x2 * c], axis=-1)


def _mlstm_scan(q, k, v, i_pre, f_pre, state):
    B, T, H, DK = q.shape
    DV = v.shape[-1]
    L = ML_CHUNK
    NC = T // L

    def chunks(a):
        a = a.astype(jnp.float32).reshape((B, NC, L, H) + a.shape[3:])
        return jnp.moveaxis(jnp.moveaxis(a, 1, 0), 3, 2)

    tri = jnp.tril(jnp.ones((L, L), dtype=bool))

    def step(carry, xs):
        C, n, m = carry
        qc, kc, vc, ic, fc = xs
        b = jnp.cumsum(jax.nn.log_sigmoid(fc), axis=-1)
        dmat = jnp.where(tri, b[..., :, None] - b[..., None, :] + ic[..., None, :], -jnp.inf)
        inter = b + m[..., None]
        m_t = jnp.maximum(inter, jnp.max(dmat, axis=-1))
        a = jnp.exp(dmat - m_t[..., None]) * jnp.einsum('bhtd,bhsd->bhts', qc, kc)
        sc = jnp.exp(inter - m_t)
        num = sc[..., None] * jnp.einsum('bhtd,bhde->bhte', qc, C) + jnp.einsum('bhts,bhse->bhte', a, vc)
        den = sc * jnp.einsum('bhtd,bhd->bht', qc, n) + jnp.sum(a, axis=-1)
        h = num / jnp.maximum(jnp.abs(den), jnp.exp(-m_t))[..., None]
        b_end = b[..., -1]
        g_s = b_end[..., None] - b + ic
        m_new = jnp.maximum(b_end + m, jnp.max(g_s, axis=-1))
        decay = jnp.exp(b_end + m - m_new)
        kw = kc * jnp.exp(g_s - m_new[..., None])[..., None]
        C_new = decay[..., None, None] * C + jnp.einsum('bhsd,bhse->bhde', kw, vc)
        n_new = decay[..., None] * n + jnp.sum(kw, axis=2)
        return (C_new, n_new, m_new), h

    final, hs = lax.scan(step, state, (chunks(q), chunks(k), chunks(v), chunks(i_pre), chunks(f_pre)))
    h = jnp.transpose(hs, (1, 0, 3, 2, 4)).reshape(B, T, H, DV)
    return h, final


def _mlstm_branch(ul, uc, gate_b, head_g):
    def prep(u):
        B, T, _ = u[0].shape
        q = u[0].reshape(B, T, ML_HEADS, ML_DK)
        k = u[1].reshape(B, T, ML_HEADS, ML_DK) * (ML_DK ** -0.5)
        v = u[2].reshape(B, T, ML_HEADS, ML_DV)
        i_f, f_f, i_b, f_b = jnp.split(u[4] + gate_b, 4, axis=-1)
        return q, k, v, i_f, f_f, i_b, f_b

    flip = lambda a: jnp.flip(a, axis=1)
    ql, kl, vl, ifl, ffl, ibl, fbl = prep(ul)
    qc, kc, vc, ifc, ffc, ibc, fbc = prep(uc)
    B = ql.shape[0]
    zero = (jnp.zeros((B, ML_HEADS, ML_DK, ML_DV), jnp.float32),
            jnp.zeros((B, ML_HEADS, ML_DK), jnp.float32),
            jnp.zeros((B, ML_HEADS), jnp.float32))
    hcf, st_f = _mlstm_scan(qc, kc, vc, ifc, ffc, zero)
    hlf, _ = _mlstm_scan(ql, kl, vl, ifl, ffl, st_f)
    hcb, st_b = _mlstm_scan(flip(qc), flip(kc), flip(vc), flip(ibc), flip(fbc), zero)
    hlb, _ = _mlstm_scan(flip(ql), flip(kl), flip(vl), flip(ibl), flip(fbl), st_b)

    def out(hf, hb, u):
        B, T = hf.shape[:2]
        h = _rmsnorm(hf + hb, head_g.reshape(ML_HEADS, ML_DV)).reshape(B, T, ML_W)
        return h.astype(u[3].dtype) * jax.nn.sigmoid(u[3])

    return out(hlf, flip(hlb), ul), out(hcf, flip(hcb), uc)


def _diff_attn_core(q, k, v, lam):
    s = jnp.einsum('bqhmd,bkhmd->bhmqk', q, k).astype(jnp.float32) * (DA_DH ** -0.5)
    p = jax.nn.softmax(s, axis=-1)
    a = p[:, :, 0] - lam * p[:, :, 1]
    return jnp.einsum('bhqk,bkhe->bqhe', a.astype(v.dtype), v)


def _diff_attn_branch(ul, uc, lam_qk, head_g, cos, sin, lam_init, need_ctx):
    def qkv(u):
        B, T, _ = u[5].shape
        return (u[5].reshape(B, T, DA_HEADS, 2, DA_DH),
                u[6].reshape(B, T, DA_HEADS, 2, DA_DH),
                u[7].reshape(B, T, DA_HEADS, DA_DV))

    lq = lam_qk.astype(jnp.float32)
    lam = jnp.exp(jnp.sum(lq[0] * lq[1])) - jnp.exp(jnp.sum(lq[2] * lq[3])) + lam_init
    ql, kl, vl = qkv(ul)
    qc, kc, vc = qkv(uc)
    ql, kl = _rope(ql, cos, sin), _rope(kl, cos, sin)
    k_all = jnp.concatenate([kl, kc], axis=1)
    v_all = jnp.concatenate([vl, vc], axis=1)
    B, T = ql.shape[:2]
    nb = T // Q_BLOCK
    qb = jnp.moveaxis(ql.reshape(B, nb, Q_BLOCK, DA_HEADS, 2, DA_DH), 1, 0)
    ob = lax.map(lambda qq: _diff_attn_core(qq, k_all, v_all, lam), qb)
    ol = jnp.moveaxis(ob, 0, 1).reshape(B, T, DA_HEADS, DA_DV)

    def post(o):
        Bo, To = o.shape[:2]
        return (_rmsnorm(o, head_g.reshape(DA_HEADS, DA_DV)) * (1.0 - lam_init)).reshape(Bo, To, DA_V_W)

    oc = post(_diff_attn_core(qc, kc, vc, lam)) if need_ctx else None
    return post(ol), oc


def _fourier(u):
    B, T, _ = u.shape
    z = u.astype(jnp.float32).reshape(B, T, FN_GROUPS, FN_GC)
    y = jnp.fft.fftn(z, axes=(1, 3), norm='ortho').real
    return y.reshape(B, T, FN_W).astype(u.dtype)


def _merge(ym, yd, yf, g_pre, w_br_ml, w_br_da, w_br_fn, w_out):
    gm, gd, gf = jnp.split(jax.nn.sigmoid(g_pre), N_BRANCH, axis=-1)
    y = gm * (ym @ w_br_ml) + gd * (yd @ w_br_da) + gf * (yf @ w_br_fn)
    return y @ w_out


def _mixer(hl, hc, w_in, ml_gate_b, ml_head_g, da_lam, da_head_g, w_br_ml, w_br_da, w_br_fn, w_out,
           cos, sin, lam_init, need_ctx):
    ul = _split_cols(hl @ w_in)
    uc = _split_cols(hc @ w_in)
    ml_l, ml_c = _mlstm_branch(ul, uc, ml_gate_b, ml_head_g)
    da_l, da_c = _diff_attn_branch(ul, uc, da_lam, da_head_g, cos, sin, lam_init, need_ctx)
    yl = _merge(ml_l, da_l, _fourier(ul[8]), ul[9], w_br_ml, w_br_da, w_br_fn, w_out)
    yc = _merge(ml_c, da_c, _fourier(uc[8]), uc[9], w_br_ml, w_br_da, w_br_fn, w_out) if need_ctx else None
    return yl, yc


def _swiglu(h, w_in, w_out):
    a, b = jnp.split(h @ w_in, 2, axis=-1)
    return (jax.nn.silu(a) * b) @ w_out


def setup_inputs(seed: int = 0) -> dict:
    key = jax.random.key(seed)
    ks = jax.random.split(key, 20)
    f32 = jnp.float32
    D = D_MODEL

    def nrm(k, shape, s):
        return jax.random.normal(k, shape, f32) * s

    f_mask = jnp.array([0.0, 1.0, 0.0, 1.0], f32)[None, :, None]
    f_bias = jnp.linspace(3.0, 6.0, ML_HEADS, dtype=f32)[None, None, :]
    ml_gate_b = (nrm(ks[8], (DEPTH, 4, ML_HEADS), 0.1) + f_mask * f_bias).reshape(DEPTH, 4 * ML_HEADS)
    return {
        'x': nrm(ks[0], (BATCH, SEQ, D), 1.0),
        'c': nrm(ks[1], (BATCH, D), 1.0),
        'ctx': nrm(ks[2], (BATCH, CTX_LEN, D), 1.0),
        'c_ctx': nrm(ks[3], (D,), 1.0),
        'w_ada': nrm(ks[4], (DEPTH, D, 6 * D), 0.5 * D ** -0.5),
        'b_ada': nrm(ks[5], (DEPTH, 6 * D), 0.02),
        'norm_g': 1.0 + nrm(ks[6], (DEPTH, 2, D), 0.02),
        'w_in': nrm(ks[7], (DEPTH, D, D_IN), D ** -0.5),
        'ml_gate_b': ml_gate_b,
        'ml_head_g': 1.0 + nrm(ks[9], (DEPTH, ML_W), 0.02),
        'da_lam': nrm(ks[10], (DEPTH, 4, DA_DH), 0.1),
        'da_head_g': 1.0 + nrm(ks[11], (DEPTH, DA_V_W), 0.02),
        'w_br_ml': nrm(ks[12], (DEPTH, ML_W, D), ML_W ** -0.5),
        'w_br_da': nrm(ks[13], (DEPTH, DA_V_W, D), DA_V_W ** -0.5),
        'w_br_fn': nrm(ks[14], (DEPTH, FN_W, D), FN_W ** -0.5),
        'w_out': nrm(ks[15], (DEPTH, D, D), D ** -0.5),
        'w_ffn_in': nrm(ks[16], (DEPTH, D, 2 * D_FF), D ** -0.5),
        'w_ffn_out': nrm(ks[17], (DEPTH, D_FF, D), D_FF ** -0.5),
        'final_g': 1.0 + nrm(ks[18], (D,), 0.02),
    }


def reference(x, c, ctx, c_ctx, w_ada, b_ada, norm_g, w_in, ml_gate_b, ml_head_g, da_lam, da_head_g,
              w_br_ml, w_br_da, w_br_fn, w_out, w_ffn_in, w_ffn_out, final_g):
    T = x.shape[1]
    ROWS = T // GRID_W
    cos, sin = _axial_rope(ROWS)
    s_lat = jax.nn.silu(c)
    s_ctx = jax.nn.silu(c_ctx)
    xl, xc = x, ctx
    for l in range(DEPTH):
        need_ctx = l < DEPTH - 1
        lam_init = 0.8 - 0.6 * math.exp(-0.3 * l)
        ml = jnp.split((s_lat @ w_ada[l] + b_ada[l])[:, None, :], 6, axis=-1)
        mc = jnp.split((s_ctx @ w_ada[l] + b_ada[l])[None, None, :], 6, axis=-1)
        hl = _modulate(_rmsnorm(xl, norm_g[l, 0]), ml[0], ml[1])
        hc = _modulate(_rmsnorm(xc, norm_g[l, 0]), mc[0], mc[1])
        yl, yc = _mixer(hl, hc, w_in[l], ml_gate_b[l], ml_head_g[l], da_lam[l], da_head_g[l],
                        w_br_ml[l], w_br_da[l], w_br_fn[l], w_out[l], cos, sin, lam_init, need_ctx)
        xl = xl + ml[2] * yl
        xl = xl + ml[5] * _swiglu(_modulate(_rmsnorm(xl, norm_g[l, 1]), ml[3], ml[4]), w_ffn_in[l], w_ffn_out[l])
        if need_ctx:
            xc = xc + mc[2] * yc
            xc = xc + mc[5] * _swiglu(_modulate(_rmsnorm(xc, norm_g[l, 1]), mc[3], mc[4]), w_ffn_in[l], w_ffn_out[l])
    return _rmsnorm(xl, final_g)
```

```python
import functools
import math

import jax
import jax.numpy as jnp
from jax import lax
from jax.experimental import pallas as pl
from jax.experimental.pallas import tpu as pltpu

D_MODEL = 1024
SEQ = 4096
DEPTH = 4
CTX_LEN = 256
GRID_W = 64
NORM_EPS = 1e-6

ML_HEADS = 4
ML_DK = 256
ML_CHUNK = 128

DA_HEADS = 8
DA_DH = 64
DA_DV = 2 * DA_DH
ROPE_BASE = 10000.0

FN_GROUPS = 4
FN_GC = 256
FFT_R = 64

D_FF = 2816
N_GATE = 4 * ML_HEADS
N_MOD = 6 * D_MODEL

U_QML, U_KML, U_VML, U_OML, U_QDA, U_KDA, U_VDA, U_GPRE = 0, 1, 2, 3, 4, 5, 6, 7
U_BLOCKS = 10
W_BLOCKS = U_BLOCKS + 1

VMEM_LIMIT_V7X = 56 * 1024 * 1024

BF16 = jnp.bfloat16
F32 = jnp.float32


def _cparams(sem):
    return pltpu.CompilerParams(dimension_semantics=sem, vmem_limit_bytes=VMEM_LIMIT_V7X)


def _dot(a, b):
    return jnp.dot(a, b, preferred_element_type=F32)


def _dot_nt(a, b):
    return lax.dot_general(a, b, (((1,), (1,)), ((), ())), preferred_element_type=F32)


def _dot_tn(a, b):
    return lax.dot_general(a, b, (((0,), (0,)), ((), ())), preferred_element_type=F32)


def _mod_row(mod_ref, i, rows_per_batch, n_batch, col):
    r = jnp.minimum(i // rows_per_batch, n_batch)
    return mod_ref[pl.ds(r, 1), col * D_MODEL:(col + 1) * D_MODEL]


def _row_tile(n_batch):
    return min(512, n_batch * CTX_LEN)


def _rms(x, g):
    return x * lax.rsqrt(jnp.mean(x * x, axis=-1, keepdims=True) + NORM_EPS) * g


def _mods_kernel(c_ref, w_ref, b_ref, o_ref):
    c = c_ref[...]
    s = (c * jax.nn.sigmoid(c)).astype(BF16)
    o_ref[0] = _dot(s, w_ref[0].astype(BF16)) + b_ref[0]


def _mods(cc, w_ada, b_ada):
    tn = 1536
    return pl.pallas_call(
        _mods_kernel,
        grid=(DEPTH, N_MOD // tn),
        in_specs=[
            pl.BlockSpec((8, D_MODEL), lambda l, j: (0, 0)),
            pl.BlockSpec((1, D_MODEL, tn), lambda l, j: (l, 0, j)),
            pl.BlockSpec((1, 1, tn), lambda l, j: (l, 0, j)),
        ],
        out_specs=pl.BlockSpec((1, 8, tn), lambda l, j: (l, 0, j)),
        out_shape=jax.ShapeDtypeStruct((DEPTH, 8, N_MOD), F32),
        compiler_params=_cparams(("arbitrary", "arbitrary")),
        name="adaln_mods",
    )(cc, w_ada, b_ada.reshape(DEPTH, 1, N_MOD))


def _inproj_kernel(x_ref, mod_ref, g_ref, w_ref, wg_ref, cos_ref, sin_ref,
                   u_ref, fn_ref, gate_ref, xn_ref, *, rows_per_batch, n_batch):
    i = pl.program_id(0)
    j = pl.program_id(1)

    @pl.when(j == 0)
    def _():
        y = _rms(x_ref[...], g_ref[...])
        shift = _mod_row(mod_ref, i, rows_per_batch, n_batch, 0)
        scale = _mod_row(mod_ref, i, rows_per_batch, n_batch, 1)
        xn_ref[...] = (y * (1.0 + scale) + shift).astype(BF16)
        gate_ref[...] = _dot(xn_ref[...], wg_ref[...])

    acc = _dot(xn_ref[...], w_ref[...])

    is_rope = jnp.logical_or(j == U_QDA, j == U_KDA)
    is_plain = jnp.logical_and(jnp.logical_not(is_rope), jnp.logical_and(j != U_KML, j < U_BLOCKS))

    @pl.when(is_plain)
    def _():
        u_ref[...] = acc.astype(BF16)

    @pl.when(j == U_KML)
    def _():
        u_ref[...] = (acc * (ML_DK ** -0.5)).astype(BF16)

    @pl.when(is_rope)
    def _():
        n = acc.shape[1]
        lane = lax.broadcasted_iota(jnp.int32, (1, n), 1)
        first_half = (lane % DA_DH) < (DA_DH // 2)
        swapped = jnp.where(first_half, pltpu.roll(acc, n - DA_DH // 2, 1), pltpu.roll(acc, DA_DH // 2, 1))
        cos = cos_ref[...]
        sin = sin_ref[...]
        for t in range(n // 128):
            sl = slice(t * 128, (t + 1) * 128)
            u_ref[:, sl] = (acc[:, sl] * cos + swapped[:, sl] * sin).astype(BF16)

    @pl.when(j == U_BLOCKS)
    def _():
        for g in range(FN_GROUPS):
            fn_ref[g] = acc[:, g * FN_GC:(g + 1) * FN_GC].astype(BF16)


def _inproj(x, mods, g, w_main, w_gate, cos_t, sin_t, n_batch):
    n_rows = x.shape[0]
    tm = n_batch * CTX_LEN
    assert SEQ % tm == 0 and cos_t.shape[0] == SEQ + tm
    ni = n_rows // tm
    rows_per_batch = SEQ // tm
    lat_blocks = n_batch * rows_per_batch

    def tab_idx(i, j):
        return (jnp.where(i < lat_blocks, i % rows_per_batch, rows_per_batch), 0)

    kern = functools.partial(_inproj_kernel, rows_per_batch=rows_per_batch, n_batch=n_batch)
    return pl.pallas_call(
        kern,
        grid=(ni, W_BLOCKS),
        in_specs=[
            pl.BlockSpec((tm, D_MODEL), lambda i, j: (i, 0)),
            pl.BlockSpec((8, N_MOD), lambda i, j: (0, 0)),
            pl.BlockSpec((1, D_MODEL), lambda i, j: (0, 0)),
            pl.BlockSpec((D_MODEL, D_MODEL), lambda i, j: (0, j)),
            pl.BlockSpec((D_MODEL, N_GATE), lambda i, j: (0, 0)),
            pl.BlockSpec((tm, 128), tab_idx),
            pl.BlockSpec((tm, 128), tab_idx),
        ],
        out_specs=[
            pl.BlockSpec((tm, D_MODEL), lambda i, j: (i, jnp.minimum(j, U_BLOCKS - 1))),
            pl.BlockSpec((FN_GROUPS, tm, FN_GC), lambda i, j: (0, i, 0)),
            pl.BlockSpec((tm, N_GATE), lambda i, j: (i, 0)),
        ],
        out_shape=[
            jax.ShapeDtypeStruct((n_rows, U_BLOCKS * D_MODEL), BF16),
            jax.ShapeDtypeStruct((FN_GROUPS, n_rows, FN_GC), BF16),
            jax.ShapeDtypeStruct((n_rows, N_GATE), F32),
        ],
        scratch_shapes=[pltpu.VMEM((tm, D_MODEL), BF16)],
        compiler_params=_cparams(("arbitrary", "arbitrary")),
        name="in_proj",
    )(x, mods, g, w_main, w_gate, cos_t, sin_t)


def _mlstm_kernel(q_ref, k_ref, v_ref, gt_ref, gb_ref, h_ref, c_ref, n_ref, m_ref):
    d = pl.program_id(1)
    s = pl.program_id(2)
    L = ML_CHUNK

    @pl.when(s == 0)
    def _():
        c_ref[...] = jnp.zeros_like(c_ref)
        n_ref[...] = jnp.zeros_like(n_ref)
        m_ref[...] = jnp.zeros_like(m_ref)

    t_idx = lax.broadcasted_iota(jnp.int32, (L, L), 0)
    s_idx = lax.broadcasted_iota(jnp.int32, (L, L), 1)
    eye = t_idx == s_idx
    causal = ((s_idx - t_idx) * (1 - 2 * d)) <= 0
    gates = gt_ref[...] + gb_ref[...]
    is_fwd = d == 0

    for h in range(ML_HEADS):
        hs = slice(h * ML_DK, (h + 1) * ML_DK)
        i_col = jnp.where(is_fwd, gates[:, h:h + 1], gates[:, 2 * ML_HEADS + h:2 * ML_HEADS + h + 1])
        f_col = jnp.where(is_fwd, gates[:, ML_HEADS + h:ML_HEADS + h + 1],
                          gates[:, 3 * ML_HEADS + h:3 * ML_HEADS + h + 1])
        lf_col = jnp.minimum(f_col, 0.0) - jnp.log1p(jnp.exp(-jnp.abs(f_col)))
        lf_row = jnp.sum(jnp.where(eye, lf_col, 0.0), axis=0, keepdims=True)
        i_row = jnp.sum(jnp.where(eye, i_col, 0.0), axis=0, keepdims=True)
        b_col = jnp.sum(jnp.where(causal, lf_row, 0.0), axis=1, keepdims=True)
        b_row = jnp.sum(jnp.where(causal, 0.0, lf_col) + jnp.where(eye, lf_col, 0.0), axis=0, keepdims=True)
        b_end = jnp.sum(lf_row, axis=1, keepdims=True)

        m_prev = m_ref[h][:, 0:1]
        dmat = jnp.where(causal, b_col - b_row + i_row, -jnp.inf)
        inter = b_col + m_prev
        m_t = jnp.maximum(inter, jnp.max(dmat, axis=1, keepdims=True))

        q = q_ref[:, hs]
        k = k_ref[:, hs]
        v = v_ref[:, hs]
        a = jnp.exp(dmat - m_t) * _dot_nt(q, k)
        sc = jnp.exp(inter - m_t)
        c_prev = c_ref[h]
        n_prev = n_ref[h]
        num = sc * _dot(q, c_prev.astype(BF16)) + _dot(a.astype(BF16), v)
        qn = jnp.sum(q.astype(F32) * n_prev, axis=1, keepdims=True)
        den = sc * qn + jnp.sum(a, axis=1, keepdims=True)
        inv = 1.0 / jnp.maximum(jnp.abs(den), jnp.exp(-m_t))
        h_ref[0, :, hs] = (num * inv).astype(h_ref.dtype)

        g_col = b_end - b_col + i_col
        m_new = jnp.maximum(b_end + m_prev, jnp.max(g_col, axis=0, keepdims=True))
        decay = jnp.exp(b_end + m_prev - m_new)
        kw = k.astype(F32) * jnp.exp(g_col - m_new)
        c_ref[h] = decay * c_prev + _dot_tn(kw.astype(BF16), v)
        n_ref[h] = decay * n_prev + jnp.sum(kw, axis=0, keepdims=True)
        m_ref[h] = jnp.broadcast_to(m_new, (1, 128))


def _mlstm(u, gates, gate_b, n_batch):
    n_rows = u.shape[0]
    L = ML_CHUNK
    lat_chunks = SEQ // L
    ctx_chunks = CTX_LEN // L
    n_steps = ctx_chunks + lat_chunks
    ctx_base = n_batch * lat_chunks

    def rowblk(b, d, s):
        in_ctx = s < ctx_chunks
        c_f = jnp.where(in_ctx, s, s - ctx_chunks)
        c_b = jnp.where(in_ctx, ctx_chunks - 1 - s, lat_chunks - 1 - (s - ctx_chunks))
        base = jnp.where(in_ctx, ctx_base + ctx_chunks * b, lat_chunks * b)
        return base + jnp.where(d == 0, c_f, c_b)

    def col_spec(col):
        return pl.BlockSpec((L, D_MODEL), lambda b, d, s: (rowblk(b, d, s), col))

    return pl.pallas_call(
        _mlstm_kernel,
        grid=(n_batch, 2, n_steps),
        in_specs=[
            col_spec(U_QML), col_spec(U_KML), col_spec(U_VML),
            pl.BlockSpec((L, N_GATE), lambda b, d, s: (rowblk(b, d, s), 0)),
            pl.BlockSpec((1, N_GATE), lambda b, d, s: (0, 0)),
        ],
        out_specs=pl.BlockSpec((1, L, D_MODEL), lambda b, d, s: (d, rowblk(b, d, s), 0)),
        out_shape=jax.ShapeDtypeStruct((2, n_rows, D_MODEL), BF16),
        scratch_shapes=[
            pltpu.VMEM((ML_HEADS, ML_DK, ML_DK), F32),
            pltpu.VMEM((ML_HEADS, 1, ML_DK), F32),
            pltpu.VMEM((ML_HEADS, 1, 128), F32),
        ],
        compiler_params=_cparams(("arbitrary", "arbitrary", "arbitrary")),
        name="mlstm_scan",
    )(u, u, u, gates, gate_b)


def _attn_kernel(q_ref, kl_ref, vl_ref, kc_ref, vc_ref, lam_ref, g_ref, o_ref, *, lam_init, lat_qblocks):
    qi = pl.program_id(2)
    lq = lam_ref[...]
    lam = (jnp.exp(jnp.sum(lq[0:1] * lq[1:2], axis=1, keepdims=True))
           - jnp.exp(jnp.sum(lq[2:3] * lq[3:4], axis=1, keepdims=True)) + lam_init)
    q = q_ref[...]
    lane = lax.broadcasted_iota(jnp.int32, (1, DA_DV), 1)
    scale = DA_DH ** -0.5
    zero = jnp.zeros_like(q)
    q_maps = (jnp.where(lane < DA_DH, q, zero) * scale, jnp.where(lane >= DA_DH, q, zero) * scale)

    def finish(o):
        o_ref[...] = (_rms(o, g_ref[...]) * (1.0 - lam_init)).astype(o_ref.dtype)

    @pl.when(qi < lat_qblocks)
    def _():
        kl = kl_ref[...]
        kc = kc_ref[...]
        a_l, a_c = None, None
        for mi, qm in enumerate(q_maps):
            sl = _dot_nt(qm, kl)
            sc = _dot_nt(qm, kc)
            mx = jnp.maximum(jnp.max(sl, axis=1, keepdims=True), jnp.max(sc, axis=1, keepdims=True))
            pl_ = jnp.exp(sl - mx)
            pc_ = jnp.exp(sc - mx)
            r = 1.0 / (jnp.sum(pl_, axis=1, keepdims=True) + jnp.sum(pc_, axis=1, keepdims=True))
            w = r if mi == 0 else -(lam * r)
            a_l = pl_ * w if a_l is None else a_l + pl_ * w
            a_c = pc_ * w if a_c is None else a_c + pc_ * w
        finish(_dot(a_l.astype(BF16), vl_ref[...]) + _dot(a_c.astype(BF16), vc_ref[...]))

    @pl.when(qi >= lat_qblocks)
    def _():
        kc = kc_ref[...]
        a_c = None
        for mi, qm in enumerate(q_maps):
            sc = _dot_nt(qm, kc)
            pc_ = jnp.exp(sc - jnp.max(sc, axis=1, keepdims=True))
            r = 1.0 / jnp.sum(pc_, axis=1, keepdims=True)
            w = r if mi == 0 else -(lam * r)
            a_c = pc_ * w if a_c is None else a_c + pc_ * w
        finish(_dot(a_c.astype(BF16), vc_ref[...]))


def _attn(u, da_lam, head_g, lam_init, need_ctx, n_batch):
    n_rows = u.shape[0]
    tq = CTX_LEN
    lat_qblocks = SEQ // tq
    nq = lat_qblocks + (1 if need_ctx else 0)
    ctx_base = n_batch * lat_qblocks
    cpb = D_MODEL // DA_DV

    def qrow(b, qi):
        return jnp.where(qi < lat_qblocks, b * lat_qblocks + qi, ctx_base + b)

    kern = functools.partial(_attn_kernel, lam_init=lam_init, lat_qblocks=lat_qblocks)
    return pl.pallas_call(
        kern,
        grid=(n_batch, DA_HEADS, nq),
        in_specs=[
            pl.BlockSpec((tq, DA_DV), lambda b, h, qi: (qrow(b, qi), U_QDA * cpb + h)),
            pl.BlockSpec((SEQ, DA_DV), lambda b, h, qi: (b, U_KDA * cpb + h)),
            pl.BlockSpec((SEQ, DA_DV), lambda b, h, qi: (b, U_VDA * cpb + h)),
            pl.BlockSpec((CTX_LEN, DA_DV), lambda b, h, qi: (ctx_base + b, U_KDA * cpb + h)),
            pl.BlockSpec((CTX_LEN, DA_DV), lambda b, h, qi: (ctx_base + b, U_VDA * cpb + h)),
            pl.BlockSpec((4, DA_DH), lambda b, h, qi: (0, 0)),
            pl.BlockSpec((1, DA_DV), lambda b, h, qi: (0, h)),
        ],
        out_specs=pl.BlockSpec((tq, DA_DV), lambda b, h, qi: (qrow(b, qi), h)),
        out_shape=jax.ShapeDtypeStruct((n_rows, D_MODEL), BF16),
        compiler_params=_cparams(("arbitrary", "arbitrary", "arbitrary")),
        name="diff_attn",
    )(u, u, u, u, u, da_lam, head_g)


def _dft_tables():
    R = FFT_R

    def cs(num, period):
        ang = (num % period).astype(F32) * (2.0 * math.pi / period)
        return jnp.cos(ang), jnp.sin(ang)

    idx = jnp.arange(R, dtype=jnp.int32)
    c1, s1 = cs(idx[:, None] * idx[None, :], R)
    a1 = jnp.concatenate([c1, -s1], axis=0).astype(BF16)
    f2 = idx[:, None, None]
    f1 = idx[None, :, None]
    t1 = idx[None, None, :]
    mc, ms = cs(t1 * (R * f1 + f2), SEQ)
    b2 = jnp.concatenate([jnp.concatenate([mc, ms], axis=2),
                          jnp.concatenate([-ms, mc], axis=2)], axis=1).astype(BF16)
    ch = jnp.arange(FN_GC, dtype=jnp.int32)
    cc, sc = cs(ch[:, None] * ch[None, :], FN_GC)
    csm = jnp.concatenate([cc, sc], axis=0).astype(BF16)
    actx = jnp.concatenate([cc, -sc], axis=0).astype(BF16)
    return a1, b2, csm, actx


def _fft1_kernel(a_ref, z_ref, p_ref):
    p_ref[0, 0] = _dot(a_ref[...], z_ref[0])


def _fft1(fn_view, a1, n_batch):
    R = FFT_R
    lanes = R * FN_GC
    lc = 4096
    return pl.pallas_call(
        _fft1_kernel,
        grid=(FN_GROUPS, n_batch, lanes // lc),
        in_specs=[
            pl.BlockSpec((2 * R, R), lambda g, b, c: (0, 0)),
            pl.BlockSpec((1, R, lc), lambda g, b, c: (g, b, c)),
        ],
        out_specs=pl.BlockSpec((1, 1, 2 * R, lc), lambda g, b, c: (g, b, 0, c)),
        out_shape=jax.ShapeDtypeStruct((FN_GROUPS, n_batch, 2 * R, lanes), F32),
        compiler_params=_cparams(("arbitrary", "arbitrary", "arbitrary")),
        name="fourier_stage1",
    )(a1, fn_view)


def _fft2_kernel(p_ref, b_ref, cs_ref, o_ref, *, f2b):
    R = FFT_R
    cc = cs_ref[0:FN_GC, :]
    sc = cs_ref[FN_GC:2 * FN_GC, :]
    norm = 1.0 / math.sqrt(SEQ * FN_GC)
    for g in range(FN_GROUPS):
        for jj in range(f2b):
            stacked = jnp.concatenate([p_ref[g, 0, 0, jj], p_ref[g, 0, 1, jj]], axis=0).astype(BF16)
            x = _dot(b_ref[jj], stacked)
            y = _dot(x[0:R].astype(BF16), cc) + _dot(x[R:2 * R].astype(BF16), sc)
            lo = jj * D_MODEL + g * FN_GC
            o_ref[:, lo:lo + FN_GC] = (y * norm).astype(o_ref.dtype)


def _fft2(p6, b2, csm, n_rows, n_batch):
    R = FFT_R
    f2b = 8
    kern = functools.partial(_fft2_kernel, f2b=f2b)
    return pl.pallas_call(
        kern,
        grid=(n_batch, R // f2b),
        in_specs=[
            pl.BlockSpec((FN_GROUPS, 1, 2, f2b, R, FN_GC), lambda b, f: (0, b, 0, f, 0, 0)),
            pl.BlockSpec((f2b, 2 * R, 2 * R), lambda b, f: (f, 0, 0)),
            pl.BlockSpec((2 * FN_GC, FN_GC), lambda b, f: (0, 0)),
        ],
        out_specs=pl.BlockSpec((R, f2b * D_MODEL), lambda b, f: (b, f)),
        out_shape=jax.ShapeDtypeStruct((n_rows // R, R * D_MODEL), BF16),
        compiler_params=_cparams(("arbitrary", "arbitrary")),
        name="fourier_stage2",
    )(p6, b2, csm)


def _fft_ctx_kernel(z_ref, a_ref, cs_ref, yin_ref, o_ref):
    del yin_ref
    cc = cs_ref[0:FN_GC, :]
    sc = cs_ref[FN_GC:2 * FN_GC, :]
    norm = 1.0 / math.sqrt(CTX_LEN * FN_GC)
    for g in range(FN_GROUPS):
        p = _dot(a_ref[...], z_ref[g])
        y = _dot(p[0:CTX_LEN].astype(BF16), cc) + _dot(p[CTX_LEN:2 * CTX_LEN].astype(BF16), sc)
        o_ref[:, g * FN_GC:(g + 1) * FN_GC] = (y * norm).astype(o_ref.dtype)


def _fft_ctx(fn, actx, csm, yf, n_batch):
    n_rows = yf.shape[0]
    ctx_base = n_batch * (SEQ // CTX_LEN)
    return pl.pallas_call(
        _fft_ctx_kernel,
        grid=(n_batch,),
        in_specs=[
            pl.BlockSpec((FN_GROUPS, CTX_LEN, FN_GC), lambda b: (0, ctx_base + b, 0)),
            pl.BlockSpec((2 * CTX_LEN, CTX_LEN), lambda b: (0, 0)),
            pl.BlockSpec((2 * FN_GC, FN_GC), lambda b: (0, 0)),
            pl.BlockSpec(memory_space=pl.ANY),
        ],
        out_specs=pl.BlockSpec((CTX_LEN, D_MODEL), lambda b: (ctx_base + b, 0)),
        out_shape=jax.ShapeDtypeStruct((n_rows, D_MODEL), BF16),
        input_output_aliases={3: 0},
        compiler_params=_cparams(("arbitrary",)),
        name="fourier_ctx",
    )(fn, actx, csm, yf)


def _fourier(fn, tables, need_ctx, n_batch):
    a1, b2, csm, actx = tables
    n_rows = fn.shape[1]
    R = FFT_R
    p = _fft1(fn.reshape(FN_GROUPS, n_rows // R, R * FN_GC), a1, n_batch)
    yf = _fft2(p.reshape(FN_GROUPS, n_batch, 2, R, R, FN_GC), b2, csm, n_rows, n_batch)
    yf = yf.reshape(n_rows, D_MODEL)
    if need_ctx:
        yf = _fft_ctx(fn, actx, csm, yf, n_batch)
    return yf


def _merge_kernel(h2_ref, o_ref, gm_ref, gd_ref, gf_ref, yd_ref, yf_ref, x_ref, mod_ref, hg_ref,
                  wml_ref, wda_ref, wfn_ref, wout_ref, out_ref, *, rows_per_batch, n_batch):
    i = pl.program_id(0)
    hsum = h2_ref[0].astype(F32) + h2_ref[1].astype(F32)
    hg = hg_ref[...]
    parts = []
    for h in range(ML_HEADS):
        hs = slice(h * ML_DK, (h + 1) * ML_DK)
        parts.append(_rms(hsum[:, hs], hg[:, hs]))
    ym = (jnp.concatenate(parts, axis=1) * jax.nn.sigmoid(o_ref[...].astype(F32))).astype(BF16)
    y = (jax.nn.sigmoid(gm_ref[...].astype(F32)) * _dot(ym, wml_ref[...])
         + jax.nn.sigmoid(gd_ref[...].astype(F32)) * _dot(yd_ref[...], wda_ref[...])
         + jax.nn.sigmoid(gf_ref[...].astype(F32)) * _dot(yf_ref[...], wfn_ref[...]))
    gate = _mod_row(mod_ref, i, rows_per_batch, n_batch, 2)
    out_ref[...] = x_ref[...] + gate * _dot(y.astype(BF16), wout_ref[...])


def _merge(h2, u, yd, yf, x, mods, head_g, wml, wda, wfn, wout, need_ctx, n_batch):
    n_rows = x.shape[0]
    tm = _row_tile(n_batch)
    rows_per_batch = SEQ // tm
    ni = (n_rows if need_ctx else n_batch * SEQ) // tm
    kern = functools.partial(_merge_kernel, rows_per_batch=rows_per_batch, n_batch=n_batch)
    row = lambda i: (i, 0)
    full = lambda i: (0, 0)
    wspec = pl.BlockSpec((D_MODEL, D_MODEL), full)
    return pl.pallas_call(
        kern,
        grid=(ni,),
        in_specs=[
            pl.BlockSpec((2, tm, D_MODEL), lambda i: (0, i, 0)),
            pl.BlockSpec((tm, D_MODEL), lambda i: (i, U_OML)),
            pl.BlockSpec((tm, D_MODEL), lambda i: (i, U_GPRE)),
            pl.BlockSpec((tm, D_MODEL), lambda i: (i, U_GPRE + 1)),
            pl.BlockSpec((tm, D_MODEL), lambda i: (i, U_GPRE + 2)),
            pl.BlockSpec((tm, D_MODEL), row),
            pl.BlockSpec((tm, D_MODEL), row),
            pl.BlockSpec((tm, D_MODEL), row),
            pl.BlockSpec((8, N_MOD), full),
            pl.BlockSpec((1, D_MODEL), full),
            wspec, wspec, wspec, wspec,
        ],
        out_specs=pl.BlockSpec((tm, D_MODEL), row),
        out_shape=jax.ShapeDtypeStruct((n_rows, D_MODEL), F32),
        compiler_params=_cparams(("arbitrary",)),
        name="merge_out_proj",
    )(h2, u, u, u, u, yd, yf, x, mods, head_g, wml, wda, wfn, wout)


FFN_CHUNKS = ((0, 1024), (1024, 1024), (2048, 768))


def _ffn_kernel(x_ref, mod_ref, g_ref, win_ref, wout_ref, fg_ref, out_ref, *, rows_per_batch, n_batch, final):
    i = pl.program_id(0)
    x = x_ref[...]
    shift = _mod_row(mod_ref, i, rows_per_batch, n_batch, 3)
    scale = _mod_row(mod_ref, i, rows_per_batch, n_batch, 4)
    gate = _mod_row(mod_ref, i, rows_per_batch, n_batch, 5)
    h = (_rms(x, g_ref[...]) * (1.0 + scale) + shift).astype(BF16)
    acc = None
    for lo, width in FFN_CHUNKS:
        a = _dot(h, win_ref[:, lo:lo + width])
        b = _dot(h, win_ref[:, D_FF + lo:D_FF + lo + width])
        act = (a * jax.nn.sigmoid(a) * b).astype(BF16)
        part = _dot(act, wout_ref[lo:lo + width, :])
        acc = part if acc is None else acc + part
    xn = x + gate * acc
    out_ref[...] = _rms(xn, fg_ref[...]) if final else xn


def _ffn(x, mods, g, w_in, w_out, final_g, final, n_rows_out, n_batch):
    tm = _row_tile(n_batch)
    rows_per_batch = SEQ // tm
    kern = functools.partial(_ffn_kernel, rows_per_batch=rows_per_batch, n_batch=n_batch, final=final)
    row = lambda i: (i, 0)
    full = lambda i: (0, 0)
    return pl.pallas_call(
        kern,
        grid=(n_rows_out // tm,),
        in_specs=[
            pl.BlockSpec((tm, D_MODEL), row),
            pl.BlockSpec((8, N_MOD), full),
            pl.BlockSpec((1, D_MODEL), full),
            pl.BlockSpec((D_MODEL, 2 * D_FF), full),
            pl.BlockSpec((D_FF, D_MODEL), full),
            pl.BlockSpec((1, D_MODEL), full),
        ],
        out_specs=pl.BlockSpec((tm, D_MODEL), row),
        out_shape=jax.ShapeDtypeStruct((n_rows_out, D_MODEL), F32),
        compiler_params=_cparams(("arbitrary",)),
        name="swiglu_ffn",
    )(x, mods, g, w_in, w_out, final_g)


def _rope_tables(pad):
    n_freq = DA_DH // 4
    rows = SEQ // GRID_W
    inv = ROPE_BASE ** (-jnp.arange(n_freq, dtype=F32) / n_freq)
    r = jnp.repeat(jnp.arange(rows, dtype=F32), GRID_W)
    col = jnp.tile(jnp.arange(GRID_W, dtype=F32), rows)
    ang = jnp.concatenate([r[:, None] * inv, col[:, None] * inv], axis=-1)
    cos, sin = jnp.cos(ang), jnp.sin(ang)
    cos_t = jnp.concatenate([cos, cos, cos, cos], axis=-1)
    sin_t = jnp.concatenate([-sin, sin, -sin, sin], axis=-1)
    cos_t = jnp.concatenate([cos_t, jnp.ones((pad, 128), F32)], axis=0)
    sin_t = jnp.concatenate([sin_t, jnp.zeros((pad, 128), F32)], axis=0)
    return cos_t, sin_t


def kernel(x, c, ctx, c_ctx, w_ada, b_ada, norm_g, w_in, ml_gate_b, ml_head_g, da_lam, da_head_g,
           w_br_ml, w_br_da, w_br_fn, w_out, w_ffn_in, w_ffn_out, final_g):
    n_batch = x.shape[0]
    n_lat = n_batch * SEQ
    xs = jnp.concatenate([x.reshape(n_lat, D_MODEL), ctx.reshape(n_batch * CTX_LEN, D_MODEL)], axis=0)
    cc = jnp.concatenate([c, c_ctx[None, :], jnp.zeros((8 - n_batch - 1, D_MODEL), F32)], axis=0)
    mods = _mods(cc, w_ada, b_ada)
    cos_t, sin_t = _rope_tables(n_batch * CTX_LEN)
    tables = _dft_tables()
    gate_lo = 4 * D_MODEL
    fn_lo = gate_lo + N_GATE + 3 * D_MODEL
    final_g2 = final_g.reshape(1, D_MODEL)

    for l in range(DEPTH):
        need_ctx = l < DEPTH - 1
        lam_init = 0.8 - 0.6 * math.exp(-0.3 * l)
        wl = w_in[l]
        w_main = jnp.concatenate([wl[:, :gate_lo], wl[:, gate_lo + N_GATE:fn_lo],
                                  wl[:, fn_lo + D_MODEL:], wl[:, fn_lo:fn_lo + D_MODEL]], axis=1).astype(BF16)
        w_gate = wl[:, gate_lo:gate_lo + N_GATE].astype(BF16)
        u, fn, gates = _inproj(xs, mods[l], norm_g[l, 0].reshape(1, D_MODEL), w_main, w_gate,
                               cos_t, sin_t, n_batch)
        h2 = _mlstm(u, gates, ml_gate_b[l].reshape(1, N_GATE), n_batch)
        yd = _attn(u, da_lam[l], da_head_g[l].reshape(1, D_MODEL), lam_init, need_ctx, n_batch)
        yf = _fourier(fn, tables, need_ctx, n_batch)
        xs = _merge(h2, u, yd, yf, xs, mods[l], ml_head_g[l].reshape(1, D_MODEL),
                    w_br_ml[l].astype(BF16), w_br_da[l].astype(BF16), w_br_fn[l].astype(BF16),
                    w_out[l].astype(BF16), need_ctx, n_batch)
        final = l == DEPTH - 1
        n_out = n_lat if final else xs.shape[0]
        xs = _ffn(xs, mods[l], norm_g[l, 1].reshape(1, D_MODEL), w_ffn_in[l].astype(BF16),
                  w_ffn_out[l].astype(BF16), final_g2, final, n_out, n_batch)
    return xs.reshape(n_batch, SEQ, D_MODEL)
```

```python
import functools
import math

import jax
import jax.numpy as jnp
from jax import lax
from jax.experimental import pallas as pl
from jax.experimental.pallas import tpu as pltpu

D_MODEL = 1024
SEQ = 4096
DEPTH = 4
CTX_LEN = 256
GRID_W = 64
NORM_EPS = 1e-6

ML_HEADS = 4
ML_DK = 256
ML_CHUNK = 128

DA_HEADS = 8
DA_DH = 64
DA_DV = 2 * DA_DH
ROPE_BASE = 10000.0
ATTN_ROW_CHUNKS = 2

FN_GROUPS = 4
FN_GC = 256
FFT_R = 64

D_FF = 2816
N_GATE = 4 * ML_HEADS
N_MOD = 6 * D_MODEL

U_QML, U_KML, U_VML, U_OML, U_QDA, U_KDA, U_VDA, U_GPRE = 0, 1, 2, 3, 4, 5, 6, 7
U_BLOCKS = 10
W_BLOCKS = U_BLOCKS + 1

VMEM_LIMIT_V7X = 56 * 1024 * 1024

BF16 = jnp.bfloat16
F32 = jnp.float32


def _cparams(sem):
    return pltpu.CompilerParams(dimension_semantics=sem, vmem_limit_bytes=VMEM_LIMIT_V7X)


def _dot(a, b):
    return jnp.dot(a, b, preferred_element_type=F32)


def _dot_nt(a, b):
    return lax.dot_general(a, b, (((1,), (1,)), ((), ())), preferred_element_type=F32)


def _dot_tn(a, b):
    return lax.dot_general(a, b, (((0,), (0,)), ((), ())), preferred_element_type=F32)


def _mod_row(mod_ref, i, rows_per_batch, n_batch, col):
    r = jnp.minimum(i // rows_per_batch, n_batch)
    return mod_ref[pl.ds(r, 1), col * D_MODEL:(col + 1) * D_MODEL]


def _row_tile(n_batch):
    return min(512, n_batch * CTX_LEN)


def _rms(x, g):
    return x * lax.rsqrt(jnp.mean(x * x, axis=-1, keepdims=True) + NORM_EPS) * g


def _mods_kernel(c_ref, w_ref, b_ref, o_ref):
    c = c_ref[...]
    s = (c * jax.nn.sigmoid(c)).astype(BF16)
    o_ref[0] = _dot(s, w_ref[0].astype(BF16)) + b_ref[0]


def _mods(cc, w_ada, b_ada):
    tn = 1536
    return pl.pallas_call(
        _mods_kernel,
        grid=(DEPTH, N_MOD // tn),
        in_specs=[
            pl.BlockSpec((8, D_MODEL), lambda l, j: (0, 0)),
            pl.BlockSpec((1, D_MODEL, tn), lambda l, j: (l, 0, j)),
            pl.BlockSpec((1, 1, tn), lambda l, j: (l, 0, j)),
        ],
        out_specs=pl.BlockSpec((1, 8, tn), lambda l, j: (l, 0, j)),
        out_shape=jax.ShapeDtypeStruct((DEPTH, 8, N_MOD), F32),
        compiler_params=_cparams(("arbitrary", "arbitrary")),
        name="adaln_mods",
    )(cc, w_ada, b_ada.reshape(DEPTH, 1, N_MOD))


def _inproj_kernel(x_ref, mod_ref, g_ref, w_ref, wg_ref, cos_ref, sin_ref,
                   u_ref, fn_ref, gate_ref, xn_ref, *, rows_per_batch, n_batch):
    i = pl.program_id(0)
    j = pl.program_id(1)

    @pl.when(j == 0)
    def _():
        y = _rms(x_ref[...], g_ref[...])
        shift = _mod_row(mod_ref, i, rows_per_batch, n_batch, 0)
        scale = _mod_row(mod_ref, i, rows_per_batch, n_batch, 1)
        xn_ref[...] = (y * (1.0 + scale) + shift).astype(BF16)
        gate_ref[...] = _dot(xn_ref[...], wg_ref[...])

    acc = _dot(xn_ref[...], w_ref[...])

    is_rope = jnp.logical_or(j == U_QDA, j == U_KDA)
    is_plain = jnp.logical_and(jnp.logical_not(is_rope), jnp.logical_and(j != U_KML, j < U_BLOCKS))

    @pl.when(is_plain)
    def _():
        u_ref[...] = acc.astype(BF16)

    @pl.when(j == U_KML)
    def _():
        u_ref[...] = (acc * (ML_DK ** -0.5)).astype(BF16)

    @pl.when(is_rope)
    def _():
        n = acc.shape[1]
        lane = lax.broadcasted_iota(jnp.int32, (1, n), 1)
        first_half = (lane % DA_DH) < (DA_DH // 2)
        swapped = jnp.where(first_half, pltpu.roll(acc, n - DA_DH // 2, 1), pltpu.roll(acc, DA_DH // 2, 1))
        cos = cos_ref[...]
        sin = sin_ref[...]
        for t in range(n // 128):
            sl = slice(t * 128, (t + 1) * 128)
            u_ref[:, sl] = (acc[:, sl] * cos + swapped[:, sl] * sin).astype(BF16)

    @pl.when(j == U_BLOCKS)
    def _():
        for g in range(FN_GROUPS):
            fn_ref[g] = acc[:, g * FN_GC:(g + 1) * FN_GC].astype(BF16)


def _inproj(x, mods, g, w_main, w_gate, cos_t, sin_t, n_batch):
    n_rows = x.shape[0]
    tm = n_batch * CTX_LEN
    assert SEQ % tm == 0 and cos_t.shape[0] == SEQ + tm
    ni = n_rows // tm
    rows_per_batch = SEQ // tm
    lat_blocks = n_batch * rows_per_batch

    def tab_idx(i, j):
        return (jnp.where(i < lat_blocks, i % rows_per_batch, rows_per_batch), 0)

    kern = functools.partial(_inproj_kernel, rows_per_batch=rows_per_batch, n_batch=n_batch)
    return pl.pallas_call(
        kern,
        grid=(ni, W_BLOCKS),
        in_specs=[
            pl.BlockSpec((tm, D_MODEL), lambda i, j: (i, 0)),
            pl.BlockSpec((8, N_MOD), lambda i, j: (0, 0)),
            pl.BlockSpec((1, D_MODEL), lambda i, j: (0, 0)),
            pl.BlockSpec((D_MODEL, D_MODEL), lambda i, j: (0, j)),
            pl.BlockSpec((D_MODEL, N_GATE), lambda i, j: (0, 0)),
            pl.BlockSpec((tm, 128), tab_idx),
            pl.BlockSpec((tm, 128), tab_idx),
        ],
        out_specs=[
            pl.BlockSpec((tm, D_MODEL), lambda i, j: (i, jnp.minimum(j, U_BLOCKS - 1))),
            pl.BlockSpec((FN_GROUPS, tm, FN_GC), lambda i, j: (0, i, 0)),
            pl.BlockSpec((tm, N_GATE), lambda i, j: (i, 0)),
        ],
        out_shape=[
            jax.ShapeDtypeStruct((n_rows, U_BLOCKS * D_MODEL), BF16),
            jax.ShapeDtypeStruct((FN_GROUPS, n_rows, FN_GC), BF16),
            jax.ShapeDtypeStruct((n_rows, N_GATE), F32),
        ],
        scratch_shapes=[pltpu.VMEM((tm, D_MODEL), BF16)],
        compiler_params=_cparams(("arbitrary", "arbitrary")),
        name="in_proj",
    )(x, mods, g, w_main, w_gate, cos_t, sin_t)


def _mlstm_kernel(qf_ref, kf_ref, vf_ref, gf_ref, gtf_ref, qb_ref, kb_ref, vb_ref, gb_ref, gtb_ref,
                  bias_ref, biast_ref, hf_ref, hb_ref, c_ref, n_ref, m_ref):
    s = pl.program_id(1)
    L = ML_CHUNK

    @pl.when(s == 0)
    def _():
        c_ref[...] = jnp.zeros_like(c_ref)
        n_ref[...] = jnp.zeros_like(n_ref)
        m_ref[...] = jnp.zeros_like(m_ref)

    t_idx = lax.broadcasted_iota(jnp.int32, (L, L), 0)
    s_idx = lax.broadcasted_iota(jnp.int32, (L, L), 1)
    dirs = (
        (0, qf_ref, kf_ref, vf_ref, gf_ref, gtf_ref, hf_ref, s_idx <= t_idx),
        (1, qb_ref, kb_ref, vb_ref, gb_ref, gtb_ref, hb_ref, s_idx >= t_idx),
    )
    for d, q_ref, k_ref, v_ref, g_ref, gt_ref, h_ref, causal in dirs:
        gates = g_ref[...] + bias_ref[...]
        gates_t = gt_ref[...] + biast_ref[...]
        for h in range(ML_HEADS):
            st = d * ML_HEADS + h
            hs = slice(h * ML_DK, (h + 1) * ML_DK)
            gi = 2 * ML_HEADS * d + h
            gf = gi + ML_HEADS
            i_col, f_col = gates[:, gi:gi + 1], gates[:, gf:gf + 1]
            i_row, f_row = gates_t[gi:gi + 1, :], gates_t[gf:gf + 1, :]
            lf_col = jnp.minimum(f_col, 0.0) - jnp.log1p(jnp.exp(-jnp.abs(f_col)))
            lf_row = jnp.minimum(f_row, 0.0) - jnp.log1p(jnp.exp(-jnp.abs(f_row)))
            b_col = jnp.sum(jnp.where(causal, lf_row, 0.0), axis=1, keepdims=True)
            b_row = jnp.sum(jnp.where(causal, 0.0, lf_col) + jnp.where(t_idx == s_idx, lf_col, 0.0),
                            axis=0, keepdims=True)
            b_end = jnp.sum(lf_row, axis=1, keepdims=True)

            m_prev = m_ref[st][:, 0:1]
            dmat = jnp.where(causal, b_col - b_row + i_row, -jnp.inf)
            inter = b_col + m_prev
            m_t = jnp.maximum(inter, jnp.max(dmat, axis=1, keepdims=True))

            q = q_ref[:, hs]
            k = k_ref[:, hs]
            v = v_ref[:, hs]
            a = jnp.exp(dmat - m_t) * _dot_nt(q, k)
            sc = jnp.exp(inter - m_t)
            c_prev = c_ref[st]
            n_prev = n_ref[st]
            num = sc * _dot(q, c_prev.astype(BF16)) + _dot(a.astype(BF16), v)
            qn = jnp.sum(q.astype(F32) * n_prev, axis=1, keepdims=True)
            den = sc * qn + jnp.sum(a, axis=1, keepdims=True)
            inv = 1.0 / jnp.maximum(jnp.abs(den), jnp.exp(-m_t))
            h_ref[:, hs] = (num * inv).astype(h_ref.dtype)

            g_col = b_end - b_col + i_col
            m_new = jnp.maximum(b_end + m_prev, jnp.max(g_col, axis=0, keepdims=True))
            decay = jnp.exp(b_end + m_prev - m_new)
            kw = k.astype(F32) * jnp.exp(g_col - m_new)
            c_ref[st] = decay * c_prev + _dot_tn(kw.astype(BF16), v)
            n_ref[st] = decay * n_prev + jnp.sum(kw, axis=0, keepdims=True)
            m_ref[st] = jnp.broadcast_to(m_new, (1, 128))


def _mlstm(u, gates, gates_t, gate_b, n_batch):
    n_rows = u.shape[0]
    L = ML_CHUNK
    lat_chunks = SEQ // L
    ctx_chunks = CTX_LEN // L
    n_steps = ctx_chunks + lat_chunks
    ctx_base = n_batch * lat_chunks

    def rowblk(d):
        def f(b, s):
            in_ctx = s < ctx_chunks
            if d == 0:
                c = jnp.where(in_ctx, s, s - ctx_chunks)
            else:
                c = jnp.where(in_ctx, ctx_chunks - 1 - s, lat_chunks - 1 - (s - ctx_chunks))
            return jnp.where(in_ctx, ctx_base + ctx_chunks * b, lat_chunks * b) + c
        return f

    def dir_specs(d):
        rb = rowblk(d)
        return [
            pl.BlockSpec((L, D_MODEL), lambda b, s: (rb(b, s), U_QML)),
            pl.BlockSpec((L, D_MODEL), lambda b, s: (rb(b, s), U_KML)),
            pl.BlockSpec((L, D_MODEL), lambda b, s: (rb(b, s), U_VML)),
            pl.BlockSpec((L, N_GATE), lambda b, s: (rb(b, s), 0)),
            pl.BlockSpec((N_GATE, L), lambda b, s: (0, rb(b, s))),
        ]

    def out_spec(d):
        rb = rowblk(d)
        return pl.BlockSpec((L, D_MODEL), lambda b, s: (rb(b, s), 0))

    n_state = 2 * ML_HEADS
    return pl.pallas_call(
        _mlstm_kernel,
        grid=(n_batch, n_steps),
        in_specs=dir_specs(0) + dir_specs(1) + [
            pl.BlockSpec((1, N_GATE), lambda b, s: (0, 0)),
            pl.BlockSpec((N_GATE, 1), lambda b, s: (0, 0)),
        ],
        out_specs=[out_spec(0), out_spec(1)],
        out_shape=[jax.ShapeDtypeStruct((n_rows, D_MODEL), BF16)] * 2,
        scratch_shapes=[
            pltpu.VMEM((n_state, ML_DK, ML_DK), F32),
            pltpu.VMEM((n_state, 1, ML_DK), F32),
            pltpu.VMEM((n_state, 1, 128), F32),
        ],
        compiler_params=_cparams(("arbitrary", "arbitrary")),
        name="mlstm_scan",
    )(u, u, u, gates, gates_t, u, u, u, gates, gates_t, gate_b, gate_b.reshape(N_GATE, 1))


def _attn_lambda(lam_ref, lam_init):
    lq = lam_ref[...]
    return (jnp.exp(jnp.sum(lq[0:1] * lq[1:2], axis=1, keepdims=True))
            - jnp.exp(jnp.sum(lq[2:3] * lq[3:4], axis=1, keepdims=True)) + lam_init)


def _attn_queries(q_ref):
    q = q_ref[...]
    lane = lax.broadcasted_iota(jnp.int32, (1, DA_DV), 1)
    zero = jnp.zeros_like(q)
    q2 = jnp.concatenate([jnp.where(lane < DA_DH, q, zero), jnp.where(lane >= DA_DH, q, zero)], axis=0)
    return q2 * (DA_DH ** -0.5)


def _attn_values(s_chunks, m_chunks, vx, lam, g, lam_init, tq):
    accs = [_dot(jnp.exp(s - m).astype(BF16), vx) for s, m in zip(s_chunks, m_chunks)]
    acc = jnp.concatenate(accs, axis=0)
    o0 = acc[0:tq, 0:DA_DV] * (1.0 / acc[0:tq, DA_DV:DA_DV + 1])
    o1 = acc[tq:, 0:DA_DV] * (1.0 / acc[tq:, DA_DV:DA_DV + 1])
    return _rms(o0 - lam * o1, g) * (1.0 - lam_init)


def _attn_kernel(q_ref, kl_ref, vl_ref, kc_ref, vc_ref, lam_ref, g_ref, o_ref, kx_ref, vx_ref, s_ref, m_ref,
                 *, lam_init):
    qi = pl.program_id(2)
    tq = q_ref.shape[0]
    rows = 2 * tq // ATTN_ROW_CHUNKS

    @pl.when(qi == 0)
    def _():
        kx_ref[0:SEQ, :] = kl_ref[...]
        kx_ref[SEQ:, :] = kc_ref[...]
        vx_ref[0:SEQ, 0:DA_DV] = vl_ref[...]
        vx_ref[SEQ:, 0:DA_DV] = vc_ref[...]
        lane_v = lax.broadcasted_iota(jnp.int32, (SEQ + CTX_LEN, DA_DV), 1)
        vx_ref[:, DA_DV:] = jnp.where(lane_v == 0, 1.0, 0.0).astype(BF16)

    def score(slot):
        s = _dot_nt(_attn_queries(q_ref), kx_ref[...])
        m_ref[slot] = jnp.max(s, axis=1, keepdims=True)
        s_ref[slot] = s

    def drain(slot):
        s_chunks = [s_ref[slot, c * rows:(c + 1) * rows, :] for c in range(ATTN_ROW_CHUNKS)]
        m_chunks = [m_ref[slot, c * rows:(c + 1) * rows, :] for c in range(ATTN_ROW_CHUNKS)]
        o = _attn_values(s_chunks, m_chunks, vx_ref[...], _attn_lambda(lam_ref, lam_init), g_ref[...], lam_init, tq)
        o_ref[...] = o.astype(o_ref.dtype)

    @pl.when(qi == 0)
    def _():
        score(0)

    for parity in (0, 1):
        @pl.when(jnp.logical_and(qi > 0, qi % 2 == parity))
        def _():
            drain(1 - parity)
            score(parity)


def _attn_ctx_kernel(q_ref, kc_ref, vc_ref, lam_ref, g_ref, yin_ref, o_ref, *, lam_init):
    del yin_ref
    tq = q_ref.shape[0]
    s = _dot_nt(_attn_queries(q_ref), kc_ref[...])
    lane_v = lax.broadcasted_iota(jnp.int32, (CTX_LEN, DA_DV), 1)
    vx = jnp.concatenate([vc_ref[...], jnp.where(lane_v == 0, 1.0, 0.0).astype(BF16)], axis=1)
    o = _attn_values([s], [jnp.max(s, axis=1, keepdims=True)], vx, _attn_lambda(lam_ref, lam_init),
                     g_ref[...], lam_init, tq)
    o_ref[...] = o.astype(o_ref.dtype)


def _attn(u, da_lam, head_g, lam_init, need_ctx, n_batch):
    n_rows = u.shape[0]
    tq = CTX_LEN
    lat_qblocks = SEQ // tq
    ctx_base = n_batch * lat_qblocks
    cpb = D_MODEL // DA_DV
    n_keys = SEQ + CTX_LEN

    kern = functools.partial(_attn_kernel, lam_init=lam_init)
    yd = pl.pallas_call(
        kern,
        grid=(n_batch, DA_HEADS, lat_qblocks + 1),
        in_specs=[
            pl.BlockSpec((tq, DA_DV),
                         lambda b, h, qi: (b * lat_qblocks + jnp.minimum(qi, lat_qblocks - 1), U_QDA * cpb + h)),
            pl.BlockSpec((SEQ, DA_DV), lambda b, h, qi: (b, U_KDA * cpb + h)),
            pl.BlockSpec((SEQ, DA_DV), lambda b, h, qi: (b, U_VDA * cpb + h)),
            pl.BlockSpec((CTX_LEN, DA_DV), lambda b, h, qi: (ctx_base + b, U_KDA * cpb + h)),
            pl.BlockSpec((CTX_LEN, DA_DV), lambda b, h, qi: (ctx_base + b, U_VDA * cpb + h)),
            pl.BlockSpec((4, DA_DH), lambda b, h, qi: (0, 0)),
            pl.BlockSpec((1, DA_DV), lambda b, h, qi: (0, h)),
        ],
        out_specs=pl.BlockSpec((tq, DA_DV), lambda b, h, qi: (b * lat_qblocks + jnp.maximum(qi - 1, 0), h)),
        out_shape=jax.ShapeDtypeStruct((n_rows, D_MODEL), BF16),
        scratch_shapes=[
            pltpu.VMEM((n_keys, DA_DV), BF16),
            pltpu.VMEM((n_keys, 2 * DA_DV), BF16),
            pltpu.VMEM((2, 2 * tq, n_keys), F32),
            pltpu.VMEM((2, 2 * tq, 1), F32),
        ],
        compiler_params=_cparams(("arbitrary", "arbitrary", "arbitrary")),
        name="diff_attn",
    )(u, u, u, u, u, da_lam, head_g)
    if not need_ctx:
        return yd
    kern_ctx = functools.partial(_attn_ctx_kernel, lam_init=lam_init)
    return pl.pallas_call(
        kern_ctx,
        grid=(n_batch, DA_HEADS),
        in_specs=[
            pl.BlockSpec((CTX_LEN, DA_DV), lambda b, h: (ctx_base + b, U_QDA * cpb + h)),
            pl.BlockSpec((CTX_LEN, DA_DV), lambda b, h: (ctx_base + b, U_KDA * cpb + h)),
            pl.BlockSpec((CTX_LEN, DA_DV), lambda b, h: (ctx_base + b, U_VDA * cpb + h)),
            pl.BlockSpec((4, DA_DH), lambda b, h: (0, 0)),
            pl.BlockSpec((1, DA_DV), lambda b, h: (0, h)),
            pl.BlockSpec(memory_space=pl.ANY),
        ],
        out_specs=pl.BlockSpec((CTX_LEN, DA_DV), lambda b, h: (ctx_base + b, h)),
        out_shape=jax.ShapeDtypeStruct((n_rows, D_MODEL), BF16),
        input_output_aliases={5: 0},
        compiler_params=_cparams(("arbitrary", "arbitrary")),
        name="diff_attn_ctx",
    )(u, u, u, da_lam, head_g, yd)


def _dft_tables():
    R = FFT_R

    def cs(num, period):
        ang = (num % period).astype(F32) * (2.0 * math.pi / period)
        return jnp.cos(ang), jnp.sin(ang)

    idx = jnp.arange(R, dtype=jnp.int32)
    c1, s1 = cs(idx[:, None] * idx[None, :], R)
    a1 = jnp.concatenate([c1, -s1], axis=0).astype(BF16)
    f2 = idx[:, None, None]
    f1 = idx[None, :, None]
    t1 = idx[None, None, :]
    mc, ms = cs(t1 * (R * f1 + f2), SEQ)
    b2 = jnp.concatenate([jnp.concatenate([mc, ms], axis=2),
                          jnp.concatenate([-ms, mc], axis=2)], axis=1).astype(BF16)
    ch = jnp.arange(FN_GC, dtype=jnp.int32)
    cc, sc = cs(ch[:, None] * ch[None, :], FN_GC)
    csm = jnp.concatenate([cc, sc], axis=0).astype(BF16)
    actx = jnp.concatenate([cc, -sc], axis=0).astype(BF16)
    return a1, b2, csm, actx


def _fft1_kernel(a_ref, z_ref, p_ref):
    p_ref[0, 0] = _dot(a_ref[...], z_ref[0])


def _fft1(fn_view, a1, n_batch):
    R = FFT_R
    lanes = R * FN_GC
    lc = 4096
    return pl.pallas_call(
        _fft1_kernel,
        grid=(FN_GROUPS, n_batch, lanes // lc),
        in_specs=[
            pl.BlockSpec((2 * R, R), lambda g, b, c: (0, 0)),
            pl.BlockSpec((1, R, lc), lambda g, b, c: (g, b, c)),
        ],
        out_specs=pl.BlockSpec((1, 1, 2 * R, lc), lambda g, b, c: (g, b, 0, c)),
        out_shape=jax.ShapeDtypeStruct((FN_GROUPS, n_batch, 2 * R, lanes), F32),
        compiler_params=_cparams(("arbitrary", "arbitrary", "arbitrary")),
        name="fourier_stage1",
    )(a1, fn_view)


def _fft2_kernel(p_ref, b_ref, cs_ref, o_ref, *, f2b):
    R = FFT_R
    cc = cs_ref[0:FN_GC, :]
    sc = cs_ref[FN_GC:2 * FN_GC, :]
    norm = 1.0 / math.sqrt(SEQ * FN_GC)
    for g in range(FN_GROUPS):
        for jj in range(f2b):
            stacked = jnp.concatenate([p_ref[g, 0, 0, jj], p_ref[g, 0, 1, jj]], axis=0).astype(BF16)
            x = _dot(b_ref[jj], stacked)
            y = _dot(x[0:R].astype(BF16), cc) + _dot(x[R:2 * R].astype(BF16), sc)
            lo = jj * D_MODEL + g * FN_GC
            o_ref[:, lo:lo + FN_GC] = (y * norm).astype(o_ref.dtype)


def _fft2(p6, b2, csm, n_rows, n_batch):
    R = FFT_R
    f2b = 8
    kern = functools.partial(_fft2_kernel, f2b=f2b)
    return pl.pallas_call(
        kern,
        grid=(n_batch, R // f2b),
        in_specs=[
            pl.BlockSpec((FN_GROUPS, 1, 2, f2b, R, FN_GC), lambda b, f: (0, b, 0, f, 0, 0)),
            pl.BlockSpec((f2b, 2 * R, 2 * R), lambda b, f: (f, 0, 0)),
            pl.BlockSpec((2 * FN_GC, FN_GC), lambda b, f: (0, 0)),
        ],
        out_specs=pl.BlockSpec((R, f2b * D_MODEL), lambda b, f: (b, f)),
        out_shape=jax.ShapeDtypeStruct((n_rows // R, R * D_MODEL), BF16),
        compiler_params=_cparams(("arbitrary", "arbitrary")),
        name="fourier_stage2",
    )(p6, b2, csm)


def _fft_ctx_kernel(z_ref, a_ref, cs_ref, yin_ref, o_ref):
    del yin_ref
    cc = cs_ref[0:FN_GC, :]
    sc = cs_ref[FN_GC:2 * FN_GC, :]
    norm = 1.0 / math.sqrt(CTX_LEN * FN_GC)
    for g in range(FN_GROUPS):
        p = _dot(a_ref[...], z_ref[g])
        y = _dot(p[0:CTX_LEN].astype(BF16), cc) + _dot(p[CTX_LEN:2 * CTX_LEN].astype(BF16), sc)
        o_ref[:, g * FN_GC:(g + 1) * FN_GC] = (y * norm).astype(o_ref.dtype)


def _fft_ctx(fn, actx, csm, yf, n_batch):
    n_rows = yf.shape[0]
    ctx_base = n_batch * (SEQ // CTX_LEN)
    return pl.pallas_call(
        _fft_ctx_kernel,
        grid=(n_batch,),
        in_specs=[
            pl.BlockSpec((FN_GROUPS, CTX_LEN, FN_GC), lambda b: (0, ctx_base + b, 0)),
            pl.BlockSpec((2 * CTX_LEN, CTX_LEN), lambda b: (0, 0)),
            pl.BlockSpec((2 * FN_GC, FN_GC), lambda b: (0, 0)),
            pl.BlockSpec(memory_space=pl.ANY),
        ],
        out_specs=pl.BlockSpec((CTX_LEN, D_MODEL), lambda b: (ctx_base + b, 0)),
        out_shape=jax.ShapeDtypeStruct((n_rows, D_MODEL), BF16),
        input_output_aliases={3: 0},
        compiler_params=_cparams(("arbitrary",)),
        name="fourier_ctx",
    )(fn, actx, csm, yf)


def _fourier(fn, tables, need_ctx, n_batch):
    a1, b2, csm, actx = tables
    n_rows = fn.shape[1]
    R = FFT_R
    p = _fft1(fn.reshape(FN_GROUPS, n_rows // R, R * FN_GC), a1, n_batch)
    yf = _fft2(p.reshape(FN_GROUPS, n_batch, 2, R, R, FN_GC), b2, csm, n_rows, n_batch)
    yf = yf.reshape(n_rows, D_MODEL)
    if need_ctx:
        yf = _fft_ctx(fn, actx, csm, yf, n_batch)
    return yf


def _merge_kernel(hf_ref, hb_ref, o_ref, gm_ref, gd_ref, gf_ref, yd_ref, yf_ref, x_ref, mod_ref, hg_ref,
                  wml_ref, wda_ref, wfn_ref, wout_ref, out_ref, *, rows_per_batch, n_batch):
    i = pl.program_id(0)
    hsum = hf_ref[...].astype(F32) + hb_ref[...].astype(F32)
    hg = hg_ref[...]
    parts = []
    for h in range(ML_HEADS):
        hs = slice(h * ML_DK, (h + 1) * ML_DK)
        parts.append(_rms(hsum[:, hs], hg[:, hs]))
    ym = (jnp.concatenate(parts, axis=1) * jax.nn.sigmoid(o_ref[...].astype(F32))).astype(BF16)
    y = (jax.nn.sigmoid(gm_ref[...].astype(F32)) * _dot(ym, wml_ref[...])
         + jax.nn.sigmoid(gd_ref[...].astype(F32)) * _dot(yd_ref[...], wda_ref[...])
         + jax.nn.sigmoid(gf_ref[...].astype(F32)) * _dot(yf_ref[...], wfn_ref[...]))
    gate = _mod_row(mod_ref, i, rows_per_batch, n_batch, 2)
    out_ref[...] = x_ref[...] + gate * _dot(y.astype(BF16), wout_ref[...])


def _merge(hf, hb, u, yd, yf, x, mods, head_g, wml, wda, wfn, wout, need_ctx, n_batch):
    n_rows = x.shape[0]
    tm = _row_tile(n_batch)
    rows_per_batch = SEQ // tm
    ni = (n_rows if need_ctx else n_batch * SEQ) // tm
    kern = functools.partial(_merge_kernel, rows_per_batch=rows_per_batch, n_batch=n_batch)
    row = lambda i: (i, 0)
    full = lambda i: (0, 0)
    wspec = pl.BlockSpec((D_MODEL, D_MODEL), full)
    return pl.pallas_call(
        kern,
        grid=(ni,),
        in_specs=[
            pl.BlockSpec((tm, D_MODEL), row),
            pl.BlockSpec((tm, D_MODEL), row),
            pl.BlockSpec((tm, D_MODEL), lambda i: (i, U_OML)),
            pl.BlockSpec((tm, D_MODEL), lambda i: (i, U_GPRE)),
            pl.BlockSpec((tm, D_MODEL), lambda i: (i, U_GPRE + 1)),
            pl.BlockSpec((tm, D_MODEL), lambda i: (i, U_GPRE + 2)),
            pl.BlockSpec((tm, D_MODEL), row),
            pl.BlockSpec((tm, D_MODEL), row),
            pl.BlockSpec((tm, D_MODEL), row),
            pl.BlockSpec((8, N_MOD), full),
            pl.BlockSpec((1, D_MODEL), full),
            wspec, wspec, wspec, wspec,
        ],
        out_specs=pl.BlockSpec((tm, D_MODEL), row),
        out_shape=jax.ShapeDtypeStruct((n_rows, D_MODEL), F32),
        compiler_params=_cparams(("arbitrary",)),
        name="merge_out_proj",
    )(hf, hb, u, u, u, u, yd, yf, x, mods, head_g, wml, wda, wfn, wout)


FFN_CHUNKS = ((0, 1024), (1024, 1024), (2048, 768))


def _ffn_kernel(x_ref, mod_ref, g_ref, win_ref, wout_ref, fg_ref, out_ref, *, rows_per_batch, n_batch, final):
    i = pl.program_id(0)
    x = x_ref[...]
    shift = _mod_row(mod_ref, i, rows_per_batch, n_batch, 3)
    scale = _mod_row(mod_ref, i, rows_per_batch, n_batch, 4)
    gate = _mod_row(mod_ref, i, rows_per_batch, n_batch, 5)
    h = (_rms(x, g_ref[...]) * (1.0 + scale) + shift).astype(BF16)
    acc = None
    for lo, width in FFN_CHUNKS:
        a = _dot(h, win_ref[:, lo:lo + width])
        b = _dot(h, win_ref[:, D_FF + lo:D_FF + lo + width])
        act = (a * jax.nn.sigmoid(a) * b).astype(BF16)
        part = _dot(act, wout_ref[lo:lo + width, :])
        acc = part if acc is None else acc + part
    xn = x + gate * acc
    out_ref[...] = _rms(xn, fg_ref[...]) if final else xn


def _ffn(x, mods, g, w_in, w_out, final_g, final, n_rows_out, n_batch):
    tm = _row_tile(n_batch)
    rows_per_batch = SEQ // tm
    kern = functools.partial(_ffn_kernel, rows_per_batch=rows_per_batch, n_batch=n_batch, final=final)
    row = lambda i: (i, 0)
    full = lambda i: (0, 0)
    return pl.pallas_call(
        kern,
        grid=(n_rows_out // tm,),
        in_specs=[
            pl.BlockSpec((tm, D_MODEL), row),
            pl.BlockSpec((8, N_MOD), full),
            pl.BlockSpec((1, D_MODEL), full),
            pl.BlockSpec((D_MODEL, 2 * D_FF), full),
            pl.BlockSpec((D_FF, D_MODEL), full),
            pl.BlockSpec((1, D_MODEL), full),
        ],
        out_specs=pl.BlockSpec((tm, D_MODEL), row),
        out_shape=jax.ShapeDtypeStruct((n_rows_out, D_MODEL), F32),
        compiler_params=_cparams(("arbitrary",)),
        name="swiglu_ffn",
    )(x, mods, g, w_in, w_out, final_g)


def _rope_tables(pad):
    n_freq = DA_DH // 4
    rows = SEQ // GRID_W
    inv = ROPE_BASE ** (-jnp.arange(n_freq, dtype=F32) / n_freq)
    r = jnp.repeat(jnp.arange(rows, dtype=F32), GRID_W)
    col = jnp.tile(jnp.arange(GRID_W, dtype=F32), rows)
    ang = jnp.concatenate([r[:, None] * inv, col[:, None] * inv], axis=-1)
    cos, sin = jnp.cos(ang), jnp.sin(ang)
    cos_t = jnp.concatenate([cos, cos, cos, cos], axis=-1)
    sin_t = jnp.concatenate([-sin, sin, -sin, sin], axis=-1)
    cos_t = jnp.concatenate([cos_t, jnp.ones((pad, 128), F32)], axis=0)
    sin_t = jnp.concatenate([sin_t, jnp.zeros((pad, 128), F32)], axis=0)
    return cos_t, sin_t


def kernel(x, c, ctx, c_ctx, w_ada, b_ada, norm_g, w_in, ml_gate_b, ml_head_g, da_lam, da_head_g,
           w_br_ml, w_br_da, w_br_fn, w_out, w_ffn_in, w_ffn_out, final_g):
    n_batch = x.shape[0]
    n_lat = n_batch * SEQ
    xs = jnp.concatenate([x.reshape(n_lat, D_MODEL), ctx.reshape(n_batch * CTX_LEN, D_MODEL)], axis=0)
    cc = jnp.concatenate([c, c_ctx[None, :], jnp.zeros((8 - n_batch - 1, D_MODEL), F32)], axis=0)
    mods = _mods(cc, w_ada, b_ada)
    cos_t, sin_t = _rope_tables(n_batch * CTX_LEN)
    tables = _dft_tables()
    gate_lo = 4 * D_MODEL
    fn_lo = gate_lo + N_GATE + 3 * D_MODEL
    final_g2 = final_g.reshape(1, D_MODEL)

    for l in range(DEPTH):
        need_ctx = l < DEPTH - 1
        lam_init = 0.8 - 0.6 * math.exp(-0.3 * l)
        wl = w_in[l]
        w_main = jnp.concatenate([wl[:, :gate_lo], wl[:, gate_lo + N_GATE:fn_lo],
                                  wl[:, fn_lo + D_MODEL:], wl[:, fn_lo:fn_lo + D_MODEL]], axis=1).astype(BF16)
        w_gate = wl[:, gate_lo:gate_lo + N_GATE].astype(BF16)
        u, fn, gates = _inproj(xs, mods[l], norm_g[l, 0].reshape(1, D_MODEL), w_main, w_gate,
                               cos_t, sin_t, n_batch)
        hf, hb = _mlstm(u, gates, gates.T, ml_gate_b[l].reshape(1, N_GATE), n_batch)
        yd = _attn(u, da_lam[l], da_head_g[l].reshape(1, D_MODEL), lam_init, need_ctx, n_batch)
        yf = _fourier(fn, tables, need_ctx, n_batch)
        xs = _merge(hf, hb, u, yd, yf, xs, mods[l], ml_head_g[l].reshape(1, D_MODEL),
                    w_br_ml[l].astype(BF16), w_br_da[l].astype(BF16), w_br_fn[l].astype(BF16),
                    w_out[l].astype(BF16), need_ctx, n_batch)
        final = l == DEPTH - 1
        n_out = n_lat if final else xs.shape[0]
        xs = _ffn(xs, mods[l], norm_g[l, 1].reshape(1, D_MODEL), w_ffn_in[l].astype(BF16),
                  w_ffn_out[l].astype(BF16), final_g2, final, n_out, n_batch)
    return xs.reshape(n_batch, SEQ, D_MODEL)
```

```python
import functools
import math

import jax
import jax.numpy as jnp
from jax import lax
from jax.experimental import pallas as pl
from jax.experimental.pallas import tpu as pltpu

D_MODEL = 1024
SEQ = 4096
DEPTH = 4
CTX_LEN = 256
GRID_W = 64
NORM_EPS = 1e-6

ML_HEADS = 4
ML_DK = 256
ML_CHUNK = 128

DA_HEADS = 8
DA_DH = 64
DA_DV = 2 * DA_DH
ROPE_BASE = 10000.0
ATTN_ROW_CHUNKS = 2

FN_GROUPS = 4
FN_GC = 256
FFT_R = 64

D_FF = 2816
N_GATE = 4 * ML_HEADS
N_MOD = 6 * D_MODEL

U_QML, U_KML, U_VML, U_OML, U_QDA, U_KDA, U_VDA, U_GPRE = 0, 1, 2, 3, 4, 5, 6, 7
U_BLOCKS = 10
W_BLOCKS = U_BLOCKS + 1

VMEM_LIMIT_V7X = 56 * 1024 * 1024

BF16 = jnp.bfloat16
F32 = jnp.float32


def _cparams(sem):
    return pltpu.CompilerParams(dimension_semantics=sem, vmem_limit_bytes=VMEM_LIMIT_V7X)


def _dot(a, b):
    return jnp.dot(a, b, preferred_element_type=F32)


def _dot_nt(a, b):
    return lax.dot_general(a, b, (((1,), (1,)), ((), ())), preferred_element_type=F32)


def _dot_tn(a, b):
    return lax.dot_general(a, b, (((0,), (0,)), ((), ())), preferred_element_type=F32)


def _mod_row(mod_ref, i, rows_per_batch, n_batch, col):
    r = jnp.minimum(i // rows_per_batch, n_batch)
    return mod_ref[pl.ds(r, 1), col * D_MODEL:(col + 1) * D_MODEL]


def _row_tile(n_batch):
    return min(512, n_batch * CTX_LEN)


def _rms(x, g):
    return x * lax.rsqrt(jnp.mean(x * x, axis=-1, keepdims=True) + NORM_EPS) * g


def _mods_kernel(c_ref, w_ref, b_ref, o_ref):
    c = c_ref[...]
    s = (c * jax.nn.sigmoid(c)).astype(BF16)
    o_ref[0] = _dot(s, w_ref[0].astype(BF16)) + b_ref[0]


def _mods(cc, w_ada, b_ada):
    tn = 1536
    return pl.pallas_call(
        _mods_kernel,
        grid=(DEPTH, N_MOD // tn),
        in_specs=[
            pl.BlockSpec((8, D_MODEL), lambda l, j: (0, 0)),
            pl.BlockSpec((1, D_MODEL, tn), lambda l, j: (l, 0, j)),
            pl.BlockSpec((1, 1, tn), lambda l, j: (l, 0, j)),
        ],
        out_specs=pl.BlockSpec((1, 8, tn), lambda l, j: (l, 0, j)),
        out_shape=jax.ShapeDtypeStruct((DEPTH, 8, N_MOD), F32),
        compiler_params=_cparams(("arbitrary", "arbitrary")),
        name="adaln_mods",
    )(cc, w_ada, b_ada.reshape(DEPTH, 1, N_MOD))


def _inproj_kernel(x_ref, mod_ref, g_ref, w_ref, wg_ref, cos_ref, sin_ref,
                   u_ref, fn_ref, gate_ref, xn_ref, *, rows_per_batch, n_batch):
    i = pl.program_id(0)
    j = pl.program_id(1)

    @pl.when(j == 0)
    def _():
        y = _rms(x_ref[...], g_ref[...])
        shift = _mod_row(mod_ref, i, rows_per_batch, n_batch, 0)
        scale = _mod_row(mod_ref, i, rows_per_batch, n_batch, 1)
        xn_ref[...] = (y * (1.0 + scale) + shift).astype(BF16)
        gate_ref[...] = _dot(xn_ref[...], wg_ref[...])

    def product():
        return _dot(xn_ref[...], w_ref[...])

    is_rope = jnp.logical_or(j == U_QDA, j == U_KDA)
    is_plain = jnp.logical_and(jnp.logical_not(is_rope), jnp.logical_and(j != U_KML, j < U_BLOCKS))

    @pl.when(is_plain)
    def _():
        u_ref[...] = product().astype(BF16)

    @pl.when(j == U_KML)
    def _():
        u_ref[...] = (product() * (ML_DK ** -0.5)).astype(BF16)

    @pl.when(is_rope)
    def _():
        acc = product()
        cos = cos_ref[...]
        sin = sin_ref[...]
        for t in range(acc.shape[1] // DA_DV):
            sl = slice(t * DA_DV, (t + 1) * DA_DV)
            x = acc[:, sl]
            u_ref[:, sl] = (x * cos + pltpu.roll(x, DA_DV // 2, 1) * sin).astype(BF16)

    @pl.when(j == U_BLOCKS)
    def _():
        acc = product()
        for g in range(FN_GROUPS):
            fn_ref[g] = acc[:, g * FN_GC:(g + 1) * FN_GC].astype(BF16)


def _inproj(x, mods, g, w_main, w_gate, cos_t, sin_t, n_batch):
    n_rows = x.shape[0]
    tm = n_batch * CTX_LEN
    assert SEQ % tm == 0 and cos_t.shape[0] == SEQ + tm
    ni = n_rows // tm
    rows_per_batch = SEQ // tm
    lat_blocks = n_batch * rows_per_batch

    def tab_idx(i, j):
        return (jnp.where(i < lat_blocks, i % rows_per_batch, rows_per_batch), 0)

    kern = functools.partial(_inproj_kernel, rows_per_batch=rows_per_batch, n_batch=n_batch)
    return pl.pallas_call(
        kern,
        grid=(ni, W_BLOCKS),
        in_specs=[
            pl.BlockSpec((tm, D_MODEL), lambda i, j: (i, 0)),
            pl.BlockSpec((8, N_MOD), lambda i, j: (0, 0)),
            pl.BlockSpec((1, D_MODEL), lambda i, j: (0, 0)),
            pl.BlockSpec((D_MODEL, D_MODEL), lambda i, j: (0, j)),
            pl.BlockSpec((D_MODEL, N_GATE), lambda i, j: (0, 0)),
            pl.BlockSpec((tm, 128), tab_idx),
            pl.BlockSpec((tm, 128), tab_idx),
        ],
        out_specs=[
            pl.BlockSpec((tm, D_MODEL), lambda i, j: (i, jnp.minimum(j, U_BLOCKS - 1))),
            pl.BlockSpec((FN_GROUPS, tm, FN_GC), lambda i, j: (0, i, 0)),
            pl.BlockSpec((tm, N_GATE), lambda i, j: (i, 0)),
        ],
        out_shape=[
            jax.ShapeDtypeStruct((n_rows, U_BLOCKS * D_MODEL), BF16),
            jax.ShapeDtypeStruct((FN_GROUPS, n_rows, FN_GC), BF16),
            jax.ShapeDtypeStruct((n_rows, N_GATE), F32),
        ],
        scratch_shapes=[pltpu.VMEM((tm, D_MODEL), BF16)],
        compiler_params=_cparams(("arbitrary", "arbitrary")),
        name="in_proj",
    )(x, mods, g, w_main, w_gate, cos_t, sin_t)


def _mlstm_kernel(qf_ref, kf_ref, vf_ref, gf_ref, gtf_ref, qb_ref, kb_ref, vb_ref, gb_ref, gtb_ref,
                  bias_ref, biast_ref, hf_ref, hb_ref, c_ref, n_ref, m_ref):
    s = pl.program_id(1)
    L = ML_CHUNK

    @pl.when(s == 0)
    def _():
        c_ref[...] = jnp.zeros_like(c_ref)
        n_ref[...] = jnp.zeros_like(n_ref)
        m_ref[...] = jnp.zeros_like(m_ref)

    t_idx = lax.broadcasted_iota(jnp.int32, (L, L), 0)
    s_idx = lax.broadcasted_iota(jnp.int32, (L, L), 1)
    dirs = (
        (0, qf_ref, kf_ref, vf_ref, gf_ref, gtf_ref, hf_ref, s_idx <= t_idx),
        (1, qb_ref, kb_ref, vb_ref, gb_ref, gtb_ref, hb_ref, s_idx >= t_idx),
    )
    for d, q_ref, k_ref, v_ref, g_ref, gt_ref, h_ref, causal in dirs:
        gates = g_ref[...] + bias_ref[...]
        gates_t = gt_ref[...] + biast_ref[...]
        for h in range(ML_HEADS):
            st = d * ML_HEADS + h
            hs = slice(h * ML_DK, (h + 1) * ML_DK)
            gi = 2 * ML_HEADS * d + h
            gf = gi + ML_HEADS
            i_col, f_col = gates[:, gi:gi + 1], gates[:, gf:gf + 1]
            i_row, f_row = gates_t[gi:gi + 1, :], gates_t[gf:gf + 1, :]
            lf_col = jnp.minimum(f_col, 0.0) - jnp.log1p(jnp.exp(-jnp.abs(f_col)))
            lf_row = jnp.minimum(f_row, 0.0) - jnp.log1p(jnp.exp(-jnp.abs(f_row)))
            b_col = jnp.sum(jnp.where(causal, lf_row, 0.0), axis=1, keepdims=True)
            b_row = jnp.sum(jnp.where(causal, 0.0, lf_col) + jnp.where(t_idx == s_idx, lf_col, 0.0),
                            axis=0, keepdims=True)
            b_end = jnp.sum(lf_row, axis=1, keepdims=True)

            m_prev = m_ref[st][:, 0:1]
            dmat = jnp.where(causal, b_col - b_row + i_row, -jnp.inf)
            inter = b_col + m_prev
            m_t = jnp.maximum(inter, jnp.max(dmat, axis=1, keepdims=True))

            q = q_ref[:, hs]
            k = k_ref[:, hs]
            v = v_ref[:, hs]
            a = jnp.exp(dmat - m_t) * _dot_nt(q, k)
            sc = jnp.exp(inter - m_t)
            c_prev = c_ref[st]
            n_prev = n_ref[st]
            num = sc * _dot(q, c_prev.astype(BF16)) + _dot(a.astype(BF16), v)
            qn = jnp.sum(q.astype(F32) * n_prev, axis=1, keepdims=True)
            den = sc * qn + jnp.sum(a, axis=1, keepdims=True)
            inv = 1.0 / jnp.maximum(jnp.abs(den), jnp.exp(-m_t))
            h_ref[:, hs] = (num * inv).astype(h_ref.dtype)

            g_col = b_end - b_col + i_col
            m_new = jnp.maximum(b_end + m_prev, jnp.max(g_col, axis=0, keepdims=True))
            decay = jnp.exp(b_end + m_prev - m_new)
            kw = k.astype(F32) * jnp.exp(g_col - m_new)
            c_ref[st] = decay * c_prev + _dot_tn(kw.astype(BF16), v)
            n_ref[st] = decay * n_prev + jnp.sum(kw, axis=0, keepdims=True)
            m_ref[st] = jnp.broadcast_to(m_new, (1, 128))


def _mlstm(u, gates, gates_t, gate_b, n_batch):
    n_rows = u.shape[0]
    L = ML_CHUNK
    lat_chunks = SEQ // L
    ctx_chunks = CTX_LEN // L
    n_steps = ctx_chunks + lat_chunks
    ctx_base = n_batch * lat_chunks

    def rowblk(d):
        def f(b, s):
            in_ctx = s < ctx_chunks
            if d == 0:
                c = jnp.where(in_ctx, s, s - ctx_chunks)
            else:
                c = jnp.where(in_ctx, ctx_chunks - 1 - s, lat_chunks - 1 - (s - ctx_chunks))
            return jnp.where(in_ctx, ctx_base + ctx_chunks * b, lat_chunks * b) + c
        return f

    def dir_specs(d):
        rb = rowblk(d)
        return [
            pl.BlockSpec((L, D_MODEL), lambda b, s: (rb(b, s), U_QML)),
            pl.BlockSpec((L, D_MODEL), lambda b, s: (rb(b, s), U_KML)),
            pl.BlockSpec((L, D_MODEL), lambda b, s: (rb(b, s), U_VML)),
            pl.BlockSpec((L, N_GATE), lambda b, s: (rb(b, s), 0)),
            pl.BlockSpec((N_GATE, L), lambda b, s: (0, rb(b, s))),
        ]

    def out_spec(d):
        rb = rowblk(d)
        return pl.BlockSpec((L, D_MODEL), lambda b, s: (rb(b, s), 0))

    n_state = 2 * ML_HEADS
    return pl.pallas_call(
        _mlstm_kernel,
        grid=(n_batch, n_steps),
        in_specs=dir_specs(0) + dir_specs(1) + [
            pl.BlockSpec((1, N_GATE), lambda b, s: (0, 0)),
            pl.BlockSpec((N_GATE, 1), lambda b, s: (0, 0)),
        ],
        out_specs=[out_spec(0), out_spec(1)],
        out_shape=[jax.ShapeDtypeStruct((n_rows, D_MODEL), BF16)] * 2,
        scratch_shapes=[
            pltpu.VMEM((n_state, ML_DK, ML_DK), F32),
            pltpu.VMEM((n_state, 1, ML_DK), F32),
            pltpu.VMEM((n_state, 1, 128), F32),
        ],
        compiler_params=_cparams(("arbitrary", "arbitrary")),
        name="mlstm_scan",
    )(u, u, u, gates, gates_t, u, u, u, gates, gates_t, gate_b, gate_b.reshape(N_GATE, 1))


def _attn_lambda(lam_ref, lam_init):
    lq = lam_ref[...]
    return (jnp.exp(jnp.sum(lq[0:1] * lq[1:2], axis=1, keepdims=True))
            - jnp.exp(jnp.sum(lq[2:3] * lq[3:4], axis=1, keepdims=True)) + lam_init)


def _attn_queries(q_ref):
    q = q_ref[...]
    lane = lax.broadcasted_iota(jnp.int32, (1, DA_DV), 1)
    zero = jnp.zeros_like(q)
    is_map0 = (lane % DA_DH) < (DA_DH // 2)
    q2 = jnp.concatenate([jnp.where(is_map0, q, zero), jnp.where(is_map0, zero, q)], axis=0)
    return q2 * (DA_DH ** -0.5)


def _attn_values(s_chunks, m_chunks, vx, lam, g, lam_init, tq):
    accs = [_dot(jnp.exp(s - m).astype(BF16), vx) for s, m in zip(s_chunks, m_chunks)]
    acc = jnp.concatenate(accs, axis=0)
    o0 = acc[0:tq, 0:DA_DV] * (1.0 / acc[0:tq, DA_DV:DA_DV + 1])
    o1 = acc[tq:, 0:DA_DV] * (1.0 / acc[tq:, DA_DV:DA_DV + 1])
    return _rms(o0 - lam * o1, g) * (1.0 - lam_init)


def _attn_kernel(q_ref, kl_ref, vl_ref, kc_ref, vc_ref, lam_ref, g_ref, o_ref, kx_ref, vx_ref, s_ref, m_ref,
                 *, lam_init):
    qi = pl.program_id(2)
    tq = q_ref.shape[0]
    rows = 2 * tq // ATTN_ROW_CHUNKS

    @pl.when(qi == 0)
    def _():
        kx_ref[0:SEQ, :] = kl_ref[...]
        kx_ref[SEQ:, :] = kc_ref[...]
        vx_ref[0:SEQ, 0:DA_DV] = vl_ref[...]
        vx_ref[SEQ:, 0:DA_DV] = vc_ref[...]
        lane_v = lax.broadcasted_iota(jnp.int32, (SEQ + CTX_LEN, DA_DV), 1)
        vx_ref[:, DA_DV:] = jnp.where(lane_v == 0, 1.0, 0.0).astype(BF16)

    def score(slot):
        s = _dot_nt(_attn_queries(q_ref), kx_ref[...])
        m_ref[slot] = jnp.max(s, axis=1, keepdims=True)
        s_ref[slot] = s

    def drain(slot):
        s_chunks = [s_ref[slot, c * rows:(c + 1) * rows, :] for c in range(ATTN_ROW_CHUNKS)]
        m_chunks = [m_ref[slot, c * rows:(c + 1) * rows, :] for c in range(ATTN_ROW_CHUNKS)]
        o = _attn_values(s_chunks, m_chunks, vx_ref[...], _attn_lambda(lam_ref, lam_init), g_ref[...], lam_init, tq)
        o_ref[...] = o.astype(o_ref.dtype)

    @pl.when(qi == 0)
    def _():
        score(0)

    for parity in (0, 1):
        @pl.when(jnp.logical_and(qi > 0, qi % 2 == parity))
        def _():
            drain(1 - parity)
            score(parity)


def _attn_ctx_kernel(q_ref, kc_ref, vc_ref, lam_ref, g_ref, yin_ref, o_ref, *, lam_init):
    del yin_ref
    tq = q_ref.shape[0]
    s = _dot_nt(_attn_queries(q_ref), kc_ref[...])
    lane_v = lax.broadcasted_iota(jnp.int32, (CTX_LEN, DA_DV), 1)
    vx = jnp.concatenate([vc_ref[...], jnp.where(lane_v == 0, 1.0, 0.0).astype(BF16)], axis=1)
    o = _attn_values([s], [jnp.max(s, axis=1, keepdims=True)], vx, _attn_lambda(lam_ref, lam_init),
                     g_ref[...], lam_init, tq)
    o_ref[...] = o.astype(o_ref.dtype)


def _attn(u, da_lam, head_g, lam_init, need_ctx, n_batch):
    n_rows = u.shape[0]
    tq = CTX_LEN
    lat_qblocks = SEQ // tq
    ctx_base = n_batch * lat_qblocks
    cpb = D_MODEL // DA_DV
    n_keys = SEQ + CTX_LEN

    kern = functools.partial(_attn_kernel, lam_init=lam_init)
    yd = pl.pallas_call(
        kern,
        grid=(n_batch, DA_HEADS, lat_qblocks + 1),
        in_specs=[
            pl.BlockSpec((tq, DA_DV),
                         lambda b, h, qi: (b * lat_qblocks + jnp.minimum(qi, lat_qblocks - 1), U_QDA * cpb + h)),
            pl.BlockSpec((SEQ, DA_DV), lambda b, h, qi: (b, U_KDA * cpb + h)),
            pl.BlockSpec((SEQ, DA_DV), lambda b, h, qi: (b, U_VDA * cpb + h)),
            pl.BlockSpec((CTX_LEN, DA_DV), lambda b, h, qi: (ctx_base + b, U_KDA * cpb + h)),
            pl.BlockSpec((CTX_LEN, DA_DV), lambda b, h, qi: (ctx_base + b, U_VDA * cpb + h)),
            pl.BlockSpec((4, DA_DH), lambda b, h, qi: (0, 0)),
            pl.BlockSpec((1, DA_DV), lambda b, h, qi: (0, h)),
        ],
        out_specs=pl.BlockSpec((tq, DA_DV), lambda b, h, qi: (b * lat_qblocks + jnp.maximum(qi - 1, 0), h)),
        out_shape=jax.ShapeDtypeStruct((n_rows, D_MODEL), BF16),
        scratch_shapes=[
            pltpu.VMEM((n_keys, DA_DV), BF16),
            pltpu.VMEM((n_keys, 2 * DA_DV), BF16),
            pltpu.VMEM((2, 2 * tq, n_keys), F32),
            pltpu.VMEM((2, 2 * tq, 1), F32),
        ],
        compiler_params=_cparams(("arbitrary", "arbitrary", "arbitrary")),
        name="diff_attn",
    )(u, u, u, u, u, da_lam, head_g)
    if not need_ctx:
        return yd
    kern_ctx = functools.partial(_attn_ctx_kernel, lam_init=lam_init)
    return pl.pallas_call(
        kern_ctx,
        grid=(n_batch, DA_HEADS),
        in_specs=[
            pl.BlockSpec((CTX_LEN, DA_DV), lambda b, h: (ctx_base + b, U_QDA * cpb + h)),
            pl.BlockSpec((CTX_LEN, DA_DV), lambda b, h: (ctx_base + b, U_KDA * cpb + h)),
            pl.BlockSpec((CTX_LEN, DA_DV), lambda b, h: (ctx_base + b, U_VDA * cpb + h)),
            pl.BlockSpec((4, DA_DH), lambda b, h: (0, 0)),
            pl.BlockSpec((1, DA_DV), lambda b, h: (0, h)),
            pl.BlockSpec(memory_space=pl.ANY),
        ],
        out_specs=pl.BlockSpec((CTX_LEN, DA_DV), lambda b, h: (ctx_base + b, h)),
        out_shape=jax.ShapeDtypeStruct((n_rows, D_MODEL), BF16),
        input_output_aliases={5: 0},
        compiler_params=_cparams(("arbitrary", "arbitrary")),
        name="diff_attn_ctx",
    )(u, u, u, da_lam, head_g, yd)


def _dft_tables():
    R = FFT_R

    def cs(num, period):
        ang = (num % period).astype(F32) * (2.0 * math.pi / period)
        return jnp.cos(ang), jnp.sin(ang)

    idx = jnp.arange(R, dtype=jnp.int32)
    c1, s1 = cs(idx[:, None] * idx[None, :], R)
    a1 = jnp.concatenate([c1, -s1], axis=0).astype(BF16)
    f2 = idx[:, None, None]
    f1 = idx[None, :, None]
    t1 = idx[None, None, :]
    mc, ms = cs(t1 * (R * f1 + f2), SEQ)
    b2 = jnp.concatenate([jnp.concatenate([mc, ms], axis=2),
                          jnp.concatenate([-ms, mc], axis=2)], axis=1).astype(BF16)
    ch = jnp.arange(FN_GC, dtype=jnp.int32)
    cc, sc = cs(ch[:, None] * ch[None, :], FN_GC)
    csm = jnp.concatenate([cc, sc], axis=0).astype(BF16)
    actx = jnp.concatenate([cc, -sc], axis=0).astype(BF16)
    return a1, b2, csm, actx


def _fft1_kernel(a_ref, z_ref, p_ref):
    p_ref[0, 0] = _dot(a_ref[...], z_ref[0])


def _fft1(fn_view, a1, n_batch):
    R = FFT_R
    lanes = R * FN_GC
    lc = 4096
    return pl.pallas_call(
        _fft1_kernel,
        grid=(FN_GROUPS, n_batch, lanes // lc),
        in_specs=[
            pl.BlockSpec((2 * R, R), lambda g, b, c: (0, 0)),
            pl.BlockSpec((1, R, lc), lambda g, b, c: (g, b, c)),
        ],
        out_specs=pl.BlockSpec((1, 1, 2 * R, lc), lambda g, b, c: (g, b, 0, c)),
        out_shape=jax.ShapeDtypeStruct((FN_GROUPS, n_batch, 2 * R, lanes), F32),
        compiler_params=_cparams(("arbitrary", "arbitrary", "arbitrary")),
        name="fourier_stage1",
    )(a1, fn_view)


def _fft2_kernel(p_ref, b_ref, cs_ref, o_ref, *, f2b):
    R = FFT_R
    cc = cs_ref[0:FN_GC, :]
    sc = cs_ref[FN_GC:2 * FN_GC, :]
    norm = 1.0 / math.sqrt(SEQ * FN_GC)
    for g in range(FN_GROUPS):
        for jj in range(f2b):
            stacked = jnp.concatenate([p_ref[g, 0, 0, jj], p_ref[g, 0, 1, jj]], axis=0).astype(BF16)
            x = _dot(b_ref[jj], stacked)
            y = _dot(x[0:R].astype(BF16), cc) + _dot(x[R:2 * R].astype(BF16), sc)
            lo = jj * D_MODEL + g * FN_GC
            o_ref[:, lo:lo + FN_GC] = (y * norm).astype(o_ref.dtype)


def _fft2(p6, b2, csm, n_rows, n_batch):
    R = FFT_R
    f2b = 8
    kern = functools.partial(_fft2_kernel, f2b=f2b)
    return pl.pallas_call(
        kern,
        grid=(n_batch, R // f2b),
        in_specs=[
            pl.BlockSpec((FN_GROUPS, 1, 2, f2b, R, FN_GC), lambda b, f: (0, b, 0, f, 0, 0)),
            pl.BlockSpec((f2b, 2 * R, 2 * R), lambda b, f: (f, 0, 0)),
            pl.BlockSpec((2 * FN_GC, FN_GC), lambda b, f: (0, 0)),
        ],
        out_specs=pl.BlockSpec((R, f2b * D_MODEL), lambda b, f: (b, f)),
        out_shape=jax.ShapeDtypeStruct((n_rows // R, R * D_MODEL), BF16),
        compiler_params=_cparams(("arbitrary", "arbitrary")),
        name="fourier_stage2",
    )(p6, b2, csm)


def _fft_ctx_kernel(z_ref, a_ref, cs_ref, yin_ref, o_ref):
    del yin_ref
    cc = cs_ref[0:FN_GC, :]
    sc = cs_ref[FN_GC:2 * FN_GC, :]
    norm = 1.0 / math.sqrt(CTX_LEN * FN_GC)
    for g in range(FN_GROUPS):
        p = _dot(a_ref[...], z_ref[g])
        y = _dot(p[0:CTX_LEN].astype(BF16), cc) + _dot(p[CTX_LEN:2 * CTX_LEN].astype(BF16), sc)
        o_ref[:, g * FN_GC:(g + 1) * FN_GC] = (y * norm).astype(o_ref.dtype)


def _fft_ctx(fn, actx, csm, yf, n_batch):
    n_rows = yf.shape[0]
    ctx_base = n_batch * (SEQ // CTX_LEN)
    return pl.pallas_call(
        _fft_ctx_kernel,
        grid=(n_batch,),
        in_specs=[
            pl.BlockSpec((FN_GROUPS, CTX_LEN, FN_GC), lambda b: (0, ctx_base + b, 0)),
            pl.BlockSpec((2 * CTX_LEN, CTX_LEN), lambda b: (0, 0)),
            pl.BlockSpec((2 * FN_GC, FN_GC), lambda b: (0, 0)),
            pl.BlockSpec(memory_space=pl.ANY),
        ],
        out_specs=pl.BlockSpec((CTX_LEN, D_MODEL), lambda b: (ctx_base + b, 0)),
        out_shape=jax.ShapeDtypeStruct((n_rows, D_MODEL), BF16),
        input_output_aliases={3: 0},
        compiler_params=_cparams(("arbitrary",)),
        name="fourier_ctx",
    )(fn, actx, csm, yf)


def _fourier(fn, tables, need_ctx, n_batch):
    a1, b2, csm, actx = tables
    n_rows = fn.shape[1]
    R = FFT_R
    p = _fft1(fn.reshape(FN_GROUPS, n_rows // R, R * FN_GC), a1, n_batch)
    yf = _fft2(p.reshape(FN_GROUPS, n_batch, 2, R, R, FN_GC), b2, csm, n_rows, n_batch)
    yf = yf.reshape(n_rows, D_MODEL)
    if need_ctx:
        yf = _fft_ctx(fn, actx, csm, yf, n_batch)
    return yf


def _merge_kernel(hf_ref, hb_ref, o_ref, gm_ref, gd_ref, gf_ref, yd_ref, yf_ref, x_ref, mod_ref, hg_ref,
                  wml_ref, wda_ref, wfn_ref, wout_ref, out_ref, *, rows_per_batch, n_batch):
    i = pl.program_id(0)
    hsum = hf_ref[...].astype(F32) + hb_ref[...].astype(F32)
    hg = hg_ref[...]
    parts = []
    for h in range(ML_HEADS):
        hs = slice(h * ML_DK, (h + 1) * ML_DK)
        parts.append(_rms(hsum[:, hs], hg[:, hs]))
    ym = (jnp.concatenate(parts, axis=1) * jax.nn.sigmoid(o_ref[...].astype(F32))).astype(BF16)
    y = (jax.nn.sigmoid(gm_ref[...].astype(F32)) * _dot(ym, wml_ref[...])
         + jax.nn.sigmoid(gd_ref[...].astype(F32)) * _dot(yd_ref[...], wda_ref[...])
         + jax.nn.sigmoid(gf_ref[...].astype(F32)) * _dot(yf_ref[...], wfn_ref[...]))
    gate = _mod_row(mod_ref, i, rows_per_batch, n_batch, 2)
    out_ref[...] = x_ref[...] + gate * _dot(y.astype(BF16), wout_ref[...])


def _merge(hf, hb, u, yd, yf, x, mods, head_g, wml, wda, wfn, wout, need_ctx, n_batch):
    n_rows = x.shape[0]
    tm = _row_tile(n_batch)
    rows_per_batch = SEQ // tm
    ni = (n_rows if need_ctx else n_batch * SEQ) // tm
    kern = functools.partial(_merge_kernel, rows_per_batch=rows_per_batch, n_batch=n_batch)
    row = lambda i: (i, 0)
    full = lambda i: (0, 0)
    wspec = pl.BlockSpec((D_MODEL, D_MODEL), full)
    return pl.pallas_call(
        kern,
        grid=(ni,),
        in_specs=[
            pl.BlockSpec((tm, D_MODEL), row),
            pl.BlockSpec((tm, D_MODEL), row),
            pl.BlockSpec((tm, D_MODEL), lambda i: (i, U_OML)),
            pl.BlockSpec((tm, D_MODEL), lambda i: (i, U_GPRE)),
            pl.BlockSpec((tm, D_MODEL), lambda i: (i, U_GPRE + 1)),
            pl.BlockSpec((tm, D_MODEL), lambda i: (i, U_GPRE + 2)),
            pl.BlockSpec((tm, D_MODEL), row),
            pl.BlockSpec((tm, D_MODEL), row),
            pl.BlockSpec((tm, D_MODEL), row),
            pl.BlockSpec((8, N_MOD), full),
            pl.BlockSpec((1, D_MODEL), full),
            wspec, wspec, wspec, wspec,
        ],
        out_specs=pl.BlockSpec((tm, D_MODEL), row),
        out_shape=jax.ShapeDtypeStruct((n_rows, D_MODEL), F32),
        compiler_params=_cparams(("arbitrary",)),
        name="merge_out_proj",
    )(hf, hb, u, u, u, u, yd, yf, x, mods, head_g, wml, wda, wfn, wout)


FFN_CHUNKS = ((0, 1024), (1024, 1024), (2048, 768))


def _ffn_kernel(x_ref, mod_ref, g_ref, win_ref, wout_ref, fg_ref, out_ref, *, rows_per_batch, n_batch, final):
    i = pl.program_id(0)
    x = x_ref[...]
    shift = _mod_row(mod_ref, i, rows_per_batch, n_batch, 3)
    scale = _mod_row(mod_ref, i, rows_per_batch, n_batch, 4)
    gate = _mod_row(mod_ref, i, rows_per_batch, n_batch, 5)
    h = (_rms(x, g_ref[...]) * (1.0 + scale) + shift).astype(BF16)
    acc = None
    for lo, width in FFN_CHUNKS:
        a = _dot(h, win_ref[:, lo:lo + width])
        b = _dot(h, win_ref[:, D_FF + lo:D_FF + lo + width])
        act = (a * jax.nn.sigmoid(a) * b).astype(BF16)
        part = _dot(act, wout_ref[lo:lo + width, :])
        acc = part if acc is None else acc + part
    xn = x + gate * acc
    out_ref[...] = _rms(xn, fg_ref[...]) if final else xn


def _ffn(x, mods, g, w_in, w_out, final_g, final, n_rows_out, n_batch):
    tm = _row_tile(n_batch)
    rows_per_batch = SEQ // tm
    kern = functools.partial(_ffn_kernel, rows_per_batch=rows_per_batch, n_batch=n_batch, final=final)
    row = lambda i: (i, 0)
    full = lambda i: (0, 0)
    return pl.pallas_call(
        kern,
        grid=(n_rows_out // tm,),
        in_specs=[
            pl.BlockSpec((tm, D_MODEL), row),
            pl.BlockSpec((8, N_MOD), full),
            pl.BlockSpec((1, D_MODEL), full),
            pl.BlockSpec((D_MODEL, 2 * D_FF), full),
            pl.BlockSpec((D_FF, D_MODEL), full),
            pl.BlockSpec((1, D_MODEL), full),
        ],
        out_specs=pl.BlockSpec((tm, D_MODEL), row),
        out_shape=jax.ShapeDtypeStruct((n_rows_out, D_MODEL), F32),
        compiler_params=_cparams(("arbitrary",)),
        name="swiglu_ffn",
    )(x, mods, g, w_in, w_out, final_g)


def _da_col_perm(w):
    half = DA_DH // 2
    return w.reshape(-1, DA_HEADS, 2, 2, half).transpose(0, 1, 3, 2, 4).reshape(-1, DA_HEADS * DA_DV)


def _rope_tables(pad):
    n_freq = DA_DH // 4
    rows = SEQ // GRID_W
    inv = ROPE_BASE ** (-jnp.arange(n_freq, dtype=F32) / n_freq)
    r = jnp.repeat(jnp.arange(rows, dtype=F32), GRID_W)
    col = jnp.tile(jnp.arange(GRID_W, dtype=F32), rows)
    ang = jnp.concatenate([r[:, None] * inv, col[:, None] * inv], axis=-1)
    cos, sin = jnp.cos(ang), jnp.sin(ang)
    cos_t = jnp.concatenate([cos, cos, cos, cos], axis=-1)
    sin_t = jnp.concatenate([-sin, -sin, sin, sin], axis=-1)
    cos_t = jnp.concatenate([cos_t, jnp.ones((pad, 128), F32)], axis=0)
    sin_t = jnp.concatenate([sin_t, jnp.zeros((pad, 128), F32)], axis=0)
    return cos_t, sin_t


def kernel(x, c, ctx, c_ctx, w_ada, b_ada, norm_g, w_in, ml_gate_b, ml_head_g, da_lam, da_head_g,
           w_br_ml, w_br_da, w_br_fn, w_out, w_ffn_in, w_ffn_out, final_g):
    n_batch = x.shape[0]
    n_lat = n_batch * SEQ
    xs = jnp.concatenate([x.reshape(n_lat, D_MODEL), ctx.reshape(n_batch * CTX_LEN, D_MODEL)], axis=0)
    cc = jnp.concatenate([c, c_ctx[None, :], jnp.zeros((8 - n_batch - 1, D_MODEL), F32)], axis=0)
    mods = _mods(cc, w_ada, b_ada)
    cos_t, sin_t = _rope_tables(n_batch * CTX_LEN)
    tables = _dft_tables()
    gate_lo = 4 * D_MODEL
    fn_lo = gate_lo + N_GATE + 3 * D_MODEL
    final_g2 = final_g.reshape(1, D_MODEL)

    for l in range(DEPTH):
        need_ctx = l < DEPTH - 1
        lam_init = 0.8 - 0.6 * math.exp(-0.3 * l)
        wl = w_in[l]
        da_lo = gate_lo + N_GATE
        w_main = jnp.concatenate([wl[:, :gate_lo],
                                  _da_col_perm(wl[:, da_lo:da_lo + D_MODEL]),
                                  _da_col_perm(wl[:, da_lo + D_MODEL:da_lo + 2 * D_MODEL]),
                                  wl[:, da_lo + 2 * D_MODEL:fn_lo],
                                  wl[:, fn_lo + D_MODEL:], wl[:, fn_lo:fn_lo + D_MODEL]], axis=1).astype(BF16)
        w_gate = wl[:, gate_lo:gate_lo + N_GATE].astype(BF16)
        u, fn, gates = _inproj(xs, mods[l], norm_g[l, 0].reshape(1, D_MODEL), w_main, w_gate,
                               cos_t, sin_t, n_batch)
        hf, hb = _mlstm(u, gates, gates.T, ml_gate_b[l].reshape(1, N_GATE), n_batch)
        yd = _attn(u, da_lam[l], da_head_g[l].reshape(1, D_MODEL), lam_init, need_ctx, n_batch)
        yf = _fourier(fn, tables, need_ctx, n_batch)
        xs = _merge(hf, hb, u, yd, yf, xs, mods[l], ml_head_g[l].reshape(1, D_MODEL),
                    w_br_ml[l].astype(BF16), w_br_da[l].astype(BF16), w_br_fn[l].astype(BF16),
                    w_out[l].astype(BF16), need_ctx, n_batch)
        final = l == DEPTH - 1
        n_out = n_lat if final else xs.shape[0]
        xs = _ffn(xs, mods[l], norm_g[l, 1].reshape(1, D_MODEL), w_ffn_in[l].astype(BF16),
                  w_ffn_out[l].astype(BF16), final_g2, final, n_out, n_batch)
    return xs.reshape(n_batch, SEQ, D_MODEL)
```

```python
import functools
import math

import jax
import jax.numpy as jnp
from jax import lax
from jax.experimental import pallas as pl
from jax.experimental.pallas import tpu as pltpu

D_MODEL = 1024
SEQ = 4096
DEPTH = 4
CTX_LEN = 256
GRID_W = 64
NORM_EPS = 1e-6

ML_HEADS = 4
ML_DK = 256
ML_CHUNK = 128

DA_HEADS = 8
DA_DH = 64
DA_DV = 2 * DA_DH
ROPE_BASE = 10000.0
ATTN_ROW_CHUNKS = 2

FN_GROUPS = 4
FN_GC = 256
FFT_R = 64

D_FF = 2816
N_GATE = 4 * ML_HEADS
N_MOD = 6 * D_MODEL

U_QML, U_VML, U_OML, U_QDA, U_KDA, U_VDA, U_GPRE = 0, 1, 2, 3, 4, 5, 6
U_BLOCKS = 9
W_BLOCKS = U_BLOCKS + 1
STEP_KT = 1
N_COL_STEPS = W_BLOCKS + 1

VMEM_LIMIT_V7X = 56 * 1024 * 1024

BF16 = jnp.bfloat16
F32 = jnp.float32


def _cparams(sem):
    return pltpu.CompilerParams(dimension_semantics=sem, vmem_limit_bytes=VMEM_LIMIT_V7X)


def _dot(a, b):
    return jnp.dot(a, b, preferred_element_type=F32)


def _dot_nt(a, b):
    return lax.dot_general(a, b, (((1,), (1,)), ((), ())), preferred_element_type=F32)


def _dot_tn(a, b):
    return lax.dot_general(a, b, (((0,), (0,)), ((), ())), preferred_element_type=F32)


def _mod_row(mod_ref, i, rows_per_batch, n_batch, col):
    r = jnp.minimum(i // rows_per_batch, n_batch)
    return mod_ref[pl.ds(r, 1), col * D_MODEL:(col + 1) * D_MODEL]


def _row_tile(n_batch):
    return min(512, n_batch * CTX_LEN)


def _rms(x, g):
    return x * lax.rsqrt(jnp.mean(x * x, axis=-1, keepdims=True) + NORM_EPS) * g


def _mods_kernel(c_ref, w_ref, b_ref, o_ref):
    c = c_ref[...]
    s = (c * jax.nn.sigmoid(c)).astype(BF16)
    o_ref[0] = _dot(s, w_ref[0].astype(BF16)) + b_ref[0]


def _mods(cc, w_ada, b_ada):
    tn = 1536
    return pl.pallas_call(
        _mods_kernel,
        grid=(DEPTH, N_MOD // tn),
        in_specs=[
            pl.BlockSpec((8, D_MODEL), lambda l, j: (0, 0)),
            pl.BlockSpec((1, D_MODEL, tn), lambda l, j: (l, 0, j)),
            pl.BlockSpec((1, 1, tn), lambda l, j: (l, 0, j)),
        ],
        out_specs=pl.BlockSpec((1, 8, tn), lambda l, j: (l, 0, j)),
        out_shape=jax.ShapeDtypeStruct((DEPTH, 8, N_MOD), F32),
        compiler_params=_cparams(("arbitrary", "arbitrary")),
        name="adaln_mods",
    )(cc, w_ada, b_ada.reshape(DEPTH, 1, N_MOD))


def _inproj_kernel(x_ref, mod_ref, g_ref, w_ref, wkt_ref, wgt_ref, cos_ref, sin_ref,
                   u_ref, kt_ref, fn_ref, gate_ref, xn_ref, *, rows_per_batch, n_batch):
    i = pl.program_id(0)
    j = pl.program_id(1)

    @pl.when(j == 0)
    def _():
        y = _rms(x_ref[...], g_ref[...])
        shift = _mod_row(mod_ref, i, rows_per_batch, n_batch, 0)
        scale = _mod_row(mod_ref, i, rows_per_batch, n_batch, 1)
        xn_ref[...] = (y * (1.0 + scale) + shift).astype(BF16)
        gate_ref[...] = _dot_nt(wgt_ref[...], xn_ref[...])

    def product():
        return _dot(xn_ref[...], w_ref[...])

    blk = jnp.where(j == 0, 0, j - 1)
    is_rope = jnp.logical_or(blk == U_QDA, blk == U_KDA)
    is_plain = jnp.logical_and(j != STEP_KT, jnp.logical_and(jnp.logical_not(is_rope), blk < U_BLOCKS))

    @pl.when(is_plain)
    def _():
        u_ref[...] = product().astype(BF16)

    @pl.when(j == STEP_KT)
    def _():
        kt_ref[...] = (_dot_nt(wkt_ref[...], xn_ref[...]) * (ML_DK ** -0.5)).astype(BF16)

    @pl.when(is_rope)
    def _():
        acc = product()
        cos = cos_ref[...]
        sin = sin_ref[...]
        for t in range(acc.shape[1] // DA_DV):
            sl = slice(t * DA_DV, (t + 1) * DA_DV)
            x = acc[:, sl]
            u_ref[:, sl] = (x * cos + pltpu.roll(x, DA_DV // 2, 1) * sin).astype(BF16)

    @pl.when(blk == U_BLOCKS)
    def _():
        acc = product()
        for g in range(FN_GROUPS):
            fn_ref[g] = acc[:, g * FN_GC:(g + 1) * FN_GC].astype(BF16)


def _inproj(x, mods, g, w_main, w_kt, w_gate_t, cos_t, sin_t, n_batch):
    n_rows = x.shape[0]
    tm = n_batch * CTX_LEN
    assert SEQ % tm == 0 and cos_t.shape[0] == SEQ + tm
    ni = n_rows // tm
    rows_per_batch = SEQ // tm
    lat_blocks = n_batch * rows_per_batch

    def tab_idx(i, j):
        return (jnp.where(i < lat_blocks, i % rows_per_batch, rows_per_batch), 0)

    def w_blk(j):
        return jnp.where(j == 0, 0, j - 1)

    kern = functools.partial(_inproj_kernel, rows_per_batch=rows_per_batch, n_batch=n_batch)
    return pl.pallas_call(
        kern,
        grid=(ni, N_COL_STEPS),
        in_specs=[
            pl.BlockSpec((tm, D_MODEL), lambda i, j: (i, 0)),
            pl.BlockSpec((8, N_MOD), lambda i, j: (0, 0)),
            pl.BlockSpec((1, D_MODEL), lambda i, j: (0, 0)),
            pl.BlockSpec((D_MODEL, D_MODEL), lambda i, j: (0, w_blk(j))),
            pl.BlockSpec((D_MODEL, D_MODEL), lambda i, j: (0, 0)),
            pl.BlockSpec((N_GATE, D_MODEL), lambda i, j: (0, 0)),
            pl.BlockSpec((tm, 128), tab_idx),
            pl.BlockSpec((tm, 128), tab_idx),
        ],
        out_specs=[
            pl.BlockSpec((tm, D_MODEL), lambda i, j: (i, jnp.minimum(w_blk(j), U_BLOCKS - 1))),
            pl.BlockSpec((D_MODEL, tm), lambda i, j: (0, i)),
            pl.BlockSpec((FN_GROUPS, tm, FN_GC), lambda i, j: (0, i, 0)),
            pl.BlockSpec((N_GATE, tm), lambda i, j: (0, i)),
        ],
        out_shape=[
            jax.ShapeDtypeStruct((n_rows, U_BLOCKS * D_MODEL), BF16),
            jax.ShapeDtypeStruct((D_MODEL, n_rows), BF16),
            jax.ShapeDtypeStruct((FN_GROUPS, n_rows, FN_GC), BF16),
            jax.ShapeDtypeStruct((N_GATE, n_rows), F32),
        ],
        scratch_shapes=[pltpu.VMEM((tm, D_MODEL), BF16)],
        compiler_params=_cparams(("arbitrary", "arbitrary")),
        name="in_proj",
    )(x, mods, g, w_main, w_kt, w_gate_t, cos_t, sin_t)


def _split3(x):
    hi = x.astype(BF16).astype(F32)
    mid = (x - hi).astype(BF16).astype(F32)
    lo = (x - hi - mid).astype(BF16).astype(F32)
    return hi, mid, lo


def _mlstm_kernel(qf_ref, ktf_ref, vf_ref, gtf_ref, qb_ref, ktb_ref, vb_ref, gtb_ref, bias_ref,
                  hf_ref, hb_ref, *state_refs):
    cx_refs, m_refs = state_refs[:2 * ML_HEADS], state_refs[2 * ML_HEADS:]
    s = pl.program_id(1)
    L = ML_CHUNK
    H = ML_HEADS
    W = 128

    @pl.when(s == 0)
    def _():
        for ref in state_refs:
            ref[...] = jnp.zeros_like(ref)

    t_idx = lax.broadcasted_iota(jnp.int32, (L, L), 0)
    s_idx = lax.broadcasted_iota(jnp.int32, (L, L), 1)
    eye = t_idx == s_idx
    sub8 = lax.broadcasted_iota(jnp.int32, (8, W), 0)
    ones_w = jnp.ones((L, W), BF16)
    er = lax.broadcasted_iota(jnp.int32, (4 * L, 2 * W), 0)
    ec = lax.broadcasted_iota(jnp.int32, (4 * L, 2 * W), 1)
    expand = jnp.where((er < 3 * L) == (ec < W), 1.0, 0.0).astype(BF16)

    def running_max_rows(x, d):
        n_tiles = L // 8
        out = [None] * n_tiles
        carry = None
        for j in (range(n_tiles) if d == 0 else range(n_tiles - 1, -1, -1)):
            r = x[8 * j:8 * (j + 1)]
            k = 1
            while k < 8:
                if d == 0:
                    r = jnp.maximum(r, jnp.where(sub8 >= k, pltpu.roll(r, k, 0), -jnp.inf))
                else:
                    r = jnp.maximum(r, jnp.where(sub8 < 8 - k, pltpu.roll(r, 8 - k, 0), -jnp.inf))
                k *= 2
            if carry is not None:
                r = jnp.maximum(r, carry)
            carry = jnp.broadcast_to(r[7:8] if d == 0 else r[0:1], (8, W))
            out[j] = r
        return jnp.concatenate(out, axis=0)

    def prelude(d, gt_ref):
        before = (t_idx <= s_idx) if d == 0 else (t_idx >= s_idx)
        gt = gt_ref[2 * H * d:2 * H * (d + 1), :] + bias_ref[2 * H * d:2 * H * (d + 1), :]
        i4, f4 = gt[0:H], gt[H:2 * H]
        lf4 = jnp.minimum(f4, 0.0) - jnp.log1p(jnp.exp(-jnp.abs(f4)))
        lf_terms = jnp.concatenate(list(_split3(lf4)) + [jnp.zeros((H, L), F32)], axis=0).astype(BF16)
        cum_rhs = jnp.concatenate([jnp.where(before, 1.0, 0.0).astype(BF16), ones_w], axis=1)
        r = _dot(lf_terms, cum_rhs)
        bx = r[0:H] + r[H:2 * H] + r[2 * H:3 * H]
        b4, bend4 = bx[:, 0:L], bx[:, L:L + W]
        c4 = i4 - b4
        m_prev4 = m_refs[d][...]
        g4 = bend4 - b4 + i4
        m_new4 = jnp.maximum(bend4 + m_prev4, jnp.max(g4, axis=1, keepdims=True))
        m_refs[d][...] = m_new4
        return dict(c4=c4, col_terms=_split3(b4) + (c4.astype(BF16).astype(F32),), m_prev4=m_prev4,
                    decay4=jnp.exp(bend4 + m_prev4 - m_new4), w4=jnp.exp(g4 - m_new4))

    dirs = ((0, qf_ref, ktf_ref, vf_ref, gtf_ref, hf_ref), (1, qb_ref, ktb_ref, vb_ref, gtb_ref, hb_ref))
    pre = [prelude(d, gt_ref) for d, _, _, _, gt_ref, _ in dirs]
    for d, q_ref, kt_ref, v_ref, _, h_ref in dirs:
        causal = (s_idx <= t_idx) if d == 0 else (s_idx >= t_idx)
        c4, m_prev4, decay4, w4 = pre[d]["c4"], pre[d]["m_prev4"], pre[d]["decay4"], pre[d]["w4"]
        for h in range(H):
            st = d * H + h
            hs = slice(h * ML_DK, (h + 1) * ML_DK)
            diag = jnp.concatenate([jnp.where(eye, x[h:h + 1, :], 0.0) for x in pre[d]["col_terms"]],
                                   axis=1).astype(BF16)
            wide = _dot(diag, expand)
            b_w = wide[:, 0:W]
            m_w = jnp.maximum(running_max_rows(wide[:, W:2 * W], d), m_prev4[h:h + 1, :])

            q = q_ref[:, hs]
            kt = kt_ref[hs, :]
            vx = jnp.concatenate([v_ref[:, hs], ones_w], axis=1)
            a = (jnp.where(causal, jnp.exp(c4[h:h + 1, :] - m_w), 0.0) * _dot(q, kt)).astype(BF16)
            cx_prev = cx_refs[st][...]
            qc = _dot(q, cx_prev.astype(BF16))
            av = _dot(a, vx)
            sc_w = jnp.exp(m_prev4[h:h + 1, :] - m_w)
            den = sc_w * qc[:, ML_DK:] + av[:, ML_DK:]
            inv = 1.0 / jnp.maximum(jnp.abs(den), jnp.exp(-(b_w + m_w)))
            for t in range(ML_DK // W):
                ts = slice(t * W, (t + 1) * W)
                h_ref[:, h * ML_DK + t * W:h * ML_DK + (t + 1) * W] = (
                    (sc_w * qc[:, ts] + av[:, ts]) * inv).astype(h_ref.dtype)

            kw = (kt.astype(F32) * w4[h:h + 1, :]).astype(BF16)
            dec = jnp.concatenate([decay4[h:h + 1, :]] * (ML_DK // W + 1), axis=1)
            cx_refs[st][...] = dec * cx_prev + _dot(kw, vx)


def _mlstm(u, kt, gates_t, gate_b, n_batch):
    n_rows = u.shape[0]
    L = ML_CHUNK
    lat_chunks = SEQ // L
    ctx_chunks = CTX_LEN // L
    n_steps = ctx_chunks + lat_chunks
    ctx_base = n_batch * lat_chunks

    def rowblk(d):
        def f(b, s):
            in_ctx = s < ctx_chunks
            if d == 0:
                c = jnp.where(in_ctx, s, s - ctx_chunks)
            else:
                c = jnp.where(in_ctx, ctx_chunks - 1 - s, lat_chunks - 1 - (s - ctx_chunks))
            return jnp.where(in_ctx, ctx_base + ctx_chunks * b, lat_chunks * b) + c
        return f

    def dir_specs(d):
        rb = rowblk(d)
        return [
            pl.BlockSpec((L, D_MODEL), lambda b, s: (rb(b, s), U_QML)),
            pl.BlockSpec((D_MODEL, L), lambda b, s: (0, rb(b, s))),
            pl.BlockSpec((L, D_MODEL), lambda b, s: (rb(b, s), U_VML)),
            pl.BlockSpec((N_GATE, L), lambda b, s: (0, rb(b, s))),
        ]

    def out_spec(d):
        rb = rowblk(d)
        return pl.BlockSpec((L, D_MODEL), lambda b, s: (rb(b, s), 0))

    return pl.pallas_call(
        _mlstm_kernel,
        grid=(n_batch, n_steps),
        in_specs=dir_specs(0) + dir_specs(1) + [pl.BlockSpec((N_GATE, 1), lambda b, s: (0, 0))],
        out_specs=[out_spec(0), out_spec(1)],
        out_shape=[jax.ShapeDtypeStruct((n_rows, D_MODEL), BF16)] * 2,
        scratch_shapes=(
            [pltpu.VMEM((ML_DK, ML_DK + 128), F32)] * (2 * ML_HEADS)
            + [pltpu.VMEM((ML_HEADS, 128), F32)] * 2),
        compiler_params=_cparams(("arbitrary", "arbitrary")),
        name="mlstm_scan",
    )(u, kt, u, gates_t, u, kt, u, gates_t, gate_b.reshape(N_GATE, 1))


def _attn_lambda(lam_ref, lam_init):
    lq = lam_ref[...]
    return (jnp.exp(jnp.sum(lq[0:1] * lq[1:2], axis=1, keepdims=True))
            - jnp.exp(jnp.sum(lq[2:3] * lq[3:4], axis=1, keepdims=True)) + lam_init)


def _attn_queries(q_ref):
    q = q_ref[...]
    lane = lax.broadcasted_iota(jnp.int32, (1, DA_DV), 1)
    zero = jnp.zeros_like(q)
    is_map0 = (lane % DA_DH) < (DA_DH // 2)
    q2 = jnp.concatenate([jnp.where(is_map0, q, zero), jnp.where(is_map0, zero, q)], axis=0)
    return q2 * (DA_DH ** -0.5)


def _attn_values(s_chunks, m_chunks, vx, lam, g, lam_init, tq):
    accs = [_dot(jnp.exp(s - m).astype(BF16), vx) for s, m in zip(s_chunks, m_chunks)]
    acc = jnp.concatenate(accs, axis=0)
    o0 = acc[0:tq, 0:DA_DV] * (1.0 / acc[0:tq, DA_DV:DA_DV + 1])
    o1 = acc[tq:, 0:DA_DV] * (1.0 / acc[tq:, DA_DV:DA_DV + 1])
    return _rms(o0 - lam * o1, g) * (1.0 - lam_init)


def _attn_kernel(q_ref, kl_ref, vl_ref, kc_ref, vc_ref, lam_ref, g_ref, o_ref, kx_ref, vx_ref, s_ref, m_ref,
                 *, lam_init):
    qi = pl.program_id(2)
    tq = q_ref.shape[0]
    rows = 2 * tq // ATTN_ROW_CHUNKS

    @pl.when(qi == 0)
    def _():
        kx_ref[0:SEQ, :] = kl_ref[...]
        kx_ref[SEQ:, :] = kc_ref[...]
        vx_ref[0:SEQ, 0:DA_DV] = vl_ref[...]
        vx_ref[SEQ:, 0:DA_DV] = vc_ref[...]
        lane_v = lax.broadcasted_iota(jnp.int32, (SEQ + CTX_LEN, DA_DV), 1)
        vx_ref[:, DA_DV:] = jnp.where(lane_v == 0, 1.0, 0.0).astype(BF16)

    def score(slot):
        s = _dot_nt(_attn_queries(q_ref), kx_ref[...])
        m_ref[slot] = jnp.max(s, axis=1, keepdims=True)
        s_ref[slot] = s

    def drain(slot):
        s_chunks = [s_ref[slot, c * rows:(c + 1) * rows, :] for c in range(ATTN_ROW_CHUNKS)]
        m_chunks = [m_ref[slot, c * rows:(c + 1) * rows, :] for c in range(ATTN_ROW_CHUNKS)]
        o = _attn_values(s_chunks, m_chunks, vx_ref[...], _attn_lambda(lam_ref, lam_init), g_ref[...], lam_init, tq)
        o_ref[...] = o.astype(o_ref.dtype)

    @pl.when(qi == 0)
    def _():
        score(0)

    for parity in (0, 1):
        @pl.when(jnp.logical_and(qi > 0, qi % 2 == parity))
        def _():
            drain(1 - parity)
            score(parity)


def _attn_ctx_kernel(q_ref, kc_ref, vc_ref, lam_ref, g_ref, yin_ref, o_ref, *, lam_init):
    del yin_ref
    tq = q_ref.shape[0]
    s = _dot_nt(_attn_queries(q_ref), kc_ref[...])
    lane_v = lax.broadcasted_iota(jnp.int32, (CTX_LEN, DA_DV), 1)
    vx = jnp.concatenate([vc_ref[...], jnp.where(lane_v == 0, 1.0, 0.0).astype(BF16)], axis=1)
    o = _attn_values([s], [jnp.max(s, axis=1, keepdims=True)], vx, _attn_lambda(lam_ref, lam_init),
                     g_ref[...], lam_init, tq)
    o_ref[...] = o.astype(o_ref.dtype)


def _attn(u, da_lam, head_g, lam_init, need_ctx, n_batch):
    n_rows = u.shape[0]
    tq = CTX_LEN
    lat_qblocks = SEQ // tq
    ctx_base = n_batch * lat_qblocks
    cpb = D_MODEL // DA_DV
    n_keys = SEQ + CTX_LEN

    kern = functools.partial(_attn_kernel, lam_init=lam_init)
    yd = pl.pallas_call(
        kern,
        grid=(n_batch, DA_HEADS, lat_qblocks + 1),
        in_specs=[
            pl.BlockSpec((tq, DA_DV),
                         lambda b, h, qi: (b * lat_qblocks + jnp.minimum(qi, lat_qblocks - 1), U_QDA * cpb + h)),
            pl.BlockSpec((SEQ, DA_DV), lambda b, h, qi: (b, U_KDA * cpb + h)),
            pl.BlockSpec((SEQ, DA_DV), lambda b, h, qi: (b, U_VDA * cpb + h)),
            pl.BlockSpec((CTX_LEN, DA_DV), lambda b, h, qi: (ctx_base + b, U_KDA * cpb + h)),
            pl.BlockSpec((CTX_LEN, DA_DV), lambda b, h, qi: (ctx_base + b, U_VDA * cpb + h)),
            pl.BlockSpec((4, DA_DH), lambda b, h, qi: (0, 0)),
            pl.BlockSpec((1, DA_DV), lambda b, h, qi: (0, h)),
        ],
        out_specs=pl.BlockSpec((tq, DA_DV), lambda b, h, qi: (b * lat_qblocks + jnp.maximum(qi - 1, 0), h)),
        out_shape=jax.ShapeDtypeStruct((n_rows, D_MODEL), BF16),
        scratch_shapes=[
            pltpu.VMEM((n_keys, DA_DV), BF16),
            pltpu.VMEM((n_keys, 2 * DA_DV), BF16),
            pltpu.VMEM((2, 2 * tq, n_keys), F32),
            pltpu.VMEM((2, 2 * tq, 1), F32),
        ],
        compiler_params=_cparams(("arbitrary", "arbitrary", "arbitrary")),
        name="diff_attn",
    )(u, u, u, u, u, da_lam, head_g)
    if not need_ctx:
        return yd
    kern_ctx = functools.partial(_attn_ctx_kernel, lam_init=lam_init)
    return pl.pallas_call(
        kern_ctx,
        grid=(n_batch, DA_HEADS),
        in_specs=[
            pl.BlockSpec((CTX_LEN, DA_DV), lambda b, h: (ctx_base + b, U_QDA * cpb + h)),
            pl.BlockSpec((CTX_LEN, DA_DV), lambda b, h: (ctx_base + b, U_KDA * cpb + h)),
            pl.BlockSpec((CTX_LEN, DA_DV), lambda b, h: (ctx_base + b, U_VDA * cpb + h)),
            pl.BlockSpec((4, DA_DH), lambda b, h: (0, 0)),
            pl.BlockSpec((1, DA_DV), lambda b, h: (0, h)),
            pl.BlockSpec(memory_space=pl.ANY),
        ],
        out_specs=pl.BlockSpec((CTX_LEN, DA_DV), lambda b, h: (ctx_base + b, h)),
        out_shape=jax.ShapeDtypeStruct((n_rows, D_MODEL), BF16),
        input_output_aliases={5: 0},
        compiler_params=_cparams(("arbitrary", "arbitrary")),
        name="diff_attn_ctx",
    )(u, u, u, da_lam, head_g, yd)


def _dft_tables():
    R = FFT_R

    def cs(num, period):
        ang = (num % period).astype(F32) * (2.0 * math.pi / period)
        return jnp.cos(ang), jnp.sin(ang)

    idx = jnp.arange(R, dtype=jnp.int32)
    c1, s1 = cs(idx[:, None] * idx[None, :], R)
    a1 = jnp.concatenate([c1, -s1], axis=0).astype(BF16)
    f2 = idx[:, None, None]
    f1 = idx[None, :, None]
    t1 = idx[None, None, :]
    mc, ms = cs(t1 * (R * f1 + f2), SEQ)
    b2 = jnp.concatenate([jnp.concatenate([mc, ms], axis=2),
                          jnp.concatenate([-ms, mc], axis=2)], axis=1).astype(BF16)
    ch = jnp.arange(FN_GC, dtype=jnp.int32)
    cc, sc = cs(ch[:, None] * ch[None, :], FN_GC)
    csm = jnp.concatenate([cc, sc], axis=0).astype(BF16)
    actx = jnp.concatenate([cc, -sc], axis=0).astype(BF16)
    return a1, b2, csm, actx


def _fft1_kernel(a_ref, z_ref, p_ref):
    p_ref[0, 0] = _dot(a_ref[...], z_ref[0])


def _fft1(fn_view, a1, n_batch):
    R = FFT_R
    lanes = R * FN_GC
    lc = 4096
    return pl.pallas_call(
        _fft1_kernel,
        grid=(FN_GROUPS, n_batch, lanes // lc),
        in_specs=[
            pl.BlockSpec((2 * R, R), lambda g, b, c: (0, 0)),
            pl.BlockSpec((1, R, lc), lambda g, b, c: (g, b, c)),
        ],
        out_specs=pl.BlockSpec((1, 1, 2 * R, lc), lambda g, b, c: (g, b, 0, c)),
        out_shape=jax.ShapeDtypeStruct((FN_GROUPS, n_batch, 2 * R, lanes), F32),
        compiler_params=_cparams(("arbitrary", "arbitrary", "arbitrary")),
        name="fourier_stage1",
    )(a1, fn_view)


def _fft2_kernel(p_ref, b_ref, cs_ref, o_ref, *, f2b):
    R = FFT_R
    cc = cs_ref[0:FN_GC, :]
    sc = cs_ref[FN_GC:2 * FN_GC, :]
    norm = 1.0 / math.sqrt(SEQ * FN_GC)
    for g in range(FN_GROUPS):
        for jj in range(f2b):
            stacked = jnp.concatenate([p_ref[g, 0, 0, jj], p_ref[g, 0, 1, jj]], axis=0).astype(BF16)
            x = _dot(b_ref[jj], stacked)
            y = _dot(x[0:R].astype(BF16), cc) + _dot(x[R:2 * R].astype(BF16), sc)
            lo = jj * D_MODEL + g * FN_GC
            o_ref[:, lo:lo + FN_GC] = (y * norm).astype(o_ref.dtype)


def _fft2(p6, b2, csm, n_rows, n_batch):
    R = FFT_R
    f2b = 8
    kern = functools.partial(_fft2_kernel, f2b=f2b)
    return pl.pallas_call(
        kern,
        grid=(n_batch, R // f2b),
        in_specs=[
            pl.BlockSpec((FN_GROUPS, 1, 2, f2b, R, FN_GC), lambda b, f: (0, b, 0, f, 0, 0)),
            pl.BlockSpec((f2b, 2 * R, 2 * R), lambda b, f: (f, 0, 0)),
            pl.BlockSpec((2 * FN_GC, FN_GC), lambda b, f: (0, 0)),
        ],
        out_specs=pl.BlockSpec((R, f2b * D_MODEL), lambda b, f: (b, f)),
        out_shape=jax.ShapeDtypeStruct((n_rows // R, R * D_MODEL), BF16),
        compiler_params=_cparams(("arbitrary", "arbitrary")),
        name="fourier_stage2",
    )(p6, b2, csm)


def _fft_ctx_kernel(z_ref, a_ref, cs_ref, yin_ref, o_ref):
    del yin_ref
    cc = cs_ref[0:FN_GC, :]
    sc = cs_ref[FN_GC:2 * FN_GC, :]
    norm = 1.0 / math.sqrt(CTX_LEN * FN_GC)
    for g in range(FN_GROUPS):
        p = _dot(a_ref[...], z_ref[g])
        y = _dot(p[0:CTX_LEN].astype(BF16), cc) + _dot(p[CTX_LEN:2 * CTX_LEN].astype(BF16), sc)
        o_ref[:, g * FN_GC:(g + 1) * FN_GC] = (y * norm).astype(o_ref.dtype)


def _fft_ctx(fn, actx, csm, yf, n_batch):
    n_rows = yf.shape[0]
    ctx_base = n_batch * (SEQ // CTX_LEN)
    return pl.pallas_call(
        _fft_ctx_kernel,
        grid=(n_batch,),
        in_specs=[
            pl.BlockSpec((FN_GROUPS, CTX_LEN, FN_GC), lambda b: (0, ctx_base + b, 0)),
            pl.BlockSpec((2 * CTX_LEN, CTX_LEN), lambda b: (0, 0)),
            pl.BlockSpec((2 * FN_GC, FN_GC), lambda b: (0, 0)),
            pl.BlockSpec(memory_space=pl.ANY),
        ],
        out_specs=pl.BlockSpec((CTX_LEN, D_MODEL), lambda b: (ctx_base + b, 0)),
        out_shape=jax.ShapeDtypeStruct((n_rows, D_MODEL), BF16),
        input_output_aliases={3: 0},
        compiler_params=_cparams(("arbitrary",)),
        name="fourier_ctx",
    )(fn, actx, csm, yf)


def _fourier(fn, tables, need_ctx, n_batch):
    a1, b2, csm, actx = tables
    n_rows = fn.shape[1]
    R = FFT_R
    p = _fft1(fn.reshape(FN_GROUPS, n_rows // R, R * FN_GC), a1, n_batch)
    yf = _fft2(p.reshape(FN_GROUPS, n_batch, 2, R, R, FN_GC), b2, csm, n_rows, n_batch)
    yf = yf.reshape(n_rows, D_MODEL)
    if need_ctx:
        yf = _fft_ctx(fn, actx, csm, yf, n_batch)
    return yf


def _merge_kernel(hf_ref, hb_ref, o_ref, gm_ref, gd_ref, gf_ref, yd_ref, yf_ref, x_ref, mod_ref, hg_ref,
                  wml_ref, wda_ref, wfn_ref, wout_ref, out_ref, *, rows_per_batch, n_batch):
    i = pl.program_id(0)
    hsum = hf_ref[...].astype(F32) + hb_ref[...].astype(F32)
    hg = hg_ref[...]
    parts = []
    for h in range(ML_HEADS):
        hs = slice(h * ML_DK, (h + 1) * ML_DK)
        parts.append(_rms(hsum[:, hs], hg[:, hs]))
    ym = (jnp.concatenate(parts, axis=1) * jax.nn.sigmoid(o_ref[...].astype(F32))).astype(BF16)
    y = (jax.nn.sigmoid(gm_ref[...].astype(F32)) * _dot(ym, wml_ref[...])
         + jax.nn.sigmoid(gd_ref[...].astype(F32)) * _dot(yd_ref[...], wda_ref[...])
         + jax.nn.sigmoid(gf_ref[...].astype(F32)) * _dot(yf_ref[...], wfn_ref[...]))
    gate = _mod_row(mod_ref, i, rows_per_batch, n_batch, 2)
    out_ref[...] = x_ref[...] + gate * _dot(y.astype(BF16), wout_ref[...])


def _merge(hf, hb, u, yd, yf, x, mods, head_g, wml, wda, wfn, wout, need_ctx, n_batch):
    n_rows = x.shape[0]
    tm = _row_tile(n_batch)
    rows_per_batch = SEQ // tm
    ni = (n_rows if need_ctx else n_batch * SEQ) // tm
    kern = functools.partial(_merge_kernel, rows_per_batch=rows_per_batch, n_batch=n_batch)
    row = lambda i: (i, 0)
    full = lambda i: (0, 0)
    wspec = pl.BlockSpec((D_MODEL, D_MODEL), full)
    return pl.pallas_call(
        kern,
        grid=(ni,),
        in_specs=[
            pl.BlockSpec((tm, D_MODEL), row),
            pl.BlockSpec((tm, D_MODEL), row),
            pl.BlockSpec((tm, D_MODEL), lambda i: (i, U_OML)),
            pl.BlockSpec((tm, D_MODEL), lambda i: (i, U_GPRE)),
            pl.BlockSpec((tm, D_MODEL), lambda i: (i, U_GPRE + 1)),
            pl.BlockSpec((tm, D_MODEL), lambda i: (i, U_GPRE + 2)),
            pl.BlockSpec((tm, D_MODEL), row),
            pl.BlockSpec((tm, D_MODEL), row),
            pl.BlockSpec((tm, D_MODEL), row),
            pl.BlockSpec((8, N_MOD), full),
            pl.BlockSpec((1, D_MODEL), full),
            wspec, wspec, wspec, wspec,
        ],
        out_specs=pl.BlockSpec((tm, D_MODEL), row),
        out_shape=jax.ShapeDtypeStruct((n_rows, D_MODEL), F32),
        compiler_params=_cparams(("arbitrary",)),
        name="merge_out_proj",
    )(hf, hb, u, u, u, u, yd, yf, x, mods, head_g, wml, wda, wfn, wout)


FFN_CHUNKS = ((0, 1024), (1024, 1024), (2048, 768))


def _ffn_kernel(x_ref, mod_ref, g_ref, win_ref, wout_ref, fg_ref, out_ref, *, rows_per_batch, n_batch, final):
    i = pl.program_id(0)
    x = x_ref[...]
    shift = _mod_row(mod_ref, i, rows_per_batch, n_batch, 3)
    scale = _mod_row(mod_ref, i, rows_per_batch, n_batch, 4)
    gate = _mod_row(mod_ref, i, rows_per_batch, n_batch, 5)
    h = (_rms(x, g_ref[...]) * (1.0 + scale) + shift).astype(BF16)
    acc = None
    for lo, width in FFN_CHUNKS:
        a = _dot(h, win_ref[:, lo:lo + width])
        b = _dot(h, win_ref[:, D_FF + lo:D_FF + lo + width])
        act = (a * jax.nn.sigmoid(a) * b).astype(BF16)
        part = _dot(act, wout_ref[lo:lo + width, :])
        acc = part if acc is None else acc + part
    xn = x + gate * acc
    out_ref[...] = _rms(xn, fg_ref[...]) if final else xn


def _ffn(x, mods, g, w_in, w_out, final_g, final, n_rows_out, n_batch):
    tm = _row_tile(n_batch)
    rows_per_batch = SEQ // tm
    kern = functools.partial(_ffn_kernel, rows_per_batch=rows_per_batch, n_batch=n_batch, final=final)
    row = lambda i: (i, 0)
    full = lambda i: (0, 0)
    return pl.pallas_call(
        kern,
        grid=(n_rows_out // tm,),
        in_specs=[
            pl.BlockSpec((tm, D_MODEL), row),
            pl.BlockSpec((8, N_MOD), full),
            pl.BlockSpec((1, D_MODEL), full),
            pl.BlockSpec((D_MODEL, 2 * D_FF), full),
            pl.BlockSpec((D_FF, D_MODEL), full),
            pl.BlockSpec((1, D_MODEL), full),
        ],
        out_specs=pl.BlockSpec((tm, D_MODEL), row),
        out_shape=jax.ShapeDtypeStruct((n_rows_out, D_MODEL), F32),
        compiler_params=_cparams(("arbitrary",)),
        name="swiglu_ffn",
    )(x, mods, g, w_in, w_out, final_g)


def _da_col_perm(w):
    half = DA_DH // 2
    return w.reshape(-1, DA_HEADS, 2, 2, half).transpose(0, 1, 3, 2, 4).reshape(-1, DA_HEADS * DA_DV)


def _rope_tables(pad):
    n_freq = DA_DH // 4
    rows = SEQ // GRID_W
    inv = ROPE_BASE ** (-jnp.arange(n_freq, dtype=F32) / n_freq)
    r = jnp.repeat(jnp.arange(rows, dtype=F32), GRID_W)
    col = jnp.tile(jnp.arange(GRID_W, dtype=F32), rows)
    ang = jnp.concatenate([r[:, None] * inv, col[:, None] * inv], axis=-1)
    cos, sin = jnp.cos(ang), jnp.sin(ang)
    cos_t = jnp.concatenate([cos, cos, cos, cos], axis=-1)
    sin_t = jnp.concatenate([-sin, -sin, sin, sin], axis=-1)
    cos_t = jnp.concatenate([cos_t, jnp.ones((pad, 128), F32)], axis=0)
    sin_t = jnp.concatenate([sin_t, jnp.zeros((pad, 128), F32)], axis=0)
    return cos_t, sin_t


def kernel(x, c, ctx, c_ctx, w_ada, b_ada, norm_g, w_in, ml_gate_b, ml_head_g, da_lam, da_head_g,
           w_br_ml, w_br_da, w_br_fn, w_out, w_ffn_in, w_ffn_out, final_g):
    n_batch = x.shape[0]
    n_lat = n_batch * SEQ
    xs = jnp.concatenate([x.reshape(n_lat, D_MODEL), ctx.reshape(n_batch * CTX_LEN, D_MODEL)], axis=0)
    cc = jnp.concatenate([c, c_ctx[None, :], jnp.zeros((8 - n_batch - 1, D_MODEL), F32)], axis=0)
    mods = _mods(cc, w_ada, b_ada)
    cos_t, sin_t = _rope_tables(n_batch * CTX_LEN)
    tables = _dft_tables()
    gate_lo = 4 * D_MODEL
    fn_lo = gate_lo + N_GATE + 3 * D_MODEL
    final_g2 = final_g.reshape(1, D_MODEL)

    for l in range(DEPTH):
        need_ctx = l < DEPTH - 1
        lam_init = 0.8 - 0.6 * math.exp(-0.3 * l)
        wl = w_in[l]
        da_lo = gate_lo + N_GATE
        w_main = jnp.concatenate([wl[:, :D_MODEL], wl[:, 2 * D_MODEL:gate_lo],
                                  _da_col_perm(wl[:, da_lo:da_lo + D_MODEL]),
                                  _da_col_perm(wl[:, da_lo + D_MODEL:da_lo + 2 * D_MODEL]),
                                  wl[:, da_lo + 2 * D_MODEL:fn_lo],
                                  wl[:, fn_lo + D_MODEL:], wl[:, fn_lo:fn_lo + D_MODEL]], axis=1).astype(BF16)
        w_kt = wl[:, D_MODEL:2 * D_MODEL].T.astype(BF16)
        w_gate_t = wl[:, gate_lo:gate_lo + N_GATE].T.astype(BF16)
        u, kt, fn, gates_t = _inproj(xs, mods[l], norm_g[l, 0].reshape(1, D_MODEL), w_main, w_kt, w_gate_t,
                                     cos_t, sin_t, n_batch)
        hf, hb = _mlstm(u, kt, gates_t, ml_gate_b[l], n_batch)
        yd = _attn(u, da_lam[l], da_head_g[l].reshape(1, D_MODEL), lam_init, need_ctx, n_batch)
        yf = _fourier(fn, tables, need_ctx, n_batch)
        xs = _merge(hf, hb, u, yd, yf, xs, mods[l], ml_head_g[l].reshape(1, D_MODEL),
                    w_br_ml[l].astype(BF16), w_br_da[l].astype(BF16), w_br_fn[l].astype(BF16),
                    w_out[l].astype(BF16), need_ctx, n_batch)
        final = l == DEPTH - 1
        n_out = n_lat if final else xs.shape[0]
        xs = _ffn(xs, mods[l], norm_g[l, 1].reshape(1, D_MODEL), w_ffn_in[l].astype(BF16),
                  w_ffn_out[l].astype(BF16), final_g2, final, n_out, n_batch)
    return xs.reshape(n_batch, SEQ, D_MODEL)
```

```python
import functools
import math

import jax
import jax.numpy as jnp
from jax import lax
from jax.experimental import pallas as pl
from jax.experimental.pallas import tpu as pltpu

D_MODEL = 1024
SEQ = 4096
DEPTH = 4
CTX_LEN = 256
GRID_W = 64
NORM_EPS = 1e-6

ML_HEADS = 4
ML_DK = 256
ML_CHUNK = 128

DA_HEADS = 8
DA_DH = 64
DA_DV = 2 * DA_DH
ROPE_BASE = 10000.0
ATTN_COL_CHUNKS = 2
ATTN_VROWS = 2 * DA_DH + 16

FN_GROUPS = 4
FN_GC = 256
FFT_R = 64

D_FF = 2816
N_GATE = 4 * ML_HEADS
N_MOD = 6 * D_MODEL

U_QML, U_VML, U_OML, U_QDA, U_KDA, U_GPRE = 0, 1, 2, 3, 4, 5
U_BLOCKS = 8
W_BLOCKS = U_BLOCKS + 1
STEP_KT, STEP_VT = 1, 6
N_COL_STEPS = W_BLOCKS + 2

VMEM_LIMIT_V7X = 56 * 1024 * 1024

BF16 = jnp.bfloat16
F32 = jnp.float32


def _cparams(sem):
    return pltpu.CompilerParams(dimension_semantics=sem, vmem_limit_bytes=VMEM_LIMIT_V7X)


def _dot(a, b):
    return jnp.dot(a, b, preferred_element_type=F32)


def _dot_nt(a, b):
    return lax.dot_general(a, b, (((1,), (1,)), ((), ())), preferred_element_type=F32)


def _dot_tn(a, b):
    return lax.dot_general(a, b, (((0,), (0,)), ((), ())), preferred_element_type=F32)


def _mod_row(mod_ref, i, rows_per_batch, n_batch, col):
    r = jnp.minimum(i // rows_per_batch, n_batch)
    return mod_ref[pl.ds(r, 1), col * D_MODEL:(col + 1) * D_MODEL]


def _row_tile(n_batch):
    return min(512, n_batch * CTX_LEN)


def _rms(x, g):
    return x * lax.rsqrt(jnp.mean(x * x, axis=-1, keepdims=True) + NORM_EPS) * g


def _mods_kernel(c_ref, w_ref, b_ref, o_ref):
    c = c_ref[...]
    s = (c * jax.nn.sigmoid(c)).astype(BF16)
    o_ref[0] = _dot(s, w_ref[0].astype(BF16)) + b_ref[0]


def _mods(cc, w_ada, b_ada):
    tn = 1536
    return pl.pallas_call(
        _mods_kernel,
        grid=(DEPTH, N_MOD // tn),
        in_specs=[
            pl.BlockSpec((8, D_MODEL), lambda l, j: (0, 0)),
            pl.BlockSpec((1, D_MODEL, tn), lambda l, j: (l, 0, j)),
            pl.BlockSpec((1, 1, tn), lambda l, j: (l, 0, j)),
        ],
        out_specs=pl.BlockSpec((1, 8, tn), lambda l, j: (l, 0, j)),
        out_shape=jax.ShapeDtypeStruct((DEPTH, 8, N_MOD), F32),
        compiler_params=_cparams(("arbitrary", "arbitrary")),
        name="adaln_mods",
    )(cc, w_ada, b_ada.reshape(DEPTH, 1, N_MOD))


def _inproj_w_blk(j):
    return j - (j >= STEP_KT).astype(jnp.int32) - (j >= STEP_VT).astype(jnp.int32)


def _inproj_kernel(x_ref, mod_ref, g_ref, w_ref, wt_ref, wgt_ref, cos_ref, sin_ref,
                   u_ref, kt_ref, vt_ref, fn_ref, gate_ref, xn_ref, *, rows_per_batch, n_batch):
    i = pl.program_id(0)
    j = pl.program_id(1)

    @pl.when(j == 0)
    def _():
        y = _rms(x_ref[...], g_ref[...])
        shift = _mod_row(mod_ref, i, rows_per_batch, n_batch, 0)
        scale = _mod_row(mod_ref, i, rows_per_batch, n_batch, 1)
        xn_ref[...] = (y * (1.0 + scale) + shift).astype(BF16)
        gate_ref[...] = _dot_nt(wgt_ref[...], xn_ref[...])

    def product():
        return _dot(xn_ref[...], w_ref[...])

    blk = _inproj_w_blk(j)
    is_t = jnp.logical_or(j == STEP_KT, j == STEP_VT)
    is_rope = jnp.logical_and(jnp.logical_not(is_t), jnp.logical_or(blk == U_QDA, blk == U_KDA))
    is_plain = jnp.logical_and(jnp.logical_not(jnp.logical_or(is_t, is_rope)), blk < U_BLOCKS)

    @pl.when(is_plain)
    def _():
        u_ref[...] = product().astype(BF16)

    @pl.when(j == STEP_KT)
    def _():
        kt_ref[...] = (_dot_nt(wt_ref[0], xn_ref[...]) * (ML_DK ** -0.5)).astype(BF16)

    @pl.when(j == STEP_VT)
    def _():
        vt_ref[...] = _dot_nt(wt_ref[0], xn_ref[...]).astype(BF16)

    @pl.when(is_rope)
    def _():
        acc = product()
        cos = cos_ref[...]
        sin = sin_ref[...]
        for t in range(acc.shape[1] // DA_DV):
            sl = slice(t * DA_DV, (t + 1) * DA_DV)
            x = acc[:, sl]
            u_ref[:, sl] = (x * cos + pltpu.roll(x, DA_DV // 2, 1) * sin).astype(BF16)

    @pl.when(jnp.logical_and(jnp.logical_not(is_t), blk == U_BLOCKS))
    def _():
        acc = product()
        for g in range(FN_GROUPS):
            fn_ref[g] = acc[:, g * FN_GC:(g + 1) * FN_GC].astype(BF16)


def _inproj(x, mods, g, w_main, w_t, w_gate_t, cos_t, sin_t, n_batch):
    n_rows = x.shape[0]
    tm = n_batch * CTX_LEN
    assert SEQ % tm == 0 and cos_t.shape[0] == SEQ + tm
    ni = n_rows // tm
    rows_per_batch = SEQ // tm
    lat_blocks = n_batch * rows_per_batch

    def tab_idx(i, j):
        return (jnp.where(i < lat_blocks, i % rows_per_batch, rows_per_batch), 0)

    kern = functools.partial(_inproj_kernel, rows_per_batch=rows_per_batch, n_batch=n_batch)
    return pl.pallas_call(
        kern,
        grid=(ni, N_COL_STEPS),
        in_specs=[
            pl.BlockSpec((tm, D_MODEL), lambda i, j: (i, 0)),
            pl.BlockSpec((8, N_MOD), lambda i, j: (0, 0)),
            pl.BlockSpec((1, D_MODEL), lambda i, j: (0, 0)),
            pl.BlockSpec((D_MODEL, D_MODEL), lambda i, j: (0, _inproj_w_blk(j))),
            pl.BlockSpec((1, D_MODEL, D_MODEL), lambda i, j: ((j >= STEP_VT).astype(jnp.int32), 0, 0)),
            pl.BlockSpec((N_GATE, D_MODEL), lambda i, j: (0, 0)),
            pl.BlockSpec((tm, 128), tab_idx),
            pl.BlockSpec((tm, 128), tab_idx),
        ],
        out_specs=[
            pl.BlockSpec((tm, D_MODEL), lambda i, j: (i, jnp.minimum(_inproj_w_blk(j), U_BLOCKS - 1))),
            pl.BlockSpec((D_MODEL, tm), lambda i, j: (0, i)),
            pl.BlockSpec((D_MODEL, tm), lambda i, j: (0, i)),
            pl.BlockSpec((FN_GROUPS, tm, FN_GC), lambda i, j: (0, i, 0)),
            pl.BlockSpec((N_GATE, tm), lambda i, j: (0, i)),
        ],
        out_shape=[
            jax.ShapeDtypeStruct((n_rows, U_BLOCKS * D_MODEL), BF16),
            jax.ShapeDtypeStruct((D_MODEL, n_rows), BF16),
            jax.ShapeDtypeStruct((D_MODEL, n_rows), BF16),
            jax.ShapeDtypeStruct((FN_GROUPS, n_rows, FN_GC), BF16),
            jax.ShapeDtypeStruct((N_GATE, n_rows), F32),
        ],
        scratch_shapes=[pltpu.VMEM((tm, D_MODEL), BF16)],
        compiler_params=_cparams(("arbitrary", "arbitrary")),
        name="in_proj",
    )(x, mods, g, w_main, w_t, w_gate_t, cos_t, sin_t)


def _split3(x):
    hi = x.astype(BF16).astype(F32)
    mid = (x - hi).astype(BF16).astype(F32)
    lo = (x - hi - mid).astype(BF16).astype(F32)
    return hi, mid, lo


def _mlstm_kernel(qf_ref, ktf_ref, vf_ref, gtf_ref, qb_ref, ktb_ref, vb_ref, gtb_ref, bias_ref,
                  hf_ref, hb_ref, *state_refs):
    cx_refs, m_refs = state_refs[:2 * ML_HEADS], state_refs[2 * ML_HEADS:]
    s = pl.program_id(1)
    L = ML_CHUNK
    H = ML_HEADS
    W = 128

    @pl.when(s == 0)
    def _():
        for ref in state_refs:
            ref[...] = jnp.zeros_like(ref)

    t_idx = lax.broadcasted_iota(jnp.int32, (L, L), 0)
    s_idx = lax.broadcasted_iota(jnp.int32, (L, L), 1)
    eye = t_idx == s_idx
    sub8 = lax.broadcasted_iota(jnp.int32, (8, W), 0)
    ones_w = jnp.ones((L, W), BF16)
    er = lax.broadcasted_iota(jnp.int32, (4 * L, 2 * W), 0)
    ec = lax.broadcasted_iota(jnp.int32, (4 * L, 2 * W), 1)
    expand = jnp.where((er < 3 * L) == (ec < W), 1.0, 0.0).astype(BF16)

    def running_max_rows(x, d):
        n_tiles = L // 8
        out = [None] * n_tiles
        carry = None
        for j in (range(n_tiles) if d == 0 else range(n_tiles - 1, -1, -1)):
            r = x[8 * j:8 * (j + 1)]
            k = 1
            while k < 8:
                if d == 0:
                    r = jnp.maximum(r, jnp.where(sub8 >= k, pltpu.roll(r, k, 0), -jnp.inf))
                else:
                    r = jnp.maximum(r, jnp.where(sub8 < 8 - k, pltpu.roll(r, 8 - k, 0), -jnp.inf))
                k *= 2
            if carry is not None:
                r = jnp.maximum(r, carry)
            carry = jnp.broadcast_to(r[7:8] if d == 0 else r[0:1], (8, W))
            out[j] = r
        return jnp.concatenate(out, axis=0)

    def prelude(d, gt_ref):
        before = (t_idx <= s_idx) if d == 0 else (t_idx >= s_idx)
        gt = gt_ref[2 * H * d:2 * H * (d + 1), :] + bias_ref[2 * H * d:2 * H * (d + 1), :]
        i4, f4 = gt[0:H], gt[H:2 * H]
        lf4 = jnp.minimum(f4, 0.0) - jnp.log1p(jnp.exp(-jnp.abs(f4)))
        lf_terms = jnp.concatenate(list(_split3(lf4)) + [jnp.zeros((H, L), F32)], axis=0).astype(BF16)
        cum_rhs = jnp.concatenate([jnp.where(before, 1.0, 0.0).astype(BF16), ones_w], axis=1)
        r = _dot(lf_terms, cum_rhs)
        bx = r[0:H] + r[H:2 * H] + r[2 * H:3 * H]
        b4, bend4 = bx[:, 0:L], bx[:, L:L + W]
        c4 = i4 - b4
        m_prev4 = m_refs[d][...]
        g4 = bend4 - b4 + i4
        m_new4 = jnp.maximum(bend4 + m_prev4, jnp.max(g4, axis=1, keepdims=True))
        m_refs[d][...] = m_new4
        return dict(c4=c4, col_terms=_split3(b4) + (c4.astype(BF16).astype(F32),), m_prev4=m_prev4,
                    decay4=jnp.exp(bend4 + m_prev4 - m_new4), w4=jnp.exp(g4 - m_new4))

    dirs = ((0, qf_ref, ktf_ref, vf_ref, gtf_ref, hf_ref), (1, qb_ref, ktb_ref, vb_ref, gtb_ref, hb_ref))
    pre = [prelude(d, gt_ref) for d, _, _, _, gt_ref, _ in dirs]
    for d, q_ref, kt_ref, v_ref, _, h_ref in dirs:
        causal = (s_idx <= t_idx) if d == 0 else (s_idx >= t_idx)
        c4, m_prev4, decay4, w4 = pre[d]["c4"], pre[d]["m_prev4"], pre[d]["decay4"], pre[d]["w4"]
        for h in range(H):
            st = d * H + h
            hs = slice(h * ML_DK, (h + 1) * ML_DK)
            diag = jnp.concatenate([jnp.where(eye, x[h:h + 1, :], 0.0) for x in pre[d]["col_terms"]],
                                   axis=1).astype(BF16)
            wide = _dot(diag, expand)
            b_w = wide[:, 0:W]
            m_w = jnp.maximum(running_max_rows(wide[:, W:2 * W], d), m_prev4[h:h + 1, :])

            q = q_ref[:, hs]
            kt = kt_ref[hs, :]
            vx = jnp.concatenate([v_ref[:, hs], ones_w], axis=1)
            a = (jnp.where(causal, jnp.exp(c4[h:h + 1, :] - m_w), 0.0) * _dot(q, kt)).astype(BF16)
            cx_prev = cx_refs[st][...]
            qc = _dot(q, cx_prev.astype(BF16))
            av = _dot(a, vx)
            sc_w = jnp.exp(m_prev4[h:h + 1, :] - m_w)
            den = sc_w * qc[:, ML_DK:] + av[:, ML_DK:]
            inv = 1.0 / jnp.maximum(jnp.abs(den), jnp.exp(-(b_w + m_w)))
            for t in range(ML_DK // W):
                ts = slice(t * W, (t + 1) * W)
                h_ref[:, h * ML_DK + t * W:h * ML_DK + (t + 1) * W] = (
                    (sc_w * qc[:, ts] + av[:, ts]) * inv).astype(h_ref.dtype)

            kw = (kt.astype(F32) * w4[h:h + 1, :]).astype(BF16)
            dec = jnp.concatenate([decay4[h:h + 1, :]] * (ML_DK // W + 1), axis=1)
            cx_refs[st][...] = dec * cx_prev + _dot(kw, vx)


def _mlstm(u, kt, gates_t, gate_b, n_batch):
    n_rows = u.shape[0]
    L = ML_CHUNK
    lat_chunks = SEQ // L
    ctx_chunks = CTX_LEN // L
    n_steps = ctx_chunks + lat_chunks
    ctx_base = n_batch * lat_chunks

    def rowblk(d):
        def f(b, s):
            in_ctx = s < ctx_chunks
            if d == 0:
                c = jnp.where(in_ctx, s, s - ctx_chunks)
            else:
                c = jnp.where(in_ctx, ctx_chunks - 1 - s, lat_chunks - 1 - (s - ctx_chunks))
            return jnp.where(in_ctx, ctx_base + ctx_chunks * b, lat_chunks * b) + c
        return f

    def dir_specs(d):
        rb = rowblk(d)
        return [
            pl.BlockSpec((L, D_MODEL), lambda b, s: (rb(b, s), U_QML)),
            pl.BlockSpec((D_MODEL, L), lambda b, s: (0, rb(b, s))),
            pl.BlockSpec((L, D_MODEL), lambda b, s: (rb(b, s), U_VML)),
            pl.BlockSpec((N_GATE, L), lambda b, s: (0, rb(b, s))),
        ]

    def out_spec(d):
        rb = rowblk(d)
        return pl.BlockSpec((L, D_MODEL), lambda b, s: (rb(b, s), 0))

    return pl.pallas_call(
        _mlstm_kernel,
        grid=(n_batch, n_steps),
        in_specs=dir_specs(0) + dir_specs(1) + [pl.BlockSpec((N_GATE, 1), lambda b, s: (0, 0))],
        out_specs=[out_spec(0), out_spec(1)],
        out_shape=[jax.ShapeDtypeStruct((n_rows, D_MODEL), BF16)] * 2,
        scratch_shapes=(
            [pltpu.VMEM((ML_DK, ML_DK + 128), F32)] * (2 * ML_HEADS)
            + [pltpu.VMEM((ML_HEADS, 128), F32)] * 2),
        compiler_params=_cparams(("arbitrary", "arbitrary")),
        name="mlstm_scan",
    )(u, kt, u, gates_t, u, kt, u, gates_t, gate_b.reshape(N_GATE, 1))


def _attn_lambda(lam_ref, lam_init):
    lq = lam_ref[...]
    return (jnp.exp(jnp.sum(lq[0:1] * lq[1:2], axis=1, keepdims=True))
            - jnp.exp(jnp.sum(lq[2:3] * lq[3:4], axis=1, keepdims=True)) + lam_init)


def _attn_queries(q_ref):
    q = q_ref[...]
    lane = lax.broadcasted_iota(jnp.int32, (1, DA_DV), 1)
    zero = jnp.zeros_like(q)
    is_map0 = (lane % DA_DH) < (DA_DH // 2)
    q2 = jnp.concatenate([jnp.where(is_map0, q, zero), jnp.where(is_map0, zero, q)], axis=0)
    return q2 * (DA_DH ** -0.5)


def _attn_ones_rows(n_keys):
    row = lax.broadcasted_iota(jnp.int32, (ATTN_VROWS - DA_DV, n_keys), 0)
    return jnp.where(row == 0, 1.0, 0.0).astype(BF16)


def _attn_out(acc, lam, g_col, lam_init, tq):
    o0 = acc[0:DA_DV, 0:tq] * (1.0 / acc[DA_DV:DA_DV + 1, 0:tq])
    o1 = acc[0:DA_DV, tq:] * (1.0 / acc[DA_DV:DA_DV + 1, tq:])
    o = o0 - lam * o1
    o = o * lax.rsqrt(jnp.mean(o * o, axis=0, keepdims=True) + NORM_EPS) * g_col * (1.0 - lam_init)
    return o.T


def _attn_kernel(q_ref, kl_ref, kc_ref, vtl_ref, vtc_ref, lam_ref, g_ref, o_ref, kx_ref, vxt_ref, s_ref, m_ref,
                 *, lam_init):
    qi = pl.program_id(2)
    tq = q_ref.shape[0]
    cols = 2 * tq // ATTN_COL_CHUNKS

    @pl.when(qi == 0)
    def _():
        kx_ref[0:SEQ, :] = kl_ref[...]
        kx_ref[SEQ:, :] = kc_ref[...]
        vxt_ref[0:DA_DV, 0:SEQ] = vtl_ref[...]
        vxt_ref[0:DA_DV, SEQ:] = vtc_ref[...]
        vxt_ref[DA_DV:, :] = _attn_ones_rows(SEQ + CTX_LEN)

    def score(slot):
        s = _dot_nt(kx_ref[...], _attn_queries(q_ref))
        m_ref[slot] = jnp.max(s, axis=0, keepdims=True)
        s_ref[slot] = s

    def drain(slot):
        accs = []
        for c in range(ATTN_COL_CHUNKS):
            cs = slice(c * cols, (c + 1) * cols)
            p = jnp.exp(s_ref[slot, :, cs] - m_ref[slot, :, cs]).astype(BF16)
            accs.append(_dot(vxt_ref[...], p))
        acc = jnp.concatenate(accs, axis=1)
        o = _attn_out(acc, _attn_lambda(lam_ref, lam_init), g_ref[...], lam_init, tq)
        o_ref[...] = o.astype(o_ref.dtype)

    @pl.when(qi == 0)
    def _():
        score(0)

    for parity in (0, 1):
        @pl.when(jnp.logical_and(qi > 0, qi % 2 == parity))
        def _():
            drain(1 - parity)
            score(parity)


def _attn_ctx_kernel(q_ref, kc_ref, vtc_ref, lam_ref, g_ref, yin_ref, o_ref, *, lam_init):
    del yin_ref
    tq = q_ref.shape[0]
    s = _dot_nt(kc_ref[...], _attn_queries(q_ref))
    p = jnp.exp(s - jnp.max(s, axis=0, keepdims=True)).astype(BF16)
    acc = _dot(jnp.concatenate([vtc_ref[...], _attn_ones_rows(CTX_LEN)], axis=0), p)
    o = _attn_out(acc, _attn_lambda(lam_ref, lam_init), g_ref[...], lam_init, tq)
    o_ref[...] = o.astype(o_ref.dtype)


def _attn(u, vt, da_lam, head_g, lam_init, need_ctx, n_batch):
    n_rows = u.shape[0]
    tq = CTX_LEN
    lat_qblocks = SEQ // tq
    ctx_base = n_batch * lat_qblocks
    cpb = D_MODEL // DA_DV
    n_keys = SEQ + CTX_LEN

    kern = functools.partial(_attn_kernel, lam_init=lam_init)
    yd = pl.pallas_call(
        kern,
        grid=(n_batch, DA_HEADS, lat_qblocks + 1),
        in_specs=[
            pl.BlockSpec((tq, DA_DV),
                         lambda b, h, qi: (b * lat_qblocks + jnp.minimum(qi, lat_qblocks - 1), U_QDA * cpb + h)),
            pl.BlockSpec((SEQ, DA_DV), lambda b, h, qi: (b, U_KDA * cpb + h)),
            pl.BlockSpec((CTX_LEN, DA_DV), lambda b, h, qi: (ctx_base + b, U_KDA * cpb + h)),
            pl.BlockSpec((DA_DV, SEQ), lambda b, h, qi: (h, b)),
            pl.BlockSpec((DA_DV, CTX_LEN), lambda b, h, qi: (h, ctx_base + b)),
            pl.BlockSpec((4, DA_DH), lambda b, h, qi: (0, 0)),
            pl.BlockSpec((DA_DV, 1), lambda b, h, qi: (h, 0)),
        ],
        out_specs=pl.BlockSpec((tq, DA_DV), lambda b, h, qi: (b * lat_qblocks + jnp.maximum(qi - 1, 0), h)),
        out_shape=jax.ShapeDtypeStruct((n_rows, D_MODEL), BF16),
        scratch_shapes=[
            pltpu.VMEM((n_keys, DA_DV), BF16),
            pltpu.VMEM((ATTN_VROWS, n_keys), BF16),
            pltpu.VMEM((2, n_keys, 2 * tq), F32),
            pltpu.VMEM((2, 1, 2 * tq), F32),
        ],
        compiler_params=_cparams(("arbitrary", "arbitrary", "arbitrary")),
        name="diff_attn",
    )(u, u, u, vt, vt, da_lam, head_g)
    if not need_ctx:
        return yd
    kern_ctx = functools.partial(_attn_ctx_kernel, lam_init=lam_init)
    return pl.pallas_call(
        kern_ctx,
        grid=(n_batch, DA_HEADS),
        in_specs=[
            pl.BlockSpec((CTX_LEN, DA_DV), lambda b, h: (ctx_base + b, U_QDA * cpb + h)),
            pl.BlockSpec((CTX_LEN, DA_DV), lambda b, h: (ctx_base + b, U_KDA * cpb + h)),
            pl.BlockSpec((DA_DV, CTX_LEN), lambda b, h: (h, ctx_base + b)),
            pl.BlockSpec((4, DA_DH), lambda b, h: (0, 0)),
            pl.BlockSpec((DA_DV, 1), lambda b, h: (h, 0)),
            pl.BlockSpec(memory_space=pl.ANY),
        ],
        out_specs=pl.BlockSpec((CTX_LEN, DA_DV), lambda b, h: (ctx_base + b, h)),
        out_shape=jax.ShapeDtypeStruct((n_rows, D_MODEL), BF16),
        input_output_aliases={5: 0},
        compiler_params=_cparams(("arbitrary", "arbitrary")),
        name="diff_attn_ctx",
    )(u, u, vt, da_lam, head_g, yd)


def _dft_tables():
    R = FFT_R

    def cs(num, period):
        ang = (num % period).astype(F32) * (2.0 * math.pi / period)
        return jnp.cos(ang), jnp.sin(ang)

    idx = jnp.arange(R, dtype=jnp.int32)
    c1, s1 = cs(idx[:, None] * idx[None, :], R)
    a1 = jnp.concatenate([c1, -s1], axis=0).astype(BF16)
    f2 = idx[:, None, None]
    f1 = idx[None, :, None]
    t1 = idx[None, None, :]
    mc, ms = cs(t1 * (R * f1 + f2), SEQ)
    b2 = jnp.concatenate([jnp.concatenate([mc, ms], axis=2),
                          jnp.concatenate([-ms, mc], axis=2)], axis=1).astype(BF16)
    ch = jnp.arange(FN_GC, dtype=jnp.int32)
    cc, sc = cs(ch[:, None] * ch[None, :], FN_GC)
    csm = jnp.concatenate([cc, sc], axis=0).astype(BF16)
    actx = jnp.concatenate([cc, -sc], axis=0).astype(BF16)
    return a1, b2, csm, actx


def _fft1_kernel(a_ref, z_ref, p_ref):
    p_ref[0, 0] = _dot(a_ref[...], z_ref[0])


def _fft1(fn_view, a1, n_batch):
    R = FFT_R
    lanes = R * FN_GC
    lc = 4096
    return pl.pallas_call(
        _fft1_kernel,
        grid=(FN_GROUPS, n_batch, lanes // lc),
        in_specs=[
            pl.BlockSpec((2 * R, R), lambda g, b, c: (0, 0)),
            pl.BlockSpec((1, R, lc), lambda g, b, c: (g, b, c)),
        ],
        out_specs=pl.BlockSpec((1, 1, 2 * R, lc), lambda g, b, c: (g, b, 0, c)),
        out_shape=jax.ShapeDtypeStruct((FN_GROUPS, n_batch, 2 * R, lanes), F32),
        compiler_params=_cparams(("arbitrary", "arbitrary", "arbitrary")),
        name="fourier_stage1",
    )(a1, fn_view)


def _fft2_kernel(p_ref, b_ref, cs_ref, o_ref, *, f2b):
    R = FFT_R
    cc = cs_ref[0:FN_GC, :]
    sc = cs_ref[FN_GC:2 * FN_GC, :]
    norm = 1.0 / math.sqrt(SEQ * FN_GC)
    for g in range(FN_GROUPS):
        for jj in range(f2b):
            stacked = jnp.concatenate([p_ref[g, 0, 0, jj], p_ref[g, 0, 1, jj]], axis=0).astype(BF16)
            x = _dot(b_ref[jj], stacked)
            y = _dot(x[0:R].astype(BF16), cc) + _dot(x[R:2 * R].astype(BF16), sc)
            lo = jj * D_MODEL + g * FN_GC
            o_ref[:, lo:lo + FN_GC] = (y * norm).astype(o_ref.dtype)


def _fft2(p6, b2, csm, n_rows, n_batch):
    R = FFT_R
    f2b = 8
    kern = functools.partial(_fft2_kernel, f2b=f2b)
    return pl.pallas_call(
        kern,
        grid=(n_batch, R // f2b),
        in_specs=[
            pl.BlockSpec((FN_GROUPS, 1, 2, f2b, R, FN_GC), lambda b, f: (0, b, 0, f, 0, 0)),
            pl.BlockSpec((f2b, 2 * R, 2 * R), lambda b, f: (f, 0, 0)),
            pl.BlockSpec((2 * FN_GC, FN_GC), lambda b, f: (0, 0)),
        ],
        out_specs=pl.BlockSpec((R, f2b * D_MODEL), lambda b, f: (b, f)),
        out_shape=jax.ShapeDtypeStruct((n_rows // R, R * D_MODEL), BF16),
        compiler_params=_cparams(("arbitrary", "arbitrary")),
        name="fourier_stage2",
    )(p6, b2, csm)


def _fft_ctx_kernel(z_ref, a_ref, cs_ref, yin_ref, o_ref):
    del yin_ref
    cc = cs_ref[0:FN_GC, :]
    sc = cs_ref[FN_GC:2 * FN_GC, :]
    norm = 1.0 / math.sqrt(CTX_LEN * FN_GC)
    for g in range(FN_GROUPS):
        p = _dot(a_ref[...], z_ref[g])
        y = _dot(p[0:CTX_LEN].astype(BF16), cc) + _dot(p[CTX_LEN:2 * CTX_LEN].astype(BF16), sc)
        o_ref[:, g * FN_GC:(g + 1) * FN_GC] = (y * norm).astype(o_ref.dtype)


def _fft_ctx(fn, actx, csm, yf, n_batch):
    n_rows = yf.shape[0]
    ctx_base = n_batch * (SEQ // CTX_LEN)
    return pl.pallas_call(
        _fft_ctx_kernel,
        grid=(n_batch,),
        in_specs=[
            pl.BlockSpec((FN_GROUPS, CTX_LEN, FN_GC), lambda b: (0, ctx_base + b, 0)),
            pl.BlockSpec((2 * CTX_LEN, CTX_LEN), lambda b: (0, 0)),
            pl.BlockSpec((2 * FN_GC, FN_GC), lambda b: (0, 0)),
            pl.BlockSpec(memory_space=pl.ANY),
        ],
        out_specs=pl.BlockSpec((CTX_LEN, D_MODEL), lambda b: (ctx_base + b, 0)),
        out_shape=jax.ShapeDtypeStruct((n_rows, D_MODEL), BF16),
        input_output_aliases={3: 0},
        compiler_params=_cparams(("arbitrary",)),
        name="fourier_ctx",
    )(fn, actx, csm, yf)


def _fourier(fn, tables, need_ctx, n_batch):
    a1, b2, csm, actx = tables
    n_rows = fn.shape[1]
    R = FFT_R
    p = _fft1(fn.reshape(FN_GROUPS, n_rows // R, R * FN_GC), a1, n_batch)
    yf = _fft2(p.reshape(FN_GROUPS, n_batch, 2, R, R, FN_GC), b2, csm, n_rows, n_batch)
    yf = yf.reshape(n_rows, D_MODEL)
    if need_ctx:
        yf = _fft_ctx(fn, actx, csm, yf, n_batch)
    return yf


def _merge_kernel(hf_ref, hb_ref, o_ref, gm_ref, gd_ref, gf_ref, yd_ref, yf_ref, x_ref, mod_ref, hg_ref,
                  wml_ref, wda_ref, wfn_ref, wout_ref, out_ref, *, rows_per_batch, n_batch):
    i = pl.program_id(0)
    hsum = hf_ref[...].astype(F32) + hb_ref[...].astype(F32)
    hg = hg_ref[...]
    parts = []
    for h in range(ML_HEADS):
        hs = slice(h * ML_DK, (h + 1) * ML_DK)
        parts.append(_rms(hsum[:, hs], hg[:, hs]))
    ym = (jnp.concatenate(parts, axis=1) * jax.nn.sigmoid(o_ref[...].astype(F32))).astype(BF16)
    y = (jax.nn.sigmoid(gm_ref[...].astype(F32)) * _dot(ym, wml_ref[...])
         + jax.nn.sigmoid(gd_ref[...].astype(F32)) * _dot(yd_ref[...], wda_ref[...])
         + jax.nn.sigmoid(gf_ref[...].astype(F32)) * _dot(yf_ref[...], wfn_ref[...]))
    gate = _mod_row(mod_ref, i, rows_per_batch, n_batch, 2)
    out_ref[...] = x_ref[...] + gate * _dot(y.astype(BF16), wout_ref[...])


def _merge(hf, hb, u, yd, yf, x, mods, head_g, wml, wda, wfn, wout, need_ctx, n_batch):
    n_rows = x.shape[0]
    tm = _row_tile(n_batch)
    rows_per_batch = SEQ // tm
    ni = (n_rows if need_ctx else n_batch * SEQ) // tm
    kern = functools.partial(_merge_kernel, rows_per_batch=rows_per_batch, n_batch=n_batch)
    row = lambda i: (i, 0)
    full = lambda i: (0, 0)
    wspec = pl.BlockSpec((D_MODEL, D_MODEL), full)
    return pl.pallas_call(
        kern,
        grid=(ni,),
        in_specs=[
            pl.BlockSpec((tm, D_MODEL), row),
            pl.BlockSpec((tm, D_MODEL), row),
            pl.BlockSpec((tm, D_MODEL), lambda i: (i, U_OML)),
            pl.BlockSpec((tm, D_MODEL), lambda i: (i, U_GPRE)),
            pl.BlockSpec((tm, D_MODEL), lambda i: (i, U_GPRE + 1)),
            pl.BlockSpec((tm, D_MODEL), lambda i: (i, U_GPRE + 2)),
            pl.BlockSpec((tm, D_MODEL), row),
            pl.BlockSpec((tm, D_MODEL), row),
            pl.BlockSpec((tm, D_MODEL), row),
            pl.BlockSpec((8, N_MOD), full),
            pl.BlockSpec((1, D_MODEL), full),
            wspec, wspec, wspec, wspec,
        ],
        out_specs=pl.BlockSpec((tm, D_MODEL), row),
        out_shape=jax.ShapeDtypeStruct((n_rows, D_MODEL), F32),
        compiler_params=_cparams(("arbitrary",)),
        name="merge_out_proj",
    )(hf, hb, u, u, u, u, yd, yf, x, mods, head_g, wml, wda, wfn, wout)


FFN_CHUNKS = ((0, 1024), (1024, 1024), (2048, 768))


def _ffn_kernel(x_ref, mod_ref, g_ref, win_ref, wout_ref, fg_ref, out_ref, *, rows_per_batch, n_batch, final):
    i = pl.program_id(0)
    x = x_ref[...]
    shift = _mod_row(mod_ref, i, rows_per_batch, n_batch, 3)
    scale = _mod_row(mod_ref, i, rows_per_batch, n_batch, 4)
    gate = _mod_row(mod_ref, i, rows_per_batch, n_batch, 5)
    h = (_rms(x, g_ref[...]) * (1.0 + scale) + shift).astype(BF16)
    acc = None
    for lo, width in FFN_CHUNKS:
        a = _dot(h, win_ref[:, lo:lo + width])
        b = _dot(h, win_ref[:, D_FF + lo:D_FF + lo + width])
        act = (a * jax.nn.sigmoid(a) * b).astype(BF16)
        part = _dot(act, wout_ref[lo:lo + width, :])
        acc = part if acc is None else acc + part
    xn = x + gate * acc
    out_ref[...] = _rms(xn, fg_ref[...]) if final else xn


def _ffn(x, mods, g, w_in, w_out, final_g, final, n_rows_out, n_batch):
    tm = _row_tile(n_batch)
    rows_per_batch = SEQ // tm
    kern = functools.partial(_ffn_kernel, rows_per_batch=rows_per_batch, n_batch=n_batch, final=final)
    row = lambda i: (i, 0)
    full = lambda i: (0, 0)
    return pl.pallas_call(
        kern,
        grid=(n_rows_out // tm,),
        in_specs=[
            pl.BlockSpec((tm, D_MODEL), row),
            pl.BlockSpec((8, N_MOD), full),
            pl.BlockSpec((1, D_MODEL), full),
            pl.BlockSpec((D_MODEL, 2 * D_FF), full),
            pl.BlockSpec((D_FF, D_MODEL), full),
            pl.BlockSpec((1, D_MODEL), full),
        ],
        out_specs=pl.BlockSpec((tm, D_MODEL), row),
        out_shape=jax.ShapeDtypeStruct((n_rows_out, D_MODEL), F32),
        compiler_params=_cparams(("arbitrary",)),
        name="swiglu_ffn",
    )(x, mods, g, w_in, w_out, final_g)


def _da_col_perm(w):
    half = DA_DH // 2
    return w.reshape(-1, DA_HEADS, 2, 2, half).transpose(0, 1, 3, 2, 4).reshape(-1, DA_HEADS * DA_DV)


def _rope_tables(pad):
    n_freq = DA_DH // 4
    rows = SEQ // GRID_W
    inv = ROPE_BASE ** (-jnp.arange(n_freq, dtype=F32) / n_freq)
    r = jnp.repeat(jnp.arange(rows, dtype=F32), GRID_W)
    col = jnp.tile(jnp.arange(GRID_W, dtype=F32), rows)
    ang = jnp.concatenate([r[:, None] * inv, col[:, None] * inv], axis=-1)
    cos, sin = jnp.cos(ang), jnp.sin(ang)
    cos_t = jnp.concatenate([cos, cos, cos, cos], axis=-1)
    sin_t = jnp.concatenate([-sin, -sin, sin, sin], axis=-1)
    cos_t = jnp.concatenate([cos_t, jnp.ones((pad, 128), F32)], axis=0)
    sin_t = jnp.concatenate([sin_t, jnp.zeros((pad, 128), F32)], axis=0)
    return cos_t, sin_t


def kernel(x, c, ctx, c_ctx, w_ada, b_ada, norm_g, w_in, ml_gate_b, ml_head_g, da_lam, da_head_g,
           w_br_ml, w_br_da, w_br_fn, w_out, w_ffn_in, w_ffn_out, final_g):
    n_batch = x.shape[0]
    n_lat = n_batch * SEQ
    xs = jnp.concatenate([x.reshape(n_lat, D_MODEL), ctx.reshape(n_batch * CTX_LEN, D_MODEL)], axis=0)
    cc = jnp.concatenate([c, c_ctx[None, :], jnp.zeros((8 - n_batch - 1, D_MODEL), F32)], axis=0)
    mods = _mods(cc, w_ada, b_ada)
    cos_t, sin_t = _rope_tables(n_batch * CTX_LEN)
    tables = _dft_tables()
    gate_lo = 4 * D_MODEL
    fn_lo = gate_lo + N_GATE + 3 * D_MODEL
    final_g2 = final_g.reshape(1, D_MODEL)

    for l in range(DEPTH):
        need_ctx = l < DEPTH - 1
        lam_init = 0.8 - 0.6 * math.exp(-0.3 * l)
        wl = w_in[l]
        da_lo = gate_lo + N_GATE
        w_main = jnp.concatenate([wl[:, :D_MODEL], wl[:, 2 * D_MODEL:gate_lo],
                                  _da_col_perm(wl[:, da_lo:da_lo + D_MODEL]),
                                  _da_col_perm(wl[:, da_lo + D_MODEL:da_lo + 2 * D_MODEL]),
                                  wl[:, fn_lo + D_MODEL:], wl[:, fn_lo:fn_lo + D_MODEL]], axis=1).astype(BF16)
        w_t = jnp.stack([wl[:, D_MODEL:2 * D_MODEL].T, wl[:, da_lo + 2 * D_MODEL:fn_lo].T]).astype(BF16)
        w_gate_t = wl[:, gate_lo:gate_lo + N_GATE].T.astype(BF16)
        u, kt, vt, fn, gates_t = _inproj(xs, mods[l], norm_g[l, 0].reshape(1, D_MODEL), w_main, w_t, w_gate_t,
                                         cos_t, sin_t, n_batch)
        hf, hb = _mlstm(u, kt, gates_t, ml_gate_b[l], n_batch)
        yd = _attn(u, vt, da_lam[l], da_head_g[l].reshape(D_MODEL, 1), lam_init, need_ctx, n_batch)
        yf = _fourier(fn, tables, need_ctx, n_batch)
        xs = _merge(hf, hb, u, yd, yf, xs, mods[l], ml_head_g[l].reshape(1, D_MODEL),
                    w_br_ml[l].astype(BF16), w_br_da[l].astype(BF16), w_br_fn[l].astype(BF16),
                    w_out[l].astype(BF16), need_ctx, n_batch)
        final = l == DEPTH - 1
        n_out = n_lat if final else xs.shape[0]
        xs = _ffn(xs, mods[l], norm_g[l, 1].reshape(1, D_MODEL), w_ffn_in[l].astype(BF16),
                  w_ffn_out[l].astype(BF16), final_g2, final, n_out, n_batch)
    return xs.reshape(n_batch, SEQ, D_MODEL)
```

```python
import functools
import math

import jax
import jax.numpy as jnp
from jax import lax
from jax.experimental import pallas as pl
from jax.experimental.pallas import tpu as pltpu

D_MODEL = 1024
SEQ = 4096
DEPTH = 4
CTX_LEN = 256
GRID_W = 64
NORM_EPS = 1e-6

ML_HEADS = 4
ML_DK = 256
ML_CHUNK = 128

DA_HEADS = 8
DA_DH = 64
DA_DV = 2 * DA_DH
ROPE_BASE = 10000.0
ATTN_TQ = 512
ATTN_ROW_CHUNKS = 4

FN_GROUPS = 4
FN_GC = 256
FFT_R = 64

D_FF = 2816
N_GATE = 4 * ML_HEADS
N_MOD = 6 * D_MODEL

U_QML, U_VML, U_OML, U_QDA, U_KDA, U_VDA, U_GPRE = 0, 1, 2, 3, 4, 5, 6
U_BLOCKS = 9
W_BLOCKS = U_BLOCKS + 1
STEP_KT = 1
N_COL_STEPS = W_BLOCKS + 1

VMEM_LIMIT_V7X = 56 * 1024 * 1024

BF16 = jnp.bfloat16
F32 = jnp.float32


def _cparams(sem):
    return pltpu.CompilerParams(dimension_semantics=sem, vmem_limit_bytes=VMEM_LIMIT_V7X)


def _dot(a, b):
    return jnp.dot(a, b, preferred_element_type=F32)


def _dot_nt(a, b):
    return lax.dot_general(a, b, (((1,), (1,)), ((), ())), preferred_element_type=F32)


def _mod_row(mod_ref, i, rows_per_batch, n_batch, col):
    r = jnp.minimum(i // rows_per_batch, n_batch)
    return mod_ref[pl.ds(r, 1), col * D_MODEL:(col + 1) * D_MODEL]


def _row_tile(n_batch):
    return min(512, n_batch * CTX_LEN)


def _rms(x, g):
    return x * lax.rsqrt(jnp.mean(x * x, axis=-1, keepdims=True) + NORM_EPS) * g


def _mods_kernel(c_ref, w_ref, b_ref, o_ref):
    c = c_ref[...]
    s = (c * jax.nn.sigmoid(c)).astype(BF16)
    o_ref[0] = _dot(s, w_ref[0].astype(BF16)) + b_ref[0]


def _mods(cc, w_ada, b_ada):
    tn = 1536
    return pl.pallas_call(
        _mods_kernel,
        grid=(DEPTH, N_MOD // tn),
        in_specs=[
            pl.BlockSpec((8, D_MODEL), lambda l, j: (0, 0)),
            pl.BlockSpec((1, D_MODEL, tn), lambda l, j: (l, 0, j)),
            pl.BlockSpec((1, 1, tn), lambda l, j: (l, 0, j)),
        ],
        out_specs=pl.BlockSpec((1, 8, tn), lambda l, j: (l, 0, j)),
        out_shape=jax.ShapeDtypeStruct((DEPTH, 8, N_MOD), F32),
        compiler_params=_cparams(("arbitrary", "arbitrary")),
        name="adaln_mods",
    )(cc, w_ada, b_ada.reshape(DEPTH, 1, N_MOD))


def _inproj_kernel(x_ref, mod_ref, g_ref, w_ref, wkt_ref, wgt_ref, cos_ref, sin_ref,
                   u_ref, kt_ref, fn_ref, gate_ref, xn_ref, *, rows_per_batch, n_batch):
    i = pl.program_id(0)
    j = pl.program_id(1)

    @pl.when(j == 0)
    def _():
        y = _rms(x_ref[...], g_ref[...])
        shift = _mod_row(mod_ref, i, rows_per_batch, n_batch, 0)
        scale = _mod_row(mod_ref, i, rows_per_batch, n_batch, 1)
        xn_ref[...] = (y * (1.0 + scale) + shift).astype(BF16)
        gate_ref[...] = _dot_nt(wgt_ref[...], xn_ref[...])

    def product():
        return _dot(xn_ref[...], w_ref[...])

    blk = jnp.where(j == 0, 0, j - 1)
    is_rope = jnp.logical_or(blk == U_QDA, blk == U_KDA)
    is_plain = jnp.logical_and(j != STEP_KT, jnp.logical_and(jnp.logical_not(is_rope), blk < U_BLOCKS))

    @pl.when(is_plain)
    def _():
        u_ref[...] = product().astype(BF16)

    @pl.when(j == STEP_KT)
    def _():
        kt_ref[...] = (_dot_nt(wkt_ref[...], xn_ref[...]) * (ML_DK ** -0.5)).astype(BF16)

    @pl.when(is_rope)
    def _():
        acc = product()
        cos = cos_ref[...]
        sin = sin_ref[...]
        for t in range(acc.shape[1] // DA_DV):
            sl = slice(t * DA_DV, (t + 1) * DA_DV)
            x = acc[:, sl]
            u_ref[:, sl] = (x * cos + pltpu.roll(x, DA_DV // 2, 1) * sin).astype(BF16)

    @pl.when(blk == U_BLOCKS)
    def _():
        acc = product()
        for g in range(FN_GROUPS):
            fn_ref[g] = acc[:, g * FN_GC:(g + 1) * FN_GC].astype(BF16)


def _inproj(x, mods, g, w_main, w_kt, w_gate_t, cos_t, sin_t, n_batch):
    n_rows = x.shape[0]
    tm = n_batch * CTX_LEN
    assert SEQ % tm == 0 and cos_t.shape[0] == SEQ + tm
    ni = n_rows // tm
    rows_per_batch = SEQ // tm
    lat_blocks = n_batch * rows_per_batch

    def tab_idx(i, j):
        return (jnp.where(i < lat_blocks, i % rows_per_batch, rows_per_batch), 0)

    def w_blk(j):
        return jnp.where(j == 0, 0, j - 1)

    kern = functools.partial(_inproj_kernel, rows_per_batch=rows_per_batch, n_batch=n_batch)
    return pl.pallas_call(
        kern,
        grid=(ni, N_COL_STEPS),
        in_specs=[
            pl.BlockSpec((tm, D_MODEL), lambda i, j: (i, 0)),
            pl.BlockSpec((8, N_MOD), lambda i, j: (0, 0)),
            pl.BlockSpec((1, D_MODEL), lambda i, j: (0, 0)),
            pl.BlockSpec((D_MODEL, D_MODEL), lambda i, j: (0, w_blk(j))),
            pl.BlockSpec((D_MODEL, D_MODEL), lambda i, j: (0, 0)),
            pl.BlockSpec((N_GATE, D_MODEL), lambda i, j: (0, 0)),
            pl.BlockSpec((tm, 128), tab_idx),
            pl.BlockSpec((tm, 128), tab_idx),
        ],
        out_specs=[
            pl.BlockSpec((tm, D_MODEL), lambda i, j: (i, jnp.minimum(w_blk(j), U_BLOCKS - 1))),
            pl.BlockSpec((D_MODEL, tm), lambda i, j: (0, i)),
            pl.BlockSpec((FN_GROUPS, tm, FN_GC), lambda i, j: (0, i, 0)),
            pl.BlockSpec((N_GATE, tm), lambda i, j: (0, i)),
        ],
        out_shape=[
            jax.ShapeDtypeStruct((n_rows, U_BLOCKS * D_MODEL), BF16),
            jax.ShapeDtypeStruct((D_MODEL, n_rows), BF16),
            jax.ShapeDtypeStruct((FN_GROUPS, n_rows, FN_GC), BF16),
            jax.ShapeDtypeStruct((N_GATE, n_rows), F32),
        ],
        scratch_shapes=[pltpu.VMEM((tm, D_MODEL), BF16)],
        compiler_params=_cparams(("arbitrary", "arbitrary")),
        name="in_proj",
    )(x, mods, g, w_main, w_kt, w_gate_t, cos_t, sin_t)


def _split3(x):
    hi = x.astype(BF16).astype(F32)
    mid = (x - hi).astype(BF16).astype(F32)
    lo = (x - hi - mid).astype(BF16).astype(F32)
    return hi, mid, lo


def _mlstm_kernel(qf_ref, ktf_ref, vf_ref, gtf_ref, qb_ref, ktb_ref, vb_ref, gtb_ref, bias_ref,
                  hf_ref, hb_ref, *state_refs):
    cx_refs, m_refs = state_refs[:2 * ML_HEADS], state_refs[2 * ML_HEADS:]
    s = pl.program_id(1)
    L = ML_CHUNK
    H = ML_HEADS
    W = 128

    @pl.when(s == 0)
    def _():
        for ref in state_refs:
            ref[...] = jnp.zeros_like(ref)

    t_idx = lax.broadcasted_iota(jnp.int32, (L, L), 0)
    s_idx = lax.broadcasted_iota(jnp.int32, (L, L), 1)
    eye = t_idx == s_idx
    sub8 = lax.broadcasted_iota(jnp.int32, (8, W), 0)
    ones_w = jnp.ones((L, W), BF16)
    er = lax.broadcasted_iota(jnp.int32, (4 * L, 2 * W), 0)
    ec = lax.broadcasted_iota(jnp.int32, (4 * L, 2 * W), 1)
    expand = jnp.where((er < 3 * L) == (ec < W), 1.0, 0.0).astype(BF16)

    def running_max_rows(x, d):
        n_tiles = L // 8
        out = [None] * n_tiles
        carry = None
        for j in (range(n_tiles) if d == 0 else range(n_tiles - 1, -1, -1)):
            r = x[8 * j:8 * (j + 1)]
            k = 1
            while k < 8:
                if d == 0:
                    r = jnp.maximum(r, jnp.where(sub8 >= k, pltpu.roll(r, k, 0), -jnp.inf))
                else:
                    r = jnp.maximum(r, jnp.where(sub8 < 8 - k, pltpu.roll(r, 8 - k, 0), -jnp.inf))
                k *= 2
            if carry is not None:
                r = jnp.maximum(r, carry)
            carry = jnp.broadcast_to(r[7:8] if d == 0 else r[0:1], (8, W))
            out[j] = r
        return jnp.concatenate(out, axis=0)

    def prelude(d, gt_ref):
        before = (t_idx <= s_idx) if d == 0 else (t_idx >= s_idx)
        gt = gt_ref[2 * H * d:2 * H * (d + 1), :] + bias_ref[2 * H * d:2 * H * (d + 1), :]
        i4, f4 = gt[0:H], gt[H:2 * H]
        lf4 = jnp.minimum(f4, 0.0) - jnp.log1p(jnp.exp(-jnp.abs(f4)))
        lf_terms = jnp.concatenate(list(_split3(lf4)) + [jnp.zeros((H, L), F32)], axis=0).astype(BF16)
        cum_rhs = jnp.concatenate([jnp.where(before, 1.0, 0.0).astype(BF16), ones_w], axis=1)
        r = _dot(lf_terms, cum_rhs)
        bx = r[0:H] + r[H:2 * H] + r[2 * H:3 * H]
        b4, bend4 = bx[:, 0:L], bx[:, L:L + W]
        c4 = i4 - b4
        m_prev4 = m_refs[d][...]
        g4 = bend4 - b4 + i4
        m_new4 = jnp.maximum(bend4 + m_prev4, jnp.max(g4, axis=1, keepdims=True))
        m_refs[d][...] = m_new4
        return dict(c4=c4, col_terms=_split3(b4) + (c4.astype(BF16).astype(F32),), m_prev4=m_prev4,
                    decay4=jnp.exp(bend4 + m_prev4 - m_new4), w4=jnp.exp(g4 - m_new4))

    dirs = ((0, qf_ref, ktf_ref, vf_ref, gtf_ref, hf_ref), (1, qb_ref, ktb_ref, vb_ref, gtb_ref, hb_ref))
    pre = [prelude(d, gt_ref) for d, _, _, _, gt_ref, _ in dirs]
    for d, q_ref, kt_ref, v_ref, _, h_ref in dirs:
        causal = (s_idx <= t_idx) if d == 0 else (s_idx >= t_idx)
        c4, m_prev4, decay4, w4 = pre[d]["c4"], pre[d]["m_prev4"], pre[d]["decay4"], pre[d]["w4"]
        for h in range(H):
            st = d * H + h
            hs = slice(h * ML_DK, (h + 1) * ML_DK)
            diag = jnp.concatenate([jnp.where(eye, x[h:h + 1, :], 0.0) for x in pre[d]["col_terms"]],
                                   axis=1).astype(BF16)
            wide = _dot(diag, expand)
            b_w = wide[:, 0:W]
            m_w = jnp.maximum(running_max_rows(wide[:, W:2 * W], d), m_prev4[h:h + 1, :])

            q = q_ref[:, hs]
            kt = kt_ref[hs, :]
            vx = jnp.concatenate([v_ref[:, hs], ones_w], axis=1)
            a = (jnp.where(causal, jnp.exp(c4[h:h + 1, :] - m_w), 0.0) * _dot(q, kt)).astype(BF16)
            cx_prev = cx_refs[st][...]
            qc = _dot(q, cx_prev.astype(BF16))
            av = _dot(a, vx)
            sc_w = jnp.exp(m_prev4[h:h + 1, :] - m_w)
            den = sc_w * qc[:, ML_DK:] + av[:, ML_DK:]
            inv = 1.0 / jnp.maximum(jnp.abs(den), jnp.exp(-(b_w + m_w)))
            for t in range(ML_DK // W):
                ts = slice(t * W, (t + 1) * W)
                h_ref[:, h * ML_DK + t * W:h * ML_DK + (t + 1) * W] = (
                    (sc_w * qc[:, ts] + av[:, ts]) * inv).astype(h_ref.dtype)

            kw = (kt.astype(F32) * w4[h:h + 1, :]).astype(BF16)
            dec = jnp.concatenate([decay4[h:h + 1, :]] * (ML_DK // W + 1), axis=1)
            cx_refs[st][...] = dec * cx_prev + _dot(kw, vx)


def _mlstm(u, kt, gates_t, gate_b, n_batch):
    n_rows = u.shape[0]
    L = ML_CHUNK
    lat_chunks = SEQ // L
    ctx_chunks = CTX_LEN // L
    n_steps = ctx_chunks + lat_chunks
    ctx_base = n_batch * lat_chunks

    def rowblk(d):
        def f(b, s):
            in_ctx = s < ctx_chunks
            if d == 0:
                c = jnp.where(in_ctx, s, s - ctx_chunks)
            else:
                c = jnp.where(in_ctx, ctx_chunks - 1 - s, lat_chunks - 1 - (s - ctx_chunks))
            return jnp.where(in_ctx, ctx_base + ctx_chunks * b, lat_chunks * b) + c
        return f

    def dir_specs(d):
        rb = rowblk(d)
        return [
            pl.BlockSpec((L, D_MODEL), lambda b, s: (rb(b, s), U_QML)),
            pl.BlockSpec((D_MODEL, L), lambda b, s: (0, rb(b, s))),
            pl.BlockSpec((L, D_MODEL), lambda b, s: (rb(b, s), U_VML)),
            pl.BlockSpec((N_GATE, L), lambda b, s: (0, rb(b, s))),
        ]

    def out_spec(d):
        rb = rowblk(d)
        return pl.BlockSpec((L, D_MODEL), lambda b, s: (rb(b, s), 0))

    return pl.pallas_call(
        _mlstm_kernel,
        grid=(n_batch, n_steps),
        in_specs=dir_specs(0) + dir_specs(1) + [pl.BlockSpec((N_GATE, 1), lambda b, s: (0, 0))],
        out_specs=[out_spec(0), out_spec(1)],
        out_shape=[jax.ShapeDtypeStruct((n_rows, D_MODEL), BF16)] * 2,
        scratch_shapes=(
            [pltpu.VMEM((ML_DK, ML_DK + 128), F32)] * (2 * ML_HEADS)
            + [pltpu.VMEM((ML_HEADS, 128), F32)] * 2),
        compiler_params=_cparams(("arbitrary", "arbitrary")),
        name="mlstm_scan",
    )(u, kt, u, gates_t, u, kt, u, gates_t, gate_b.reshape(N_GATE, 1))


def _attn_lambda(lam_ref, lam_init):
    lq = lam_ref[...]
    return (jnp.exp(jnp.sum(lq[0:1] * lq[1:2], axis=1, keepdims=True))
            - jnp.exp(jnp.sum(lq[2:3] * lq[3:4], axis=1, keepdims=True)) + lam_init)


def _attn_queries(q_ref):
    q = q_ref[...]
    lane = lax.broadcasted_iota(jnp.int32, (1, DA_DV), 1)
    zero = jnp.zeros_like(q)
    is_map0 = (lane % DA_DH) < (DA_DH // 2)
    q2 = jnp.concatenate([jnp.where(is_map0, q, zero), jnp.where(is_map0, zero, q)], axis=0)
    return q2 * (DA_DH ** -0.5)


def _attn_values(s_chunks, m_chunks, vx, lam, g, lam_init, tq):
    accs = [_dot(jnp.exp(s - m).astype(BF16), vx) for s, m in zip(s_chunks, m_chunks)]
    acc = jnp.concatenate(accs, axis=0)
    o0 = acc[0:tq, 0:DA_DV] * (1.0 / acc[0:tq, DA_DV:DA_DV + 1])
    o1 = acc[tq:, 0:DA_DV] * (1.0 / acc[tq:, DA_DV:DA_DV + 1])
    return _rms(o0 - lam * o1, g) * (1.0 - lam_init)


def _attn_kernel(q_ref, kl_ref, vl_ref, kc_ref, vc_ref, lam_ref, g_ref, o_ref, kx_ref, vx_ref, s_ref, m_ref,
                 *, lam_init):
    qi = pl.program_id(2)
    tq = q_ref.shape[0]
    rows = 2 * tq // ATTN_ROW_CHUNKS

    @pl.when(qi == 0)
    def _():
        kx_ref[0:SEQ, :] = kl_ref[...]
        kx_ref[SEQ:, :] = kc_ref[...]
        vx_ref[0:SEQ, 0:DA_DV] = vl_ref[...]
        vx_ref[SEQ:, 0:DA_DV] = vc_ref[...]
        lane_v = lax.broadcasted_iota(jnp.int32, (SEQ + CTX_LEN, DA_DV), 1)
        vx_ref[:, DA_DV:] = jnp.where(lane_v == 0, 1.0, 0.0).astype(BF16)

    def score(slot):
        s = _dot_nt(_attn_queries(q_ref), kx_ref[...])
        m_ref[slot] = jnp.max(s, axis=1, keepdims=True)
        s_ref[slot] = s

    def drain(slot):
        s_chunks = [s_ref[slot, c * rows:(c + 1) * rows, :] for c in range(ATTN_ROW_CHUNKS)]
        m_chunks = [m_ref[slot, c * rows:(c + 1) * rows, :] for c in range(ATTN_ROW_CHUNKS)]
        o = _attn_values(s_chunks, m_chunks, vx_ref[...], _attn_lambda(lam_ref, lam_init), g_ref[...], lam_init, tq)
        o_ref[...] = o.astype(o_ref.dtype)

    @pl.when(qi == 0)
    def _():
        score(0)

    for parity in (0, 1):
        @pl.when(jnp.logical_and(qi > 0, qi % 2 == parity))
        def _():
            drain(1 - parity)
            score(parity)


def _attn_ctx_kernel(q_ref, kc_ref, vc_ref, lam_ref, g_ref, yin_ref, o_ref, *, lam_init):
    del yin_ref
    tq = q_ref.shape[0]
    s = _dot_nt(_attn_queries(q_ref), kc_ref[...])
    lane_v = lax.broadcasted_iota(jnp.int32, (CTX_LEN, DA_DV), 1)
    vx = jnp.concatenate([vc_ref[...], jnp.where(lane_v == 0, 1.0, 0.0).astype(BF16)], axis=1)
    o = _attn_values([s], [jnp.max(s, axis=1, keepdims=True)], vx, _attn_lambda(lam_ref, lam_init),
                     g_ref[...], lam_init, tq)
    o_ref[...] = o.astype(o_ref.dtype)


def _attn(u, da_lam, head_g, lam_init, need_ctx, n_batch):
    n_rows = u.shape[0]
    tq = ATTN_TQ
    lat_qblocks = SEQ // tq
    ctx_base = n_batch * (SEQ // CTX_LEN)
    cpb = D_MODEL // DA_DV
    n_keys = SEQ + CTX_LEN

    kern = functools.partial(_attn_kernel, lam_init=lam_init)
    yd = pl.pallas_call(
        kern,
        grid=(n_batch, DA_HEADS, lat_qblocks + 1),
        in_specs=[
            pl.BlockSpec((tq, DA_DV),
                         lambda b, h, qi: (b * lat_qblocks + jnp.minimum(qi, lat_qblocks - 1), U_QDA * cpb + h)),
            pl.BlockSpec((SEQ, DA_DV), lambda b, h, qi: (b, U_KDA * cpb + h)),
            pl.BlockSpec((SEQ, DA_DV), lambda b, h, qi: (b, U_VDA * cpb + h)),
            pl.BlockSpec((CTX_LEN, DA_DV), lambda b, h, qi: (ctx_base + b, U_KDA * cpb + h)),
            pl.BlockSpec((CTX_LEN, DA_DV), lambda b, h, qi: (ctx_base + b, U_VDA * cpb + h)),
            pl.BlockSpec((4, DA_DH), lambda b, h, qi: (0, 0)),
            pl.BlockSpec((1, DA_DV), lambda b, h, qi: (0, h)),
        ],
        out_specs=pl.BlockSpec((tq, DA_DV), lambda b, h, qi: (b * lat_qblocks + jnp.maximum(qi - 1, 0), h)),
        out_shape=jax.ShapeDtypeStruct((n_rows, D_MODEL), BF16),
        scratch_shapes=[
            pltpu.VMEM((n_keys, DA_DV), BF16),
            pltpu.VMEM((n_keys, 2 * DA_DV), BF16),
            pltpu.VMEM((2, 2 * tq, n_keys), F32),
            pltpu.VMEM((2, 2 * tq, 1), F32),
        ],
        compiler_params=_cparams(("arbitrary", "arbitrary", "arbitrary")),
        name="diff_attn",
    )(u, u, u, u, u, da_lam, head_g)
    if not need_ctx:
        return yd
    kern_ctx = functools.partial(_attn_ctx_kernel, lam_init=lam_init)
    return pl.pallas_call(
        kern_ctx,
        grid=(n_batch, DA_HEADS),
        in_specs=[
            pl.BlockSpec((CTX_LEN, DA_DV), lambda b, h: (ctx_base + b, U_QDA * cpb + h)),
            pl.BlockSpec((CTX_LEN, DA_DV), lambda b, h: (ctx_base + b, U_KDA * cpb + h)),
            pl.BlockSpec((CTX_LEN, DA_DV), lambda b, h: (ctx_base + b, U_VDA * cpb + h)),
            pl.BlockSpec((4, DA_DH), lambda b, h: (0, 0)),
            pl.BlockSpec((1, DA_DV), lambda b, h: (0, h)),
            pl.BlockSpec(memory_space=pl.ANY),
        ],
        out_specs=pl.BlockSpec((CTX_LEN, DA_DV), lambda b, h: (ctx_base + b, h)),
        out_shape=jax.ShapeDtypeStruct((n_rows, D_MODEL), BF16),
        input_output_aliases={5: 0},
        compiler_params=_cparams(("arbitrary", "arbitrary")),
        name="diff_attn_ctx",
    )(u, u, u, da_lam, head_g, yd)


def _dft_tables():
    R = FFT_R

    def cs(num, period):
        ang = (num % period).astype(F32) * (2.0 * math.pi / period)
        return jnp.cos(ang), jnp.sin(ang)

    idx = jnp.arange(R, dtype=jnp.int32)
    c1, s1 = cs(idx[:, None] * idx[None, :], R)
    a1 = jnp.concatenate([c1, -s1], axis=0).astype(BF16)
    f2 = idx[:, None, None]
    f1 = idx[None, :, None]
    t1 = idx[None, None, :]
    mc, ms = cs(t1 * (R * f1 + f2), SEQ)
    b2 = jnp.concatenate([jnp.concatenate([mc, ms], axis=2),
                          jnp.concatenate([-ms, mc], axis=2)], axis=1).astype(BF16)
    ch = jnp.arange(FN_GC, dtype=jnp.int32)
    cc, sc = cs(ch[:, None] * ch[None, :], FN_GC)
    csm = jnp.concatenate([cc, sc], axis=0).astype(BF16)
    actx = jnp.concatenate([cc, -sc], axis=0).astype(BF16)
    return a1, b2, csm, actx


def _fft1_kernel(a_ref, z_ref, p_ref):
    p_ref[0, 0] = _dot(a_ref[...], z_ref[0]).astype(p_ref.dtype)


def _fft1(fn_view, a1, n_batch):
    R = FFT_R
    lanes = R * FN_GC
    lc = 4096
    return pl.pallas_call(
        _fft1_kernel,
        grid=(FN_GROUPS, n_batch, lanes // lc),
        in_specs=[
            pl.BlockSpec((2 * R, R), lambda g, b, c: (0, 0)),
            pl.BlockSpec((1, R, lc), lambda g, b, c: (g, b, c)),
        ],
        out_specs=pl.BlockSpec((1, 1, 2 * R, lc), lambda g, b, c: (g, b, 0, c)),
        out_shape=jax.ShapeDtypeStruct((FN_GROUPS, n_batch, 2 * R, lanes), BF16),
        compiler_params=_cparams(("arbitrary", "arbitrary", "arbitrary")),
        name="fourier_stage1",
    )(a1, fn_view)


def _fft2_kernel(p_ref, b_ref, cs_ref, o_ref, *, f2b):
    R = FFT_R
    cc = cs_ref[0:FN_GC, :]
    sc = cs_ref[FN_GC:2 * FN_GC, :]
    norm = 1.0 / math.sqrt(SEQ * FN_GC)
    for g in range(FN_GROUPS):
        for jj in range(f2b):
            stacked = jnp.concatenate([p_ref[g, 0, 0, jj], p_ref[g, 0, 1, jj]], axis=0)
            x = _dot(b_ref[jj], stacked)
            y = _dot(x[0:R].astype(BF16), cc) + _dot(x[R:2 * R].astype(BF16), sc)
            lo = jj * D_MODEL + g * FN_GC
            o_ref[:, lo:lo + FN_GC] = (y * norm).astype(o_ref.dtype)


def _fft2(p6, b2, csm, n_rows, n_batch):
    R = FFT_R
    f2b = 8
    kern = functools.partial(_fft2_kernel, f2b=f2b)
    return pl.pallas_call(
        kern,
        grid=(n_batch, R // f2b),
        in_specs=[
            pl.BlockSpec((FN_GROUPS, 1, 2, f2b, R, FN_GC), lambda b, f: (0, b, 0, f, 0, 0)),
            pl.BlockSpec((f2b, 2 * R, 2 * R), lambda b, f: (f, 0, 0)),
            pl.BlockSpec((2 * FN_GC, FN_GC), lambda b, f: (0, 0)),
        ],
        out_specs=pl.BlockSpec((R, f2b * D_MODEL), lambda b, f: (b, f)),
        out_shape=jax.ShapeDtypeStruct((n_rows // R, R * D_MODEL), BF16),
        compiler_params=_cparams(("arbitrary", "arbitrary")),
        name="fourier_stage2",
    )(p6, b2, csm)


def _fft_ctx_kernel(z_ref, a_ref, cs_ref, yin_ref, o_ref):
    del yin_ref
    cc = cs_ref[0:FN_GC, :]
    sc = cs_ref[FN_GC:2 * FN_GC, :]
    norm = 1.0 / math.sqrt(CTX_LEN * FN_GC)
    for g in range(FN_GROUPS):
        p = _dot(a_ref[...], z_ref[g])
        y = _dot(p[0:CTX_LEN].astype(BF16), cc) + _dot(p[CTX_LEN:2 * CTX_LEN].astype(BF16), sc)
        o_ref[:, g * FN_GC:(g + 1) * FN_GC] = (y * norm).astype(o_ref.dtype)


def _fft_ctx(fn, actx, csm, yf, n_batch):
    n_rows = yf.shape[0]
    ctx_base = n_batch * (SEQ // CTX_LEN)
    return pl.pallas_call(
        _fft_ctx_kernel,
        grid=(n_batch,),
        in_specs=[
            pl.BlockSpec((FN_GROUPS, CTX_LEN, FN_GC), lambda b: (0, ctx_base + b, 0)),
            pl.BlockSpec((2 * CTX_LEN, CTX_LEN), lambda b: (0, 0)),
            pl.BlockSpec((2 * FN_GC, FN_GC), lambda b: (0, 0)),
            pl.BlockSpec(memory_space=pl.ANY),
        ],
        out_specs=pl.BlockSpec((CTX_LEN, D_MODEL), lambda b: (ctx_base + b, 0)),
        out_shape=jax.ShapeDtypeStruct((n_rows, D_MODEL), BF16),
        input_output_aliases={3: 0},
        compiler_params=_cparams(("arbitrary",)),
        name="fourier_ctx",
    )(fn, actx, csm, yf)


def _fourier(fn, tables, need_ctx, n_batch):
    a1, b2, csm, actx = tables
    n_rows = fn.shape[1]
    R = FFT_R
    p = _fft1(fn.reshape(FN_GROUPS, n_rows // R, R * FN_GC), a1, n_batch)
    yf = _fft2(p.reshape(FN_GROUPS, n_batch, 2, R, R, FN_GC), b2, csm, n_rows, n_batch)
    yf = yf.reshape(n_rows, D_MODEL)
    if need_ctx:
        yf = _fft_ctx(fn, actx, csm, yf, n_batch)
    return yf


def _merge_kernel(hf_ref, hb_ref, o_ref, gm_ref, gd_ref, gf_ref, yd_ref, yf_ref, x_ref, mod_ref, hg_ref,
                  wml_ref, wda_ref, wfn_ref, wout_ref, out_ref, *, rows_per_batch, n_batch):
    i = pl.program_id(0)
    hsum = hf_ref[...].astype(F32) + hb_ref[...].astype(F32)
    hg = hg_ref[...]
    parts = []
    for h in range(ML_HEADS):
        hs = slice(h * ML_DK, (h + 1) * ML_DK)
        parts.append(_rms(hsum[:, hs], hg[:, hs]))
    ym = (jnp.concatenate(parts, axis=1) * jax.nn.sigmoid(o_ref[...].astype(F32))).astype(BF16)
    y = (jax.nn.sigmoid(gm_ref[...].astype(F32)) * _dot(ym, wml_ref[...])
         + jax.nn.sigmoid(gd_ref[...].astype(F32)) * _dot(yd_ref[...], wda_ref[...])
         + jax.nn.sigmoid(gf_ref[...].astype(F32)) * _dot(yf_ref[...], wfn_ref[...]))
    gate = _mod_row(mod_ref, i, rows_per_batch, n_batch, 2)
    out_ref[...] = x_ref[...] + gate * _dot(y.astype(BF16), wout_ref[...])


def _merge(hf, hb, u, yd, yf, x, mods, head_g, wml, wda, wfn, wout, need_ctx, n_batch):
    n_rows = x.shape[0]
    tm = _row_tile(n_batch)
    rows_per_batch = SEQ // tm
    ni = (n_rows if need_ctx else n_batch * SEQ) // tm
    kern = functools.partial(_merge_kernel, rows_per_batch=rows_per_batch, n_batch=n_batch)
    row = lambda i: (i, 0)
    full = lambda i: (0, 0)
    wspec = pl.BlockSpec((D_MODEL, D_MODEL), full)
    return pl.pallas_call(
        kern,
        grid=(ni,),
        in_specs=[
            pl.BlockSpec((tm, D_MODEL), row),
            pl.BlockSpec((tm, D_MODEL), row),
            pl.BlockSpec((tm, D_MODEL), lambda i: (i, U_OML)),
            pl.BlockSpec((tm, D_MODEL), lambda i: (i, U_GPRE)),
            pl.BlockSpec((tm, D_MODEL), lambda i: (i, U_GPRE + 1)),
            pl.BlockSpec((tm, D_MODEL), lambda i: (i, U_GPRE + 2)),
            pl.BlockSpec((tm, D_MODEL), row),
            pl.BlockSpec((tm, D_MODEL), row),
            pl.BlockSpec((tm, D_MODEL), row),
            pl.BlockSpec((8, N_MOD), full),
            pl.BlockSpec((1, D_MODEL), full),
            wspec, wspec, wspec, wspec,
        ],
        out_specs=pl.BlockSpec((tm, D_MODEL), row),
        out_shape=jax.ShapeDtypeStruct((n_rows, D_MODEL), F32),
        compiler_params=_cparams(("arbitrary",)),
        name="merge_out_proj",
    )(hf, hb, u, u, u, u, yd, yf, x, mods, head_g, wml, wda, wfn, wout)


FFN_CHUNKS = ((0, 1024), (1024, 1024), (2048, 768))


def _ffn_kernel(x_ref, mod_ref, g_ref, win_ref, wout_ref, fg_ref, out_ref, *, rows_per_batch, n_batch, final):
    i = pl.program_id(0)
    x = x_ref[...]
    shift = _mod_row(mod_ref, i, rows_per_batch, n_batch, 3)
    scale = _mod_row(mod_ref, i, rows_per_batch, n_batch, 4)
    gate = _mod_row(mod_ref, i, rows_per_batch, n_batch, 5)
    h = (_rms(x, g_ref[...]) * (1.0 + scale) + shift).astype(BF16)
    acc = None
    for lo, width in FFN_CHUNKS:
        a = _dot(h, win_ref[:, lo:lo + width])
        b = _dot(h, win_ref[:, D_FF + lo:D_FF + lo + width])
        act = (a * jax.nn.sigmoid(a) * b).astype(BF16)
        part = _dot(act, wout_ref[lo:lo + width, :])
        acc = part if acc is None else acc + part
    xn = x + gate * acc
    out_ref[...] = _rms(xn, fg_ref[...]) if final else xn


def _ffn(x, mods, g, w_in, w_out, final_g, final, n_rows_out, n_batch):
    tm = _row_tile(n_batch)
    rows_per_batch = SEQ // tm
    kern = functools.partial(_ffn_kernel, rows_per_batch=rows_per_batch, n_batch=n_batch, final=final)
    row = lambda i: (i, 0)
    full = lambda i: (0, 0)
    return pl.pallas_call(
        kern,
        grid=(n_rows_out // tm,),
        in_specs=[
            pl.BlockSpec((tm, D_MODEL), row),
            pl.BlockSpec((8, N_MOD), full),
            pl.BlockSpec((1, D_MODEL), full),
            pl.BlockSpec((D_MODEL, 2 * D_FF), full),
            pl.BlockSpec((D_FF, D_MODEL), full),
            pl.BlockSpec((1, D_MODEL), full),
        ],
        out_specs=pl.BlockSpec((tm, D_MODEL), row),
        out_shape=jax.ShapeDtypeStruct((n_rows_out, D_MODEL), F32),
        compiler_params=_cparams(("arbitrary",)),
        name="swiglu_ffn",
    )(x, mods, g, w_in, w_out, final_g)


def _da_col_perm(w):
    half = DA_DH // 2
    return w.reshape(-1, DA_HEADS, 2, 2, half).transpose(0, 1, 3, 2, 4).reshape(-1, DA_HEADS * DA_DV)


def _rope_tables(pad):
    n_freq = DA_DH // 4
    rows = SEQ // GRID_W
    inv = ROPE_BASE ** (-jnp.arange(n_freq, dtype=F32) / n_freq)
    r = jnp.repeat(jnp.arange(rows, dtype=F32), GRID_W)
    col = jnp.tile(jnp.arange(GRID_W, dtype=F32), rows)
    ang = jnp.concatenate([r[:, None] * inv, col[:, None] * inv], axis=-1)
    cos, sin = jnp.cos(ang), jnp.sin(ang)
    cos_t = jnp.concatenate([cos, cos, cos, cos], axis=-1)
    sin_t = jnp.concatenate([-sin, -sin, sin, sin], axis=-1)
    cos_t = jnp.concatenate([cos_t, jnp.ones((pad, 128), F32)], axis=0)
    sin_t = jnp.concatenate([sin_t, jnp.zeros((pad, 128), F32)], axis=0)
    return cos_t, sin_t


def kernel(x, c, ctx, c_ctx, w_ada, b_ada, norm_g, w_in, ml_gate_b, ml_head_g, da_lam, da_head_g,
           w_br_ml, w_br_da, w_br_fn, w_out, w_ffn_in, w_ffn_out, final_g):
    n_batch = x.shape[0]
    n_lat = n_batch * SEQ
    xs = jnp.concatenate([x.reshape(n_lat, D_MODEL), ctx.reshape(n_batch * CTX_LEN, D_MODEL)], axis=0)
    cc = jnp.concatenate([c, c_ctx[None, :], jnp.zeros((8 - n_batch - 1, D_MODEL), F32)], axis=0)
    mods = _mods(cc, w_ada, b_ada)
    cos_t, sin_t = _rope_tables(n_batch * CTX_LEN)
    tables = _dft_tables()
    gate_lo = 4 * D_MODEL
    fn_lo = gate_lo + N_GATE + 3 * D_MODEL
    final_g2 = final_g.reshape(1, D_MODEL)

    for l in range(DEPTH):
        need_ctx = l < DEPTH - 1
        lam_init = 0.8 - 0.6 * math.exp(-0.3 * l)
        wl = w_in[l]
        da_lo = gate_lo + N_GATE
        w_main = jnp.concatenate([wl[:, :D_MODEL], wl[:, 2 * D_MODEL:gate_lo],
                                  _da_col_perm(wl[:, da_lo:da_lo + D_MODEL]),
                                  _da_col_perm(wl[:, da_lo + D_MODEL:da_lo + 2 * D_MODEL]),
                                  wl[:, da_lo + 2 * D_MODEL:fn_lo],
                                  wl[:, fn_lo + D_MODEL:], wl[:, fn_lo:fn_lo + D_MODEL]], axis=1).astype(BF16)
        w_kt = wl[:, D_MODEL:2 * D_MODEL].T.astype(BF16)
        w_gate_t = wl[:, gate_lo:gate_lo + N_GATE].T.astype(BF16)
        u, kt, fn, gates_t = _inproj(xs, mods[l], norm_g[l, 0].reshape(1, D_MODEL), w_main, w_kt, w_gate_t,
                                     cos_t, sin_t, n_batch)
        hf, hb = _mlstm(u, kt, gates_t, ml_gate_b[l], n_batch)
        yd = _attn(u, da_lam[l], da_head_g[l].reshape(1, D_MODEL), lam_init, need_ctx, n_batch)
        yf = _fourier(fn, tables, need_ctx, n_batch)
        xs = _merge(hf, hb, u, yd, yf, xs, mods[l], ml_head_g[l].reshape(1, D_MODEL),
                    w_br_ml[l].astype(BF16), w_br_da[l].astype(BF16), w_br_fn[l].astype(BF16),
                    w_out[l].astype(BF16), need_ctx, n_batch)
        final = l == DEPTH - 1
        n_out = n_lat if final else xs.shape[0]
        xs = _ffn(xs, mods[l], norm_g[l, 1].reshape(1, D_MODEL), w_ffn_in[l].astype(BF16),
                  w_ffn_out[l].astype(BF16), final_g2, final, n_out, n_batch)
    return xs.reshape(n_batch, SEQ, D_MODEL)
```

```python
import functools
import math

import jax
import jax.numpy as jnp
from jax import lax
from jax.experimental import pallas as pl
from jax.experimental.pallas import tpu as pltpu

D_MODEL = 1024
SEQ = 4096
DEPTH = 4
CTX_LEN = 256
GRID_W = 64
NORM_EPS = 1e-6

ML_HEADS = 4
ML_DK = 256
ML_CHUNK = 128

DA_HEADS = 8
DA_DH = 64
DA_DV = 2 * DA_DH
ROPE_BASE = 10000.0
ATTN_TQ = 512
ATTN_ROW_CHUNKS = 4

FN_GROUPS = 4
FN_GC = 256
FFT_R = 64

D_FF = 2816
N_GATE = 4 * ML_HEADS
N_MOD = 6 * D_MODEL

U_QML, U_VML, U_OML, U_QDA, U_KDA, U_VDA, U_GPRE = 0, 1, 2, 3, 4, 5, 6
U_BLOCKS = 9
W_BLOCKS = U_BLOCKS + 1
STEP_KT = 1
N_COL_STEPS = W_BLOCKS + 1

VMEM_LIMIT_V7X = 56 * 1024 * 1024

BF16 = jnp.bfloat16
F32 = jnp.float32


def _cparams(sem):
    return pltpu.CompilerParams(dimension_semantics=sem, vmem_limit_bytes=VMEM_LIMIT_V7X)


def _dot(a, b):
    return jnp.dot(a, b, preferred_element_type=F32)


def _dot_nt(a, b):
    return lax.dot_general(a, b, (((1,), (1,)), ((), ())), preferred_element_type=F32)


def _mod_row(mod_ref, i, rows_per_batch, n_batch, col):
    r = jnp.minimum(i // rows_per_batch, n_batch)
    return mod_ref[pl.ds(r, 1), col * D_MODEL:(col + 1) * D_MODEL]


def _row_tile(n_batch):
    return min(512, n_batch * CTX_LEN)


def _rms(x, g):
    return x * lax.rsqrt(jnp.mean(x * x, axis=-1, keepdims=True) + NORM_EPS) * g


def _cast_kernel(w_ref, o_ref):
    o_ref[...] = w_ref[...].astype(o_ref.dtype)


def _to_bf16(w):
    n_l, rows, cols = w.shape
    tr = 256
    return pl.pallas_call(
        _cast_kernel,
        grid=(n_l, rows // tr),
        in_specs=[pl.BlockSpec((1, tr, cols), lambda l, i: (l, i, 0))],
        out_specs=pl.BlockSpec((1, tr, cols), lambda l, i: (l, i, 0)),
        out_shape=jax.ShapeDtypeStruct(w.shape, BF16),
        compiler_params=_cparams(("arbitrary", "arbitrary")),
        name="cast_bf16",
    )(w)


def _mods_kernel(c_ref, w_ref, b_ref, o_ref):
    c = c_ref[...]
    s = (c * jax.nn.sigmoid(c)).astype(BF16)
    o_ref[0] = _dot(s, w_ref[0].astype(BF16)) + b_ref[0]


def _mods(cc, w_ada, b_ada):
    tn = 1536
    return pl.pallas_call(
        _mods_kernel,
        grid=(DEPTH, N_MOD // tn),
        in_specs=[
            pl.BlockSpec((8, D_MODEL), lambda l, j: (0, 0)),
            pl.BlockSpec((1, D_MODEL, tn), lambda l, j: (l, 0, j)),
            pl.BlockSpec((1, 1, tn), lambda l, j: (l, 0, j)),
        ],
        out_specs=pl.BlockSpec((1, 8, tn), lambda l, j: (l, 0, j)),
        out_shape=jax.ShapeDtypeStruct((DEPTH, 8, N_MOD), F32),
        compiler_params=_cparams(("arbitrary", "arbitrary")),
        name="adaln_mods",
    )(cc, w_ada, b_ada.reshape(DEPTH, 1, N_MOD))


def _inproj_kernel(x_ref, mod_ref, g_ref, w_ref, wkt_ref, wgt_ref, cos_ref, sin_ref,
                   u_ref, kt_ref, fn_ref, gate_ref, xn_ref, *, rows_per_batch, n_batch):
    i = pl.program_id(0)
    j = pl.program_id(1)

    @pl.when(j == 0)
    def _():
        y = _rms(x_ref[...], g_ref[...])
        shift = _mod_row(mod_ref, i, rows_per_batch, n_batch, 0)
        scale = _mod_row(mod_ref, i, rows_per_batch, n_batch, 1)
        xn_ref[...] = (y * (1.0 + scale) + shift).astype(BF16)
        gate_ref[...] = _dot_nt(wgt_ref[...], xn_ref[...])

    def product():
        return _dot(xn_ref[...], w_ref[...])

    blk = jnp.where(j == 0, 0, j - 1)
    is_rope = jnp.logical_or(blk == U_QDA, blk == U_KDA)
    is_plain = jnp.logical_and(j != STEP_KT, jnp.logical_and(jnp.logical_not(is_rope), blk < U_BLOCKS))

    @pl.when(is_plain)
    def _():
        u_ref[...] = product().astype(BF16)

    @pl.when(j == STEP_KT)
    def _():
        kt_ref[...] = (_dot_nt(wkt_ref[...], xn_ref[...]) * (ML_DK ** -0.5)).astype(BF16)

    @pl.when(is_rope)
    def _():
        acc = product()
        cos = cos_ref[...]
        sin = sin_ref[...]
        for t in range(acc.shape[1] // DA_DV):
            sl = slice(t * DA_DV, (t + 1) * DA_DV)
            x = acc[:, sl]
            u_ref[:, sl] = (x * cos + pltpu.roll(x, DA_DV // 2, 1) * sin).astype(BF16)

    @pl.when(blk == U_BLOCKS)
    def _():
        acc = product()
        for g in range(FN_GROUPS):
            fn_ref[g] = acc[:, g * FN_GC:(g + 1) * FN_GC].astype(BF16)


def _inproj(x, mods, g, w_main, w_kt, w_gate_t, cos_t, sin_t, n_batch):
    n_rows = x.shape[0]
    tm = n_batch * CTX_LEN
    assert SEQ % tm == 0 and cos_t.shape[0] == SEQ + tm
    ni = n_rows // tm
    rows_per_batch = SEQ // tm
    lat_blocks = n_batch * rows_per_batch

    def tab_idx(i, j):
        return (jnp.where(i < lat_blocks, i % rows_per_batch, rows_per_batch), 0)

    def w_blk(j):
        return jnp.where(j == 0, 0, j - 1)

    kern = functools.partial(_inproj_kernel, rows_per_batch=rows_per_batch, n_batch=n_batch)
    return pl.pallas_call(
        kern,
        grid=(ni, N_COL_STEPS),
        in_specs=[
            pl.BlockSpec((tm, D_MODEL), lambda i, j: (i, 0)),
            pl.BlockSpec((8, N_MOD), lambda i, j: (0, 0)),
            pl.BlockSpec((1, D_MODEL), lambda i, j: (0, 0)),
            pl.BlockSpec((D_MODEL, D_MODEL), lambda i, j: (0, w_blk(j))),
            pl.BlockSpec((D_MODEL, D_MODEL), lambda i, j: (0, 0)),
            pl.BlockSpec((N_GATE, D_MODEL), lambda i, j: (0, 0)),
            pl.BlockSpec((tm, 128), tab_idx),
            pl.BlockSpec((tm, 128), tab_idx),
        ],
        out_specs=[
            pl.BlockSpec((tm, D_MODEL), lambda i, j: (i, jnp.minimum(w_blk(j), U_BLOCKS - 1))),
            pl.BlockSpec((D_MODEL, tm), lambda i, j: (0, i)),
            pl.BlockSpec((FN_GROUPS, tm, FN_GC), lambda i, j: (0, i, 0)),
            pl.BlockSpec((N_GATE, tm), lambda i, j: (0, i)),
        ],
        out_shape=[
            jax.ShapeDtypeStruct((n_rows, U_BLOCKS * D_MODEL), BF16),
            jax.ShapeDtypeStruct((D_MODEL, n_rows), BF16),
            jax.ShapeDtypeStruct((FN_GROUPS, n_rows, FN_GC), BF16),
            jax.ShapeDtypeStruct((N_GATE, n_rows), F32),
        ],
        scratch_shapes=[pltpu.VMEM((tm, D_MODEL), BF16)],
        compiler_params=_cparams(("arbitrary", "arbitrary")),
        name="in_proj",
    )(x, mods, g, w_main, w_kt, w_gate_t, cos_t, sin_t)


def _split3(x):
    hi = x.astype(BF16).astype(F32)
    mid = (x - hi).astype(BF16).astype(F32)
    lo = (x - hi - mid).astype(BF16).astype(F32)
    return hi, mid, lo


def _mlstm_kernel(qf_ref, ktf_ref, vf_ref, gtf_ref, qb_ref, ktb_ref, vb_ref, gtb_ref, bias_ref,
                  hf_ref, hb_ref, *state_refs):
    cx_refs, m_refs = state_refs[:2 * ML_HEADS], state_refs[2 * ML_HEADS:]
    s = pl.program_id(1)
    L = ML_CHUNK
    H = ML_HEADS
    W = 128

    @pl.when(s == 0)
    def _():
        for ref in state_refs:
            ref[...] = jnp.zeros_like(ref)

    t_idx = lax.broadcasted_iota(jnp.int32, (L, L), 0)
    s_idx = lax.broadcasted_iota(jnp.int32, (L, L), 1)
    eye = t_idx == s_idx
    sub8 = lax.broadcasted_iota(jnp.int32, (8, W), 0)
    ones_w = jnp.ones((L, W), BF16)
    er = lax.broadcasted_iota(jnp.int32, (4 * L, 2 * W), 0)
    ec = lax.broadcasted_iota(jnp.int32, (4 * L, 2 * W), 1)
    expand = jnp.where((er < 3 * L) == (ec < W), 1.0, 0.0).astype(BF16)

    def running_max_rows(x, d):
        n_tiles = L // 8
        out = [None] * n_tiles
        carry = None
        for j in (range(n_tiles) if d == 0 else range(n_tiles - 1, -1, -1)):
            r = x[8 * j:8 * (j + 1)]
            k = 1
            while k < 8:
                if d == 0:
                    r = jnp.maximum(r, jnp.where(sub8 >= k, pltpu.roll(r, k, 0), -jnp.inf))
                else:
                    r = jnp.maximum(r, jnp.where(sub8 < 8 - k, pltpu.roll(r, 8 - k, 0), -jnp.inf))
                k *= 2
            if carry is not None:
                r = jnp.maximum(r, carry)
            carry = jnp.broadcast_to(r[7:8] if d == 0 else r[0:1], (8, W))
            out[j] = r
        return jnp.concatenate(out, axis=0)

    def prelude(d, gt_ref):
        before = (t_idx <= s_idx) if d == 0 else (t_idx >= s_idx)
        gt = gt_ref[2 * H * d:2 * H * (d + 1), :] + bias_ref[2 * H * d:2 * H * (d + 1), :]
        i4, f4 = gt[0:H], gt[H:2 * H]
        lf4 = jnp.minimum(f4, 0.0) - jnp.log1p(jnp.exp(-jnp.abs(f4)))
        lf_terms = jnp.concatenate(list(_split3(lf4)) + [jnp.zeros((H, L), F32)], axis=0).astype(BF16)
        cum_rhs = jnp.concatenate([jnp.where(before, 1.0, 0.0).astype(BF16), ones_w], axis=1)
        r = _dot(lf_terms, cum_rhs)
        bx = r[0:H] + r[H:2 * H] + r[2 * H:3 * H]
        b4, bend4 = bx[:, 0:L], bx[:, L:L + W]
        c4 = i4 - b4
        m_prev4 = m_refs[d][...]
        g4 = bend4 - b4 + i4
        m_new4 = jnp.maximum(bend4 + m_prev4, jnp.max(g4, axis=1, keepdims=True))
        m_refs[d][...] = m_new4
        return dict(c4=c4, col_terms=_split3(b4) + (c4.astype(BF16).astype(F32),), m_prev4=m_prev4,
                    decay4=jnp.exp(bend4 + m_prev4 - m_new4), w4=jnp.exp(g4 - m_new4))

    dirs = ((0, qf_ref, ktf_ref, vf_ref, gtf_ref, hf_ref), (1, qb_ref, ktb_ref, vb_ref, gtb_ref, hb_ref))
    pre = [prelude(d, gt_ref) for d, _, _, _, gt_ref, _ in dirs]
    for d, q_ref, kt_ref, v_ref, _, h_ref in dirs:
        causal = (s_idx <= t_idx) if d == 0 else (s_idx >= t_idx)
        c4, m_prev4, decay4, w4 = pre[d]["c4"], pre[d]["m_prev4"], pre[d]["decay4"], pre[d]["w4"]
        for h in range(H):
            st = d * H + h
            hs = slice(h * ML_DK, (h + 1) * ML_DK)
            diag = jnp.concatenate([jnp.where(eye, x[h:h + 1, :], 0.0) for x in pre[d]["col_terms"]],
                                   axis=1).astype(BF16)
            wide = _dot(diag, expand)
            b_w = wide[:, 0:W]
            m_w = jnp.maximum(running_max_rows(wide[:, W:2 * W], d), m_prev4[h:h + 1, :])

            q = q_ref[:, hs]
            kt = kt_ref[hs, :]
            vx = jnp.concatenate([v_ref[:, hs], ones_w], axis=1)
            a = (jnp.where(causal, jnp.exp(c4[h:h + 1, :] - m_w), 0.0) * _dot(q, kt)).astype(BF16)
            cx_prev = cx_refs[st][...]
            qc = _dot(q, cx_prev.astype(BF16))
            av = _dot(a, vx)
            sc_w = jnp.exp(m_prev4[h:h + 1, :] - m_w)
            den = sc_w * qc[:, ML_DK:] + av[:, ML_DK:]
            inv = 1.0 / jnp.maximum(jnp.abs(den), jnp.exp(-(b_w + m_w)))
            for t in range(ML_DK // W):
                ts = slice(t * W, (t + 1) * W)
                h_ref[:, h * ML_DK + t * W:h * ML_DK + (t + 1) * W] = (
                    (sc_w * qc[:, ts] + av[:, ts]) * inv).astype(h_ref.dtype)

            kw = (kt.astype(F32) * w4[h:h + 1, :]).astype(BF16)
            dec = jnp.concatenate([decay4[h:h + 1, :]] * (ML_DK // W + 1), axis=1)
            cx_refs[st][...] = dec * cx_prev + _dot(kw, vx)


def _mlstm(u, kt, gates_t, gate_b, n_batch):
    n_rows = u.shape[0]
    L = ML_CHUNK
    lat_chunks = SEQ // L
    ctx_chunks = CTX_LEN // L
    n_steps = ctx_chunks + lat_chunks
    ctx_base = n_batch * lat_chunks

    def rowblk(d):
        def f(b, s):
            in_ctx = s < ctx_chunks
            if d == 0:
                c = jnp.where(in_ctx, s, s - ctx_chunks)
            else:
                c = jnp.where(in_ctx, ctx_chunks - 1 - s, lat_chunks - 1 - (s - ctx_chunks))
            return jnp.where(in_ctx, ctx_base + ctx_chunks * b, lat_chunks * b) + c
        return f

    def dir_specs(d):
        rb = rowblk(d)
        return [
            pl.BlockSpec((L, D_MODEL), lambda b, s: (rb(b, s), U_QML)),
            pl.BlockSpec((D_MODEL, L), lambda b, s: (0, rb(b, s))),
            pl.BlockSpec((L, D_MODEL), lambda b, s: (rb(b, s), U_VML)),
            pl.BlockSpec((N_GATE, L), lambda b, s: (0, rb(b, s))),
        ]

    def out_spec(d):
        rb = rowblk(d)
        return pl.BlockSpec((L, D_MODEL), lambda b, s: (rb(b, s), 0))

    return pl.pallas_call(
        _mlstm_kernel,
        grid=(n_batch, n_steps),
        in_specs=dir_specs(0) + dir_specs(1) + [pl.BlockSpec((N_GATE, 1), lambda b, s: (0, 0))],
        out_specs=[out_spec(0), out_spec(1)],
        out_shape=[jax.ShapeDtypeStruct((n_rows, D_MODEL), BF16)] * 2,
        scratch_shapes=(
            [pltpu.VMEM((ML_DK, ML_DK + 128), F32)] * (2 * ML_HEADS)
            + [pltpu.VMEM((ML_HEADS, 128), F32)] * 2),
        compiler_params=_cparams(("arbitrary", "arbitrary")),
        name="mlstm_scan",
    )(u, kt, u, gates_t, u, kt, u, gates_t, gate_b.reshape(N_GATE, 1))


def _attn_lambda(lam_ref, lam_init):
    lq = lam_ref[...]
    return (jnp.exp(jnp.sum(lq[0:1] * lq[1:2], axis=1, keepdims=True))
            - jnp.exp(jnp.sum(lq[2:3] * lq[3:4], axis=1, keepdims=True)) + lam_init)


def _attn_queries(q_ref):
    q = q_ref[...]
    lane = lax.broadcasted_iota(jnp.int32, (1, DA_DV), 1)
    zero = jnp.zeros_like(q)
    is_map0 = (lane % DA_DH) < (DA_DH // 2)
    q2 = jnp.concatenate([jnp.where(is_map0, q, zero), jnp.where(is_map0, zero, q)], axis=0)
    return q2 * (DA_DH ** -0.5)


def _attn_values(s_chunks, m_chunks, vx, lam, g, lam_init, tq):
    accs = [_dot(jnp.exp(s - m).astype(BF16), vx) for s, m in zip(s_chunks, m_chunks)]
    acc = jnp.concatenate(accs, axis=0)
    o0 = acc[0:tq, 0:DA_DV] * (1.0 / acc[0:tq, DA_DV:DA_DV + 1])
    o1 = acc[tq:, 0:DA_DV] * (1.0 / acc[tq:, DA_DV:DA_DV + 1])
    return _rms(o0 - lam * o1, g) * (1.0 - lam_init)


def _attn_kernel(q_ref, kl_ref, vl_ref, kc_ref, vc_ref, lam_ref, g_ref, o_ref, kx_ref, vx_ref, s_ref, m_ref,
                 *, lam_init):
    qi = pl.program_id(2)
    tq = q_ref.shape[0]
    rows = 2 * tq // ATTN_ROW_CHUNKS

    @pl.when(qi == 0)
    def _():
        kx_ref[0:SEQ, :] = kl_ref[...]
        kx_ref[SEQ:, :] = kc_ref[...]
        vx_ref[0:SEQ, 0:DA_DV] = vl_ref[...]
        vx_ref[SEQ:, 0:DA_DV] = vc_ref[...]
        lane_v = lax.broadcasted_iota(jnp.int32, (SEQ + CTX_LEN, DA_DV), 1)
        vx_ref[:, DA_DV:] = jnp.where(lane_v == 0, 1.0, 0.0).astype(BF16)

    def score(slot):
        s = _dot_nt(_attn_queries(q_ref), kx_ref[...])
        m_ref[slot] = jnp.max(s, axis=1, keepdims=True)
        s_ref[slot] = s

    def drain(slot):
        s_chunks = [s_ref[slot, c * rows:(c + 1) * rows, :] for c in range(ATTN_ROW_CHUNKS)]
        m_chunks = [m_ref[slot, c * rows:(c + 1) * rows, :] for c in range(ATTN_ROW_CHUNKS)]
        o = _attn_values(s_chunks, m_chunks, vx_ref[...], _attn_lambda(lam_ref, lam_init), g_ref[...], lam_init, tq)
        o_ref[...] = o.astype(o_ref.dtype)

    @pl.when(qi == 0)
    def _():
        score(0)

    for parity in (0, 1):
        @pl.when(jnp.logical_and(qi > 0, qi % 2 == parity))
        def _():
            drain(1 - parity)
            score(parity)


def _attn_ctx_kernel(q_ref, kc_ref, vc_ref, lam_ref, g_ref, yin_ref, o_ref, *, lam_init):
    del yin_ref
    tq = q_ref.shape[0]
    s = _dot_nt(_attn_queries(q_ref), kc_ref[...])
    lane_v = lax.broadcasted_iota(jnp.int32, (CTX_LEN, DA_DV), 1)
    vx = jnp.concatenate([vc_ref[...], jnp.where(lane_v == 0, 1.0, 0.0).astype(BF16)], axis=1)
    o = _attn_values([s], [jnp.max(s, axis=1, keepdims=True)], vx, _attn_lambda(lam_ref, lam_init),
                     g_ref[...], lam_init, tq)
    o_ref[...] = o.astype(o_ref.dtype)


def _attn(u, da_lam, head_g, lam_init, need_ctx, n_batch):
    n_rows = u.shape[0]
    tq = ATTN_TQ
    lat_qblocks = SEQ // tq
    ctx_base = n_batch * (SEQ // CTX_LEN)
    cpb = D_MODEL // DA_DV
    n_keys = SEQ + CTX_LEN

    kern = functools.partial(_attn_kernel, lam_init=lam_init)
    yd = pl.pallas_call(
        kern,
        grid=(n_batch, DA_HEADS, lat_qblocks + 1),
        in_specs=[
            pl.BlockSpec((tq, DA_DV),
                         lambda b, h, qi: (b * lat_qblocks + jnp.minimum(qi, lat_qblocks - 1), U_QDA * cpb + h)),
            pl.BlockSpec((SEQ, DA_DV), lambda b, h, qi: (b, U_KDA * cpb + h)),
            pl.BlockSpec((SEQ, DA_DV), lambda b, h, qi: (b, U_VDA * cpb + h)),
            pl.BlockSpec((CTX_LEN, DA_DV), lambda b, h, qi: (ctx_base + b, U_KDA * cpb + h)),
            pl.BlockSpec((CTX_LEN, DA_DV), lambda b, h, qi: (ctx_base + b, U_VDA * cpb + h)),
            pl.BlockSpec((4, DA_DH), lambda b, h, qi: (0, 0)),
            pl.BlockSpec((1, DA_DV), lambda b, h, qi: (0, h)),
        ],
        out_specs=pl.BlockSpec((tq, DA_DV), lambda b, h, qi: (b * lat_qblocks + jnp.maximum(qi - 1, 0), h)),
        out_shape=jax.ShapeDtypeStruct((n_rows, D_MODEL), BF16),
        scratch_shapes=[
            pltpu.VMEM((n_keys, DA_DV), BF16),
            pltpu.VMEM((n_keys, 2 * DA_DV), BF16),
            pltpu.VMEM((2, 2 * tq, n_keys), F32),
            pltpu.VMEM((2, 2 * tq, 1), F32),
        ],
        compiler_params=_cparams(("arbitrary", "arbitrary", "arbitrary")),
        name="diff_attn",
    )(u, u, u, u, u, da_lam, head_g)
    if not need_ctx:
        return yd
    kern_ctx = functools.partial(_attn_ctx_kernel, lam_init=lam_init)
    return pl.pallas_call(
        kern_ctx,
        grid=(n_batch, DA_HEADS),
        in_specs=[
            pl.BlockSpec((CTX_LEN, DA_DV), lambda b, h: (ctx_base + b, U_QDA * cpb + h)),
            pl.BlockSpec((CTX_LEN, DA_DV), lambda b, h: (ctx_base + b, U_KDA * cpb + h)),
            pl.BlockSpec((CTX_LEN, DA_DV), lambda b, h: (ctx_base + b, U_VDA * cpb + h)),
            pl.BlockSpec((4, DA_DH), lambda b, h: (0, 0)),
            pl.BlockSpec((1, DA_DV), lambda b, h: (0, h)),
            pl.BlockSpec(memory_space=pl.ANY),
        ],
        out_specs=pl.BlockSpec((CTX_LEN, DA_DV), lambda b, h: (ctx_base + b, h)),
        out_shape=jax.ShapeDtypeStruct((n_rows, D_MODEL), BF16),
        input_output_aliases={5: 0},
        compiler_params=_cparams(("arbitrary", "arbitrary")),
        name="diff_attn_ctx",
    )(u, u, u, da_lam, head_g, yd)


def _dft_tables():
    R = FFT_R

    def cs(num, period):
        ang = (num % period).astype(F32) * (2.0 * math.pi / period)
        return jnp.cos(ang), jnp.sin(ang)

    idx = jnp.arange(R, dtype=jnp.int32)
    c1, s1 = cs(idx[:, None] * idx[None, :], R)
    a1 = jnp.concatenate([c1, -s1], axis=0).astype(BF16)
    f2 = idx[:, None, None]
    f1 = idx[None, :, None]
    t1 = idx[None, None, :]
    mc, ms = cs(t1 * (R * f1 + f2), SEQ)
    b2 = jnp.concatenate([jnp.concatenate([mc, ms], axis=2),
                          jnp.concatenate([-ms, mc], axis=2)], axis=1).astype(BF16)
    ch = jnp.arange(FN_GC, dtype=jnp.int32)
    cc, sc = cs(ch[:, None] * ch[None, :], FN_GC)
    csm = jnp.concatenate([cc, sc], axis=0).astype(BF16)
    actx = jnp.concatenate([cc, -sc], axis=0).astype(BF16)
    return a1, b2, csm, actx


def _fft1_kernel(a_ref, z_ref, p_ref):
    p_ref[0, 0] = _dot(a_ref[...], z_ref[0]).astype(p_ref.dtype)


def _fft1(fn_view, a1, n_batch):
    R = FFT_R
    lanes = R * FN_GC
    lc = lanes
    return pl.pallas_call(
        _fft1_kernel,
        grid=(FN_GROUPS, n_batch, lanes // lc),
        in_specs=[
            pl.BlockSpec((2 * R, R), lambda g, b, c: (0, 0)),
            pl.BlockSpec((1, R, lc), lambda g, b, c: (g, b, c)),
        ],
        out_specs=pl.BlockSpec((1, 1, 2 * R, lc), lambda g, b, c: (g, b, 0, c)),
        out_shape=jax.ShapeDtypeStruct((FN_GROUPS, n_batch, 2 * R, lanes), BF16),
        compiler_params=_cparams(("arbitrary", "arbitrary", "arbitrary")),
        name="fourier_stage1",
    )(a1, fn_view)


def _fft2_kernel(p_ref, b_ref, cs_ref, o_ref, *, f2b):
    R = FFT_R
    cc = cs_ref[0:FN_GC, :]
    sc = cs_ref[FN_GC:2 * FN_GC, :]
    norm = 1.0 / math.sqrt(SEQ * FN_GC)
    for g in range(FN_GROUPS):
        xr, xi = [], []
        for jj in range(f2b):
            stacked = jnp.concatenate([p_ref[g, 0, 0, jj], p_ref[g, 0, 1, jj]], axis=0)
            x = _dot(b_ref[jj], stacked)
            xr.append(x[0:R])
            xi.append(x[R:2 * R])
        y = (_dot(jnp.concatenate(xr, axis=0).astype(BF16), cc)
             + _dot(jnp.concatenate(xi, axis=0).astype(BF16), sc)) * norm
        for jj in range(f2b):
            lo = jj * D_MODEL + g * FN_GC
            o_ref[:, lo:lo + FN_GC] = y[jj * R:(jj + 1) * R].astype(o_ref.dtype)


def _fft2(p6, b2, csm, n_rows, n_batch):
    R = FFT_R
    f2b = 8
    kern = functools.partial(_fft2_kernel, f2b=f2b)
    return pl.pallas_call(
        kern,
        grid=(n_batch, R // f2b),
        in_specs=[
            pl.BlockSpec((FN_GROUPS, 1, 2, f2b, R, FN_GC), lambda b, f: (0, b, 0, f, 0, 0)),
            pl.BlockSpec((f2b, 2 * R, 2 * R), lambda b, f: (f, 0, 0)),
            pl.BlockSpec((2 * FN_GC, FN_GC), lambda b, f: (0, 0)),
        ],
        out_specs=pl.BlockSpec((R, f2b * D_MODEL), lambda b, f: (b, f)),
        out_shape=jax.ShapeDtypeStruct((n_rows // R, R * D_MODEL), BF16),
        compiler_params=_cparams(("arbitrary", "arbitrary")),
        name="fourier_stage2",
    )(p6, b2, csm)


def _fft_ctx_kernel(z_ref, a_ref, cs_ref, yin_ref, o_ref):
    del yin_ref
    cc = cs_ref[0:FN_GC, :]
    sc = cs_ref[FN_GC:2 * FN_GC, :]
    norm = 1.0 / math.sqrt(CTX_LEN * FN_GC)
    for g in range(FN_GROUPS):
        p = _dot(a_ref[...], z_ref[g])
        y = _dot(p[0:CTX_LEN].astype(BF16), cc) + _dot(p[CTX_LEN:2 * CTX_LEN].astype(BF16), sc)
        o_ref[:, g * FN_GC:(g + 1) * FN_GC] = (y * norm).astype(o_ref.dtype)


def _fft_ctx(fn, actx, csm, yf, n_batch):
    n_rows = yf.shape[0]
    ctx_base = n_batch * (SEQ // CTX_LEN)
    return pl.pallas_call(
        _fft_ctx_kernel,
        grid=(n_batch,),
        in_specs=[
            pl.BlockSpec((FN_GROUPS, CTX_LEN, FN_GC), lambda b: (0, ctx_base + b, 0)),
            pl.BlockSpec((2 * CTX_LEN, CTX_LEN), lambda b: (0, 0)),
            pl.BlockSpec((2 * FN_GC, FN_GC), lambda b: (0, 0)),
            pl.BlockSpec(memory_space=pl.ANY),
        ],
        out_specs=pl.BlockSpec((CTX_LEN, D_MODEL), lambda b: (ctx_base + b, 0)),
        out_shape=jax.ShapeDtypeStruct((n_rows, D_MODEL), BF16),
        input_output_aliases={3: 0},
        compiler_params=_cparams(("arbitrary",)),
        name="fourier_ctx",
    )(fn, actx, csm, yf)


def _fourier(fn, tables, need_ctx, n_batch):
    a1, b2, csm, actx = tables
    n_rows = fn.shape[1]
    R = FFT_R
    p = _fft1(fn.reshape(FN_GROUPS, n_rows // R, R * FN_GC), a1, n_batch)
    yf = _fft2(p.reshape(FN_GROUPS, n_batch, 2, R, R, FN_GC), b2, csm, n_rows, n_batch)
    yf = yf.reshape(n_rows, D_MODEL)
    if need_ctx:
        yf = _fft_ctx(fn, actx, csm, yf, n_batch)
    return yf


def _merge_kernel(hf_ref, hb_ref, o_ref, gm_ref, gd_ref, gf_ref, yd_ref, yf_ref, x_ref, mod_ref, hg_ref,
                  wml_ref, wda_ref, wfn_ref, wout_ref, out_ref, *, rows_per_batch, n_batch):
    i = pl.program_id(0)
    hsum = hf_ref[...].astype(F32) + hb_ref[...].astype(F32)
    hg = hg_ref[...]
    parts = []
    for h in range(ML_HEADS):
        hs = slice(h * ML_DK, (h + 1) * ML_DK)
        parts.append(_rms(hsum[:, hs], hg[:, hs]))
    ym = (jnp.concatenate(parts, axis=1) * jax.nn.sigmoid(o_ref[...].astype(F32))).astype(BF16)
    y = (jax.nn.sigmoid(gm_ref[...].astype(F32)) * _dot(ym, wml_ref[...])
         + jax.nn.sigmoid(gd_ref[...].astype(F32)) * _dot(yd_ref[...], wda_ref[...])
         + jax.nn.sigmoid(gf_ref[...].astype(F32)) * _dot(yf_ref[...], wfn_ref[...]))
    gate = _mod_row(mod_ref, i, rows_per_batch, n_batch, 2)
    out_ref[...] = x_ref[...] + gate * _dot(y.astype(BF16), wout_ref[...])


def _merge(hf, hb, u, yd, yf, x, mods, head_g, wml, wda, wfn, wout, layer, need_ctx, n_batch):
    n_rows = x.shape[0]
    tm = _row_tile(n_batch)
    rows_per_batch = SEQ // tm
    ni = (n_rows if need_ctx else n_batch * SEQ) // tm
    kern = functools.partial(_merge_kernel, rows_per_batch=rows_per_batch, n_batch=n_batch)
    row = lambda i: (i, 0)
    full = lambda i: (0, 0)
    wspec = pl.BlockSpec((None, D_MODEL, D_MODEL), lambda i: (layer, 0, 0))
    return pl.pallas_call(
        kern,
        grid=(ni,),
        in_specs=[
            pl.BlockSpec((tm, D_MODEL), row),
            pl.BlockSpec((tm, D_MODEL), row),
            pl.BlockSpec((tm, D_MODEL), lambda i: (i, U_OML)),
            pl.BlockSpec((tm, D_MODEL), lambda i: (i, U_GPRE)),
            pl.BlockSpec((tm, D_MODEL), lambda i: (i, U_GPRE + 1)),
            pl.BlockSpec((tm, D_MODEL), lambda i: (i, U_GPRE + 2)),
            pl.BlockSpec((tm, D_MODEL), row),
            pl.BlockSpec((tm, D_MODEL), row),
            pl.BlockSpec((tm, D_MODEL), row),
            pl.BlockSpec((8, N_MOD), full),
            pl.BlockSpec((1, D_MODEL), full),
            wspec, wspec, wspec, wspec,
        ],
        out_specs=pl.BlockSpec((tm, D_MODEL), row),
        out_shape=jax.ShapeDtypeStruct((n_rows, D_MODEL), F32),
        compiler_params=_cparams(("arbitrary",)),
        name="merge_out_proj",
    )(hf, hb, u, u, u, u, yd, yf, x, mods, head_g, wml, wda, wfn, wout)


FFN_CHUNKS = ((0, 1024), (1024, 1024), (2048, 768))


def _ffn_kernel(x_ref, mod_ref, g_ref, win_ref, wout_ref, fg_ref, out_ref, *, rows_per_batch, n_batch, final):
    i = pl.program_id(0)
    x = x_ref[...]
    shift = _mod_row(mod_ref, i, rows_per_batch, n_batch, 3)
    scale = _mod_row(mod_ref, i, rows_per_batch, n_batch, 4)
    gate = _mod_row(mod_ref, i, rows_per_batch, n_batch, 5)
    h = (_rms(x, g_ref[...]) * (1.0 + scale) + shift).astype(BF16)
    acc = None
    for lo, width in FFN_CHUNKS:
        a = _dot(h, win_ref[:, lo:lo + width])
        b = _dot(h, win_ref[:, D_FF + lo:D_FF + lo + width])
        act = (a * jax.nn.sigmoid(a) * b).astype(BF16)
        part = _dot(act, wout_ref[lo:lo + width, :])
        acc = part if acc is None else acc + part
    xn = x + gate * acc
    out_ref[...] = _rms(xn, fg_ref[...]) if final else xn


def _ffn(x, mods, g, w_in, w_out, layer, final_g, final, n_rows_out, n_batch):
    tm = _row_tile(n_batch)
    rows_per_batch = SEQ // tm
    kern = functools.partial(_ffn_kernel, rows_per_batch=rows_per_batch, n_batch=n_batch, final=final)
    row = lambda i: (i, 0)
    full = lambda i: (0, 0)
    return pl.pallas_call(
        kern,
        grid=(n_rows_out // tm,),
        in_specs=[
            pl.BlockSpec((tm, D_MODEL), row),
            pl.BlockSpec((8, N_MOD), full),
            pl.BlockSpec((1, D_MODEL), full),
            pl.BlockSpec((None, D_MODEL, 2 * D_FF), lambda i: (layer, 0, 0)),
            pl.BlockSpec((None, D_FF, D_MODEL), lambda i: (layer, 0, 0)),
            pl.BlockSpec((1, D_MODEL), full),
        ],
        out_specs=pl.BlockSpec((tm, D_MODEL), row),
        out_shape=jax.ShapeDtypeStruct((n_rows_out, D_MODEL), F32),
        compiler_params=_cparams(("arbitrary",)),
        name="swiglu_ffn",
    )(x, mods, g, w_in, w_out, final_g)


def _da_col_perm(w):
    half = DA_DH // 2
    return w.reshape(-1, DA_HEADS, 2, 2, half).transpose(0, 1, 3, 2, 4).reshape(-1, DA_HEADS * DA_DV)


def _rope_tables(pad):
    n_freq = DA_DH // 4
    rows = SEQ // GRID_W
    inv = ROPE_BASE ** (-jnp.arange(n_freq, dtype=F32) / n_freq)
    r = jnp.repeat(jnp.arange(rows, dtype=F32), GRID_W)
    col = jnp.tile(jnp.arange(GRID_W, dtype=F32), rows)
    ang = jnp.concatenate([r[:, None] * inv, col[:, None] * inv], axis=-1)
    cos, sin = jnp.cos(ang), jnp.sin(ang)
    cos_t = jnp.concatenate([cos, cos, cos, cos], axis=-1)
    sin_t = jnp.concatenate([-sin, -sin, sin, sin], axis=-1)
    cos_t = jnp.concatenate([cos_t, jnp.ones((pad, 128), F32)], axis=0)
    sin_t = jnp.concatenate([sin_t, jnp.zeros((pad, 128), F32)], axis=0)
    return cos_t, sin_t


def kernel(x, c, ctx, c_ctx, w_ada, b_ada, norm_g, w_in, ml_gate_b, ml_head_g, da_lam, da_head_g,
           w_br_ml, w_br_da, w_br_fn, w_out, w_ffn_in, w_ffn_out, final_g):
    n_batch = x.shape[0]
    n_lat = n_batch * SEQ
    xs = jnp.concatenate([x.reshape(n_lat, D_MODEL), ctx.reshape(n_batch * CTX_LEN, D_MODEL)], axis=0)
    cc = jnp.concatenate([c, c_ctx[None, :], jnp.zeros((8 - n_batch - 1, D_MODEL), F32)], axis=0)
    mods = _mods(cc, w_ada, b_ada)
    cos_t, sin_t = _rope_tables(n_batch * CTX_LEN)
    tables = _dft_tables()
    gate_lo = 4 * D_MODEL
    fn_lo = gate_lo + N_GATE + 3 * D_MODEL
    final_g2 = final_g.reshape(1, D_MODEL)
    wb_ml, wb_da, wb_fn, wb_out = _to_bf16(w_br_ml), _to_bf16(w_br_da), _to_bf16(w_br_fn), _to_bf16(w_out)
    wb_ffn_in, wb_ffn_out = _to_bf16(w_ffn_in), _to_bf16(w_ffn_out)

    for l in range(DEPTH):
        need_ctx = l < DEPTH - 1
        lam_init = 0.8 - 0.6 * math.exp(-0.3 * l)
        wl = w_in[l]
        da_lo = gate_lo + N_GATE
        w_main = jnp.concatenate([wl[:, :D_MODEL], wl[:, 2 * D_MODEL:gate_lo],
                                  _da_col_perm(wl[:, da_lo:da_lo + D_MODEL]),
                                  _da_col_perm(wl[:, da_lo + D_MODEL:da_lo + 2 * D_MODEL]),
                                  wl[:, da_lo + 2 * D_MODEL:fn_lo],
                                  wl[:, fn_lo + D_MODEL:], wl[:, fn_lo:fn_lo + D_MODEL]], axis=1).astype(BF16)
        w_kt = wl[:, D_MODEL:2 * D_MODEL].T.astype(BF16)
        w_gate_t = wl[:, gate_lo:gate_lo + N_GATE].T.astype(BF16)
        u, kt, fn, gates_t = _inproj(xs, mods[l], norm_g[l, 0].reshape(1, D_MODEL), w_main, w_kt, w_gate_t,
                                     cos_t, sin_t, n_batch)
        hf, hb = _mlstm(u, kt, gates_t, ml_gate_b[l], n_batch)
        yd = _attn(u, da_lam[l], da_head_g[l].reshape(1, D_MODEL), lam_init, need_ctx, n_batch)
        yf = _fourier(fn, tables, need_ctx, n_batch)
        xs = _merge(hf, hb, u, yd, yf, xs, mods[l], ml_head_g[l].reshape(1, D_MODEL),
                    wb_ml, wb_da, wb_fn, wb_out, l, need_ctx, n_batch)
        final = l == DEPTH - 1
        n_out = n_lat if final else xs.shape[0]
        xs = _ffn(xs, mods[l], norm_g[l, 1].reshape(1, D_MODEL), wb_ffn_in, wb_ffn_out, l,
                  final_g2, final, n_out, n_batch)
    return xs.reshape(n_batch, SEQ, D_MODEL)
```

```python
import functools
import math

import jax
import jax.numpy as jnp
from jax import lax
from jax.experimental import pallas as pl
from jax.experimental.pallas import tpu as pltpu

D_MODEL = 1024
SEQ = 4096
DEPTH = 4
CTX_LEN = 256
GRID_W = 64
NORM_EPS = 1e-6

ML_HEADS = 4
ML_DK = 256
ML_CHUNK = 128

DA_HEADS = 8
DA_DH = 64
DA_DV = 2 * DA_DH
ROPE_BASE = 10000.0
ATTN_TQ = 512
ATTN_ROW_CHUNKS = 4

FN_GROUPS = 4
FN_GC = 256
FFT_R = 64

D_FF = 2816
N_GATE = 4 * ML_HEADS
N_MOD = 6 * D_MODEL

U_QML, U_VML, U_OML, U_QDA, U_KDA, U_VDA, U_GPRE = 0, 1, 2, 3, 4, 5, 6
U_BLOCKS = 9
W_BLOCKS = U_BLOCKS + 1
STEP_KT = 1
N_COL_STEPS = W_BLOCKS + 1

VMEM_LIMIT_V7X = 56 * 1024 * 1024

BF16 = jnp.bfloat16
F32 = jnp.float32


def _cparams(sem):
    return pltpu.CompilerParams(dimension_semantics=sem, vmem_limit_bytes=VMEM_LIMIT_V7X)


def _dot(a, b):
    return jnp.dot(a, b, preferred_element_type=F32)


def _dot_nt(a, b):
    return lax.dot_general(a, b, (((1,), (1,)), ((), ())), preferred_element_type=F32)


def _mod_row(mod_ref, i, rows_per_batch, n_batch, col):
    r = jnp.minimum(i // rows_per_batch, n_batch)
    return mod_ref[pl.ds(r, 1), col * D_MODEL:(col + 1) * D_MODEL]


def _row_tile(n_batch):
    return min(512, n_batch * CTX_LEN)


def _rms(x, g):
    return x * lax.rsqrt(jnp.mean(x * x, axis=-1, keepdims=True) + NORM_EPS) * g


def _cast_kernel(w_ref, o_ref):
    o_ref[...] = w_ref[...].astype(o_ref.dtype)


def _to_bf16(w):
    n_l, rows, cols = w.shape
    tr = 256
    return pl.pallas_call(
        _cast_kernel,
        grid=(n_l, rows // tr),
        in_specs=[pl.BlockSpec((1, tr, cols), lambda l, i: (l, i, 0))],
        out_specs=pl.BlockSpec((1, tr, cols), lambda l, i: (l, i, 0)),
        out_shape=jax.ShapeDtypeStruct(w.shape, BF16),
        compiler_params=_cparams(("arbitrary", "arbitrary")),
        name="cast_bf16",
    )(w)


def _mods_kernel(c_ref, w_ref, b_ref, o_ref):
    c = c_ref[...]
    s = (c * jax.nn.sigmoid(c)).astype(BF16)
    o_ref[0] = _dot(s, w_ref[0].astype(BF16)) + b_ref[0]


def _mods(cc, w_ada, b_ada):
    tn = 1536
    return pl.pallas_call(
        _mods_kernel,
        grid=(DEPTH, N_MOD // tn),
        in_specs=[
            pl.BlockSpec((8, D_MODEL), lambda l, j: (0, 0)),
            pl.BlockSpec((1, D_MODEL, tn), lambda l, j: (l, 0, j)),
            pl.BlockSpec((1, 1, tn), lambda l, j: (l, 0, j)),
        ],
        out_specs=pl.BlockSpec((1, 8, tn), lambda l, j: (l, 0, j)),
        out_shape=jax.ShapeDtypeStruct((DEPTH, 8, N_MOD), F32),
        compiler_params=_cparams(("arbitrary", "arbitrary")),
        name="adaln_mods",
    )(cc, w_ada, b_ada.reshape(DEPTH, 1, N_MOD))


def _inproj_kernel(xl_ref, xc_ref, mod_ref, g_ref, w_ref, wkt_ref, wgt_ref, cos_ref, sin_ref,
                   u_ref, kt_ref, fn_ref, gate_ref, xn_ref, *, rows_per_batch, n_batch):
    i = pl.program_id(0)
    j = pl.program_id(1)

    @pl.when(j == 0)
    def _():
        x = jnp.where(i < rows_per_batch * n_batch, xl_ref[...], xc_ref[...])
        y = _rms(x, g_ref[...])
        shift = _mod_row(mod_ref, i, rows_per_batch, n_batch, 0)
        scale = _mod_row(mod_ref, i, rows_per_batch, n_batch, 1)
        xn_ref[...] = (y * (1.0 + scale) + shift).astype(BF16)
        gate_ref[...] = _dot_nt(wgt_ref[...], xn_ref[...])

    def product():
        return _dot(xn_ref[...], w_ref[...])

    blk = jnp.where(j == 0, 0, j - 1)
    is_rope = jnp.logical_or(blk == U_QDA, blk == U_KDA)
    is_plain = jnp.logical_and(j != STEP_KT, jnp.logical_and(jnp.logical_not(is_rope), blk < U_BLOCKS))

    @pl.when(is_plain)
    def _():
        u_ref[...] = product().astype(BF16)

    @pl.when(j == STEP_KT)
    def _():
        kt_ref[...] = (_dot_nt(wkt_ref[...], xn_ref[...]) * (ML_DK ** -0.5)).astype(BF16)

    @pl.when(is_rope)
    def _():
        acc = product()
        cos = cos_ref[...]
        sin = sin_ref[...]
        for t in range(acc.shape[1] // DA_DV):
            sl = slice(t * DA_DV, (t + 1) * DA_DV)
            x = acc[:, sl]
            u_ref[:, sl] = (x * cos + pltpu.roll(x, DA_DV // 2, 1) * sin).astype(BF16)

    @pl.when(blk == U_BLOCKS)
    def _():
        acc = product()
        for g in range(FN_GROUPS):
            fn_ref[g] = acc[:, g * FN_GC:(g + 1) * FN_GC].astype(BF16)


def _inproj(x_lat, x_ctx, ctx_blk, mods, g, w_main, w_kt, w_gate_t, layer, cos_t, sin_t, n_batch):
    tm = n_batch * CTX_LEN
    assert SEQ % tm == 0 and cos_t.shape[0] == SEQ + tm
    n_rows = n_batch * (SEQ + CTX_LEN)
    ni = n_rows // tm
    rows_per_batch = SEQ // tm
    lat_blocks = n_batch * rows_per_batch

    def tab_idx(i, j):
        return (jnp.where(i < lat_blocks, i % rows_per_batch, rows_per_batch), 0)

    def w_blk(j):
        return jnp.where(j == 0, 0, j - 1)

    kern = functools.partial(_inproj_kernel, rows_per_batch=rows_per_batch, n_batch=n_batch)
    return pl.pallas_call(
        kern,
        grid=(ni, N_COL_STEPS),
        in_specs=[
            pl.BlockSpec((tm, D_MODEL), lambda i, j: (jnp.minimum(i, lat_blocks - 1), 0)),
            pl.BlockSpec((tm, D_MODEL), lambda i, j: (ctx_blk, 0)),
            pl.BlockSpec((None, 8, N_MOD), lambda i, j: (layer, 0, 0)),
            pl.BlockSpec((1, D_MODEL), lambda i, j: (0, 0)),
            pl.BlockSpec((None, D_MODEL, D_MODEL), lambda i, j: (layer, 0, w_blk(j))),
            pl.BlockSpec((None, D_MODEL, D_MODEL), lambda i, j: (layer, 0, 0)),
            pl.BlockSpec((None, N_GATE, D_MODEL), lambda i, j: (layer, 0, 0)),
            pl.BlockSpec((tm, 128), tab_idx),
            pl.BlockSpec((tm, 128), tab_idx),
        ],
        out_specs=[
            pl.BlockSpec((tm, D_MODEL), lambda i, j: (i, jnp.minimum(w_blk(j), U_BLOCKS - 1))),
            pl.BlockSpec((D_MODEL, tm), lambda i, j: (0, i)),
            pl.BlockSpec((FN_GROUPS, tm, FN_GC), lambda i, j: (0, i, 0)),
            pl.BlockSpec((N_GATE, tm), lambda i, j: (0, i)),
        ],
        out_shape=[
            jax.ShapeDtypeStruct((n_rows, U_BLOCKS * D_MODEL), BF16),
            jax.ShapeDtypeStruct((D_MODEL, n_rows), BF16),
            jax.ShapeDtypeStruct((FN_GROUPS, n_rows, FN_GC), BF16),
            jax.ShapeDtypeStruct((N_GATE, n_rows), F32),
        ],
        scratch_shapes=[pltpu.VMEM((tm, D_MODEL), BF16)],
        compiler_params=_cparams(("arbitrary", "arbitrary")),
        name="in_proj",
    )(x_lat, x_ctx, mods, g, w_main, w_kt, w_gate_t, cos_t, sin_t)


def _split3(x):
    hi = x.astype(BF16).astype(F32)
    mid = (x - hi).astype(BF16).astype(F32)
    lo = (x - hi - mid).astype(BF16).astype(F32)
    return hi, mid, lo


def _mlstm_kernel(qf_ref, ktf_ref, vf_ref, gtf_ref, qb_ref, ktb_ref, vb_ref, gtb_ref, bias_ref,
                  hf_ref, hb_ref, *state_refs):
    cx_refs, m_refs = state_refs[:2 * ML_HEADS], state_refs[2 * ML_HEADS:]
    s = pl.program_id(1)
    L = ML_CHUNK
    H = ML_HEADS
    W = 128

    @pl.when(s == 0)
    def _():
        for ref in state_refs:
            ref[...] = jnp.zeros_like(ref)

    t_idx = lax.broadcasted_iota(jnp.int32, (L, L), 0)
    s_idx = lax.broadcasted_iota(jnp.int32, (L, L), 1)
    eye = t_idx == s_idx
    sub8 = lax.broadcasted_iota(jnp.int32, (8, W), 0)
    ones_w = jnp.ones((L, W), BF16)
    er = lax.broadcasted_iota(jnp.int32, (4 * L, 2 * W), 0)
    ec = lax.broadcasted_iota(jnp.int32, (4 * L, 2 * W), 1)
    expand = jnp.where((er < 3 * L) == (ec < W), 1.0, 0.0).astype(BF16)

    def running_max_rows(x, d):
        n_tiles = L // 8
        out = [None] * n_tiles
        carry = None
        for j in (range(n_tiles) if d == 0 else range(n_tiles - 1, -1, -1)):
            r = x[8 * j:8 * (j + 1)]
            k = 1
            while k < 8:
                if d == 0:
                    r = jnp.maximum(r, jnp.where(sub8 >= k, pltpu.roll(r, k, 0), -jnp.inf))
                else:
                    r = jnp.maximum(r, jnp.where(sub8 < 8 - k, pltpu.roll(r, 8 - k, 0), -jnp.inf))
                k *= 2
            if carry is not None:
                r = jnp.maximum(r, carry)
            carry = jnp.broadcast_to(r[7:8] if d == 0 else r[0:1], (8, W))
            out[j] = r
        return jnp.concatenate(out, axis=0)

    def prelude(d, gt_ref):
        before = (t_idx <= s_idx) if d == 0 else (t_idx >= s_idx)
        gt = gt_ref[2 * H * d:2 * H * (d + 1), :] + bias_ref[2 * H * d:2 * H * (d + 1), :]
        i4, f4 = gt[0:H], gt[H:2 * H]
        lf4 = jnp.minimum(f4, 0.0) - jnp.log1p(jnp.exp(-jnp.abs(f4)))
        lf_terms = jnp.concatenate(list(_split3(lf4)) + [jnp.zeros((H, L), F32)], axis=0).astype(BF16)
        cum_rhs = jnp.concatenate([jnp.where(before, 1.0, 0.0).astype(BF16), ones_w], axis=1)
        r = _dot(lf_terms, cum_rhs)
        bx = r[0:H] + r[H:2 * H] + r[2 * H:3 * H]
        b4, bend4 = bx[:, 0:L], bx[:, L:L + W]
        c4 = i4 - b4
        m_prev4 = m_refs[d][...]
        g4 = bend4 - b4 + i4
        m_new4 = jnp.maximum(bend4 + m_prev4, jnp.max(g4, axis=1, keepdims=True))
        m_refs[d][...] = m_new4
        return dict(c4=c4, col_terms=_split3(b4) + (c4.astype(BF16).astype(F32),), m_prev4=m_prev4,
                    decay4=jnp.exp(bend4 + m_prev4 - m_new4), w4=jnp.exp(g4 - m_new4))

    dirs = ((0, qf_ref, ktf_ref, vf_ref, gtf_ref, hf_ref), (1, qb_ref, ktb_ref, vb_ref, gtb_ref, hb_ref))
    pre = [prelude(d, gt_ref) for d, _, _, _, gt_ref, _ in dirs]
    for d, q_ref, kt_ref, v_ref, _, h_ref in dirs:
        causal = (s_idx <= t_idx) if d == 0 else (s_idx >= t_idx)
        c4, m_prev4, decay4, w4 = pre[d]["c4"], pre[d]["m_prev4"], pre[d]["decay4"], pre[d]["w4"]
        for h in range(H):
            st = d * H + h
            hs = slice(h * ML_DK, (h + 1) * ML_DK)
            diag = jnp.concatenate([jnp.where(eye, x[h:h + 1, :], 0.0) for x in pre[d]["col_terms"]],
                                   axis=1).astype(BF16)
            wide = _dot(diag, expand)
            b_w = wide[:, 0:W]
            m_w = jnp.maximum(running_max_rows(wide[:, W:2 * W], d), m_prev4[h:h + 1, :])

            q = q_ref[:, hs]
            kt = kt_ref[hs, :]
            vx = jnp.concatenate([v_ref[:, hs], ones_w], axis=1)
            a = (jnp.where(causal, jnp.exp(c4[h:h + 1, :] - m_w), 0.0) * _dot(q, kt)).astype(BF16)
            cx_prev = cx_refs[st][...]
            qc = _dot(q, cx_prev.astype(BF16))
            av = _dot(a, vx)
            sc_w = jnp.exp(m_prev4[h:h + 1, :] - m_w)
            den = sc_w * qc[:, ML_DK:] + av[:, ML_DK:]
            inv = 1.0 / jnp.maximum(jnp.abs(den), jnp.exp(-(b_w + m_w)))
            for t in range(ML_DK // W):
                ts = slice(t * W, (t + 1) * W)
                h_ref[:, h * ML_DK + t * W:h * ML_DK + (t + 1) * W] = (
                    (sc_w * qc[:, ts] + av[:, ts]) * inv).astype(h_ref.dtype)

            kw = (kt.astype(F32) * w4[h:h + 1, :]).astype(BF16)
            dec = jnp.concatenate([decay4[h:h + 1, :]] * (ML_DK // W + 1), axis=1)
            cx_refs[st][...] = dec * cx_prev + _dot(kw, vx)


def _mlstm(u, kt, gates_t, gate_b, n_batch):
    n_rows = u.shape[0]
    L = ML_CHUNK
    lat_chunks = SEQ // L
    ctx_chunks = CTX_LEN // L
    n_steps = ctx_chunks + lat_chunks
    ctx_base = n_batch * lat_chunks

    def rowblk(d):
        def f(b, s):
            in_ctx = s < ctx_chunks
            if d == 0:
                c = jnp.where(in_ctx, s, s - ctx_chunks)
            else:
                c = jnp.where(in_ctx, ctx_chunks - 1 - s, lat_chunks - 1 - (s - ctx_chunks))
            return jnp.where(in_ctx, ctx_base + ctx_chunks * b, lat_chunks * b) + c
        return f

    def dir_specs(d):
        rb = rowblk(d)
        return [
            pl.BlockSpec((L, D_MODEL), lambda b, s: (rb(b, s), U_QML)),
            pl.BlockSpec((D_MODEL, L), lambda b, s: (0, rb(b, s))),
            pl.BlockSpec((L, D_MODEL), lambda b, s: (rb(b, s), U_VML)),
            pl.BlockSpec((N_GATE, L), lambda b, s: (0, rb(b, s))),
        ]

    def out_spec(d):
        rb = rowblk(d)
        return pl.BlockSpec((L, D_MODEL), lambda b, s: (rb(b, s), 0))

    return pl.pallas_call(
        _mlstm_kernel,
        grid=(n_batch, n_steps),
        in_specs=dir_specs(0) + dir_specs(1) + [pl.BlockSpec((N_GATE, 1), lambda b, s: (0, 0))],
        out_specs=[out_spec(0), out_spec(1)],
        out_shape=[jax.ShapeDtypeStruct((n_rows, D_MODEL), BF16)] * 2,
        scratch_shapes=(
            [pltpu.VMEM((ML_DK, ML_DK + 128), F32)] * (2 * ML_HEADS)
            + [pltpu.VMEM((ML_HEADS, 128), F32)] * 2),
        compiler_params=_cparams(("arbitrary", "arbitrary")),
        name="mlstm_scan",
    )(u, kt, u, gates_t, u, kt, u, gates_t, gate_b.reshape(N_GATE, 1))


def _attn_lambda(lam_ref, lam_init):
    lq = lam_ref[...]
    return (jnp.exp(jnp.sum(lq[0:1] * lq[1:2], axis=1, keepdims=True))
            - jnp.exp(jnp.sum(lq[2:3] * lq[3:4], axis=1, keepdims=True)) + lam_init)


def _attn_queries(q_ref):
    q = q_ref[...]
    lane = lax.broadcasted_iota(jnp.int32, (1, DA_DV), 1)
    zero = jnp.zeros_like(q)
    is_map0 = (lane % DA_DH) < (DA_DH // 2)
    q2 = jnp.concatenate([jnp.where(is_map0, q, zero), jnp.where(is_map0, zero, q)], axis=0)
    return q2 * (DA_DH ** -0.5)


def _attn_values(s_chunks, m_chunks, vx, lam, g, lam_init, tq):
    accs = [_dot(jnp.exp(s - m).astype(BF16), vx) for s, m in zip(s_chunks, m_chunks)]
    acc = jnp.concatenate(accs, axis=0)
    o0 = acc[0:tq, 0:DA_DV] * (1.0 / acc[0:tq, DA_DV:DA_DV + 1])
    o1 = acc[tq:, 0:DA_DV] * (1.0 / acc[tq:, DA_DV:DA_DV + 1])
    return _rms(o0 - lam * o1, g) * (1.0 - lam_init)


def _attn_kernel(q_ref, kl_ref, vl_ref, kc_ref, vc_ref, lam_ref, g_ref, o_ref, vx_ref, s_ref, m_ref,
                 *, lam_init):
    qi = pl.program_id(2)
    tq = q_ref.shape[0]
    rows = 2 * tq // ATTN_ROW_CHUNKS

    @pl.when(qi == 0)
    def _():
        vx_ref[0:SEQ, 0:DA_DV] = vl_ref[...]
        vx_ref[SEQ:, 0:DA_DV] = vc_ref[...]
        lane_v = lax.broadcasted_iota(jnp.int32, (SEQ + CTX_LEN, DA_DV), 1)
        vx_ref[:, DA_DV:] = jnp.where(lane_v == 0, 1.0, 0.0).astype(BF16)

    def score(slot):
        q2 = _attn_queries(q_ref)
        s_lat = _dot_nt(q2, kl_ref[...])
        s_ctx = _dot_nt(q2, kc_ref[...])
        m_ref[slot] = jnp.maximum(jnp.max(s_lat, axis=1, keepdims=True), jnp.max(s_ctx, axis=1, keepdims=True))
        s_ref[slot, :, 0:SEQ] = s_lat
        s_ref[slot, :, SEQ:] = s_ctx

    def drain(slot):
        s_chunks = [s_ref[slot, c * rows:(c + 1) * rows, :] for c in range(ATTN_ROW_CHUNKS)]
        m_chunks = [m_ref[slot, c * rows:(c + 1) * rows, :] for c in range(ATTN_ROW_CHUNKS)]
        o = _attn_values(s_chunks, m_chunks, vx_ref[...], _attn_lambda(lam_ref, lam_init), g_ref[...], lam_init, tq)
        o_ref[...] = o.astype(o_ref.dtype)

    @pl.when(qi == 0)
    def _():
        score(0)

    for parity in (0, 1):
        @pl.when(jnp.logical_and(qi > 0, qi % 2 == parity))
        def _():
            drain(1 - parity)
            score(parity)


def _attn_ctx_kernel(q_ref, kc_ref, vc_ref, lam_ref, g_ref, yin_ref, o_ref, *, lam_init):
    del yin_ref
    tq = q_ref.shape[0]
    s = _dot_nt(_attn_queries(q_ref), kc_ref[...])
    lane_v = lax.broadcasted_iota(jnp.int32, (CTX_LEN, DA_DV), 1)
    vx = jnp.concatenate([vc_ref[...], jnp.where(lane_v == 0, 1.0, 0.0).astype(BF16)], axis=1)
    o = _attn_values([s], [jnp.max(s, axis=1, keepdims=True)], vx, _attn_lambda(lam_ref, lam_init),
                     g_ref[...], lam_init, tq)
    o_ref[...] = o.astype(o_ref.dtype)


def _attn(u, da_lam, head_g, lam_init, need_ctx, n_batch):
    n_rows = u.shape[0]
    tq = ATTN_TQ
    lat_qblocks = SEQ // tq
    ctx_base = n_batch * (SEQ // CTX_LEN)
    cpb = D_MODEL // DA_DV
    n_keys = SEQ + CTX_LEN

    kern = functools.partial(_attn_kernel, lam_init=lam_init)
    yd = pl.pallas_call(
        kern,
        grid=(n_batch, DA_HEADS, lat_qblocks + 1),
        in_specs=[
            pl.BlockSpec((tq, DA_DV),
                         lambda b, h, qi: (b * lat_qblocks + jnp.minimum(qi, lat_qblocks - 1), U_QDA * cpb + h)),
            pl.BlockSpec((SEQ, DA_DV), lambda b, h, qi: (b, U_KDA * cpb + h)),
            pl.BlockSpec((SEQ, DA_DV), lambda b, h, qi: (b, U_VDA * cpb + h)),
            pl.BlockSpec((CTX_LEN, DA_DV), lambda b, h, qi: (ctx_base + b, U_KDA * cpb + h)),
            pl.BlockSpec((CTX_LEN, DA_DV), lambda b, h, qi: (ctx_base + b, U_VDA * cpb + h)),
            pl.BlockSpec((4, DA_DH), lambda b, h, qi: (0, 0)),
            pl.BlockSpec((1, DA_DV), lambda b, h, qi: (0, h)),
        ],
        out_specs=pl.BlockSpec((tq, DA_DV), lambda b, h, qi: (b * lat_qblocks + jnp.maximum(qi - 1, 0), h)),
        out_shape=jax.ShapeDtypeStruct((n_rows, D_MODEL), BF16),
        scratch_shapes=[
            pltpu.VMEM((n_keys, 2 * DA_DV), BF16),
            pltpu.VMEM((2, 2 * tq, n_keys), F32),
            pltpu.VMEM((2, 2 * tq, 1), F32),
        ],
        compiler_params=_cparams(("arbitrary", "arbitrary", "arbitrary")),
        name="diff_attn",
    )(u, u, u, u, u, da_lam, head_g)
    if not need_ctx:
        return yd
    kern_ctx = functools.partial(_attn_ctx_kernel, lam_init=lam_init)
    return pl.pallas_call(
        kern_ctx,
        grid=(n_batch, DA_HEADS),
        in_specs=[
            pl.BlockSpec((CTX_LEN, DA_DV), lambda b, h: (ctx_base + b, U_QDA * cpb + h)),
            pl.BlockSpec((CTX_LEN, DA_DV), lambda b, h: (ctx_base + b, U_KDA * cpb + h)),
            pl.BlockSpec((CTX_LEN, DA_DV), lambda b, h: (ctx_base + b, U_VDA * cpb + h)),
            pl.BlockSpec((4, DA_DH), lambda b, h: (0, 0)),
            pl.BlockSpec((1, DA_DV), lambda b, h: (0, h)),
            pl.BlockSpec(memory_space=pl.ANY),
        ],
        out_specs=pl.BlockSpec((CTX_LEN, DA_DV), lambda b, h: (ctx_base + b, h)),
        out_shape=jax.ShapeDtypeStruct((n_rows, D_MODEL), BF16),
        input_output_aliases={5: 0},
        compiler_params=_cparams(("arbitrary", "arbitrary")),
        name="diff_attn_ctx",
    )(u, u, u, da_lam, head_g, yd)


def _dft_tables():
    R = FFT_R

    def cs(num, period):
        ang = (num % period).astype(F32) * (2.0 * math.pi / period)
        return jnp.cos(ang), jnp.sin(ang)

    idx = jnp.arange(R, dtype=jnp.int32)
    c1, s1 = cs(idx[:, None] * idx[None, :], R)
    a1 = jnp.concatenate([c1, -s1], axis=0).astype(BF16)
    f2 = idx[:, None, None]
    f1 = idx[None, :, None]
    t1 = idx[None, None, :]
    mc, ms = cs(t1 * (R * f1 + f2), SEQ)
    b2 = jnp.concatenate([jnp.concatenate([mc, ms], axis=2),
                          jnp.concatenate([-ms, mc], axis=2)], axis=1).astype(BF16)
    ch = jnp.arange(FN_GC, dtype=jnp.int32)
    cc, sc = cs(ch[:, None] * ch[None, :], FN_GC)
    csm = jnp.concatenate([cc, sc], axis=0).astype(BF16)
    actx = jnp.concatenate([cc, -sc], axis=0).astype(BF16)
    return a1, b2, csm, actx


def _fft1_kernel(a_ref, z_ref, p_ref):
    p_ref[0, 0] = _dot(a_ref[...], z_ref[0]).astype(p_ref.dtype)


def _fft1(fn_view, a1, n_batch):
    R = FFT_R
    lanes = R * FN_GC
    lc = lanes
    return pl.pallas_call(
        _fft1_kernel,
        grid=(FN_GROUPS, n_batch, lanes // lc),
        in_specs=[
            pl.BlockSpec((2 * R, R), lambda g, b, c: (0, 0)),
            pl.BlockSpec((1, R, lc), lambda g, b, c: (g, b, c)),
        ],
        out_specs=pl.BlockSpec((1, 1, 2 * R, lc), lambda g, b, c: (g, b, 0, c)),
        out_shape=jax.ShapeDtypeStruct((FN_GROUPS, n_batch, 2 * R, lanes), BF16),
        compiler_params=_cparams(("arbitrary", "arbitrary", "arbitrary")),
        name="fourier_stage1",
    )(a1, fn_view)


def _fft2_kernel(p_ref, b_ref, cs_ref, o_ref, *, f2b):
    R = FFT_R
    cc = cs_ref[0:FN_GC, :]
    sc = cs_ref[FN_GC:2 * FN_GC, :]
    norm = 1.0 / math.sqrt(SEQ * FN_GC)
    for g in range(FN_GROUPS):
        xr, xi = [], []
        for jj in range(f2b):
            stacked = jnp.concatenate([p_ref[g, 0, 0, jj], p_ref[g, 0, 1, jj]], axis=0)
            x = _dot(b_ref[jj], stacked)
            xr.append(x[0:R])
            xi.append(x[R:2 * R])
        y = (_dot(jnp.concatenate(xr, axis=0).astype(BF16), cc)
             + _dot(jnp.concatenate(xi, axis=0).astype(BF16), sc)) * norm
        for jj in range(f2b):
            lo = jj * D_MODEL + g * FN_GC
            o_ref[:, lo:lo + FN_GC] = y[jj * R:(jj + 1) * R].astype(o_ref.dtype)


def _fft2(p6, b2, csm, n_rows, n_batch):
    R = FFT_R
    f2b = 8
    kern = functools.partial(_fft2_kernel, f2b=f2b)
    return pl.pallas_call(
        kern,
        grid=(n_batch, R // f2b),
        in_specs=[
            pl.BlockSpec((FN_GROUPS, 1, 2, f2b, R, FN_GC), lambda b, f: (0, b, 0, f, 0, 0)),
            pl.BlockSpec((f2b, 2 * R, 2 * R), lambda b, f: (f, 0, 0)),
            pl.BlockSpec((2 * FN_GC, FN_GC), lambda b, f: (0, 0)),
        ],
        out_specs=pl.BlockSpec((R, f2b * D_MODEL), lambda b, f: (b, f)),
        out_shape=jax.ShapeDtypeStruct((n_rows // R, R * D_MODEL), BF16),
        compiler_params=_cparams(("arbitrary", "arbitrary")),
        name="fourier_stage2",
    )(p6, b2, csm)


def _fft_ctx_kernel(z_ref, a_ref, cs_ref, yin_ref, o_ref):
    del yin_ref
    cc = cs_ref[0:FN_GC, :]
    sc = cs_ref[FN_GC:2 * FN_GC, :]
    norm = 1.0 / math.sqrt(CTX_LEN * FN_GC)
    for g in range(FN_GROUPS):
        p = _dot(a_ref[...], z_ref[g])
        y = _dot(p[0:CTX_LEN].astype(BF16), cc) + _dot(p[CTX_LEN:2 * CTX_LEN].astype(BF16), sc)
        o_ref[:, g * FN_GC:(g + 1) * FN_GC] = (y * norm).astype(o_ref.dtype)


def _fft_ctx(fn, actx, csm, yf, n_batch):
    n_rows = yf.shape[0]
    ctx_base = n_batch * (SEQ // CTX_LEN)
    return pl.pallas_call(
        _fft_ctx_kernel,
        grid=(n_batch,),
        in_specs=[
            pl.BlockSpec((FN_GROUPS, CTX_LEN, FN_GC), lambda b: (0, ctx_base + b, 0)),
            pl.BlockSpec((2 * CTX_LEN, CTX_LEN), lambda b: (0, 0)),
            pl.BlockSpec((2 * FN_GC, FN_GC), lambda b: (0, 0)),
            pl.BlockSpec(memory_space=pl.ANY),
        ],
        out_specs=pl.BlockSpec((CTX_LEN, D_MODEL), lambda b: (ctx_base + b, 0)),
        out_shape=jax.ShapeDtypeStruct((n_rows, D_MODEL), BF16),
        input_output_aliases={3: 0},
        compiler_params=_cparams(("arbitrary",)),
        name="fourier_ctx",
    )(fn, actx, csm, yf)


def _fourier(fn, tables, need_ctx, n_batch):
    a1, b2, csm, actx = tables
    n_rows = fn.shape[1]
    R = FFT_R
    p = _fft1(fn.reshape(FN_GROUPS, n_rows // R, R * FN_GC), a1, n_batch)
    yf = _fft2(p.reshape(FN_GROUPS, n_batch, 2, R, R, FN_GC), b2, csm, n_rows, n_batch)
    yf = yf.reshape(n_rows, D_MODEL)
    if need_ctx:
        yf = _fft_ctx(fn, actx, csm, yf, n_batch)
    return yf


def _merge_kernel(hf_ref, hb_ref, o_ref, gm_ref, gd_ref, gf_ref, yd_ref, yf_ref, xl_ref, xc_ref, mod_ref, hg_ref,
                  wml_ref, wda_ref, wfn_ref, wout_ref, out_ref, *, rows_per_batch, n_batch):
    i = pl.program_id(0)
    x = jnp.where(i < rows_per_batch * n_batch, xl_ref[...], xc_ref[...])
    hsum = hf_ref[...].astype(F32) + hb_ref[...].astype(F32)
    hg = hg_ref[...]
    parts = []
    for h in range(ML_HEADS):
        hs = slice(h * ML_DK, (h + 1) * ML_DK)
        parts.append(_rms(hsum[:, hs], hg[:, hs]))
    ym = (jnp.concatenate(parts, axis=1) * jax.nn.sigmoid(o_ref[...].astype(F32))).astype(BF16)
    y = (jax.nn.sigmoid(gm_ref[...].astype(F32)) * _dot(ym, wml_ref[...])
         + jax.nn.sigmoid(gd_ref[...].astype(F32)) * _dot(yd_ref[...], wda_ref[...])
         + jax.nn.sigmoid(gf_ref[...].astype(F32)) * _dot(yf_ref[...], wfn_ref[...]))
    gate = _mod_row(mod_ref, i, rows_per_batch, n_batch, 2)
    out_ref[...] = x + gate * _dot(y.astype(BF16), wout_ref[...])


def _merge(hf, hb, u, yd, yf, x_lat, x_ctx, ctx_blk0, mods, head_g, wml, wda, wfn, wout, layer, need_ctx, n_batch):
    n_rows = n_batch * (SEQ + CTX_LEN)
    tm = _row_tile(n_batch)
    rows_per_batch = SEQ // tm
    lat_blocks = n_batch * rows_per_batch
    ni = (n_rows if need_ctx else n_batch * SEQ) // tm
    kern = functools.partial(_merge_kernel, rows_per_batch=rows_per_batch, n_batch=n_batch)
    row = lambda i: (i, 0)
    full = lambda i: (0, 0)
    wspec = pl.BlockSpec((None, D_MODEL, D_MODEL), lambda i: (layer, 0, 0))
    return pl.pallas_call(
        kern,
        grid=(ni,),
        in_specs=[
            pl.BlockSpec((tm, D_MODEL), row),
            pl.BlockSpec((tm, D_MODEL), row),
            pl.BlockSpec((tm, D_MODEL), lambda i: (i, U_OML)),
            pl.BlockSpec((tm, D_MODEL), lambda i: (i, U_GPRE)),
            pl.BlockSpec((tm, D_MODEL), lambda i: (i, U_GPRE + 1)),
            pl.BlockSpec((tm, D_MODEL), lambda i: (i, U_GPRE + 2)),
            pl.BlockSpec((tm, D_MODEL), row),
            pl.BlockSpec((tm, D_MODEL), row),
            pl.BlockSpec((tm, D_MODEL), lambda i: (jnp.minimum(i, lat_blocks - 1), 0)),
            pl.BlockSpec((tm, D_MODEL), lambda i: (ctx_blk0 + jnp.maximum(i - lat_blocks, 0), 0)),
            pl.BlockSpec((None, 8, N_MOD), lambda i: (layer, 0, 0)),
            pl.BlockSpec((1, D_MODEL), full),
            wspec, wspec, wspec, wspec,
        ],
        out_specs=pl.BlockSpec((tm, D_MODEL), row),
        out_shape=jax.ShapeDtypeStruct((n_rows, D_MODEL), F32),
        compiler_params=_cparams(("arbitrary",)),
        name="merge_out_proj",
    )(hf, hb, u, u, u, u, yd, yf, x_lat, x_ctx, mods, head_g, wml, wda, wfn, wout)


FFN_CHUNKS = ((0, 1024), (1024, 1024), (2048, 768))


def _ffn_kernel(x_ref, mod_ref, g_ref, win_ref, wout_ref, fg_ref, out_ref, *, rows_per_batch, n_batch, final):
    i = pl.program_id(0)
    x = x_ref[...]
    shift = _mod_row(mod_ref, i, rows_per_batch, n_batch, 3)
    scale = _mod_row(mod_ref, i, rows_per_batch, n_batch, 4)
    gate = _mod_row(mod_ref, i, rows_per_batch, n_batch, 5)
    h = (_rms(x, g_ref[...]) * (1.0 + scale) + shift).astype(BF16)
    acc = None
    for lo, width in FFN_CHUNKS:
        a = _dot(h, win_ref[:, lo:lo + width])
        b = _dot(h, win_ref[:, D_FF + lo:D_FF + lo + width])
        act = (a * jax.nn.sigmoid(a) * b).astype(BF16)
        part = _dot(act, wout_ref[lo:lo + width, :])
        acc = part if acc is None else acc + part
    xn = x + gate * acc
    out_ref[...] = _rms(xn, fg_ref[...]) if final else xn


def _ffn(x, mods, g, w_in, w_out, layer, final_g, final, n_rows_out, n_batch):
    tm = _row_tile(n_batch)
    rows_per_batch = SEQ // tm
    kern = functools.partial(_ffn_kernel, rows_per_batch=rows_per_batch, n_batch=n_batch, final=final)
    row = lambda i: (i, 0)
    full = lambda i: (0, 0)
    return pl.pallas_call(
        kern,
        grid=(n_rows_out // tm,),
        in_specs=[
            pl.BlockSpec((tm, D_MODEL), row),
            pl.BlockSpec((None, 8, N_MOD), lambda i: (layer, 0, 0)),
            pl.BlockSpec((1, D_MODEL), full),
            pl.BlockSpec((None, D_MODEL, 2 * D_FF), lambda i: (layer, 0, 0)),
            pl.BlockSpec((None, D_FF, D_MODEL), lambda i: (layer, 0, 0)),
            pl.BlockSpec((1, D_MODEL), full),
        ],
        out_specs=pl.BlockSpec((tm, D_MODEL), row),
        out_shape=jax.ShapeDtypeStruct((n_rows_out, D_MODEL), F32),
        compiler_params=_cparams(("arbitrary",)),
        name="swiglu_ffn",
    )(x, mods, g, w_in, w_out, final_g)


def _da_col_perm(w):
    half = DA_DH // 2
    lead = w.shape[:-1]
    return jnp.swapaxes(w.reshape(lead + (DA_HEADS, 2, 2, half)), -3, -2).reshape(lead + (DA_HEADS * DA_DV,))


def _rope_tables(pad):
    n_freq = DA_DH // 4
    rows = SEQ // GRID_W
    inv = ROPE_BASE ** (-jnp.arange(n_freq, dtype=F32) / n_freq)
    r = jnp.repeat(jnp.arange(rows, dtype=F32), GRID_W)
    col = jnp.tile(jnp.arange(GRID_W, dtype=F32), rows)
    ang = jnp.concatenate([r[:, None] * inv, col[:, None] * inv], axis=-1)
    cos, sin = jnp.cos(ang), jnp.sin(ang)
    cos_t = jnp.concatenate([cos, cos, cos, cos], axis=-1)
    sin_t = jnp.concatenate([-sin, -sin, sin, sin], axis=-1)
    cos_t = jnp.concatenate([cos_t, jnp.ones((pad, 128), F32)], axis=0)
    sin_t = jnp.concatenate([sin_t, jnp.zeros((pad, 128), F32)], axis=0)
    return cos_t, sin_t


def kernel(x, c, ctx, c_ctx, w_ada, b_ada, norm_g, w_in, ml_gate_b, ml_head_g, da_lam, da_head_g,
           w_br_ml, w_br_da, w_br_fn, w_out, w_ffn_in, w_ffn_out, final_g):
    n_batch = x.shape[0]
    n_lat = n_batch * SEQ
    x_lat = x.reshape(n_lat, D_MODEL)
    x_ctx = ctx.reshape(n_batch * CTX_LEN, D_MODEL)
    cc = jnp.concatenate([c, c_ctx[None, :], jnp.zeros((8 - n_batch - 1, D_MODEL), F32)], axis=0)
    mods = _mods(cc, w_ada, b_ada)
    cos_t, sin_t = _rope_tables(n_batch * CTX_LEN)
    tables = _dft_tables()
    final_g2 = final_g.reshape(1, D_MODEL)
    wb_ml, wb_da, wb_fn, wb_out = _to_bf16(w_br_ml), _to_bf16(w_br_da), _to_bf16(w_br_fn), _to_bf16(w_out)
    wb_ffn_in, wb_ffn_out = _to_bf16(w_ffn_in), _to_bf16(w_ffn_out)

    gate_lo = 4 * D_MODEL
    da_lo = gate_lo + N_GATE
    fn_lo = da_lo + 3 * D_MODEL
    w_main = jnp.concatenate([w_in[..., :D_MODEL], w_in[..., 2 * D_MODEL:gate_lo],
                              _da_col_perm(w_in[..., da_lo:da_lo + D_MODEL]),
                              _da_col_perm(w_in[..., da_lo + D_MODEL:da_lo + 2 * D_MODEL]),
                              w_in[..., da_lo + 2 * D_MODEL:fn_lo],
                              w_in[..., fn_lo + D_MODEL:], w_in[..., fn_lo:fn_lo + D_MODEL]], axis=-1).astype(BF16)
    w_kt = jnp.swapaxes(w_in[..., D_MODEL:2 * D_MODEL], 1, 2).astype(BF16)
    w_gate_t = jnp.swapaxes(w_in[..., gate_lo:da_lo], 1, 2).astype(BF16)

    tm_in = n_batch * CTX_LEN
    tm_tok = _row_tile(n_batch)
    xs = None
    for l in range(DEPTH):
        need_ctx = l < DEPTH - 1
        lam_init = 0.8 - 0.6 * math.exp(-0.3 * l)
        xl, xc = (x_lat, x_ctx) if xs is None else (xs, xs)
        u, kt, fn, gates_t = _inproj(xl, xc, 0 if xs is None else n_lat // tm_in, mods,
                                     norm_g[l, 0].reshape(1, D_MODEL), w_main, w_kt, w_gate_t, l,
                                     cos_t, sin_t, n_batch)
        hf, hb = _mlstm(u, kt, gates_t, ml_gate_b[l], n_batch)
        yd = _attn(u, da_lam[l], da_head_g[l].reshape(1, D_MODEL), lam_init, need_ctx, n_batch)
        yf = _fourier(fn, tables, need_ctx, n_batch)
        xs = _merge(hf, hb, u, yd, yf, xl, xc, 0 if xs is None else n_lat // tm_tok, mods,
                    ml_head_g[l].reshape(1, D_MODEL), wb_ml, wb_da, wb_fn, wb_out, l, need_ctx, n_batch)
        final = l == DEPTH - 1
        n_out = n_lat if final else xs.shape[0]
        xs = _ffn(xs, mods, norm_g[l, 1].reshape(1, D_MODEL), wb_ffn_in, wb_ffn_out, l,
                  final_g2, final, n_out, n_batch)
    return xs.reshape(n_batch, SEQ, D_MODEL)
```

```python
import functools
import math

import jax
import jax.numpy as jnp
from jax import lax
from jax.experimental import pallas as pl
from jax.experimental.pallas import tpu as pltpu

D_MODEL = 1024
SEQ = 4096
DEPTH = 4
CTX_LEN = 256
GRID_W = 64
NORM_EPS = 1e-6

ML_HEADS = 4
ML_DK = 256
ML_CHUNK = 128

DA_HEADS = 8
DA_DH = 64
DA_DV = 2 * DA_DH
ROPE_BASE = 10000.0
ATTN_TQ = 512
ATTN_ROW_CHUNKS = 4

FN_GROUPS = 4
FN_GC = 256
FFT_R = 64

D_FF = 2816
N_GATE = 4 * ML_HEADS
N_MOD = 6 * D_MODEL

U_QML, U_VML, U_OML, U_QDA, U_KDA, U_VDA, U_GPRE = 0, 1, 2, 3, 4, 5, 6
U_BLOCKS = 9
W_BLOCKS = U_BLOCKS + 1
STEP_KT = 1
N_COL_STEPS = W_BLOCKS + 1

VMEM_LIMIT_V7X = 56 * 1024 * 1024

BF16 = jnp.bfloat16
F32 = jnp.float32


def _cparams(sem):
    return pltpu.CompilerParams(dimension_semantics=sem, vmem_limit_bytes=VMEM_LIMIT_V7X)


def _dot(a, b):
    return jnp.dot(a, b, preferred_element_type=F32)


def _dot_nt(a, b):
    return lax.dot_general(a, b, (((1,), (1,)), ((), ())), preferred_element_type=F32)


def _mod_row(mod_ref, i, rows_per_batch, n_batch, col):
    r = jnp.minimum(i // rows_per_batch, n_batch)
    return mod_ref[pl.ds(r, 1), col * D_MODEL:(col + 1) * D_MODEL]


def _row_tile(n_batch):
    return min(512, n_batch * CTX_LEN)


def _rms(x, g):
    return x * lax.rsqrt(jnp.mean(x * x, axis=-1, keepdims=True) + NORM_EPS) * g


def _cast_kernel(w_ref, o_ref):
    o_ref[...] = w_ref[...].astype(o_ref.dtype)


def _to_bf16(w):
    n_l, rows, cols = w.shape
    tr = 256
    return pl.pallas_call(
        _cast_kernel,
        grid=(n_l, rows // tr),
        in_specs=[pl.BlockSpec((1, tr, cols), lambda l, i: (l, i, 0))],
        out_specs=pl.BlockSpec((1, tr, cols), lambda l, i: (l, i, 0)),
        out_shape=jax.ShapeDtypeStruct(w.shape, BF16),
        compiler_params=_cparams(("arbitrary", "arbitrary")),
        name="cast_bf16",
    )(w)


def _mods_kernel(c_ref, w_ref, b_ref, o_ref):
    c = c_ref[...]
    s = (c * jax.nn.sigmoid(c)).astype(BF16)
    o_ref[0] = _dot(s, w_ref[0].astype(BF16)) + b_ref[0]


def _mods(cc, w_ada, b_ada):
    tn = 1536
    return pl.pallas_call(
        _mods_kernel,
        grid=(DEPTH, N_MOD // tn),
        in_specs=[
            pl.BlockSpec((8, D_MODEL), lambda l, j: (0, 0)),
            pl.BlockSpec((1, D_MODEL, tn), lambda l, j: (l, 0, j)),
            pl.BlockSpec((1, 1, tn), lambda l, j: (l, 0, j)),
        ],
        out_specs=pl.BlockSpec((1, 8, tn), lambda l, j: (l, 0, j)),
        out_shape=jax.ShapeDtypeStruct((DEPTH, 8, N_MOD), F32),
        compiler_params=_cparams(("arbitrary", "arbitrary")),
        name="adaln_mods",
    )(cc, w_ada, b_ada.reshape(DEPTH, 1, N_MOD))


def _inproj_kernel(xl_ref, xc_ref, mod_ref, g_ref, w_ref, wkt_ref, wgt_ref, cos_ref, sin_ref,
                   u_ref, kt_ref, fn_ref, gate_ref, xn_ref, *, rows_per_batch, n_batch):
    i = pl.program_id(0)
    j = pl.program_id(1)

    def normalise(x_ref):
        y = _rms(x_ref[...], g_ref[...])
        shift = _mod_row(mod_ref, i, rows_per_batch, n_batch, 0)
        scale = _mod_row(mod_ref, i, rows_per_batch, n_batch, 1)
        xn_ref[...] = (y * (1.0 + scale) + shift).astype(BF16)
        gate_ref[...] = _dot_nt(wgt_ref[...], xn_ref[...])

    is_lat = i < rows_per_batch * n_batch

    @pl.when(jnp.logical_and(j == 0, is_lat))
    def _():
        normalise(xl_ref)

    @pl.when(jnp.logical_and(j == 0, jnp.logical_not(is_lat)))
    def _():
        normalise(xc_ref)

    def product():
        return _dot(xn_ref[...], w_ref[...])

    blk = jnp.where(j == 0, 0, j - 1)
    is_rope = jnp.logical_or(blk == U_QDA, blk == U_KDA)
    is_plain = jnp.logical_and(j != STEP_KT, jnp.logical_and(jnp.logical_not(is_rope), blk < U_BLOCKS))

    @pl.when(is_plain)
    def _():
        u_ref[...] = product().astype(BF16)

    @pl.when(j == STEP_KT)
    def _():
        kt_ref[...] = (_dot_nt(wkt_ref[...], xn_ref[...]) * (ML_DK ** -0.5)).astype(BF16)

    @pl.when(is_rope)
    def _():
        acc = product()
        cos = cos_ref[...]
        sin = sin_ref[...]
        for t in range(acc.shape[1] // DA_DV):
            sl = slice(t * DA_DV, (t + 1) * DA_DV)
            x = acc[:, sl]
            u_ref[:, sl] = (x * cos + pltpu.roll(x, DA_DV // 2, 1) * sin).astype(BF16)

    @pl.when(blk == U_BLOCKS)
    def _():
        acc = product()
        for g in range(FN_GROUPS):
            fn_ref[g] = acc[:, g * FN_GC:(g + 1) * FN_GC].astype(BF16)


def _inproj(x_lat, x_ctx, ctx_blk, mods, g, w_main, w_kt, w_gate_t, layer, cos_t, sin_t, n_batch):
    tm = n_batch * CTX_LEN
    assert SEQ % tm == 0 and cos_t.shape[0] == SEQ + tm
    n_rows = n_batch * (SEQ + CTX_LEN)
    ni = n_rows // tm
    rows_per_batch = SEQ // tm
    lat_blocks = n_batch * rows_per_batch

    def tab_idx(i, j):
        return (jnp.where(i < lat_blocks, i % rows_per_batch, rows_per_batch), 0)

    def w_blk(j):
        return jnp.where(j == 0, 0, j - 1)

    kern = functools.partial(_inproj_kernel, rows_per_batch=rows_per_batch, n_batch=n_batch)
    return pl.pallas_call(
        kern,
        grid=(ni, N_COL_STEPS),
        in_specs=[
            pl.BlockSpec((tm, D_MODEL), lambda i, j: (jnp.minimum(i, lat_blocks - 1), 0)),
            pl.BlockSpec((tm, D_MODEL), lambda i, j: (ctx_blk, 0)),
            pl.BlockSpec((None, 8, N_MOD), lambda i, j: (layer, 0, 0)),
            pl.BlockSpec((1, D_MODEL), lambda i, j: (0, 0)),
            pl.BlockSpec((None, D_MODEL, D_MODEL), lambda i, j: (layer, 0, w_blk(j))),
            pl.BlockSpec((None, D_MODEL, D_MODEL), lambda i, j: (layer, 0, 0)),
            pl.BlockSpec((None, N_GATE, D_MODEL), lambda i, j: (layer, 0, 0)),
            pl.BlockSpec((tm, 128), tab_idx),
            pl.BlockSpec((tm, 128), tab_idx),
        ],
        out_specs=[
            pl.BlockSpec((tm, D_MODEL), lambda i, j: (i, jnp.minimum(w_blk(j), U_BLOCKS - 1))),
            pl.BlockSpec((D_MODEL, tm), lambda i, j: (0, i)),
            pl.BlockSpec((FN_GROUPS, tm, FN_GC), lambda i, j: (0, i, 0)),
            pl.BlockSpec((N_GATE, tm), lambda i, j: (0, i)),
        ],
        out_shape=[
            jax.ShapeDtypeStruct((n_rows, U_BLOCKS * D_MODEL), BF16),
            jax.ShapeDtypeStruct((D_MODEL, n_rows), BF16),
            jax.ShapeDtypeStruct((FN_GROUPS, n_rows, FN_GC), BF16),
            jax.ShapeDtypeStruct((N_GATE, n_rows), F32),
        ],
        scratch_shapes=[pltpu.VMEM((tm, D_MODEL), BF16)],
        compiler_params=_cparams(("arbitrary", "arbitrary")),
        name="in_proj",
    )(x_lat, x_ctx, mods, g, w_main, w_kt, w_gate_t, cos_t, sin_t)


def _split3(x):
    hi = x.astype(BF16).astype(F32)
    mid = (x - hi).astype(BF16).astype(F32)
    lo = (x - hi - mid).astype(BF16).astype(F32)
    return hi, mid, lo


def _mlstm_kernel(qf_ref, ktf_ref, vf_ref, gtf_ref, gtfn_ref, qb_ref, ktb_ref, vb_ref, gtb_ref, gtbn_ref, bias_ref,
                  hf_ref, hb_ref, *scratch):
    n_st = 2 * ML_HEADS
    cx_refs, bw_refs, pmw_refs = scratch[:n_st], scratch[n_st:2 * n_st], scratch[2 * n_st:3 * n_st]
    m_refs, c_refs, bend_refs, g_refs = (scratch[3 * n_st + 2 * k:3 * n_st + 2 * k + 2] for k in range(4))
    state_refs = cx_refs + m_refs
    s = pl.program_id(1)
    L = ML_CHUNK
    H = ML_HEADS
    W = 128

    @pl.when(s == 0)
    def _():
        for ref in state_refs:
            ref[...] = jnp.zeros_like(ref)

    t_idx = lax.broadcasted_iota(jnp.int32, (L, L), 0)
    s_idx = lax.broadcasted_iota(jnp.int32, (L, L), 1)
    eye = t_idx == s_idx
    sub8 = lax.broadcasted_iota(jnp.int32, (8, W), 0)
    ones_w = jnp.ones((L, W), BF16)
    er = lax.broadcasted_iota(jnp.int32, (4 * L, 2 * W), 0)
    ec = lax.broadcasted_iota(jnp.int32, (4 * L, 2 * W), 1)
    expand = jnp.where((er < 3 * L) == (ec < W), 1.0, 0.0).astype(BF16)

    def running_max_rows(x, d):
        n_tiles = L // 8
        out = [None] * n_tiles
        carry = None
        for j in (range(n_tiles) if d == 0 else range(n_tiles - 1, -1, -1)):
            r = x[8 * j:8 * (j + 1)]
            k = 1
            while k < 8:
                if d == 0:
                    r = jnp.maximum(r, jnp.where(sub8 >= k, pltpu.roll(r, k, 0), -jnp.inf))
                else:
                    r = jnp.maximum(r, jnp.where(sub8 < 8 - k, pltpu.roll(r, 8 - k, 0), -jnp.inf))
                k *= 2
            if carry is not None:
                r = jnp.maximum(r, carry)
            carry = jnp.broadcast_to(r[7:8] if d == 0 else r[0:1], (8, W))
            out[j] = r
        return jnp.concatenate(out, axis=0)

    def gate_part(d, gt_ref):
        before = (t_idx <= s_idx) if d == 0 else (t_idx >= s_idx)
        gt = gt_ref[2 * H * d:2 * H * (d + 1), :] + bias_ref[2 * H * d:2 * H * (d + 1), :]
        i4, f4 = gt[0:H], gt[H:2 * H]
        lf4 = jnp.minimum(f4, 0.0) - jnp.log1p(jnp.exp(-jnp.abs(f4)))
        lf_terms = jnp.concatenate(list(_split3(lf4)) + [jnp.zeros((H, L), F32)], axis=0).astype(BF16)
        cum_rhs = jnp.concatenate([jnp.where(before, 1.0, 0.0).astype(BF16), ones_w], axis=1)
        r = _dot(lf_terms, cum_rhs)
        bx = r[0:H] + r[H:2 * H] + r[2 * H:3 * H]
        b4, bend4 = bx[:, 0:L], bx[:, L:L + W]
        c4 = i4 - b4
        c_refs[d][...] = c4
        bend_refs[d][...] = bend4
        g_refs[d][...] = bend4 - b4 + i4
        col_terms = _split3(b4) + (c4.astype(BF16).astype(F32),)
        for h in range(H):
            diag = jnp.concatenate([jnp.where(eye, x[h:h + 1, :], 0.0) for x in col_terms], axis=1).astype(BF16)
            wide = _dot(diag, expand)
            bw_refs[d * H + h][...] = wide[:, 0:W]
            pmw_refs[d * H + h][...] = running_max_rows(wide[:, W:2 * W], d)

    dirs = ((0, qf_ref, ktf_ref, vf_ref, gtf_ref, gtfn_ref, hf_ref),
            (1, qb_ref, ktb_ref, vb_ref, gtb_ref, gtbn_ref, hb_ref))

    @pl.when(s == 0)
    def _():
        for d, _, _, _, gt_ref, _, _ in dirs:
            gate_part(d, gt_ref)

    for d, q_ref, kt_ref, v_ref, _, _, h_ref in dirs:
        causal = (s_idx <= t_idx) if d == 0 else (s_idx >= t_idx)
        c4, bend4, g4 = c_refs[d][...], bend_refs[d][...], g_refs[d][...]
        m_prev4 = m_refs[d][...]
        m_new4 = jnp.maximum(bend4 + m_prev4, jnp.max(g4, axis=1, keepdims=True))
        m_refs[d][...] = m_new4
        decay4 = jnp.exp(bend4 + m_prev4 - m_new4)
        w4 = jnp.exp(g4 - m_new4)
        for h in range(H):
            st = d * H + h
            hs = slice(h * ML_DK, (h + 1) * ML_DK)
            b_w = bw_refs[st][...]
            m_w = jnp.maximum(pmw_refs[st][...], m_prev4[h:h + 1, :])

            q = q_ref[:, hs]
            kt = kt_ref[hs, :]
            vx = jnp.concatenate([v_ref[:, hs], ones_w], axis=1)
            a = (jnp.where(causal, jnp.exp(c4[h:h + 1, :] - m_w), 0.0) * _dot(q, kt)).astype(BF16)
            cx_prev = cx_refs[st][...]
            qc = _dot(q, cx_prev.astype(BF16))
            av = _dot(a, vx)
            sc_w = jnp.exp(m_prev4[h:h + 1, :] - m_w)
            den = sc_w * qc[:, ML_DK:] + av[:, ML_DK:]
            inv = 1.0 / jnp.maximum(jnp.abs(den), jnp.exp(-(b_w + m_w)))
            for t in range(ML_DK // W):
                ts = slice(t * W, (t + 1) * W)
                h_ref[:, h * ML_DK + t * W:h * ML_DK + (t + 1) * W] = (
                    (sc_w * qc[:, ts] + av[:, ts]) * inv).astype(h_ref.dtype)

            kw = (kt.astype(F32) * w4[h:h + 1, :]).astype(BF16)
            dec = jnp.concatenate([decay4[h:h + 1, :]] * (ML_DK // W + 1), axis=1)
            cx_refs[st][...] = dec * cx_prev + _dot(kw, vx)

    for d, _, _, _, _, gtn_ref, _ in dirs:
        gate_part(d, gtn_ref)


def _mlstm(u, kt, gates_t, gate_b, n_batch):
    n_rows = u.shape[0]
    L = ML_CHUNK
    lat_chunks = SEQ // L
    ctx_chunks = CTX_LEN // L
    n_steps = ctx_chunks + lat_chunks
    ctx_base = n_batch * lat_chunks

    def rowblk(d):
        def f(b, s):
            in_ctx = s < ctx_chunks
            if d == 0:
                c = jnp.where(in_ctx, s, s - ctx_chunks)
            else:
                c = jnp.where(in_ctx, ctx_chunks - 1 - s, lat_chunks - 1 - (s - ctx_chunks))
            return jnp.where(in_ctx, ctx_base + ctx_chunks * b, lat_chunks * b) + c
        return f

    def dir_specs(d):
        rb = rowblk(d)
        return [
            pl.BlockSpec((L, D_MODEL), lambda b, s: (rb(b, s), U_QML)),
            pl.BlockSpec((D_MODEL, L), lambda b, s: (0, rb(b, s))),
            pl.BlockSpec((L, D_MODEL), lambda b, s: (rb(b, s), U_VML)),
            pl.BlockSpec((N_GATE, L), lambda b, s: (0, rb(b, s))),
            pl.BlockSpec((N_GATE, L), lambda b, s: (0, rb(b, jnp.minimum(s + 1, n_steps - 1)))),
        ]

    def out_spec(d):
        rb = rowblk(d)
        return pl.BlockSpec((L, D_MODEL), lambda b, s: (rb(b, s), 0))

    n_st = 2 * ML_HEADS
    return pl.pallas_call(
        _mlstm_kernel,
        grid=(n_batch, n_steps),
        in_specs=dir_specs(0) + dir_specs(1) + [pl.BlockSpec((N_GATE, 1), lambda b, s: (0, 0))],
        out_specs=[out_spec(0), out_spec(1)],
        out_shape=[jax.ShapeDtypeStruct((n_rows, D_MODEL), BF16)] * 2,
        scratch_shapes=(
            [pltpu.VMEM((ML_DK, ML_DK + 128), F32)] * n_st
            + [pltpu.VMEM((L, 128), F32)] * (2 * n_st)
            + [pltpu.VMEM((ML_HEADS, 128), F32)] * 2
            + [pltpu.VMEM((ML_HEADS, L), F32)] * 2
            + [pltpu.VMEM((ML_HEADS, 128), F32)] * 2
            + [pltpu.VMEM((ML_HEADS, L), F32)] * 2),
        compiler_params=_cparams(("arbitrary", "arbitrary")),
        name="mlstm_scan",
    )(u, kt, u, gates_t, gates_t, u, kt, u, gates_t, gates_t, gate_b.reshape(N_GATE, 1))


def _attn_lambda(lam_ref, lam_init):
    lq = lam_ref[...]
    return (jnp.exp(jnp.sum(lq[0:1] * lq[1:2], axis=1, keepdims=True))
            - jnp.exp(jnp.sum(lq[2:3] * lq[3:4], axis=1, keepdims=True)) + lam_init)


def _attn_queries(q_ref):
    q = q_ref[...]
    lane = lax.broadcasted_iota(jnp.int32, (1, DA_DV), 1)
    zero = jnp.zeros_like(q)
    is_map0 = (lane % DA_DH) < (DA_DH // 2)
    q2 = jnp.concatenate([jnp.where(is_map0, q, zero), jnp.where(is_map0, zero, q)], axis=0)
    return q2 * (DA_DH ** -0.5)


def _attn_values(s_chunks, m_chunks, vx, lam, g, lam_init, tq):
    accs = [_dot(jnp.exp(s - m).astype(BF16), vx) for s, m in zip(s_chunks, m_chunks)]
    acc = jnp.concatenate(accs, axis=0)
    o0 = acc[0:tq, 0:DA_DV] * (1.0 / acc[0:tq, DA_DV:DA_DV + 1])
    o1 = acc[tq:, 0:DA_DV] * (1.0 / acc[tq:, DA_DV:DA_DV + 1])
    return _rms(o0 - lam * o1, g) * (1.0 - lam_init)


def _attn_kernel(q_ref, kl_ref, vl_ref, kc_ref, vc_ref, lam_ref, g_ref, o_ref, vx_ref, s_ref, m_ref,
                 *, lam_init):
    qi = pl.program_id(2)
    tq = q_ref.shape[0]
    rows = 2 * tq // ATTN_ROW_CHUNKS

    @pl.when(qi == 0)
    def _():
        vx_ref[0:SEQ, 0:DA_DV] = vl_ref[...]
        vx_ref[SEQ:, 0:DA_DV] = vc_ref[...]
        lane_v = lax.broadcasted_iota(jnp.int32, (SEQ + CTX_LEN, DA_DV), 1)
        vx_ref[:, DA_DV:] = jnp.where(lane_v == 0, 1.0, 0.0).astype(BF16)

    def score(slot):
        q2 = _attn_queries(q_ref)
        s_lat = _dot_nt(q2, kl_ref[...])
        s_ctx = _dot_nt(q2, kc_ref[...])
        m_ref[slot] = jnp.maximum(jnp.max(s_lat, axis=1, keepdims=True), jnp.max(s_ctx, axis=1, keepdims=True))
        s_ref[slot, :, 0:SEQ] = s_lat
        s_ref[slot, :, SEQ:] = s_ctx

    def drain(slot):
        s_chunks = [s_ref[slot, c * rows:(c + 1) * rows, :] for c in range(ATTN_ROW_CHUNKS)]
        m_chunks = [m_ref[slot, c * rows:(c + 1) * rows, :] for c in range(ATTN_ROW_CHUNKS)]
        o = _attn_values(s_chunks, m_chunks, vx_ref[...], _attn_lambda(lam_ref, lam_init), g_ref[...], lam_init, tq)
        o_ref[...] = o.astype(o_ref.dtype)

    @pl.when(qi == 0)
    def _():
        score(0)

    for parity in (0, 1):
        @pl.when(jnp.logical_and(qi > 0, qi % 2 == parity))
        def _():
            drain(1 - parity)
            score(parity)


def _attn_ctx_kernel(q_ref, kc_ref, vc_ref, lam_ref, g_ref, yin_ref, o_ref, *, lam_init):
    del yin_ref
    tq = q_ref.shape[0]
    s = _dot_nt(_attn_queries(q_ref), kc_ref[...])
    lane_v = lax.broadcasted_iota(jnp.int32, (CTX_LEN, DA_DV), 1)
    vx = jnp.concatenate([vc_ref[...], jnp.where(lane_v == 0, 1.0, 0.0).astype(BF16)], axis=1)
    o = _attn_values([s], [jnp.max(s, axis=1, keepdims=True)], vx, _attn_lambda(lam_ref, lam_init),
                     g_ref[...], lam_init, tq)
    o_ref[...] = o.astype(o_ref.dtype)


def _attn(u, da_lam, head_g, lam_init, need_ctx, n_batch):
    n_rows = u.shape[0]
    tq = ATTN_TQ
    lat_qblocks = SEQ // tq
    ctx_base = n_batch * (SEQ // CTX_LEN)
    cpb = D_MODEL // DA_DV
    n_keys = SEQ + CTX_LEN

    kern = functools.partial(_attn_kernel, lam_init=lam_init)
    yd = pl.pallas_call(
        kern,
        grid=(n_batch, DA_HEADS, lat_qblocks + 1),
        in_specs=[
            pl.BlockSpec((tq, DA_DV),
                         lambda b, h, qi: (b * lat_qblocks + jnp.minimum(qi, lat_qblocks - 1), U_QDA * cpb + h)),
            pl.BlockSpec((SEQ, DA_DV), lambda b, h, qi: (b, U_KDA * cpb + h)),
            pl.BlockSpec((SEQ, DA_DV), lambda b, h, qi: (b, U_VDA * cpb + h)),
            pl.BlockSpec((CTX_LEN, DA_DV), lambda b, h, qi: (ctx_base + b, U_KDA * cpb + h)),
            pl.BlockSpec((CTX_LEN, DA_DV), lambda b, h, qi: (ctx_base + b, U_VDA * cpb + h)),
            pl.BlockSpec((4, DA_DH), lambda b, h, qi: (0, 0)),
            pl.BlockSpec((1, DA_DV), lambda b, h, qi: (0, h)),
        ],
        out_specs=pl.BlockSpec((tq, DA_DV), lambda b, h, qi: (b * lat_qblocks + jnp.maximum(qi - 1, 0), h)),
        out_shape=jax.ShapeDtypeStruct((n_rows, D_MODEL), BF16),
        scratch_shapes=[
            pltpu.VMEM((n_keys, 2 * DA_DV), BF16),
            pltpu.VMEM((2, 2 * tq, n_keys), F32),
            pltpu.VMEM((2, 2 * tq, 1), F32),
        ],
        compiler_params=_cparams(("arbitrary", "arbitrary", "arbitrary")),
        name="diff_attn",
    )(u, u, u, u, u, da_lam, head_g)
    if not need_ctx:
        return yd
    kern_ctx = functools.partial(_attn_ctx_kernel, lam_init=lam_init)
    return pl.pallas_call(
        kern_ctx,
        grid=(n_batch, DA_HEADS),
        in_specs=[
            pl.BlockSpec((CTX_LEN, DA_DV), lambda b, h: (ctx_base + b, U_QDA * cpb + h)),
            pl.BlockSpec((CTX_LEN, DA_DV), lambda b, h: (ctx_base + b, U_KDA * cpb + h)),
            pl.BlockSpec((CTX_LEN, DA_DV), lambda b, h: (ctx_base + b, U_VDA * cpb + h)),
            pl.BlockSpec((4, DA_DH), lambda b, h: (0, 0)),
            pl.BlockSpec((1, DA_DV), lambda b, h: (0, h)),
            pl.BlockSpec(memory_space=pl.ANY),
        ],
        out_specs=pl.BlockSpec((CTX_LEN, DA_DV), lambda b, h: (ctx_base + b, h)),
        out_shape=jax.ShapeDtypeStruct((n_rows, D_MODEL), BF16),
        input_output_aliases={5: 0},
        compiler_params=_cparams(("arbitrary", "arbitrary")),
        name="diff_attn_ctx",
    )(u, u, u, da_lam, head_g, yd)


def _dft_tables():
    R = FFT_R

    def cs(num, period):
        ang = (num % period).astype(F32) * (2.0 * math.pi / period)
        return jnp.cos(ang), jnp.sin(ang)

    idx = jnp.arange(R, dtype=jnp.int32)
    c1, s1 = cs(idx[:, None] * idx[None, :], R)
    a1 = jnp.concatenate([c1, -s1], axis=0).astype(BF16)
    f2 = idx[:, None, None]
    f1 = idx[None, :, None]
    t1 = idx[None, None, :]
    mc, ms = cs(t1 * (R * f1 + f2), SEQ)
    b2 = jnp.concatenate([jnp.concatenate([mc, ms], axis=2),
                          jnp.concatenate([-ms, mc], axis=2)], axis=1).astype(BF16)
    ch = jnp.arange(FN_GC, dtype=jnp.int32)
    cc, sc = cs(ch[:, None] * ch[None, :], FN_GC)
    csm = jnp.concatenate([cc, sc], axis=0).astype(BF16)
    actx = jnp.concatenate([cc, -sc], axis=0).astype(BF16)
    return a1, b2, csm, actx


def _fft1_kernel(a_ref, z_ref, p_ref):
    p_ref[0, 0] = _dot(a_ref[...], z_ref[0]).astype(p_ref.dtype)


def _fft1(fn_view, a1, n_batch):
    R = FFT_R
    lanes = R * FN_GC
    lc = lanes
    return pl.pallas_call(
        _fft1_kernel,
        grid=(FN_GROUPS, n_batch, lanes // lc),
        in_specs=[
            pl.BlockSpec((2 * R, R), lambda g, b, c: (0, 0)),
            pl.BlockSpec((1, R, lc), lambda g, b, c: (g, b, c)),
        ],
        out_specs=pl.BlockSpec((1, 1, 2 * R, lc), lambda g, b, c: (g, b, 0, c)),
        out_shape=jax.ShapeDtypeStruct((FN_GROUPS, n_batch, 2 * R, lanes), BF16),
        compiler_params=_cparams(("arbitrary", "arbitrary", "arbitrary")),
        name="fourier_stage1",
    )(a1, fn_view)


def _fft2_kernel(p_ref, b_ref, cs_ref, o_ref, *, f2b):
    R = FFT_R
    cc = cs_ref[0:FN_GC, :]
    sc = cs_ref[FN_GC:2 * FN_GC, :]
    norm = 1.0 / math.sqrt(SEQ * FN_GC)
    for g in range(FN_GROUPS):
        xr, xi = [], []
        for jj in range(f2b):
            stacked = jnp.concatenate([p_ref[g, 0, 0, jj], p_ref[g, 0, 1, jj]], axis=0)
            x = _dot(b_ref[jj], stacked)
            xr.append(x[0:R])
            xi.append(x[R:2 * R])
        y = (_dot(jnp.concatenate(xr, axis=0).astype(BF16), cc)
             + _dot(jnp.concatenate(xi, axis=0).astype(BF16), sc)) * norm
        for jj in range(f2b):
            lo = jj * D_MODEL + g * FN_GC
            o_ref[:, lo:lo + FN_GC] = y[jj * R:(jj + 1) * R].astype(o_ref.dtype)


def _fft2(p6, b2, csm, n_rows, n_batch):
    R = FFT_R
    f2b = 8
    kern = functools.partial(_fft2_kernel, f2b=f2b)
    return pl.pallas_call(
        kern,
        grid=(n_batch, R // f2b),
        in_specs=[
            pl.BlockSpec((FN_GROUPS, 1, 2, f2b, R, FN_GC), lambda b, f: (0, b, 0, f, 0, 0)),
            pl.BlockSpec((f2b, 2 * R, 2 * R), lambda b, f: (f, 0, 0)),
            pl.BlockSpec((2 * FN_GC, FN_GC), lambda b, f: (0, 0)),
        ],
        out_specs=pl.BlockSpec((R, f2b * D_MODEL), lambda b, f: (b, f)),
        out_shape=jax.ShapeDtypeStruct((n_rows // R, R * D_MODEL), BF16),
        compiler_params=_cparams(("arbitrary", "arbitrary")),
        name="fourier_stage2",
    )(p6, b2, csm)


def _fft_ctx_kernel(z_ref, a_ref, cs_ref, yin_ref, o_ref):
    del yin_ref
    cc = cs_ref[0:FN_GC, :]
    sc = cs_ref[FN_GC:2 * FN_GC, :]
    norm = 1.0 / math.sqrt(CTX_LEN * FN_GC)
    for g in range(FN_GROUPS):
        p = _dot(a_ref[...], z_ref[g])
        y = _dot(p[0:CTX_LEN].astype(BF16), cc) + _dot(p[CTX_LEN:2 * CTX_LEN].astype(BF16), sc)
        o_ref[:, g * FN_GC:(g + 1) * FN_GC] = (y * norm).astype(o_ref.dtype)


def _fft_ctx(fn, actx, csm, yf, n_batch):
    n_rows = yf.shape[0]
    ctx_base = n_batch * (SEQ // CTX_LEN)
    return pl.pallas_call(
        _fft_ctx_kernel,
        grid=(n_batch,),
        in_specs=[
            pl.BlockSpec((FN_GROUPS, CTX_LEN, FN_GC), lambda b: (0, ctx_base + b, 0)),
            pl.BlockSpec((2 * CTX_LEN, CTX_LEN), lambda b: (0, 0)),
            pl.BlockSpec((2 * FN_GC, FN_GC), lambda b: (0, 0)),
            pl.BlockSpec(memory_space=pl.ANY),
        ],
        out_specs=pl.BlockSpec((CTX_LEN, D_MODEL), lambda b: (ctx_base + b, 0)),
        out_shape=jax.ShapeDtypeStruct((n_rows, D_MODEL), BF16),
        input_output_aliases={3: 0},
        compiler_params=_cparams(("arbitrary",)),
        name="fourier_ctx",
    )(fn, actx, csm, yf)


def _fourier(fn, tables, need_ctx, n_batch):
    a1, b2, csm, actx = tables
    n_rows = fn.shape[1]
    R = FFT_R
    p = _fft1(fn.reshape(FN_GROUPS, n_rows // R, R * FN_GC), a1, n_batch)
    yf = _fft2(p.reshape(FN_GROUPS, n_batch, 2, R, R, FN_GC), b2, csm, n_rows, n_batch)
    yf = yf.reshape(n_rows, D_MODEL)
    if need_ctx:
        yf = _fft_ctx(fn, actx, csm, yf, n_batch)
    return yf


def _merge_kernel(hf_ref, hb_ref, o_ref, gm_ref, gd_ref, gf_ref, yd_ref, yf_ref, xl_ref, xc_ref, mod_ref, hg_ref,
                  wml_ref, wda_ref, wfn_ref, wout_ref, out_ref, *, rows_per_batch, n_batch):
    i = pl.program_id(0)
    x = jnp.where(i < rows_per_batch * n_batch, xl_ref[...], xc_ref[...])
    hsum = hf_ref[...].astype(F32) + hb_ref[...].astype(F32)
    hg = hg_ref[...]
    parts = []
    for h in range(ML_HEADS):
        hs = slice(h * ML_DK, (h + 1) * ML_DK)
        parts.append(_rms(hsum[:, hs], hg[:, hs]))
    ym = (jnp.concatenate(parts, axis=1) * jax.nn.sigmoid(o_ref[...].astype(F32))).astype(BF16)
    y = (jax.nn.sigmoid(gm_ref[...].astype(F32)) * _dot(ym, wml_ref[...])
         + jax.nn.sigmoid(gd_ref[...].astype(F32)) * _dot(yd_ref[...], wda_ref[...])
         + jax.nn.sigmoid(gf_ref[...].astype(F32)) * _dot(yf_ref[...], wfn_ref[...]))
    gate = _mod_row(mod_ref, i, rows_per_batch, n_batch, 2)
    out_ref[...] = x + gate * _dot(y.astype(BF16), wout_ref[...])


def _merge(hf, hb, u, yd, yf, x_lat, x_ctx, ctx_blk0, mods, head_g, wml, wda, wfn, wout, layer, need_ctx, n_batch):
    n_rows = n_batch * (SEQ + CTX_LEN)
    tm = _row_tile(n_batch)
    rows_per_batch = SEQ // tm
    lat_blocks = n_batch * rows_per_batch
    ni = (n_rows if need_ctx else n_batch * SEQ) // tm
    kern = functools.partial(_merge_kernel, rows_per_batch=rows_per_batch, n_batch=n_batch)
    row = lambda i: (i, 0)
    full = lambda i: (0, 0)
    wspec = pl.BlockSpec((None, D_MODEL, D_MODEL), lambda i: (layer, 0, 0))
    return pl.pallas_call(
        kern,
        grid=(ni,),
        in_specs=[
            pl.BlockSpec((tm, D_MODEL), row),
            pl.BlockSpec((tm, D_MODEL), row),
            pl.BlockSpec((tm, D_MODEL), lambda i: (i, U_OML)),
            pl.BlockSpec((tm, D_MODEL), lambda i: (i, U_GPRE)),
            pl.BlockSpec((tm, D_MODEL), lambda i: (i, U_GPRE + 1)),
            pl.BlockSpec((tm, D_MODEL), lambda i: (i, U_GPRE + 2)),
            pl.BlockSpec((tm, D_MODEL), row),
            pl.BlockSpec((tm, D_MODEL), row),
            pl.BlockSpec((tm, D_MODEL), lambda i: (jnp.minimum(i, lat_blocks - 1), 0)),
            pl.BlockSpec((tm, D_MODEL), lambda i: (ctx_blk0 + jnp.maximum(i - lat_blocks, 0), 0)),
            pl.BlockSpec((None, 8, N_MOD), lambda i: (layer, 0, 0)),
            pl.BlockSpec((1, D_MODEL), full),
            wspec, wspec, wspec, wspec,
        ],
        out_specs=pl.BlockSpec((tm, D_MODEL), row),
        out_shape=jax.ShapeDtypeStruct((n_rows, D_MODEL), F32),
        compiler_params=_cparams(("arbitrary",)),
        name="merge_out_proj",
    )(hf, hb, u, u, u, u, yd, yf, x_lat, x_ctx, mods, head_g, wml, wda, wfn, wout)


FFN_CHUNKS = ((0, 1024), (1024, 1024), (2048, 768))


def _ffn_kernel(x_ref, mod_ref, g_ref, win_ref, wout_ref, fg_ref, out_ref, *, rows_per_batch, n_batch, final):
    i = pl.program_id(0)
    x = x_ref[...]
    shift = _mod_row(mod_ref, i, rows_per_batch, n_batch, 3)
    scale = _mod_row(mod_ref, i, rows_per_batch, n_batch, 4)
    gate = _mod_row(mod_ref, i, rows_per_batch, n_batch, 5)
    h = (_rms(x, g_ref[...]) * (1.0 + scale) + shift).astype(BF16)
    acc = None
    for lo, width in FFN_CHUNKS:
        a = _dot(h, win_ref[:, lo:lo + width])
        b = _dot(h, win_ref[:, D_FF + lo:D_FF + lo + width])
        act = (a * jax.nn.sigmoid(a) * b).astype(BF16)
        part = _dot(act, wout_ref[lo:lo + width, :])
        acc = part if acc is None else acc + part
    xn = x + gate * acc
    out_ref[...] = _rms(xn, fg_ref[...]) if final else xn


def _ffn(x, mods, g, w_in, w_out, layer, final_g, final, n_rows_out, n_batch):
    tm = _row_tile(n_batch)
    rows_per_batch = SEQ // tm
    kern = functools.partial(_ffn_kernel, rows_per_batch=rows_per_batch, n_batch=n_batch, final=final)
    row = lambda i: (i, 0)
    full = lambda i: (0, 0)
    return pl.pallas_call(
        kern,
        grid=(n_rows_out // tm,),
        in_specs=[
            pl.BlockSpec((tm, D_MODEL), row),
            pl.BlockSpec((None, 8, N_MOD), lambda i: (layer, 0, 0)),
            pl.BlockSpec((1, D_MODEL), full),
            pl.BlockSpec((None, D_MODEL, 2 * D_FF), lambda i: (layer, 0, 0)),
            pl.BlockSpec((None, D_FF, D_MODEL), lambda i: (layer, 0, 0)),
            pl.BlockSpec((1, D_MODEL), full),
        ],
        out_specs=pl.BlockSpec((tm, D_MODEL), row),
        out_shape=jax.ShapeDtypeStruct((n_rows_out, D_MODEL), F32),
        compiler_params=_cparams(("arbitrary",)),
        name="swiglu_ffn",
    )(x, mods, g, w_in, w_out, final_g)


def _da_col_perm(w):
    half = DA_DH // 2
    lead = w.shape[:-1]
    return jnp.swapaxes(w.reshape(lead + (DA_HEADS, 2, 2, half)), -3, -2).reshape(lead + (DA_HEADS * DA_DV,))


def _rope_tables(pad):
    n_freq = DA_DH // 4
    rows = SEQ // GRID_W
    inv = ROPE_BASE ** (-jnp.arange(n_freq, dtype=F32) / n_freq)
    r = jnp.repeat(jnp.arange(rows, dtype=F32), GRID_W)
    col = jnp.tile(jnp.arange(GRID_W, dtype=F32), rows)
    ang = jnp.concatenate([r[:, None] * inv, col[:, None] * inv], axis=-1)
    cos, sin = jnp.cos(ang), jnp.sin(ang)
    cos_t = jnp.concatenate([cos, cos, cos, cos], axis=-1)
    sin_t = jnp.concatenate([-sin, -sin, sin, sin], axis=-1)
    cos_t = jnp.concatenate([cos_t, jnp.ones((pad, 128), F32)], axis=0)
    sin_t = jnp.concatenate([sin_t, jnp.zeros((pad, 128), F32)], axis=0)
    return cos_t, sin_t


def kernel(x, c, ctx, c_ctx, w_ada, b_ada, norm_g, w_in, ml_gate_b, ml_head_g, da_lam, da_head_g,
           w_br_ml, w_br_da, w_br_fn, w_out, w_ffn_in, w_ffn_out, final_g):
    n_batch = x.shape[0]
    n_lat = n_batch * SEQ
    x_lat = x.reshape(n_lat, D_MODEL)
    x_ctx = ctx.reshape(n_batch * CTX_LEN, D_MODEL)
    cc = jnp.concatenate([c, c_ctx[None, :], jnp.zeros((8 - n_batch - 1, D_MODEL), F32)], axis=0)
    mods = _mods(cc, w_ada, b_ada)
    cos_t, sin_t = _rope_tables(n_batch * CTX_LEN)
    tables = _dft_tables()
    final_g2 = final_g.reshape(1, D_MODEL)
    wb_ml, wb_da, wb_fn, wb_out = _to_bf16(w_br_ml), _to_bf16(w_br_da), _to_bf16(w_br_fn), _to_bf16(w_out)
    wb_ffn_in, wb_ffn_out = _to_bf16(w_ffn_in), _to_bf16(w_ffn_out)

    gate_lo = 4 * D_MODEL
    da_lo = gate_lo + N_GATE
    fn_lo = da_lo + 3 * D_MODEL
    w_main = jnp.concatenate([w_in[..., :D_MODEL], w_in[..., 2 * D_MODEL:gate_lo],
                              _da_col_perm(w_in[..., da_lo:da_lo + D_MODEL]),
                              _da_col_perm(w_in[..., da_lo + D_MODEL:da_lo + 2 * D_MODEL]),
                              w_in[..., da_lo + 2 * D_MODEL:fn_lo],
                              w_in[..., fn_lo + D_MODEL:], w_in[..., fn_lo:fn_lo + D_MODEL]], axis=-1).astype(BF16)
    w_kt = jnp.swapaxes(w_in[..., D_MODEL:2 * D_MODEL], 1, 2).astype(BF16)
    w_gate_t = jnp.swapaxes(w_in[..., gate_lo:da_lo], 1, 2).astype(BF16)

    tm_in = n_batch * CTX_LEN
    tm_tok = _row_tile(n_batch)
    xs = None
    for l in range(DEPTH):
        need_ctx = l < DEPTH - 1
        lam_init = 0.8 - 0.6 * math.exp(-0.3 * l)
        xl, xc = (x_lat, x_ctx) if xs is None else (xs, xs)
        u, kt, fn, gates_t = _inproj(xl, xc, 0 if xs is None else n_lat // tm_in, mods,
                                     norm_g[l, 0].reshape(1, D_MODEL), w_main, w_kt, w_gate_t, l,
                                     cos_t, sin_t, n_batch)
        hf, hb = _mlstm(u, kt, gates_t, ml_gate_b[l], n_batch)
        yd = _attn(u, da_lam[l], da_head_g[l].reshape(1, D_MODEL), lam_init, need_ctx, n_batch)
        yf = _fourier(fn, tables, need_ctx, n_batch)
        xs = _merge(hf, hb, u, yd, yf, xl, xc, 0 if xs is None else n_lat // tm_tok, mods,
                    ml_head_g[l].reshape(1, D_MODEL), wb_ml, wb_da, wb_fn, wb_out, l, need_ctx, n_batch)
        final = l == DEPTH - 1
        n_out = n_lat if final else xs.shape[0]
        xs = _ffn(xs, mods, norm_g[l, 1].reshape(1, D_MODEL), wb_ffn_in, wb_ffn_out, l,
                  final_g2, final, n_out, n_batch)
    return xs.reshape(n_batch, SEQ, D_MODEL)
```

```python
import functools
import math

import jax
import jax.numpy as jnp
from jax import lax
from jax.experimental import pallas as pl
from jax.experimental.pallas import tpu as pltpu

D_MODEL = 1024
SEQ = 4096
DEPTH = 4
CTX_LEN = 256
GRID_W = 64
NORM_EPS = 1e-6

ML_HEADS = 4
ML_DK = 256
ML_CHUNK = 128

DA_HEADS = 8
DA_DH = 64
DA_DV = 2 * DA_DH
ROPE_BASE = 10000.0
ATTN_TQ = 512
ATTN_ROW_CHUNKS = 4

FN_GROUPS = 4
FN_GC = 256
FFT_R = 64

D_FF = 2816
N_GATE = 4 * ML_HEADS
N_MOD = 6 * D_MODEL

U_QML, U_VML, U_OML, U_QDA, U_KDA, U_VDA, U_GPRE = 0, 1, 2, 3, 4, 5, 6
U_BLOCKS = 9
W_BLOCKS = U_BLOCKS + 1
STEP_KT = 1
N_COL_STEPS = W_BLOCKS + 1

VMEM_LIMIT_V7X = 56 * 1024 * 1024

BF16 = jnp.bfloat16
F32 = jnp.float32


def _cparams(sem):
    return pltpu.CompilerParams(dimension_semantics=sem, vmem_limit_bytes=VMEM_LIMIT_V7X)


def _dot(a, b):
    return jnp.dot(a, b, preferred_element_type=F32)


def _dot_nt(a, b):
    return lax.dot_general(a, b, (((1,), (1,)), ((), ())), preferred_element_type=F32)


def _mod_row(mod_ref, i, rows_per_batch, n_batch, col):
    r = jnp.minimum(i // rows_per_batch, n_batch)
    return mod_ref[pl.ds(r, 1), col * D_MODEL:(col + 1) * D_MODEL]


def _row_tile(n_batch):
    return min(512, n_batch * CTX_LEN)


def _rms(x, g):
    return x * lax.rsqrt(jnp.mean(x * x, axis=-1, keepdims=True) + NORM_EPS) * g


def _cast_kernel(w_ref, o_ref):
    o_ref[...] = w_ref[...].astype(o_ref.dtype)


def _to_bf16(w):
    n_l, rows, cols = w.shape
    tr = 256
    return pl.pallas_call(
        _cast_kernel,
        grid=(n_l, rows // tr),
        in_specs=[pl.BlockSpec((1, tr, cols), lambda l, i: (l, i, 0))],
        out_specs=pl.BlockSpec((1, tr, cols), lambda l, i: (l, i, 0)),
        out_shape=jax.ShapeDtypeStruct(w.shape, BF16),
        compiler_params=_cparams(("arbitrary", "arbitrary")),
        name="cast_bf16",
    )(w)


def _mods_kernel(c_ref, w_ref, b_ref, o_ref):
    c = c_ref[...]
    s = (c * jax.nn.sigmoid(c)).astype(BF16)
    o_ref[0] = _dot(s, w_ref[0].astype(BF16)) + b_ref[0]


def _mods(cc, w_ada, b_ada):
    tn = 1536
    return pl.pallas_call(
        _mods_kernel,
        grid=(DEPTH, N_MOD // tn),
        in_specs=[
            pl.BlockSpec((8, D_MODEL), lambda l, j: (0, 0)),
            pl.BlockSpec((1, D_MODEL, tn), lambda l, j: (l, 0, j)),
            pl.BlockSpec((1, 1, tn), lambda l, j: (l, 0, j)),
        ],
        out_specs=pl.BlockSpec((1, 8, tn), lambda l, j: (l, 0, j)),
        out_shape=jax.ShapeDtypeStruct((DEPTH, 8, N_MOD), F32),
        compiler_params=_cparams(("arbitrary", "arbitrary")),
        name="adaln_mods",
    )(cc, w_ada, b_ada.reshape(DEPTH, 1, N_MOD))


def _inproj_kernel(xl_ref, xc_ref, mod_ref, g_ref, w_ref, wkt_ref, wgt_ref, cos_ref, sin_ref,
                   u_ref, kt_ref, fn_ref, gate_ref, xn_ref, *, rows_per_batch, n_batch):
    i = pl.program_id(0)
    j = pl.program_id(1)

    def normalise(x_ref):
        y = _rms(x_ref[...], g_ref[...])
        shift = _mod_row(mod_ref, i, rows_per_batch, n_batch, 0)
        scale = _mod_row(mod_ref, i, rows_per_batch, n_batch, 1)
        xn_ref[...] = (y * (1.0 + scale) + shift).astype(BF16)
        gate_ref[...] = _dot_nt(wgt_ref[...], xn_ref[...])

    is_lat = i < rows_per_batch * n_batch

    @pl.when(jnp.logical_and(j == 0, is_lat))
    def _():
        normalise(xl_ref)

    @pl.when(jnp.logical_and(j == 0, jnp.logical_not(is_lat)))
    def _():
        normalise(xc_ref)

    def product():
        return _dot(xn_ref[...], w_ref[...])

    blk = jnp.where(j == 0, 0, j - 1)
    is_rope = jnp.logical_or(blk == U_QDA, blk == U_KDA)
    is_plain = jnp.logical_and(j != STEP_KT, jnp.logical_and(jnp.logical_not(is_rope), blk < U_BLOCKS))

    @pl.when(is_plain)
    def _():
        u_ref[...] = product().astype(BF16)

    @pl.when(j == STEP_KT)
    def _():
        kt_ref[...] = (_dot_nt(wkt_ref[...], xn_ref[...]) * (ML_DK ** -0.5)).astype(BF16)

    @pl.when(is_rope)
    def _():
        acc = product()
        cos = cos_ref[...]
        sin = sin_ref[...]
        for t in range(acc.shape[1] // DA_DV):
            sl = slice(t * DA_DV, (t + 1) * DA_DV)
            x = acc[:, sl]
            u_ref[:, sl] = (x * cos + pltpu.roll(x, DA_DV // 2, 1) * sin).astype(BF16)

    @pl.when(blk == U_BLOCKS)
    def _():
        acc = product()
        for g in range(FN_GROUPS):
            fn_ref[g] = acc[:, g * FN_GC:(g + 1) * FN_GC].astype(BF16)


def _inproj(x_lat, x_ctx, ctx_blk, mods, g, w_main, w_kt, w_gate_t, layer, cos_t, sin_t, n_batch):
    tm = n_batch * CTX_LEN
    assert SEQ % tm == 0 and cos_t.shape[0] == SEQ + tm
    n_rows = n_batch * (SEQ + CTX_LEN)
    ni = n_rows // tm
    rows_per_batch = SEQ // tm
    lat_blocks = n_batch * rows_per_batch

    def tab_idx(i, j):
        return (jnp.where(i < lat_blocks, i % rows_per_batch, rows_per_batch), 0)

    def w_blk(j):
        return jnp.where(j == 0, 0, j - 1)

    kern = functools.partial(_inproj_kernel, rows_per_batch=rows_per_batch, n_batch=n_batch)
    return pl.pallas_call(
        kern,
        grid=(ni, N_COL_STEPS),
        in_specs=[
            pl.BlockSpec((tm, D_MODEL), lambda i, j: (jnp.minimum(i, lat_blocks - 1), 0)),
            pl.BlockSpec((tm, D_MODEL), lambda i, j: (ctx_blk, 0)),
            pl.BlockSpec((None, 8, N_MOD), lambda i, j: (layer, 0, 0)),
            pl.BlockSpec((1, D_MODEL), lambda i, j: (0, 0)),
            pl.BlockSpec((None, D_MODEL, D_MODEL), lambda i, j: (layer, 0, w_blk(j))),
            pl.BlockSpec((None, D_MODEL, D_MODEL), lambda i, j: (layer, 0, 0)),
            pl.BlockSpec((None, N_GATE, D_MODEL), lambda i, j: (layer, 0, 0)),
            pl.BlockSpec((tm, 128), tab_idx),
            pl.BlockSpec((tm, 128), tab_idx),
        ],
        out_specs=[
            pl.BlockSpec((tm, D_MODEL), lambda i, j: (i, jnp.minimum(w_blk(j), U_BLOCKS - 1))),
            pl.BlockSpec((D_MODEL, tm), lambda i, j: (0, i)),
            pl.BlockSpec((FN_GROUPS, tm, FN_GC), lambda i, j: (0, i, 0)),
            pl.BlockSpec((N_GATE, tm), lambda i, j: (0, i)),
        ],
        out_shape=[
            jax.ShapeDtypeStruct((n_rows, U_BLOCKS * D_MODEL), BF16),
            jax.ShapeDtypeStruct((D_MODEL, n_rows), BF16),
            jax.ShapeDtypeStruct((FN_GROUPS, n_rows, FN_GC), BF16),
            jax.ShapeDtypeStruct((N_GATE, n_rows), F32),
        ],
        scratch_shapes=[pltpu.VMEM((tm, D_MODEL), BF16)],
        compiler_params=_cparams(("arbitrary", "arbitrary")),
        name="in_proj",
    )(x_lat, x_ctx, mods, g, w_main, w_kt, w_gate_t, cos_t, sin_t)


def _split3(x):
    hi = x.astype(BF16).astype(F32)
    mid = (x - hi).astype(BF16).astype(F32)
    lo = (x - hi - mid).astype(BF16).astype(F32)
    return hi, mid, lo


def _mlstm_kernel(qf_ref, ktf_ref, vf_ref, gtf_ref, gtfn_ref, qb_ref, ktb_ref, vb_ref, gtb_ref, gtbn_ref, bias_ref,
                  hf_ref, hb_ref, *scratch):
    n_st = 2 * ML_HEADS
    cx_refs, bw_refs, pmw_refs = scratch[:n_st], scratch[n_st:2 * n_st], scratch[2 * n_st:3 * n_st]
    m_refs, c_refs, bend_refs, g_refs = (scratch[3 * n_st + 2 * k:3 * n_st + 2 * k + 2] for k in range(4))
    state_refs = cx_refs + m_refs
    s = pl.program_id(1)
    L = ML_CHUNK
    H = ML_HEADS
    W = 128

    @pl.when(s == 0)
    def _():
        for ref in state_refs:
            ref[...] = jnp.zeros_like(ref)

    t_idx = lax.broadcasted_iota(jnp.int32, (L, L), 0)
    s_idx = lax.broadcasted_iota(jnp.int32, (L, L), 1)
    eye = t_idx == s_idx
    sub8 = lax.broadcasted_iota(jnp.int32, (8, W), 0)
    ones_w = jnp.ones((L, W), BF16)
    er = lax.broadcasted_iota(jnp.int32, (4 * L, 2 * W), 0)
    ec = lax.broadcasted_iota(jnp.int32, (4 * L, 2 * W), 1)
    expand = jnp.where((er < 3 * L) == (ec < W), 1.0, 0.0).astype(BF16)

    def running_max_rows(x, d):
        n_tiles = L // 8
        out = [None] * n_tiles
        carry = None
        for j in (range(n_tiles) if d == 0 else range(n_tiles - 1, -1, -1)):
            r = x[8 * j:8 * (j + 1)]
            k = 1
            while k < 8:
                if d == 0:
                    r = jnp.maximum(r, jnp.where(sub8 >= k, pltpu.roll(r, k, 0), -jnp.inf))
                else:
                    r = jnp.maximum(r, jnp.where(sub8 < 8 - k, pltpu.roll(r, 8 - k, 0), -jnp.inf))
                k *= 2
            if carry is not None:
                r = jnp.maximum(r, carry)
            carry = jnp.broadcast_to(r[7:8] if d == 0 else r[0:1], (8, W))
            out[j] = r
        return jnp.concatenate(out, axis=0)

    def gate_part(d, gt_ref):
        before = (t_idx <= s_idx) if d == 0 else (t_idx >= s_idx)
        gt = gt_ref[2 * H * d:2 * H * (d + 1), :] + bias_ref[2 * H * d:2 * H * (d + 1), :]
        i4, f4 = gt[0:H], gt[H:2 * H]
        lf4 = jnp.minimum(f4, 0.0) - jnp.log1p(jnp.exp(-jnp.abs(f4)))
        lf_terms = jnp.concatenate(list(_split3(lf4)) + [jnp.zeros((H, L), F32)], axis=0).astype(BF16)
        cum_rhs = jnp.concatenate([jnp.where(before, 1.0, 0.0).astype(BF16), ones_w], axis=1)
        r = _dot(lf_terms, cum_rhs)
        bx = r[0:H] + r[H:2 * H] + r[2 * H:3 * H]
        b4, bend4 = bx[:, 0:L], bx[:, L:L + W]
        c4 = i4 - b4
        c_refs[d][...] = c4
        bend_refs[d][...] = bend4
        g_refs[d][...] = bend4 - b4 + i4
        col_terms = _split3(b4) + (c4.astype(BF16).astype(F32),)
        for h in range(H):
            diag = jnp.concatenate([jnp.where(eye, x[h:h + 1, :], 0.0) for x in col_terms], axis=1).astype(BF16)
            wide = _dot(diag, expand)
            bw_refs[d * H + h][...] = wide[:, 0:W]
            pmw_refs[d * H + h][...] = running_max_rows(wide[:, W:2 * W], d)

    dirs = ((0, qf_ref, ktf_ref, vf_ref, gtf_ref, gtfn_ref, hf_ref),
            (1, qb_ref, ktb_ref, vb_ref, gtb_ref, gtbn_ref, hb_ref))

    @pl.when(s == 0)
    def _():
        for d, _, _, _, gt_ref, _, _ in dirs:
            gate_part(d, gt_ref)

    for d, q_ref, kt_ref, v_ref, _, _, h_ref in dirs:
        causal = (s_idx <= t_idx) if d == 0 else (s_idx >= t_idx)
        c4, bend4, g4 = c_refs[d][...], bend_refs[d][...], g_refs[d][...]
        m_prev4 = m_refs[d][...]
        m_new4 = jnp.maximum(bend4 + m_prev4, jnp.max(g4, axis=1, keepdims=True))
        m_refs[d][...] = m_new4
        decay4 = jnp.exp(bend4 + m_prev4 - m_new4)
        w4 = jnp.exp(g4 - m_new4)
        for h in range(H):
            st = d * H + h
            hs = slice(h * ML_DK, (h + 1) * ML_DK)
            b_w = bw_refs[st][...]
            m_w = jnp.maximum(pmw_refs[st][...], m_prev4[h:h + 1, :])

            q = q_ref[:, hs]
            kt = kt_ref[hs, :]
            vx = jnp.concatenate([v_ref[:, hs], ones_w], axis=1)
            a = (jnp.where(causal, jnp.exp(c4[h:h + 1, :] - m_w), 0.0) * _dot(q, kt)).astype(BF16)
            cx_prev = cx_refs[st][...]
            qc = _dot(q, cx_prev.astype(BF16))
            av = _dot(a, vx)
            sc_w = jnp.exp(m_prev4[h:h + 1, :] - m_w)
            den = sc_w * qc[:, ML_DK:] + av[:, ML_DK:]
            inv = 1.0 / jnp.maximum(jnp.abs(den), jnp.exp(-(b_w + m_w)))
            for t in range(ML_DK // W):
                ts = slice(t * W, (t + 1) * W)
                h_ref[:, h * ML_DK + t * W:h * ML_DK + (t + 1) * W] = (
                    (sc_w * qc[:, ts] + av[:, ts]) * inv).astype(h_ref.dtype)

            kw = (kt.astype(F32) * w4[h:h + 1, :]).astype(BF16)
            dec = jnp.concatenate([decay4[h:h + 1, :]] * (ML_DK // W + 1), axis=1)
            cx_refs[st][...] = dec * cx_prev + _dot(kw, vx)

    for d, _, _, _, _, gtn_ref, _ in dirs:
        gate_part(d, gtn_ref)


def _mlstm(u, kt, gates_t, gate_b, n_batch):
    n_rows = u.shape[0]
    L = ML_CHUNK
    lat_chunks = SEQ // L
    ctx_chunks = CTX_LEN // L
    n_steps = ctx_chunks + lat_chunks
    ctx_base = n_batch * lat_chunks

    def rowblk(d):
        def f(b, s):
            in_ctx = s < ctx_chunks
            if d == 0:
                c = jnp.where(in_ctx, s, s - ctx_chunks)
            else:
                c = jnp.where(in_ctx, ctx_chunks - 1 - s, lat_chunks - 1 - (s - ctx_chunks))
            return jnp.where(in_ctx, ctx_base + ctx_chunks * b, lat_chunks * b) + c
        return f

    def dir_specs(d):
        rb = rowblk(d)
        return [
            pl.BlockSpec((L, D_MODEL), lambda b, s: (rb(b, s), U_QML)),
            pl.BlockSpec((D_MODEL, L), lambda b, s: (0, rb(b, s))),
            pl.BlockSpec((L, D_MODEL), lambda b, s: (rb(b, s), U_VML)),
            pl.BlockSpec((N_GATE, L), lambda b, s: (0, rb(b, s))),
            pl.BlockSpec((N_GATE, L), lambda b, s: (0, rb(b, jnp.minimum(s + 1, n_steps - 1)))),
        ]

    def out_spec(d):
        rb = rowblk(d)
        return pl.BlockSpec((L, D_MODEL), lambda b, s: (rb(b, s), 0))

    n_st = 2 * ML_HEADS
    return pl.pallas_call(
        _mlstm_kernel,
        grid=(n_batch, n_steps),
        in_specs=dir_specs(0) + dir_specs(1) + [pl.BlockSpec((N_GATE, 1), lambda b, s: (0, 0))],
        out_specs=[out_spec(0), out_spec(1)],
        out_shape=[jax.ShapeDtypeStruct((n_rows, D_MODEL), BF16)] * 2,
        scratch_shapes=(
            [pltpu.VMEM((ML_DK, ML_DK + 128), F32)] * n_st
            + [pltpu.VMEM((L, 128), F32)] * (2 * n_st)
            + [pltpu.VMEM((ML_HEADS, 128), F32)] * 2
            + [pltpu.VMEM((ML_HEADS, L), F32)] * 2
            + [pltpu.VMEM((ML_HEADS, 128), F32)] * 2
            + [pltpu.VMEM((ML_HEADS, L), F32)] * 2),
        compiler_params=_cparams(("arbitrary", "arbitrary")),
        name="mlstm_scan",
    )(u, kt, u, gates_t, gates_t, u, kt, u, gates_t, gates_t, gate_b.reshape(N_GATE, 1))


def _attn_lambda(lam_ref, lam_init):
    lq = lam_ref[...]
    return (jnp.exp(jnp.sum(lq[0:1] * lq[1:2], axis=1, keepdims=True))
            - jnp.exp(jnp.sum(lq[2:3] * lq[3:4], axis=1, keepdims=True)) + lam_init)


def _attn_queries(q_ref):
    q = q_ref[...]
    lane = lax.broadcasted_iota(jnp.int32, (1, DA_DV), 1)
    zero = jnp.zeros_like(q)
    is_map0 = (lane % DA_DH) < (DA_DH // 2)
    q2 = jnp.concatenate([jnp.where(is_map0, q, zero), jnp.where(is_map0, zero, q)], axis=0)
    return q2 * (DA_DH ** -0.5)


def _attn_values(s_chunks, m_chunks, vx, lam, g, lam_init, tq):
    accs = [_dot(jnp.exp(s - m).astype(BF16), vx) for s, m in zip(s_chunks, m_chunks)]
    acc = jnp.concatenate(accs, axis=0)
    o0 = acc[0:tq, 0:DA_DV] * (1.0 / acc[0:tq, DA_DV:DA_DV + 1])
    o1 = acc[tq:, 0:DA_DV] * (1.0 / acc[tq:, DA_DV:DA_DV + 1])
    return _rms(o0 - lam * o1, g) * (1.0 - lam_init)


def _attn_kernel(q_ref, kl_ref, vl_ref, kc_ref, vc_ref, lam_ref, g_ref, o_ref, vx_ref, s_ref, m_ref,
                 *, lam_init, q_blocks):
    t = pl.program_id(0)
    tq = q_ref.shape[0]
    rows = 2 * tq // ATTN_ROW_CHUNKS

    @pl.when(jnp.logical_and(t > 0, (t - 1) % q_blocks == 0))
    def _():
        vx_ref[0:SEQ, 0:DA_DV] = vl_ref[...]
        vx_ref[SEQ:, 0:DA_DV] = vc_ref[...]
        lane_v = lax.broadcasted_iota(jnp.int32, (SEQ + CTX_LEN, DA_DV), 1)
        vx_ref[:, DA_DV:] = jnp.where(lane_v == 0, 1.0, 0.0).astype(BF16)

    def score(slot):
        q2 = _attn_queries(q_ref)
        s_lat = _dot_nt(q2, kl_ref[...])
        s_ctx = _dot_nt(q2, kc_ref[...])
        m_ref[slot] = jnp.maximum(jnp.max(s_lat, axis=1, keepdims=True), jnp.max(s_ctx, axis=1, keepdims=True))
        s_ref[slot, :, 0:SEQ] = s_lat
        s_ref[slot, :, SEQ:] = s_ctx

    def drain(slot):
        s_chunks = [s_ref[slot, c * rows:(c + 1) * rows, :] for c in range(ATTN_ROW_CHUNKS)]
        m_chunks = [m_ref[slot, c * rows:(c + 1) * rows, :] for c in range(ATTN_ROW_CHUNKS)]
        o = _attn_values(s_chunks, m_chunks, vx_ref[...], _attn_lambda(lam_ref, lam_init), g_ref[...], lam_init, tq)
        o_ref[...] = o.astype(o_ref.dtype)

    @pl.when(t == 0)
    def _():
        score(0)

    for parity in (0, 1):
        @pl.when(jnp.logical_and(t > 0, t % 2 == parity))
        def _():
            drain(1 - parity)
            score(parity)


def _attn_ctx_kernel(q_ref, kc_ref, vc_ref, lam_ref, g_ref, yin_ref, o_ref, *, lam_init):
    del yin_ref
    tq = q_ref.shape[0]
    s = _dot_nt(_attn_queries(q_ref), kc_ref[...])
    lane_v = lax.broadcasted_iota(jnp.int32, (CTX_LEN, DA_DV), 1)
    vx = jnp.concatenate([vc_ref[...], jnp.where(lane_v == 0, 1.0, 0.0).astype(BF16)], axis=1)
    o = _attn_values([s], [jnp.max(s, axis=1, keepdims=True)], vx, _attn_lambda(lam_ref, lam_init),
                     g_ref[...], lam_init, tq)
    o_ref[...] = o.astype(o_ref.dtype)


def _attn(u, da_lam, head_g, lam_init, need_ctx, n_batch):
    n_rows = u.shape[0]
    tq = ATTN_TQ
    lat_qblocks = SEQ // tq
    ctx_base = n_batch * (SEQ // CTX_LEN)
    cpb = D_MODEL // DA_DV
    n_keys = SEQ + CTX_LEN

    n_items = n_batch * DA_HEADS * lat_qblocks

    def item(t):
        t = jnp.clip(t, 0, n_items - 1)
        return t // (DA_HEADS * lat_qblocks), (t // lat_qblocks) % DA_HEADS, t % lat_qblocks

    def scored(f):
        return lambda t: f(*item(t))

    def drained(f):
        return lambda t: f(*item(t - 1))

    kern = functools.partial(_attn_kernel, lam_init=lam_init, q_blocks=lat_qblocks)
    yd = pl.pallas_call(
        kern,
        grid=(n_items + 1,),
        in_specs=[
            pl.BlockSpec((tq, DA_DV), scored(lambda b, h, qi: (b * lat_qblocks + qi, U_QDA * cpb + h))),
            pl.BlockSpec((SEQ, DA_DV), scored(lambda b, h, qi: (b, U_KDA * cpb + h))),
            pl.BlockSpec((SEQ, DA_DV), drained(lambda b, h, qi: (b, U_VDA * cpb + h))),
            pl.BlockSpec((CTX_LEN, DA_DV), scored(lambda b, h, qi: (ctx_base + b, U_KDA * cpb + h))),
            pl.BlockSpec((CTX_LEN, DA_DV), drained(lambda b, h, qi: (ctx_base + b, U_VDA * cpb + h))),
            pl.BlockSpec((4, DA_DH), lambda t: (0, 0)),
            pl.BlockSpec((1, DA_DV), drained(lambda b, h, qi: (0, h))),
        ],
        out_specs=pl.BlockSpec((tq, DA_DV), drained(lambda b, h, qi: (b * lat_qblocks + qi, h))),
        out_shape=jax.ShapeDtypeStruct((n_rows, D_MODEL), BF16),
        scratch_shapes=[
            pltpu.VMEM((n_keys, 2 * DA_DV), BF16),
            pltpu.VMEM((2, 2 * tq, n_keys), F32),
            pltpu.VMEM((2, 2 * tq, 1), F32),
        ],
        compiler_params=_cparams(("arbitrary",)),
        name="diff_attn",
    )(u, u, u, u, u, da_lam, head_g)
    if not need_ctx:
        return yd
    kern_ctx = functools.partial(_attn_ctx_kernel, lam_init=lam_init)
    return pl.pallas_call(
        kern_ctx,
        grid=(n_batch, DA_HEADS),
        in_specs=[
            pl.BlockSpec((CTX_LEN, DA_DV), lambda b, h: (ctx_base + b, U_QDA * cpb + h)),
            pl.BlockSpec((CTX_LEN, DA_DV), lambda b, h: (ctx_base + b, U_KDA * cpb + h)),
            pl.BlockSpec((CTX_LEN, DA_DV), lambda b, h: (ctx_base + b, U_VDA * cpb + h)),
            pl.BlockSpec((4, DA_DH), lambda b, h: (0, 0)),
            pl.BlockSpec((1, DA_DV), lambda b, h: (0, h)),
            pl.BlockSpec(memory_space=pl.ANY),
        ],
        out_specs=pl.BlockSpec((CTX_LEN, DA_DV), lambda b, h: (ctx_base + b, h)),
        out_shape=jax.ShapeDtypeStruct((n_rows, D_MODEL), BF16),
        input_output_aliases={5: 0},
        compiler_params=_cparams(("arbitrary", "arbitrary")),
        name="diff_attn_ctx",
    )(u, u, u, da_lam, head_g, yd)


def _dft_tables():
    R = FFT_R

    def cs(num, period):
        ang = (num % period).astype(F32) * (2.0 * math.pi / period)
        return jnp.cos(ang), jnp.sin(ang)

    idx = jnp.arange(R, dtype=jnp.int32)
    c1, s1 = cs(idx[:, None] * idx[None, :], R)
    a1 = jnp.concatenate([c1, -s1], axis=0).astype(BF16)
    f2 = idx[:, None, None]
    f1 = idx[None, :, None]
    t1 = idx[None, None, :]
    mc, ms = cs(t1 * (R * f1 + f2), SEQ)
    b2 = jnp.concatenate([jnp.concatenate([mc, ms], axis=2),
                          jnp.concatenate([-ms, mc], axis=2)], axis=1).astype(BF16)
    ch = jnp.arange(FN_GC, dtype=jnp.int32)
    cc, sc = cs(ch[:, None] * ch[None, :], FN_GC)
    csm = jnp.concatenate([cc, sc], axis=0).astype(BF16)
    actx = jnp.concatenate([cc, -sc], axis=0).astype(BF16)
    return a1, b2, csm, actx


def _fft1_kernel(a_ref, z_ref, p_ref):
    p_ref[0, 0] = _dot(a_ref[...], z_ref[0]).astype(p_ref.dtype)


def _fft1(fn_view, a1, n_batch):
    R = FFT_R
    lanes = R * FN_GC
    lc = lanes
    return pl.pallas_call(
        _fft1_kernel,
        grid=(FN_GROUPS, n_batch, lanes // lc),
        in_specs=[
            pl.BlockSpec((2 * R, R), lambda g, b, c: (0, 0)),
            pl.BlockSpec((1, R, lc), lambda g, b, c: (g, b, c)),
        ],
        out_specs=pl.BlockSpec((1, 1, 2 * R, lc), lambda g, b, c: (g, b, 0, c)),
        out_shape=jax.ShapeDtypeStruct((FN_GROUPS, n_batch, 2 * R, lanes), BF16),
        compiler_params=_cparams(("arbitrary", "arbitrary", "arbitrary")),
        name="fourier_stage1",
    )(a1, fn_view)


def _fft2_kernel(p_ref, b_ref, cs_ref, o_ref, *, f2b):
    R = FFT_R
    cc = cs_ref[0:FN_GC, :]
    sc = cs_ref[FN_GC:2 * FN_GC, :]
    norm = 1.0 / math.sqrt(SEQ * FN_GC)
    for g in range(FN_GROUPS):
        xr, xi = [], []
        for jj in range(f2b):
            stacked = jnp.concatenate([p_ref[g, 0, 0, jj], p_ref[g, 0, 1, jj]], axis=0)
            x = _dot(b_ref[jj], stacked)
            xr.append(x[0:R])
            xi.append(x[R:2 * R])
        y = (_dot(jnp.concatenate(xr, axis=0).astype(BF16), cc)
             + _dot(jnp.concatenate(xi, axis=0).astype(BF16), sc)) * norm
        for jj in range(f2b):
            lo = jj * D_MODEL + g * FN_GC
            o_ref[:, lo:lo + FN_GC] = y[jj * R:(jj + 1) * R].astype(o_ref.dtype)


def _fft2(p6, b2, csm, n_rows, n_batch):
    R = FFT_R
    f2b = 8
    kern = functools.partial(_fft2_kernel, f2b=f2b)
    return pl.pallas_call(
        kern,
        grid=(n_batch, R // f2b),
        in_specs=[
            pl.BlockSpec((FN_GROUPS, 1, 2, f2b, R, FN_GC), lambda b, f: (0, b, 0, f, 0, 0)),
            pl.BlockSpec((f2b, 2 * R, 2 * R), lambda b, f: (f, 0, 0)),
            pl.BlockSpec((2 * FN_GC, FN_GC), lambda b, f: (0, 0)),
        ],
        out_specs=pl.BlockSpec((R, f2b * D_MODEL), lambda b, f: (b, f)),
        out_shape=jax.ShapeDtypeStruct((n_rows // R, R * D_MODEL), BF16),
        compiler_params=_cparams(("arbitrary", "arbitrary")),
        name="fourier_stage2",
    )(p6, b2, csm)


def _fft_ctx_kernel(z_ref, a_ref, cs_ref, yin_ref, o_ref):
    del yin_ref
    cc = cs_ref[0:FN_GC, :]
    sc = cs_ref[FN_GC:2 * FN_GC, :]
    norm = 1.0 / math.sqrt(CTX_LEN * FN_GC)
    for g in range(FN_GROUPS):
        p = _dot(a_ref[...], z_ref[g])
        y = _dot(p[0:CTX_LEN].astype(BF16), cc) + _dot(p[CTX_LEN:2 * CTX_LEN].astype(BF16), sc)
        o_ref[:, g * FN_GC:(g + 1) * FN_GC] = (y * norm).astype(o_ref.dtype)


def _fft_ctx(fn, actx, csm, yf, n_batch):
    n_rows = yf.shape[0]
    ctx_base = n_batch * (SEQ // CTX_LEN)
    return pl.pallas_call(
        _fft_ctx_kernel,
        grid=(n_batch,),
        in_specs=[
            pl.BlockSpec((FN_GROUPS, CTX_LEN, FN_GC), lambda b: (0, ctx_base + b, 0)),
            pl.BlockSpec((2 * CTX_LEN, CTX_LEN), lambda b: (0, 0)),
            pl.BlockSpec((2 * FN_GC, FN_GC), lambda b: (0, 0)),
            pl.BlockSpec(memory_space=pl.ANY),
        ],
        out_specs=pl.BlockSpec((CTX_LEN, D_MODEL), lambda b: (ctx_base + b, 0)),
        out_shape=jax.ShapeDtypeStruct((n_rows, D_MODEL), BF16),
        input_output_aliases={3: 0},
        compiler_params=_cparams(("arbitrary",)),
        name="fourier_ctx",
    )(fn, actx, csm, yf)


def _fourier(fn, tables, need_ctx, n_batch):
    a1, b2, csm, actx = tables
    n_rows = fn.shape[1]
    R = FFT_R
    p = _fft1(fn.reshape(FN_GROUPS, n_rows // R, R * FN_GC), a1, n_batch)
    yf = _fft2(p.reshape(FN_GROUPS, n_batch, 2, R, R, FN_GC), b2, csm, n_rows, n_batch)
    yf = yf.reshape(n_rows, D_MODEL)
    if need_ctx:
        yf = _fft_ctx(fn, actx, csm, yf, n_batch)
    return yf


def _merge_kernel(hf_ref, hb_ref, o_ref, gm_ref, gd_ref, gf_ref, yd_ref, yf_ref, xl_ref, xc_ref, mod_ref, hg_ref,
                  wml_ref, wda_ref, wfn_ref, wout_ref, out_ref, *, rows_per_batch, n_batch):
    i = pl.program_id(0)
    x = jnp.where(i < rows_per_batch * n_batch, xl_ref[...], xc_ref[...])
    hsum = hf_ref[...].astype(F32) + hb_ref[...].astype(F32)
    hg = hg_ref[...]
    parts = []
    for h in range(ML_HEADS):
        hs = slice(h * ML_DK, (h + 1) * ML_DK)
        parts.append(_rms(hsum[:, hs], hg[:, hs]))
    ym = (jnp.concatenate(parts, axis=1) * jax.nn.sigmoid(o_ref[...].astype(F32))).astype(BF16)
    y = (jax.nn.sigmoid(gm_ref[...].astype(F32)) * _dot(ym, wml_ref[...])
         + jax.nn.sigmoid(gd_ref[...].astype(F32)) * _dot(yd_ref[...], wda_ref[...])
         + jax.nn.sigmoid(gf_ref[...].astype(F32)) * _dot(yf_ref[...], wfn_ref[...]))
    gate = _mod_row(mod_ref, i, rows_per_batch, n_batch, 2)
    out_ref[...] = x + gate * _dot(y.astype(BF16), wout_ref[...])


def _merge(hf, hb, u, yd, yf, x_lat, x_ctx, ctx_blk0, mods, head_g, wml, wda, wfn, wout, layer, need_ctx, n_batch):
    n_rows = n_batch * (SEQ + CTX_LEN)
    tm = _row_tile(n_batch)
    rows_per_batch = SEQ // tm
    lat_blocks = n_batch * rows_per_batch
    ni = (n_rows if need_ctx else n_batch * SEQ) // tm
    kern = functools.partial(_merge_kernel, rows_per_batch=rows_per_batch, n_batch=n_batch)
    row = lambda i: (i, 0)
    full = lambda i: (0, 0)
    wspec = pl.BlockSpec((None, D_MODEL, D_MODEL), lambda i: (layer, 0, 0))
    return pl.pallas_call(
        kern,
        grid=(ni,),
        in_specs=[
            pl.BlockSpec((tm, D_MODEL), row),
            pl.BlockSpec((tm, D_MODEL), row),
            pl.BlockSpec((tm, D_MODEL), lambda i: (i, U_OML)),
            pl.BlockSpec((tm, D_MODEL), lambda i: (i, U_GPRE)),
            pl.BlockSpec((tm, D_MODEL), lambda i: (i, U_GPRE + 1)),
            pl.BlockSpec((tm, D_MODEL), lambda i: (i, U_GPRE + 2)),
            pl.BlockSpec((tm, D_MODEL), row),
            pl.BlockSpec((tm, D_MODEL), row),
            pl.BlockSpec((tm, D_MODEL), lambda i: (jnp.minimum(i, lat_blocks - 1), 0)),
            pl.BlockSpec((tm, D_MODEL), lambda i: (ctx_blk0 + jnp.maximum(i - lat_blocks, 0), 0)),
            pl.BlockSpec((None, 8, N_MOD), lambda i: (layer, 0, 0)),
            pl.BlockSpec((1, D_MODEL), full),
            wspec, wspec, wspec, wspec,
        ],
        out_specs=pl.BlockSpec((tm, D_MODEL), row),
        out_shape=jax.ShapeDtypeStruct((n_rows, D_MODEL), F32),
        compiler_params=_cparams(("arbitrary",)),
        name="merge_out_proj",
    )(hf, hb, u, u, u, u, yd, yf, x_lat, x_ctx, mods, head_g, wml, wda, wfn, wout)


FFN_CHUNKS = ((0, 1024), (1024, 1024), (2048, 768))


def _ffn_kernel(x_ref, mod_ref, g_ref, win_ref, wout_ref, fg_ref, out_ref, *, rows_per_batch, n_batch, final):
    i = pl.program_id(0)
    x = x_ref[...]
    shift = _mod_row(mod_ref, i, rows_per_batch, n_batch, 3)
    scale = _mod_row(mod_ref, i, rows_per_batch, n_batch, 4)
    gate = _mod_row(mod_ref, i, rows_per_batch, n_batch, 5)
    h = (_rms(x, g_ref[...]) * (1.0 + scale) + shift).astype(BF16)
    acc = None
    for lo, width in FFN_CHUNKS:
        a = _dot(h, win_ref[:, lo:lo + width])
        b = _dot(h, win_ref[:, D_FF + lo:D_FF + lo + width])
        act = (a * jax.nn.sigmoid(a) * b).astype(BF16)
        part = _dot(act, wout_ref[lo:lo + width, :])
        acc = part if acc is None else acc + part
    xn = x + gate * acc
    out_ref[...] = _rms(xn, fg_ref[...]) if final else xn


def _ffn(x, mods, g, w_in, w_out, layer, final_g, final, n_rows_out, n_batch):
    tm = _row_tile(n_batch)
    rows_per_batch = SEQ // tm
    kern = functools.partial(_ffn_kernel, rows_per_batch=rows_per_batch, n_batch=n_batch, final=final)
    row = lambda i: (i, 0)
    full = lambda i: (0, 0)
    return pl.pallas_call(
        kern,
        grid=(n_rows_out // tm,),
        in_specs=[
            pl.BlockSpec((tm, D_MODEL), row),
            pl.BlockSpec((None, 8, N_MOD), lambda i: (layer, 0, 0)),
            pl.BlockSpec((1, D_MODEL), full),
            pl.BlockSpec((None, D_MODEL, 2 * D_FF), lambda i: (layer, 0, 0)),
            pl.BlockSpec((None, D_FF, D_MODEL), lambda i: (layer, 0, 0)),
            pl.BlockSpec((1, D_MODEL), full),
        ],
        out_specs=pl.BlockSpec((tm, D_MODEL), row),
        out_shape=jax.ShapeDtypeStruct((n_rows_out, D_MODEL), F32),
        compiler_params=_cparams(("arbitrary",)),
        name="swiglu_ffn",
    )(x, mods, g, w_in, w_out, final_g)


def _da_col_perm(w):
    half = DA_DH // 2
    lead = w.shape[:-1]
    return jnp.swapaxes(w.reshape(lead + (DA_HEADS, 2, 2, half)), -3, -2).reshape(lead + (DA_HEADS * DA_DV,))


def _rope_tables(pad):
    n_freq = DA_DH // 4
    rows = SEQ // GRID_W
    inv = ROPE_BASE ** (-jnp.arange(n_freq, dtype=F32) / n_freq)
    r = jnp.repeat(jnp.arange(rows, dtype=F32), GRID_W)
    col = jnp.tile(jnp.arange(GRID_W, dtype=F32), rows)
    ang = jnp.concatenate([r[:, None] * inv, col[:, None] * inv], axis=-1)
    cos, sin = jnp.cos(ang), jnp.sin(ang)
    cos_t = jnp.concatenate([cos, cos, cos, cos], axis=-1)
    sin_t = jnp.concatenate([-sin, -sin, sin, sin], axis=-1)
    cos_t = jnp.concatenate([cos_t, jnp.ones((pad, 128), F32)], axis=0)
    sin_t = jnp.concatenate([sin_t, jnp.zeros((pad, 128), F32)], axis=0)
    return cos_t, sin_t


def kernel(x, c, ctx, c_ctx, w_ada, b_ada, norm_g, w_in, ml_gate_b, ml_head_g, da_lam, da_head_g,
           w_br_ml, w_br_da, w_br_fn, w_out, w_ffn_in, w_ffn_out, final_g):
    n_batch = x.shape[0]
    n_lat = n_batch * SEQ
    x_lat = x.reshape(n_lat, D_MODEL)
    x_ctx = ctx.reshape(n_batch * CTX_LEN, D_MODEL)
    cc = jnp.concatenate([c, c_ctx[None, :], jnp.zeros((8 - n_batch - 1, D_MODEL), F32)], axis=0)
    mods = _mods(cc, w_ada, b_ada)
    cos_t, sin_t = _rope_tables(n_batch * CTX_LEN)
    tables = _dft_tables()
    final_g2 = final_g.reshape(1, D_MODEL)
    wb_ml, wb_da, wb_fn, wb_out = _to_bf16(w_br_ml), _to_bf16(w_br_da), _to_bf16(w_br_fn), _to_bf16(w_out)
    wb_ffn_in, wb_ffn_out = _to_bf16(w_ffn_in), _to_bf16(w_ffn_out)

    gate_lo = 4 * D_MODEL
    da_lo = gate_lo + N_GATE
    fn_lo = da_lo + 3 * D_MODEL
    w_main = jnp.concatenate([w_in[..., :D_MODEL], w_in[..., 2 * D_MODEL:gate_lo],
                              _da_col_perm(w_in[..., da_lo:da_lo + D_MODEL]),
                              _da_col_perm(w_in[..., da_lo + D_MODEL:da_lo + 2 * D_MODEL]),
                              w_in[..., da_lo + 2 * D_MODEL:fn_lo],
                              w_in[..., fn_lo + D_MODEL:], w_in[..., fn_lo:fn_lo + D_MODEL]], axis=-1).astype(BF16)
    w_kt = jnp.swapaxes(w_in[..., D_MODEL:2 * D_MODEL], 1, 2).astype(BF16)
    w_gate_t = jnp.swapaxes(w_in[..., gate_lo:da_lo], 1, 2).astype(BF16)

    tm_in = n_batch * CTX_LEN
    tm_tok = _row_tile(n_batch)
    xs = None
    for l in range(DEPTH):
        need_ctx = l < DEPTH - 1
        lam_init = 0.8 - 0.6 * math.exp(-0.3 * l)
        xl, xc = (x_lat, x_ctx) if xs is None else (xs, xs)
        u, kt, fn, gates_t = _inproj(xl, xc, 0 if xs is None else n_lat // tm_in, mods,
                                     norm_g[l, 0].reshape(1, D_MODEL), w_main, w_kt, w_gate_t, l,
                                     cos_t, sin_t, n_batch)
        hf, hb = _mlstm(u, kt, gates_t, ml_gate_b[l], n_batch)
        yd = _attn(u, da_lam[l], da_head_g[l].reshape(1, D_MODEL), lam_init, need_ctx, n_batch)
        yf = _fourier(fn, tables, need_ctx, n_batch)
        xs = _merge(hf, hb, u, yd, yf, xl, xc, 0 if xs is None else n_lat // tm_tok, mods,
                    ml_head_g[l].reshape(1, D_MODEL), wb_ml, wb_da, wb_fn, wb_out, l, need_ctx, n_batch)
        final = l == DEPTH - 1
        n_out = n_lat if final else xs.shape[0]
        xs = _ffn(xs, mods, norm_g[l, 1].reshape(1, D_MODEL), wb_ffn_in, wb_ffn_out, l,
                  final_g2, final, n_out, n_batch)
    return xs.reshape(n_batch, SEQ, D_MODEL)
```

```python
import functools
import math

import jax
import jax.numpy as jnp
from jax import lax
from jax.experimental import pallas as pl
from jax.experimental.pallas import tpu as pltpu

D_MODEL = 1024
SEQ = 4096
DEPTH = 4
CTX_LEN = 256
GRID_W = 64
NORM_EPS = 1e-6

ML_HEADS = 4
ML_DK = 256
ML_CHUNK = 128

DA_HEADS = 8
DA_DH = 64
DA_DV = 2 * DA_DH
ROPE_BASE = 10000.0
ATTN_TQ = 512
ATTN_ROW_CHUNKS = 4

FN_GROUPS = 4
FN_GC = 256
FFT_R = 64

D_FF = 2816
N_GATE = 4 * ML_HEADS
N_MOD = 6 * D_MODEL

U_QML, U_VML, U_OML, U_QDA, U_KDA, U_VDA, U_GPRE = 0, 1, 2, 3, 4, 5, 6
U_BLOCKS = 9
W_BLOCKS = U_BLOCKS + 1
STEP_KT = 1
N_COL_STEPS = W_BLOCKS + 1

VMEM_LIMIT_V7X = 56 * 1024 * 1024

BF16 = jnp.bfloat16
F32 = jnp.float32


def _cparams(sem):
    return pltpu.CompilerParams(dimension_semantics=sem, vmem_limit_bytes=VMEM_LIMIT_V7X)


def _dot(a, b):
    return jnp.dot(a, b, preferred_element_type=F32)


def _dot_nt(a, b):
    return lax.dot_general(a, b, (((1,), (1,)), ((), ())), preferred_element_type=F32)


def _mod_row(mod_ref, i, rows_per_batch, n_batch, col):
    r = jnp.minimum(i // rows_per_batch, n_batch)
    return mod_ref[pl.ds(r, 1), col * D_MODEL:(col + 1) * D_MODEL]


def _row_tile(n_batch):
    return min(512, n_batch * CTX_LEN)


def _rms(x, g):
    return x * lax.rsqrt(jnp.mean(x * x, axis=-1, keepdims=True) + NORM_EPS) * g


def _cast_kernel(w_ref, o_ref):
    o_ref[...] = w_ref[...].astype(o_ref.dtype)


def _to_bf16(w):
    n_l, rows, cols = w.shape
    tr = 256
    return pl.pallas_call(
        _cast_kernel,
        grid=(n_l, rows // tr),
        in_specs=[pl.BlockSpec((1, tr, cols), lambda l, i: (l, i, 0))],
        out_specs=pl.BlockSpec((1, tr, cols), lambda l, i: (l, i, 0)),
        out_shape=jax.ShapeDtypeStruct(w.shape, BF16),
        compiler_params=_cparams(("arbitrary", "arbitrary")),
        name="cast_bf16",
    )(w)


def _mods_kernel(c_ref, w_ref, b_ref, o_ref):
    c = c_ref[...]
    s = (c * jax.nn.sigmoid(c)).astype(BF16)
    o_ref[0] = _dot(s, w_ref[0].astype(BF16)) + b_ref[0]


def _mods(cc, w_ada, b_ada):
    tn = 1536
    return pl.pallas_call(
        _mods_kernel,
        grid=(DEPTH, N_MOD // tn),
        in_specs=[
            pl.BlockSpec((8, D_MODEL), lambda l, j: (0, 0)),
            pl.BlockSpec((1, D_MODEL, tn), lambda l, j: (l, 0, j)),
            pl.BlockSpec((1, 1, tn), lambda l, j: (l, 0, j)),
        ],
        out_specs=pl.BlockSpec((1, 8, tn), lambda l, j: (l, 0, j)),
        out_shape=jax.ShapeDtypeStruct((DEPTH, 8, N_MOD), F32),
        compiler_params=_cparams(("arbitrary", "arbitrary")),
        name="adaln_mods",
    )(cc, w_ada, b_ada.reshape(DEPTH, 1, N_MOD))


def _inproj_kernel(xl_ref, xc_ref, mod_ref, g_ref, w_ref, wkt_ref, wgt_ref, cos_ref, sin_ref,
                   u_ref, kt_ref, fn_ref, gate_ref, xn_ref, *, rows_per_batch, n_batch):
    i = pl.program_id(0)
    j = pl.program_id(1)

    def normalise(x_ref):
        y = _rms(x_ref[...], g_ref[...])
        shift = _mod_row(mod_ref, i, rows_per_batch, n_batch, 0)
        scale = _mod_row(mod_ref, i, rows_per_batch, n_batch, 1)
        xn_ref[...] = (y * (1.0 + scale) + shift).astype(BF16)
        gate_ref[...] = _dot_nt(wgt_ref[...], xn_ref[...])

    is_lat = i < rows_per_batch * n_batch

    @pl.when(jnp.logical_and(j == 0, is_lat))
    def _():
        normalise(xl_ref)

    @pl.when(jnp.logical_and(j == 0, jnp.logical_not(is_lat)))
    def _():
        normalise(xc_ref)

    def product():
        return _dot(xn_ref[...], w_ref[...])

    blk = jnp.where(j == 0, 0, j - 1)
    is_rope = jnp.logical_or(blk == U_QDA, blk == U_KDA)
    is_plain = jnp.logical_and(j != STEP_KT, jnp.logical_and(jnp.logical_not(is_rope), blk < U_BLOCKS))

    @pl.when(is_plain)
    def _():
        u_ref[...] = product().astype(BF16)

    @pl.when(j == STEP_KT)
    def _():
        kt_ref[...] = (_dot_nt(wkt_ref[...], xn_ref[...]) * (ML_DK ** -0.5)).astype(BF16)

    @pl.when(is_rope)
    def _():
        acc = product()
        cos = cos_ref[...]
        sin = sin_ref[...]
        for t in range(acc.shape[1] // DA_DV):
            sl = slice(t * DA_DV, (t + 1) * DA_DV)
            x = acc[:, sl]
            u_ref[:, sl] = (x * cos + pltpu.roll(x, DA_DV // 2, 1) * sin).astype(BF16)

    @pl.when(blk == U_BLOCKS)
    def _():
        acc = product()
        for g in range(FN_GROUPS):
            z = acc[:, g * FN_GC:(g + 1) * FN_GC]
            fn_ref[g] = z.reshape(z.shape[0] // FFT_R, FFT_R * FN_GC).astype(BF16)


def _inproj(x_lat, x_ctx, ctx_blk, mods, g, w_main, w_kt, w_gate_t, layer, cos_t, sin_t, n_batch):
    tm = n_batch * CTX_LEN
    assert SEQ % tm == 0 and cos_t.shape[0] == SEQ + tm
    n_rows = n_batch * (SEQ + CTX_LEN)
    ni = n_rows // tm
    rows_per_batch = SEQ // tm
    lat_blocks = n_batch * rows_per_batch

    def tab_idx(i, j):
        return (jnp.where(i < lat_blocks, i % rows_per_batch, rows_per_batch), 0)

    def w_blk(j):
        return jnp.where(j == 0, 0, j - 1)

    kern = functools.partial(_inproj_kernel, rows_per_batch=rows_per_batch, n_batch=n_batch)
    return pl.pallas_call(
        kern,
        grid=(ni, N_COL_STEPS),
        in_specs=[
            pl.BlockSpec((tm, D_MODEL), lambda i, j: (jnp.minimum(i, lat_blocks - 1), 0)),
            pl.BlockSpec((tm, D_MODEL), lambda i, j: (ctx_blk, 0)),
            pl.BlockSpec((None, 8, N_MOD), lambda i, j: (layer, 0, 0)),
            pl.BlockSpec((1, D_MODEL), lambda i, j: (0, 0)),
            pl.BlockSpec((None, D_MODEL, D_MODEL), lambda i, j: (layer, 0, w_blk(j))),
            pl.BlockSpec((None, D_MODEL, D_MODEL), lambda i, j: (layer, 0, 0)),
            pl.BlockSpec((None, N_GATE, D_MODEL), lambda i, j: (layer, 0, 0)),
            pl.BlockSpec((tm, 128), tab_idx),
            pl.BlockSpec((tm, 128), tab_idx),
        ],
        out_specs=[
            pl.BlockSpec((tm, D_MODEL), lambda i, j: (i, jnp.minimum(w_blk(j), U_BLOCKS - 1))),
            pl.BlockSpec((D_MODEL, tm), lambda i, j: (0, i)),
            pl.BlockSpec((FN_GROUPS, tm // FFT_R, FFT_R * FN_GC), lambda i, j: (0, i, 0)),
            pl.BlockSpec((N_GATE, tm), lambda i, j: (0, i)),
        ],
        out_shape=[
            jax.ShapeDtypeStruct((n_rows, U_BLOCKS * D_MODEL), BF16),
            jax.ShapeDtypeStruct((D_MODEL, n_rows), BF16),
            jax.ShapeDtypeStruct((FN_GROUPS, n_rows // FFT_R, FFT_R * FN_GC), BF16),
            jax.ShapeDtypeStruct((N_GATE, n_rows), F32),
        ],
        scratch_shapes=[pltpu.VMEM((tm, D_MODEL), BF16)],
        compiler_params=_cparams(("arbitrary", "arbitrary")),
        name="in_proj",
    )(x_lat, x_ctx, mods, g, w_main, w_kt, w_gate_t, cos_t, sin_t)


def _split3(x):
    hi = x.astype(BF16).astype(F32)
    mid = (x - hi).astype(BF16).astype(F32)
    lo = (x - hi - mid).astype(BF16).astype(F32)
    return hi, mid, lo


def _mlstm_kernel(qf_ref, ktf_ref, vf_ref, gtf_ref, gtfn_ref, qb_ref, ktb_ref, vb_ref, gtb_ref, gtbn_ref, bias_ref,
                  hf_ref, hb_ref, *scratch):
    n_st = 2 * ML_HEADS
    cx_refs, bw_refs, pmw_refs = scratch[:n_st], scratch[n_st:2 * n_st], scratch[2 * n_st:3 * n_st]
    m_refs, c_refs, bend_refs, g_refs = (scratch[3 * n_st + 2 * k:3 * n_st + 2 * k + 2] for k in range(4))
    state_refs = cx_refs + m_refs
    s = pl.program_id(1)
    L = ML_CHUNK
    H = ML_HEADS
    W = 128

    @pl.when(s == 0)
    def _():
        for ref in state_refs:
            ref[...] = jnp.zeros_like(ref)

    t_idx = lax.broadcasted_iota(jnp.int32, (L, L), 0)
    s_idx = lax.broadcasted_iota(jnp.int32, (L, L), 1)
    eye = t_idx == s_idx
    sub8 = lax.broadcasted_iota(jnp.int32, (8, W), 0)
    ones_w = jnp.ones((L, W), BF16)
    er = lax.broadcasted_iota(jnp.int32, (4 * L, 2 * W), 0)
    ec = lax.broadcasted_iota(jnp.int32, (4 * L, 2 * W), 1)
    expand = jnp.where((er < 3 * L) == (ec < W), 1.0, 0.0).astype(BF16)

    def running_max_rows(x, d):
        n_tiles = L // 8
        out = [None] * n_tiles
        carry = None
        for j in (range(n_tiles) if d == 0 else range(n_tiles - 1, -1, -1)):
            r = x[8 * j:8 * (j + 1)]
            k = 1
            while k < 8:
                if d == 0:
                    r = jnp.maximum(r, jnp.where(sub8 >= k, pltpu.roll(r, k, 0), -jnp.inf))
                else:
                    r = jnp.maximum(r, jnp.where(sub8 < 8 - k, pltpu.roll(r, 8 - k, 0), -jnp.inf))
                k *= 2
            if carry is not None:
                r = jnp.maximum(r, carry)
            carry = jnp.broadcast_to(r[7:8] if d == 0 else r[0:1], (8, W))
            out[j] = r
        return jnp.concatenate(out, axis=0)

    def gate_part(d, gt_ref):
        before = (t_idx <= s_idx) if d == 0 else (t_idx >= s_idx)
        gt = gt_ref[2 * H * d:2 * H * (d + 1), :] + bias_ref[2 * H * d:2 * H * (d + 1), :]
        i4, f4 = gt[0:H], gt[H:2 * H]
        lf4 = jnp.minimum(f4, 0.0) - jnp.log1p(jnp.exp(-jnp.abs(f4)))
        lf_terms = jnp.concatenate(list(_split3(lf4)) + [jnp.zeros((H, L), F32)], axis=0).astype(BF16)
        cum_rhs = jnp.concatenate([jnp.where(before, 1.0, 0.0).astype(BF16), ones_w], axis=1)
        r = _dot(lf_terms, cum_rhs)
        bx = r[0:H] + r[H:2 * H] + r[2 * H:3 * H]
        b4, bend4 = bx[:, 0:L], bx[:, L:L + W]
        c4 = i4 - b4
        c_refs[d][...] = c4
        bend_refs[d][...] = bend4
        g_refs[d][...] = bend4 - b4 + i4
        col_terms = _split3(b4) + (c4.astype(BF16).astype(F32),)
        for h in range(H):
            diag = jnp.concatenate([jnp.where(eye, x[h:h + 1, :], 0.0) for x in col_terms], axis=1).astype(BF16)
            wide = _dot(diag, expand)
            bw_refs[d * H + h][...] = wide[:, 0:W]
            pmw_refs[d * H + h][...] = running_max_rows(wide[:, W:2 * W], d)

    dirs = ((0, qf_ref, ktf_ref, vf_ref, gtf_ref, gtfn_ref, hf_ref),
            (1, qb_ref, ktb_ref, vb_ref, gtb_ref, gtbn_ref, hb_ref))

    @pl.when(s == 0)
    def _():
        for d, _, _, _, gt_ref, _, _ in dirs:
            gate_part(d, gt_ref)

    for d, q_ref, kt_ref, v_ref, _, _, h_ref in dirs:
        causal = (s_idx <= t_idx) if d == 0 else (s_idx >= t_idx)
        c4, bend4, g4 = c_refs[d][...], bend_refs[d][...], g_refs[d][...]
        m_prev4 = m_refs[d][...]
        m_new4 = jnp.maximum(bend4 + m_prev4, jnp.max(g4, axis=1, keepdims=True))
        m_refs[d][...] = m_new4
        decay4 = jnp.exp(bend4 + m_prev4 - m_new4)
        w4 = jnp.exp(g4 - m_new4)
        for h in range(H):
            st = d * H + h
            hs = slice(h * ML_DK, (h + 1) * ML_DK)
            b_w = bw_refs[st][...]
            m_w = jnp.maximum(pmw_refs[st][...], m_prev4[h:h + 1, :])

            q = q_ref[:, hs]
            kt = kt_ref[hs, :]
            vx = jnp.concatenate([v_ref[:, hs], ones_w], axis=1)
            a = (jnp.where(causal, jnp.exp(c4[h:h + 1, :] - m_w), 0.0) * _dot(q, kt)).astype(BF16)
            cx_prev = cx_refs[st][...]
            qc = _dot(q, cx_prev.astype(BF16))
            av = _dot(a, vx)
            sc_w = jnp.exp(m_prev4[h:h + 1, :] - m_w)
            den = sc_w * qc[:, ML_DK:] + av[:, ML_DK:]
            inv = 1.0 / jnp.maximum(jnp.abs(den), jnp.exp(-(b_w + m_w)))
            for t in range(ML_DK // W):
                ts = slice(t * W, (t + 1) * W)
                h_ref[:, h * ML_DK + t * W:h * ML_DK + (t + 1) * W] = (
                    (sc_w * qc[:, ts] + av[:, ts]) * inv).astype(h_ref.dtype)

            kw = (kt.astype(F32) * w4[h:h + 1, :]).astype(BF16)
            dec = jnp.concatenate([decay4[h:h + 1, :]] * (ML_DK // W + 1), axis=1)
            cx_refs[st][...] = dec * cx_prev + _dot(kw, vx)

    for d, _, _, _, _, gtn_ref, _ in dirs:
        gate_part(d, gtn_ref)


def _mlstm(u, kt, gates_t, gate_b, n_batch):
    n_rows = u.shape[0]
    L = ML_CHUNK
    lat_chunks = SEQ // L
    ctx_chunks = CTX_LEN // L
    n_steps = ctx_chunks + lat_chunks
    ctx_base = n_batch * lat_chunks

    def rowblk(d):
        def f(b, s):
            in_ctx = s < ctx_chunks
            if d == 0:
                c = jnp.where(in_ctx, s, s - ctx_chunks)
            else:
                c = jnp.where(in_ctx, ctx_chunks - 1 - s, lat_chunks - 1 - (s - ctx_chunks))
            return jnp.where(in_ctx, ctx_base + ctx_chunks * b, lat_chunks * b) + c
        return f

    def dir_specs(d):
        rb = rowblk(d)
        return [
            pl.BlockSpec((L, D_MODEL), lambda b, s: (rb(b, s), U_QML)),
            pl.BlockSpec((D_MODEL, L), lambda b, s: (0, rb(b, s))),
            pl.BlockSpec((L, D_MODEL), lambda b, s: (rb(b, s), U_VML)),
            pl.BlockSpec((N_GATE, L), lambda b, s: (0, rb(b, s))),
            pl.BlockSpec((N_GATE, L), lambda b, s: (0, rb(b, jnp.minimum(s + 1, n_steps - 1)))),
        ]

    def out_spec(d):
        rb = rowblk(d)
        return pl.BlockSpec((L, D_MODEL), lambda b, s: (rb(b, s), 0))

    n_st = 2 * ML_HEADS
    return pl.pallas_call(
        _mlstm_kernel,
        grid=(n_batch, n_steps),
        in_specs=dir_specs(0) + dir_specs(1) + [pl.BlockSpec((N_GATE, 1), lambda b, s: (0, 0))],
        out_specs=[out_spec(0), out_spec(1)],
        out_shape=[jax.ShapeDtypeStruct((n_rows, D_MODEL), BF16)] * 2,
        scratch_shapes=(
            [pltpu.VMEM((ML_DK, ML_DK + 128), F32)] * n_st
            + [pltpu.VMEM((L, 128), F32)] * (2 * n_st)
            + [pltpu.VMEM((ML_HEADS, 128), F32)] * 2
            + [pltpu.VMEM((ML_HEADS, L), F32)] * 2
            + [pltpu.VMEM((ML_HEADS, 128), F32)] * 2
            + [pltpu.VMEM((ML_HEADS, L), F32)] * 2),
        compiler_params=_cparams(("arbitrary", "arbitrary")),
        name="mlstm_scan",
    )(u, kt, u, gates_t, gates_t, u, kt, u, gates_t, gates_t, gate_b.reshape(N_GATE, 1))


def _attn_lambda(lam_ref, lam_init):
    lq = lam_ref[...]
    return (jnp.exp(jnp.sum(lq[0:1] * lq[1:2], axis=1, keepdims=True))
            - jnp.exp(jnp.sum(lq[2:3] * lq[3:4], axis=1, keepdims=True)) + lam_init)


def _attn_queries(q_ref):
    q = q_ref[...]
    lane = lax.broadcasted_iota(jnp.int32, (1, DA_DV), 1)
    zero = jnp.zeros_like(q)
    is_map0 = (lane % DA_DH) < (DA_DH // 2)
    q2 = jnp.concatenate([jnp.where(is_map0, q, zero), jnp.where(is_map0, zero, q)], axis=0)
    return q2 * (DA_DH ** -0.5)


def _attn_values(s_chunks, m_chunks, vx, lam, g, lam_init, tq):
    accs = [_dot(jnp.exp(s - m).astype(BF16), vx) for s, m in zip(s_chunks, m_chunks)]
    acc = jnp.concatenate(accs, axis=0)
    o0 = acc[0:tq, 0:DA_DV] * (1.0 / acc[0:tq, DA_DV:DA_DV + 1])
    o1 = acc[tq:, 0:DA_DV] * (1.0 / acc[tq:, DA_DV:DA_DV + 1])
    return _rms(o0 - lam * o1, g) * (1.0 - lam_init)


def _attn_kernel(q_ref, kl_ref, vl_ref, kc_ref, vc_ref, lam_ref, g_ref, o_ref, vx_ref, s_ref, m_ref,
                 *, lam_init, q_blocks):
    t = pl.program_id(0)
    tq = q_ref.shape[0]
    rows = 2 * tq // ATTN_ROW_CHUNKS

    @pl.when(jnp.logical_and(t > 0, (t - 1) % q_blocks == 0))
    def _():
        vx_ref[0:SEQ, 0:DA_DV] = vl_ref[...]
        vx_ref[SEQ:, 0:DA_DV] = vc_ref[...]
        lane_v = lax.broadcasted_iota(jnp.int32, (SEQ + CTX_LEN, DA_DV), 1)
        vx_ref[:, DA_DV:] = jnp.where(lane_v == 0, 1.0, 0.0).astype(BF16)

    def score(slot):
        q2 = _attn_queries(q_ref)
        s_lat = _dot_nt(q2, kl_ref[...])
        s_ctx = _dot_nt(q2, kc_ref[...])
        m_ref[slot] = jnp.maximum(jnp.max(s_lat, axis=1, keepdims=True), jnp.max(s_ctx, axis=1, keepdims=True))
        s_ref[slot, :, 0:SEQ] = s_lat
        s_ref[slot, :, SEQ:] = s_ctx

    def drain(slot):
        s_chunks = [s_ref[slot, c * rows:(c + 1) * rows, :] for c in range(ATTN_ROW_CHUNKS)]
        m_chunks = [m_ref[slot, c * rows:(c + 1) * rows, :] for c in range(ATTN_ROW_CHUNKS)]
        o = _attn_values(s_chunks, m_chunks, vx_ref[...], _attn_lambda(lam_ref, lam_init), g_ref[...], lam_init, tq)
        o_ref[...] = o.astype(o_ref.dtype)

    @pl.when(t == 0)
    def _():
        score(0)

    for parity in (0, 1):
        @pl.when(jnp.logical_and(t > 0, t % 2 == parity))
        def _():
            drain(1 - parity)
            score(parity)


def _attn_ctx_kernel(q_ref, kc_ref, vc_ref, lam_ref, g_ref, yin_ref, o_ref, *, lam_init):
    del yin_ref
    tq = q_ref.shape[0]
    s = _dot_nt(_attn_queries(q_ref), kc_ref[...])
    lane_v = lax.broadcasted_iota(jnp.int32, (CTX_LEN, DA_DV), 1)
    vx = jnp.concatenate([vc_ref[...], jnp.where(lane_v == 0, 1.0, 0.0).astype(BF16)], axis=1)
    o = _attn_values([s], [jnp.max(s, axis=1, keepdims=True)], vx, _attn_lambda(lam_ref, lam_init),
                     g_ref[...], lam_init, tq)
    o_ref[...] = o.astype(o_ref.dtype)


def _attn(u, da_lam, head_g, lam_init, need_ctx, n_batch):
    n_rows = u.shape[0]
    tq = ATTN_TQ
    lat_qblocks = SEQ // tq
    ctx_base = n_batch * (SEQ // CTX_LEN)
    cpb = D_MODEL // DA_DV
    n_keys = SEQ + CTX_LEN

    n_items = n_batch * DA_HEADS * lat_qblocks

    def item(t):
        t = jnp.clip(t, 0, n_items - 1)
        return t // (DA_HEADS * lat_qblocks), (t // lat_qblocks) % DA_HEADS, t % lat_qblocks

    def scored(f):
        return lambda t: f(*item(t))

    def drained(f):
        return lambda t: f(*item(t - 1))

    kern = functools.partial(_attn_kernel, lam_init=lam_init, q_blocks=lat_qblocks)
    yd = pl.pallas_call(
        kern,
        grid=(n_items + 1,),
        in_specs=[
            pl.BlockSpec((tq, DA_DV), scored(lambda b, h, qi: (b * lat_qblocks + qi, U_QDA * cpb + h))),
            pl.BlockSpec((SEQ, DA_DV), scored(lambda b, h, qi: (b, U_KDA * cpb + h))),
            pl.BlockSpec((SEQ, DA_DV), drained(lambda b, h, qi: (b, U_VDA * cpb + h))),
            pl.BlockSpec((CTX_LEN, DA_DV), scored(lambda b, h, qi: (ctx_base + b, U_KDA * cpb + h))),
            pl.BlockSpec((CTX_LEN, DA_DV), drained(lambda b, h, qi: (ctx_base + b, U_VDA * cpb + h))),
            pl.BlockSpec((4, DA_DH), lambda t: (0, 0)),
            pl.BlockSpec((1, DA_DV), drained(lambda b, h, qi: (0, h))),
        ],
        out_specs=pl.BlockSpec((tq, DA_DV), drained(lambda b, h, qi: (b * lat_qblocks + qi, h))),
        out_shape=jax.ShapeDtypeStruct((n_rows, D_MODEL), BF16),
        scratch_shapes=[
            pltpu.VMEM((n_keys, 2 * DA_DV), BF16),
            pltpu.VMEM((2, 2 * tq, n_keys), F32),
            pltpu.VMEM((2, 2 * tq, 1), F32),
        ],
        compiler_params=_cparams(("arbitrary",)),
        name="diff_attn",
    )(u, u, u, u, u, da_lam, head_g)
    if not need_ctx:
        return yd
    kern_ctx = functools.partial(_attn_ctx_kernel, lam_init=lam_init)
    return pl.pallas_call(
        kern_ctx,
        grid=(n_batch, DA_HEADS),
        in_specs=[
            pl.BlockSpec((CTX_LEN, DA_DV), lambda b, h: (ctx_base + b, U_QDA * cpb + h)),
            pl.BlockSpec((CTX_LEN, DA_DV), lambda b, h: (ctx_base + b, U_KDA * cpb + h)),
            pl.BlockSpec((CTX_LEN, DA_DV), lambda b, h: (ctx_base + b, U_VDA * cpb + h)),
            pl.BlockSpec((4, DA_DH), lambda b, h: (0, 0)),
            pl.BlockSpec((1, DA_DV), lambda b, h: (0, h)),
            pl.BlockSpec(memory_space=pl.ANY),
        ],
        out_specs=pl.BlockSpec((CTX_LEN, DA_DV), lambda b, h: (ctx_base + b, h)),
        out_shape=jax.ShapeDtypeStruct((n_rows, D_MODEL), BF16),
        input_output_aliases={5: 0},
        compiler_params=_cparams(("arbitrary", "arbitrary")),
        name="diff_attn_ctx",
    )(u, u, u, da_lam, head_g, yd)


def _dft_tables():
    R = FFT_R

    def cs(num, period):
        ang = (num % period).astype(F32) * (2.0 * math.pi / period)
        return jnp.cos(ang), jnp.sin(ang)

    idx = jnp.arange(R, dtype=jnp.int32)
    c1, s1 = cs(idx[:, None] * idx[None, :], R)
    a1 = jnp.concatenate([c1, -s1], axis=0).astype(BF16)
    f2 = idx[:, None, None]
    f1 = idx[None, :, None]
    t1 = idx[None, None, :]
    mc, ms = cs(t1 * (R * f1 + f2), SEQ)
    b2 = jnp.concatenate([jnp.concatenate([mc, ms], axis=2),
                          jnp.concatenate([-ms, mc], axis=2)], axis=1).astype(BF16)
    ch = jnp.arange(FN_GC, dtype=jnp.int32)
    cc, sc = cs(ch[:, None] * ch[None, :], FN_GC)
    csm = jnp.concatenate([cc, sc], axis=0).astype(BF16)
    actx = jnp.concatenate([cc, -sc], axis=0).astype(BF16)
    return a1, b2, csm, actx


def _fft1_kernel(a_ref, z_ref, p_ref):
    p_ref[0, 0] = _dot(a_ref[...], z_ref[0]).astype(p_ref.dtype)


def _fft1(fn_view, a1, n_batch):
    R = FFT_R
    lanes = R * FN_GC
    lc = lanes
    return pl.pallas_call(
        _fft1_kernel,
        grid=(FN_GROUPS, n_batch, lanes // lc),
        in_specs=[
            pl.BlockSpec((2 * R, R), lambda g, b, c: (0, 0)),
            pl.BlockSpec((1, R, lc), lambda g, b, c: (g, b, c)),
        ],
        out_specs=pl.BlockSpec((1, 1, 2 * R, lc), lambda g, b, c: (g, b, 0, c)),
        out_shape=jax.ShapeDtypeStruct((FN_GROUPS, n_batch, 2 * R, lanes), BF16),
        compiler_params=_cparams(("arbitrary", "arbitrary", "arbitrary")),
        name="fourier_stage1",
    )(a1, fn_view)


def _fft2_kernel(p_ref, b_ref, cs_ref, o_ref, *, f2b):
    R = FFT_R
    cc = cs_ref[0:FN_GC, :]
    sc = cs_ref[FN_GC:2 * FN_GC, :]
    norm = 1.0 / math.sqrt(SEQ * FN_GC)
    for g in range(FN_GROUPS):
        xr, xi = [], []
        for jj in range(f2b):
            stacked = jnp.concatenate([p_ref[g, 0, 0, jj], p_ref[g, 0, 1, jj]], axis=0)
            x = _dot(b_ref[jj], stacked)
            xr.append(x[0:R])
            xi.append(x[R:2 * R])
        y = (_dot(jnp.concatenate(xr, axis=0).astype(BF16), cc)
             + _dot(jnp.concatenate(xi, axis=0).astype(BF16), sc)) * norm
        for jj in range(f2b):
            lo = jj * D_MODEL + g * FN_GC
            o_ref[:, lo:lo + FN_GC] = y[jj * R:(jj + 1) * R].astype(o_ref.dtype)


def _fft2(p6, b2, csm, n_rows, n_batch):
    R = FFT_R
    f2b = 8
    kern = functools.partial(_fft2_kernel, f2b=f2b)
    return pl.pallas_call(
        kern,
        grid=(n_batch, R // f2b),
        in_specs=[
            pl.BlockSpec((FN_GROUPS, 1, 2, f2b, R, FN_GC), lambda b, f: (0, b, 0, f, 0, 0)),
            pl.BlockSpec((f2b, 2 * R, 2 * R), lambda b, f: (f, 0, 0)),
            pl.BlockSpec((2 * FN_GC, FN_GC), lambda b, f: (0, 0)),
        ],
        out_specs=pl.BlockSpec((R, f2b * D_MODEL), lambda b, f: (b, f)),
        out_shape=jax.ShapeDtypeStruct((n_rows // R, R * D_MODEL), BF16),
        compiler_params=_cparams(("arbitrary", "arbitrary")),
        name="fourier_stage2",
    )(p6, b2, csm)


def _fft_ctx_kernel(z_ref, a_ref, cs_ref, yin_ref, o_ref, *, n_batch):
    del yin_ref
    cc = cs_ref[0:FN_GC, :]
    sc = cs_ref[FN_GC:2 * FN_GC, :]
    norm = 1.0 / math.sqrt(CTX_LEN * FN_GC)
    z_all = z_ref[0].astype(F32).reshape(n_batch * CTX_LEN, FN_GC)
    for b in range(n_batch):
        rows = slice(b * CTX_LEN, (b + 1) * CTX_LEN)
        p = _dot(a_ref[...], z_all[rows].astype(BF16))
        y = _dot(p[0:CTX_LEN].astype(BF16), cc) + _dot(p[CTX_LEN:2 * CTX_LEN].astype(BF16), sc)
        o_ref[rows, :] = (y * norm).astype(o_ref.dtype)


def _fft_ctx(fn_view, actx, csm, yf, n_batch):
    n_rows = yf.shape[0]
    n_ctx = n_batch * CTX_LEN
    ctx_blk = n_batch * SEQ // n_ctx
    kern = functools.partial(_fft_ctx_kernel, n_batch=n_batch)
    return pl.pallas_call(
        kern,
        grid=(FN_GROUPS,),
        in_specs=[
            pl.BlockSpec((1, n_ctx // FFT_R, FFT_R * FN_GC), lambda g: (g, ctx_blk, 0)),
            pl.BlockSpec((2 * CTX_LEN, CTX_LEN), lambda g: (0, 0)),
            pl.BlockSpec((2 * FN_GC, FN_GC), lambda g: (0, 0)),
            pl.BlockSpec(memory_space=pl.ANY),
        ],
        out_specs=pl.BlockSpec((n_ctx, FN_GC), lambda g: (ctx_blk, g)),
        out_shape=jax.ShapeDtypeStruct((n_rows, D_MODEL), BF16),
        input_output_aliases={3: 0},
        compiler_params=_cparams(("arbitrary",)),
        name="fourier_ctx",
    )(fn_view, actx, csm, yf)


def _fourier(fn, tables, need_ctx, n_batch):
    a1, b2, csm, actx = tables
    R = FFT_R
    n_rows = fn.shape[1] * R
    p = _fft1(fn, a1, n_batch)
    yf = _fft2(p.reshape(FN_GROUPS, n_batch, 2, R, R, FN_GC), b2, csm, n_rows, n_batch)
    yf = yf.reshape(n_rows, D_MODEL)
    if need_ctx:
        yf = _fft_ctx(fn, actx, csm, yf, n_batch)
    return yf


def _merge_kernel(hf_ref, hb_ref, o_ref, gm_ref, gd_ref, gf_ref, yd_ref, yf_ref, xl_ref, xc_ref, mod_ref, hg_ref,
                  wml_ref, wda_ref, wfn_ref, wout_ref, out_ref, *, rows_per_batch, n_batch):
    i = pl.program_id(0)
    x = jnp.where(i < rows_per_batch * n_batch, xl_ref[...], xc_ref[...])
    hsum = hf_ref[...].astype(F32) + hb_ref[...].astype(F32)
    hg = hg_ref[...]
    parts = []
    for h in range(ML_HEADS):
        hs = slice(h * ML_DK, (h + 1) * ML_DK)
        parts.append(_rms(hsum[:, hs], hg[:, hs]))
    ym = (jnp.concatenate(parts, axis=1) * jax.nn.sigmoid(o_ref[...].astype(F32))).astype(BF16)
    y = (jax.nn.sigmoid(gm_ref[...].astype(F32)) * _dot(ym, wml_ref[...])
         + jax.nn.sigmoid(gd_ref[...].astype(F32)) * _dot(yd_ref[...], wda_ref[...])
         + jax.nn.sigmoid(gf_ref[...].astype(F32)) * _dot(yf_ref[...], wfn_ref[...]))
    gate = _mod_row(mod_ref, i, rows_per_batch, n_batch, 2)
    out_ref[...] = x + gate * _dot(y.astype(BF16), wout_ref[...])


def _merge(hf, hb, u, yd, yf, x_lat, x_ctx, ctx_blk0, mods, head_g, wml, wda, wfn, wout, layer, need_ctx, n_batch):
    n_rows = n_batch * (SEQ + CTX_LEN)
    tm = _row_tile(n_batch)
    rows_per_batch = SEQ // tm
    lat_blocks = n_batch * rows_per_batch
    ni = (n_rows if need_ctx else n_batch * SEQ) // tm
    kern = functools.partial(_merge_kernel, rows_per_batch=rows_per_batch, n_batch=n_batch)
    row = lambda i: (i, 0)
    full = lambda i: (0, 0)
    wspec = pl.BlockSpec((None, D_MODEL, D_MODEL), lambda i: (layer, 0, 0))
    return pl.pallas_call(
        kern,
        grid=(ni,),
        in_specs=[
            pl.BlockSpec((tm, D_MODEL), row),
            pl.BlockSpec((tm, D_MODEL), row),
            pl.BlockSpec((tm, D_MODEL), lambda i: (i, U_OML)),
            pl.BlockSpec((tm, D_MODEL), lambda i: (i, U_GPRE)),
            pl.BlockSpec((tm, D_MODEL), lambda i: (i, U_GPRE + 1)),
            pl.BlockSpec((tm, D_MODEL), lambda i: (i, U_GPRE + 2)),
            pl.BlockSpec((tm, D_MODEL), row),
            pl.BlockSpec((tm, D_MODEL), row),
            pl.BlockSpec((tm, D_MODEL), lambda i: (jnp.minimum(i, lat_blocks - 1), 0)),
            pl.BlockSpec((tm, D_MODEL), lambda i: (ctx_blk0 + jnp.maximum(i - lat_blocks, 0), 0)),
            pl.BlockSpec((None, 8, N_MOD), lambda i: (layer, 0, 0)),
            pl.BlockSpec((1, D_MODEL), full),
            wspec, wspec, wspec, wspec,
        ],
        out_specs=pl.BlockSpec((tm, D_MODEL), row),
        out_shape=jax.ShapeDtypeStruct((n_rows, D_MODEL), F32),
        compiler_params=_cparams(("arbitrary",)),
        name="merge_out_proj",
    )(hf, hb, u, u, u, u, yd, yf, x_lat, x_ctx, mods, head_g, wml, wda, wfn, wout)


FFN_CHUNKS = ((0, 1024), (1024, 1024), (2048, 768))


def _ffn_kernel(x_ref, mod_ref, g_ref, win_ref, wout_ref, fg_ref, out_ref, *, rows_per_batch, n_batch, final):
    i = pl.program_id(0)
    x = x_ref[...]
    shift = _mod_row(mod_ref, i, rows_per_batch, n_batch, 3)
    scale = _mod_row(mod_ref, i, rows_per_batch, n_batch, 4)
    gate = _mod_row(mod_ref, i, rows_per_batch, n_batch, 5)
    h = (_rms(x, g_ref[...]) * (1.0 + scale) + shift).astype(BF16)
    acc = None
    for lo, width in FFN_CHUNKS:
        a = _dot(h, win_ref[:, lo:lo + width])
        b = _dot(h, win_ref[:, D_FF + lo:D_FF + lo + width])
        act = (a * jax.nn.sigmoid(a) * b).astype(BF16)
        part = _dot(act, wout_ref[lo:lo + width, :])
        acc = part if acc is None else acc + part
    xn = x + gate * acc
    out_ref[...] = _rms(xn, fg_ref[...]) if final else xn


def _ffn(x, mods, g, w_in, w_out, layer, final_g, final, n_rows_out, n_batch):
    tm = _row_tile(n_batch)
    rows_per_batch = SEQ // tm
    kern = functools.partial(_ffn_kernel, rows_per_batch=rows_per_batch, n_batch=n_batch, final=final)
    row = lambda i: (i, 0)
    full = lambda i: (0, 0)
    return pl.pallas_call(
        kern,
        grid=(n_rows_out // tm,),
        in_specs=[
            pl.BlockSpec((tm, D_MODEL), row),
            pl.BlockSpec((None, 8, N_MOD), lambda i: (layer, 0, 0)),
            pl.BlockSpec((1, D_MODEL), full),
            pl.BlockSpec((None, D_MODEL, 2 * D_FF), lambda i: (layer, 0, 0)),
            pl.BlockSpec((None, D_FF, D_MODEL), lambda i: (layer, 0, 0)),
            pl.BlockSpec((1, D_MODEL), full),
        ],
        out_specs=pl.BlockSpec((tm, D_MODEL), row),
        out_shape=jax.ShapeDtypeStruct((n_rows_out, D_MODEL), F32),
        compiler_params=_cparams(("arbitrary",)),
        name="swiglu_ffn",
    )(x, mods, g, w_in, w_out, final_g)


def _da_col_perm(w):
    half = DA_DH // 2
    lead = w.shape[:-1]
    return jnp.swapaxes(w.reshape(lead + (DA_HEADS, 2, 2, half)), -3, -2).reshape(lead + (DA_HEADS * DA_DV,))


def _rope_tables(pad):
    n_freq = DA_DH // 4
    rows = SEQ // GRID_W
    inv = ROPE_BASE ** (-jnp.arange(n_freq, dtype=F32) / n_freq)
    r = jnp.repeat(jnp.arange(rows, dtype=F32), GRID_W)
    col = jnp.tile(jnp.arange(GRID_W, dtype=F32), rows)
    ang = jnp.concatenate([r[:, None] * inv, col[:, None] * inv], axis=-1)
    cos, sin = jnp.cos(ang), jnp.sin(ang)
    cos_t = jnp.concatenate([cos, cos, cos, cos], axis=-1)
    sin_t = jnp.concatenate([-sin, -sin, sin, sin], axis=-1)
    cos_t = jnp.concatenate([cos_t, jnp.ones((pad, 128), F32)], axis=0)
    sin_t = jnp.concatenate([sin_t, jnp.zeros((pad, 128), F32)], axis=0)
    return cos_t, sin_t


def kernel(x, c, ctx, c_ctx, w_ada, b_ada, norm_g, w_in, ml_gate_b, ml_head_g, da_lam, da_head_g,
           w_br_ml, w_br_da, w_br_fn, w_out, w_ffn_in, w_ffn_out, final_g):
    n_batch = x.shape[0]
    n_lat = n_batch * SEQ
    x_lat = x.reshape(n_lat, D_MODEL)
    x_ctx = ctx.reshape(n_batch * CTX_LEN, D_MODEL)
    cc = jnp.concatenate([c, c_ctx[None, :], jnp.zeros((8 - n_batch - 1, D_MODEL), F32)], axis=0)
    mods = _mods(cc, w_ada, b_ada)
    cos_t, sin_t = _rope_tables(n_batch * CTX_LEN)
    tables = _dft_tables()
    final_g2 = final_g.reshape(1, D_MODEL)
    wb_ml, wb_da, wb_fn, wb_out = _to_bf16(w_br_ml), _to_bf16(w_br_da), _to_bf16(w_br_fn), _to_bf16(w_out)
    wb_ffn_in, wb_ffn_out = _to_bf16(w_ffn_in), _to_bf16(w_ffn_out)

    gate_lo = 4 * D_MODEL
    da_lo = gate_lo + N_GATE
    fn_lo = da_lo + 3 * D_MODEL
    w_main = jnp.concatenate([w_in[..., :D_MODEL], w_in[..., 2 * D_MODEL:gate_lo],
                              _da_col_perm(w_in[..., da_lo:da_lo + D_MODEL]),
                              _da_col_perm(w_in[..., da_lo + D_MODEL:da_lo + 2 * D_MODEL]),
                              w_in[..., da_lo + 2 * D_MODEL:fn_lo],
                              w_in[..., fn_lo + D_MODEL:], w_in[..., fn_lo:fn_lo + D_MODEL]], axis=-1).astype(BF16)
    w_kt = jnp.swapaxes(w_in[..., D_MODEL:2 * D_MODEL], 1, 2).astype(BF16)
    w_gate_t = jnp.swapaxes(w_in[..., gate_lo:da_lo], 1, 2).astype(BF16)

    tm_in = n_batch * CTX_LEN
    tm_tok = _row_tile(n_batch)
    xs = None
    for l in range(DEPTH):
        need_ctx = l < DEPTH - 1
        lam_init = 0.8 - 0.6 * math.exp(-0.3 * l)
        xl, xc = (x_lat, x_ctx) if xs is None else (xs, xs)
        u, kt, fn, gates_t = _inproj(xl, xc, 0 if xs is None else n_lat // tm_in, mods,
                                     norm_g[l, 0].reshape(1, D_MODEL), w_main, w_kt, w_gate_t, l,
                                     cos_t, sin_t, n_batch)
        hf, hb = _mlstm(u, kt, gates_t, ml_gate_b[l], n_batch)
        yd = _attn(u, da_lam[l], da_head_g[l].reshape(1, D_MODEL), lam_init, need_ctx, n_batch)
        yf = _fourier(fn, tables, need_ctx, n_batch)
        xs = _merge(hf, hb, u, yd, yf, xl, xc, 0 if xs is None else n_lat // tm_tok, mods,
                    ml_head_g[l].reshape(1, D_MODEL), wb_ml, wb_da, wb_fn, wb_out, l, need_ctx, n_batch)
        final = l == DEPTH - 1
        n_out = n_lat if final else xs.shape[0]
        xs = _ffn(xs, mods, norm_g[l, 1].reshape(1, D_MODEL), wb_ffn_in, wb_ffn_out, l,
                  final_g2, final, n_out, n_batch)
    return xs.reshape(n_batch, SEQ, D_MODEL)
```

```python
import functools
import math

import jax
import jax.numpy as jnp
from jax import lax
from jax.experimental import pallas as pl
from jax.experimental.pallas import tpu as pltpu

D_MODEL = 1024
SEQ = 4096
DEPTH = 4
CTX_LEN = 256
GRID_W = 64
NORM_EPS = 1e-6

ML_HEADS = 4
ML_DK = 256
ML_CHUNK = 128

DA_HEADS = 8
DA_DH = 64
DA_DV = 2 * DA_DH
ROPE_BASE = 10000.0
ATTN_TQ = 512
ATTN_ROW_CHUNKS = 4

FN_GROUPS = 4
FN_GC = 256
FFT_R = 64

D_FF = 2816
N_GATE = 4 * ML_HEADS
N_MOD = 6 * D_MODEL

U_QML, U_VML, U_OML, U_QDA, U_KDA, U_VDA, U_GPRE = 0, 1, 2, 3, 4, 5, 6
U_BLOCKS = 9
W_BLOCKS = U_BLOCKS + 1
STEP_KT = 1
N_COL_STEPS = W_BLOCKS + 1

VMEM_LIMIT_V7X = 56 * 1024 * 1024

BF16 = jnp.bfloat16
F32 = jnp.float32


def _cparams(sem):
    return pltpu.CompilerParams(dimension_semantics=sem, vmem_limit_bytes=VMEM_LIMIT_V7X)


def _dot(a, b):
    return jnp.dot(a, b, preferred_element_type=F32)


def _dot_nt(a, b):
    return lax.dot_general(a, b, (((1,), (1,)), ((), ())), preferred_element_type=F32)


def _mod_row(mod_ref, i, rows_per_batch, n_batch, col):
    r = jnp.minimum(i // rows_per_batch, n_batch)
    return mod_ref[pl.ds(r, 1), col * D_MODEL:(col + 1) * D_MODEL]


def _row_tile(n_batch):
    return min(512, n_batch * CTX_LEN)


def _rms(x, g):
    return x * lax.rsqrt(jnp.mean(x * x, axis=-1, keepdims=True) + NORM_EPS) * g


def _cast_kernel(w_ref, o_ref):
    o_ref[...] = w_ref[...].astype(o_ref.dtype)


def _to_bf16(w):
    n_l, rows, cols = w.shape
    tr = 256
    return pl.pallas_call(
        _cast_kernel,
        grid=(n_l, rows // tr),
        in_specs=[pl.BlockSpec((1, tr, cols), lambda l, i: (l, i, 0))],
        out_specs=pl.BlockSpec((1, tr, cols), lambda l, i: (l, i, 0)),
        out_shape=jax.ShapeDtypeStruct(w.shape, BF16),
        compiler_params=_cparams(("arbitrary", "arbitrary")),
        name="cast_bf16",
    )(w)


def _mods_kernel(c_ref, w_ref, b_ref, o_ref):
    c = c_ref[...]
    s = (c * jax.nn.sigmoid(c)).astype(BF16)
    o_ref[0] = _dot(s, w_ref[0].astype(BF16)) + b_ref[0]


def _mods(cc, w_ada, b_ada):
    tn = 1536
    return pl.pallas_call(
        _mods_kernel,
        grid=(DEPTH, N_MOD // tn),
        in_specs=[
            pl.BlockSpec((8, D_MODEL), lambda l, j: (0, 0)),
            pl.BlockSpec((1, D_MODEL, tn), lambda l, j: (l, 0, j)),
            pl.BlockSpec((1, 1, tn), lambda l, j: (l, 0, j)),
        ],
        out_specs=pl.BlockSpec((1, 8, tn), lambda l, j: (l, 0, j)),
        out_shape=jax.ShapeDtypeStruct((DEPTH, 8, N_MOD), F32),
        compiler_params=_cparams(("arbitrary", "arbitrary")),
        name="adaln_mods",
    )(cc, w_ada, b_ada.reshape(DEPTH, 1, N_MOD))


def _inproj_kernel(xl_ref, xc_ref, mod_ref, g_ref, w_ref, wkt_ref, wgt_ref, cos_ref, sin_ref,
                   u_ref, kt_ref, fn_ref, gate_ref, xn_ref, *, rows_per_batch, n_batch):
    i = pl.program_id(0)
    j = pl.program_id(1)

    def normalise(x_ref):
        y = _rms(x_ref[...], g_ref[...])
        shift = _mod_row(mod_ref, i, rows_per_batch, n_batch, 0)
        scale = _mod_row(mod_ref, i, rows_per_batch, n_batch, 1)
        xn_ref[...] = (y * (1.0 + scale) + shift).astype(BF16)
        gate_ref[...] = _dot_nt(wgt_ref[...], xn_ref[...])

    is_lat = i < rows_per_batch * n_batch

    @pl.when(jnp.logical_and(j == 0, is_lat))
    def _():
        normalise(xl_ref)

    @pl.when(jnp.logical_and(j == 0, jnp.logical_not(is_lat)))
    def _():
        normalise(xc_ref)

    def product():
        return _dot(xn_ref[...], w_ref[...])

    blk = jnp.where(j == 0, 0, j - 1)
    is_rope = jnp.logical_or(blk == U_QDA, blk == U_KDA)
    is_plain = jnp.logical_and(j != STEP_KT, jnp.logical_and(jnp.logical_not(is_rope), blk < U_BLOCKS))

    @pl.when(is_plain)
    def _():
        u_ref[...] = product().astype(BF16)

    @pl.when(j == STEP_KT)
    def _():
        kt_ref[...] = (_dot_nt(wkt_ref[...], xn_ref[...]) * (ML_DK ** -0.5)).astype(BF16)

    @pl.when(is_rope)
    def _():
        acc = product()
        cos = cos_ref[...]
        sin = sin_ref[...]
        for t in range(acc.shape[1] // DA_DV):
            sl = slice(t * DA_DV, (t + 1) * DA_DV)
            x = acc[:, sl]
            u_ref[:, sl] = (x * cos + pltpu.roll(x, DA_DV // 2, 1) * sin).astype(BF16)

    @pl.when(blk == U_BLOCKS)
    def _():
        acc = product()
        for g in range(FN_GROUPS):
            z = acc[:, g * FN_GC:(g + 1) * FN_GC]
            fn_ref[g] = z.reshape(z.shape[0] // FFT_R, FFT_R * FN_GC).astype(BF16)


def _inproj(x_lat, x_ctx, ctx_blk, mods, g, w_main, w_kt, w_gate_t, layer, cos_t, sin_t, n_batch):
    tm = n_batch * CTX_LEN
    assert SEQ % tm == 0 and cos_t.shape[0] == SEQ + tm
    n_rows = n_batch * (SEQ + CTX_LEN)
    ni = n_rows // tm
    rows_per_batch = SEQ // tm
    lat_blocks = n_batch * rows_per_batch

    def tab_idx(i, j):
        return (jnp.where(i < lat_blocks, i % rows_per_batch, rows_per_batch), 0)

    def w_blk(j):
        return jnp.where(j == 0, 0, j - 1)

    kern = functools.partial(_inproj_kernel, rows_per_batch=rows_per_batch, n_batch=n_batch)
    return pl.pallas_call(
        kern,
        grid=(ni, N_COL_STEPS),
        in_specs=[
            pl.BlockSpec((tm, D_MODEL), lambda i, j: (jnp.minimum(i, lat_blocks - 1), 0)),
            pl.BlockSpec((tm, D_MODEL), lambda i, j: (ctx_blk, 0)),
            pl.BlockSpec((None, 8, N_MOD), lambda i, j: (layer, 0, 0)),
            pl.BlockSpec((1, D_MODEL), lambda i, j: (0, 0)),
            pl.BlockSpec((None, D_MODEL, D_MODEL), lambda i, j: (layer, 0, w_blk(j))),
            pl.BlockSpec((None, D_MODEL, D_MODEL), lambda i, j: (layer, 0, 0)),
            pl.BlockSpec((None, N_GATE, D_MODEL), lambda i, j: (layer, 0, 0)),
            pl.BlockSpec((tm, 128), tab_idx),
            pl.BlockSpec((tm, 128), tab_idx),
        ],
        out_specs=[
            pl.BlockSpec((tm, D_MODEL), lambda i, j: (i, jnp.minimum(w_blk(j), U_BLOCKS - 1))),
            pl.BlockSpec((D_MODEL, tm), lambda i, j: (0, i)),
            pl.BlockSpec((FN_GROUPS, tm // FFT_R, FFT_R * FN_GC), lambda i, j: (0, i, 0)),
            pl.BlockSpec((N_GATE, tm), lambda i, j: (0, i)),
        ],
        out_shape=[
            jax.ShapeDtypeStruct((n_rows, U_BLOCKS * D_MODEL), BF16),
            jax.ShapeDtypeStruct((D_MODEL, n_rows), BF16),
            jax.ShapeDtypeStruct((FN_GROUPS, n_rows // FFT_R, FFT_R * FN_GC), BF16),
            jax.ShapeDtypeStruct((N_GATE, n_rows), F32),
        ],
        scratch_shapes=[pltpu.VMEM((tm, D_MODEL), BF16)],
        compiler_params=_cparams(("arbitrary", "arbitrary")),
        name="in_proj",
    )(x_lat, x_ctx, mods, g, w_main, w_kt, w_gate_t, cos_t, sin_t)


def _split3(x):
    hi = x.astype(BF16).astype(F32)
    mid = (x - hi).astype(BF16).astype(F32)
    lo = (x - hi - mid).astype(BF16).astype(F32)
    return hi, mid, lo


def _mlstm_kernel(qf_ref, ktf_ref, vf_ref, gtf_ref, gtfn_ref, qb_ref, ktb_ref, vb_ref, gtb_ref, gtbn_ref, bias_ref,
                  hf_ref, hb_ref, *scratch):
    n_st = 2 * ML_HEADS
    cx_refs, bw_refs, pmw_refs = scratch[:n_st], scratch[n_st:2 * n_st], scratch[2 * n_st:3 * n_st]
    m_refs, c_refs, bend_refs, g_refs = (scratch[3 * n_st + 2 * k:3 * n_st + 2 * k + 2] for k in range(4))
    state_refs = cx_refs + m_refs
    s = pl.program_id(1)
    L = ML_CHUNK
    H = ML_HEADS
    W = 128

    @pl.when(s == 0)
    def _():
        for ref in state_refs:
            ref[...] = jnp.zeros_like(ref)

    t_idx = lax.broadcasted_iota(jnp.int32, (L, L), 0)
    s_idx = lax.broadcasted_iota(jnp.int32, (L, L), 1)
    eye = t_idx == s_idx
    sub8 = lax.broadcasted_iota(jnp.int32, (8, W), 0)
    ones_w = jnp.ones((L, W), BF16)
    er = lax.broadcasted_iota(jnp.int32, (4 * L, 2 * W), 0)
    ec = lax.broadcasted_iota(jnp.int32, (4 * L, 2 * W), 1)
    expand = jnp.where((er < 3 * L) == (ec < W), 1.0, 0.0).astype(BF16)

    def running_max_rows(x, d):
        n_tiles = L // 8
        out = [None] * n_tiles
        carry = None
        for j in (range(n_tiles) if d == 0 else range(n_tiles - 1, -1, -1)):
            r = x[8 * j:8 * (j + 1)]
            k = 1
            while k < 8:
                if d == 0:
                    r = jnp.maximum(r, jnp.where(sub8 >= k, pltpu.roll(r, k, 0), -jnp.inf))
                else:
                    r = jnp.maximum(r, jnp.where(sub8 < 8 - k, pltpu.roll(r, 8 - k, 0), -jnp.inf))
                k *= 2
            if carry is not None:
                r = jnp.maximum(r, carry)
            carry = jnp.broadcast_to(r[7:8] if d == 0 else r[0:1], (8, W))
            out[j] = r
        return jnp.concatenate(out, axis=0)

    def gate_part(d, gt_ref):
        before = (t_idx <= s_idx) if d == 0 else (t_idx >= s_idx)
        gt = gt_ref[2 * H * d:2 * H * (d + 1), :] + bias_ref[2 * H * d:2 * H * (d + 1), :]
        i4, f4 = gt[0:H], gt[H:2 * H]
        lf4 = jnp.minimum(f4, 0.0) - jnp.log1p(jnp.exp(-jnp.abs(f4)))
        lf_terms = jnp.concatenate(list(_split3(lf4)) + [jnp.zeros((H, L), F32)], axis=0).astype(BF16)
        cum_rhs = jnp.concatenate([jnp.where(before, 1.0, 0.0).astype(BF16), ones_w], axis=1)
        r = _dot(lf_terms, cum_rhs)
        bx = r[0:H] + r[H:2 * H] + r[2 * H:3 * H]
        b4, bend4 = bx[:, 0:L], bx[:, L:L + W]
        c4 = i4 - b4
        c_refs[d][...] = c4
        bend_refs[d][...] = bend4
        g_refs[d][...] = bend4 - b4 + i4
        col_terms = _split3(b4) + (c4.astype(BF16).astype(F32),)
        for h in range(H):
            diag = jnp.concatenate([jnp.where(eye, x[h:h + 1, :], 0.0) for x in col_terms], axis=1).astype(BF16)
            wide = _dot(diag, expand)
            bw_refs[d * H + h][...] = wide[:, 0:W]
            pmw_refs[d * H + h][...] = running_max_rows(wide[:, W:2 * W], d)

    dirs = ((0, qf_ref, ktf_ref, vf_ref, gtf_ref, gtfn_ref, hf_ref),
            (1, qb_ref, ktb_ref, vb_ref, gtb_ref, gtbn_ref, hb_ref))

    @pl.when(s == 0)
    def _():
        for d, _, _, _, gt_ref, _, _ in dirs:
            gate_part(d, gt_ref)

    for d, q_ref, kt_ref, v_ref, _, _, h_ref in dirs:
        causal = (s_idx <= t_idx) if d == 0 else (s_idx >= t_idx)
        c4, bend4, g4 = c_refs[d][...], bend_refs[d][...], g_refs[d][...]
        m_prev4 = m_refs[d][...]
        m_new4 = jnp.maximum(bend4 + m_prev4, jnp.max(g4, axis=1, keepdims=True))
        m_refs[d][...] = m_new4
        decay4 = jnp.exp(bend4 + m_prev4 - m_new4)
        w4 = jnp.exp(g4 - m_new4)
        for h in range(H):
            st = d * H + h
            hs = slice(h * ML_DK, (h + 1) * ML_DK)
            b_w = bw_refs[st][...]
            m_w = jnp.maximum(pmw_refs[st][...], m_prev4[h:h + 1, :])

            q = q_ref[:, hs]
            kt = kt_ref[hs, :]
            vx = jnp.concatenate([v_ref[:, hs], ones_w], axis=1)
            a = (jnp.where(causal, jnp.exp(c4[h:h + 1, :] - m_w), 0.0) * _dot(q, kt)).astype(BF16)
            cx_prev = cx_refs[st][...]
            qc = _dot(q, cx_prev.astype(BF16))
            av = _dot(a, vx)
            sc_w = jnp.exp(m_prev4[h:h + 1, :] - m_w)
            den = sc_w * qc[:, ML_DK:] + av[:, ML_DK:]
            inv = 1.0 / jnp.maximum(jnp.abs(den), jnp.exp(-(b_w + m_w)))
            for t in range(ML_DK // W):
                ts = slice(t * W, (t + 1) * W)
                h_ref[:, h * ML_DK + t * W:h * ML_DK + (t + 1) * W] = (
                    (sc_w * qc[:, ts] + av[:, ts]) * inv).astype(h_ref.dtype)

            kw = (kt.astype(F32) * w4[h:h + 1, :]).astype(BF16)
            dec = jnp.concatenate([decay4[h:h + 1, :]] * (ML_DK // W + 1), axis=1)
            cx_refs[st][...] = dec * cx_prev + _dot(kw, vx)

    for d, _, _, _, _, gtn_ref, _ in dirs:
        gate_part(d, gtn_ref)


def _mlstm(u, kt, gates_t, gate_b, n_batch):
    n_rows = u.shape[0]
    L = ML_CHUNK
    lat_chunks = SEQ // L
    ctx_chunks = CTX_LEN // L
    n_steps = ctx_chunks + lat_chunks
    ctx_base = n_batch * lat_chunks

    def rowblk(d):
        def f(b, s):
            in_ctx = s < ctx_chunks
            if d == 0:
                c = jnp.where(in_ctx, s, s - ctx_chunks)
            else:
                c = jnp.where(in_ctx, ctx_chunks - 1 - s, lat_chunks - 1 - (s - ctx_chunks))
            return jnp.where(in_ctx, ctx_base + ctx_chunks * b, lat_chunks * b) + c
        return f

    def dir_specs(d):
        rb = rowblk(d)
        return [
            pl.BlockSpec((L, D_MODEL), lambda b, s: (rb(b, s), U_QML)),
            pl.BlockSpec((D_MODEL, L), lambda b, s: (0, rb(b, s))),
            pl.BlockSpec((L, D_MODEL), lambda b, s: (rb(b, s), U_VML)),
            pl.BlockSpec((N_GATE, L), lambda b, s: (0, rb(b, s))),
            pl.BlockSpec((N_GATE, L), lambda b, s: (0, rb(b, jnp.minimum(s + 1, n_steps - 1)))),
        ]

    def out_spec(d):
        rb = rowblk(d)
        return pl.BlockSpec((L, D_MODEL), lambda b, s: (rb(b, s), 0))

    n_st = 2 * ML_HEADS
    return pl.pallas_call(
        _mlstm_kernel,
        grid=(n_batch, n_steps),
        in_specs=dir_specs(0) + dir_specs(1) + [pl.BlockSpec((N_GATE, 1), lambda b, s: (0, 0))],
        out_specs=[out_spec(0), out_spec(1)],
        out_shape=[jax.ShapeDtypeStruct((n_rows, D_MODEL), BF16)] * 2,
        scratch_shapes=(
            [pltpu.VMEM((ML_DK, ML_DK + 128), F32)] * n_st
            + [pltpu.VMEM((L, 128), F32)] * (2 * n_st)
            + [pltpu.VMEM((ML_HEADS, 128), F32)] * 2
            + [pltpu.VMEM((ML_HEADS, L), F32)] * 2
            + [pltpu.VMEM((ML_HEADS, 128), F32)] * 2
            + [pltpu.VMEM((ML_HEADS, L), F32)] * 2),
        compiler_params=_cparams(("arbitrary", "arbitrary")),
        name="mlstm_scan",
    )(u, kt, u, gates_t, gates_t, u, kt, u, gates_t, gates_t, gate_b.reshape(N_GATE, 1))


def _attn_lambda(lam_ref, lam_init):
    lq = lam_ref[...]
    return (jnp.exp(jnp.sum(lq[0:1] * lq[1:2], axis=1, keepdims=True))
            - jnp.exp(jnp.sum(lq[2:3] * lq[3:4], axis=1, keepdims=True)) + lam_init)


def _attn_queries(q_ref):
    q = q_ref[...]
    lane = lax.broadcasted_iota(jnp.int32, (1, DA_DV), 1)
    zero = jnp.zeros_like(q)
    is_map0 = (lane % DA_DH) < (DA_DH // 2)
    q2 = jnp.concatenate([jnp.where(is_map0, q, zero), jnp.where(is_map0, zero, q)], axis=0)
    return q2 * (DA_DH ** -0.5)


def _attn_values(s_chunks, m_chunks, vx, lam, g, lam_init, tq):
    accs = [_dot(jnp.exp(s - m).astype(BF16), vx) for s, m in zip(s_chunks, m_chunks)]
    acc = jnp.concatenate(accs, axis=0)
    o0 = acc[0:tq, 0:DA_DV] * (1.0 / acc[0:tq, DA_DV:DA_DV + 1])
    o1 = acc[tq:, 0:DA_DV] * (1.0 / acc[tq:, DA_DV:DA_DV + 1])
    return _rms(o0 - lam * o1, g) * (1.0 - lam_init)


def _attn_kernel(q_ref, kl_ref, vl_ref, kc_ref, vc_ref, lam_ref, g_ref, o_ref, vx_ref, s_ref, m_ref,
                 *, lam_init, q_blocks):
    t = pl.program_id(0)
    tq = q_ref.shape[0]
    rows = 2 * tq // ATTN_ROW_CHUNKS

    @pl.when(jnp.logical_and(t > 0, (t - 1) % q_blocks == 0))
    def _():
        vx_ref[0:SEQ, 0:DA_DV] = vl_ref[...]
        vx_ref[SEQ:, 0:DA_DV] = vc_ref[...]
        lane_v = lax.broadcasted_iota(jnp.int32, (SEQ + CTX_LEN, DA_DV), 1)
        vx_ref[:, DA_DV:] = jnp.where(lane_v == 0, 1.0, 0.0).astype(BF16)

    def score(slot):
        q2 = _attn_queries(q_ref)
        s_lat = _dot_nt(q2, kl_ref[...])
        s_ctx = _dot_nt(q2, kc_ref[...])
        m_ref[slot] = jnp.maximum(jnp.max(s_lat, axis=1, keepdims=True), jnp.max(s_ctx, axis=1, keepdims=True))
        s_ref[slot, :, 0:SEQ] = s_lat
        s_ref[slot, :, SEQ:] = s_ctx

    def drain(slot):
        s_chunks = [s_ref[slot, c * rows:(c + 1) * rows, :] for c in range(ATTN_ROW_CHUNKS)]
        m_chunks = [m_ref[slot, c * rows:(c + 1) * rows, :] for c in range(ATTN_ROW_CHUNKS)]
        o = _attn_values(s_chunks, m_chunks, vx_ref[...], _attn_lambda(lam_ref, lam_init), g_ref[...], lam_init, tq)
        o_ref[...] = o.astype(o_ref.dtype)

    @pl.when(t == 0)
    def _():
        score(0)

    for parity in (0, 1):
        @pl.when(jnp.logical_and(t > 0, t % 2 == parity))
        def _():
            drain(1 - parity)
            score(parity)


def _attn_ctx_kernel(q_ref, kc_ref, vc_ref, lam_ref, g_ref, yin_ref, o_ref, *, lam_init):
    del yin_ref
    tq = q_ref.shape[0]
    s = _dot_nt(_attn_queries(q_ref), kc_ref[...])
    lane_v = lax.broadcasted_iota(jnp.int32, (CTX_LEN, DA_DV), 1)
    vx = jnp.concatenate([vc_ref[...], jnp.where(lane_v == 0, 1.0, 0.0).astype(BF16)], axis=1)
    o = _attn_values([s], [jnp.max(s, axis=1, keepdims=True)], vx, _attn_lambda(lam_ref, lam_init),
                     g_ref[...], lam_init, tq)
    o_ref[...] = o.astype(o_ref.dtype)


def _attn(u, da_lam, head_g, lam_init, need_ctx, n_batch):
    n_rows = u.shape[0]
    tq = ATTN_TQ
    lat_qblocks = SEQ // tq
    ctx_base = n_batch * (SEQ // CTX_LEN)
    cpb = D_MODEL // DA_DV
    n_keys = SEQ + CTX_LEN

    n_items = n_batch * DA_HEADS * lat_qblocks

    def item(t):
        t = jnp.clip(t, 0, n_items - 1)
        return t // (DA_HEADS * lat_qblocks), (t // lat_qblocks) % DA_HEADS, t % lat_qblocks

    def scored(f):
        return lambda t: f(*item(t))

    def drained(f):
        return lambda t: f(*item(t - 1))

    kern = functools.partial(_attn_kernel, lam_init=lam_init, q_blocks=lat_qblocks)
    yd = pl.pallas_call(
        kern,
        grid=(n_items + 1,),
        in_specs=[
            pl.BlockSpec((tq, DA_DV), scored(lambda b, h, qi: (b * lat_qblocks + qi, U_QDA * cpb + h))),
            pl.BlockSpec((SEQ, DA_DV), scored(lambda b, h, qi: (b, U_KDA * cpb + h))),
            pl.BlockSpec((SEQ, DA_DV), drained(lambda b, h, qi: (b, U_VDA * cpb + h))),
            pl.BlockSpec((CTX_LEN, DA_DV), scored(lambda b, h, qi: (ctx_base + b, U_KDA * cpb + h))),
            pl.BlockSpec((CTX_LEN, DA_DV), drained(lambda b, h, qi: (ctx_base + b, U_VDA * cpb + h))),
            pl.BlockSpec((4, DA_DH), lambda t: (0, 0)),
            pl.BlockSpec((1, DA_DV), drained(lambda b, h, qi: (0, h))),
        ],
        out_specs=pl.BlockSpec((tq, DA_DV), drained(lambda b, h, qi: (b * lat_qblocks + qi, h))),
        out_shape=jax.ShapeDtypeStruct((n_rows, D_MODEL), BF16),
        scratch_shapes=[
            pltpu.VMEM((n_keys, 2 * DA_DV), BF16),
            pltpu.VMEM((2, 2 * tq, n_keys), F32),
            pltpu.VMEM((2, 2 * tq, 1), F32),
        ],
        compiler_params=_cparams(("arbitrary",)),
        name="diff_attn",
    )(u, u, u, u, u, da_lam, head_g)
    if not need_ctx:
        return yd
    kern_ctx = functools.partial(_attn_ctx_kernel, lam_init=lam_init)
    return pl.pallas_call(
        kern_ctx,
        grid=(n_batch, DA_HEADS),
        in_specs=[
            pl.BlockSpec((CTX_LEN, DA_DV), lambda b, h: (ctx_base + b, U_QDA * cpb + h)),
            pl.BlockSpec((CTX_LEN, DA_DV), lambda b, h: (ctx_base + b, U_KDA * cpb + h)),
            pl.BlockSpec((CTX_LEN, DA_DV), lambda b, h: (ctx_base + b, U_VDA * cpb + h)),
            pl.BlockSpec((4, DA_DH), lambda b, h: (0, 0)),
            pl.BlockSpec((1, DA_DV), lambda b, h: (0, h)),
            pl.BlockSpec(memory_space=pl.ANY),
        ],
        out_specs=pl.BlockSpec((CTX_LEN, DA_DV), lambda b, h: (ctx_base + b, h)),
        out_shape=jax.ShapeDtypeStruct((n_rows, D_MODEL), BF16),
        input_output_aliases={5: 0},
        compiler_params=_cparams(("arbitrary", "arbitrary")),
        name="diff_attn_ctx",
    )(u, u, u, da_lam, head_g, yd)


def _dft_tables():
    R = FFT_R

    def cs(num, period):
        ang = (num % period).astype(F32) * (2.0 * math.pi / period)
        return jnp.cos(ang), jnp.sin(ang)

    idx = jnp.arange(R, dtype=jnp.int32)
    c1, s1 = cs(idx[:, None] * idx[None, :], R)
    a1 = jnp.concatenate([c1, -s1], axis=0).astype(BF16)
    f2 = idx[:, None, None]
    f1 = idx[None, :, None]
    t1 = idx[None, None, :]
    mc, ms = cs(t1 * (R * f1 + f2), SEQ)
    b2 = jnp.concatenate([jnp.concatenate([mc, ms], axis=2),
                          jnp.concatenate([-ms, mc], axis=2)], axis=1).astype(BF16)
    ch = jnp.arange(FN_GC, dtype=jnp.int32)
    cc, sc = cs(ch[:, None] * ch[None, :], FN_GC)
    csm = jnp.concatenate([cc, sc], axis=0).astype(BF16)
    actx = jnp.concatenate([cc, -sc], axis=0).astype(BF16)
    return a1, b2, csm, actx


def _fft1_kernel(a_ref, z_ref, p_ref):
    p_ref[0, 0] = _dot(a_ref[...], z_ref[0]).astype(p_ref.dtype)


def _fft1(fn_view, a1, n_batch):
    R = FFT_R
    lanes = R * FN_GC
    lc = lanes
    return pl.pallas_call(
        _fft1_kernel,
        grid=(FN_GROUPS, n_batch, lanes // lc),
        in_specs=[
            pl.BlockSpec((2 * R, R), lambda g, b, c: (0, 0)),
            pl.BlockSpec((1, R, lc), lambda g, b, c: (g, b, c)),
        ],
        out_specs=pl.BlockSpec((1, 1, 2 * R, lc), lambda g, b, c: (g, b, 0, c)),
        out_shape=jax.ShapeDtypeStruct((FN_GROUPS, n_batch, 2 * R, lanes), BF16),
        compiler_params=_cparams(("arbitrary", "arbitrary", "arbitrary")),
        name="fourier_stage1",
    )(a1, fn_view)


def _fft2_kernel(p_ref, b_ref, cs_ref, o_ref, *, f2b):
    R = FFT_R
    cc = cs_ref[0:FN_GC, :]
    sc = cs_ref[FN_GC:2 * FN_GC, :]
    norm = 1.0 / math.sqrt(SEQ * FN_GC)
    for g in range(FN_GROUPS):
        xr, xi = [], []
        for jj in range(f2b):
            stacked = jnp.concatenate([p_ref[g, 0, 0, jj], p_ref[g, 0, 1, jj]], axis=0)
            x = _dot(b_ref[jj], stacked)
            xr.append(x[0:R])
            xi.append(x[R:2 * R])
        y = (_dot(jnp.concatenate(xr, axis=0).astype(BF16), cc)
             + _dot(jnp.concatenate(xi, axis=0).astype(BF16), sc)) * norm
        for jj in range(f2b):
            lo = jj * D_MODEL + g * FN_GC
            o_ref[:, lo:lo + FN_GC] = y[jj * R:(jj + 1) * R].astype(o_ref.dtype)


def _fft2(p6, b2, csm, n_rows, n_batch):
    R = FFT_R
    f2b = 8
    kern = functools.partial(_fft2_kernel, f2b=f2b)
    return pl.pallas_call(
        kern,
        grid=(n_batch, R // f2b),
        in_specs=[
            pl.BlockSpec((FN_GROUPS, 1, 2, f2b, R, FN_GC), lambda b, f: (0, b, 0, f, 0, 0)),
            pl.BlockSpec((f2b, 2 * R, 2 * R), lambda b, f: (f, 0, 0)),
            pl.BlockSpec((2 * FN_GC, FN_GC), lambda b, f: (0, 0)),
        ],
        out_specs=pl.BlockSpec((R, f2b * D_MODEL), lambda b, f: (b, f)),
        out_shape=jax.ShapeDtypeStruct((n_rows // R, R * D_MODEL), BF16),
        compiler_params=_cparams(("arbitrary", "arbitrary")),
        name="fourier_stage2",
    )(p6, b2, csm)


def _fft_ctx_kernel(z_ref, a_ref, cs_ref, yin_ref, o_ref, *, n_batch):
    del yin_ref
    cc = cs_ref[0:FN_GC, :]
    sc = cs_ref[FN_GC:2 * FN_GC, :]
    norm = 1.0 / math.sqrt(CTX_LEN * FN_GC)
    n_ctx = n_batch * CTX_LEN
    groups = []
    for g in range(FN_GROUPS):
        z_all = z_ref[g].astype(F32).reshape(n_ctx, FN_GC)
        ys = []
        for b in range(n_batch):
            p = _dot(a_ref[...], z_all[b * CTX_LEN:(b + 1) * CTX_LEN].astype(BF16))
            ys.append(_dot(p[0:CTX_LEN].astype(BF16), cc) + _dot(p[CTX_LEN:2 * CTX_LEN].astype(BF16), sc))
        groups.append(jnp.concatenate(ys, axis=0))
    y = jnp.concatenate(groups, axis=1) * norm
    o_ref[...] = y.reshape(n_ctx // FFT_R, FFT_R * D_MODEL).astype(o_ref.dtype)


def _fft_ctx(fn_view, actx, csm, yf, n_batch):
    n_ctx = n_batch * CTX_LEN
    ctx_blk = n_batch * SEQ // n_ctx
    kern = functools.partial(_fft_ctx_kernel, n_batch=n_batch)
    return pl.pallas_call(
        kern,
        grid=(1,),
        in_specs=[
            pl.BlockSpec((FN_GROUPS, n_ctx // FFT_R, FFT_R * FN_GC), lambda i: (0, ctx_blk, 0)),
            pl.BlockSpec((2 * CTX_LEN, CTX_LEN), lambda i: (0, 0)),
            pl.BlockSpec((2 * FN_GC, FN_GC), lambda i: (0, 0)),
            pl.BlockSpec(memory_space=pl.ANY),
        ],
        out_specs=pl.BlockSpec((n_ctx // FFT_R, FFT_R * D_MODEL), lambda i: (ctx_blk, 0)),
        out_shape=jax.ShapeDtypeStruct(yf.shape, BF16),
        input_output_aliases={3: 0},
        compiler_params=_cparams(("arbitrary",)),
        name="fourier_ctx",
    )(fn_view, actx, csm, yf)


def _fourier(fn, tables, need_ctx, n_batch):
    a1, b2, csm, actx = tables
    R = FFT_R
    n_rows = fn.shape[1] * R
    p = _fft1(fn, a1, n_batch)
    yf = _fft2(p.reshape(FN_GROUPS, n_batch, 2, R, R, FN_GC), b2, csm, n_rows, n_batch)
    if need_ctx:
        yf = _fft_ctx(fn, actx, csm, yf, n_batch)
    return yf


def _merge_kernel(hf_ref, hb_ref, o_ref, gm_ref, gd_ref, gf_ref, yd_ref, yf_ref, xl_ref, xc_ref, mod_ref, hg_ref,
                  wml_ref, wda_ref, wfn_ref, wout_ref, out_ref, *, rows_per_batch, n_batch):
    i = pl.program_id(0)
    x = jnp.where(i < rows_per_batch * n_batch, xl_ref[...], xc_ref[...])
    hsum = hf_ref[...].astype(F32) + hb_ref[...].astype(F32)
    hg = hg_ref[...]
    parts = []
    for h in range(ML_HEADS):
        hs = slice(h * ML_DK, (h + 1) * ML_DK)
        parts.append(_rms(hsum[:, hs], hg[:, hs]))
    ym = (jnp.concatenate(parts, axis=1) * jax.nn.sigmoid(o_ref[...].astype(F32))).astype(BF16)
    yf = yf_ref[...].astype(F32).reshape(x.shape).astype(BF16)
    y = (jax.nn.sigmoid(gm_ref[...].astype(F32)) * _dot(ym, wml_ref[...])
         + jax.nn.sigmoid(gd_ref[...].astype(F32)) * _dot(yd_ref[...], wda_ref[...])
         + jax.nn.sigmoid(gf_ref[...].astype(F32)) * _dot(yf, wfn_ref[...]))
    gate = _mod_row(mod_ref, i, rows_per_batch, n_batch, 2)
    out_ref[...] = x + gate * _dot(y.astype(BF16), wout_ref[...])


def _merge(hf, hb, u, yd, yf, x_lat, x_ctx, ctx_blk0, mods, head_g, wml, wda, wfn, wout, layer, need_ctx, n_batch):
    n_rows = n_batch * (SEQ + CTX_LEN)
    tm = _row_tile(n_batch)
    rows_per_batch = SEQ // tm
    lat_blocks = n_batch * rows_per_batch
    ni = (n_rows if need_ctx else n_batch * SEQ) // tm
    kern = functools.partial(_merge_kernel, rows_per_batch=rows_per_batch, n_batch=n_batch)
    row = lambda i: (i, 0)
    full = lambda i: (0, 0)
    wspec = pl.BlockSpec((None, D_MODEL, D_MODEL), lambda i: (layer, 0, 0))
    return pl.pallas_call(
        kern,
        grid=(ni,),
        in_specs=[
            pl.BlockSpec((tm, D_MODEL), row),
            pl.BlockSpec((tm, D_MODEL), row),
            pl.BlockSpec((tm, D_MODEL), lambda i: (i, U_OML)),
            pl.BlockSpec((tm, D_MODEL), lambda i: (i, U_GPRE)),
            pl.BlockSpec((tm, D_MODEL), lambda i: (i, U_GPRE + 1)),
            pl.BlockSpec((tm, D_MODEL), lambda i: (i, U_GPRE + 2)),
            pl.BlockSpec((tm, D_MODEL), row),
            pl.BlockSpec((tm // FFT_R, FFT_R * D_MODEL), row),
            pl.BlockSpec((tm, D_MODEL), lambda i: (jnp.minimum(i, lat_blocks - 1), 0)),
            pl.BlockSpec((tm, D_MODEL), lambda i: (ctx_blk0 + jnp.maximum(i - lat_blocks, 0), 0)),
            pl.BlockSpec((None, 8, N_MOD), lambda i: (layer, 0, 0)),
            pl.BlockSpec((1, D_MODEL), full),
            wspec, wspec, wspec, wspec,
        ],
        out_specs=pl.BlockSpec((tm, D_MODEL), row),
        out_shape=jax.ShapeDtypeStruct((n_rows, D_MODEL), F32),
        compiler_params=_cparams(("arbitrary",)),
        name="merge_out_proj",
    )(hf, hb, u, u, u, u, yd, yf, x_lat, x_ctx, mods, head_g, wml, wda, wfn, wout)


FFN_CHUNKS = ((0, 1024), (1024, 1024), (2048, 768))


def _ffn_kernel(x_ref, mod_ref, g_ref, win_ref, wout_ref, fg_ref, out_ref, *, rows_per_batch, n_batch, final):
    i = pl.program_id(0)
    x = x_ref[...]
    shift = _mod_row(mod_ref, i, rows_per_batch, n_batch, 3)
    scale = _mod_row(mod_ref, i, rows_per_batch, n_batch, 4)
    gate = _mod_row(mod_ref, i, rows_per_batch, n_batch, 5)
    h = (_rms(x, g_ref[...]) * (1.0 + scale) + shift).astype(BF16)
    acc = None
    for lo, width in FFN_CHUNKS:
        a = _dot(h, win_ref[:, lo:lo + width])
        b = _dot(h, win_ref[:, D_FF + lo:D_FF + lo + width])
        act = (a * jax.nn.sigmoid(a) * b).astype(BF16)
        part = _dot(act, wout_ref[lo:lo + width, :])
        acc = part if acc is None else acc + part
    xn = x + gate * acc
    out_ref[...] = _rms(xn, fg_ref[...]) if final else xn


def _ffn(x, mods, g, w_in, w_out, layer, final_g, final, n_rows_out, n_batch):
    tm = _row_tile(n_batch)
    rows_per_batch = SEQ // tm
    kern = functools.partial(_ffn_kernel, rows_per_batch=rows_per_batch, n_batch=n_batch, final=final)
    row = lambda i: (i, 0)
    full = lambda i: (0, 0)
    return pl.pallas_call(
        kern,
        grid=(n_rows_out // tm,),
        in_specs=[
            pl.BlockSpec((tm, D_MODEL), row),
            pl.BlockSpec((None, 8, N_MOD), lambda i: (layer, 0, 0)),
            pl.BlockSpec((1, D_MODEL), full),
            pl.BlockSpec((None, D_MODEL, 2 * D_FF), lambda i: (layer, 0, 0)),
            pl.BlockSpec((None, D_FF, D_MODEL), lambda i: (layer, 0, 0)),
            pl.BlockSpec((1, D_MODEL), full),
        ],
        out_specs=pl.BlockSpec((tm, D_MODEL), row),
        out_shape=jax.ShapeDtypeStruct((n_rows_out, D_MODEL), F32),
        compiler_params=_cparams(("arbitrary",)),
        name="swiglu_ffn",
    )(x, mods, g, w_in, w_out, final_g)


def _da_col_perm(w):
    half = DA_DH // 2
    lead = w.shape[:-1]
    return jnp.swapaxes(w.reshape(lead + (DA_HEADS, 2, 2, half)), -3, -2).reshape(lead + (DA_HEADS * DA_DV,))


def _rope_tables(pad):
    n_freq = DA_DH // 4
    rows = SEQ // GRID_W
    inv = ROPE_BASE ** (-jnp.arange(n_freq, dtype=F32) / n_freq)
    r = jnp.repeat(jnp.arange(rows, dtype=F32), GRID_W)
    col = jnp.tile(jnp.arange(GRID_W, dtype=F32), rows)
    ang = jnp.concatenate([r[:, None] * inv, col[:, None] * inv], axis=-1)
    cos, sin = jnp.cos(ang), jnp.sin(ang)
    cos_t = jnp.concatenate([cos, cos, cos, cos], axis=-1)
    sin_t = jnp.concatenate([-sin, -sin, sin, sin], axis=-1)
    cos_t = jnp.concatenate([cos_t, jnp.ones((pad, 128), F32)], axis=0)
    sin_t = jnp.concatenate([sin_t, jnp.zeros((pad, 128), F32)], axis=0)
    return cos_t, sin_t


def kernel(x, c, ctx, c_ctx, w_ada, b_ada, norm_g, w_in, ml_gate_b, ml_head_g, da_lam, da_head_g,
           w_br_ml, w_br_da, w_br_fn, w_out, w_ffn_in, w_ffn_out, final_g):
    n_batch = x.shape[0]
    n_lat = n_batch * SEQ
    x_lat = x.reshape(n_lat, D_MODEL)
    x_ctx = ctx.reshape(n_batch * CTX_LEN, D_MODEL)
    cc = jnp.concatenate([c, c_ctx[None, :], jnp.zeros((8 - n_batch - 1, D_MODEL), F32)], axis=0)
    mods = _mods(cc, w_ada, b_ada)
    cos_t, sin_t = _rope_tables(n_batch * CTX_LEN)
    tables = _dft_tables()
    final_g2 = final_g.reshape(1, D_MODEL)
    wb_ml, wb_da, wb_fn, wb_out = _to_bf16(w_br_ml), _to_bf16(w_br_da), _to_bf16(w_br_fn), _to_bf16(w_out)
    wb_ffn_in, wb_ffn_out = _to_bf16(w_ffn_in), _to_bf16(w_ffn_out)

    gate_lo = 4 * D_MODEL
    da_lo = gate_lo + N_GATE
    fn_lo = da_lo + 3 * D_MODEL
    w_main = jnp.concatenate([w_in[..., :D_MODEL], w_in[..., 2 * D_MODEL:gate_lo],
                              _da_col_perm(w_in[..., da_lo:da_lo + D_MODEL]),
                              _da_col_perm(w_in[..., da_lo + D_MODEL:da_lo + 2 * D_MODEL]),
                              w_in[..., da_lo + 2 * D_MODEL:fn_lo],
                              w_in[..., fn_lo + D_MODEL:], w_in[..., fn_lo:fn_lo + D_MODEL]], axis=-1).astype(BF16)
    w_kt = jnp.swapaxes(w_in[..., D_MODEL:2 * D_MODEL], 1, 2).astype(BF16)
    w_gate_t = jnp.swapaxes(w_in[..., gate_lo:da_lo], 1, 2).astype(BF16)

    tm_in = n_batch * CTX_LEN
    tm_tok = _row_tile(n_batch)
    xs = None
    for l in range(DEPTH):
        need_ctx = l < DEPTH - 1
        lam_init = 0.8 - 0.6 * math.exp(-0.3 * l)
        xl, xc = (x_lat, x_ctx) if xs is None else (xs, xs)
        u, kt, fn, gates_t = _inproj(xl, xc, 0 if xs is None else n_lat // tm_in, mods,
                                     norm_g[l, 0].reshape(1, D_MODEL), w_main, w_kt, w_gate_t, l,
                                     cos_t, sin_t, n_batch)
        hf, hb = _mlstm(u, kt, gates_t, ml_gate_b[l], n_batch)
        yd = _attn(u, da_lam[l], da_head_g[l].reshape(1, D_MODEL), lam_init, need_ctx, n_batch)
        yf = _fourier(fn, tables, need_ctx, n_batch)
        xs = _merge(hf, hb, u, yd, yf, xl, xc, 0 if xs is None else n_lat // tm_tok, mods,
                    ml_head_g[l].reshape(1, D_MODEL), wb_ml, wb_da, wb_fn, wb_out, l, need_ctx, n_batch)
        final = l == DEPTH - 1
        n_out = n_lat if final else xs.shape[0]
        xs = _ffn(xs, mods, norm_g[l, 1].reshape(1, D_MODEL), wb_ffn_in, wb_ffn_out, l,
                  final_g2, final, n_out, n_batch)
    return xs.reshape(n_batch, SEQ, D_MODEL)
```

```python
import functools
import math

import jax
import jax.numpy as jnp
from jax import lax
from jax.experimental import pallas as pl
from jax.experimental.pallas import tpu as pltpu

D_MODEL = 1024
SEQ = 4096
DEPTH = 4
CTX_LEN = 256
GRID_W = 64
NORM_EPS = 1e-6

ML_HEADS = 4
ML_DK = 256
ML_CHUNK = 256

DA_HEADS = 8
DA_DH = 64
DA_DV = 2 * DA_DH
ROPE_BASE = 10000.0
ATTN_TQ = 512
ATTN_ROW_CHUNKS = 4

FN_GROUPS = 4
FN_GC = 256
FFT_R = 64

D_FF = 2816
N_GATE = 4 * ML_HEADS
N_MOD = 6 * D_MODEL

U_QML, U_VML, U_OML, U_QDA, U_KDA, U_VDA, U_GPRE = 0, 1, 2, 3, 4, 5, 6
U_BLOCKS = 9
W_BLOCKS = U_BLOCKS + 1
STEP_KT = 1
N_COL_STEPS = W_BLOCKS + 1

VMEM_LIMIT_V7X = 56 * 1024 * 1024

BF16 = jnp.bfloat16
F32 = jnp.float32


def _cparams(sem):
    return pltpu.CompilerParams(dimension_semantics=sem, vmem_limit_bytes=VMEM_LIMIT_V7X)


def _dot(a, b):
    return jnp.dot(a, b, preferred_element_type=F32)


def _dot_nt(a, b):
    return lax.dot_general(a, b, (((1,), (1,)), ((), ())), preferred_element_type=F32)


def _mod_row(mod_ref, i, rows_per_batch, n_batch, col):
    r = jnp.minimum(i // rows_per_batch, n_batch)
    return mod_ref[pl.ds(r, 1), col * D_MODEL:(col + 1) * D_MODEL]


def _row_tile(n_batch):
    return min(512, n_batch * CTX_LEN)


def _rms(x, g):
    return x * lax.rsqrt(jnp.mean(x * x, axis=-1, keepdims=True) + NORM_EPS) * g


def _cast_kernel(w_ref, o_ref):
    o_ref[...] = w_ref[...].astype(o_ref.dtype)


def _to_bf16(w):
    n_l, rows, cols = w.shape
    tr = 256
    return pl.pallas_call(
        _cast_kernel,
        grid=(n_l, rows // tr),
        in_specs=[pl.BlockSpec((1, tr, cols), lambda l, i: (l, i, 0))],
        out_specs=pl.BlockSpec((1, tr, cols), lambda l, i: (l, i, 0)),
        out_shape=jax.ShapeDtypeStruct(w.shape, BF16),
        compiler_params=_cparams(("arbitrary", "arbitrary")),
        name="cast_bf16",
    )(w)


def _mods_kernel(c_ref, w_ref, b_ref, o_ref):
    c = c_ref[...]
    s = (c * jax.nn.sigmoid(c)).astype(BF16)
    o_ref[0] = _dot(s, w_ref[0].astype(BF16)) + b_ref[0]


def _mods(cc, w_ada, b_ada):
    tn = 1536
    return pl.pallas_call(
        _mods_kernel,
        grid=(DEPTH, N_MOD // tn),
        in_specs=[
            pl.BlockSpec((8, D_MODEL), lambda l, j: (0, 0)),
            pl.BlockSpec((1, D_MODEL, tn), lambda l, j: (l, 0, j)),
            pl.BlockSpec((1, 1, tn), lambda l, j: (l, 0, j)),
        ],
        out_specs=pl.BlockSpec((1, 8, tn), lambda l, j: (l, 0, j)),
        out_shape=jax.ShapeDtypeStruct((DEPTH, 8, N_MOD), F32),
        compiler_params=_cparams(("arbitrary", "arbitrary")),
        name="adaln_mods",
    )(cc, w_ada, b_ada.reshape(DEPTH, 1, N_MOD))


def _inproj_kernel(xl_ref, xc_ref, mod_ref, g_ref, w_ref, wkt_ref, wgt_ref, cos_ref, sin_ref,
                   u_ref, kt_ref, fn_ref, gate_ref, xn_ref, *, rows_per_batch, n_batch):
    i = pl.program_id(0)
    j = pl.program_id(1)

    def normalise(x_ref):
        y = _rms(x_ref[...], g_ref[...])
        shift = _mod_row(mod_ref, i, rows_per_batch, n_batch, 0)
        scale = _mod_row(mod_ref, i, rows_per_batch, n_batch, 1)
        xn_ref[...] = (y * (1.0 + scale) + shift).astype(BF16)
        gate_ref[...] = _dot_nt(wgt_ref[...], xn_ref[...])

    is_lat = i < rows_per_batch * n_batch

    @pl.when(jnp.logical_and(j == 0, is_lat))
    def _():
        normalise(xl_ref)

    @pl.when(jnp.logical_and(j == 0, jnp.logical_not(is_lat)))
    def _():
        normalise(xc_ref)

    def product():
        return _dot(xn_ref[...], w_ref[...])

    blk = jnp.where(j == 0, 0, j - 1)
    is_rope = jnp.logical_or(blk == U_QDA, blk == U_KDA)
    is_plain = jnp.logical_and(j != STEP_KT, jnp.logical_and(jnp.logical_not(is_rope), blk < U_BLOCKS))

    @pl.when(is_plain)
    def _():
        u_ref[...] = product().astype(BF16)

    @pl.when(j == STEP_KT)
    def _():
        kt_ref[...] = (_dot_nt(wkt_ref[...], xn_ref[...]) * (ML_DK ** -0.5)).astype(BF16)

    @pl.when(is_rope)
    def _():
        acc = product()
        cos = cos_ref[...]
        sin = sin_ref[...]
        for t in range(acc.shape[1] // DA_DV):
            sl = slice(t * DA_DV, (t + 1) * DA_DV)
            x = acc[:, sl]
            u_ref[:, sl] = (x * cos + pltpu.roll(x, DA_DV // 2, 1) * sin).astype(BF16)

    @pl.when(blk == U_BLOCKS)
    def _():
        acc = product()
        for g in range(FN_GROUPS):
            z = acc[:, g * FN_GC:(g + 1) * FN_GC]
            fn_ref[g] = z.reshape(z.shape[0] // FFT_R, FFT_R * FN_GC).astype(BF16)


def _inproj(x_lat, x_ctx, ctx_blk, mods, g, w_main, w_kt, w_gate_t, layer, cos_t, sin_t, n_batch):
    tm = n_batch * CTX_LEN
    assert SEQ % tm == 0 and cos_t.shape[0] == SEQ + tm
    n_rows = n_batch * (SEQ + CTX_LEN)
    ni = n_rows // tm
    rows_per_batch = SEQ // tm
    lat_blocks = n_batch * rows_per_batch

    def tab_idx(i, j):
        return (jnp.where(i < lat_blocks, i % rows_per_batch, rows_per_batch), 0)

    def w_blk(j):
        return jnp.where(j == 0, 0, j - 1)

    kern = functools.partial(_inproj_kernel, rows_per_batch=rows_per_batch, n_batch=n_batch)
    return pl.pallas_call(
        kern,
        grid=(ni, N_COL_STEPS),
        in_specs=[
            pl.BlockSpec((tm, D_MODEL), lambda i, j: (jnp.minimum(i, lat_blocks - 1), 0)),
            pl.BlockSpec((tm, D_MODEL), lambda i, j: (ctx_blk, 0)),
            pl.BlockSpec((None, 8, N_MOD), lambda i, j: (layer, 0, 0)),
            pl.BlockSpec((1, D_MODEL), lambda i, j: (0, 0)),
            pl.BlockSpec((None, D_MODEL, D_MODEL), lambda i, j: (layer, 0, w_blk(j))),
            pl.BlockSpec((None, D_MODEL, D_MODEL), lambda i, j: (layer, 0, 0)),
            pl.BlockSpec((None, N_GATE, D_MODEL), lambda i, j: (layer, 0, 0)),
            pl.BlockSpec((tm, 128), tab_idx),
            pl.BlockSpec((tm, 128), tab_idx),
        ],
        out_specs=[
            pl.BlockSpec((tm, D_MODEL), lambda i, j: (i, jnp.minimum(w_blk(j), U_BLOCKS - 1))),
            pl.BlockSpec((D_MODEL, tm), lambda i, j: (0, i)),
            pl.BlockSpec((FN_GROUPS, tm // FFT_R, FFT_R * FN_GC), lambda i, j: (0, i, 0)),
            pl.BlockSpec((N_GATE, tm), lambda i, j: (0, i)),
        ],
        out_shape=[
            jax.ShapeDtypeStruct((n_rows, U_BLOCKS * D_MODEL), BF16),
            jax.ShapeDtypeStruct((D_MODEL, n_rows), BF16),
            jax.ShapeDtypeStruct((FN_GROUPS, n_rows // FFT_R, FFT_R * FN_GC), BF16),
            jax.ShapeDtypeStruct((N_GATE, n_rows), F32),
        ],
        scratch_shapes=[pltpu.VMEM((tm, D_MODEL), BF16)],
        compiler_params=_cparams(("arbitrary", "arbitrary")),
        name="in_proj",
    )(x_lat, x_ctx, mods, g, w_main, w_kt, w_gate_t, cos_t, sin_t)


def _split3(x):
    hi = x.astype(BF16).astype(F32)
    mid = (x - hi).astype(BF16).astype(F32)
    lo = (x - hi - mid).astype(BF16).astype(F32)
    return hi, mid, lo


def _mlstm_kernel(qf_ref, ktf_ref, vf_ref, gtf_ref, gtfn_ref, qb_ref, ktb_ref, vb_ref, gtb_ref, gtbn_ref, bias_ref,
                  hf_ref, hb_ref, *scratch):
    n_st = 2 * ML_HEADS
    cx_refs, bw_refs, pmw_refs = scratch[:n_st], scratch[n_st:2 * n_st], scratch[2 * n_st:3 * n_st]
    m_refs, c_refs, bend_refs, g_refs = (scratch[3 * n_st + 2 * k:3 * n_st + 2 * k + 2] for k in range(4))
    state_refs = cx_refs + m_refs
    s = pl.program_id(1)
    L = ML_CHUNK
    H = ML_HEADS
    W = 128

    @pl.when(s == 0)
    def _():
        for ref in state_refs:
            ref[...] = jnp.zeros_like(ref)

    t_idx = lax.broadcasted_iota(jnp.int32, (L, L), 0)
    s_idx = lax.broadcasted_iota(jnp.int32, (L, L), 1)
    eye = t_idx == s_idx
    sub8 = lax.broadcasted_iota(jnp.int32, (8, W), 0)
    ones_w = jnp.ones((L, W), BF16)
    er = lax.broadcasted_iota(jnp.int32, (4 * L, 2 * W), 0)
    ec = lax.broadcasted_iota(jnp.int32, (4 * L, 2 * W), 1)
    expand = jnp.where((er < 3 * L) == (ec < W), 1.0, 0.0).astype(BF16)

    def running_max_rows(x, d):
        n_tiles = L // 8
        out = [None] * n_tiles
        carry = None
        for j in (range(n_tiles) if d == 0 else range(n_tiles - 1, -1, -1)):
            r = x[8 * j:8 * (j + 1)]
            k = 1
            while k < 8:
                if d == 0:
                    r = jnp.maximum(r, jnp.where(sub8 >= k, pltpu.roll(r, k, 0), -jnp.inf))
                else:
                    r = jnp.maximum(r, jnp.where(sub8 < 8 - k, pltpu.roll(r, 8 - k, 0), -jnp.inf))
                k *= 2
            if carry is not None:
                r = jnp.maximum(r, carry)
            carry = jnp.broadcast_to(r[7:8] if d == 0 else r[0:1], (8, W))
            out[j] = r
        return jnp.concatenate(out, axis=0)

    def gate_part(d, gt_ref):
        before = (t_idx <= s_idx) if d == 0 else (t_idx >= s_idx)
        gt = gt_ref[2 * H * d:2 * H * (d + 1), :] + bias_ref[2 * H * d:2 * H * (d + 1), :]
        i4, f4 = gt[0:H], gt[H:2 * H]
        lf4 = jnp.minimum(f4, 0.0) - jnp.log1p(jnp.exp(-jnp.abs(f4)))
        lf_terms = jnp.concatenate(list(_split3(lf4)) + [jnp.zeros((H, L), F32)], axis=0).astype(BF16)
        cum_rhs = jnp.concatenate([jnp.where(before, 1.0, 0.0).astype(BF16), ones_w], axis=1)
        r = _dot(lf_terms, cum_rhs)
        bx = r[0:H] + r[H:2 * H] + r[2 * H:3 * H]
        b4, bend4 = bx[:, 0:L], bx[:, L:L + W]
        c4 = i4 - b4
        c_refs[d][...] = c4
        bend_refs[d][...] = bend4
        g_refs[d][...] = bend4[:, 0:1] - b4 + i4
        col_terms = _split3(b4) + (c4.astype(BF16).astype(F32),)
        for h in range(H):
            diag = jnp.concatenate([jnp.where(eye, x[h:h + 1, :], 0.0) for x in col_terms], axis=1).astype(BF16)
            wide = _dot(diag, expand)
            bw_refs[d * H + h][...] = wide[:, 0:W]
            pmw_refs[d * H + h][...] = running_max_rows(wide[:, W:2 * W], d)

    dirs = ((0, qf_ref, ktf_ref, vf_ref, gtf_ref, gtfn_ref, hf_ref),
            (1, qb_ref, ktb_ref, vb_ref, gtb_ref, gtbn_ref, hb_ref))

    @pl.when(s == 0)
    def _():
        for d, _, _, _, gt_ref, _, _ in dirs:
            gate_part(d, gt_ref)

    for d, q_ref, kt_ref, v_ref, _, _, h_ref in dirs:
        causal = (s_idx <= t_idx) if d == 0 else (s_idx >= t_idx)
        c4, bend4, g4 = c_refs[d][...], bend_refs[d][...], g_refs[d][...]
        m_prev4 = m_refs[d][...]
        m_new4 = jnp.maximum(bend4 + m_prev4, jnp.max(g4, axis=1, keepdims=True))
        m_refs[d][...] = m_new4
        decay4 = jnp.exp(bend4 + m_prev4 - m_new4)
        w4 = jnp.exp(g4 - m_new4[:, 0:1])
        for h in range(H):
            st = d * H + h
            hs = slice(h * ML_DK, (h + 1) * ML_DK)
            b_w = bw_refs[st][...]
            m_w = jnp.maximum(pmw_refs[st][...], m_prev4[h:h + 1, :])

            q = q_ref[:, hs]
            kt = kt_ref[hs, :]
            vx = jnp.concatenate([v_ref[:, hs], ones_w], axis=1)
            m_ll = jnp.concatenate([m_w] * (L // W), axis=1)
            a = (jnp.where(causal, jnp.exp(c4[h:h + 1, :] - m_ll), 0.0) * _dot(q, kt)).astype(BF16)
            cx_prev = cx_refs[st][...]
            qc = _dot(q, cx_prev.astype(BF16))
            av = _dot(a, vx)
            sc_w = jnp.exp(m_prev4[h:h + 1, :] - m_w)
            den = sc_w * qc[:, ML_DK:] + av[:, ML_DK:]
            inv = 1.0 / jnp.maximum(jnp.abs(den), jnp.exp(-(b_w + m_w)))
            for t in range(ML_DK // W):
                ts = slice(t * W, (t + 1) * W)
                h_ref[:, h * ML_DK + t * W:h * ML_DK + (t + 1) * W] = (
                    (sc_w * qc[:, ts] + av[:, ts]) * inv).astype(h_ref.dtype)

            kw = (kt.astype(F32) * w4[h:h + 1, :]).astype(BF16)
            dec = jnp.concatenate([decay4[h:h + 1, :]] * (ML_DK // W + 1), axis=1)
            cx_refs[st][...] = dec * cx_prev + _dot(kw, vx)

    for d, _, _, _, _, gtn_ref, _ in dirs:
        gate_part(d, gtn_ref)


def _mlstm(u, kt, gates_t, gate_b, n_batch):
    n_rows = u.shape[0]
    L = ML_CHUNK
    lat_chunks = SEQ // L
    ctx_chunks = CTX_LEN // L
    n_steps = ctx_chunks + lat_chunks
    ctx_base = n_batch * lat_chunks

    def rowblk(d):
        def f(b, s):
            in_ctx = s < ctx_chunks
            if d == 0:
                c = jnp.where(in_ctx, s, s - ctx_chunks)
            else:
                c = jnp.where(in_ctx, ctx_chunks - 1 - s, lat_chunks - 1 - (s - ctx_chunks))
            return jnp.where(in_ctx, ctx_base + ctx_chunks * b, lat_chunks * b) + c
        return f

    def dir_specs(d):
        rb = rowblk(d)
        return [
            pl.BlockSpec((L, D_MODEL), lambda b, s: (rb(b, s), U_QML)),
            pl.BlockSpec((D_MODEL, L), lambda b, s: (0, rb(b, s))),
            pl.BlockSpec((L, D_MODEL), lambda b, s: (rb(b, s), U_VML)),
            pl.BlockSpec((N_GATE, L), lambda b, s: (0, rb(b, s))),
            pl.BlockSpec((N_GATE, L), lambda b, s: (0, rb(b, jnp.minimum(s + 1, n_steps - 1)))),
        ]

    def out_spec(d):
        rb = rowblk(d)
        return pl.BlockSpec((L, D_MODEL), lambda b, s: (rb(b, s), 0))

    n_st = 2 * ML_HEADS
    return pl.pallas_call(
        _mlstm_kernel,
        grid=(n_batch, n_steps),
        in_specs=dir_specs(0) + dir_specs(1) + [pl.BlockSpec((N_GATE, 1), lambda b, s: (0, 0))],
        out_specs=[out_spec(0), out_spec(1)],
        out_shape=[jax.ShapeDtypeStruct((n_rows, D_MODEL), BF16)] * 2,
        scratch_shapes=(
            [pltpu.VMEM((ML_DK, ML_DK + 128), F32)] * n_st
            + [pltpu.VMEM((L, 128), F32)] * (2 * n_st)
            + [pltpu.VMEM((ML_HEADS, 128), F32)] * 2
            + [pltpu.VMEM((ML_HEADS, L), F32)] * 2
            + [pltpu.VMEM((ML_HEADS, 128), F32)] * 2
            + [pltpu.VMEM((ML_HEADS, L), F32)] * 2),
        compiler_params=_cparams(("arbitrary", "arbitrary")),
        name="mlstm_scan",
    )(u, kt, u, gates_t, gates_t, u, kt, u, gates_t, gates_t, gate_b.reshape(N_GATE, 1))


def _attn_lambda(lam_ref, lam_init):
    lq = lam_ref[...]
    return (jnp.exp(jnp.sum(lq[0:1] * lq[1:2], axis=1, keepdims=True))
            - jnp.exp(jnp.sum(lq[2:3] * lq[3:4], axis=1, keepdims=True)) + lam_init)


def _attn_queries(q_ref):
    q = q_ref[...]
    lane = lax.broadcasted_iota(jnp.int32, (1, DA_DV), 1)
    zero = jnp.zeros_like(q)
    is_map0 = (lane % DA_DH) < (DA_DH // 2)
    q2 = jnp.concatenate([jnp.where(is_map0, q, zero), jnp.where(is_map0, zero, q)], axis=0)
    return q2 * (DA_DH ** -0.5)


def _attn_sums(s_chunks, m_chunks, vx):
    return jnp.concatenate([_dot(jnp.exp(s - m).astype(BF16), vx) for s, m in zip(s_chunks, m_chunks)], axis=0)


def _attn_finish(acc, lam, g, lam_init, tq):
    o0 = acc[0:tq, 0:DA_DV] * (1.0 / acc[0:tq, DA_DV:DA_DV + 1])
    o1 = acc[tq:, 0:DA_DV] * (1.0 / acc[tq:, DA_DV:DA_DV + 1])
    return _rms(o0 - lam * o1, g) * (1.0 - lam_init)


def _attn_kernel(q_ref, kl_ref, vl_ref, kc_ref, vc_ref, lam_ref, g_ref, o_ref, vx_ref, s_ref, m_ref,
                 *, lam_init, q_blocks):
    t = pl.program_id(0)
    tq = q_ref.shape[0]
    rows = 2 * tq // ATTN_ROW_CHUNKS

    @pl.when(jnp.logical_and(t > 0, (t - 1) % q_blocks == 0))
    def _():
        vx_ref[0:SEQ, 0:DA_DV] = vl_ref[...]
        vx_ref[SEQ:, 0:DA_DV] = vc_ref[...]
        lane_v = lax.broadcasted_iota(jnp.int32, (SEQ + CTX_LEN, DA_DV), 1)
        vx_ref[:, DA_DV:] = jnp.where(lane_v == 0, 1.0, 0.0).astype(BF16)

    def score(slot):
        q2 = _attn_queries(q_ref)
        s_lat = _dot_nt(q2, kl_ref[...])
        s_ctx = _dot_nt(q2, kc_ref[...])
        m_ref[slot] = jnp.maximum(jnp.max(s_lat, axis=1, keepdims=True), jnp.max(s_ctx, axis=1, keepdims=True))
        s_ref[slot, :, 0:SEQ] = s_lat
        s_ref[slot, :, SEQ:] = s_ctx

    def drain(slot):
        s_chunks = [s_ref[slot, c * rows:(c + 1) * rows, :] for c in range(ATTN_ROW_CHUNKS)]
        m_chunks = [m_ref[slot, c * rows:(c + 1) * rows, :] for c in range(ATTN_ROW_CHUNKS)]
        acc = _attn_sums(s_chunks, m_chunks, vx_ref[...])
        o = _attn_finish(acc, _attn_lambda(lam_ref, lam_init), g_ref[...], lam_init, tq)
        o_ref[...] = o.astype(o_ref.dtype)

    @pl.when(t == 0)
    def _():
        score(0)

    for parity in (0, 1):
        @pl.when(jnp.logical_and(t > 0, t % 2 == parity))
        def _():
            drain(1 - parity)
            score(parity)


def _attn_ctx_kernel(q_ref, kc_ref, vc_ref, lam_ref, g_ref, yin_ref, o_ref, *, lam_init):
    del yin_ref
    tq = q_ref.shape[0]
    s = _dot_nt(_attn_queries(q_ref), kc_ref[...])
    lane_v = lax.broadcasted_iota(jnp.int32, (CTX_LEN, DA_DV), 1)
    vx = jnp.concatenate([vc_ref[...], jnp.where(lane_v == 0, 1.0, 0.0).astype(BF16)], axis=1)
    acc = _attn_sums([s], [jnp.max(s, axis=1, keepdims=True)], vx)
    o = _attn_finish(acc, _attn_lambda(lam_ref, lam_init), g_ref[...], lam_init, tq)
    o_ref[...] = o.astype(o_ref.dtype)


def _attn(u, da_lam, head_g, lam_init, need_ctx, n_batch):
    n_rows = u.shape[0]
    tq = ATTN_TQ
    lat_qblocks = SEQ // tq
    ctx_base = n_batch * (SEQ // CTX_LEN)
    cpb = D_MODEL // DA_DV
    n_keys = SEQ + CTX_LEN

    n_items = n_batch * DA_HEADS * lat_qblocks

    def item(t):
        t = jnp.clip(t, 0, n_items - 1)
        return t // (DA_HEADS * lat_qblocks), (t // lat_qblocks) % DA_HEADS, t % lat_qblocks

    def scored(f):
        return lambda t: f(*item(t))

    def drained(f):
        return lambda t: f(*item(t - 1))

    kern = functools.partial(_attn_kernel, lam_init=lam_init, q_blocks=lat_qblocks)
    yd = pl.pallas_call(
        kern,
        grid=(n_items + 1,),
        in_specs=[
            pl.BlockSpec((tq, DA_DV), scored(lambda b, h, qi: (b * lat_qblocks + qi, U_QDA * cpb + h))),
            pl.BlockSpec((SEQ, DA_DV), scored(lambda b, h, qi: (b, U_KDA * cpb + h))),
            pl.BlockSpec((SEQ, DA_DV), drained(lambda b, h, qi: (b, U_VDA * cpb + h))),
            pl.BlockSpec((CTX_LEN, DA_DV), scored(lambda b, h, qi: (ctx_base + b, U_KDA * cpb + h))),
            pl.BlockSpec((CTX_LEN, DA_DV), drained(lambda b, h, qi: (ctx_base + b, U_VDA * cpb + h))),
            pl.BlockSpec((4, DA_DH), lambda t: (0, 0)),
            pl.BlockSpec((1, DA_DV), drained(lambda b, h, qi: (0, h))),
        ],
        out_specs=pl.BlockSpec((tq, DA_DV), drained(lambda b, h, qi: (b * lat_qblocks + qi, h))),
        out_shape=jax.ShapeDtypeStruct((n_rows, D_MODEL), BF16),
        scratch_shapes=[
            pltpu.VMEM((n_keys, 2 * DA_DV), BF16),
            pltpu.VMEM((2, 2 * tq, n_keys), F32),
            pltpu.VMEM((2, 2 * tq, 1), F32),
        ],
        compiler_params=_cparams(("arbitrary",)),
        name="diff_attn",
    )(u, u, u, u, u, da_lam, head_g)
    if not need_ctx:
        return yd
    kern_ctx = functools.partial(_attn_ctx_kernel, lam_init=lam_init)
    return pl.pallas_call(
        kern_ctx,
        grid=(n_batch, DA_HEADS),
        in_specs=[
            pl.BlockSpec((CTX_LEN, DA_DV), lambda b, h: (ctx_base + b, U_QDA * cpb + h)),
            pl.BlockSpec((CTX_LEN, DA_DV), lambda b, h: (ctx_base + b, U_KDA * cpb + h)),
            pl.BlockSpec((CTX_LEN, DA_DV), lambda b, h: (ctx_base + b, U_VDA * cpb + h)),
            pl.BlockSpec((4, DA_DH), lambda b, h: (0, 0)),
            pl.BlockSpec((1, DA_DV), lambda b, h: (0, h)),
            pl.BlockSpec(memory_space=pl.ANY),
        ],
        out_specs=pl.BlockSpec((CTX_LEN, DA_DV), lambda b, h: (ctx_base + b, h)),
        out_shape=jax.ShapeDtypeStruct((n_rows, D_MODEL), BF16),
        input_output_aliases={5: 0},
        compiler_params=_cparams(("arbitrary", "arbitrary")),
        name="diff_attn_ctx",
    )(u, u, u, da_lam, head_g, yd)


def _dft_tables():
    R = FFT_R

    def cs(num, period):
        ang = (num % period).astype(F32) * (2.0 * math.pi / period)
        return jnp.cos(ang), jnp.sin(ang)

    idx = jnp.arange(R, dtype=jnp.int32)
    c1, s1 = cs(idx[:, None] * idx[None, :], R)
    a1 = jnp.concatenate([c1, -s1], axis=0).astype(BF16)
    f2 = idx[:, None, None]
    f1 = idx[None, :, None]
    t1 = idx[None, None, :]
    mc, ms = cs(t1 * (R * f1 + f2), SEQ)
    b2 = jnp.concatenate([jnp.concatenate([mc, ms], axis=2),
                          jnp.concatenate([-ms, mc], axis=2)], axis=1).astype(BF16)
    ch = jnp.arange(FN_GC, dtype=jnp.int32)
    cc, sc = cs(ch[:, None] * ch[None, :], FN_GC)
    csm = jnp.concatenate([cc, sc], axis=0).astype(BF16)
    actx = jnp.concatenate([cc, -sc], axis=0).astype(BF16)
    return a1, b2, csm, actx


def _fft1_kernel(a_ref, z_ref, p_ref):
    p_ref[0, 0] = _dot(a_ref[...], z_ref[0]).astype(p_ref.dtype)


def _fft1(fn_view, a1, n_batch):
    R = FFT_R
    lanes = R * FN_GC
    lc = lanes
    return pl.pallas_call(
        _fft1_kernel,
        grid=(FN_GROUPS, n_batch, lanes // lc),
        in_specs=[
            pl.BlockSpec((2 * R, R), lambda g, b, c: (0, 0)),
            pl.BlockSpec((1, R, lc), lambda g, b, c: (g, b, c)),
        ],
        out_specs=pl.BlockSpec((1, 1, 2 * R, lc), lambda g, b, c: (g, b, 0, c)),
        out_shape=jax.ShapeDtypeStruct((FN_GROUPS, n_batch, 2 * R, lanes), BF16),
        compiler_params=_cparams(("arbitrary", "arbitrary", "arbitrary")),
        name="fourier_stage1",
    )(a1, fn_view)


def _fft2_kernel(p_ref, b_ref, cs_ref, o_ref, *, f2b):
    R = FFT_R
    cc = cs_ref[0:FN_GC, :]
    sc = cs_ref[FN_GC:2 * FN_GC, :]
    norm = 1.0 / math.sqrt(SEQ * FN_GC)
    for g in range(FN_GROUPS):
        xr, xi = [], []
        for jj in range(f2b):
            stacked = jnp.concatenate([p_ref[g, 0, 0, jj], p_ref[g, 0, 1, jj]], axis=0)
            x = _dot(b_ref[jj], stacked)
            xr.append(x[0:R])
            xi.append(x[R:2 * R])
        y = (_dot(jnp.concatenate(xr, axis=0).astype(BF16), cc)
             + _dot(jnp.concatenate(xi, axis=0).astype(BF16), sc)) * norm
        for jj in range(f2b):
            lo = jj * D_MODEL + g * FN_GC
            o_ref[:, lo:lo + FN_GC] = y[jj * R:(jj + 1) * R].astype(o_ref.dtype)


def _fft2(p6, b2, csm, n_rows, n_batch):
    R = FFT_R
    f2b = 8
    kern = functools.partial(_fft2_kernel, f2b=f2b)
    return pl.pallas_call(
        kern,
        grid=(n_batch, R // f2b),
        in_specs=[
            pl.BlockSpec((FN_GROUPS, 1, 2, f2b, R, FN_GC), lambda b, f: (0, b, 0, f, 0, 0)),
            pl.BlockSpec((f2b, 2 * R, 2 * R), lambda b, f: (f, 0, 0)),
            pl.BlockSpec((2 * FN_GC, FN_GC), lambda b, f: (0, 0)),
        ],
        out_specs=pl.BlockSpec((R, f2b * D_MODEL), lambda b, f: (b, f)),
        out_shape=jax.ShapeDtypeStruct((n_rows // R, R * D_MODEL), BF16),
        compiler_params=_cparams(("arbitrary", "arbitrary")),
        name="fourier_stage2",
    )(p6, b2, csm)


def _fft_ctx_kernel(z_ref, a_ref, cs_ref, yin_ref, o_ref, *, n_batch):
    del yin_ref
    cc = cs_ref[0:FN_GC, :]
    sc = cs_ref[FN_GC:2 * FN_GC, :]
    norm = 1.0 / math.sqrt(CTX_LEN * FN_GC)
    n_ctx = n_batch * CTX_LEN
    groups = []
    for g in range(FN_GROUPS):
        z_all = z_ref[g].astype(F32).reshape(n_ctx, FN_GC)
        ys = []
        for b in range(n_batch):
            p = _dot(a_ref[...], z_all[b * CTX_LEN:(b + 1) * CTX_LEN].astype(BF16))
            ys.append(_dot(p[0:CTX_LEN].astype(BF16), cc) + _dot(p[CTX_LEN:2 * CTX_LEN].astype(BF16), sc))
        groups.append(jnp.concatenate(ys, axis=0))
    y = jnp.concatenate(groups, axis=1) * norm
    o_ref[...] = y.reshape(n_ctx // FFT_R, FFT_R * D_MODEL).astype(o_ref.dtype)


def _fft_ctx(fn_view, actx, csm, yf, n_batch):
    n_ctx = n_batch * CTX_LEN
    ctx_blk = n_batch * SEQ // n_ctx
    kern = functools.partial(_fft_ctx_kernel, n_batch=n_batch)
    return pl.pallas_call(
        kern,
        grid=(1,),
        in_specs=[
            pl.BlockSpec((FN_GROUPS, n_ctx // FFT_R, FFT_R * FN_GC), lambda i: (0, ctx_blk, 0)),
            pl.BlockSpec((2 * CTX_LEN, CTX_LEN), lambda i: (0, 0)),
            pl.BlockSpec((2 * FN_GC, FN_GC), lambda i: (0, 0)),
            pl.BlockSpec(memory_space=pl.ANY),
        ],
        out_specs=pl.BlockSpec((n_ctx // FFT_R, FFT_R * D_MODEL), lambda i: (ctx_blk, 0)),
        out_shape=jax.ShapeDtypeStruct(yf.shape, BF16),
        input_output_aliases={3: 0},
        compiler_params=_cparams(("arbitrary",)),
        name="fourier_ctx",
    )(fn_view, actx, csm, yf)


def _fourier(fn, tables, need_ctx, n_batch):
    a1, b2, csm, actx = tables
    R = FFT_R
    n_rows = fn.shape[1] * R
    p = _fft1(fn, a1, n_batch)
    yf = _fft2(p.reshape(FN_GROUPS, n_batch, 2, R, R, FN_GC), b2, csm, n_rows, n_batch)
    if need_ctx:
        yf = _fft_ctx(fn, actx, csm, yf, n_batch)
    return yf


def _merge_kernel(hf_ref, hb_ref, o_ref, gm_ref, gd_ref, gf_ref, yd_ref, yf_ref, xl_ref, xc_ref, mod_ref, hg_ref,
                  wml_ref, wda_ref, wfn_ref, wout_ref, out_ref, *, rows_per_batch, n_batch):
    i = pl.program_id(0)
    x = jnp.where(i < rows_per_batch * n_batch, xl_ref[...], xc_ref[...])
    hsum = hf_ref[...].astype(F32) + hb_ref[...].astype(F32)
    hg = hg_ref[...]
    parts = []
    for h in range(ML_HEADS):
        hs = slice(h * ML_DK, (h + 1) * ML_DK)
        parts.append(_rms(hsum[:, hs], hg[:, hs]))
    ym = (jnp.concatenate(parts, axis=1) * jax.nn.sigmoid(o_ref[...].astype(F32))).astype(BF16)
    yf = yf_ref[...].astype(F32).reshape(x.shape).astype(BF16)
    y = (jax.nn.sigmoid(gm_ref[...].astype(F32)) * _dot(ym, wml_ref[...])
         + jax.nn.sigmoid(gd_ref[...].astype(F32)) * _dot(yd_ref[...], wda_ref[...])
         + jax.nn.sigmoid(gf_ref[...].astype(F32)) * _dot(yf, wfn_ref[...]))
    gate = _mod_row(mod_ref, i, rows_per_batch, n_batch, 2)
    out_ref[...] = x + gate * _dot(y.astype(BF16), wout_ref[...])


def _merge(hf, hb, u, yd, yf, x_lat, x_ctx, ctx_blk0, mods, head_g, wml, wda, wfn, wout, layer, need_ctx, n_batch):
    n_rows = n_batch * (SEQ + CTX_LEN)
    tm = _row_tile(n_batch)
    rows_per_batch = SEQ // tm
    lat_blocks = n_batch * rows_per_batch
    ni = (n_rows if need_ctx else n_batch * SEQ) // tm
    kern = functools.partial(_merge_kernel, rows_per_batch=rows_per_batch, n_batch=n_batch)
    row = lambda i: (i, 0)
    full = lambda i: (0, 0)
    wspec = pl.BlockSpec((None, D_MODEL, D_MODEL), lambda i: (layer, 0, 0))
    return pl.pallas_call(
        kern,
        grid=(ni,),
        in_specs=[
            pl.BlockSpec((tm, D_MODEL), row),
            pl.BlockSpec((tm, D_MODEL), row),
            pl.BlockSpec((tm, D_MODEL), lambda i: (i, U_OML)),
            pl.BlockSpec((tm, D_MODEL), lambda i: (i, U_GPRE)),
            pl.BlockSpec((tm, D_MODEL), lambda i: (i, U_GPRE + 1)),
            pl.BlockSpec((tm, D_MODEL), lambda i: (i, U_GPRE + 2)),
            pl.BlockSpec((tm, D_MODEL), row),
            pl.BlockSpec((tm // FFT_R, FFT_R * D_MODEL), row),
            pl.BlockSpec((tm, D_MODEL), lambda i: (jnp.minimum(i, lat_blocks - 1), 0)),
            pl.BlockSpec((tm, D_MODEL), lambda i: (ctx_blk0 + jnp.maximum(i - lat_blocks, 0), 0)),
            pl.BlockSpec((None, 8, N_MOD), lambda i: (layer, 0, 0)),
            pl.BlockSpec((1, D_MODEL), full),
            wspec, wspec, wspec, wspec,
        ],
        out_specs=pl.BlockSpec((tm, D_MODEL), row),
        out_shape=jax.ShapeDtypeStruct((ni * tm, D_MODEL), F32),
        compiler_params=_cparams(("arbitrary",)),
        name="merge_out_proj",
    )(hf, hb, u, u, u, u, yd, yf, x_lat, x_ctx, mods, head_g, wml, wda, wfn, wout)


FFN_CHUNKS = ((0, 1024), (1024, 1024), (2048, 768))


def _ffn_kernel(x_ref, mod_ref, g_ref, win_ref, wout_ref, fg_ref, out_ref, *, rows_per_batch, n_batch, final):
    i = pl.program_id(0)
    x = x_ref[...]
    shift = _mod_row(mod_ref, i, rows_per_batch, n_batch, 3)
    scale = _mod_row(mod_ref, i, rows_per_batch, n_batch, 4)
    gate = _mod_row(mod_ref, i, rows_per_batch, n_batch, 5)
    h = (_rms(x, g_ref[...]) * (1.0 + scale) + shift).astype(BF16)
    acc = None
    for lo, width in FFN_CHUNKS:
        a = _dot(h, win_ref[:, lo:lo + width])
        b = _dot(h, win_ref[:, D_FF + lo:D_FF + lo + width])
        act = (a * jax.nn.sigmoid(a) * b).astype(BF16)
        part = _dot(act, wout_ref[lo:lo + width, :])
        acc = part if acc is None else acc + part
    xn = x + gate * acc
    out_ref[...] = _rms(xn, fg_ref[...]) if final else xn


def _ffn(x, mods, g, w_in, w_out, layer, final_g, final, n_rows_out, n_batch):
    tm = _row_tile(n_batch)
    rows_per_batch = SEQ // tm
    kern = functools.partial(_ffn_kernel, rows_per_batch=rows_per_batch, n_batch=n_batch, final=final)
    row = lambda i: (i, 0)
    full = lambda i: (0, 0)
    return pl.pallas_call(
        kern,
        grid=(n_rows_out // tm,),
        in_specs=[
            pl.BlockSpec((tm, D_MODEL), row),
            pl.BlockSpec((None, 8, N_MOD), lambda i: (layer, 0, 0)),
            pl.BlockSpec((1, D_MODEL), full),
            pl.BlockSpec((None, D_MODEL, 2 * D_FF), lambda i: (layer, 0, 0)),
            pl.BlockSpec((None, D_FF, D_MODEL), lambda i: (layer, 0, 0)),
            pl.BlockSpec((1, D_MODEL), full),
        ],
        out_specs=pl.BlockSpec((tm, D_MODEL), row),
        out_shape=jax.ShapeDtypeStruct((n_rows_out, D_MODEL), F32),
        compiler_params=_cparams(("arbitrary",)),
        name="swiglu_ffn",
    )(x, mods, g, w_in, w_out, final_g)


def _da_col_perm(w):
    half = DA_DH // 2
    lead = w.shape[:-1]
    return jnp.swapaxes(w.reshape(lead + (DA_HEADS, 2, 2, half)), -3, -2).reshape(lead + (DA_HEADS * DA_DV,))


def _rope_tables(pad):
    n_freq = DA_DH // 4
    rows = SEQ // GRID_W
    inv = ROPE_BASE ** (-jnp.arange(n_freq, dtype=F32) / n_freq)
    r = jnp.repeat(jnp.arange(rows, dtype=F32), GRID_W)
    col = jnp.tile(jnp.arange(GRID_W, dtype=F32), rows)
    ang = jnp.concatenate([r[:, None] * inv, col[:, None] * inv], axis=-1)
    cos, sin = jnp.cos(ang), jnp.sin(ang)
    cos_t = jnp.concatenate([cos, cos, cos, cos], axis=-1)
    sin_t = jnp.concatenate([-sin, -sin, sin, sin], axis=-1)
    cos_t = jnp.concatenate([cos_t, jnp.ones((pad, 128), F32)], axis=0)
    sin_t = jnp.concatenate([sin_t, jnp.zeros((pad, 128), F32)], axis=0)
    return cos_t, sin_t


def kernel(x, c, ctx, c_ctx, w_ada, b_ada, norm_g, w_in, ml_gate_b, ml_head_g, da_lam, da_head_g,
           w_br_ml, w_br_da, w_br_fn, w_out, w_ffn_in, w_ffn_out, final_g):
    n_batch = x.shape[0]
    n_lat = n_batch * SEQ
    x_lat = x.reshape(n_lat, D_MODEL)
    x_ctx = ctx.reshape(n_batch * CTX_LEN, D_MODEL)
    cc = jnp.concatenate([c, c_ctx[None, :], jnp.zeros((8 - n_batch - 1, D_MODEL), F32)], axis=0)
    mods = _mods(cc, w_ada, b_ada)
    cos_t, sin_t = _rope_tables(n_batch * CTX_LEN)
    tables = _dft_tables()
    final_g2 = final_g.reshape(1, D_MODEL)
    wb_ml, wb_da, wb_fn, wb_out = _to_bf16(w_br_ml), _to_bf16(w_br_da), _to_bf16(w_br_fn), _to_bf16(w_out)
    wb_ffn_in, wb_ffn_out = _to_bf16(w_ffn_in), _to_bf16(w_ffn_out)

    gate_lo = 4 * D_MODEL
    da_lo = gate_lo + N_GATE
    fn_lo = da_lo + 3 * D_MODEL
    wb_in = _to_bf16(w_in)
    w_main = jnp.concatenate([wb_in[..., :D_MODEL], wb_in[..., 2 * D_MODEL:gate_lo],
                              _da_col_perm(wb_in[..., da_lo:da_lo + D_MODEL]),
                              _da_col_perm(wb_in[..., da_lo + D_MODEL:da_lo + 2 * D_MODEL]),
                              wb_in[..., da_lo + 2 * D_MODEL:fn_lo],
                              wb_in[..., fn_lo + D_MODEL:], wb_in[..., fn_lo:fn_lo + D_MODEL]], axis=-1)
    w_kt = jnp.swapaxes(wb_in[..., D_MODEL:2 * D_MODEL], 1, 2)
    w_gate_t = jnp.swapaxes(wb_in[..., gate_lo:da_lo], 1, 2)

    tm_in = n_batch * CTX_LEN
    tm_tok = _row_tile(n_batch)
    xs = None
    for l in range(DEPTH):
        need_ctx = l < DEPTH - 1
        lam_init = 0.8 - 0.6 * math.exp(-0.3 * l)
        xl, xc = (x_lat, x_ctx) if xs is None else (xs, xs)
        u, kt, fn, gates_t = _inproj(xl, xc, 0 if xs is None else n_lat // tm_in, mods,
                                     norm_g[l, 0].reshape(1, D_MODEL), w_main, w_kt, w_gate_t, l,
                                     cos_t, sin_t, n_batch)
        hf, hb = _mlstm(u, kt, gates_t, ml_gate_b[l], n_batch)
        yd = _attn(u, da_lam[l], da_head_g[l].reshape(1, D_MODEL), lam_init, need_ctx, n_batch)
        yf = _fourier(fn, tables, need_ctx, n_batch)
        xs = _merge(hf, hb, u, yd, yf, xl, xc, 0 if xs is None else n_lat // tm_tok, mods,
                    ml_head_g[l].reshape(1, D_MODEL), wb_ml, wb_da, wb_fn, wb_out, l, need_ctx, n_batch)
        final = l == DEPTH - 1
        n_out = n_lat if final else xs.shape[0]
        xs = _ffn(xs, mods, norm_g[l, 1].reshape(1, D_MODEL), wb_ffn_in, wb_ffn_out, l,
                  final_g2, final, n_out, n_batch)
    return xs.reshape(n_batch, SEQ, D_MODEL)
```

```python
import functools
import math

import jax
import jax.numpy as jnp
from jax import lax
from jax.experimental import pallas as pl
from jax.experimental.pallas import tpu as pltpu

D_MODEL = 1024
SEQ = 4096
DEPTH = 4
CTX_LEN = 256
GRID_W = 64
NORM_EPS = 1e-6

ML_HEADS = 4
ML_DK = 256
ML_CHUNK = 256

DA_HEADS = 8
DA_DH = 64
DA_DV = 2 * DA_DH
ROPE_BASE = 10000.0
ATTN_TQ = 512
ATTN_ROW_CHUNKS = 4

FN_GROUPS = 4
FN_GC = 256
FFT_R = 64

D_FF = 2816
N_GATE = 4 * ML_HEADS
N_MOD = 6 * D_MODEL

U_QML, U_VML, U_OML, U_QDA, U_KDA, U_VDA, U_GPRE = 0, 1, 2, 3, 4, 5, 6
U_BLOCKS = 9
W_BLOCKS = U_BLOCKS + 1
STEP_KT = 1
N_COL_STEPS = W_BLOCKS + 1

VMEM_LIMIT_V7X = 56 * 1024 * 1024

BF16 = jnp.bfloat16
F32 = jnp.float32


def _cparams(sem):
    return pltpu.CompilerParams(dimension_semantics=sem, vmem_limit_bytes=VMEM_LIMIT_V7X)


def _dot(a, b):
    return jnp.dot(a, b, preferred_element_type=F32)


def _dot_nt(a, b):
    return lax.dot_general(a, b, (((1,), (1,)), ((), ())), preferred_element_type=F32)


def _mod_row(mod_ref, i, rows_per_batch, n_batch, col):
    r = jnp.minimum(i // rows_per_batch, n_batch)
    return mod_ref[pl.ds(r, 1), col * D_MODEL:(col + 1) * D_MODEL]


def _row_tile(n_batch):
    return min(512, n_batch * CTX_LEN)


def _rms(x, g):
    return x * lax.rsqrt(jnp.mean(x * x, axis=-1, keepdims=True) + NORM_EPS) * g


def _cast_kernel(w_ref, o_ref):
    o_ref[...] = w_ref[...].astype(o_ref.dtype)


def _to_bf16(w):
    n_l, rows, cols = w.shape
    tr = 256
    return pl.pallas_call(
        _cast_kernel,
        grid=(n_l, rows // tr),
        in_specs=[pl.BlockSpec((1, tr, cols), lambda l, i: (l, i, 0))],
        out_specs=pl.BlockSpec((1, tr, cols), lambda l, i: (l, i, 0)),
        out_shape=jax.ShapeDtypeStruct(w.shape, BF16),
        compiler_params=_cparams(("arbitrary", "arbitrary")),
        name="cast_bf16",
    )(w)


def _mods_kernel(c_ref, w_ref, b_ref, o_ref):
    c = c_ref[...]
    s = (c * jax.nn.sigmoid(c)).astype(BF16)
    o_ref[0] = _dot(s, w_ref[0].astype(BF16)) + b_ref[0]


def _mods(cc, w_ada, b_ada):
    tn = 1536
    return pl.pallas_call(
        _mods_kernel,
        grid=(DEPTH, N_MOD // tn),
        in_specs=[
            pl.BlockSpec((8, D_MODEL), lambda l, j: (0, 0)),
            pl.BlockSpec((1, D_MODEL, tn), lambda l, j: (l, 0, j)),
            pl.BlockSpec((1, 1, tn), lambda l, j: (l, 0, j)),
        ],
        out_specs=pl.BlockSpec((1, 8, tn), lambda l, j: (l, 0, j)),
        out_shape=jax.ShapeDtypeStruct((DEPTH, 8, N_MOD), F32),
        compiler_params=_cparams(("arbitrary", "arbitrary")),
        name="adaln_mods",
    )(cc, w_ada, b_ada.reshape(DEPTH, 1, N_MOD))


def _inproj_kernel(xl_ref, xc_ref, mod_ref, g_ref, w_ref, wkt_ref, wgt_ref, cos_ref, sin_ref,
                   u_ref, kt_ref, fn_ref, gate_ref, xn_ref, *, rows_per_batch, n_batch):
    i = pl.program_id(0)
    j = pl.program_id(1)

    def normalise(x_ref):
        y = _rms(x_ref[...], g_ref[...])
        shift = _mod_row(mod_ref, i, rows_per_batch, n_batch, 0)
        scale = _mod_row(mod_ref, i, rows_per_batch, n_batch, 1)
        xn_ref[...] = (y * (1.0 + scale) + shift).astype(BF16)
        gate_ref[...] = _dot_nt(wgt_ref[...], xn_ref[...])

    is_lat = i < rows_per_batch * n_batch

    @pl.when(jnp.logical_and(j == 0, is_lat))
    def _():
        normalise(xl_ref)

    @pl.when(jnp.logical_and(j == 0, jnp.logical_not(is_lat)))
    def _():
        normalise(xc_ref)

    def product():
        return _dot(xn_ref[...], w_ref[...])

    blk = jnp.where(j == 0, 0, j - 1)
    is_rope = jnp.logical_or(blk == U_QDA, blk == U_KDA)
    is_plain = jnp.logical_and(j != STEP_KT, jnp.logical_and(jnp.logical_not(is_rope), blk < U_BLOCKS))

    @pl.when(is_plain)
    def _():
        u_ref[...] = product().astype(BF16)

    @pl.when(j == STEP_KT)
    def _():
        kt_ref[...] = (_dot_nt(wkt_ref[...], xn_ref[...]) * (ML_DK ** -0.5)).astype(BF16)

    @pl.when(is_rope)
    def _():
        acc = product()
        cos = cos_ref[...]
        sin = sin_ref[...]
        for t in range(acc.shape[1] // DA_DV):
            sl = slice(t * DA_DV, (t + 1) * DA_DV)
            x = acc[:, sl]
            u_ref[:, sl] = (x * cos + pltpu.roll(x, DA_DV // 2, 1) * sin).astype(BF16)

    @pl.when(blk == U_BLOCKS)
    def _():
        acc = product()
        for g in range(FN_GROUPS):
            z = acc[:, g * FN_GC:(g + 1) * FN_GC]
            fn_ref[g] = z.reshape(z.shape[0] // FFT_R, FFT_R * FN_GC).astype(BF16)


def _inproj(x_lat, x_ctx, ctx_blk, mods, g, w_main, w_kt, w_gate_t, layer, cos_t, sin_t, n_batch):
    tm = n_batch * CTX_LEN
    assert SEQ % tm == 0 and cos_t.shape[0] == SEQ + tm
    n_rows = n_batch * (SEQ + CTX_LEN)
    ni = n_rows // tm
    rows_per_batch = SEQ // tm
    lat_blocks = n_batch * rows_per_batch

    def tab_idx(i, j):
        return (jnp.where(i < lat_blocks, i % rows_per_batch, rows_per_batch), 0)

    def w_blk(j):
        return jnp.where(j == 0, 0, j - 1)

    kern = functools.partial(_inproj_kernel, rows_per_batch=rows_per_batch, n_batch=n_batch)
    return pl.pallas_call(
        kern,
        grid=(ni, N_COL_STEPS),
        in_specs=[
            pl.BlockSpec((tm, D_MODEL), lambda i, j: (jnp.minimum(i, lat_blocks - 1), 0)),
            pl.BlockSpec((tm, D_MODEL), lambda i, j: (ctx_blk, 0)),
            pl.BlockSpec((None, 8, N_MOD), lambda i, j: (layer, 0, 0)),
            pl.BlockSpec((1, D_MODEL), lambda i, j: (0, 0)),
            pl.BlockSpec((None, D_MODEL, D_MODEL), lambda i, j: (layer, 0, w_blk(j))),
            pl.BlockSpec((None, D_MODEL, D_MODEL), lambda i, j: (layer, 0, 0)),
            pl.BlockSpec((None, N_GATE, D_MODEL), lambda i, j: (layer, 0, 0)),
            pl.BlockSpec((tm, 128), tab_idx),
            pl.BlockSpec((tm, 128), tab_idx),
        ],
        out_specs=[
            pl.BlockSpec((tm, D_MODEL), lambda i, j: (i, jnp.minimum(w_blk(j), U_BLOCKS - 1))),
            pl.BlockSpec((D_MODEL, tm), lambda i, j: (0, i)),
            pl.BlockSpec((FN_GROUPS, tm // FFT_R, FFT_R * FN_GC), lambda i, j: (0, i, 0)),
            pl.BlockSpec((N_GATE, tm), lambda i, j: (0, i)),
        ],
        out_shape=[
            jax.ShapeDtypeStruct((n_rows, U_BLOCKS * D_MODEL), BF16),
            jax.ShapeDtypeStruct((D_MODEL, n_rows), BF16),
            jax.ShapeDtypeStruct((FN_GROUPS, n_rows // FFT_R, FFT_R * FN_GC), BF16),
            jax.ShapeDtypeStruct((N_GATE, n_rows), F32),
        ],
        scratch_shapes=[pltpu.VMEM((tm, D_MODEL), BF16)],
        compiler_params=_cparams(("arbitrary", "arbitrary")),
        name="in_proj",
    )(x_lat, x_ctx, mods, g, w_main, w_kt, w_gate_t, cos_t, sin_t)


def _split3(x):
    hi = x.astype(BF16).astype(F32)
    mid = (x - hi).astype(BF16).astype(F32)
    lo = (x - hi - mid).astype(BF16).astype(F32)
    return hi, mid, lo


def _mlstm_kernel(qf_ref, ktf_ref, vf_ref, gtf_ref, gtfn_ref, qb_ref, ktb_ref, vb_ref, gtb_ref, gtbn_ref, bias_ref,
                  hf_ref, hb_ref, *scratch):
    n_st = 2 * ML_HEADS
    cx_refs, bw_refs, pmw_refs = scratch[:n_st], scratch[n_st:2 * n_st], scratch[2 * n_st:3 * n_st]
    m_refs, c_refs, bend_refs, g_refs = (scratch[3 * n_st + 2 * k:3 * n_st + 2 * k + 2] for k in range(4))
    state_refs = cx_refs + m_refs
    s = pl.program_id(1)
    L = ML_CHUNK
    H = ML_HEADS
    W = 128

    @pl.when(s == 0)
    def _():
        for ref in state_refs:
            ref[...] = jnp.zeros_like(ref)

    t_idx = lax.broadcasted_iota(jnp.int32, (L, L), 0)
    s_idx = lax.broadcasted_iota(jnp.int32, (L, L), 1)
    eye = t_idx == s_idx
    sub8 = lax.broadcasted_iota(jnp.int32, (8, W), 0)
    ones_w = jnp.ones((L, W), BF16)
    er = lax.broadcasted_iota(jnp.int32, (4 * L, 2 * W), 0)
    ec = lax.broadcasted_iota(jnp.int32, (4 * L, 2 * W), 1)
    expand = jnp.where((er < 3 * L) == (ec < W), 1.0, 0.0).astype(BF16)

    def running_max_rows(x, d):
        n_tiles = L // 8
        out = [None] * n_tiles
        carry = None
        for j in (range(n_tiles) if d == 0 else range(n_tiles - 1, -1, -1)):
            r = x[8 * j:8 * (j + 1)]
            k = 1
            while k < 8:
                if d == 0:
                    r = jnp.maximum(r, jnp.where(sub8 >= k, pltpu.roll(r, k, 0), -jnp.inf))
                else:
                    r = jnp.maximum(r, jnp.where(sub8 < 8 - k, pltpu.roll(r, 8 - k, 0), -jnp.inf))
                k *= 2
            if carry is not None:
                r = jnp.maximum(r, carry)
            carry = jnp.broadcast_to(r[7:8] if d == 0 else r[0:1], (8, W))
            out[j] = r
        return jnp.concatenate(out, axis=0)

    def gate_part(d, gt_ref):
        before = (t_idx <= s_idx) if d == 0 else (t_idx >= s_idx)
        gt = gt_ref[2 * H * d:2 * H * (d + 1), :] + bias_ref[2 * H * d:2 * H * (d + 1), :]
        i4, f4 = gt[0:H], gt[H:2 * H]
        lf4 = jnp.minimum(f4, 0.0) - jnp.log1p(jnp.exp(-jnp.abs(f4)))
        lf_terms = jnp.concatenate(list(_split3(lf4)) + [jnp.zeros((H, L), F32)], axis=0).astype(BF16)
        cum_rhs = jnp.concatenate([jnp.where(before, 1.0, 0.0).astype(BF16), ones_w], axis=1)
        r = _dot(lf_terms, cum_rhs)
        bx = r[0:H] + r[H:2 * H] + r[2 * H:3 * H]
        b4, bend4 = bx[:, 0:L], bx[:, L:L + W]
        c4 = i4 - b4
        c_refs[d][...] = c4
        bend_refs[d][...] = bend4
        g_refs[d][...] = bend4[:, 0:1] - b4 + i4
        col_terms = _split3(b4) + (c4.astype(BF16).astype(F32),)
        for h in range(H):
            diag = jnp.concatenate([jnp.where(eye, x[h:h + 1, :], 0.0) for x in col_terms], axis=1).astype(BF16)
            wide = _dot(diag, expand)
            bw_refs[d * H + h][...] = wide[:, 0:W]
            pmw_refs[d * H + h][...] = running_max_rows(wide[:, W:2 * W], d)

    dirs = ((0, qf_ref, ktf_ref, vf_ref, gtf_ref, gtfn_ref, hf_ref),
            (1, qb_ref, ktb_ref, vb_ref, gtb_ref, gtbn_ref, hb_ref))

    @pl.when(s == 0)
    def _():
        for d, _, _, _, gt_ref, _, _ in dirs:
            gate_part(d, gt_ref)

    for d, q_ref, kt_ref, v_ref, _, _, h_ref in dirs:
        causal = (s_idx <= t_idx) if d == 0 else (s_idx >= t_idx)
        c4, bend4, g4 = c_refs[d][...], bend_refs[d][...], g_refs[d][...]
        m_prev4 = m_refs[d][...]
        m_new4 = jnp.maximum(bend4 + m_prev4, jnp.max(g4, axis=1, keepdims=True))
        m_refs[d][...] = m_new4
        decay4 = jnp.exp(bend4 + m_prev4 - m_new4)
        w4 = jnp.exp(g4 - m_new4[:, 0:1])
        for h in range(H):
            st = d * H + h
            hs = slice(h * ML_DK, (h + 1) * ML_DK)
            b_w = bw_refs[st][...]
            m_w = jnp.maximum(pmw_refs[st][...], m_prev4[h:h + 1, :])

            q = q_ref[:, hs]
            kt = kt_ref[hs, :]
            vx = jnp.concatenate([v_ref[:, hs], ones_w], axis=1)
            m_ll = jnp.concatenate([m_w] * (L // W), axis=1)
            a = (jnp.where(causal, jnp.exp(c4[h:h + 1, :] - m_ll), 0.0) * _dot(q, kt)).astype(BF16)
            cx_prev = cx_refs[st][...]
            qc = _dot(q, cx_prev.astype(BF16))
            av = _dot(a, vx)
            sc_w = jnp.exp(m_prev4[h:h + 1, :] - m_w)
            den = sc_w * qc[:, ML_DK:] + av[:, ML_DK:]
            inv = 1.0 / jnp.maximum(jnp.abs(den), jnp.exp(-(b_w + m_w)))
            for t in range(ML_DK // W):
                ts = slice(t * W, (t + 1) * W)
                h_ref[:, h * ML_DK + t * W:h * ML_DK + (t + 1) * W] = (
                    (sc_w * qc[:, ts] + av[:, ts]) * inv).astype(h_ref.dtype)

            kw = (kt.astype(F32) * w4[h:h + 1, :]).astype(BF16)
            dec = jnp.concatenate([decay4[h:h + 1, :]] * (ML_DK // W + 1), axis=1)
            cx_refs[st][...] = dec * cx_prev + _dot(kw, vx)

    for d, _, _, _, _, gtn_ref, _ in dirs:
        gate_part(d, gtn_ref)


def _mlstm(u, kt, gates_t, gate_b, n_batch):
    n_rows = u.shape[0]
    L = ML_CHUNK
    lat_chunks = SEQ // L
    ctx_chunks = CTX_LEN // L
    n_steps = ctx_chunks + lat_chunks
    ctx_base = n_batch * lat_chunks

    def rowblk(d):
        def f(b, s):
            in_ctx = s < ctx_chunks
            if d == 0:
                c = jnp.where(in_ctx, s, s - ctx_chunks)
            else:
                c = jnp.where(in_ctx, ctx_chunks - 1 - s, lat_chunks - 1 - (s - ctx_chunks))
            return jnp.where(in_ctx, ctx_base + ctx_chunks * b, lat_chunks * b) + c
        return f

    def dir_specs(d):
        rb = rowblk(d)
        return [
            pl.BlockSpec((L, D_MODEL), lambda b, s: (rb(b, s), U_QML)),
            pl.BlockSpec((D_MODEL, L), lambda b, s: (0, rb(b, s))),
            pl.BlockSpec((L, D_MODEL), lambda b, s: (rb(b, s), U_VML)),
            pl.BlockSpec((N_GATE, L), lambda b, s: (0, rb(b, s))),
            pl.BlockSpec((N_GATE, L), lambda b, s: (0, rb(b, jnp.minimum(s + 1, n_steps - 1)))),
        ]

    def out_spec(d):
        rb = rowblk(d)
        return pl.BlockSpec((L, D_MODEL), lambda b, s: (rb(b, s), 0))

    n_st = 2 * ML_HEADS
    return pl.pallas_call(
        _mlstm_kernel,
        grid=(n_batch, n_steps),
        in_specs=dir_specs(0) + dir_specs(1) + [pl.BlockSpec((N_GATE, 1), lambda b, s: (0, 0))],
        out_specs=[out_spec(0), out_spec(1)],
        out_shape=[jax.ShapeDtypeStruct((n_rows, D_MODEL), BF16)] * 2,
        scratch_shapes=(
            [pltpu.VMEM((ML_DK, ML_DK + 128), F32)] * n_st
            + [pltpu.VMEM((L, 128), F32)] * (2 * n_st)
            + [pltpu.VMEM((ML_HEADS, 128), F32)] * 2
            + [pltpu.VMEM((ML_HEADS, L), F32)] * 2
            + [pltpu.VMEM((ML_HEADS, 128), F32)] * 2
            + [pltpu.VMEM((ML_HEADS, L), F32)] * 2),
        compiler_params=_cparams(("arbitrary", "arbitrary")),
        name="mlstm_scan",
    )(u, kt, u, gates_t, gates_t, u, kt, u, gates_t, gates_t, gate_b.reshape(N_GATE, 1))


def _attn_lambda(lam_ref, lam_init):
    lq = lam_ref[...]
    return (jnp.exp(jnp.sum(lq[0:1] * lq[1:2], axis=1, keepdims=True))
            - jnp.exp(jnp.sum(lq[2:3] * lq[3:4], axis=1, keepdims=True)) + lam_init)


def _attn_queries(q_ref):
    q = q_ref[...]
    lane = lax.broadcasted_iota(jnp.int32, (1, DA_DV), 1)
    zero = jnp.zeros_like(q)
    is_map0 = (lane % DA_DH) < (DA_DH // 2)
    q2 = jnp.concatenate([jnp.where(is_map0, q, zero), jnp.where(is_map0, zero, q)], axis=0)
    return q2 * (DA_DH ** -0.5)


def _attn_sums(s_chunks, m_chunks, vx):
    return jnp.concatenate([_dot(jnp.exp(s - m).astype(BF16), vx) for s, m in zip(s_chunks, m_chunks)], axis=0)


def _attn_finish(acc, lam, g, lam_init, tq):
    o0 = acc[0:tq, 0:DA_DV] * (1.0 / acc[0:tq, DA_DV:DA_DV + 1])
    o1 = acc[tq:, 0:DA_DV] * (1.0 / acc[tq:, DA_DV:DA_DV + 1])
    return _rms(o0 - lam * o1, g) * (1.0 - lam_init)


def _attn_kernel(q_ref, kl_ref, vl_ref, kc_ref, vc_ref, lam_ref, g_ref, o_ref, vx_ref, s_ref, m_ref,
                 *, lam_init, q_blocks):
    t = pl.program_id(0)
    tq = q_ref.shape[0]
    rows = 2 * tq // ATTN_ROW_CHUNKS

    @pl.when(jnp.logical_and(t > 0, (t - 1) % q_blocks == 0))
    def _():
        vx_ref[0:SEQ, 0:DA_DV] = vl_ref[...]
        vx_ref[SEQ:, 0:DA_DV] = vc_ref[...]
        lane_v = lax.broadcasted_iota(jnp.int32, (SEQ + CTX_LEN, DA_DV), 1)
        vx_ref[:, DA_DV:] = jnp.where(lane_v == 0, 1.0, 0.0).astype(BF16)

    def score(slot):
        q2 = _attn_queries(q_ref)
        s_lat = _dot_nt(q2, kl_ref[...])
        s_ctx = _dot_nt(q2, kc_ref[...])
        m_ref[slot] = jnp.maximum(jnp.max(s_lat, axis=1, keepdims=True), jnp.max(s_ctx, axis=1, keepdims=True))
        s_ref[slot, :, 0:SEQ] = s_lat
        s_ref[slot, :, SEQ:] = s_ctx

    def drain(slot):
        s_chunks = [s_ref[slot, c * rows:(c + 1) * rows, :] for c in range(ATTN_ROW_CHUNKS)]
        m_chunks = [m_ref[slot, c * rows:(c + 1) * rows, :] for c in range(ATTN_ROW_CHUNKS)]
        acc = _attn_sums(s_chunks, m_chunks, vx_ref[...])
        o = _attn_finish(acc, _attn_lambda(lam_ref, lam_init), g_ref[...], lam_init, tq)
        o_ref[...] = o.astype(o_ref.dtype)

    @pl.when(t == 0)
    def _():
        score(0)

    for parity in (0, 1):
        @pl.when(jnp.logical_and(t > 0, t % 2 == parity))
        def _():
            drain(1 - parity)
            score(parity)


def _attn_ctx_kernel(q_ref, kc_ref, vc_ref, lam_ref, g_ref, yin_ref, o_ref, *, lam_init):
    del yin_ref
    tq = q_ref.shape[0]
    s = _dot_nt(_attn_queries(q_ref), kc_ref[...])
    lane_v = lax.broadcasted_iota(jnp.int32, (CTX_LEN, DA_DV), 1)
    vx = jnp.concatenate([vc_ref[...], jnp.where(lane_v == 0, 1.0, 0.0).astype(BF16)], axis=1)
    acc = _attn_sums([s], [jnp.max(s, axis=1, keepdims=True)], vx)
    o = _attn_finish(acc, _attn_lambda(lam_ref, lam_init), g_ref[...], lam_init, tq)
    o_ref[...] = o.astype(o_ref.dtype)


def _attn(u, da_lam, head_g, lam_init, need_ctx, n_batch):
    n_rows = u.shape[0]
    tq = ATTN_TQ
    lat_qblocks = SEQ // tq
    ctx_base = n_batch * (SEQ // CTX_LEN)
    cpb = D_MODEL // DA_DV
    n_keys = SEQ + CTX_LEN

    n_items = n_batch * DA_HEADS * lat_qblocks

    def item(t):
        t = jnp.clip(t, 0, n_items - 1)
        return t // (DA_HEADS * lat_qblocks), (t // lat_qblocks) % DA_HEADS, t % lat_qblocks

    def scored(f):
        return lambda t: f(*item(t))

    def drained(f):
        return lambda t: f(*item(t - 1))

    kern = functools.partial(_attn_kernel, lam_init=lam_init, q_blocks=lat_qblocks)
    yd = pl.pallas_call(
        kern,
        grid=(n_items + 1,),
        in_specs=[
            pl.BlockSpec((tq, DA_DV), scored(lambda b, h, qi: (b * lat_qblocks + qi, U_QDA * cpb + h))),
            pl.BlockSpec((SEQ, DA_DV), scored(lambda b, h, qi: (b, U_KDA * cpb + h))),
            pl.BlockSpec((SEQ, DA_DV), drained(lambda b, h, qi: (b, U_VDA * cpb + h))),
            pl.BlockSpec((CTX_LEN, DA_DV), scored(lambda b, h, qi: (ctx_base + b, U_KDA * cpb + h))),
            pl.BlockSpec((CTX_LEN, DA_DV), drained(lambda b, h, qi: (ctx_base + b, U_VDA * cpb + h))),
            pl.BlockSpec((4, DA_DH), lambda t: (0, 0)),
            pl.BlockSpec((1, DA_DV), drained(lambda b, h, qi: (0, h))),
        ],
        out_specs=pl.BlockSpec((tq, DA_DV), drained(lambda b, h, qi: (b * lat_qblocks + qi, h))),
        out_shape=jax.ShapeDtypeStruct((n_rows, D_MODEL), BF16),
        scratch_shapes=[
            pltpu.VMEM((n_keys, 2 * DA_DV), BF16),
            pltpu.VMEM((2, 2 * tq, n_keys), F32),
            pltpu.VMEM((2, 2 * tq, 1), F32),
        ],
        compiler_params=_cparams(("arbitrary",)),
        name="diff_attn",
    )(u, u, u, u, u, da_lam, head_g)
    if not need_ctx:
        return yd
    kern_ctx = functools.partial(_attn_ctx_kernel, lam_init=lam_init)
    return pl.pallas_call(
        kern_ctx,
        grid=(n_batch, DA_HEADS),
        in_specs=[
            pl.BlockSpec((CTX_LEN, DA_DV), lambda b, h: (ctx_base + b, U_QDA * cpb + h)),
            pl.BlockSpec((CTX_LEN, DA_DV), lambda b, h: (ctx_base + b, U_KDA * cpb + h)),
            pl.BlockSpec((CTX_LEN, DA_DV), lambda b, h: (ctx_base + b, U_VDA * cpb + h)),
            pl.BlockSpec((4, DA_DH), lambda b, h: (0, 0)),
            pl.BlockSpec((1, DA_DV), lambda b, h: (0, h)),
            pl.BlockSpec(memory_space=pl.ANY),
        ],
        out_specs=pl.BlockSpec((CTX_LEN, DA_DV), lambda b, h: (ctx_base + b, h)),
        out_shape=jax.ShapeDtypeStruct((n_rows, D_MODEL), BF16),
        input_output_aliases={5: 0},
        compiler_params=_cparams(("arbitrary", "arbitrary")),
        name="diff_attn_ctx",
    )(u, u, u, da_lam, head_g, yd)


def _dft_tables():
    R = FFT_R

    def cs(num, period):
        ang = (num % period).astype(F32) * (2.0 * math.pi / period)
        return jnp.cos(ang), jnp.sin(ang)

    idx = jnp.arange(R, dtype=jnp.int32)
    c1, s1 = cs(idx[:, None] * idx[None, :], R)
    a1 = jnp.concatenate([c1, -s1], axis=0).astype(BF16)
    f2 = idx[:, None, None]
    f1 = idx[None, :, None]
    t1 = idx[None, None, :]
    mc, ms = cs(t1 * (R * f1 + f2), SEQ)
    b2 = jnp.concatenate([jnp.concatenate([mc, ms], axis=2),
                          jnp.concatenate([-ms, mc], axis=2)], axis=1).astype(BF16)
    ch = jnp.arange(FN_GC, dtype=jnp.int32)
    cc, sc = cs(ch[:, None] * ch[None, :], FN_GC)
    csm = jnp.concatenate([cc, sc], axis=0).astype(BF16)
    actx = jnp.concatenate([cc, -sc], axis=0).astype(BF16)
    return a1, b2, csm, actx


def _fft1_kernel(a_ref, z_ref, p_ref):
    p_ref[0, 0] = _dot(a_ref[...], z_ref[0]).astype(p_ref.dtype)


def _fft1(fn_view, a1, n_batch):
    R = FFT_R
    lanes = R * FN_GC
    lc = lanes
    return pl.pallas_call(
        _fft1_kernel,
        grid=(FN_GROUPS, n_batch, lanes // lc),
        in_specs=[
            pl.BlockSpec((2 * R, R), lambda g, b, c: (0, 0)),
            pl.BlockSpec((1, R, lc), lambda g, b, c: (g, b, c)),
        ],
        out_specs=pl.BlockSpec((1, 1, 2 * R, lc), lambda g, b, c: (g, b, 0, c)),
        out_shape=jax.ShapeDtypeStruct((FN_GROUPS, n_batch, 2 * R, lanes), BF16),
        compiler_params=_cparams(("arbitrary", "arbitrary", "arbitrary")),
        name="fourier_stage1",
    )(a1, fn_view)


def _fft2_kernel(p_ref, b_ref, cs_ref, o_ref, *, f2b):
    R = FFT_R
    cc = cs_ref[0:FN_GC, :]
    sc = cs_ref[FN_GC:2 * FN_GC, :]
    norm = 1.0 / math.sqrt(SEQ * FN_GC)
    for g in range(FN_GROUPS):
        xr, xi = [], []
        for jj in range(f2b):
            stacked = jnp.concatenate([p_ref[g, 0, 0, jj], p_ref[g, 0, 1, jj]], axis=0)
            x = _dot(b_ref[jj], stacked)
            xr.append(x[0:R])
            xi.append(x[R:2 * R])
        y = (_dot(jnp.concatenate(xr, axis=0).astype(BF16), cc)
             + _dot(jnp.concatenate(xi, axis=0).astype(BF16), sc)) * norm
        for jj in range(f2b):
            lo = jj * D_MODEL + g * FN_GC
            o_ref[:, lo:lo + FN_GC] = y[jj * R:(jj + 1) * R].astype(o_ref.dtype)


def _fft2(p6, b2, csm, n_rows, n_batch):
    R = FFT_R
    f2b = 8
    kern = functools.partial(_fft2_kernel, f2b=f2b)
    return pl.pallas_call(
        kern,
        grid=(n_batch, R // f2b),
        in_specs=[
            pl.BlockSpec((FN_GROUPS, 1, 2, f2b, R, FN_GC), lambda b, f: (0, b, 0, f, 0, 0)),
            pl.BlockSpec((f2b, 2 * R, 2 * R), lambda b, f: (f, 0, 0)),
            pl.BlockSpec((2 * FN_GC, FN_GC), lambda b, f: (0, 0)),
        ],
        out_specs=pl.BlockSpec((R, f2b * D_MODEL), lambda b, f: (b, f)),
        out_shape=jax.ShapeDtypeStruct((n_rows // R, R * D_MODEL), BF16),
        compiler_params=_cparams(("arbitrary", "arbitrary")),
        name="fourier_stage2",
    )(p6, b2, csm)


def _fft_ctx_kernel(z_ref, a_ref, cs_ref, yin_ref, o_ref, *, n_batch):
    del yin_ref
    cc = cs_ref[0:FN_GC, :]
    sc = cs_ref[FN_GC:2 * FN_GC, :]
    norm = 1.0 / math.sqrt(CTX_LEN * FN_GC)
    n_ctx = n_batch * CTX_LEN
    groups = []
    for g in range(FN_GROUPS):
        z_all = z_ref[g].astype(F32).reshape(n_ctx, FN_GC)
        ys = []
        for b in range(n_batch):
            p = _dot(a_ref[...], z_all[b * CTX_LEN:(b + 1) * CTX_LEN].astype(BF16))
            ys.append(_dot(p[0:CTX_LEN].astype(BF16), cc) + _dot(p[CTX_LEN:2 * CTX_LEN].astype(BF16), sc))
        groups.append(jnp.concatenate(ys, axis=0))
    y = jnp.concatenate(groups, axis=1) * norm
    o_ref[...] = y.reshape(n_ctx // FFT_R, FFT_R * D_MODEL).astype(o_ref.dtype)


def _fft_ctx(fn_view, actx, csm, yf, n_batch):
    n_ctx = n_batch * CTX_LEN
    ctx_blk = n_batch * SEQ // n_ctx
    kern = functools.partial(_fft_ctx_kernel, n_batch=n_batch)
    return pl.pallas_call(
        kern,
        grid=(1,),
        in_specs=[
            pl.BlockSpec((FN_GROUPS, n_ctx // FFT_R, FFT_R * FN_GC), lambda i: (0, ctx_blk, 0)),
            pl.BlockSpec((2 * CTX_LEN, CTX_LEN), lambda i: (0, 0)),
            pl.BlockSpec((2 * FN_GC, FN_GC), lambda i: (0, 0)),
            pl.BlockSpec(memory_space=pl.ANY),
        ],
        out_specs=pl.BlockSpec((n_ctx // FFT_R, FFT_R * D_MODEL), lambda i: (ctx_blk, 0)),
        out_shape=jax.ShapeDtypeStruct(yf.shape, BF16),
        input_output_aliases={3: 0},
        compiler_params=_cparams(("arbitrary",)),
        name="fourier_ctx",
    )(fn_view, actx, csm, yf)


def _fourier(fn, tables, need_ctx, n_batch):
    a1, b2, csm, actx = tables
    R = FFT_R
    n_rows = fn.shape[1] * R
    p = _fft1(fn, a1, n_batch)
    yf = _fft2(p.reshape(FN_GROUPS, n_batch, 2, R, R, FN_GC), b2, csm, n_rows, n_batch)
    if need_ctx:
        yf = _fft_ctx(fn, actx, csm, yf, n_batch)
    return yf


def _merge_kernel(hf_ref, hb_ref, o_ref, gm_ref, gd_ref, gf_ref, yd_ref, yf_ref, xl_ref, xc_ref, mod_ref, hg_ref,
                  wml_ref, wda_ref, wfn_ref, wout_ref, out_ref, *, rows_per_batch, n_batch):
    i = pl.program_id(0)
    x = jnp.where(i < rows_per_batch * n_batch, xl_ref[...], xc_ref[...])
    hsum = hf_ref[...].astype(F32) + hb_ref[...].astype(F32)
    hg = hg_ref[...]
    parts = []
    for h in range(ML_HEADS):
        hs = slice(h * ML_DK, (h + 1) * ML_DK)
        parts.append(_rms(hsum[:, hs], hg[:, hs]))
    ym = (jnp.concatenate(parts, axis=1) * jax.nn.sigmoid(o_ref[...].astype(F32))).astype(BF16)
    yf = yf_ref[...].astype(F32).reshape(x.shape).astype(BF16)
    y = (jax.nn.sigmoid(gm_ref[...].astype(F32)) * _dot(ym, wml_ref[...])
         + jax.nn.sigmoid(gd_ref[...].astype(F32)) * _dot(yd_ref[...], wda_ref[...])
         + jax.nn.sigmoid(gf_ref[...].astype(F32)) * _dot(yf, wfn_ref[...]))
    gate = _mod_row(mod_ref, i, rows_per_batch, n_batch, 2)
    out_ref[...] = x + gate * _dot(y.astype(BF16), wout_ref[...])


def _merge(hf, hb, u, yd, yf, x_lat, x_ctx, ctx_blk0, mods, head_g, wml, wda, wfn, wout, layer, need_ctx, n_batch):
    n_rows = n_batch * (SEQ + CTX_LEN)
    tm = _row_tile(n_batch)
    rows_per_batch = SEQ // tm
    lat_blocks = n_batch * rows_per_batch
    ni = (n_rows if need_ctx else n_batch * SEQ) // tm
    kern = functools.partial(_merge_kernel, rows_per_batch=rows_per_batch, n_batch=n_batch)
    row = lambda i: (i, 0)
    full = lambda i: (0, 0)
    wspec = pl.BlockSpec((None, D_MODEL, D_MODEL), lambda i: (layer, 0, 0))
    return pl.pallas_call(
        kern,
        grid=(ni,),
        in_specs=[
            pl.BlockSpec((tm, D_MODEL), row),
            pl.BlockSpec((tm, D_MODEL), row),
            pl.BlockSpec((tm, D_MODEL), lambda i: (i, U_OML)),
            pl.BlockSpec((tm, D_MODEL), lambda i: (i, U_GPRE)),
            pl.BlockSpec((tm, D_MODEL), lambda i: (i, U_GPRE + 1)),
            pl.BlockSpec((tm, D_MODEL), lambda i: (i, U_GPRE + 2)),
            pl.BlockSpec((tm, D_MODEL), row),
            pl.BlockSpec((tm // FFT_R, FFT_R * D_MODEL), row),
            pl.BlockSpec((tm, D_MODEL), lambda i: (jnp.minimum(i, lat_blocks - 1), 0)),
            pl.BlockSpec((tm, D_MODEL), lambda i: (ctx_blk0 + jnp.maximum(i - lat_blocks, 0), 0)),
            pl.BlockSpec((None, 8, N_MOD), lambda i: (layer, 0, 0)),
            pl.BlockSpec((1, D_MODEL), full),
            wspec, wspec, wspec, wspec,
        ],
        out_specs=pl.BlockSpec((tm, D_MODEL), row),
        out_shape=jax.ShapeDtypeStruct((ni * tm, D_MODEL), F32),
        compiler_params=_cparams(("arbitrary",)),
        name="merge_out_proj",
    )(hf, hb, u, u, u, u, yd, yf, x_lat, x_ctx, mods, head_g, wml, wda, wfn, wout)


FFN_CHUNKS = ((0, 1024), (1024, 1024), (2048, 768))


def _ffn_kernel(x_ref, mod_ref, g_ref, win_ref, wout_ref, fg_ref, out_ref, *, rows_per_batch, n_batch, final):
    i = pl.program_id(0)
    x = x_ref[...]
    shift = _mod_row(mod_ref, i, rows_per_batch, n_batch, 3)
    scale = _mod_row(mod_ref, i, rows_per_batch, n_batch, 4)
    gate = _mod_row(mod_ref, i, rows_per_batch, n_batch, 5)
    h = (_rms(x, g_ref[...]) * (1.0 + scale) + shift).astype(BF16)
    acc = None
    for lo, width in FFN_CHUNKS:
        a = _dot(h, win_ref[:, lo:lo + width])
        b = _dot(h, win_ref[:, D_FF + lo:D_FF + lo + width])
        act = (a * jax.nn.sigmoid(a) * b).astype(BF16)
        part = _dot(act, wout_ref[lo:lo + width, :])
        acc = part if acc is None else acc + part
    xn = x + gate * acc
    out_ref[...] = _rms(xn, fg_ref[...]) if final else xn


def _ffn(x, mods, g, w_in, w_out, layer, final_g, final, n_rows_out, n_batch):
    tm = _row_tile(n_batch)
    rows_per_batch = SEQ // tm
    kern = functools.partial(_ffn_kernel, rows_per_batch=rows_per_batch, n_batch=n_batch, final=final)
    row = lambda i: (i, 0)
    full = lambda i: (0, 0)
    return pl.pallas_call(
        kern,
        grid=(n_rows_out // tm,),
        in_specs=[
            pl.BlockSpec((tm, D_MODEL), row),
            pl.BlockSpec((None, 8, N_MOD), lambda i: (layer, 0, 0)),
            pl.BlockSpec((1, D_MODEL), full),
            pl.BlockSpec((None, D_MODEL, 2 * D_FF), lambda i: (layer, 0, 0)),
            pl.BlockSpec((None, D_FF, D_MODEL), lambda i: (layer, 0, 0)),
            pl.BlockSpec((1, D_MODEL), full),
        ],
        out_specs=pl.BlockSpec((tm, D_MODEL), row),
        out_shape=jax.ShapeDtypeStruct((n_rows_out, D_MODEL), F32),
        compiler_params=_cparams(("arbitrary",)),
        name="swiglu_ffn",
    )(x, mods, g, w_in, w_out, final_g)


def _da_col_perm(w):
    half = DA_DH // 2
    lead = w.shape[:-1]
    return jnp.swapaxes(w.reshape(lead + (DA_HEADS, 2, 2, half)), -3, -2).reshape(lead + (DA_HEADS * DA_DV,))


def _rope_tables(pad):
    n_freq = DA_DH // 4
    rows = SEQ // GRID_W
    inv = ROPE_BASE ** (-jnp.arange(n_freq, dtype=F32) / n_freq)
    r = jnp.repeat(jnp.arange(rows, dtype=F32), GRID_W)
    col = jnp.tile(jnp.arange(GRID_W, dtype=F32), rows)
    ang = jnp.concatenate([r[:, None] * inv, col[:, None] * inv], axis=-1)
    cos, sin = jnp.cos(ang), jnp.sin(ang)
    cos_t = jnp.concatenate([cos, cos, cos, cos], axis=-1)
    sin_t = jnp.concatenate([-sin, -sin, sin, sin], axis=-1)
    cos_t = jnp.concatenate([cos_t, jnp.ones((pad, 128), F32)], axis=0)
    sin_t = jnp.concatenate([sin_t, jnp.zeros((pad, 128), F32)], axis=0)
    return cos_t, sin_t


def kernel(x, c, ctx, c_ctx, w_ada, b_ada, norm_g, w_in, ml_gate_b, ml_head_g, da_lam, da_head_g,
           w_br_ml, w_br_da, w_br_fn, w_out, w_ffn_in, w_ffn_out, final_g):
    n_batch = x.shape[0]
    n_lat = n_batch * SEQ
    x_lat = x.reshape(n_lat, D_MODEL)
    x_ctx = ctx.reshape(n_batch * CTX_LEN, D_MODEL)
    cc = jnp.concatenate([c, c_ctx[None, :], jnp.zeros((8 - n_batch - 1, D_MODEL), F32)], axis=0)
    mods = _mods(cc, w_ada, b_ada)
    cos_t, sin_t = _rope_tables(n_batch * CTX_LEN)
    tables = _dft_tables()
    final_g2 = final_g.reshape(1, D_MODEL)
    wb_ml, wb_da, wb_fn, wb_out = _to_bf16(w_br_ml), _to_bf16(w_br_da), _to_bf16(w_br_fn), _to_bf16(w_out)
    wb_ffn_in, wb_ffn_out = _to_bf16(w_ffn_in), _to_bf16(w_ffn_out)

    gate_lo = 4 * D_MODEL
    da_lo = gate_lo + N_GATE
    fn_lo = da_lo + 3 * D_MODEL
    w_main = jnp.concatenate([w_in[..., :D_MODEL], w_in[..., 2 * D_MODEL:gate_lo],
                              _da_col_perm(w_in[..., da_lo:da_lo + D_MODEL]),
                              _da_col_perm(w_in[..., da_lo + D_MODEL:da_lo + 2 * D_MODEL]),
                              w_in[..., da_lo + 2 * D_MODEL:fn_lo],
                              w_in[..., fn_lo + D_MODEL:], w_in[..., fn_lo:fn_lo + D_MODEL]], axis=-1).astype(BF16)
    w_kt = jnp.swapaxes(w_in[..., D_MODEL:2 * D_MODEL], 1, 2).astype(BF16)
    w_gate_t = jnp.swapaxes(w_in[..., gate_lo:da_lo], 1, 2).astype(BF16)

    tm_in = n_batch * CTX_LEN
    tm_tok = _row_tile(n_batch)
    xs = None
    for l in range(DEPTH):
        need_ctx = l < DEPTH - 1
        lam_init = 0.8 - 0.6 * math.exp(-0.3 * l)
        xl, xc = (x_lat, x_ctx) if xs is None else (xs, xs)
        u, kt, fn, gates_t = _inproj(xl, xc, 0 if xs is None else n_lat // tm_in, mods,
                                     norm_g[l, 0].reshape(1, D_MODEL), w_main, w_kt, w_gate_t, l,
                                     cos_t, sin_t, n_batch)
        hf, hb = _mlstm(u, kt, gates_t, ml_gate_b[l], n_batch)
        yd = _attn(u, da_lam[l], da_head_g[l].reshape(1, D_MODEL), lam_init, need_ctx, n_batch)
        yf = _fourier(fn, tables, need_ctx, n_batch)
        xs = _merge(hf, hb, u, yd, yf, xl, xc, 0 if xs is None else n_lat // tm_tok, mods,
                    ml_head_g[l].reshape(1, D_MODEL), wb_ml, wb_da, wb_fn, wb_out, l, need_ctx, n_batch)
        final = l == DEPTH - 1
        n_out = n_lat if final else xs.shape[0]
        xs = _ffn(xs, mods, norm_g[l, 1].reshape(1, D_MODEL), wb_ffn_in, wb_ffn_out, l,
                  final_g2, final, n_out, n_batch)
    return xs.reshape(n_batch, SEQ, D_MODEL)
```

```python
import functools
import math

import jax
import jax.numpy as jnp
from jax import lax
from jax.experimental import pallas as pl
from jax.experimental.pallas import tpu as pltpu

D_MODEL = 1024
SEQ = 4096
DEPTH = 4
CTX_LEN = 256
GRID_W = 64
NORM_EPS = 1e-6

ML_HEADS = 4
ML_DK = 256
ML_CHUNK = 256

DA_HEADS = 8
DA_DH = 64
DA_DV = 2 * DA_DH
ROPE_BASE = 10000.0
ATTN_TQ = 512
ATTN_ROW_CHUNKS = 2

FN_GROUPS = 4
FN_GC = 256
FFT_R = 64

D_FF = 2816
N_GATE = 4 * ML_HEADS
N_MOD = 6 * D_MODEL

U_QML, U_VML, U_OML, U_QDA, U_KDA, U_VDA, U_GPRE = 0, 1, 2, 3, 4, 5, 6
U_BLOCKS = 9
W_BLOCKS = U_BLOCKS + 1
STEP_KT = 1
N_COL_STEPS = W_BLOCKS + 1

VMEM_LIMIT_V7X = 56 * 1024 * 1024
LANES_V7X = 128

BF16 = jnp.bfloat16
F32 = jnp.float32


def _cparams(sem):
    return pltpu.CompilerParams(dimension_semantics=sem, vmem_limit_bytes=VMEM_LIMIT_V7X)


def _dot(a, b):
    return jnp.dot(a, b, preferred_element_type=F32)


def _dot_nt(a, b):
    return lax.dot_general(a, b, (((1,), (1,)), ((), ())), preferred_element_type=F32)


def _mod_row(mod_ref, i, rows_per_batch, n_batch, col):
    r = jnp.minimum(i // rows_per_batch, n_batch)
    return mod_ref[pl.ds(r, 1), col * D_MODEL:(col + 1) * D_MODEL]


def _row_tile(n_batch):
    return min(512, n_batch * CTX_LEN)


def _rms(x, g):
    return x * lax.rsqrt(jnp.mean(x * x, axis=-1, keepdims=True) + NORM_EPS) * g


def _cast_kernel(w_ref, o_ref):
    o_ref[...] = w_ref[...].astype(o_ref.dtype)


def _to_bf16(w):
    n_l, rows, cols = w.shape
    tr = 256
    return pl.pallas_call(
        _cast_kernel,
        grid=(n_l, rows // tr),
        in_specs=[pl.BlockSpec((1, tr, cols), lambda l, i: (l, i, 0))],
        out_specs=pl.BlockSpec((1, tr, cols), lambda l, i: (l, i, 0)),
        out_shape=jax.ShapeDtypeStruct(w.shape, BF16),
        compiler_params=_cparams(("arbitrary", "arbitrary")),
        name="cast_bf16",
    )(w)


def _mods_kernel(c_ref, w_ref, b_ref, o_ref):
    c = c_ref[...]
    s = (c * jax.nn.sigmoid(c)).astype(BF16)
    o_ref[0] = _dot(s, w_ref[0].astype(BF16)) + b_ref[0]


def _mods(cc, w_ada, b_ada):
    tn = 1536
    return pl.pallas_call(
        _mods_kernel,
        grid=(DEPTH, N_MOD // tn),
        in_specs=[
            pl.BlockSpec((8, D_MODEL), lambda l, j: (0, 0)),
            pl.BlockSpec((1, D_MODEL, tn), lambda l, j: (l, 0, j)),
            pl.BlockSpec((1, 1, tn), lambda l, j: (l, 0, j)),
        ],
        out_specs=pl.BlockSpec((1, 8, tn), lambda l, j: (l, 0, j)),
        out_shape=jax.ShapeDtypeStruct((DEPTH, 8, N_MOD), F32),
        compiler_params=_cparams(("arbitrary", "arbitrary")),
        name="adaln_mods",
    )(cc, w_ada, b_ada.reshape(DEPTH, 1, N_MOD))


def _inproj_kernel(xl_ref, xc_ref, mod_ref, g_ref, w_ref, wkt_ref, wgt_ref, cos_ref, sin_ref,
                   u_ref, kt_ref, fn_ref, gate_ref, xn_ref, *, rows_per_batch, n_batch):
    i = pl.program_id(0)
    j = pl.program_id(1)

    def normalise(x_ref):
        y = _rms(x_ref[...], g_ref[...])
        shift = _mod_row(mod_ref, i, rows_per_batch, n_batch, 0)
        scale = _mod_row(mod_ref, i, rows_per_batch, n_batch, 1)
        xn_ref[...] = (y * (1.0 + scale) + shift).astype(BF16)
        gate_ref[...] = _dot_nt(wgt_ref[...], xn_ref[...])

    is_lat = i < rows_per_batch * n_batch

    @pl.when(jnp.logical_and(j == 0, is_lat))
    def _():
        normalise(xl_ref)

    @pl.when(jnp.logical_and(j == 0, jnp.logical_not(is_lat)))
    def _():
        normalise(xc_ref)

    def product():
        return _dot(xn_ref[...], w_ref[...])

    blk = jnp.where(j == 0, 0, j - 1)
    is_rope = jnp.logical_or(blk == U_QDA, blk == U_KDA)
    is_plain = jnp.logical_and(j != STEP_KT, jnp.logical_and(jnp.logical_not(is_rope), blk < U_BLOCKS))

    @pl.when(is_plain)
    def _():
        u_ref[...] = product().astype(BF16)

    @pl.when(j == STEP_KT)
    def _():
        kt_ref[...] = (_dot_nt(wkt_ref[...], xn_ref[...]) * (ML_DK ** -0.5)).astype(BF16)

    @pl.when(is_rope)
    def _():
        acc = product()
        cos = cos_ref[...]
        sin = sin_ref[...]
        for t in range(acc.shape[1] // DA_DV):
            sl = slice(t * DA_DV, (t + 1) * DA_DV)
            x = acc[:, sl]
            u_ref[:, sl] = (x * cos + pltpu.roll(x, DA_DV // 2, 1) * sin).astype(BF16)

    @pl.when(blk == U_BLOCKS)
    def _():
        acc = product()
        for g in range(FN_GROUPS):
            z = acc[:, g * FN_GC:(g + 1) * FN_GC]
            fn_ref[g] = z.reshape(z.shape[0] // FFT_R, FFT_R * FN_GC).astype(BF16)


def _inproj(x_lat, x_ctx, ctx_blk, mods, g, w_main, w_kt, w_gate_t, layer, cos_t, sin_t, n_batch):
    tm = n_batch * CTX_LEN
    assert SEQ % tm == 0 and cos_t.shape[0] == SEQ + tm
    n_rows = n_batch * (SEQ + CTX_LEN)
    ni = n_rows // tm
    rows_per_batch = SEQ // tm
    lat_blocks = n_batch * rows_per_batch

    def tab_idx(i, j):
        return (jnp.where(i < lat_blocks, i % rows_per_batch, rows_per_batch), 0)

    def w_blk(j):
        return jnp.where(j == 0, 0, j - 1)

    kern = functools.partial(_inproj_kernel, rows_per_batch=rows_per_batch, n_batch=n_batch)
    return pl.pallas_call(
        kern,
        grid=(ni, N_COL_STEPS),
        in_specs=[
            pl.BlockSpec((tm, D_MODEL), lambda i, j: (jnp.minimum(i, lat_blocks - 1), 0)),
            pl.BlockSpec((tm, D_MODEL), lambda i, j: (ctx_blk, 0)),
            pl.BlockSpec((None, 8, N_MOD), lambda i, j: (layer, 0, 0)),
            pl.BlockSpec((1, D_MODEL), lambda i, j: (0, 0)),
            pl.BlockSpec((None, D_MODEL, D_MODEL), lambda i, j: (layer, 0, w_blk(j))),
            pl.BlockSpec((None, D_MODEL, D_MODEL), lambda i, j: (layer, 0, 0)),
            pl.BlockSpec((None, N_GATE, D_MODEL), lambda i, j: (layer, 0, 0)),
            pl.BlockSpec((tm, DA_DV), tab_idx),
            pl.BlockSpec((tm, DA_DV), tab_idx),
        ],
        out_specs=[
            pl.BlockSpec((tm, D_MODEL), lambda i, j: (i, jnp.minimum(w_blk(j), U_BLOCKS - 1))),
            pl.BlockSpec((D_MODEL, tm), lambda i, j: (0, i)),
            pl.BlockSpec((FN_GROUPS, tm // FFT_R, FFT_R * FN_GC), lambda i, j: (0, i, 0)),
            pl.BlockSpec((N_GATE, tm), lambda i, j: (0, i)),
        ],
        out_shape=[
            jax.ShapeDtypeStruct((n_rows, U_BLOCKS * D_MODEL), BF16),
            jax.ShapeDtypeStruct((D_MODEL, n_rows), BF16),
            jax.ShapeDtypeStruct((FN_GROUPS, n_rows // FFT_R, FFT_R * FN_GC), BF16),
            jax.ShapeDtypeStruct((N_GATE, n_rows), F32),
        ],
        scratch_shapes=[pltpu.VMEM((tm, D_MODEL), BF16)],
        compiler_params=_cparams(("arbitrary", "arbitrary")),
        name="in_proj",
    )(x_lat, x_ctx, mods, g, w_main, w_kt, w_gate_t, cos_t, sin_t)


def _split3(x):
    hi = x.astype(BF16).astype(F32)
    mid = (x - hi).astype(BF16).astype(F32)
    lo = (x - hi - mid).astype(BF16).astype(F32)
    return hi, mid, lo


def _mlstm_kernel(qf_ref, ktf_ref, vf_ref, gtf_ref, gtfn_ref, qb_ref, ktb_ref, vb_ref, gtb_ref, gtbn_ref, bias_ref,
                  hf_ref, hb_ref, *scratch):
    n_st = 2 * ML_HEADS
    cx_refs, bw_refs, pmw_refs = scratch[:n_st], scratch[n_st:2 * n_st], scratch[2 * n_st:3 * n_st]
    m_refs, c_refs, bend_refs, g_refs = (scratch[3 * n_st + 2 * k:3 * n_st + 2 * k + 2] for k in range(4))
    state_refs = cx_refs + m_refs
    s = pl.program_id(1)
    L = ML_CHUNK
    H = ML_HEADS
    W = LANES_V7X

    @pl.when(s == 0)
    def _():
        for ref in state_refs:
            ref[...] = jnp.zeros_like(ref)

    t_idx = lax.broadcasted_iota(jnp.int32, (L, L), 0)
    s_idx = lax.broadcasted_iota(jnp.int32, (L, L), 1)
    eye = t_idx == s_idx
    sub8 = lax.broadcasted_iota(jnp.int32, (8, W), 0)
    ones_w = jnp.ones((L, W), BF16)
    er = lax.broadcasted_iota(jnp.int32, (4 * L, 2 * W), 0)
    ec = lax.broadcasted_iota(jnp.int32, (4 * L, 2 * W), 1)
    expand = jnp.where((er < 3 * L) == (ec < W), 1.0, 0.0).astype(BF16)

    def running_max_rows(x, d):
        n_tiles = L // 8
        out = [None] * n_tiles
        carry = None
        for j in (range(n_tiles) if d == 0 else range(n_tiles - 1, -1, -1)):
            r = x[8 * j:8 * (j + 1)]
            k = 1
            while k < 8:
                if d == 0:
                    r = jnp.maximum(r, jnp.where(sub8 >= k, pltpu.roll(r, k, 0), -jnp.inf))
                else:
                    r = jnp.maximum(r, jnp.where(sub8 < 8 - k, pltpu.roll(r, 8 - k, 0), -jnp.inf))
                k *= 2
            if carry is not None:
                r = jnp.maximum(r, carry)
            carry = jnp.broadcast_to(r[7:8] if d == 0 else r[0:1], (8, W))
            out[j] = r
        return jnp.concatenate(out, axis=0)

    def gate_part(d, gt_ref):
        before = (t_idx <= s_idx) if d == 0 else (t_idx >= s_idx)
        gt = gt_ref[2 * H * d:2 * H * (d + 1), :] + bias_ref[2 * H * d:2 * H * (d + 1), :]
        i4, f4 = gt[0:H], gt[H:2 * H]
        lf4 = jnp.minimum(f4, 0.0) - jnp.log1p(jnp.exp(-jnp.abs(f4)))
        lf_terms = jnp.concatenate(list(_split3(lf4)) + [jnp.zeros((H, L), F32)], axis=0).astype(BF16)
        cum_rhs = jnp.concatenate([jnp.where(before, 1.0, 0.0).astype(BF16), ones_w], axis=1)
        r = _dot(lf_terms, cum_rhs)
        bx = r[0:H] + r[H:2 * H] + r[2 * H:3 * H]
        b4, bend4 = bx[:, 0:L], bx[:, L:L + W]
        c4 = i4 - b4
        c_refs[d][...] = c4
        bend_refs[d][...] = bend4
        g_refs[d][...] = bend4[:, 0:1] - b4 + i4
        col_terms = _split3(b4) + (c4.astype(BF16).astype(F32),)
        for h in range(H):
            diag = jnp.concatenate([jnp.where(eye, x[h:h + 1, :], 0.0) for x in col_terms], axis=1).astype(BF16)
            wide = _dot(diag, expand)
            bw_refs[d * H + h][...] = wide[:, 0:W]
            pmw_refs[d * H + h][...] = running_max_rows(wide[:, W:2 * W], d)

    dirs = ((0, qf_ref, ktf_ref, vf_ref, gtf_ref, gtfn_ref, hf_ref),
            (1, qb_ref, ktb_ref, vb_ref, gtb_ref, gtbn_ref, hb_ref))

    @pl.when(s == 0)
    def _():
        for d, _, _, _, gt_ref, _, _ in dirs:
            gate_part(d, gt_ref)

    for d, q_ref, kt_ref, v_ref, _, _, h_ref in dirs:
        causal = (s_idx <= t_idx) if d == 0 else (s_idx >= t_idx)
        c4, bend4, g4 = c_refs[d][...], bend_refs[d][...], g_refs[d][...]
        m_prev4 = m_refs[d][...]
        m_new4 = jnp.maximum(bend4 + m_prev4, jnp.max(g4, axis=1, keepdims=True))
        m_refs[d][...] = m_new4
        decay4 = jnp.exp(bend4 + m_prev4 - m_new4)
        w4 = jnp.exp(g4 - m_new4[:, 0:1])
        for h in range(H):
            st = d * H + h
            hs = slice(h * ML_DK, (h + 1) * ML_DK)
            b_w = bw_refs[st][...]
            m_w = jnp.maximum(pmw_refs[st][...], m_prev4[h:h + 1, :])

            q = q_ref[:, hs]
            kt = kt_ref[hs, :]
            vx = jnp.concatenate([v_ref[:, hs], ones_w], axis=1)
            m_ll = jnp.concatenate([m_w] * (L // W), axis=1)
            a = (jnp.where(causal, jnp.exp(c4[h:h + 1, :] - m_ll), 0.0) * _dot(q, kt)).astype(BF16)
            cx_prev = cx_refs[st][...]
            qc = _dot(q, cx_prev.astype(BF16))
            av = _dot(a, vx)
            sc_w = jnp.exp(m_prev4[h:h + 1, :] - m_w)
            den = sc_w * qc[:, ML_DK:] + av[:, ML_DK:]
            inv = 1.0 / jnp.maximum(jnp.abs(den), jnp.exp(-(b_w + m_w)))
            for t in range(ML_DK // W):
                ts = slice(t * W, (t + 1) * W)
                h_ref[:, h * ML_DK + t * W:h * ML_DK + (t + 1) * W] = (
                    (sc_w * qc[:, ts] + av[:, ts]) * inv).astype(h_ref.dtype)

            kw = (kt.astype(F32) * w4[h:h + 1, :]).astype(BF16)
            dec = jnp.concatenate([decay4[h:h + 1, :]] * (ML_DK // W + 1), axis=1)
            cx_refs[st][...] = dec * cx_prev + _dot(kw, vx)

    for d, _, _, _, _, gtn_ref, _ in dirs:
        gate_part(d, gtn_ref)


def _mlstm(u, kt, gates_t, gate_b, n_batch):
    n_rows = u.shape[0]
    L = ML_CHUNK
    lat_chunks = SEQ // L
    ctx_chunks = CTX_LEN // L
    n_steps = ctx_chunks + lat_chunks
    ctx_base = n_batch * lat_chunks

    def rowblk(d):
        def f(b, s):
            in_ctx = s < ctx_chunks
            if d == 0:
                c = jnp.where(in_ctx, s, s - ctx_chunks)
            else:
                c = jnp.where(in_ctx, ctx_chunks - 1 - s, lat_chunks - 1 - (s - ctx_chunks))
            return jnp.where(in_ctx, ctx_base + ctx_chunks * b, lat_chunks * b) + c
        return f

    def dir_specs(d):
        rb = rowblk(d)
        return [
            pl.BlockSpec((L, D_MODEL), lambda b, s: (rb(b, s), U_QML)),
            pl.BlockSpec((D_MODEL, L), lambda b, s: (0, rb(b, s))),
            pl.BlockSpec((L, D_MODEL), lambda b, s: (rb(b, s), U_VML)),
            pl.BlockSpec((N_GATE, L), lambda b, s: (0, rb(b, s))),
            pl.BlockSpec((N_GATE, L), lambda b, s: (0, rb(b, jnp.minimum(s + 1, n_steps - 1)))),
        ]

    def out_spec(d):
        rb = rowblk(d)
        return pl.BlockSpec((L, D_MODEL), lambda b, s: (rb(b, s), 0))

    n_st = 2 * ML_HEADS
    return pl.pallas_call(
        _mlstm_kernel,
        grid=(n_batch, n_steps),
        in_specs=dir_specs(0) + dir_specs(1) + [pl.BlockSpec((N_GATE, 1), lambda b, s: (0, 0))],
        out_specs=[out_spec(0), out_spec(1)],
        out_shape=[jax.ShapeDtypeStruct((n_rows, D_MODEL), BF16)] * 2,
        scratch_shapes=(
            [pltpu.VMEM((ML_DK, ML_DK + LANES_V7X), F32)] * n_st
            + [pltpu.VMEM((L, LANES_V7X), F32)] * (2 * n_st)
            + [pltpu.VMEM((ML_HEADS, LANES_V7X), F32)] * 2
            + [pltpu.VMEM((ML_HEADS, L), F32)] * 2
            + [pltpu.VMEM((ML_HEADS, LANES_V7X), F32)] * 2
            + [pltpu.VMEM((ML_HEADS, L), F32)] * 2),
        compiler_params=_cparams(("arbitrary", "arbitrary")),
        name="mlstm_scan",
    )(u, kt, u, gates_t, gates_t, u, kt, u, gates_t, gates_t, gate_b.reshape(N_GATE, 1))


def _attn_lambda(lam_ref, lam_init):
    lq = lam_ref[...]
    return (jnp.exp(jnp.sum(lq[0:1] * lq[1:2], axis=1, keepdims=True))
            - jnp.exp(jnp.sum(lq[2:3] * lq[3:4], axis=1, keepdims=True)) + lam_init)


def _attn_queries(q_ref):
    q = q_ref[...]
    lane = lax.broadcasted_iota(jnp.int32, (1, DA_DV), 1)
    zero = jnp.zeros_like(q)
    is_map0 = (lane % DA_DH) < (DA_DH // 2)
    q2 = jnp.concatenate([jnp.where(is_map0, q, zero), jnp.where(is_map0, zero, q)], axis=0)
    return q2 * (DA_DH ** -0.5)


def _attn_sums(s_chunks, m_chunks, vx):
    return jnp.concatenate([_dot(jnp.exp(s - m).astype(BF16), vx) for s, m in zip(s_chunks, m_chunks)], axis=0)


def _attn_finish(acc, lam, g, lam_init, tq):
    o0 = acc[0:tq, 0:DA_DV] * (1.0 / acc[0:tq, DA_DV:DA_DV + 1])
    o1 = acc[tq:, 0:DA_DV] * (1.0 / acc[tq:, DA_DV:DA_DV + 1])
    return _rms(o0 - lam * o1, g) * (1.0 - lam_init)


def _attn_kernel(q_ref, kl_ref, vl_ref, kc_ref, vc_ref, lam_ref, g_ref, o_ref, vx_ref, s_ref, m_ref,
                 *, lam_init, q_blocks):
    t = pl.program_id(0)
    tq = q_ref.shape[0]
    rows = 2 * tq // ATTN_ROW_CHUNKS

    @pl.when(jnp.logical_and(t > 0, (t - 1) % q_blocks == 0))
    def _():
        vx_ref[0:SEQ, 0:DA_DV] = vl_ref[...]
        vx_ref[SEQ:, 0:DA_DV] = vc_ref[...]
        lane_v = lax.broadcasted_iota(jnp.int32, (SEQ + CTX_LEN, DA_DV), 1)
        vx_ref[:, DA_DV:] = jnp.where(lane_v == 0, 1.0, 0.0).astype(BF16)

    def score(slot):
        q2 = _attn_queries(q_ref)
        s_lat = _dot_nt(q2, kl_ref[...])
        s_ctx = _dot_nt(q2, kc_ref[...])
        m_ref[slot] = jnp.maximum(jnp.max(s_lat, axis=1, keepdims=True), jnp.max(s_ctx, axis=1, keepdims=True))
        s_ref[slot, :, 0:SEQ] = s_lat
        s_ref[slot, :, SEQ:] = s_ctx

    def drain(slot):
        s_chunks = [s_ref[slot, c * rows:(c + 1) * rows, :] for c in range(ATTN_ROW_CHUNKS)]
        m_chunks = [m_ref[slot, c * rows:(c + 1) * rows, :] for c in range(ATTN_ROW_CHUNKS)]
        acc = _attn_sums(s_chunks, m_chunks, vx_ref[...])
        o = _attn_finish(acc, _attn_lambda(lam_ref, lam_init), g_ref[...], lam_init, tq)
        o_ref[...] = o.astype(o_ref.dtype)

    @pl.when(t == 0)
    def _():
        score(0)

    for parity in (0, 1):
        @pl.when(jnp.logical_and(t > 0, t % 2 == parity))
        def _():
            drain(1 - parity)
            score(parity)


def _attn_ctx_kernel(q_ref, kc_ref, vc_ref, lam_ref, g_ref, yin_ref, o_ref, *, lam_init):
    del yin_ref
    tq = q_ref.shape[0]
    s = _dot_nt(_attn_queries(q_ref), kc_ref[...])
    lane_v = lax.broadcasted_iota(jnp.int32, (CTX_LEN, DA_DV), 1)
    vx = jnp.concatenate([vc_ref[...], jnp.where(lane_v == 0, 1.0, 0.0).astype(BF16)], axis=1)
    acc = _attn_sums([s], [jnp.max(s, axis=1, keepdims=True)], vx)
    o = _attn_finish(acc, _attn_lambda(lam_ref, lam_init), g_ref[...], lam_init, tq)
    o_ref[...] = o.astype(o_ref.dtype)


def _attn(u, da_lam, head_g, lam_init, need_ctx, n_batch):
    n_rows = u.shape[0]
    tq = ATTN_TQ
    lat_qblocks = SEQ // tq
    ctx_base = n_batch * (SEQ // CTX_LEN)
    cpb = D_MODEL // DA_DV
    n_keys = SEQ + CTX_LEN

    n_items = n_batch * DA_HEADS * lat_qblocks

    def item(t):
        t = jnp.clip(t, 0, n_items - 1)
        return t // (DA_HEADS * lat_qblocks), (t // lat_qblocks) % DA_HEADS, t % lat_qblocks

    def scored(f):
        return lambda t: f(*item(t))

    def drained(f):
        return lambda t: f(*item(t - 1))

    kern = functools.partial(_attn_kernel, lam_init=lam_init, q_blocks=lat_qblocks)
    yd = pl.pallas_call(
        kern,
        grid=(n_items + 1,),
        in_specs=[
            pl.BlockSpec((tq, DA_DV), scored(lambda b, h, qi: (b * lat_qblocks + qi, U_QDA * cpb + h))),
            pl.BlockSpec((SEQ, DA_DV), scored(lambda b, h, qi: (b, U_KDA * cpb + h))),
            pl.BlockSpec((SEQ, DA_DV), drained(lambda b, h, qi: (b, U_VDA * cpb + h))),
            pl.BlockSpec((CTX_LEN, DA_DV), scored(lambda b, h, qi: (ctx_base + b, U_KDA * cpb + h))),
            pl.BlockSpec((CTX_LEN, DA_DV), drained(lambda b, h, qi: (ctx_base + b, U_VDA * cpb + h))),
            pl.BlockSpec((4, DA_DH), lambda t: (0, 0)),
            pl.BlockSpec((1, DA_DV), drained(lambda b, h, qi: (0, h))),
        ],
        out_specs=pl.BlockSpec((tq, DA_DV), drained(lambda b, h, qi: (b * lat_qblocks + qi, h))),
        out_shape=jax.ShapeDtypeStruct((n_rows, D_MODEL), BF16),
        scratch_shapes=[
            pltpu.VMEM((n_keys, 2 * DA_DV), BF16),
            pltpu.VMEM((2, 2 * tq, n_keys), F32),
            pltpu.VMEM((2, 2 * tq, 1), F32),
        ],
        compiler_params=_cparams(("arbitrary",)),
        name="diff_attn",
    )(u, u, u, u, u, da_lam, head_g)
    if not need_ctx:
        return yd
    kern_ctx = functools.partial(_attn_ctx_kernel, lam_init=lam_init)
    return pl.pallas_call(
        kern_ctx,
        grid=(n_batch, DA_HEADS),
        in_specs=[
            pl.BlockSpec((CTX_LEN, DA_DV), lambda b, h: (ctx_base + b, U_QDA * cpb + h)),
            pl.BlockSpec((CTX_LEN, DA_DV), lambda b, h: (ctx_base + b, U_KDA * cpb + h)),
            pl.BlockSpec((CTX_LEN, DA_DV), lambda b, h: (ctx_base + b, U_VDA * cpb + h)),
            pl.BlockSpec((4, DA_DH), lambda b, h: (0, 0)),
            pl.BlockSpec((1, DA_DV), lambda b, h: (0, h)),
            pl.BlockSpec(memory_space=pl.ANY),
        ],
        out_specs=pl.BlockSpec((CTX_LEN, DA_DV), lambda b, h: (ctx_base + b, h)),
        out_shape=jax.ShapeDtypeStruct((n_rows, D_MODEL), BF16),
        input_output_aliases={5: 0},
        compiler_params=_cparams(("arbitrary", "arbitrary")),
        name="diff_attn_ctx",
    )(u, u, u, da_lam, head_g, yd)


def _dft_tables():
    R = FFT_R

    def cs(num, period):
        ang = (num % period).astype(F32) * (2.0 * math.pi / period)
        return jnp.cos(ang), jnp.sin(ang)

    idx = jnp.arange(R, dtype=jnp.int32)
    c1, s1 = cs(idx[:, None] * idx[None, :], R)
    a1 = jnp.concatenate([c1, -s1], axis=0).astype(BF16)
    f2 = idx[:, None, None]
    f1 = idx[None, :, None]
    t1 = idx[None, None, :]
    mc, ms = cs(t1 * (R * f1 + f2), SEQ)
    b2 = jnp.concatenate([jnp.concatenate([mc, ms], axis=2),
                          jnp.concatenate([-ms, mc], axis=2)], axis=1).astype(BF16)
    ch = jnp.arange(FN_GC, dtype=jnp.int32)
    cc, sc = cs(ch[:, None] * ch[None, :], FN_GC)
    csm = jnp.concatenate([cc, sc], axis=0).astype(BF16)
    actx = jnp.concatenate([cc, -sc], axis=0).astype(BF16)
    return a1, b2, csm, actx


def _fft1_kernel(a_ref, z_ref, p_ref):
    p_ref[0, 0] = _dot(a_ref[...], z_ref[0]).astype(p_ref.dtype)


def _fft1(fn_view, a1, n_batch):
    R = FFT_R
    lanes = R * FN_GC
    lc = lanes
    return pl.pallas_call(
        _fft1_kernel,
        grid=(FN_GROUPS, n_batch, lanes // lc),
        in_specs=[
            pl.BlockSpec((2 * R, R), lambda g, b, c: (0, 0)),
            pl.BlockSpec((1, R, lc), lambda g, b, c: (g, b, c)),
        ],
        out_specs=pl.BlockSpec((1, 1, 2 * R, lc), lambda g, b, c: (g, b, 0, c)),
        out_shape=jax.ShapeDtypeStruct((FN_GROUPS, n_batch, 2 * R, lanes), BF16),
        compiler_params=_cparams(("arbitrary", "arbitrary", "arbitrary")),
        name="fourier_stage1",
    )(a1, fn_view)


def _fft2_kernel(p_ref, b_ref, cs_ref, o_ref, *, f2b):
    R = FFT_R
    cc = cs_ref[0:FN_GC, :]
    sc = cs_ref[FN_GC:2 * FN_GC, :]
    norm = 1.0 / math.sqrt(SEQ * FN_GC)
    for g in range(FN_GROUPS):
        xr, xi = [], []
        for jj in range(f2b):
            stacked = jnp.concatenate([p_ref[g, 0, 0, jj], p_ref[g, 0, 1, jj]], axis=0)
            x = _dot(b_ref[jj], stacked)
            xr.append(x[0:R])
            xi.append(x[R:2 * R])
        y = (_dot(jnp.concatenate(xr, axis=0).astype(BF16), cc)
             + _dot(jnp.concatenate(xi, axis=0).astype(BF16), sc)) * norm
        for jj in range(f2b):
            lo = jj * D_MODEL + g * FN_GC
            o_ref[:, lo:lo + FN_GC] = y[jj * R:(jj + 1) * R].astype(o_ref.dtype)


def _fft2(p6, b2, csm, n_rows, n_batch):
    R = FFT_R
    f2b = 8
    kern = functools.partial(_fft2_kernel, f2b=f2b)
    return pl.pallas_call(
        kern,
        grid=(n_batch, R // f2b),
        in_specs=[
            pl.BlockSpec((FN_GROUPS, 1, 2, f2b, R, FN_GC), lambda b, f: (0, b, 0, f, 0, 0)),
            pl.BlockSpec((f2b, 2 * R, 2 * R), lambda b, f: (f, 0, 0)),
            pl.BlockSpec((2 * FN_GC, FN_GC), lambda b, f: (0, 0)),
        ],
        out_specs=pl.BlockSpec((R, f2b * D_MODEL), lambda b, f: (b, f)),
        out_shape=jax.ShapeDtypeStruct((n_rows // R, R * D_MODEL), BF16),
        compiler_params=_cparams(("arbitrary", "arbitrary")),
        name="fourier_stage2",
    )(p6, b2, csm)


def _fft_ctx_kernel(z_ref, a_ref, cs_ref, yin_ref, o_ref, *, n_batch):
    del yin_ref
    cc = cs_ref[0:FN_GC, :]
    sc = cs_ref[FN_GC:2 * FN_GC, :]
    norm = 1.0 / math.sqrt(CTX_LEN * FN_GC)
    n_ctx = n_batch * CTX_LEN
    groups = []
    for g in range(FN_GROUPS):
        z_all = z_ref[g].astype(F32).reshape(n_ctx, FN_GC)
        ys = []
        for b in range(n_batch):
            p = _dot(a_ref[...], z_all[b * CTX_LEN:(b + 1) * CTX_LEN].astype(BF16))
            ys.append(_dot(p[0:CTX_LEN].astype(BF16), cc) + _dot(p[CTX_LEN:2 * CTX_LEN].astype(BF16), sc))
        groups.append(jnp.concatenate(ys, axis=0))
    y = jnp.concatenate(groups, axis=1) * norm
    o_ref[...] = y.reshape(n_ctx // FFT_R, FFT_R * D_MODEL).astype(o_ref.dtype)


def _fft_ctx(fn_view, actx, csm, yf, n_batch):
    n_ctx = n_batch * CTX_LEN
    ctx_blk = n_batch * SEQ // n_ctx
    kern = functools.partial(_fft_ctx_kernel, n_batch=n_batch)
    return pl.pallas_call(
        kern,
        grid=(1,),
        in_specs=[
            pl.BlockSpec((FN_GROUPS, n_ctx // FFT_R, FFT_R * FN_GC), lambda i: (0, ctx_blk, 0)),
            pl.BlockSpec((2 * CTX_LEN, CTX_LEN), lambda i: (0, 0)),
            pl.BlockSpec((2 * FN_GC, FN_GC), lambda i: (0, 0)),
            pl.BlockSpec(memory_space=pl.ANY),
        ],
        out_specs=pl.BlockSpec((n_ctx // FFT_R, FFT_R * D_MODEL), lambda i: (ctx_blk, 0)),
        out_shape=jax.ShapeDtypeStruct(yf.shape, BF16),
        input_output_aliases={3: 0},
        compiler_params=_cparams(("arbitrary",)),
        name="fourier_ctx",
    )(fn_view, actx, csm, yf)


def _fourier(fn, tables, need_ctx, n_batch):
    a1, b2, csm, actx = tables
    R = FFT_R
    n_rows = fn.shape[1] * R
    p = _fft1(fn, a1, n_batch)
    yf = _fft2(p.reshape(FN_GROUPS, n_batch, 2, R, R, FN_GC), b2, csm, n_rows, n_batch)
    if need_ctx:
        yf = _fft_ctx(fn, actx, csm, yf, n_batch)
    return yf


def _merge_kernel(hf_ref, hb_ref, o_ref, gm_ref, gd_ref, gf_ref, yd_ref, yf_ref, xl_ref, xc_ref, mod_ref, hg_ref,
                  wml_ref, wda_ref, wfn_ref, wout_ref, out_ref, *, rows_per_batch, n_batch):
    i = pl.program_id(0)
    x = jnp.where(i < rows_per_batch * n_batch, xl_ref[...], xc_ref[...])
    hsum = hf_ref[...].astype(F32) + hb_ref[...].astype(F32)
    hg = hg_ref[...]
    parts = []
    for h in range(ML_HEADS):
        hs = slice(h * ML_DK, (h + 1) * ML_DK)
        parts.append(_rms(hsum[:, hs], hg[:, hs]))
    ym = (jnp.concatenate(parts, axis=1) * jax.nn.sigmoid(o_ref[...].astype(F32))).astype(BF16)
    yf = yf_ref[...].astype(F32).reshape(x.shape).astype(BF16)
    y = (jax.nn.sigmoid(gm_ref[...].astype(F32)) * _dot(ym, wml_ref[...])
         + jax.nn.sigmoid(gd_ref[...].astype(F32)) * _dot(yd_ref[...], wda_ref[...])
         + jax.nn.sigmoid(gf_ref[...].astype(F32)) * _dot(yf, wfn_ref[...]))
    gate = _mod_row(mod_ref, i, rows_per_batch, n_batch, 2)
    out_ref[...] = x + gate * _dot(y.astype(BF16), wout_ref[...])


def _merge(hf, hb, u, yd, yf, x_lat, x_ctx, ctx_blk0, mods, head_g, wml, wda, wfn, wout, layer, need_ctx, n_batch):
    n_rows = n_batch * (SEQ + CTX_LEN)
    tm = _row_tile(n_batch)
    rows_per_batch = SEQ // tm
    lat_blocks = n_batch * rows_per_batch
    ni = (n_rows if need_ctx else n_batch * SEQ) // tm
    kern = functools.partial(_merge_kernel, rows_per_batch=rows_per_batch, n_batch=n_batch)
    row = lambda i: (i, 0)
    full = lambda i: (0, 0)
    wspec = pl.BlockSpec((None, D_MODEL, D_MODEL), lambda i: (layer, 0, 0))
    return pl.pallas_call(
        kern,
        grid=(ni,),
        in_specs=[
            pl.BlockSpec((tm, D_MODEL), row),
            pl.BlockSpec((tm, D_MODEL), row),
            pl.BlockSpec((tm, D_MODEL), lambda i: (i, U_OML)),
            pl.BlockSpec((tm, D_MODEL), lambda i: (i, U_GPRE)),
            pl.BlockSpec((tm, D_MODEL), lambda i: (i, U_GPRE + 1)),
            pl.BlockSpec((tm, D_MODEL), lambda i: (i, U_GPRE + 2)),
            pl.BlockSpec((tm, D_MODEL), row),
            pl.BlockSpec((tm // FFT_R, FFT_R * D_MODEL), row),
            pl.BlockSpec((tm, D_MODEL), lambda i: (jnp.minimum(i, lat_blocks - 1), 0)),
            pl.BlockSpec((tm, D_MODEL), lambda i: (ctx_blk0 + jnp.maximum(i - lat_blocks, 0), 0)),
            pl.BlockSpec((None, 8, N_MOD), lambda i: (layer, 0, 0)),
            pl.BlockSpec((1, D_MODEL), full),
            wspec, wspec, wspec, wspec,
        ],
        out_specs=pl.BlockSpec((tm, D_MODEL), row),
        out_shape=jax.ShapeDtypeStruct((ni * tm, D_MODEL), F32),
        compiler_params=_cparams(("arbitrary",)),
        name="merge_out_proj",
    )(hf, hb, u, u, u, u, yd, yf, x_lat, x_ctx, mods, head_g, wml, wda, wfn, wout)


FFN_CHUNKS = ((0, 1024), (1024, 1024), (2048, 768))


def _ffn_kernel(x_ref, mod_ref, g_ref, win_ref, wout_ref, fg_ref, out_ref, *, rows_per_batch, n_batch, final):
    i = pl.program_id(0)
    x = x_ref[...]
    shift = _mod_row(mod_ref, i, rows_per_batch, n_batch, 3)
    scale = _mod_row(mod_ref, i, rows_per_batch, n_batch, 4)
    gate = _mod_row(mod_ref, i, rows_per_batch, n_batch, 5)
    h = (_rms(x, g_ref[...]) * (1.0 + scale) + shift).astype(BF16)
    acc = None
    for lo, width in FFN_CHUNKS:
        a = _dot(h, win_ref[:, lo:lo + width])
        b = _dot(h, win_ref[:, D_FF + lo:D_FF + lo + width])
        act = (a * jax.nn.sigmoid(a) * b).astype(BF16)
        part = _dot(act, wout_ref[lo:lo + width, :])
        acc = part if acc is None else acc + part
    xn = x + gate * acc
    out_ref[...] = _rms(xn, fg_ref[...]) if final else xn


def _ffn(x, mods, g, w_in, w_out, layer, final_g, final, n_rows_out, n_batch):
    tm = _row_tile(n_batch)
    rows_per_batch = SEQ // tm
    kern = functools.partial(_ffn_kernel, rows_per_batch=rows_per_batch, n_batch=n_batch, final=final)
    row = lambda i: (i, 0)
    full = lambda i: (0, 0)
    return pl.pallas_call(
        kern,
        grid=(n_rows_out // tm,),
        in_specs=[
            pl.BlockSpec((tm, D_MODEL), row),
            pl.BlockSpec((None, 8, N_MOD), lambda i: (layer, 0, 0)),
            pl.BlockSpec((1, D_MODEL), full),
            pl.BlockSpec((None, D_MODEL, 2 * D_FF), lambda i: (layer, 0, 0)),
            pl.BlockSpec((None, D_FF, D_MODEL), lambda i: (layer, 0, 0)),
            pl.BlockSpec((1, D_MODEL), full),
        ],
        out_specs=pl.BlockSpec((tm, D_MODEL), row),
        out_shape=jax.ShapeDtypeStruct((n_rows_out, D_MODEL), F32),
        compiler_params=_cparams(("arbitrary",)),
        name="swiglu_ffn",
    )(x, mods, g, w_in, w_out, final_g)


def _da_col_perm(w):
    half = DA_DH // 2
    lead = w.shape[:-1]
    return jnp.swapaxes(w.reshape(lead + (DA_HEADS, 2, 2, half)), -3, -2).reshape(lead + (DA_HEADS * DA_DV,))


def _rope_tables(pad):
    n_freq = DA_DH // 4
    rows = SEQ // GRID_W
    inv = ROPE_BASE ** (-jnp.arange(n_freq, dtype=F32) / n_freq)
    r = jnp.repeat(jnp.arange(rows, dtype=F32), GRID_W)
    col = jnp.tile(jnp.arange(GRID_W, dtype=F32), rows)
    ang = jnp.concatenate([r[:, None] * inv, col[:, None] * inv], axis=-1)
    cos, sin = jnp.cos(ang), jnp.sin(ang)
    cos_t = jnp.concatenate([cos, cos, cos, cos], axis=-1)
    sin_t = jnp.concatenate([-sin, -sin, sin, sin], axis=-1)
    cos_t = jnp.concatenate([cos_t, jnp.ones((pad, DA_DV), F32)], axis=0)
    sin_t = jnp.concatenate([sin_t, jnp.zeros((pad, DA_DV), F32)], axis=0)
    return cos_t, sin_t


def kernel(x, c, ctx, c_ctx, w_ada, b_ada, norm_g, w_in, ml_gate_b, ml_head_g, da_lam, da_head_g,
           w_br_ml, w_br_da, w_br_fn, w_out, w_ffn_in, w_ffn_out, final_g):
    n_batch = x.shape[0]
    n_lat = n_batch * SEQ
    x_lat = x.reshape(n_lat, D_MODEL)
    x_ctx = ctx.reshape(n_batch * CTX_LEN, D_MODEL)
    cc = jnp.concatenate([c, c_ctx[None, :], jnp.zeros((8 - n_batch - 1, D_MODEL), F32)], axis=0)
    mods = _mods(cc, w_ada, b_ada)
    cos_t, sin_t = _rope_tables(n_batch * CTX_LEN)
    tables = _dft_tables()
    final_g2 = final_g.reshape(1, D_MODEL)
    wb_ml, wb_da, wb_fn, wb_out = _to_bf16(w_br_ml), _to_bf16(w_br_da), _to_bf16(w_br_fn), _to_bf16(w_out)
    wb_ffn_in, wb_ffn_out = _to_bf16(w_ffn_in), _to_bf16(w_ffn_out)

    gate_lo = 4 * D_MODEL
    da_lo = gate_lo + N_GATE
    fn_lo = da_lo + 3 * D_MODEL
    w_main = jnp.concatenate([w_in[..., :D_MODEL], w_in[..., 2 * D_MODEL:gate_lo],
                              _da_col_perm(w_in[..., da_lo:da_lo + D_MODEL]),
                              _da_col_perm(w_in[..., da_lo + D_MODEL:da_lo + 2 * D_MODEL]),
                              w_in[..., da_lo + 2 * D_MODEL:fn_lo],
                              w_in[..., fn_lo + D_MODEL:], w_in[..., fn_lo:fn_lo + D_MODEL]], axis=-1).astype(BF16)
    w_kt = jnp.swapaxes(w_in[..., D_MODEL:2 * D_MODEL], 1, 2).astype(BF16)
    w_gate_t = jnp.swapaxes(w_in[..., gate_lo:da_lo], 1, 2).astype(BF16)

    tm_in = n_batch * CTX_LEN
    tm_tok = _row_tile(n_batch)
    xs = None
    for l in range(DEPTH):
        need_ctx = l < DEPTH - 1
        lam_init = 0.8 - 0.6 * math.exp(-0.3 * l)
        xl, xc = (x_lat, x_ctx) if xs is None else (xs, xs)
        u, kt, fn, gates_t = _inproj(xl, xc, 0 if xs is None else n_lat // tm_in, mods,
                                     norm_g[l, 0].reshape(1, D_MODEL), w_main, w_kt, w_gate_t, l,
                                     cos_t, sin_t, n_batch)
        hf, hb = _mlstm(u, kt, gates_t, ml_gate_b[l], n_batch)
        yd = _attn(u, da_lam[l], da_head_g[l].reshape(1, D_MODEL), lam_init, need_ctx, n_batch)
        yf = _fourier(fn, tables, need_ctx, n_batch)
        xs = _merge(hf, hb, u, yd, yf, xl, xc, 0 if xs is None else n_lat // tm_tok, mods,
                    ml_head_g[l].reshape(1, D_MODEL), wb_ml, wb_da, wb_fn, wb_out, l, need_ctx, n_batch)
        final = l == DEPTH - 1
        n_out = n_lat if final else xs.shape[0]
        xs = _ffn(xs, mods, norm_g[l, 1].reshape(1, D_MODEL), wb_ffn_in, wb_ffn_out, l,
                  final_g2, final, n_out, n_batch)
    return xs.reshape(n_batch, SEQ, D_MODEL)
```

```python
import functools
import math

import jax
import jax.numpy as jnp
from jax import lax
from jax.experimental import pallas as pl
from jax.experimental.pallas import tpu as pltpu

D_MODEL = 1024
SEQ = 4096
DEPTH = 4
CTX_LEN = 256
GRID_W = 64
NORM_EPS = 1e-6

ML_HEADS = 4
ML_DK = 256
ML_CHUNK = 256

DA_HEADS = 8
DA_DH = 64
DA_DV = 2 * DA_DH
ROPE_BASE = 10000.0
ATTN_TQ = 512
ATTN_ROW_CHUNKS = 2

FN_GROUPS = 4
FN_GC = 256
FFT_R = 64

D_FF = 2816
N_GATE = 4 * ML_HEADS
N_MOD = 6 * D_MODEL

U_QML, U_VML, U_OML, U_QDA, U_KDA, U_VDA, U_GPRE = 0, 1, 2, 3, 4, 5, 6
U_BLOCKS = 9
W_BLOCKS = U_BLOCKS + 1
STEP_KT = 1
N_COL_STEPS = W_BLOCKS // 2

VMEM_LIMIT_V7X = 56 * 1024 * 1024
LANES_V7X = 128

BF16 = jnp.bfloat16
F32 = jnp.float32


def _cparams(sem):
    return pltpu.CompilerParams(dimension_semantics=sem, vmem_limit_bytes=VMEM_LIMIT_V7X)


def _dot(a, b):
    return jnp.dot(a, b, preferred_element_type=F32)


def _dot_nt(a, b):
    return lax.dot_general(a, b, (((1,), (1,)), ((), ())), preferred_element_type=F32)


def _mod_row(mod_ref, i, rows_per_batch, n_batch, col):
    r = jnp.minimum(i // rows_per_batch, n_batch)
    return mod_ref[pl.ds(r, 1), col * D_MODEL:(col + 1) * D_MODEL]


def _row_tile(n_batch):
    return min(512, n_batch * CTX_LEN)


def _rms(x, g):
    return x * lax.rsqrt(jnp.mean(x * x, axis=-1, keepdims=True) + NORM_EPS) * g


def _cast_kernel(w_ref, o_ref):
    o_ref[...] = w_ref[...].astype(o_ref.dtype)


def _to_bf16(w):
    n_l, rows, cols = w.shape
    tr = 256
    return pl.pallas_call(
        _cast_kernel,
        grid=(n_l, rows // tr),
        in_specs=[pl.BlockSpec((1, tr, cols), lambda l, i: (l, i, 0))],
        out_specs=pl.BlockSpec((1, tr, cols), lambda l, i: (l, i, 0)),
        out_shape=jax.ShapeDtypeStruct(w.shape, BF16),
        compiler_params=_cparams(("arbitrary", "arbitrary")),
        name="cast_bf16",
    )(w)


def _mods_kernel(c_ref, w_ref, b_ref, o_ref):
    c = c_ref[...]
    s = (c * jax.nn.sigmoid(c)).astype(BF16)
    o_ref[0] = _dot(s, w_ref[0].astype(BF16)) + b_ref[0]


def _mods(cc, w_ada, b_ada):
    tn = 1536
    return pl.pallas_call(
        _mods_kernel,
        grid=(DEPTH, N_MOD // tn),
        in_specs=[
            pl.BlockSpec((8, D_MODEL), lambda l, j: (0, 0)),
            pl.BlockSpec((1, D_MODEL, tn), lambda l, j: (l, 0, j)),
            pl.BlockSpec((1, 1, tn), lambda l, j: (l, 0, j)),
        ],
        out_specs=pl.BlockSpec((1, 8, tn), lambda l, j: (l, 0, j)),
        out_shape=jax.ShapeDtypeStruct((DEPTH, 8, N_MOD), F32),
        compiler_params=_cparams(("arbitrary", "arbitrary")),
        name="adaln_mods",
    )(cc, w_ada, b_ada.reshape(DEPTH, 1, N_MOD))


def _inproj_kernel(xl_ref, xc_ref, mod_ref, g_ref, w_ref, wkt_ref, wgt_ref, cos_ref, sin_ref,
                   u_ref, kt_ref, fn_ref, gate_ref, xn_ref, *, rows_per_batch, n_batch):
    i = pl.program_id(0)
    j = pl.program_id(1)

    def normalise(x_ref):
        y = _rms(x_ref[...], g_ref[...])
        shift = _mod_row(mod_ref, i, rows_per_batch, n_batch, 0)
        scale = _mod_row(mod_ref, i, rows_per_batch, n_batch, 1)
        xn_ref[...] = (y * (1.0 + scale) + shift).astype(BF16)
        gate_ref[...] = _dot_nt(wgt_ref[...], xn_ref[...])

    is_lat = i < rows_per_batch * n_batch

    @pl.when(jnp.logical_and(j == 0, is_lat))
    def _():
        normalise(xl_ref)

    @pl.when(jnp.logical_and(j == 0, jnp.logical_not(is_lat)))
    def _():
        normalise(xc_ref)

    def product(h):
        return _dot(xn_ref[...], w_ref[:, h * D_MODEL:(h + 1) * D_MODEL])

    def plain(h):
        u_ref[:, h * D_MODEL:(h + 1) * D_MODEL] = product(h).astype(BF16)

    def rope(h):
        acc = product(h)
        cos = cos_ref[...]
        sin = sin_ref[...]
        for t in range(acc.shape[1] // DA_DV):
            x = acc[:, t * DA_DV:(t + 1) * DA_DV]
            lo = h * D_MODEL + t * DA_DV
            u_ref[:, lo:lo + DA_DV] = (x * cos + pltpu.roll(x, DA_DV // 2, 1) * sin).astype(BF16)

    def fourier_in(h):
        acc = product(h)
        for g in range(FN_GROUPS):
            z = acc[:, g * FN_GC:(g + 1) * FN_GC]
            fn_ref[g] = z.reshape(z.shape[0] // FFT_R, FFT_R * FN_GC).astype(BF16)

    def keys_t():
        kt_ref[...] = (_dot_nt(wkt_ref[...], xn_ref[...]) * (ML_DK ** -0.5)).astype(BF16)

    epilogues = {U_QML: plain, U_VML: plain, U_OML: plain, U_QDA: rope, U_KDA: rope, U_VDA: plain,
                 U_GPRE: plain, U_GPRE + 1: plain, U_GPRE + 2: plain, U_BLOCKS: fourier_in}
    for step in range(N_COL_STEPS):
        @pl.when(j == step)
        def _(step=step):
            for h in range(2):
                epilogues[2 * step + h](h)
            if step == STEP_KT:
                keys_t()


def _inproj(x_lat, x_ctx, ctx_blk, mods, g, w_main, w_kt, w_gate_t, layer, cos_t, sin_t, n_batch):
    tm = n_batch * CTX_LEN
    assert SEQ % tm == 0 and cos_t.shape[0] == SEQ + tm
    n_rows = n_batch * (SEQ + CTX_LEN)
    ni = n_rows // tm
    rows_per_batch = SEQ // tm
    lat_blocks = n_batch * rows_per_batch

    def tab_idx(i, j):
        return (jnp.where(i < lat_blocks, i % rows_per_batch, rows_per_batch), 0)

    kern = functools.partial(_inproj_kernel, rows_per_batch=rows_per_batch, n_batch=n_batch)
    return pl.pallas_call(
        kern,
        grid=(ni, N_COL_STEPS),
        in_specs=[
            pl.BlockSpec((tm, D_MODEL), lambda i, j: (jnp.minimum(i, lat_blocks - 1), 0)),
            pl.BlockSpec((tm, D_MODEL), lambda i, j: (ctx_blk, 0)),
            pl.BlockSpec((None, 8, N_MOD), lambda i, j: (layer, 0, 0)),
            pl.BlockSpec((1, D_MODEL), lambda i, j: (0, 0)),
            pl.BlockSpec((None, D_MODEL, 2 * D_MODEL), lambda i, j: (layer, 0, j)),
            pl.BlockSpec((None, D_MODEL, D_MODEL), lambda i, j: (layer, 0, 0)),
            pl.BlockSpec((None, N_GATE, D_MODEL), lambda i, j: (layer, 0, 0)),
            pl.BlockSpec((tm, DA_DV), tab_idx),
            pl.BlockSpec((tm, DA_DV), tab_idx),
        ],
        out_specs=[
            pl.BlockSpec((tm, 2 * D_MODEL), lambda i, j: (i, j)),
            pl.BlockSpec((D_MODEL, tm), lambda i, j: (0, i)),
            pl.BlockSpec((FN_GROUPS, tm // FFT_R, FFT_R * FN_GC), lambda i, j: (0, i, 0)),
            pl.BlockSpec((N_GATE, tm), lambda i, j: (0, i)),
        ],
        out_shape=[
            jax.ShapeDtypeStruct((n_rows, U_BLOCKS * D_MODEL), BF16),
            jax.ShapeDtypeStruct((D_MODEL, n_rows), BF16),
            jax.ShapeDtypeStruct((FN_GROUPS, n_rows // FFT_R, FFT_R * FN_GC), BF16),
            jax.ShapeDtypeStruct((N_GATE, n_rows), F32),
        ],
        scratch_shapes=[pltpu.VMEM((tm, D_MODEL), BF16)],
        compiler_params=_cparams(("arbitrary", "arbitrary")),
        name="in_proj",
    )(x_lat, x_ctx, mods, g, w_main, w_kt, w_gate_t, cos_t, sin_t)


def _split3(x):
    hi = x.astype(BF16).astype(F32)
    mid = (x - hi).astype(BF16).astype(F32)
    lo = (x - hi - mid).astype(BF16).astype(F32)
    return hi, mid, lo


def _mlstm_kernel(qf_ref, ktf_ref, vf_ref, gtf_ref, gtfn_ref, qb_ref, ktb_ref, vb_ref, gtb_ref, gtbn_ref, bias_ref,
                  hf_ref, hb_ref, *scratch):
    n_st = 2 * ML_HEADS
    cx_refs, bw_refs, pmw_refs = scratch[:n_st], scratch[n_st:2 * n_st], scratch[2 * n_st:3 * n_st]
    m_refs, c_refs, bend_refs, g_refs = (scratch[3 * n_st + 2 * k:3 * n_st + 2 * k + 2] for k in range(4))
    state_refs = cx_refs + m_refs
    s = pl.program_id(1)
    L = ML_CHUNK
    H = ML_HEADS
    W = LANES_V7X

    @pl.when(s == 0)
    def _():
        for ref in state_refs:
            ref[...] = jnp.zeros_like(ref)

    t_idx = lax.broadcasted_iota(jnp.int32, (L, L), 0)
    s_idx = lax.broadcasted_iota(jnp.int32, (L, L), 1)
    eye = t_idx == s_idx
    sub8 = lax.broadcasted_iota(jnp.int32, (8, W), 0)
    ones_w = jnp.ones((L, W), BF16)
    er = lax.broadcasted_iota(jnp.int32, (4 * L, 2 * W), 0)
    ec = lax.broadcasted_iota(jnp.int32, (4 * L, 2 * W), 1)
    expand = jnp.where((er < 3 * L) == (ec < W), 1.0, 0.0).astype(BF16)

    def running_max_rows(x, d):
        n_tiles = L // 8
        out = [None] * n_tiles
        carry = None
        for j in (range(n_tiles) if d == 0 else range(n_tiles - 1, -1, -1)):
            r = x[8 * j:8 * (j + 1)]
            k = 1
            while k < 8:
                if d == 0:
                    r = jnp.maximum(r, jnp.where(sub8 >= k, pltpu.roll(r, k, 0), -jnp.inf))
                else:
                    r = jnp.maximum(r, jnp.where(sub8 < 8 - k, pltpu.roll(r, 8 - k, 0), -jnp.inf))
                k *= 2
            if carry is not None:
                r = jnp.maximum(r, carry)
            carry = jnp.broadcast_to(r[7:8] if d == 0 else r[0:1], (8, W))
            out[j] = r
        return jnp.concatenate(out, axis=0)

    def gate_part(d, gt_ref):
        before = (t_idx <= s_idx) if d == 0 else (t_idx >= s_idx)
        gt = gt_ref[2 * H * d:2 * H * (d + 1), :] + bias_ref[2 * H * d:2 * H * (d + 1), :]
        i4, f4 = gt[0:H], gt[H:2 * H]
        lf4 = jnp.minimum(f4, 0.0) - jnp.log1p(jnp.exp(-jnp.abs(f4)))
        lf_terms = jnp.concatenate(list(_split3(lf4)) + [jnp.zeros((H, L), F32)], axis=0).astype(BF16)
        cum_rhs = jnp.concatenate([jnp.where(before, 1.0, 0.0).astype(BF16), ones_w], axis=1)
        r = _dot(lf_terms, cum_rhs)
        bx = r[0:H] + r[H:2 * H] + r[2 * H:3 * H]
        b4, bend4 = bx[:, 0:L], bx[:, L:L + W]
        c4 = i4 - b4
        c_refs[d][...] = c4
        bend_refs[d][...] = bend4
        g_refs[d][...] = bend4[:, 0:1] - b4 + i4
        col_terms = _split3(b4) + (c4.astype(BF16).astype(F32),)
        for h in range(H):
            diag = jnp.concatenate([jnp.where(eye, x[h:h + 1, :], 0.0) for x in col_terms], axis=1).astype(BF16)
            wide = _dot(diag, expand)
            bw_refs[d * H + h][...] = wide[:, 0:W]
            pmw_refs[d * H + h][...] = running_max_rows(wide[:, W:2 * W], d)

    dirs = ((0, qf_ref, ktf_ref, vf_ref, gtf_ref, gtfn_ref, hf_ref),
            (1, qb_ref, ktb_ref, vb_ref, gtb_ref, gtbn_ref, hb_ref))

    @pl.when(s == 0)
    def _():
        for d, _, _, _, gt_ref, _, _ in dirs:
            gate_part(d, gt_ref)

    for d, q_ref, kt_ref, v_ref, _, _, h_ref in dirs:
        causal = (s_idx <= t_idx) if d == 0 else (s_idx >= t_idx)
        c4, bend4, g4 = c_refs[d][...], bend_refs[d][...], g_refs[d][...]
        m_prev4 = m_refs[d][...]
        m_new4 = jnp.maximum(bend4 + m_prev4, jnp.max(g4, axis=1, keepdims=True))
        m_refs[d][...] = m_new4
        decay4 = jnp.exp(bend4 + m_prev4 - m_new4)
        w4 = jnp.exp(g4 - m_new4[:, 0:1])
        for h in range(H):
            st = d * H + h
            hs = slice(h * ML_DK, (h + 1) * ML_DK)
            b_w = bw_refs[st][...]
            m_w = jnp.maximum(pmw_refs[st][...], m_prev4[h:h + 1, :])

            q = q_ref[:, hs]
            kt = kt_ref[hs, :]
            vx = jnp.concatenate([v_ref[:, hs], ones_w], axis=1)
            m_ll = jnp.concatenate([m_w] * (L // W), axis=1)
            a = (jnp.where(causal, jnp.exp(c4[h:h + 1, :] - m_ll), 0.0) * _dot(q, kt)).astype(BF16)
            cx_prev = cx_refs[st][...]
            qc = _dot(q, cx_prev.astype(BF16))
            av = _dot(a, vx)
            sc_w = jnp.exp(m_prev4[h:h + 1, :] - m_w)
            den = sc_w * qc[:, ML_DK:] + av[:, ML_DK:]
            inv = 1.0 / jnp.maximum(jnp.abs(den), jnp.exp(-(b_w + m_w)))
            for t in range(ML_DK // W):
                ts = slice(t * W, (t + 1) * W)
                h_ref[:, h * ML_DK + t * W:h * ML_DK + (t + 1) * W] = (
                    (sc_w * qc[:, ts] + av[:, ts]) * inv).astype(h_ref.dtype)

            kw = (kt.astype(F32) * w4[h:h + 1, :]).astype(BF16)
            dec = jnp.concatenate([decay4[h:h + 1, :]] * (ML_DK // W + 1), axis=1)
            cx_refs[st][...] = dec * cx_prev + _dot(kw, vx)

    for d, _, _, _, _, gtn_ref, _ in dirs:
        gate_part(d, gtn_ref)


def _mlstm(u, kt, gates_t, gate_b, n_batch):
    n_rows = u.shape[0]
    L = ML_CHUNK
    lat_chunks = SEQ // L
    ctx_chunks = CTX_LEN // L
    n_steps = ctx_chunks + lat_chunks
    ctx_base = n_batch * lat_chunks

    def rowblk(d):
        def f(b, s):
            in_ctx = s < ctx_chunks
            if d == 0:
                c = jnp.where(in_ctx, s, s - ctx_chunks)
            else:
                c = jnp.where(in_ctx, ctx_chunks - 1 - s, lat_chunks - 1 - (s - ctx_chunks))
            return jnp.where(in_ctx, ctx_base + ctx_chunks * b, lat_chunks * b) + c
        return f

    def dir_specs(d):
        rb = rowblk(d)
        return [
            pl.BlockSpec((L, D_MODEL), lambda b, s: (rb(b, s), U_QML)),
            pl.BlockSpec((D_MODEL, L), lambda b, s: (0, rb(b, s))),
            pl.BlockSpec((L, D_MODEL), lambda b, s: (rb(b, s), U_VML)),
            pl.BlockSpec((N_GATE, L), lambda b, s: (0, rb(b, s))),
            pl.BlockSpec((N_GATE, L), lambda b, s: (0, rb(b, jnp.minimum(s + 1, n_steps - 1)))),
        ]

    def out_spec(d):
        rb = rowblk(d)
        return pl.BlockSpec((L, D_MODEL), lambda b, s: (rb(b, s), 0))

    n_st = 2 * ML_HEADS
    return pl.pallas_call(
        _mlstm_kernel,
        grid=(n_batch, n_steps),
        in_specs=dir_specs(0) + dir_specs(1) + [pl.BlockSpec((N_GATE, 1), lambda b, s: (0, 0))],
        out_specs=[out_spec(0), out_spec(1)],
        out_shape=[jax.ShapeDtypeStruct((n_rows, D_MODEL), BF16)] * 2,
        scratch_shapes=(
            [pltpu.VMEM((ML_DK, ML_DK + LANES_V7X), F32)] * n_st
            + [pltpu.VMEM((L, LANES_V7X), F32)] * (2 * n_st)
            + [pltpu.VMEM((ML_HEADS, LANES_V7X), F32)] * 2
            + [pltpu.VMEM((ML_HEADS, L), F32)] * 2
            + [pltpu.VMEM((ML_HEADS, LANES_V7X), F32)] * 2
            + [pltpu.VMEM((ML_HEADS, L), F32)] * 2),
        compiler_params=_cparams(("arbitrary", "arbitrary")),
        name="mlstm_scan",
    )(u, kt, u, gates_t, gates_t, u, kt, u, gates_t, gates_t, gate_b.reshape(N_GATE, 1))


def _attn_lambda(lam_ref, lam_init):
    lq = lam_ref[...]
    return (jnp.exp(jnp.sum(lq[0:1] * lq[1:2], axis=1, keepdims=True))
            - jnp.exp(jnp.sum(lq[2:3] * lq[3:4], axis=1, keepdims=True)) + lam_init)


def _attn_queries(q_ref):
    q = q_ref[...]
    lane = lax.broadcasted_iota(jnp.int32, (1, DA_DV), 1)
    zero = jnp.zeros_like(q)
    is_map0 = (lane % DA_DH) < (DA_DH // 2)
    q2 = jnp.concatenate([jnp.where(is_map0, q, zero), jnp.where(is_map0, zero, q)], axis=0)
    return q2 * (DA_DH ** -0.5)


def _attn_sums(s_chunks, m_chunks, vx):
    return jnp.concatenate([_dot(jnp.exp(s - m).astype(BF16), vx) for s, m in zip(s_chunks, m_chunks)], axis=0)


def _attn_finish(acc, lam, g, lam_init, tq):
    o0 = acc[0:tq, 0:DA_DV] * (1.0 / acc[0:tq, DA_DV:DA_DV + 1])
    o1 = acc[tq:, 0:DA_DV] * (1.0 / acc[tq:, DA_DV:DA_DV + 1])
    return _rms(o0 - lam * o1, g) * (1.0 - lam_init)


def _attn_kernel(q_ref, kl_ref, vl_ref, kc_ref, vc_ref, lam_ref, g_ref, o_ref, vx_ref, s_ref, m_ref,
                 *, lam_init, q_blocks):
    t = pl.program_id(0)
    tq = q_ref.shape[0]
    rows = 2 * tq // ATTN_ROW_CHUNKS

    @pl.when(jnp.logical_and(t > 0, (t - 1) % q_blocks == 0))
    def _():
        vx_ref[0:SEQ, 0:DA_DV] = vl_ref[...]
        vx_ref[SEQ:, 0:DA_DV] = vc_ref[...]
        lane_v = lax.broadcasted_iota(jnp.int32, (SEQ + CTX_LEN, DA_DV), 1)
        vx_ref[:, DA_DV:] = jnp.where(lane_v == 0, 1.0, 0.0).astype(BF16)

    def score(slot):
        q2 = _attn_queries(q_ref)
        s_lat = _dot_nt(q2, kl_ref[...])
        s_ctx = _dot_nt(q2, kc_ref[...])
        m_ref[slot] = jnp.maximum(jnp.max(s_lat, axis=1, keepdims=True), jnp.max(s_ctx, axis=1, keepdims=True))
        s_ref[slot, :, 0:SEQ] = s_lat
        s_ref[slot, :, SEQ:] = s_ctx

    def drain(slot):
        s_chunks = [s_ref[slot, c * rows:(c + 1) * rows, :] for c in range(ATTN_ROW_CHUNKS)]
        m_chunks = [m_ref[slot, c * rows:(c + 1) * rows, :] for c in range(ATTN_ROW_CHUNKS)]
        acc = _attn_sums(s_chunks, m_chunks, vx_ref[...])
        o = _attn_finish(acc, _attn_lambda(lam_ref, lam_init), g_ref[...], lam_init, tq)
        o_ref[...] = o.astype(o_ref.dtype)

    @pl.when(t == 0)
    def _():
        score(0)

    for parity in (0, 1):
        @pl.when(jnp.logical_and(t > 0, t % 2 == parity))
        def _():
            drain(1 - parity)
            score(parity)


def _attn_ctx_kernel(q_ref, kc_ref, vc_ref, lam_ref, g_ref, yin_ref, o_ref, *, lam_init):
    del yin_ref
    tq = q_ref.shape[0]
    s = _dot_nt(_attn_queries(q_ref), kc_ref[...])
    lane_v = lax.broadcasted_iota(jnp.int32, (CTX_LEN, DA_DV), 1)
    vx = jnp.concatenate([vc_ref[...], jnp.where(lane_v == 0, 1.0, 0.0).astype(BF16)], axis=1)
    acc = _attn_sums([s], [jnp.max(s, axis=1, keepdims=True)], vx)
    o = _attn_finish(acc, _attn_lambda(lam_ref, lam_init), g_ref[...], lam_init, tq)
    o_ref[...] = o.astype(o_ref.dtype)


def _attn(u, da_lam, head_g, lam_init, need_ctx, n_batch):
    n_rows = u.shape[0]
    tq = ATTN_TQ
    lat_qblocks = SEQ // tq
    ctx_base = n_batch * (SEQ // CTX_LEN)
    cpb = D_MODEL // DA_DV
    n_keys = SEQ + CTX_LEN

    n_items = n_batch * DA_HEADS * lat_qblocks

    def item(t):
        t = jnp.clip(t, 0, n_items - 1)
        return t // (DA_HEADS * lat_qblocks), (t // lat_qblocks) % DA_HEADS, t % lat_qblocks

    def scored(f):
        return lambda t: f(*item(t))

    def drained(f):
        return lambda t: f(*item(t - 1))

    kern = functools.partial(_attn_kernel, lam_init=lam_init, q_blocks=lat_qblocks)
    yd = pl.pallas_call(
        kern,
        grid=(n_items + 1,),
        in_specs=[
            pl.BlockSpec((tq, DA_DV), scored(lambda b, h, qi: (b * lat_qblocks + qi, U_QDA * cpb + h))),
            pl.BlockSpec((SEQ, DA_DV), scored(lambda b, h, qi: (b, U_KDA * cpb + h))),
            pl.BlockSpec((SEQ, DA_DV), drained(lambda b, h, qi: (b, U_VDA * cpb + h))),
            pl.BlockSpec((CTX_LEN, DA_DV), scored(lambda b, h, qi: (ctx_base + b, U_KDA * cpb + h))),
            pl.BlockSpec((CTX_LEN, DA_DV), drained(lambda b, h, qi: (ctx_base + b, U_VDA * cpb + h))),
            pl.BlockSpec((4, DA_DH), lambda t: (0, 0)),
            pl.BlockSpec((1, DA_DV), drained(lambda b, h, qi: (0, h))),
        ],
        out_specs=pl.BlockSpec((tq, DA_DV), drained(lambda b, h, qi: (b * lat_qblocks + qi, h))),
        out_shape=jax.ShapeDtypeStruct((n_rows, D_MODEL), BF16),
        scratch_shapes=[
            pltpu.VMEM((n_keys, 2 * DA_DV), BF16),
            pltpu.VMEM((2, 2 * tq, n_keys), F32),
            pltpu.VMEM((2, 2 * tq, 1), F32),
        ],
        compiler_params=_cparams(("arbitrary",)),
        name="diff_attn",
    )(u, u, u, u, u, da_lam, head_g)
    if not need_ctx:
        return yd
    kern_ctx = functools.partial(_attn_ctx_kernel, lam_init=lam_init)
    return pl.pallas_call(
        kern_ctx,
        grid=(n_batch, DA_HEADS),
        in_specs=[
            pl.BlockSpec((CTX_LEN, DA_DV), lambda b, h: (ctx_base + b, U_QDA * cpb + h)),
            pl.BlockSpec((CTX_LEN, DA_DV), lambda b, h: (ctx_base + b, U_KDA * cpb + h)),
            pl.BlockSpec((CTX_LEN, DA_DV), lambda b, h: (ctx_base + b, U_VDA * cpb + h)),
            pl.BlockSpec((4, DA_DH), lambda b, h: (0, 0)),
            pl.BlockSpec((1, DA_DV), lambda b, h: (0, h)),
            pl.BlockSpec(memory_space=pl.ANY),
        ],
        out_specs=pl.BlockSpec((CTX_LEN, DA_DV), lambda b, h: (ctx_base + b, h)),
        out_shape=jax.ShapeDtypeStruct((n_rows, D_MODEL), BF16),
        input_output_aliases={5: 0},
        compiler_params=_cparams(("arbitrary", "arbitrary")),
        name="diff_attn_ctx",
    )(u, u, u, da_lam, head_g, yd)


def _dft_tables():
    R = FFT_R

    def cs(num, period):
        ang = (num % period).astype(F32) * (2.0 * math.pi / period)
        return jnp.cos(ang), jnp.sin(ang)

    idx = jnp.arange(R, dtype=jnp.int32)
    c1, s1 = cs(idx[:, None] * idx[None, :], R)
    a1 = jnp.concatenate([c1, -s1], axis=0).astype(BF16)
    f2 = idx[:, None, None]
    f1 = idx[None, :, None]
    t1 = idx[None, None, :]
    mc, ms = cs(t1 * (R * f1 + f2), SEQ)
    b2 = jnp.concatenate([jnp.concatenate([mc, ms], axis=2),
                          jnp.concatenate([-ms, mc], axis=2)], axis=1).astype(BF16)
    ch = jnp.arange(FN_GC, dtype=jnp.int32)
    cc, sc = cs(ch[:, None] * ch[None, :], FN_GC)
    csm = jnp.concatenate([cc, sc], axis=0).astype(BF16)
    actx = jnp.concatenate([cc, -sc], axis=0).astype(BF16)
    return a1, b2, csm, actx


def _fft1_kernel(a_ref, z_ref, p_ref):
    p_ref[0, 0] = _dot(a_ref[...], z_ref[0]).astype(p_ref.dtype)


def _fft1(fn_view, a1, n_batch):
    R = FFT_R
    lanes = R * FN_GC
    lc = lanes
    return pl.pallas_call(
        _fft1_kernel,
        grid=(FN_GROUPS, n_batch, lanes // lc),
        in_specs=[
            pl.BlockSpec((2 * R, R), lambda g, b, c: (0, 0)),
            pl.BlockSpec((1, R, lc), lambda g, b, c: (g, b, c)),
        ],
        out_specs=pl.BlockSpec((1, 1, 2 * R, lc), lambda g, b, c: (g, b, 0, c)),
        out_shape=jax.ShapeDtypeStruct((FN_GROUPS, n_batch, 2 * R, lanes), BF16),
        compiler_params=_cparams(("arbitrary", "arbitrary", "arbitrary")),
        name="fourier_stage1",
    )(a1, fn_view)


def _fft2_kernel(p_ref, b_ref, cs_ref, o_ref, *, f2b):
    R = FFT_R
    cc = cs_ref[0:FN_GC, :]
    sc = cs_ref[FN_GC:2 * FN_GC, :]
    norm = 1.0 / math.sqrt(SEQ * FN_GC)
    for g in range(FN_GROUPS):
        xr, xi = [], []
        for jj in range(f2b):
            stacked = jnp.concatenate([p_ref[g, 0, 0, jj], p_ref[g, 0, 1, jj]], axis=0)
            x = _dot(b_ref[jj], stacked)
            xr.append(x[0:R])
            xi.append(x[R:2 * R])
        y = (_dot(jnp.concatenate(xr, axis=0).astype(BF16), cc)
             + _dot(jnp.concatenate(xi, axis=0).astype(BF16), sc)) * norm
        for jj in range(f2b):
            lo = jj * D_MODEL + g * FN_GC
            o_ref[:, lo:lo + FN_GC] = y[jj * R:(jj + 1) * R].astype(o_ref.dtype)


def _fft2(p6, b2, csm, n_rows, n_batch):
    R = FFT_R
    f2b = 8
    kern = functools.partial(_fft2_kernel, f2b=f2b)
    return pl.pallas_call(
        kern,
        grid=(n_batch, R // f2b),
        in_specs=[
            pl.BlockSpec((FN_GROUPS, 1, 2, f2b, R, FN_GC), lambda b, f: (0, b, 0, f, 0, 0)),
            pl.BlockSpec((f2b, 2 * R, 2 * R), lambda b, f: (f, 0, 0)),
            pl.BlockSpec((2 * FN_GC, FN_GC), lambda b, f: (0, 0)),
        ],
        out_specs=pl.BlockSpec((R, f2b * D_MODEL), lambda b, f: (b, f)),
        out_shape=jax.ShapeDtypeStruct((n_rows // R, R * D_MODEL), BF16),
        compiler_params=_cparams(("arbitrary", "arbitrary")),
        name="fourier_stage2",
    )(p6, b2, csm)


def _fft_ctx_kernel(z_ref, a_ref, cs_ref, yin_ref, o_ref, *, n_batch):
    del yin_ref
    cc = cs_ref[0:FN_GC, :]
    sc = cs_ref[FN_GC:2 * FN_GC, :]
    norm = 1.0 / math.sqrt(CTX_LEN * FN_GC)
    n_ctx = n_batch * CTX_LEN
    groups = []
    for g in range(FN_GROUPS):
        z_all = z_ref[g].astype(F32).reshape(n_ctx, FN_GC)
        ys = []
        for b in range(n_batch):
            p = _dot(a_ref[...], z_all[b * CTX_LEN:(b + 1) * CTX_LEN].astype(BF16))
            ys.append(_dot(p[0:CTX_LEN].astype(BF16), cc) + _dot(p[CTX_LEN:2 * CTX_LEN].astype(BF16), sc))
        groups.append(jnp.concatenate(ys, axis=0))
    y = jnp.concatenate(groups, axis=1) * norm
    o_ref[...] = y.reshape(n_ctx // FFT_R, FFT_R * D_MODEL).astype(o_ref.dtype)


def _fft_ctx(fn_view, actx, csm, yf, n_batch):
    n_ctx = n_batch * CTX_LEN
    ctx_blk = n_batch * SEQ // n_ctx
    kern = functools.partial(_fft_ctx_kernel, n_batch=n_batch)
    return pl.pallas_call(
        kern,
        grid=(1,),
        in_specs=[
            pl.BlockSpec((FN_GROUPS, n_ctx // FFT_R, FFT_R * FN_GC), lambda i: (0, ctx_blk, 0)),
            pl.BlockSpec((2 * CTX_LEN, CTX_LEN), lambda i: (0, 0)),
            pl.BlockSpec((2 * FN_GC, FN_GC), lambda i: (0, 0)),
            pl.BlockSpec(memory_space=pl.ANY),
        ],
        out_specs=pl.BlockSpec((n_ctx // FFT_R, FFT_R * D_MODEL), lambda i: (ctx_blk, 0)),
        out_shape=jax.ShapeDtypeStruct(yf.shape, BF16),
        input_output_aliases={3: 0},
        compiler_params=_cparams(("arbitrary",)),
        name="fourier_ctx",
    )(fn_view, actx, csm, yf)


def _fourier(fn, tables, need_ctx, n_batch):
    a1, b2, csm, actx = tables
    R = FFT_R
    n_rows = fn.shape[1] * R
    p = _fft1(fn, a1, n_batch)
    yf = _fft2(p.reshape(FN_GROUPS, n_batch, 2, R, R, FN_GC), b2, csm, n_rows, n_batch)
    if need_ctx:
        yf = _fft_ctx(fn, actx, csm, yf, n_batch)
    return yf


def _merge_kernel(hf_ref, hb_ref, o_ref, gm_ref, gd_ref, gf_ref, yd_ref, yf_ref, xl_ref, xc_ref, mod_ref, hg_ref,
                  wml_ref, wda_ref, wfn_ref, wout_ref, out_ref, *, rows_per_batch, n_batch):
    i = pl.program_id(0)
    x = jnp.where(i < rows_per_batch * n_batch, xl_ref[...], xc_ref[...])
    hsum = hf_ref[...].astype(F32) + hb_ref[...].astype(F32)
    hg = hg_ref[...]
    parts = []
    for h in range(ML_HEADS):
        hs = slice(h * ML_DK, (h + 1) * ML_DK)
        parts.append(_rms(hsum[:, hs], hg[:, hs]))
    ym = (jnp.concatenate(parts, axis=1) * jax.nn.sigmoid(o_ref[...].astype(F32))).astype(BF16)
    yf = yf_ref[...].astype(F32).reshape(x.shape).astype(BF16)
    y = (jax.nn.sigmoid(gm_ref[...].astype(F32)) * _dot(ym, wml_ref[...])
         + jax.nn.sigmoid(gd_ref[...].astype(F32)) * _dot(yd_ref[...], wda_ref[...])
         + jax.nn.sigmoid(gf_ref[...].astype(F32)) * _dot(yf, wfn_ref[...]))
    gate = _mod_row(mod_ref, i, rows_per_batch, n_batch, 2)
    out_ref[...] = x + gate * _dot(y.astype(BF16), wout_ref[...])


def _merge(hf, hb, u, yd, yf, x_lat, x_ctx, ctx_blk0, mods, head_g, wml, wda, wfn, wout, layer, need_ctx, n_batch):
    n_rows = n_batch * (SEQ + CTX_LEN)
    tm = _row_tile(n_batch)
    rows_per_batch = SEQ // tm
    lat_blocks = n_batch * rows_per_batch
    ni = (n_rows if need_ctx else n_batch * SEQ) // tm
    kern = functools.partial(_merge_kernel, rows_per_batch=rows_per_batch, n_batch=n_batch)
    row = lambda i: (i, 0)
    full = lambda i: (0, 0)
    wspec = pl.BlockSpec((None, D_MODEL, D_MODEL), lambda i: (layer, 0, 0))
    return pl.pallas_call(
        kern,
        grid=(ni,),
        in_specs=[
            pl.BlockSpec((tm, D_MODEL), row),
            pl.BlockSpec((tm, D_MODEL), row),
            pl.BlockSpec((tm, D_MODEL), lambda i: (i, U_OML)),
            pl.BlockSpec((tm, D_MODEL), lambda i: (i, U_GPRE)),
            pl.BlockSpec((tm, D_MODEL), lambda i: (i, U_GPRE + 1)),
            pl.BlockSpec((tm, D_MODEL), lambda i: (i, U_GPRE + 2)),
            pl.BlockSpec((tm, D_MODEL), row),
            pl.BlockSpec((tm // FFT_R, FFT_R * D_MODEL), row),
            pl.BlockSpec((tm, D_MODEL), lambda i: (jnp.minimum(i, lat_blocks - 1), 0)),
            pl.BlockSpec((tm, D_MODEL), lambda i: (ctx_blk0 + jnp.maximum(i - lat_blocks, 0), 0)),
            pl.BlockSpec((None, 8, N_MOD), lambda i: (layer, 0, 0)),
            pl.BlockSpec((1, D_MODEL), full),
            wspec, wspec, wspec, wspec,
        ],
        out_specs=pl.BlockSpec((tm, D_MODEL), row),
        out_shape=jax.ShapeDtypeStruct((ni * tm, D_MODEL), F32),
        compiler_params=_cparams(("arbitrary",)),
        name="merge_out_proj",
    )(hf, hb, u, u, u, u, yd, yf, x_lat, x_ctx, mods, head_g, wml, wda, wfn, wout)


FFN_CHUNKS = ((0, 1024), (1024, 1024), (2048, 768))


def _ffn_kernel(x_ref, mod_ref, g_ref, win_ref, wout_ref, fg_ref, out_ref, *, rows_per_batch, n_batch, final):
    i = pl.program_id(0)
    x = x_ref[...]
    shift = _mod_row(mod_ref, i, rows_per_batch, n_batch, 3)
    scale = _mod_row(mod_ref, i, rows_per_batch, n_batch, 4)
    gate = _mod_row(mod_ref, i, rows_per_batch, n_batch, 5)
    h = (_rms(x, g_ref[...]) * (1.0 + scale) + shift).astype(BF16)
    acc = None
    for lo, width in FFN_CHUNKS:
        a = _dot(h, win_ref[:, lo:lo + width])
        b = _dot(h, win_ref[:, D_FF + lo:D_FF + lo + width])
        act = (a * jax.nn.sigmoid(a) * b).astype(BF16)
        part = _dot(act, wout_ref[lo:lo + width, :])
        acc = part if acc is None else acc + part
    xn = x + gate * acc
    out_ref[...] = _rms(xn, fg_ref[...]) if final else xn


def _ffn(x, mods, g, w_in, w_out, layer, final_g, final, n_rows_out, n_batch):
    tm = _row_tile(n_batch)
    rows_per_batch = SEQ // tm
    kern = functools.partial(_ffn_kernel, rows_per_batch=rows_per_batch, n_batch=n_batch, final=final)
    row = lambda i: (i, 0)
    full = lambda i: (0, 0)
    return pl.pallas_call(
        kern,
        grid=(n_rows_out // tm,),
        in_specs=[
            pl.BlockSpec((tm, D_MODEL), row),
            pl.BlockSpec((None, 8, N_MOD), lambda i: (layer, 0, 0)),
            pl.BlockSpec((1, D_MODEL), full),
            pl.BlockSpec((None, D_MODEL, 2 * D_FF), lambda i: (layer, 0, 0)),
            pl.BlockSpec((None, D_FF, D_MODEL), lambda i: (layer, 0, 0)),
            pl.BlockSpec((1, D_MODEL), full),
        ],
        out_specs=pl.BlockSpec((tm, D_MODEL), row),
        out_shape=jax.ShapeDtypeStruct((n_rows_out, D_MODEL), F32),
        compiler_params=_cparams(("arbitrary",)),
        name="swiglu_ffn",
    )(x, mods, g, w_in, w_out, final_g)


def _da_col_perm(w):
    half = DA_DH // 2
    lead = w.shape[:-1]
    return jnp.swapaxes(w.reshape(lead + (DA_HEADS, 2, 2, half)), -3, -2).reshape(lead + (DA_HEADS * DA_DV,))


def _rope_tables(pad):
    n_freq = DA_DH // 4
    rows = SEQ // GRID_W
    inv = ROPE_BASE ** (-jnp.arange(n_freq, dtype=F32) / n_freq)
    r = jnp.repeat(jnp.arange(rows, dtype=F32), GRID_W)
    col = jnp.tile(jnp.arange(GRID_W, dtype=F32), rows)
    ang = jnp.concatenate([r[:, None] * inv, col[:, None] * inv], axis=-1)
    cos, sin = jnp.cos(ang), jnp.sin(ang)
    cos_t = jnp.concatenate([cos, cos, cos, cos], axis=-1)
    sin_t = jnp.concatenate([-sin, -sin, sin, sin], axis=-1)
    cos_t = jnp.concatenate([cos_t, jnp.ones((pad, DA_DV), F32)], axis=0)
    sin_t = jnp.concatenate([sin_t, jnp.zeros((pad, DA_DV), F32)], axis=0)
    return cos_t, sin_t


def kernel(x, c, ctx, c_ctx, w_ada, b_ada, norm_g, w_in, ml_gate_b, ml_head_g, da_lam, da_head_g,
           w_br_ml, w_br_da, w_br_fn, w_out, w_ffn_in, w_ffn_out, final_g):
    n_batch = x.shape[0]
    n_lat = n_batch * SEQ
    x_lat = x.reshape(n_lat, D_MODEL)
    x_ctx = ctx.reshape(n_batch * CTX_LEN, D_MODEL)
    cc = jnp.concatenate([c, c_ctx[None, :], jnp.zeros((8 - n_batch - 1, D_MODEL), F32)], axis=0)
    mods = _mods(cc, w_ada, b_ada)
    cos_t, sin_t = _rope_tables(n_batch * CTX_LEN)
    tables = _dft_tables()
    final_g2 = final_g.reshape(1, D_MODEL)
    wb_ml, wb_da, wb_fn, wb_out = _to_bf16(w_br_ml), _to_bf16(w_br_da), _to_bf16(w_br_fn), _to_bf16(w_out)
    wb_ffn_in, wb_ffn_out = _to_bf16(w_ffn_in), _to_bf16(w_ffn_out)

    gate_lo = 4 * D_MODEL
    da_lo = gate_lo + N_GATE
    fn_lo = da_lo + 3 * D_MODEL
    w_main = jnp.concatenate([w_in[..., :D_MODEL], w_in[..., 2 * D_MODEL:gate_lo],
                              _da_col_perm(w_in[..., da_lo:da_lo + D_MODEL]),
                              _da_col_perm(w_in[..., da_lo + D_MODEL:da_lo + 2 * D_MODEL]),
                              w_in[..., da_lo + 2 * D_MODEL:fn_lo],
                              w_in[..., fn_lo + D_MODEL:], w_in[..., fn_lo:fn_lo + D_MODEL]], axis=-1).astype(BF16)
    w_kt = jnp.swapaxes(w_in[..., D_MODEL:2 * D_MODEL], 1, 2).astype(BF16)
    w_gate_t = jnp.swapaxes(w_in[..., gate_lo:da_lo], 1, 2).astype(BF16)

    tm_in = n_batch * CTX_LEN
    tm_tok = _row_tile(n_batch)
    xs = None
    for l in range(DEPTH):
        need_ctx = l < DEPTH - 1
        lam_init = 0.8 - 0.6 * math.exp(-0.3 * l)
        xl, xc = (x_lat, x_ctx) if xs is None else (xs, xs)
        u, kt, fn, gates_t = _inproj(xl, xc, 0 if xs is None else n_lat // tm_in, mods,
                                     norm_g[l, 0].reshape(1, D_MODEL), w_main, w_kt, w_gate_t, l,
                                     cos_t, sin_t, n_batch)
        hf, hb = _mlstm(u, kt, gates_t, ml_gate_b[l], n_batch)
        yd = _attn(u, da_lam[l], da_head_g[l].reshape(1, D_MODEL), lam_init, need_ctx, n_batch)
        yf = _fourier(fn, tables, need_ctx, n_batch)
        xs = _merge(hf, hb, u, yd, yf, xl, xc, 0 if xs is None else n_lat // tm_tok, mods,
                    ml_head_g[l].reshape(1, D_MODEL), wb_ml, wb_da, wb_fn, wb_out, l, need_ctx, n_batch)
        final = l == DEPTH - 1
        n_out = n_lat if final else xs.shape[0]
        xs = _ffn(xs, mods, norm_g[l, 1].reshape(1, D_MODEL), wb_ffn_in, wb_ffn_out, l,
                  final_g2, final, n_out, n_batch)
    return xs.reshape(n_batch, SEQ, D_MODEL)
```

```python
import functools
import math

import jax
import jax.numpy as jnp
from jax import lax
from jax.experimental import pallas as pl
from jax.experimental.pallas import tpu as pltpu

D_MODEL = 1024
SEQ = 4096
DEPTH = 4
CTX_LEN = 256
GRID_W = 64
NORM_EPS = 1e-6

ML_HEADS = 4
ML_DK = 256
ML_CHUNK = 256

DA_HEADS = 8
DA_DH = 64
DA_DV = 2 * DA_DH
ROPE_BASE = 10000.0
ATTN_TQ = 512
ATTN_ROW_CHUNKS = 2

FN_GROUPS = 4
FN_GC = 256
FFT_R = 64

D_FF = 2816
N_GATE = 4 * ML_HEADS
N_MOD = 6 * D_MODEL

U_QML, U_VML, U_OML, U_QDA, U_KDA, U_VDA, U_GPRE = 0, 1, 2, 3, 4, 5, 6
U_BLOCKS = 9
W_BLOCKS = U_BLOCKS + 1
STEP_KT = 1
N_COL_STEPS = W_BLOCKS // 2

VMEM_LIMIT_V7X = 56 * 1024 * 1024
LANES_V7X = 128

BF16 = jnp.bfloat16
F32 = jnp.float32


def _cparams(sem):
    return pltpu.CompilerParams(dimension_semantics=sem, vmem_limit_bytes=VMEM_LIMIT_V7X)


def _dot(a, b):
    return jnp.dot(a, b, preferred_element_type=F32)


def _dot_nt(a, b):
    return lax.dot_general(a, b, (((1,), (1,)), ((), ())), preferred_element_type=F32)


def _mod_row(mod_ref, i, rows_per_batch, n_batch, col):
    r = jnp.minimum(i // rows_per_batch, n_batch)
    return mod_ref[pl.ds(r, 1), col * D_MODEL:(col + 1) * D_MODEL]


def _row_tile(n_batch):
    return min(512, n_batch * CTX_LEN)


def _rms(x, g):
    return x * lax.rsqrt(jnp.mean(x * x, axis=-1, keepdims=True) + NORM_EPS) * g


def _cast_kernel(w_ref, o_ref):
    o_ref[...] = w_ref[...].astype(o_ref.dtype)


def _to_bf16(w):
    n_l, rows, cols = w.shape
    tr = 256
    return pl.pallas_call(
        _cast_kernel,
        grid=(n_l, rows // tr),
        in_specs=[pl.BlockSpec((1, tr, cols), lambda l, i: (l, i, 0))],
        out_specs=pl.BlockSpec((1, tr, cols), lambda l, i: (l, i, 0)),
        out_shape=jax.ShapeDtypeStruct(w.shape, BF16),
        compiler_params=_cparams(("arbitrary", "arbitrary")),
        name="cast_bf16",
    )(w)


def _mods_kernel(c_ref, w_ref, b_ref, o_ref):
    c = c_ref[...]
    s = (c * jax.nn.sigmoid(c)).astype(BF16)
    o_ref[0] = _dot(s, w_ref[0].astype(BF16)) + b_ref[0]


def _mods(cc, w_ada, b_ada):
    tn = 1536
    return pl.pallas_call(
        _mods_kernel,
        grid=(DEPTH, N_MOD // tn),
        in_specs=[
            pl.BlockSpec((8, D_MODEL), lambda l, j: (0, 0)),
            pl.BlockSpec((1, D_MODEL, tn), lambda l, j: (l, 0, j)),
            pl.BlockSpec((1, 1, tn), lambda l, j: (l, 0, j)),
        ],
        out_specs=pl.BlockSpec((1, 8, tn), lambda l, j: (l, 0, j)),
        out_shape=jax.ShapeDtypeStruct((DEPTH, 8, N_MOD), F32),
        compiler_params=_cparams(("arbitrary", "arbitrary")),
        name="adaln_mods",
    )(cc, w_ada, b_ada.reshape(DEPTH, 1, N_MOD))


def _inproj_kernel(xl_ref, xc_ref, mod_ref, g_ref, w_ref, wkt_ref, wgt_ref, cos_ref, sin_ref,
                   u_ref, kt_ref, fn_ref, gate_ref, xn_ref, *, rows_per_batch, n_batch):
    i = pl.program_id(0)
    j = pl.program_id(1)

    def normalise(x_ref):
        y = _rms(x_ref[...], g_ref[...])
        shift = _mod_row(mod_ref, i, rows_per_batch, n_batch, 0)
        scale = _mod_row(mod_ref, i, rows_per_batch, n_batch, 1)
        xn_ref[...] = (y * (1.0 + scale) + shift).astype(BF16)
        gate_ref[...] = _dot_nt(wgt_ref[...], xn_ref[...])

    is_lat = i < rows_per_batch * n_batch

    @pl.when(jnp.logical_and(j == 0, is_lat))
    def _():
        normalise(xl_ref)

    @pl.when(jnp.logical_and(j == 0, jnp.logical_not(is_lat)))
    def _():
        normalise(xc_ref)

    def product(h):
        return _dot(xn_ref[...], w_ref[:, h * D_MODEL:(h + 1) * D_MODEL])

    def plain(h):
        u_ref[:, h * D_MODEL:(h + 1) * D_MODEL] = product(h).astype(BF16)

    def rope(h):
        acc = product(h)
        cos = cos_ref[...]
        sin = sin_ref[...]
        for t in range(acc.shape[1] // DA_DV):
            x = acc[:, t * DA_DV:(t + 1) * DA_DV]
            lo = h * D_MODEL + t * DA_DV
            u_ref[:, lo:lo + DA_DV] = (x * cos + pltpu.roll(x, DA_DV // 2, 1) * sin).astype(BF16)

    def fourier_in(h):
        acc = product(h)
        for g in range(FN_GROUPS):
            z = acc[:, g * FN_GC:(g + 1) * FN_GC]
            fn_ref[g] = z.reshape(z.shape[0] // FFT_R, FFT_R * FN_GC).astype(BF16)

    def keys_t():
        kt_ref[...] = (_dot_nt(wkt_ref[...], xn_ref[...]) * (ML_DK ** -0.5)).astype(BF16)

    epilogues = {U_QML: plain, U_VML: plain, U_OML: plain, U_QDA: rope, U_KDA: rope, U_VDA: plain,
                 U_GPRE: plain, U_GPRE + 1: plain, U_GPRE + 2: plain, U_BLOCKS: fourier_in}
    for step in range(N_COL_STEPS):
        @pl.when(j == step)
        def _(step=step):
            for h in range(2):
                epilogues[2 * step + h](h)
            if step == STEP_KT:
                keys_t()


def _inproj(x_lat, x_ctx, ctx_blk, mods, g, w_main, w_kt, w_gate_t, layer, cos_t, sin_t, n_batch):
    tm = n_batch * CTX_LEN
    assert SEQ % tm == 0 and cos_t.shape[0] == SEQ + tm
    n_rows = n_batch * (SEQ + CTX_LEN)
    ni = n_rows // tm
    rows_per_batch = SEQ // tm
    lat_blocks = n_batch * rows_per_batch

    def tab_idx(i, j):
        return (jnp.where(i < lat_blocks, i % rows_per_batch, rows_per_batch), 0)

    kern = functools.partial(_inproj_kernel, rows_per_batch=rows_per_batch, n_batch=n_batch)
    return pl.pallas_call(
        kern,
        grid=(ni, N_COL_STEPS),
        in_specs=[
            pl.BlockSpec((tm, D_MODEL), lambda i, j: (jnp.minimum(i, lat_blocks - 1), 0)),
            pl.BlockSpec((tm, D_MODEL), lambda i, j: (ctx_blk, 0)),
            pl.BlockSpec((None, 8, N_MOD), lambda i, j: (layer, 0, 0)),
            pl.BlockSpec((1, D_MODEL), lambda i, j: (0, 0)),
            pl.BlockSpec((None, D_MODEL, 2 * D_MODEL), lambda i, j: (layer, 0, j)),
            pl.BlockSpec((None, D_MODEL, D_MODEL), lambda i, j: (layer, 0, 0)),
            pl.BlockSpec((None, N_GATE, D_MODEL), lambda i, j: (layer, 0, 0)),
            pl.BlockSpec((tm, DA_DV), tab_idx),
            pl.BlockSpec((tm, DA_DV), tab_idx),
        ],
        out_specs=[
            pl.BlockSpec((tm, 2 * D_MODEL), lambda i, j: (i, j)),
            pl.BlockSpec((D_MODEL, tm), lambda i, j: (0, i)),
            pl.BlockSpec((FN_GROUPS, tm // FFT_R, FFT_R * FN_GC), lambda i, j: (0, i, 0)),
            pl.BlockSpec((N_GATE, tm), lambda i, j: (0, i)),
        ],
        out_shape=[
            jax.ShapeDtypeStruct((n_rows, U_BLOCKS * D_MODEL), BF16),
            jax.ShapeDtypeStruct((D_MODEL, n_rows), BF16),
            jax.ShapeDtypeStruct((FN_GROUPS, n_rows // FFT_R, FFT_R * FN_GC), BF16),
            jax.ShapeDtypeStruct((N_GATE, n_rows), F32),
        ],
        scratch_shapes=[pltpu.VMEM((tm, D_MODEL), BF16)],
        compiler_params=_cparams(("arbitrary", "arbitrary")),
        name="in_proj",
    )(x_lat, x_ctx, mods, g, w_main, w_kt, w_gate_t, cos_t, sin_t)


def _split3(x):
    hi = x.astype(BF16).astype(F32)
    mid = (x - hi).astype(BF16).astype(F32)
    lo = (x - hi - mid).astype(BF16).astype(F32)
    return hi, mid, lo


def _mlstm_kernel(qf_ref, ktf_ref, vf_ref, gtf_ref, gtfn_ref, qb_ref, ktb_ref, vb_ref, gtb_ref, gtbn_ref, bias_ref,
                  hf_ref, hb_ref, *scratch):
    n_st = 2 * ML_HEADS
    cx_refs, bw_refs, pmw_refs = scratch[:n_st], scratch[n_st:2 * n_st], scratch[2 * n_st:3 * n_st]
    m_refs, c_refs, bend_refs, g_refs = (scratch[3 * n_st + 2 * k:3 * n_st + 2 * k + 2] for k in range(4))
    state_refs = cx_refs + m_refs
    s = pl.program_id(1)
    L = ML_CHUNK
    H = ML_HEADS
    W = LANES_V7X

    @pl.when(s == 0)
    def _():
        for ref in state_refs:
            ref[...] = jnp.zeros_like(ref)

    t_idx = lax.broadcasted_iota(jnp.int32, (L, L), 0)
    s_idx = lax.broadcasted_iota(jnp.int32, (L, L), 1)
    eye = t_idx == s_idx
    sub8 = lax.broadcasted_iota(jnp.int32, (8, W), 0)
    ones_w = jnp.ones((L, W), BF16)
    er = lax.broadcasted_iota(jnp.int32, (4 * L, 2 * W), 0)
    ec = lax.broadcasted_iota(jnp.int32, (4 * L, 2 * W), 1)
    expand = jnp.where((er < 3 * L) == (ec < W), 1.0, 0.0).astype(BF16)

    def running_max_rows(x, d):
        n_tiles = L // 8
        out = [None] * n_tiles
        carry = None
        for j in (range(n_tiles) if d == 0 else range(n_tiles - 1, -1, -1)):
            r = x[8 * j:8 * (j + 1)]
            k = 1
            while k < 8:
                if d == 0:
                    r = jnp.maximum(r, jnp.where(sub8 >= k, pltpu.roll(r, k, 0), -jnp.inf))
                else:
                    r = jnp.maximum(r, jnp.where(sub8 < 8 - k, pltpu.roll(r, 8 - k, 0), -jnp.inf))
                k *= 2
            if carry is not None:
                r = jnp.maximum(r, carry)
            carry = jnp.broadcast_to(r[7:8] if d == 0 else r[0:1], (8, W))
            out[j] = r
        return jnp.concatenate(out, axis=0)

    def gate_part(d, gt_ref):
        before = (t_idx <= s_idx) if d == 0 else (t_idx >= s_idx)
        gt = gt_ref[2 * H * d:2 * H * (d + 1), :] + bias_ref[2 * H * d:2 * H * (d + 1), :]
        i4, f4 = gt[0:H], gt[H:2 * H]
        lf4 = jnp.minimum(f4, 0.0) - jnp.log1p(jnp.exp(-jnp.abs(f4)))
        lf_terms = jnp.concatenate(list(_split3(lf4)) + [jnp.zeros((H, L), F32)], axis=0).astype(BF16)
        cum_rhs = jnp.concatenate([jnp.where(before, 1.0, 0.0).astype(BF16), ones_w], axis=1)
        r = _dot(lf_terms, cum_rhs)
        bx = r[0:H] + r[H:2 * H] + r[2 * H:3 * H]
        b4, bend4 = bx[:, 0:L], bx[:, L:L + W]
        c4 = i4 - b4
        c_refs[d][...] = c4
        bend_refs[d][...] = bend4
        g_refs[d][...] = bend4[:, 0:1] - b4 + i4
        col_terms = _split3(b4) + (c4.astype(BF16).astype(F32),)
        for h in range(H):
            diag = jnp.concatenate([jnp.where(eye, x[h:h + 1, :], 0.0) for x in col_terms], axis=1).astype(BF16)
            wide = _dot(diag, expand)
            bw_refs[d * H + h][...] = wide[:, 0:W]
            pmw_refs[d * H + h][...] = running_max_rows(wide[:, W:2 * W], d)

    dirs = ((0, qf_ref, ktf_ref, vf_ref, gtf_ref, gtfn_ref, hf_ref),
            (1, qb_ref, ktb_ref, vb_ref, gtb_ref, gtbn_ref, hb_ref))

    @pl.when(s == 0)
    def _():
        for d, _, _, _, gt_ref, _, _ in dirs:
            gate_part(d, gt_ref)

    for d, q_ref, kt_ref, v_ref, _, _, h_ref in dirs:
        causal = (s_idx <= t_idx) if d == 0 else (s_idx >= t_idx)
        c4, bend4, g4 = c_refs[d][...], bend_refs[d][...], g_refs[d][...]
        m_prev4 = m_refs[d][...]
        m_new4 = jnp.maximum(bend4 + m_prev4, jnp.max(g4, axis=1, keepdims=True))
        m_refs[d][...] = m_new4
        decay4 = jnp.exp(bend4 + m_prev4 - m_new4)
        w4 = jnp.exp(g4 - m_new4[:, 0:1])
        for h in range(H):
            st = d * H + h
            hs = slice(h * ML_DK, (h + 1) * ML_DK)
            b_w = bw_refs[st][...]
            m_w = jnp.maximum(pmw_refs[st][...], m_prev4[h:h + 1, :])

            q = q_ref[:, hs]
            kt = kt_ref[hs, :]
            vx = jnp.concatenate([v_ref[:, hs], ones_w], axis=1)
            m_ll = jnp.concatenate([m_w] * (L // W), axis=1)
            a = (jnp.where(causal, jnp.exp(c4[h:h + 1, :] - m_ll), 0.0) * _dot(q, kt)).astype(BF16)
            cx_prev = cx_refs[st][...]
            qc = _dot(q, cx_prev.astype(BF16))
            av = _dot(a, vx)
            sc_w = jnp.exp(m_prev4[h:h + 1, :] - m_w)
            den = sc_w * qc[:, ML_DK:] + av[:, ML_DK:]
            inv = 1.0 / jnp.maximum(jnp.abs(den), jnp.exp(-(b_w + m_w)))
            for t in range(ML_DK // W):
                ts = slice(t * W, (t + 1) * W)
                h_ref[:, h * ML_DK + t * W:h * ML_DK + (t + 1) * W] = (
                    (sc_w * qc[:, ts] + av[:, ts]) * inv).astype(h_ref.dtype)

            kw = (kt.astype(F32) * w4[h:h + 1, :]).astype(BF16)
            dec = jnp.concatenate([decay4[h:h + 1, :]] * (ML_DK // W + 1), axis=1)
            cx_refs[st][...] = dec * cx_prev + _dot(kw, vx)

    for d, _, _, _, _, gtn_ref, _ in dirs:
        gate_part(d, gtn_ref)


def _mlstm(u, kt, gates_t, gate_b, n_batch):
    n_rows = u.shape[0]
    L = ML_CHUNK
    lat_chunks = SEQ // L
    ctx_chunks = CTX_LEN // L
    n_steps = ctx_chunks + lat_chunks
    ctx_base = n_batch * lat_chunks

    def rowblk(d):
        def f(b, s):
            in_ctx = s < ctx_chunks
            if d == 0:
                c = jnp.where(in_ctx, s, s - ctx_chunks)
            else:
                c = jnp.where(in_ctx, ctx_chunks - 1 - s, lat_chunks - 1 - (s - ctx_chunks))
            return jnp.where(in_ctx, ctx_base + ctx_chunks * b, lat_chunks * b) + c
        return f

    def dir_specs(d):
        rb = rowblk(d)
        return [
            pl.BlockSpec((L, D_MODEL), lambda b, s: (rb(b, s), U_QML)),
            pl.BlockSpec((D_MODEL, L), lambda b, s: (0, rb(b, s))),
            pl.BlockSpec((L, D_MODEL), lambda b, s: (rb(b, s), U_VML)),
            pl.BlockSpec((N_GATE, L), lambda b, s: (0, rb(b, s))),
            pl.BlockSpec((N_GATE, L), lambda b, s: (0, rb(b, jnp.minimum(s + 1, n_steps - 1)))),
        ]

    def out_spec(d):
        rb = rowblk(d)
        return pl.BlockSpec((L, D_MODEL), lambda b, s: (rb(b, s), 0))

    n_st = 2 * ML_HEADS
    return pl.pallas_call(
        _mlstm_kernel,
        grid=(n_batch, n_steps),
        in_specs=dir_specs(0) + dir_specs(1) + [pl.BlockSpec((N_GATE, 1), lambda b, s: (0, 0))],
        out_specs=[out_spec(0), out_spec(1)],
        out_shape=[jax.ShapeDtypeStruct((n_rows, D_MODEL), BF16)] * 2,
        scratch_shapes=(
            [pltpu.VMEM((ML_DK, ML_DK + LANES_V7X), F32)] * n_st
            + [pltpu.VMEM((L, LANES_V7X), F32)] * (2 * n_st)
            + [pltpu.VMEM((ML_HEADS, LANES_V7X), F32)] * 2
            + [pltpu.VMEM((ML_HEADS, L), F32)] * 2
            + [pltpu.VMEM((ML_HEADS, LANES_V7X), F32)] * 2
            + [pltpu.VMEM((ML_HEADS, L), F32)] * 2),
        compiler_params=_cparams(("arbitrary", "arbitrary")),
        name="mlstm_scan",
    )(u, kt, u, gates_t, gates_t, u, kt, u, gates_t, gates_t, gate_b.reshape(N_GATE, 1))


def _attn_lambda(lam_ref, lam_init):
    lq = lam_ref[...]
    return (jnp.exp(jnp.sum(lq[0:1] * lq[1:2], axis=1, keepdims=True))
            - jnp.exp(jnp.sum(lq[2:3] * lq[3:4], axis=1, keepdims=True)) + lam_init)


def _attn_queries(q_ref):
    q = q_ref[...]
    lane = lax.broadcasted_iota(jnp.int32, (1, DA_DV), 1)
    zero = jnp.zeros_like(q)
    is_map0 = (lane % DA_DH) < (DA_DH // 2)
    q2 = jnp.concatenate([jnp.where(is_map0, q, zero), jnp.where(is_map0, zero, q)], axis=0)
    return q2 * (DA_DH ** -0.5)


def _attn_sums(s_chunks, m_chunks, vx):
    return jnp.concatenate([_dot(jnp.exp(s - m).astype(BF16), vx) for s, m in zip(s_chunks, m_chunks)], axis=0)


def _attn_finish(acc, lam, g, lam_init, tq):
    o0 = acc[0:tq, 0:DA_DV] * (1.0 / acc[0:tq, DA_DV:DA_DV + 1])
    o1 = acc[tq:, 0:DA_DV] * (1.0 / acc[tq:, DA_DV:DA_DV + 1])
    return _rms(o0 - lam * o1, g) * (1.0 - lam_init)


def _attn_kernel(q_ref, kl_ref, vl_ref, kc_ref, vc_ref, lam_ref, g_ref, o_ref, vx_ref, s_ref, m_ref,
                 *, lam_init, q_blocks):
    t = pl.program_id(0)
    tq = q_ref.shape[0]
    rows = 2 * tq // ATTN_ROW_CHUNKS

    @pl.when(jnp.logical_and(t > 0, (t - 1) % q_blocks == 0))
    def _():
        vx_ref[0:SEQ, 0:DA_DV] = vl_ref[...]
        vx_ref[SEQ:, 0:DA_DV] = vc_ref[...]
        lane_v = lax.broadcasted_iota(jnp.int32, (SEQ + CTX_LEN, DA_DV), 1)
        vx_ref[:, DA_DV:] = jnp.where(lane_v == 0, 1.0, 0.0).astype(BF16)

    def score(slot):
        q2 = _attn_queries(q_ref)
        s_lat = _dot_nt(q2, kl_ref[...])
        s_ctx = _dot_nt(q2, kc_ref[...])
        m_ref[slot] = jnp.maximum(jnp.max(s_lat, axis=1, keepdims=True), jnp.max(s_ctx, axis=1, keepdims=True))
        s_ref[slot, :, 0:SEQ] = s_lat
        s_ref[slot, :, SEQ:] = s_ctx

    def drain(slot):
        s_chunks = [s_ref[slot, c * rows:(c + 1) * rows, :] for c in range(ATTN_ROW_CHUNKS)]
        m_chunks = [m_ref[slot, c * rows:(c + 1) * rows, :] for c in range(ATTN_ROW_CHUNKS)]
        acc = _attn_sums(s_chunks, m_chunks, vx_ref[...])
        o = _attn_finish(acc, _attn_lambda(lam_ref, lam_init), g_ref[...], lam_init, tq)
        o_ref[...] = o.astype(o_ref.dtype)

    @pl.when(t == 0)
    def _():
        score(0)

    for parity in (0, 1):
        @pl.when(jnp.logical_and(t > 0, t % 2 == parity))
        def _():
            drain(1 - parity)
            score(parity)


def _attn_ctx_kernel(q_ref, kc_ref, vc_ref, lam_ref, g_ref, yin_ref, o_ref, *, lam_init):
    del yin_ref
    tq = q_ref.shape[0]
    s = _dot_nt(_attn_queries(q_ref), kc_ref[...])
    lane_v = lax.broadcasted_iota(jnp.int32, (CTX_LEN, DA_DV), 1)
    vx = jnp.concatenate([vc_ref[...], jnp.where(lane_v == 0, 1.0, 0.0).astype(BF16)], axis=1)
    acc = _attn_sums([s], [jnp.max(s, axis=1, keepdims=True)], vx)
    o = _attn_finish(acc, _attn_lambda(lam_ref, lam_init), g_ref[...], lam_init, tq)
    o_ref[...] = o.astype(o_ref.dtype)


def _attn(u, da_lam, head_g, lam_init, need_ctx, n_batch):
    n_rows = u.shape[0]
    tq = ATTN_TQ
    lat_qblocks = SEQ // tq
    ctx_base = n_batch * (SEQ // CTX_LEN)
    cpb = D_MODEL // DA_DV
    n_keys = SEQ + CTX_LEN

    n_items = n_batch * DA_HEADS * lat_qblocks

    def item(t):
        t = jnp.clip(t, 0, n_items - 1)
        return t // (DA_HEADS * lat_qblocks), (t // lat_qblocks) % DA_HEADS, t % lat_qblocks

    def scored(f):
        return lambda t: f(*item(t))

    def drained(f):
        return lambda t: f(*item(t - 1))

    kern = functools.partial(_attn_kernel, lam_init=lam_init, q_blocks=lat_qblocks)
    yd = pl.pallas_call(
        kern,
        grid=(n_items + 1,),
        in_specs=[
            pl.BlockSpec((tq, DA_DV), scored(lambda b, h, qi: (b * lat_qblocks + qi, U_QDA * cpb + h))),
            pl.BlockSpec((SEQ, DA_DV), scored(lambda b, h, qi: (b, U_KDA * cpb + h))),
            pl.BlockSpec((SEQ, DA_DV), drained(lambda b, h, qi: (b, U_VDA * cpb + h))),
            pl.BlockSpec((CTX_LEN, DA_DV), scored(lambda b, h, qi: (ctx_base + b, U_KDA * cpb + h))),
            pl.BlockSpec((CTX_LEN, DA_DV), drained(lambda b, h, qi: (ctx_base + b, U_VDA * cpb + h))),
            pl.BlockSpec((4, DA_DH), lambda t: (0, 0)),
            pl.BlockSpec((1, DA_DV), drained(lambda b, h, qi: (0, h))),
        ],
        out_specs=pl.BlockSpec((tq, DA_DV), drained(lambda b, h, qi: (b * lat_qblocks + qi, h))),
        out_shape=jax.ShapeDtypeStruct((n_rows, D_MODEL), BF16),
        scratch_shapes=[
            pltpu.VMEM((n_keys, 2 * DA_DV), BF16),
            pltpu.VMEM((2, 2 * tq, n_keys), F32),
            pltpu.VMEM((2, 2 * tq, 1), F32),
        ],
        compiler_params=_cparams(("arbitrary",)),
        name="diff_attn",
    )(u, u, u, u, u, da_lam, head_g)
    if not need_ctx:
        return yd
    kern_ctx = functools.partial(_attn_ctx_kernel, lam_init=lam_init)
    return pl.pallas_call(
        kern_ctx,
        grid=(n_batch, DA_HEADS),
        in_specs=[
            pl.BlockSpec((CTX_LEN, DA_DV), lambda b, h: (ctx_base + b, U_QDA * cpb + h)),
            pl.BlockSpec((CTX_LEN, DA_DV), lambda b, h: (ctx_base + b, U_KDA * cpb + h)),
            pl.BlockSpec((CTX_LEN, DA_DV), lambda b, h: (ctx_base + b, U_VDA * cpb + h)),
            pl.BlockSpec((4, DA_DH), lambda b, h: (0, 0)),
            pl.BlockSpec((1, DA_DV), lambda b, h: (0, h)),
            pl.BlockSpec(memory_space=pl.ANY),
        ],
        out_specs=pl.BlockSpec((CTX_LEN, DA_DV), lambda b, h: (ctx_base + b, h)),
        out_shape=jax.ShapeDtypeStruct((n_rows, D_MODEL), BF16),
        input_output_aliases={5: 0},
        compiler_params=_cparams(("arbitrary", "arbitrary")),
        name="diff_attn_ctx",
    )(u, u, u, da_lam, head_g, yd)


def _dft_tables():
    R = FFT_R

    def cs(num, period):
        ang = (num % period).astype(F32) * (2.0 * math.pi / period)
        return jnp.cos(ang), jnp.sin(ang)

    idx = jnp.arange(R, dtype=jnp.int32)
    c1, s1 = cs(idx[:, None] * idx[None, :], R)
    a1 = jnp.concatenate([c1, -s1], axis=0).astype(BF16)
    f2 = idx[:, None, None]
    f1 = idx[None, :, None]
    t1 = idx[None, None, :]
    mc, ms = cs(t1 * (R * f1 + f2), SEQ)
    b2 = jnp.concatenate([jnp.concatenate([mc, ms], axis=2),
                          jnp.concatenate([-ms, mc], axis=2)], axis=1).astype(BF16)
    ch = jnp.arange(FN_GC, dtype=jnp.int32)
    cc, sc = cs(ch[:, None] * ch[None, :], FN_GC)
    csm = jnp.concatenate([cc, sc], axis=0).astype(BF16)
    actx = jnp.concatenate([cc, -sc], axis=0).astype(BF16)
    return a1, b2, csm, actx


def _fft1_kernel(a_ref, z_ref, p_ref):
    p_ref[0, 0] = _dot(a_ref[...], z_ref[0]).astype(p_ref.dtype)


def _fft1(fn_view, a1, n_batch):
    R = FFT_R
    lanes = R * FN_GC
    lc = lanes
    return pl.pallas_call(
        _fft1_kernel,
        grid=(FN_GROUPS, n_batch, lanes // lc),
        in_specs=[
            pl.BlockSpec((2 * R, R), lambda g, b, c: (0, 0)),
            pl.BlockSpec((1, R, lc), lambda g, b, c: (g, b, c)),
        ],
        out_specs=pl.BlockSpec((1, 1, 2 * R, lc), lambda g, b, c: (g, b, 0, c)),
        out_shape=jax.ShapeDtypeStruct((FN_GROUPS, n_batch, 2 * R, lanes), BF16),
        compiler_params=_cparams(("arbitrary", "arbitrary", "arbitrary")),
        name="fourier_stage1",
    )(a1, fn_view)


def _fft2_kernel(p_ref, b_ref, cs_ref, o_ref, *, f2b):
    R = FFT_R
    cc = cs_ref[0:FN_GC, :]
    sc = cs_ref[FN_GC:2 * FN_GC, :]
    norm = 1.0 / math.sqrt(SEQ * FN_GC)
    for g in range(FN_GROUPS):
        xr, xi = [], []
        for jj in range(f2b):
            stacked = jnp.concatenate([p_ref[g, 0, 0, jj], p_ref[g, 0, 1, jj]], axis=0)
            x = _dot(b_ref[jj], stacked)
            xr.append(x[0:R])
            xi.append(x[R:2 * R])
        y = (_dot(jnp.concatenate(xr, axis=0).astype(BF16), cc)
             + _dot(jnp.concatenate(xi, axis=0).astype(BF16), sc)) * norm
        for jj in range(f2b):
            lo = jj * D_MODEL + g * FN_GC
            o_ref[:, lo:lo + FN_GC] = y[jj * R:(jj + 1) * R].astype(o_ref.dtype)


def _fft2(p6, b2, csm, n_rows, n_batch):
    R = FFT_R
    f2b = 16
    kern = functools.partial(_fft2_kernel, f2b=f2b)
    return pl.pallas_call(
        kern,
        grid=(n_batch, R // f2b),
        in_specs=[
            pl.BlockSpec((FN_GROUPS, 1, 2, f2b, R, FN_GC), lambda b, f: (0, b, 0, f, 0, 0)),
            pl.BlockSpec((f2b, 2 * R, 2 * R), lambda b, f: (f, 0, 0)),
            pl.BlockSpec((2 * FN_GC, FN_GC), lambda b, f: (0, 0)),
        ],
        out_specs=pl.BlockSpec((R, f2b * D_MODEL), lambda b, f: (b, f)),
        out_shape=jax.ShapeDtypeStruct((n_rows // R, R * D_MODEL), BF16),
        compiler_params=_cparams(("arbitrary", "arbitrary")),
        name="fourier_stage2",
    )(p6, b2, csm)


def _fft_ctx_kernel(z_ref, a_ref, cs_ref, yin_ref, o_ref, *, n_batch):
    del yin_ref
    cc = cs_ref[0:FN_GC, :]
    sc = cs_ref[FN_GC:2 * FN_GC, :]
    norm = 1.0 / math.sqrt(CTX_LEN * FN_GC)
    n_ctx = n_batch * CTX_LEN
    groups = []
    for g in range(FN_GROUPS):
        z_all = z_ref[g].astype(F32).reshape(n_ctx, FN_GC)
        ys = []
        for b in range(n_batch):
            p = _dot(a_ref[...], z_all[b * CTX_LEN:(b + 1) * CTX_LEN].astype(BF16))
            ys.append(_dot(p[0:CTX_LEN].astype(BF16), cc) + _dot(p[CTX_LEN:2 * CTX_LEN].astype(BF16), sc))
        groups.append(jnp.concatenate(ys, axis=0))
    y = jnp.concatenate(groups, axis=1) * norm
    o_ref[...] = y.reshape(n_ctx // FFT_R, FFT_R * D_MODEL).astype(o_ref.dtype)


def _fft_ctx(fn_view, actx, csm, yf, n_batch):
    n_ctx = n_batch * CTX_LEN
    ctx_blk = n_batch * SEQ // n_ctx
    kern = functools.partial(_fft_ctx_kernel, n_batch=n_batch)
    return pl.pallas_call(
        kern,
        grid=(1,),
        in_specs=[
            pl.BlockSpec((FN_GROUPS, n_ctx // FFT_R, FFT_R * FN_GC), lambda i: (0, ctx_blk, 0)),
            pl.BlockSpec((2 * CTX_LEN, CTX_LEN), lambda i: (0, 0)),
            pl.BlockSpec((2 * FN_GC, FN_GC), lambda i: (0, 0)),
            pl.BlockSpec(memory_space=pl.ANY),
        ],
        out_specs=pl.BlockSpec((n_ctx // FFT_R, FFT_R * D_MODEL), lambda i: (ctx_blk, 0)),
        out_shape=jax.ShapeDtypeStruct(yf.shape, BF16),
        input_output_aliases={3: 0},
        compiler_params=_cparams(("arbitrary",)),
        name="fourier_ctx",
    )(fn_view, actx, csm, yf)


def _fourier(fn, tables, need_ctx, n_batch):
    a1, b2, csm, actx = tables
    R = FFT_R
    n_rows = fn.shape[1] * R
    p = _fft1(fn, a1, n_batch)
    yf = _fft2(p.reshape(FN_GROUPS, n_batch, 2, R, R, FN_GC), b2, csm, n_rows, n_batch)
    if need_ctx:
        yf = _fft_ctx(fn, actx, csm, yf, n_batch)
    return yf


def _merge_kernel(hf_ref, hb_ref, o_ref, gm_ref, gd_ref, gf_ref, yd_ref, yf_ref, xl_ref, xc_ref, mod_ref, hg_ref,
                  wml_ref, wda_ref, wfn_ref, wout_ref, out_ref, *, rows_per_batch, n_batch):
    i = pl.program_id(0)
    x = jnp.where(i < rows_per_batch * n_batch, xl_ref[...], xc_ref[...])
    hsum = hf_ref[...].astype(F32) + hb_ref[...].astype(F32)
    hg = hg_ref[...]
    parts = []
    for h in range(ML_HEADS):
        hs = slice(h * ML_DK, (h + 1) * ML_DK)
        parts.append(_rms(hsum[:, hs], hg[:, hs]))
    ym = (jnp.concatenate(parts, axis=1) * jax.nn.sigmoid(o_ref[...].astype(F32))).astype(BF16)
    yf = yf_ref[...].astype(F32).reshape(x.shape).astype(BF16)
    y = (jax.nn.sigmoid(gm_ref[...].astype(F32)) * _dot(ym, wml_ref[...])
         + jax.nn.sigmoid(gd_ref[...].astype(F32)) * _dot(yd_ref[...], wda_ref[...])
         + jax.nn.sigmoid(gf_ref[...].astype(F32)) * _dot(yf, wfn_ref[...]))
    gate = _mod_row(mod_ref, i, rows_per_batch, n_batch, 2)
    out_ref[...] = x + gate * _dot(y.astype(BF16), wout_ref[...])


def _merge(hf, hb, u, yd, yf, x_lat, x_ctx, ctx_blk0, mods, head_g, wml, wda, wfn, wout, layer, need_ctx, n_batch):
    n_rows = n_batch * (SEQ + CTX_LEN)
    tm = _row_tile(n_batch)
    rows_per_batch = SEQ // tm
    lat_blocks = n_batch * rows_per_batch
    ni = (n_rows if need_ctx else n_batch * SEQ) // tm
    kern = functools.partial(_merge_kernel, rows_per_batch=rows_per_batch, n_batch=n_batch)
    row = lambda i: (i, 0)
    full = lambda i: (0, 0)
    wspec = pl.BlockSpec((None, D_MODEL, D_MODEL), lambda i: (layer, 0, 0))
    return pl.pallas_call(
        kern,
        grid=(ni,),
        in_specs=[
            pl.BlockSpec((tm, D_MODEL), row),
            pl.BlockSpec((tm, D_MODEL), row),
            pl.BlockSpec((tm, D_MODEL), lambda i: (i, U_OML)),
            pl.BlockSpec((tm, D_MODEL), lambda i: (i, U_GPRE)),
            pl.BlockSpec((tm, D_MODEL), lambda i: (i, U_GPRE + 1)),
            pl.BlockSpec((tm, D_MODEL), lambda i: (i, U_GPRE + 2)),
            pl.BlockSpec((tm, D_MODEL), row),
            pl.BlockSpec((tm // FFT_R, FFT_R * D_MODEL), row),
            pl.BlockSpec((tm, D_MODEL), lambda i: (jnp.minimum(i, lat_blocks - 1), 0)),
            pl.BlockSpec((tm, D_MODEL), lambda i: (ctx_blk0 + jnp.maximum(i - lat_blocks, 0), 0)),
            pl.BlockSpec((None, 8, N_MOD), lambda i: (layer, 0, 0)),
            pl.BlockSpec((1, D_MODEL), full),
            wspec, wspec, wspec, wspec,
        ],
        out_specs=pl.BlockSpec((tm, D_MODEL), row),
        out_shape=jax.ShapeDtypeStruct((ni * tm, D_MODEL), F32),
        compiler_params=_cparams(("arbitrary",)),
        name="merge_out_proj",
    )(hf, hb, u, u, u, u, yd, yf, x_lat, x_ctx, mods, head_g, wml, wda, wfn, wout)


FFN_CHUNKS = ((0, 1024), (1024, 1024), (2048, 768))


def _ffn_kernel(x_ref, mod_ref, g_ref, win_ref, wout_ref, fg_ref, out_ref, *, rows_per_batch, n_batch, final):
    i = pl.program_id(0)
    x = x_ref[...]
    shift = _mod_row(mod_ref, i, rows_per_batch, n_batch, 3)
    scale = _mod_row(mod_ref, i, rows_per_batch, n_batch, 4)
    gate = _mod_row(mod_ref, i, rows_per_batch, n_batch, 5)
    h = (_rms(x, g_ref[...]) * (1.0 + scale) + shift).astype(BF16)
    acc = None
    for lo, width in FFN_CHUNKS:
        a = _dot(h, win_ref[:, lo:lo + width])
        b = _dot(h, win_ref[:, D_FF + lo:D_FF + lo + width])
        act = (a * jax.nn.sigmoid(a) * b).astype(BF16)
        part = _dot(act, wout_ref[lo:lo + width, :])
        acc = part if acc is None else acc + part
    xn = x + gate * acc
    out_ref[...] = _rms(xn, fg_ref[...]) if final else xn


def _ffn(x, mods, g, w_in, w_out, layer, final_g, final, n_rows_out, n_batch):
    tm = n_batch * CTX_LEN
    rows_per_batch = SEQ // tm
    resident = pl.Buffered(1)
    kern = functools.partial(_ffn_kernel, rows_per_batch=rows_per_batch, n_batch=n_batch, final=final)
    row = lambda i: (i, 0)
    full = lambda i: (0, 0)
    return pl.pallas_call(
        kern,
        grid=(n_rows_out // tm,),
        in_specs=[
            pl.BlockSpec((tm, D_MODEL), row),
            pl.BlockSpec((None, 8, N_MOD), lambda i: (layer, 0, 0)),
            pl.BlockSpec((1, D_MODEL), full),
            pl.BlockSpec((None, D_MODEL, 2 * D_FF), lambda i: (layer, 0, 0), pipeline_mode=resident),
            pl.BlockSpec((None, D_FF, D_MODEL), lambda i: (layer, 0, 0), pipeline_mode=resident),
            pl.BlockSpec((1, D_MODEL), full),
        ],
        out_specs=pl.BlockSpec((tm, D_MODEL), row),
        out_shape=jax.ShapeDtypeStruct((n_rows_out, D_MODEL), F32),
        compiler_params=_cparams(("arbitrary",)),
        name="swiglu_ffn",
    )(x, mods, g, w_in, w_out, final_g)


def _da_col_perm(w):
    half = DA_DH // 2
    lead = w.shape[:-1]
    return jnp.swapaxes(w.reshape(lead + (DA_HEADS, 2, 2, half)), -3, -2).reshape(lead + (DA_HEADS * DA_DV,))


def _rope_tables(pad):
    n_freq = DA_DH // 4
    rows = SEQ // GRID_W
    inv = ROPE_BASE ** (-jnp.arange(n_freq, dtype=F32) / n_freq)
    r = jnp.repeat(jnp.arange(rows, dtype=F32), GRID_W)
    col = jnp.tile(jnp.arange(GRID_W, dtype=F32), rows)
    ang = jnp.concatenate([r[:, None] * inv, col[:, None] * inv], axis=-1)
    cos, sin = jnp.cos(ang), jnp.sin(ang)
    cos_t = jnp.concatenate([cos, cos, cos, cos], axis=-1)
    sin_t = jnp.concatenate([-sin, -sin, sin, sin], axis=-1)
    cos_t = jnp.concatenate([cos_t, jnp.ones((pad, DA_DV), F32)], axis=0)
    sin_t = jnp.concatenate([sin_t, jnp.zeros((pad, DA_DV), F32)], axis=0)
    return cos_t, sin_t


def kernel(x, c, ctx, c_ctx, w_ada, b_ada, norm_g, w_in, ml_gate_b, ml_head_g, da_lam, da_head_g,
           w_br_ml, w_br_da, w_br_fn, w_out, w_ffn_in, w_ffn_out, final_g):
    n_batch = x.shape[0]
    n_lat = n_batch * SEQ
    x_lat = x.reshape(n_lat, D_MODEL)
    x_ctx = ctx.reshape(n_batch * CTX_LEN, D_MODEL)
    cc = jnp.concatenate([c, c_ctx[None, :], jnp.zeros((8 - n_batch - 1, D_MODEL), F32)], axis=0)
    mods = _mods(cc, w_ada, b_ada)
    cos_t, sin_t = _rope_tables(n_batch * CTX_LEN)
    tables = _dft_tables()
    final_g2 = final_g.reshape(1, D_MODEL)
    wb_ml, wb_da, wb_fn, wb_out = _to_bf16(w_br_ml), _to_bf16(w_br_da), _to_bf16(w_br_fn), _to_bf16(w_out)
    wb_ffn_in, wb_ffn_out = _to_bf16(w_ffn_in), _to_bf16(w_ffn_out)

    gate_lo = 4 * D_MODEL
    da_lo = gate_lo + N_GATE
    fn_lo = da_lo + 3 * D_MODEL
    w_main = jnp.concatenate([w_in[..., :D_MODEL], w_in[..., 2 * D_MODEL:gate_lo],
                              _da_col_perm(w_in[..., da_lo:da_lo + D_MODEL]),
                              _da_col_perm(w_in[..., da_lo + D_MODEL:da_lo + 2 * D_MODEL]),
                              w_in[..., da_lo + 2 * D_MODEL:fn_lo],
                              w_in[..., fn_lo + D_MODEL:], w_in[..., fn_lo:fn_lo + D_MODEL]], axis=-1).astype(BF16)
    w_kt = jnp.swapaxes(w_in[..., D_MODEL:2 * D_MODEL], 1, 2).astype(BF16)
    w_gate_t = jnp.swapaxes(w_in[..., gate_lo:da_lo], 1, 2).astype(BF16)

    tm_in = n_batch * CTX_LEN
    tm_tok = _row_tile(n_batch)
    xs = None
    for l in range(DEPTH):
        need_ctx = l < DEPTH - 1
        lam_init = 0.8 - 0.6 * math.exp(-0.3 * l)
        xl, xc = (x_lat, x_ctx) if xs is None else (xs, xs)
        u, kt, fn, gates_t = _inproj(xl, xc, 0 if xs is None else n_lat // tm_in, mods,
                                     norm_g[l, 0].reshape(1, D_MODEL), w_main, w_kt, w_gate_t, l,
                                     cos_t, sin_t, n_batch)
        hf, hb = _mlstm(u, kt, gates_t, ml_gate_b[l], n_batch)
        yd = _attn(u, da_lam[l], da_head_g[l].reshape(1, D_MODEL), lam_init, need_ctx, n_batch)
        yf = _fourier(fn, tables, need_ctx, n_batch)
        xs = _merge(hf, hb, u, yd, yf, xl, xc, 0 if xs is None else n_lat // tm_tok, mods,
                    ml_head_g[l].reshape(1, D_MODEL), wb_ml, wb_da, wb_fn, wb_out, l, need_ctx, n_batch)
        final = l == DEPTH - 1
        n_out = n_lat if final else xs.shape[0]
        xs = _ffn(xs, mods, norm_g[l, 1].reshape(1, D_MODEL), wb_ffn_in, wb_ffn_out, l,
                  final_g2, final, n_out, n_batch)
    return xs.reshape(n_batch, SEQ, D_MODEL)
```

```python
import functools
import math

import jax
import jax.numpy as jnp
from jax import lax
from jax.experimental import pallas as pl
from jax.experimental.pallas import tpu as pltpu

D_MODEL = 1024
SEQ = 4096
DEPTH = 4
CTX_LEN = 256
GRID_W = 64
NORM_EPS = 1e-6

ML_HEADS = 4
ML_DK = 256
ML_CHUNK = 256

DA_HEADS = 8
DA_DH = 64
DA_DV = 2 * DA_DH
ROPE_BASE = 10000.0
ATTN_TQ = 512
ATTN_ROW_CHUNKS = 2

FN_GROUPS = 4
FN_GC = 256
FFT_R = 64

D_FF = 2816
N_GATE = 4 * ML_HEADS
N_MOD = 6 * D_MODEL

U_QML, U_VML, U_OML, U_QDA, U_KDA, U_VDA, U_GPRE = 0, 1, 2, 3, 4, 5, 6
U_BLOCKS = 9
W_BLOCKS = U_BLOCKS + 1
STEP_KT = 1
N_COL_STEPS = W_BLOCKS // 2

VMEM_LIMIT_V7X = 56 * 1024 * 1024
LANES_V7X = 128

BF16 = jnp.bfloat16
F32 = jnp.float32


def _cparams(sem):
    return pltpu.CompilerParams(dimension_semantics=sem, vmem_limit_bytes=VMEM_LIMIT_V7X)


def _dot(a, b):
    return jnp.dot(a, b, preferred_element_type=F32)


def _dot_nt(a, b):
    return lax.dot_general(a, b, (((1,), (1,)), ((), ())), preferred_element_type=F32)


def _mod_row(mod_ref, i, rows_per_batch, n_batch, col):
    r = jnp.minimum(i // rows_per_batch, n_batch)
    return mod_ref[pl.ds(r, 1), col * D_MODEL:(col + 1) * D_MODEL]


def _row_tile(n_batch):
    return min(512, n_batch * CTX_LEN)


def _rms(x, g):
    return x * lax.rsqrt(jnp.mean(x * x, axis=-1, keepdims=True) + NORM_EPS) * g


def _cast_kernel(w_ref, o_ref):
    o_ref[...] = w_ref[...].astype(o_ref.dtype)


def _to_bf16(w):
    n_l, rows, cols = w.shape
    tr = 256
    return pl.pallas_call(
        _cast_kernel,
        grid=(n_l, rows // tr),
        in_specs=[pl.BlockSpec((1, tr, cols), lambda l, i: (l, i, 0))],
        out_specs=pl.BlockSpec((1, tr, cols), lambda l, i: (l, i, 0)),
        out_shape=jax.ShapeDtypeStruct(w.shape, BF16),
        compiler_params=_cparams(("arbitrary", "arbitrary")),
        name="cast_bf16",
    )(w)


def _mods_kernel(c_ref, w_ref, b_ref, o_ref):
    c = c_ref[...]
    s = (c * jax.nn.sigmoid(c)).astype(BF16)
    o_ref[0] = _dot(s, w_ref[0].astype(BF16)) + b_ref[0]


def _mods(cc, w_ada, b_ada):
    tn = 1536
    return pl.pallas_call(
        _mods_kernel,
        grid=(DEPTH, N_MOD // tn),
        in_specs=[
            pl.BlockSpec((8, D_MODEL), lambda l, j: (0, 0)),
            pl.BlockSpec((1, D_MODEL, tn), lambda l, j: (l, 0, j)),
            pl.BlockSpec((1, 1, tn), lambda l, j: (l, 0, j)),
        ],
        out_specs=pl.BlockSpec((1, 8, tn), lambda l, j: (l, 0, j)),
        out_shape=jax.ShapeDtypeStruct((DEPTH, 8, N_MOD), F32),
        compiler_params=_cparams(("arbitrary", "arbitrary")),
        name="adaln_mods",
    )(cc, w_ada, b_ada.reshape(DEPTH, 1, N_MOD))


def _inproj_kernel(xl_ref, xc_ref, mod_ref, g_ref, w_ref, wkt_ref, wgt_ref, cos_ref, sin_ref,
                   u_ref, kt_ref, fn_ref, gate_ref, xn_ref, *, rows_per_batch, n_batch):
    i = pl.program_id(0)
    j = pl.program_id(1)

    def normalise(x_ref):
        y = _rms(x_ref[...], g_ref[...])
        shift = _mod_row(mod_ref, i, rows_per_batch, n_batch, 0)
        scale = _mod_row(mod_ref, i, rows_per_batch, n_batch, 1)
        xn_ref[...] = (y * (1.0 + scale) + shift).astype(BF16)
        gate_ref[...] = _dot_nt(wgt_ref[...], xn_ref[...])

    is_lat = i < rows_per_batch * n_batch

    @pl.when(jnp.logical_and(j == 0, is_lat))
    def _():
        normalise(xl_ref)

    @pl.when(jnp.logical_and(j == 0, jnp.logical_not(is_lat)))
    def _():
        normalise(xc_ref)

    def product(h):
        return _dot(xn_ref[...], w_ref[:, h * D_MODEL:(h + 1) * D_MODEL])

    def plain(h):
        u_ref[:, h * D_MODEL:(h + 1) * D_MODEL] = product(h).astype(BF16)

    def rope(h):
        acc = product(h)
        cos = cos_ref[...]
        sin = sin_ref[...]
        for t in range(acc.shape[1] // DA_DV):
            x = acc[:, t * DA_DV:(t + 1) * DA_DV]
            lo = h * D_MODEL + t * DA_DV
            u_ref[:, lo:lo + DA_DV] = (x * cos + pltpu.roll(x, DA_DV // 2, 1) * sin).astype(BF16)

    def fourier_in(h):
        acc = product(h)
        for g in range(FN_GROUPS):
            z = acc[:, g * FN_GC:(g + 1) * FN_GC]
            fn_ref[g] = z.reshape(z.shape[0] // FFT_R, FFT_R * FN_GC).astype(BF16)

    def keys_t():
        kt_ref[...] = (_dot_nt(wkt_ref[...], xn_ref[...]) * (ML_DK ** -0.5)).astype(BF16)

    epilogues = {U_QML: plain, U_VML: plain, U_OML: plain, U_QDA: rope, U_KDA: rope, U_VDA: plain,
                 U_GPRE: plain, U_GPRE + 1: plain, U_GPRE + 2: plain, U_BLOCKS: fourier_in}
    for step in range(N_COL_STEPS):
        @pl.when(j == step)
        def _(step=step):
            for h in range(2):
                epilogues[2 * step + h](h)
            if step == STEP_KT:
                keys_t()


def _inproj(x_lat, x_ctx, ctx_blk, mods, g, w_main, w_kt, w_gate_t, layer, cos_t, sin_t, n_batch):
    tm = n_batch * CTX_LEN
    assert SEQ % tm == 0 and cos_t.shape[0] == SEQ + tm
    n_rows = n_batch * (SEQ + CTX_LEN)
    ni = n_rows // tm
    rows_per_batch = SEQ // tm
    lat_blocks = n_batch * rows_per_batch

    def tab_idx(i, j):
        return (jnp.where(i < lat_blocks, i % rows_per_batch, rows_per_batch), 0)

    kern = functools.partial(_inproj_kernel, rows_per_batch=rows_per_batch, n_batch=n_batch)
    return pl.pallas_call(
        kern,
        grid=(ni, N_COL_STEPS),
        in_specs=[
            pl.BlockSpec((tm, D_MODEL), lambda i, j: (jnp.minimum(i, lat_blocks - 1), 0)),
            pl.BlockSpec((tm, D_MODEL), lambda i, j: (ctx_blk, 0)),
            pl.BlockSpec((None, 8, N_MOD), lambda i, j: (layer, 0, 0)),
            pl.BlockSpec((1, D_MODEL), lambda i, j: (0, 0)),
            pl.BlockSpec((None, D_MODEL, 2 * D_MODEL), lambda i, j: (layer, 0, j)),
            pl.BlockSpec((None, D_MODEL, D_MODEL), lambda i, j: (layer, 0, 0)),
            pl.BlockSpec((None, N_GATE, D_MODEL), lambda i, j: (layer, 0, 0)),
            pl.BlockSpec((tm, DA_DV), tab_idx),
            pl.BlockSpec((tm, DA_DV), tab_idx),
        ],
        out_specs=[
            pl.BlockSpec((tm, 2 * D_MODEL), lambda i, j: (i, j)),
            pl.BlockSpec((D_MODEL, tm), lambda i, j: (0, i)),
            pl.BlockSpec((FN_GROUPS, tm // FFT_R, FFT_R * FN_GC), lambda i, j: (0, i, 0)),
            pl.BlockSpec((N_GATE, tm), lambda i, j: (0, i)),
        ],
        out_shape=[
            jax.ShapeDtypeStruct((n_rows, U_BLOCKS * D_MODEL), BF16),
            jax.ShapeDtypeStruct((D_MODEL, n_rows), BF16),
            jax.ShapeDtypeStruct((FN_GROUPS, n_rows // FFT_R, FFT_R * FN_GC), BF16),
            jax.ShapeDtypeStruct((N_GATE, n_rows), F32),
        ],
        scratch_shapes=[pltpu.VMEM((tm, D_MODEL), BF16)],
        compiler_params=_cparams(("arbitrary", "arbitrary")),
        name="in_proj",
    )(x_lat, x_ctx, mods, g, w_main, w_kt, w_gate_t, cos_t, sin_t)


def _split3(x):
    hi = x.astype(BF16).astype(F32)
    mid = (x - hi).astype(BF16).astype(F32)
    lo = (x - hi - mid).astype(BF16).astype(F32)
    return hi, mid, lo


def _mlstm_kernel(qf_ref, ktf_ref, vf_ref, gtf_ref, gtfn_ref, qb_ref, ktb_ref, vb_ref, gtb_ref, gtbn_ref, bias_ref,
                  hf_ref, hb_ref, *scratch):
    n_st = 2 * ML_HEADS
    cx_refs, bw_refs, pmw_refs = scratch[:n_st], scratch[n_st:2 * n_st], scratch[2 * n_st:3 * n_st]
    m_refs, c_refs, bend_refs, g_refs = (scratch[3 * n_st + 2 * k:3 * n_st + 2 * k + 2] for k in range(4))
    state_refs = cx_refs + m_refs
    s = pl.program_id(1)
    L = ML_CHUNK
    H = ML_HEADS
    W = LANES_V7X

    @pl.when(s == 0)
    def _():
        for ref in state_refs:
            ref[...] = jnp.zeros_like(ref)

    t_idx = lax.broadcasted_iota(jnp.int32, (L, L), 0)
    s_idx = lax.broadcasted_iota(jnp.int32, (L, L), 1)
    eye = t_idx == s_idx
    sub8 = lax.broadcasted_iota(jnp.int32, (8, W), 0)
    ones_w = jnp.ones((L, W), BF16)
    er = lax.broadcasted_iota(jnp.int32, (4 * L, 2 * W), 0)
    ec = lax.broadcasted_iota(jnp.int32, (4 * L, 2 * W), 1)
    expand = jnp.where((er < 3 * L) == (ec < W), 1.0, 0.0).astype(BF16)

    def running_max_rows(x, d):
        n_tiles = L // 8
        out = [None] * n_tiles
        carry = None
        for j in (range(n_tiles) if d == 0 else range(n_tiles - 1, -1, -1)):
            r = x[8 * j:8 * (j + 1)]
            k = 1
            while k < 8:
                if d == 0:
                    r = jnp.maximum(r, jnp.where(sub8 >= k, pltpu.roll(r, k, 0), -jnp.inf))
                else:
                    r = jnp.maximum(r, jnp.where(sub8 < 8 - k, pltpu.roll(r, 8 - k, 0), -jnp.inf))
                k *= 2
            if carry is not None:
                r = jnp.maximum(r, carry)
            carry = jnp.broadcast_to(r[7:8] if d == 0 else r[0:1], (8, W))
            out[j] = r
        return jnp.concatenate(out, axis=0)

    def gate_part(d, gt_ref):
        before = (t_idx <= s_idx) if d == 0 else (t_idx >= s_idx)
        gt = gt_ref[2 * H * d:2 * H * (d + 1), :] + bias_ref[2 * H * d:2 * H * (d + 1), :]
        i4, f4 = gt[0:H], gt[H:2 * H]
        lf4 = jnp.minimum(f4, 0.0) - jnp.log1p(jnp.exp(-jnp.abs(f4)))
        lf_terms = jnp.concatenate(list(_split3(lf4)) + [jnp.zeros((H, L), F32)], axis=0).astype(BF16)
        cum_rhs = jnp.concatenate([jnp.where(before, 1.0, 0.0).astype(BF16), ones_w], axis=1)
        r = _dot(lf_terms, cum_rhs)
        bx = r[0:H] + r[H:2 * H] + r[2 * H:3 * H]
        b4, bend4 = bx[:, 0:L], bx[:, L:L + W]
        c4 = i4 - b4
        c_refs[d][...] = c4
        bend_refs[d][...] = bend4
        g_refs[d][...] = bend4[:, 0:1] - b4 + i4
        col_terms = _split3(b4) + (c4.astype(BF16).astype(F32),)
        for h in range(H):
            diag = jnp.concatenate([jnp.where(eye, x[h:h + 1, :], 0.0) for x in col_terms], axis=1).astype(BF16)
            wide = _dot(diag, expand)
            bw_refs[d * H + h][...] = wide[:, 0:W]
            pmw_refs[d * H + h][...] = running_max_rows(wide[:, W:2 * W], d)

    dirs = ((0, qf_ref, ktf_ref, vf_ref, gtf_ref, gtfn_ref, hf_ref),
            (1, qb_ref, ktb_ref, vb_ref, gtb_ref, gtbn_ref, hb_ref))

    @pl.when(s == 0)
    def _():
        for d, _, _, _, gt_ref, _, _ in dirs:
            gate_part(d, gt_ref)

    for d, q_ref, kt_ref, v_ref, _, _, h_ref in dirs:
        causal = (s_idx <= t_idx) if d == 0 else (s_idx >= t_idx)
        c4, bend4, g4 = c_refs[d][...], bend_refs[d][...], g_refs[d][...]
        m_prev4 = m_refs[d][...]
        m_new4 = jnp.maximum(bend4 + m_prev4, jnp.max(g4, axis=1, keepdims=True))
        m_refs[d][...] = m_new4
        decay4 = jnp.exp(bend4 + m_prev4 - m_new4)
        w4 = jnp.exp(g4 - m_new4[:, 0:1])
        for h in range(H):
            st = d * H + h
            hs = slice(h * ML_DK, (h + 1) * ML_DK)
            b_w = bw_refs[st][...]
            m_w = jnp.maximum(pmw_refs[st][...], m_prev4[h:h + 1, :])

            q = q_ref[:, hs]
            kt = kt_ref[hs, :]
            vx = jnp.concatenate([v_ref[:, hs], ones_w], axis=1)
            m_ll = jnp.concatenate([m_w] * (L // W), axis=1)
            a = (jnp.where(causal, jnp.exp(c4[h:h + 1, :] - m_ll), 0.0) * _dot(q, kt)).astype(BF16)
            cx_prev = cx_refs[st][...]
            qc = _dot(q, cx_prev.astype(BF16))
            av = _dot(a, vx)
            sc_w = jnp.exp(m_prev4[h:h + 1, :] - m_w)
            den = sc_w * qc[:, ML_DK:] + av[:, ML_DK:]
            inv = 1.0 / jnp.maximum(jnp.abs(den), jnp.exp(-(b_w + m_w)))
            for t in range(ML_DK // W):
                ts = slice(t * W, (t + 1) * W)
                h_ref[:, h * ML_DK + t * W:h * ML_DK + (t + 1) * W] = (
                    (sc_w * qc[:, ts] + av[:, ts]) * inv).astype(h_ref.dtype)

            kw = (kt.astype(F32) * w4[h:h + 1, :]).astype(BF16)
            dec = jnp.concatenate([decay4[h:h + 1, :]] * (ML_DK // W + 1), axis=1)
            cx_refs[st][...] = dec * cx_prev + _dot(kw, vx)

    for d, _, _, _, _, gtn_ref, _ in dirs:
        gate_part(d, gtn_ref)


def _mlstm(u, kt, gates_t, gate_b, n_batch):
    n_rows = u.shape[0]
    L = ML_CHUNK
    lat_chunks = SEQ // L
    ctx_chunks = CTX_LEN // L
    n_steps = ctx_chunks + lat_chunks
    ctx_base = n_batch * lat_chunks

    def rowblk(d):
        def f(b, s):
            in_ctx = s < ctx_chunks
            if d == 0:
                c = jnp.where(in_ctx, s, s - ctx_chunks)
            else:
                c = jnp.where(in_ctx, ctx_chunks - 1 - s, lat_chunks - 1 - (s - ctx_chunks))
            return jnp.where(in_ctx, ctx_base + ctx_chunks * b, lat_chunks * b) + c
        return f

    def dir_specs(d):
        rb = rowblk(d)
        return [
            pl.BlockSpec((L, D_MODEL), lambda b, s: (rb(b, s), U_QML)),
            pl.BlockSpec((D_MODEL, L), lambda b, s: (0, rb(b, s))),
            pl.BlockSpec((L, D_MODEL), lambda b, s: (rb(b, s), U_VML)),
            pl.BlockSpec((N_GATE, L), lambda b, s: (0, rb(b, s))),
            pl.BlockSpec((N_GATE, L), lambda b, s: (0, rb(b, jnp.minimum(s + 1, n_steps - 1)))),
        ]

    def out_spec(d):
        rb = rowblk(d)
        return pl.BlockSpec((L, D_MODEL), lambda b, s: (rb(b, s), 0))

    n_st = 2 * ML_HEADS
    return pl.pallas_call(
        _mlstm_kernel,
        grid=(n_batch, n_steps),
        in_specs=dir_specs(0) + dir_specs(1) + [pl.BlockSpec((N_GATE, 1), lambda b, s: (0, 0))],
        out_specs=[out_spec(0), out_spec(1)],
        out_shape=[jax.ShapeDtypeStruct((n_rows, D_MODEL), BF16)] * 2,
        scratch_shapes=(
            [pltpu.VMEM((ML_DK, ML_DK + LANES_V7X), F32)] * n_st
            + [pltpu.VMEM((L, LANES_V7X), F32)] * (2 * n_st)
            + [pltpu.VMEM((ML_HEADS, LANES_V7X), F32)] * 2
            + [pltpu.VMEM((ML_HEADS, L), F32)] * 2
            + [pltpu.VMEM((ML_HEADS, LANES_V7X), F32)] * 2
            + [pltpu.VMEM((ML_HEADS, L), F32)] * 2),
        compiler_params=_cparams(("arbitrary", "arbitrary")),
        name="mlstm_scan",
    )(u, kt, u, gates_t, gates_t, u, kt, u, gates_t, gates_t, gate_b.reshape(N_GATE, 1))


def _attn_lambda(lam_ref, lam_init):
    lq = lam_ref[...]
    return (jnp.exp(jnp.sum(lq[0:1] * lq[1:2], axis=1, keepdims=True))
            - jnp.exp(jnp.sum(lq[2:3] * lq[3:4], axis=1, keepdims=True)) + lam_init)


def _attn_queries(q_ref):
    q = q_ref[...]
    lane = lax.broadcasted_iota(jnp.int32, (1, DA_DV), 1)
    zero = jnp.zeros_like(q)
    is_map0 = (lane % DA_DH) < (DA_DH // 2)
    q2 = jnp.concatenate([jnp.where(is_map0, q, zero), jnp.where(is_map0, zero, q)], axis=0)
    return q2 * (DA_DH ** -0.5)


def _attn_sums(s_chunks, m_chunks, vx):
    return jnp.concatenate([_dot(jnp.exp(s - m).astype(BF16), vx) for s, m in zip(s_chunks, m_chunks)], axis=0)


def _attn_finish(acc, lam, g, lam_init, tq):
    o0 = acc[0:tq, 0:DA_DV] * (1.0 / acc[0:tq, DA_DV:DA_DV + 1])
    o1 = acc[tq:, 0:DA_DV] * (1.0 / acc[tq:, DA_DV:DA_DV + 1])
    return _rms(o0 - lam * o1, g) * (1.0 - lam_init)


def _attn_kernel(qa_ref, kla_ref, kca_ref, qb_ref, klb_ref, kcb_ref, vl_ref, vc_ref, lam_ref, g_ref, o_ref,
                 vx_ref, s_ref, m_ref, *, lam_init, q_blocks):
    t = pl.program_id(0)
    tq = qa_ref.shape[0]
    rows = 2 * tq // ATTN_ROW_CHUNKS

    @pl.when(jnp.logical_and(t > 0, (2 * t - 2) % q_blocks == 0))
    def _():
        vx_ref[0:SEQ, 0:DA_DV] = vl_ref[...]
        vx_ref[SEQ:, 0:DA_DV] = vc_ref[...]
        lane_v = lax.broadcasted_iota(jnp.int32, (SEQ + CTX_LEN, DA_DV), 1)
        vx_ref[:, DA_DV:] = jnp.where(lane_v == 0, 1.0, 0.0).astype(BF16)

    def score(slot, q_ref, kl_ref, kc_ref):
        q2 = _attn_queries(q_ref)
        s_lat = _dot_nt(q2, kl_ref[...])
        s_ctx = _dot_nt(q2, kc_ref[...])
        m_ref[slot] = jnp.maximum(jnp.max(s_lat, axis=1, keepdims=True), jnp.max(s_ctx, axis=1, keepdims=True))
        s_ref[slot, :, 0:SEQ] = s_lat
        s_ref[slot, :, SEQ:] = s_ctx

    def drain(slot):
        s_chunks = [s_ref[slot, c * rows:(c + 1) * rows, :] for c in range(ATTN_ROW_CHUNKS)]
        m_chunks = [m_ref[slot, c * rows:(c + 1) * rows, :] for c in range(ATTN_ROW_CHUNKS)]
        acc = _attn_sums(s_chunks, m_chunks, vx_ref[...])
        o = _attn_finish(acc, _attn_lambda(lam_ref, lam_init), g_ref[...], lam_init, tq)
        o_ref[slot * tq:(slot + 1) * tq, :] = o.astype(o_ref.dtype)

    @pl.when(t > 0)
    def _():
        drain(0)
        score(1, qa_ref, kla_ref, kca_ref)

    @pl.when(t == 0)
    def _():
        score(0, qb_ref, klb_ref, kcb_ref)

    @pl.when(t > 0)
    def _():
        drain(1)
        score(0, qb_ref, klb_ref, kcb_ref)


def _attn_ctx_kernel(q_ref, kc_ref, vc_ref, lam_ref, g_ref, yin_ref, o_ref, *, lam_init):
    del yin_ref
    tq = q_ref.shape[0]
    s = _dot_nt(_attn_queries(q_ref), kc_ref[...])
    lane_v = lax.broadcasted_iota(jnp.int32, (CTX_LEN, DA_DV), 1)
    vx = jnp.concatenate([vc_ref[...], jnp.where(lane_v == 0, 1.0, 0.0).astype(BF16)], axis=1)
    acc = _attn_sums([s], [jnp.max(s, axis=1, keepdims=True)], vx)
    o = _attn_finish(acc, _attn_lambda(lam_ref, lam_init), g_ref[...], lam_init, tq)
    o_ref[...] = o.astype(o_ref.dtype)


def _attn(u, da_lam, head_g, lam_init, need_ctx, n_batch):
    n_rows = u.shape[0]
    tq = ATTN_TQ
    lat_qblocks = SEQ // tq
    ctx_base = n_batch * (SEQ // CTX_LEN)
    cpb = D_MODEL // DA_DV
    n_keys = SEQ + CTX_LEN

    n_items = n_batch * DA_HEADS * lat_qblocks
    assert lat_qblocks % 2 == 0

    def item(k):
        k = jnp.clip(k, 0, n_items - 1)
        return k // (DA_HEADS * lat_qblocks), (k // lat_qblocks) % DA_HEADS, k % lat_qblocks

    def of_item(offset, f):
        return lambda t: f(*item(2 * t + offset))

    def scored_specs(offset):
        return [
            pl.BlockSpec((tq, DA_DV), of_item(offset, lambda b, h, qi: (b * lat_qblocks + qi, U_QDA * cpb + h))),
            pl.BlockSpec((SEQ, DA_DV), of_item(offset, lambda b, h, qi: (b, U_KDA * cpb + h))),
            pl.BlockSpec((CTX_LEN, DA_DV), of_item(offset, lambda b, h, qi: (ctx_base + b, U_KDA * cpb + h))),
        ]

    kern = functools.partial(_attn_kernel, lam_init=lam_init, q_blocks=lat_qblocks)
    yd = pl.pallas_call(
        kern,
        grid=(n_items // 2 + 1,),
        in_specs=scored_specs(-1) + scored_specs(0) + [
            pl.BlockSpec((SEQ, DA_DV), of_item(-2, lambda b, h, qi: (b, U_VDA * cpb + h))),
            pl.BlockSpec((CTX_LEN, DA_DV), of_item(-2, lambda b, h, qi: (ctx_base + b, U_VDA * cpb + h))),
            pl.BlockSpec((4, DA_DH), lambda t: (0, 0)),
            pl.BlockSpec((1, DA_DV), of_item(-2, lambda b, h, qi: (0, h))),
        ],
        out_specs=pl.BlockSpec((2 * tq, DA_DV),
                               of_item(-2, lambda b, h, qi: ((b * lat_qblocks + qi) // 2, h))),
        out_shape=jax.ShapeDtypeStruct((n_rows, D_MODEL), BF16),
        scratch_shapes=[
            pltpu.VMEM((n_keys, 2 * DA_DV), BF16),
            pltpu.VMEM((2, 2 * tq, n_keys), F32),
            pltpu.VMEM((2, 2 * tq, 1), F32),
        ],
        compiler_params=_cparams(("arbitrary",)),
        name="diff_attn",
    )(u, u, u, u, u, u, u, u, da_lam, head_g)
    if not need_ctx:
        return yd
    kern_ctx = functools.partial(_attn_ctx_kernel, lam_init=lam_init)
    return pl.pallas_call(
        kern_ctx,
        grid=(n_batch, DA_HEADS),
        in_specs=[
            pl.BlockSpec((CTX_LEN, DA_DV), lambda b, h: (ctx_base + b, U_QDA * cpb + h)),
            pl.BlockSpec((CTX_LEN, DA_DV), lambda b, h: (ctx_base + b, U_KDA * cpb + h)),
            pl.BlockSpec((CTX_LEN, DA_DV), lambda b, h: (ctx_base + b, U_VDA * cpb + h)),
            pl.BlockSpec((4, DA_DH), lambda b, h: (0, 0)),
            pl.BlockSpec((1, DA_DV), lambda b, h: (0, h)),
            pl.BlockSpec(memory_space=pl.ANY),
        ],
        out_specs=pl.BlockSpec((CTX_LEN, DA_DV), lambda b, h: (ctx_base + b, h)),
        out_shape=jax.ShapeDtypeStruct((n_rows, D_MODEL), BF16),
        input_output_aliases={5: 0},
        compiler_params=_cparams(("arbitrary", "arbitrary")),
        name="diff_attn_ctx",
    )(u, u, u, da_lam, head_g, yd)


def _dft_tables():
    R = FFT_R

    def cs(num, period):
        ang = (num % period).astype(F32) * (2.0 * math.pi / period)
        return jnp.cos(ang), jnp.sin(ang)

    idx = jnp.arange(R, dtype=jnp.int32)
    c1, s1 = cs(idx[:, None] * idx[None, :], R)
    a1 = jnp.concatenate([c1, -s1], axis=0).astype(BF16)
    f2 = idx[:, None, None]
    f1 = idx[None, :, None]
    t1 = idx[None, None, :]
    mc, ms = cs(t1 * (R * f1 + f2), SEQ)
    b2 = jnp.concatenate([jnp.concatenate([mc, ms], axis=2),
                          jnp.concatenate([-ms, mc], axis=2)], axis=1).astype(BF16)
    ch = jnp.arange(FN_GC, dtype=jnp.int32)
    cc, sc = cs(ch[:, None] * ch[None, :], FN_GC)
    csm = jnp.concatenate([cc, sc], axis=0).astype(BF16)
    actx = jnp.concatenate([cc, -sc], axis=0).astype(BF16)
    return a1, b2, csm, actx


def _fft1_kernel(a_ref, z_ref, p_ref):
    p_ref[0, 0] = _dot(a_ref[...], z_ref[0]).astype(p_ref.dtype)


def _fft1(fn_view, a1, n_batch):
    R = FFT_R
    lanes = R * FN_GC
    lc = lanes
    return pl.pallas_call(
        _fft1_kernel,
        grid=(FN_GROUPS, n_batch, lanes // lc),
        in_specs=[
            pl.BlockSpec((2 * R, R), lambda g, b, c: (0, 0)),
            pl.BlockSpec((1, R, lc), lambda g, b, c: (g, b, c)),
        ],
        out_specs=pl.BlockSpec((1, 1, 2 * R, lc), lambda g, b, c: (g, b, 0, c)),
        out_shape=jax.ShapeDtypeStruct((FN_GROUPS, n_batch, 2 * R, lanes), BF16),
        compiler_params=_cparams(("arbitrary", "arbitrary", "arbitrary")),
        name="fourier_stage1",
    )(a1, fn_view)


def _fft2_kernel(p_ref, b_ref, cs_ref, o_ref, *, f2b):
    R = FFT_R
    cc = cs_ref[0:FN_GC, :]
    sc = cs_ref[FN_GC:2 * FN_GC, :]
    norm = 1.0 / math.sqrt(SEQ * FN_GC)
    for g in range(FN_GROUPS):
        xr, xi = [], []
        for jj in range(f2b):
            stacked = jnp.concatenate([p_ref[g, 0, 0, jj], p_ref[g, 0, 1, jj]], axis=0)
            x = _dot(b_ref[jj], stacked)
            xr.append(x[0:R])
            xi.append(x[R:2 * R])
        y = (_dot(jnp.concatenate(xr, axis=0).astype(BF16), cc)
             + _dot(jnp.concatenate(xi, axis=0).astype(BF16), sc)) * norm
        for jj in range(f2b):
            lo = jj * D_MODEL + g * FN_GC
            o_ref[:, lo:lo + FN_GC] = y[jj * R:(jj + 1) * R].astype(o_ref.dtype)


def _fft2(p6, b2, csm, n_rows, n_batch):
    R = FFT_R
    f2b = 16
    kern = functools.partial(_fft2_kernel, f2b=f2b)
    return pl.pallas_call(
        kern,
        grid=(n_batch, R // f2b),
        in_specs=[
            pl.BlockSpec((FN_GROUPS, 1, 2, f2b, R, FN_GC), lambda b, f: (0, b, 0, f, 0, 0)),
            pl.BlockSpec((f2b, 2 * R, 2 * R), lambda b, f: (f, 0, 0)),
            pl.BlockSpec((2 * FN_GC, FN_GC), lambda b, f: (0, 0)),
        ],
        out_specs=pl.BlockSpec((R, f2b * D_MODEL), lambda b, f: (b, f)),
        out_shape=jax.ShapeDtypeStruct((n_rows // R, R * D_MODEL), BF16),
        compiler_params=_cparams(("arbitrary", "arbitrary")),
        name="fourier_stage2",
    )(p6, b2, csm)


def _fft_ctx_kernel(z_ref, a_ref, cs_ref, yin_ref, o_ref, *, n_batch):
    del yin_ref
    cc = cs_ref[0:FN_GC, :]
    sc = cs_ref[FN_GC:2 * FN_GC, :]
    norm = 1.0 / math.sqrt(CTX_LEN * FN_GC)
    n_ctx = n_batch * CTX_LEN
    groups = []
    for g in range(FN_GROUPS):
        z_all = z_ref[g].astype(F32).reshape(n_ctx, FN_GC)
        ys = []
        for b in range(n_batch):
            p = _dot(a_ref[...], z_all[b * CTX_LEN:(b + 1) * CTX_LEN].astype(BF16))
            ys.append(_dot(p[0:CTX_LEN].astype(BF16), cc) + _dot(p[CTX_LEN:2 * CTX_LEN].astype(BF16), sc))
        groups.append(jnp.concatenate(ys, axis=0))
    y = jnp.concatenate(groups, axis=1) * norm
    o_ref[...] = y.reshape(n_ctx // FFT_R, FFT_R * D_MODEL).astype(o_ref.dtype)


def _fft_ctx(fn_view, actx, csm, yf, n_batch):
    n_ctx = n_batch * CTX_LEN
    ctx_blk = n_batch * SEQ // n_ctx
    kern = functools.partial(_fft_ctx_kernel, n_batch=n_batch)
    return pl.pallas_call(
        kern,
        grid=(1,),
        in_specs=[
            pl.BlockSpec((FN_GROUPS, n_ctx // FFT_R, FFT_R * FN_GC), lambda i: (0, ctx_blk, 0)),
            pl.BlockSpec((2 * CTX_LEN, CTX_LEN), lambda i: (0, 0)),
            pl.BlockSpec((2 * FN_GC, FN_GC), lambda i: (0, 0)),
            pl.BlockSpec(memory_space=pl.ANY),
        ],
        out_specs=pl.BlockSpec((n_ctx // FFT_R, FFT_R * D_MODEL), lambda i: (ctx_blk, 0)),
        out_shape=jax.ShapeDtypeStruct(yf.shape, BF16),
        input_output_aliases={3: 0},
        compiler_params=_cparams(("arbitrary",)),
        name="fourier_ctx",
    )(fn_view, actx, csm, yf)


def _fourier(fn, tables, need_ctx, n_batch):
    a1, b2, csm, actx = tables
    R = FFT_R
    n_rows = fn.shape[1] * R
    p = _fft1(fn, a1, n_batch)
    yf = _fft2(p.reshape(FN_GROUPS, n_batch, 2, R, R, FN_GC), b2, csm, n_rows, n_batch)
    if need_ctx:
        yf = _fft_ctx(fn, actx, csm, yf, n_batch)
    return yf


def _merge_kernel(hf_ref, hb_ref, o_ref, gm_ref, gd_ref, gf_ref, yd_ref, yf_ref, xl_ref, xc_ref, mod_ref, hg_ref,
                  wml_ref, wda_ref, wfn_ref, wout_ref, out_ref, *, rows_per_batch, n_batch):
    i = pl.program_id(0)
    x = jnp.where(i < rows_per_batch * n_batch, xl_ref[...], xc_ref[...])
    hsum = hf_ref[...].astype(F32) + hb_ref[...].astype(F32)
    hg = hg_ref[...]
    parts = []
    for h in range(ML_HEADS):
        hs = slice(h * ML_DK, (h + 1) * ML_DK)
        parts.append(_rms(hsum[:, hs], hg[:, hs]))
    ym = (jnp.concatenate(parts, axis=1) * jax.nn.sigmoid(o_ref[...].astype(F32))).astype(BF16)
    yf = yf_ref[...].astype(F32).reshape(x.shape).astype(BF16)
    y = (jax.nn.sigmoid(gm_ref[...].astype(F32)) * _dot(ym, wml_ref[...])
         + jax.nn.sigmoid(gd_ref[...].astype(F32)) * _dot(yd_ref[...], wda_ref[...])
         + jax.nn.sigmoid(gf_ref[...].astype(F32)) * _dot(yf, wfn_ref[...]))
    gate = _mod_row(mod_ref, i, rows_per_batch, n_batch, 2)
    out_ref[...] = x + gate * _dot(y.astype(BF16), wout_ref[...])


def _merge(hf, hb, u, yd, yf, x_lat, x_ctx, ctx_blk0, mods, head_g, wml, wda, wfn, wout, layer, need_ctx, n_batch):
    n_rows = n_batch * (SEQ + CTX_LEN)
    tm = _row_tile(n_batch)
    rows_per_batch = SEQ // tm
    lat_blocks = n_batch * rows_per_batch
    ni = (n_rows if need_ctx else n_batch * SEQ) // tm
    kern = functools.partial(_merge_kernel, rows_per_batch=rows_per_batch, n_batch=n_batch)
    row = lambda i: (i, 0)
    full = lambda i: (0, 0)
    wspec = pl.BlockSpec((None, D_MODEL, D_MODEL), lambda i: (layer, 0, 0))
    return pl.pallas_call(
        kern,
        grid=(ni,),
        in_specs=[
            pl.BlockSpec((tm, D_MODEL), row),
            pl.BlockSpec((tm, D_MODEL), row),
            pl.BlockSpec((tm, D_MODEL), lambda i: (i, U_OML)),
            pl.BlockSpec((tm, D_MODEL), lambda i: (i, U_GPRE)),
            pl.BlockSpec((tm, D_MODEL), lambda i: (i, U_GPRE + 1)),
            pl.BlockSpec((tm, D_MODEL), lambda i: (i, U_GPRE + 2)),
            pl.BlockSpec((tm, D_MODEL), row),
            pl.BlockSpec((tm // FFT_R, FFT_R * D_MODEL), row),
            pl.BlockSpec((tm, D_MODEL), lambda i: (jnp.minimum(i, lat_blocks - 1), 0)),
            pl.BlockSpec((tm, D_MODEL), lambda i: (ctx_blk0 + jnp.maximum(i - lat_blocks, 0), 0)),
            pl.BlockSpec((None, 8, N_MOD), lambda i: (layer, 0, 0)),
            pl.BlockSpec((1, D_MODEL), full),
            wspec, wspec, wspec, wspec,
        ],
        out_specs=pl.BlockSpec((tm, D_MODEL), row),
        out_shape=jax.ShapeDtypeStruct((ni * tm, D_MODEL), F32),
        compiler_params=_cparams(("arbitrary",)),
        name="merge_out_proj",
    )(hf, hb, u, u, u, u, yd, yf, x_lat, x_ctx, mods, head_g, wml, wda, wfn, wout)


FFN_CHUNKS = ((0, 1024), (1024, 1024), (2048, 768))


def _ffn_kernel(x_ref, mod_ref, g_ref, win_ref, wout_ref, fg_ref, out_ref, *, rows_per_batch, n_batch, final):
    i = pl.program_id(0)
    x = x_ref[...]
    shift = _mod_row(mod_ref, i, rows_per_batch, n_batch, 3)
    scale = _mod_row(mod_ref, i, rows_per_batch, n_batch, 4)
    gate = _mod_row(mod_ref, i, rows_per_batch, n_batch, 5)
    h = (_rms(x, g_ref[...]) * (1.0 + scale) + shift).astype(BF16)
    acc = None
    for lo, width in FFN_CHUNKS:
        a = _dot(h, win_ref[:, lo:lo + width])
        b = _dot(h, win_ref[:, D_FF + lo:D_FF + lo + width])
        act = (a * jax.nn.sigmoid(a) * b).astype(BF16)
        part = _dot(act, wout_ref[lo:lo + width, :])
        acc = part if acc is None else acc + part
    xn = x + gate * acc
    out_ref[...] = _rms(xn, fg_ref[...]) if final else xn


def _ffn(x, mods, g, w_in, w_out, layer, final_g, final, n_rows_out, n_batch):
    tm = n_batch * CTX_LEN
    rows_per_batch = SEQ // tm
    resident = pl.Buffered(1)
    kern = functools.partial(_ffn_kernel, rows_per_batch=rows_per_batch, n_batch=n_batch, final=final)
    row = lambda i: (i, 0)
    full = lambda i: (0, 0)
    return pl.pallas_call(
        kern,
        grid=(n_rows_out // tm,),
        in_specs=[
            pl.BlockSpec((tm, D_MODEL), row),
            pl.BlockSpec((None, 8, N_MOD), lambda i: (layer, 0, 0)),
            pl.BlockSpec((1, D_MODEL), full),
            pl.BlockSpec((None, D_MODEL, 2 * D_FF), lambda i: (layer, 0, 0), pipeline_mode=resident),
            pl.BlockSpec((None, D_FF, D_MODEL), lambda i: (layer, 0, 0), pipeline_mode=resident),
            pl.BlockSpec((1, D_MODEL), full),
        ],
        out_specs=pl.BlockSpec((tm, D_MODEL), row),
        out_shape=jax.ShapeDtypeStruct((n_rows_out, D_MODEL), F32),
        compiler_params=_cparams(("arbitrary",)),
        name="swiglu_ffn",
    )(x, mods, g, w_in, w_out, final_g)


def _da_col_perm(w):
    half = DA_DH // 2
    lead = w.shape[:-1]
    return jnp.swapaxes(w.reshape(lead + (DA_HEADS, 2, 2, half)), -3, -2).reshape(lead + (DA_HEADS * DA_DV,))


def _rope_tables(pad):
    n_freq = DA_DH // 4
    rows = SEQ // GRID_W
    inv = ROPE_BASE ** (-jnp.arange(n_freq, dtype=F32) / n_freq)
    r = jnp.repeat(jnp.arange(rows, dtype=F32), GRID_W)
    col = jnp.tile(jnp.arange(GRID_W, dtype=F32), rows)
    ang = jnp.concatenate([r[:, None] * inv, col[:, None] * inv], axis=-1)
    cos, sin = jnp.cos(ang), jnp.sin(ang)
    cos_t = jnp.concatenate([cos, cos, cos, cos], axis=-1)
    sin_t = jnp.concatenate([-sin, -sin, sin, sin], axis=-1)
    cos_t = jnp.concatenate([cos_t, jnp.ones((pad, DA_DV), F32)], axis=0)
    sin_t = jnp.concatenate([sin_t, jnp.zeros((pad, DA_DV), F32)], axis=0)
    return cos_t, sin_t


def kernel(x, c, ctx, c_ctx, w_ada, b_ada, norm_g, w_in, ml_gate_b, ml_head_g, da_lam, da_head_g,
           w_br_ml, w_br_da, w_br_fn, w_out, w_ffn_in, w_ffn_out, final_g):
    n_batch = x.shape[0]
    n_lat = n_batch * SEQ
    x_lat = x.reshape(n_lat, D_MODEL)
    x_ctx = ctx.reshape(n_batch * CTX_LEN, D_MODEL)
    cc = jnp.concatenate([c, c_ctx[None, :], jnp.zeros((8 - n_batch - 1, D_MODEL), F32)], axis=0)
    mods = _mods(cc, w_ada, b_ada)
    cos_t, sin_t = _rope_tables(n_batch * CTX_LEN)
    tables = _dft_tables()
    final_g2 = final_g.reshape(1, D_MODEL)
    wb_ml, wb_da, wb_fn, wb_out = _to_bf16(w_br_ml), _to_bf16(w_br_da), _to_bf16(w_br_fn), _to_bf16(w_out)
    wb_ffn_in, wb_ffn_out = _to_bf16(w_ffn_in), _to_bf16(w_ffn_out)

    gate_lo = 4 * D_MODEL
    da_lo = gate_lo + N_GATE
    fn_lo = da_lo + 3 * D_MODEL
    w_main = jnp.concatenate([w_in[..., :D_MODEL], w_in[..., 2 * D_MODEL:gate_lo],
                              _da_col_perm(w_in[..., da_lo:da_lo + D_MODEL]),
                              _da_col_perm(w_in[..., da_lo + D_MODEL:da_lo + 2 * D_MODEL]),
                              w_in[..., da_lo + 2 * D_MODEL:fn_lo],
                              w_in[..., fn_lo + D_MODEL:], w_in[..., fn_lo:fn_lo + D_MODEL]], axis=-1).astype(BF16)
    w_kt = jnp.swapaxes(w_in[..., D_MODEL:2 * D_MODEL], 1, 2).astype(BF16)
    w_gate_t = jnp.swapaxes(w_in[..., gate_lo:da_lo], 1, 2).astype(BF16)

    tm_in = n_batch * CTX_LEN
    tm_tok = _row_tile(n_batch)
    xs = None
    for l in range(DEPTH):
        need_ctx = l < DEPTH - 1
        lam_init = 0.8 - 0.6 * math.exp(-0.3 * l)
        xl, xc = (x_lat, x_ctx) if xs is None else (xs, xs)
        u, kt, fn, gates_t = _inproj(xl, xc, 0 if xs is None else n_lat // tm_in, mods,
                                     norm_g[l, 0].reshape(1, D_MODEL), w_main, w_kt, w_gate_t, l,
                                     cos_t, sin_t, n_batch)
        hf, hb = _mlstm(u, kt, gates_t, ml_gate_b[l], n_batch)
        yd = _attn(u, da_lam[l], da_head_g[l].reshape(1, D_MODEL), lam_init, need_ctx, n_batch)
        yf = _fourier(fn, tables, need_ctx, n_batch)
        xs = _merge(hf, hb, u, yd, yf, xl, xc, 0 if xs is None else n_lat // tm_tok, mods,
                    ml_head_g[l].reshape(1, D_MODEL), wb_ml, wb_da, wb_fn, wb_out, l, need_ctx, n_batch)
        final = l == DEPTH - 1
        n_out = n_lat if final else xs.shape[0]
        xs = _ffn(xs, mods, norm_g[l, 1].reshape(1, D_MODEL), wb_ffn_in, wb_ffn_out, l,
                  final_g2, final, n_out, n_batch)
    return xs.reshape(n_batch, SEQ, D_MODEL)
```

```python
import functools
import math

import jax
import jax.numpy as jnp
from jax import lax
from jax.experimental import pallas as pl
from jax.experimental.pallas import tpu as pltpu

D_MODEL = 1024
SEQ = 4096
DEPTH = 4
CTX_LEN = 256
GRID_W = 64
NORM_EPS = 1e-6

ML_HEADS = 4
ML_DK = 256
ML_CHUNK = 256

DA_HEADS = 8
DA_DH = 64
DA_DV = 2 * DA_DH
ROPE_BASE = 10000.0
ATTN_TQ = 512
ATTN_ROW_CHUNKS = 2

FN_GROUPS = 4
FN_GC = 256
FFT_R = 64

D_FF = 2816
N_GATE = 4 * ML_HEADS
N_MOD = 6 * D_MODEL

U_QML, U_VML, U_OML, U_QDA, U_KDA, U_VDA, U_GPRE = 0, 1, 2, 3, 4, 5, 6
U_BLOCKS = 9
W_BLOCKS = U_BLOCKS + 1
STEP_KT = 1
N_COL_STEPS = W_BLOCKS // 2

VMEM_LIMIT_V7X = 56 * 1024 * 1024
LANES_V7X = 128
BF16_SUBLANES_V7X = 16
CAST_STEP_BYTES = 8 * 1024 * 1024

BF16 = jnp.bfloat16
F32 = jnp.float32


def _cparams(sem):
    return pltpu.CompilerParams(dimension_semantics=sem, vmem_limit_bytes=VMEM_LIMIT_V7X)


def _dot(a, b):
    return jnp.dot(a, b, preferred_element_type=F32)


def _dot_nt(a, b):
    return lax.dot_general(a, b, (((1,), (1,)), ((), ())), preferred_element_type=F32)


def _mod_row(mod_ref, i, rows_per_batch, n_batch, col):
    r = jnp.minimum(i // rows_per_batch, n_batch)
    return mod_ref[pl.ds(r, 1), col * D_MODEL:(col + 1) * D_MODEL]


def _row_tile(n_batch):
    return min(512, n_batch * CTX_LEN)


def _rms(x, g):
    return x * lax.rsqrt(jnp.mean(x * x, axis=-1, keepdims=True) + NORM_EPS) * g


def _cast_kernel(*refs):
    n = len(refs) // 2
    for w_ref, o_ref in zip(refs[:n], refs[n:]):
        o_ref[...] = w_ref[...].astype(o_ref.dtype)


def _to_bf16(*ws):
    n_l, rows, cols = ws[0].shape
    assert all(w.shape == ws[0].shape and w.dtype == F32 for w in ws)
    step_bytes = cols * 4 * len(ws)
    tr = max(t for t in range(BF16_SUBLANES_V7X, rows + 1, BF16_SUBLANES_V7X)
             if rows % t == 0 and t * step_bytes <= CAST_STEP_BYTES)
    spec = pl.BlockSpec((1, tr, cols), lambda l, i: (l, i, 0))
    return pl.pallas_call(
        _cast_kernel,
        grid=(n_l, rows // tr),
        in_specs=[spec] * len(ws),
        out_specs=[spec] * len(ws),
        out_shape=[jax.ShapeDtypeStruct(w.shape, BF16) for w in ws],
        compiler_params=_cparams(("arbitrary", "arbitrary")),
        name="cast_bf16",
    )(*ws)


def _mods_kernel(c_ref, w_ref, b_ref, o_ref):
    c = c_ref[...]
    s = (c * jax.nn.sigmoid(c)).astype(BF16)
    o_ref[0] = _dot(s, w_ref[0].astype(BF16)) + b_ref[0]


def _mods(cc, w_ada, b_ada):
    tn = 1536
    return pl.pallas_call(
        _mods_kernel,
        grid=(DEPTH, N_MOD // tn),
        in_specs=[
            pl.BlockSpec((8, D_MODEL), lambda l, j: (0, 0)),
            pl.BlockSpec((1, D_MODEL, tn), lambda l, j: (l, 0, j)),
            pl.BlockSpec((1, 1, tn), lambda l, j: (l, 0, j)),
        ],
        out_specs=pl.BlockSpec((1, 8, tn), lambda l, j: (l, 0, j)),
        out_shape=jax.ShapeDtypeStruct((DEPTH, 8, N_MOD), F32),
        compiler_params=_cparams(("arbitrary", "arbitrary")),
        name="adaln_mods",
    )(cc, w_ada, b_ada.reshape(DEPTH, 1, N_MOD))


def _inproj_kernel(xl_ref, xc_ref, mod_ref, g_ref, w_ref, wkt_ref, wgt_ref, cos_ref, sin_ref,
                   u_ref, kt_ref, fn_ref, gate_ref, xn_ref, *, rows_per_batch, n_batch):
    i = pl.program_id(0)
    j = pl.program_id(1)

    def normalise(x_ref):
        y = _rms(x_ref[...], g_ref[...])
        shift = _mod_row(mod_ref, i, rows_per_batch, n_batch, 0)
        scale = _mod_row(mod_ref, i, rows_per_batch, n_batch, 1)
        xn_ref[...] = (y * (1.0 + scale) + shift).astype(BF16)
        gate_ref[...] = _dot_nt(wgt_ref[...], xn_ref[...])

    is_lat = i < rows_per_batch * n_batch

    @pl.when(jnp.logical_and(j == 0, is_lat))
    def _():
        normalise(xl_ref)

    @pl.when(jnp.logical_and(j == 0, jnp.logical_not(is_lat)))
    def _():
        normalise(xc_ref)

    def product(h):
        return _dot(xn_ref[...], w_ref[:, h * D_MODEL:(h + 1) * D_MODEL])

    def plain(h):
        u_ref[:, h * D_MODEL:(h + 1) * D_MODEL] = product(h).astype(BF16)

    def rope(h):
        acc = product(h)
        cos = cos_ref[...]
        sin = sin_ref[...]
        for t in range(acc.shape[1] // DA_DV):
            x = acc[:, t * DA_DV:(t + 1) * DA_DV]
            lo = h * D_MODEL + t * DA_DV
            u_ref[:, lo:lo + DA_DV] = (x * cos + pltpu.roll(x, DA_DV // 2, 1) * sin).astype(BF16)

    def fourier_in(h):
        acc = product(h)
        for g in range(FN_GROUPS):
            z = acc[:, g * FN_GC:(g + 1) * FN_GC]
            fn_ref[g] = z.reshape(z.shape[0] // FFT_R, FFT_R * FN_GC).astype(BF16)

    def keys_t():
        kt_ref[...] = (_dot_nt(wkt_ref[...], xn_ref[...]) * (ML_DK ** -0.5)).astype(BF16)

    epilogues = {U_QML: plain, U_VML: plain, U_OML: plain, U_QDA: rope, U_KDA: rope, U_VDA: plain,
                 U_GPRE: plain, U_GPRE + 1: plain, U_GPRE + 2: plain, U_BLOCKS: fourier_in}
    for step in range(N_COL_STEPS):
        @pl.when(j == step)
        def _(step=step):
            for h in range(2):
                epilogues[2 * step + h](h)
            if step == STEP_KT:
                keys_t()


def _inproj(x_lat, x_ctx, ctx_blk, mods, g, w_main, w_kt, w_gate_t, layer, cos_t, sin_t, n_batch):
    tm = n_batch * CTX_LEN
    assert SEQ % tm == 0 and cos_t.shape[0] == SEQ + tm
    n_rows = n_batch * (SEQ + CTX_LEN)
    ni = n_rows // tm
    rows_per_batch = SEQ // tm
    lat_blocks = n_batch * rows_per_batch

    def tab_idx(i, j):
        return (jnp.where(i < lat_blocks, i % rows_per_batch, rows_per_batch), 0)

    kern = functools.partial(_inproj_kernel, rows_per_batch=rows_per_batch, n_batch=n_batch)
    return pl.pallas_call(
        kern,
        grid=(ni, N_COL_STEPS),
        in_specs=[
            pl.BlockSpec((tm, D_MODEL), lambda i, j: (jnp.minimum(i, lat_blocks - 1), 0)),
            pl.BlockSpec((tm, D_MODEL), lambda i, j: (ctx_blk, 0)),
            pl.BlockSpec((None, 8, N_MOD), lambda i, j: (layer, 0, 0)),
            pl.BlockSpec((1, D_MODEL), lambda i, j: (0, 0)),
            pl.BlockSpec((None, D_MODEL, 2 * D_MODEL), lambda i, j: (layer, 0, j)),
            pl.BlockSpec((None, D_MODEL, D_MODEL), lambda i, j: (layer, 0, 0)),
            pl.BlockSpec((None, N_GATE, D_MODEL), lambda i, j: (layer, 0, 0)),
            pl.BlockSpec((tm, DA_DV), tab_idx),
            pl.BlockSpec((tm, DA_DV), tab_idx),
        ],
        out_specs=[
            pl.BlockSpec((tm, 2 * D_MODEL), lambda i, j: (i, j)),
            pl.BlockSpec((D_MODEL, tm), lambda i, j: (0, i)),
            pl.BlockSpec((FN_GROUPS, tm // FFT_R, FFT_R * FN_GC), lambda i, j: (0, i, 0)),
            pl.BlockSpec((N_GATE, tm), lambda i, j: (0, i)),
        ],
        out_shape=[
            jax.ShapeDtypeStruct((n_rows, U_BLOCKS * D_MODEL), BF16),
            jax.ShapeDtypeStruct((D_MODEL, n_rows), BF16),
            jax.ShapeDtypeStruct((FN_GROUPS, n_rows // FFT_R, FFT_R * FN_GC), BF16),
            jax.ShapeDtypeStruct((N_GATE, n_rows), F32),
        ],
        scratch_shapes=[pltpu.VMEM((tm, D_MODEL), BF16)],
        compiler_params=_cparams(("arbitrary", "arbitrary")),
        name="in_proj",
    )(x_lat, x_ctx, mods, g, w_main, w_kt, w_gate_t, cos_t, sin_t)


def _split3(x):
    hi = x.astype(BF16).astype(F32)
    mid = (x - hi).astype(BF16).astype(F32)
    lo = (x - hi - mid).astype(BF16).astype(F32)
    return hi, mid, lo


def _mlstm_kernel(qf_ref, ktf_ref, vf_ref, gtf_ref, gtfn_ref, qb_ref, ktb_ref, vb_ref, gtb_ref, gtbn_ref, bias_ref,
                  hf_ref, hb_ref, *scratch):
    n_st = 2 * ML_HEADS
    cx_refs, bw_refs, pmw_refs = scratch[:n_st], scratch[n_st:2 * n_st], scratch[2 * n_st:3 * n_st]
    m_refs, c_refs, bend_refs, g_refs = (scratch[3 * n_st + 2 * k:3 * n_st + 2 * k + 2] for k in range(4))
    state_refs = cx_refs + m_refs
    s = pl.program_id(1)
    L = ML_CHUNK
    H = ML_HEADS
    W = LANES_V7X

    @pl.when(s == 0)
    def _():
        for ref in state_refs:
            ref[...] = jnp.zeros_like(ref)

    t_idx = lax.broadcasted_iota(jnp.int32, (L, L), 0)
    s_idx = lax.broadcasted_iota(jnp.int32, (L, L), 1)
    eye = t_idx == s_idx
    sub8 = lax.broadcasted_iota(jnp.int32, (8, W), 0)
    ones_w = jnp.ones((L, W), BF16)
    er = lax.broadcasted_iota(jnp.int32, (4 * L, 2 * W), 0)
    ec = lax.broadcasted_iota(jnp.int32, (4 * L, 2 * W), 1)
    expand = jnp.where((er < 3 * L) == (ec < W), 1.0, 0.0).astype(BF16)

    def running_max_rows(x, d):
        n_tiles = L // 8
        out = [None] * n_tiles
        carry = None
        for j in (range(n_tiles) if d == 0 else range(n_tiles - 1, -1, -1)):
            r = x[8 * j:8 * (j + 1)]
            k = 1
            while k < 8:
                if d == 0:
                    r = jnp.maximum(r, jnp.where(sub8 >= k, pltpu.roll(r, k, 0), -jnp.inf))
                else:
                    r = jnp.maximum(r, jnp.where(sub8 < 8 - k, pltpu.roll(r, 8 - k, 0), -jnp.inf))
                k *= 2
            if carry is not None:
                r = jnp.maximum(r, carry)
            carry = jnp.broadcast_to(r[7:8] if d == 0 else r[0:1], (8, W))
            out[j] = r
        return jnp.concatenate(out, axis=0)

    def gate_part(d, gt_ref):
        before = (t_idx <= s_idx) if d == 0 else (t_idx >= s_idx)
        gt = gt_ref[2 * H * d:2 * H * (d + 1), :] + bias_ref[2 * H * d:2 * H * (d + 1), :]
        i4, f4 = gt[0:H], gt[H:2 * H]
        lf4 = jnp.minimum(f4, 0.0) - jnp.log1p(jnp.exp(-jnp.abs(f4)))
        lf_terms = jnp.concatenate(list(_split3(lf4)) + [jnp.zeros((H, L), F32)], axis=0).astype(BF16)
        cum_rhs = jnp.concatenate([jnp.where(before, 1.0, 0.0).astype(BF16), ones_w], axis=1)
        r = _dot(lf_terms, cum_rhs)
        bx = r[0:H] + r[H:2 * H] + r[2 * H:3 * H]
        b4, bend4 = bx[:, 0:L], bx[:, L:L + W]
        c4 = i4 - b4
        c_refs[d][...] = c4
        bend_refs[d][...] = bend4
        g_refs[d][...] = bend4[:, 0:1] - b4 + i4
        col_terms = _split3(b4) + (c4.astype(BF16).astype(F32),)
        for h in range(H):
            diag = jnp.concatenate([jnp.where(eye, x[h:h + 1, :], 0.0) for x in col_terms], axis=1).astype(BF16)
            wide = _dot(diag, expand)
            bw_refs[d * H + h][...] = wide[:, 0:W]
            pmw_refs[d * H + h][...] = running_max_rows(wide[:, W:2 * W], d)

    dirs = ((0, qf_ref, ktf_ref, vf_ref, gtf_ref, gtfn_ref, hf_ref),
            (1, qb_ref, ktb_ref, vb_ref, gtb_ref, gtbn_ref, hb_ref))

    @pl.when(s == 0)
    def _():
        for d, _, _, _, gt_ref, _, _ in dirs:
            gate_part(d, gt_ref)

    for d, q_ref, kt_ref, v_ref, _, _, h_ref in dirs:
        causal = (s_idx <= t_idx) if d == 0 else (s_idx >= t_idx)
        c4, bend4, g4 = c_refs[d][...], bend_refs[d][...], g_refs[d][...]
        m_prev4 = m_refs[d][...]
        m_new4 = jnp.maximum(bend4 + m_prev4, jnp.max(g4, axis=1, keepdims=True))
        m_refs[d][...] = m_new4
        decay4 = jnp.exp(bend4 + m_prev4 - m_new4)
        w4 = jnp.exp(g4 - m_new4[:, 0:1])
        for h in range(H):
            st = d * H + h
            hs = slice(h * ML_DK, (h + 1) * ML_DK)
            b_w = bw_refs[st][...]
            m_w = jnp.maximum(pmw_refs[st][...], m_prev4[h:h + 1, :])

            q = q_ref[:, hs]
            kt = kt_ref[hs, :]
            vx = jnp.concatenate([v_ref[:, hs], ones_w], axis=1)
            m_ll = jnp.concatenate([m_w] * (L // W), axis=1)
            a = (jnp.where(causal, jnp.exp(c4[h:h + 1, :] - m_ll), 0.0) * _dot(q, kt)).astype(BF16)
            cx_prev = cx_refs[st][...]
            qc = _dot(q, cx_prev.astype(BF16))
            av = _dot(a, vx)
            sc_w = jnp.exp(m_prev4[h:h + 1, :] - m_w)
            den = sc_w * qc[:, ML_DK:] + av[:, ML_DK:]
            inv = 1.0 / jnp.maximum(jnp.abs(den), jnp.exp(-(b_w + m_w)))
            for t in range(ML_DK // W):
                ts = slice(t * W, (t + 1) * W)
                h_ref[:, h * ML_DK + t * W:h * ML_DK + (t + 1) * W] = (
                    (sc_w * qc[:, ts] + av[:, ts]) * inv).astype(h_ref.dtype)

            kw = (kt.astype(F32) * w4[h:h + 1, :]).astype(BF16)
            dec = jnp.concatenate([decay4[h:h + 1, :]] * (ML_DK // W + 1), axis=1)
            cx_refs[st][...] = dec * cx_prev + _dot(kw, vx)

    for d, _, _, _, _, gtn_ref, _ in dirs:
        gate_part(d, gtn_ref)


def _mlstm(u, kt, gates_t, gate_b, n_batch):
    n_rows = u.shape[0]
    L = ML_CHUNK
    lat_chunks = SEQ // L
    ctx_chunks = CTX_LEN // L
    n_steps = ctx_chunks + lat_chunks
    ctx_base = n_batch * lat_chunks

    def rowblk(d):
        def f(b, s):
            in_ctx = s < ctx_chunks
            if d == 0:
                c = jnp.where(in_ctx, s, s - ctx_chunks)
            else:
                c = jnp.where(in_ctx, ctx_chunks - 1 - s, lat_chunks - 1 - (s - ctx_chunks))
            return jnp.where(in_ctx, ctx_base + ctx_chunks * b, lat_chunks * b) + c
        return f

    def dir_specs(d):
        rb = rowblk(d)
        return [
            pl.BlockSpec((L, D_MODEL), lambda b, s: (rb(b, s), U_QML)),
            pl.BlockSpec((D_MODEL, L), lambda b, s: (0, rb(b, s))),
            pl.BlockSpec((L, D_MODEL), lambda b, s: (rb(b, s), U_VML)),
            pl.BlockSpec((N_GATE, L), lambda b, s: (0, rb(b, s))),
            pl.BlockSpec((N_GATE, L), lambda b, s: (0, rb(b, jnp.minimum(s + 1, n_steps - 1)))),
        ]

    def out_spec(d):
        rb = rowblk(d)
        return pl.BlockSpec((L, D_MODEL), lambda b, s: (rb(b, s), 0))

    n_st = 2 * ML_HEADS
    return pl.pallas_call(
        _mlstm_kernel,
        grid=(n_batch, n_steps),
        in_specs=dir_specs(0) + dir_specs(1) + [pl.BlockSpec((N_GATE, 1), lambda b, s: (0, 0))],
        out_specs=[out_spec(0), out_spec(1)],
        out_shape=[jax.ShapeDtypeStruct((n_rows, D_MODEL), BF16)] * 2,
        scratch_shapes=(
            [pltpu.VMEM((ML_DK, ML_DK + LANES_V7X), F32)] * n_st
            + [pltpu.VMEM((L, LANES_V7X), F32)] * (2 * n_st)
            + [pltpu.VMEM((ML_HEADS, LANES_V7X), F32)] * 2
            + [pltpu.VMEM((ML_HEADS, L), F32)] * 2
            + [pltpu.VMEM((ML_HEADS, LANES_V7X), F32)] * 2
            + [pltpu.VMEM((ML_HEADS, L), F32)] * 2),
        compiler_params=_cparams(("arbitrary", "arbitrary")),
        name="mlstm_scan",
    )(u, kt, u, gates_t, gates_t, u, kt, u, gates_t, gates_t, gate_b.reshape(N_GATE, 1))


def _attn_lambda(lam_ref, lam_init):
    lq = lam_ref[...]
    return (jnp.exp(jnp.sum(lq[0:1] * lq[1:2], axis=1, keepdims=True))
            - jnp.exp(jnp.sum(lq[2:3] * lq[3:4], axis=1, keepdims=True)) + lam_init)


def _attn_queries(q_ref):
    q = q_ref[...]
    lane = lax.broadcasted_iota(jnp.int32, (1, DA_DV), 1)
    zero = jnp.zeros_like(q)
    is_map0 = (lane % DA_DH) < (DA_DH // 2)
    q2 = jnp.concatenate([jnp.where(is_map0, q, zero), jnp.where(is_map0, zero, q)], axis=0)
    return q2 * (DA_DH ** -0.5)


def _attn_sums(s_chunks, m_chunks, vx):
    return jnp.concatenate([_dot(jnp.exp(s - m).astype(BF16), vx) for s, m in zip(s_chunks, m_chunks)], axis=0)


def _attn_finish(acc, lam, g, lam_init, tq):
    o0 = acc[0:tq, 0:DA_DV] * (1.0 / acc[0:tq, DA_DV:DA_DV + 1])
    o1 = acc[tq:, 0:DA_DV] * (1.0 / acc[tq:, DA_DV:DA_DV + 1])
    return _rms(o0 - lam * o1, g) * (1.0 - lam_init)


def _attn_kernel(qa_ref, kla_ref, kca_ref, qb_ref, klb_ref, kcb_ref, vl_ref, vc_ref, lam_ref, g_ref, o_ref,
                 vx_ref, s_ref, m_ref, *, lam_init, q_blocks):
    t = pl.program_id(0)
    tq = qa_ref.shape[0]
    rows = 2 * tq // ATTN_ROW_CHUNKS

    @pl.when(jnp.logical_and(t > 0, (2 * t - 2) % q_blocks == 0))
    def _():
        vx_ref[0:SEQ, 0:DA_DV] = vl_ref[...]
        vx_ref[SEQ:, 0:DA_DV] = vc_ref[...]
        lane_v = lax.broadcasted_iota(jnp.int32, (SEQ + CTX_LEN, DA_DV), 1)
        vx_ref[:, DA_DV:] = jnp.where(lane_v == 0, 1.0, 0.0).astype(BF16)

    def score(slot, q_ref, kl_ref, kc_ref):
        q2 = _attn_queries(q_ref)
        s_lat = _dot_nt(q2, kl_ref[...])
        s_ctx = _dot_nt(q2, kc_ref[...])
        m_ref[slot] = jnp.maximum(jnp.max(s_lat, axis=1, keepdims=True), jnp.max(s_ctx, axis=1, keepdims=True))
        s_ref[slot, :, 0:SEQ] = s_lat
        s_ref[slot, :, SEQ:] = s_ctx

    def drain(slot):
        s_chunks = [s_ref[slot, c * rows:(c + 1) * rows, :] for c in range(ATTN_ROW_CHUNKS)]
        m_chunks = [m_ref[slot, c * rows:(c + 1) * rows, :] for c in range(ATTN_ROW_CHUNKS)]
        acc = _attn_sums(s_chunks, m_chunks, vx_ref[...])
        o = _attn_finish(acc, _attn_lambda(lam_ref, lam_init), g_ref[...], lam_init, tq)
        o_ref[slot * tq:(slot + 1) * tq, :] = o.astype(o_ref.dtype)

    @pl.when(t > 0)
    def _():
        drain(0)
        score(1, qa_ref, kla_ref, kca_ref)

    @pl.when(t == 0)
    def _():
        score(0, qb_ref, klb_ref, kcb_ref)

    @pl.when(t > 0)
    def _():
        drain(1)
        score(0, qb_ref, klb_ref, kcb_ref)


def _attn_ctx_kernel(q_ref, kc_ref, vc_ref, lam_ref, g_ref, yin_ref, o_ref, *, lam_init):
    del yin_ref
    tq = q_ref.shape[0]
    s = _dot_nt(_attn_queries(q_ref), kc_ref[...])
    lane_v = lax.broadcasted_iota(jnp.int32, (CTX_LEN, DA_DV), 1)
    vx = jnp.concatenate([vc_ref[...], jnp.where(lane_v == 0, 1.0, 0.0).astype(BF16)], axis=1)
    acc = _attn_sums([s], [jnp.max(s, axis=1, keepdims=True)], vx)
    o = _attn_finish(acc, _attn_lambda(lam_ref, lam_init), g_ref[...], lam_init, tq)
    o_ref[...] = o.astype(o_ref.dtype)


def _attn(u, da_lam, head_g, lam_init, need_ctx, n_batch):
    n_rows = u.shape[0]
    tq = ATTN_TQ
    lat_qblocks = SEQ // tq
    ctx_base = n_batch * (SEQ // CTX_LEN)
    cpb = D_MODEL // DA_DV
    n_keys = SEQ + CTX_LEN

    n_items = n_batch * DA_HEADS * lat_qblocks
    assert lat_qblocks % 2 == 0

    def item(k):
        k = jnp.clip(k, 0, n_items - 1)
        return k // (DA_HEADS * lat_qblocks), (k // lat_qblocks) % DA_HEADS, k % lat_qblocks

    def of_item(offset, f):
        return lambda t: f(*item(2 * t + offset))

    def scored_specs(offset):
        return [
            pl.BlockSpec((tq, DA_DV), of_item(offset, lambda b, h, qi: (b * lat_qblocks + qi, U_QDA * cpb + h))),
            pl.BlockSpec((SEQ, DA_DV), of_item(offset, lambda b, h, qi: (b, U_KDA * cpb + h))),
            pl.BlockSpec((CTX_LEN, DA_DV), of_item(offset, lambda b, h, qi: (ctx_base + b, U_KDA * cpb + h))),
        ]

    kern = functools.partial(_attn_kernel, lam_init=lam_init, q_blocks=lat_qblocks)
    yd = pl.pallas_call(
        kern,
        grid=(n_items // 2 + 1,),
        in_specs=scored_specs(-1) + scored_specs(0) + [
            pl.BlockSpec((SEQ, DA_DV), of_item(-2, lambda b, h, qi: (b, U_VDA * cpb + h))),
            pl.BlockSpec((CTX_LEN, DA_DV), of_item(-2, lambda b, h, qi: (ctx_base + b, U_VDA * cpb + h))),
            pl.BlockSpec((4, DA_DH), lambda t: (0, 0)),
            pl.BlockSpec((1, DA_DV), of_item(-2, lambda b, h, qi: (0, h))),
        ],
        out_specs=pl.BlockSpec((2 * tq, DA_DV),
                               of_item(-2, lambda b, h, qi: ((b * lat_qblocks + qi) // 2, h))),
        out_shape=jax.ShapeDtypeStruct((n_rows, D_MODEL), BF16),
        scratch_shapes=[
            pltpu.VMEM((n_keys, 2 * DA_DV), BF16),
            pltpu.VMEM((2, 2 * tq, n_keys), F32),
            pltpu.VMEM((2, 2 * tq, 1), F32),
        ],
        compiler_params=_cparams(("arbitrary",)),
        name="diff_attn",
    )(u, u, u, u, u, u, u, u, da_lam, head_g)
    if not need_ctx:
        return yd
    kern_ctx = functools.partial(_attn_ctx_kernel, lam_init=lam_init)
    return pl.pallas_call(
        kern_ctx,
        grid=(n_batch, DA_HEADS),
        in_specs=[
            pl.BlockSpec((CTX_LEN, DA_DV), lambda b, h: (ctx_base + b, U_QDA * cpb + h)),
            pl.BlockSpec((CTX_LEN, DA_DV), lambda b, h: (ctx_base + b, U_KDA * cpb + h)),
            pl.BlockSpec((CTX_LEN, DA_DV), lambda b, h: (ctx_base + b, U_VDA * cpb + h)),
            pl.BlockSpec((4, DA_DH), lambda b, h: (0, 0)),
            pl.BlockSpec((1, DA_DV), lambda b, h: (0, h)),
            pl.BlockSpec(memory_space=pl.ANY),
        ],
        out_specs=pl.BlockSpec((CTX_LEN, DA_DV), lambda b, h: (ctx_base + b, h)),
        out_shape=jax.ShapeDtypeStruct((n_rows, D_MODEL), BF16),
        input_output_aliases={5: 0},
        compiler_params=_cparams(("arbitrary", "arbitrary")),
        name="diff_attn_ctx",
    )(u, u, u, da_lam, head_g, yd)


def _dft_tables():
    R = FFT_R

    def cs(num, period):
        ang = (num % period).astype(F32) * (2.0 * math.pi / period)
        return jnp.cos(ang), jnp.sin(ang)

    idx = jnp.arange(R, dtype=jnp.int32)
    c1, s1 = cs(idx[:, None] * idx[None, :], R)
    a1 = jnp.concatenate([c1, -s1], axis=0).astype(BF16)
    f2 = idx[:, None, None]
    f1 = idx[None, :, None]
    t1 = idx[None, None, :]
    mc, ms = cs(t1 * (R * f1 + f2), SEQ)
    b2 = jnp.concatenate([jnp.concatenate([mc, ms], axis=2),
                          jnp.concatenate([-ms, mc], axis=2)], axis=1).astype(BF16)
    ch = jnp.arange(FN_GC, dtype=jnp.int32)
    cc, sc = cs(ch[:, None] * ch[None, :], FN_GC)
    csm = jnp.concatenate([cc, sc], axis=0).astype(BF16)
    actx = jnp.concatenate([cc, -sc], axis=0).astype(BF16)
    return a1, b2, csm, actx


def _fft1_kernel(a_ref, z_ref, p_ref):
    p_ref[0, 0] = _dot(a_ref[...], z_ref[0]).astype(p_ref.dtype)


def _fft1(fn_view, a1, n_batch):
    R = FFT_R
    lanes = R * FN_GC
    lc = lanes
    return pl.pallas_call(
        _fft1_kernel,
        grid=(FN_GROUPS, n_batch, lanes // lc),
        in_specs=[
            pl.BlockSpec((2 * R, R), lambda g, b, c: (0, 0)),
            pl.BlockSpec((1, R, lc), lambda g, b, c: (g, b, c)),
        ],
        out_specs=pl.BlockSpec((1, 1, 2 * R, lc), lambda g, b, c: (g, b, 0, c)),
        out_shape=jax.ShapeDtypeStruct((FN_GROUPS, n_batch, 2 * R, lanes), BF16),
        compiler_params=_cparams(("arbitrary", "arbitrary", "arbitrary")),
        name="fourier_stage1",
    )(a1, fn_view)


def _fft2_kernel(p_ref, b_ref, cs_ref, o_ref, *, f2b):
    R = FFT_R
    cc = cs_ref[0:FN_GC, :]
    sc = cs_ref[FN_GC:2 * FN_GC, :]
    norm = 1.0 / math.sqrt(SEQ * FN_GC)
    for g in range(FN_GROUPS):
        xr, xi = [], []
        for jj in range(f2b):
            stacked = jnp.concatenate([p_ref[g, 0, 0, jj], p_ref[g, 0, 1, jj]], axis=0)
            x = _dot(b_ref[jj], stacked)
            xr.append(x[0:R])
            xi.append(x[R:2 * R])
        y = (_dot(jnp.concatenate(xr, axis=0).astype(BF16), cc)
             + _dot(jnp.concatenate(xi, axis=0).astype(BF16), sc)) * norm
        for jj in range(f2b):
            lo = jj * D_MODEL + g * FN_GC
            o_ref[:, lo:lo + FN_GC] = y[jj * R:(jj + 1) * R].astype(o_ref.dtype)


def _fft2(p6, b2, csm, n_rows, n_batch):
    R = FFT_R
    f2b = 16
    kern = functools.partial(_fft2_kernel, f2b=f2b)
    return pl.pallas_call(
        kern,
        grid=(n_batch, R // f2b),
        in_specs=[
            pl.BlockSpec((FN_GROUPS, 1, 2, f2b, R, FN_GC), lambda b, f: (0, b, 0, f, 0, 0)),
            pl.BlockSpec((f2b, 2 * R, 2 * R), lambda b, f: (f, 0, 0)),
            pl.BlockSpec((2 * FN_GC, FN_GC), lambda b, f: (0, 0)),
        ],
        out_specs=pl.BlockSpec((R, f2b * D_MODEL), lambda b, f: (b, f)),
        out_shape=jax.ShapeDtypeStruct((n_rows // R, R * D_MODEL), BF16),
        compiler_params=_cparams(("arbitrary", "arbitrary")),
        name="fourier_stage2",
    )(p6, b2, csm)


def _fft_ctx_kernel(z_ref, a_ref, cs_ref, yin_ref, o_ref, *, n_batch):
    del yin_ref
    cc = cs_ref[0:FN_GC, :]
    sc = cs_ref[FN_GC:2 * FN_GC, :]
    norm = 1.0 / math.sqrt(CTX_LEN * FN_GC)
    n_ctx = n_batch * CTX_LEN
    groups = []
    for g in range(FN_GROUPS):
        z_all = z_ref[g].astype(F32).reshape(n_ctx, FN_GC)
        ys = []
        for b in range(n_batch):
            p = _dot(a_ref[...], z_all[b * CTX_LEN:(b + 1) * CTX_LEN].astype(BF16))
            ys.append(_dot(p[0:CTX_LEN].astype(BF16), cc) + _dot(p[CTX_LEN:2 * CTX_LEN].astype(BF16), sc))
        groups.append(jnp.concatenate(ys, axis=0))
    y = jnp.concatenate(groups, axis=1) * norm
    o_ref[...] = y.reshape(n_ctx // FFT_R, FFT_R * D_MODEL).astype(o_ref.dtype)


def _fft_ctx(fn_view, actx, csm, yf, n_batch):
    n_ctx = n_batch * CTX_LEN
    ctx_blk = n_batch * SEQ // n_ctx
    kern = functools.partial(_fft_ctx_kernel, n_batch=n_batch)
    return pl.pallas_call(
        kern,
        grid=(1,),
        in_specs=[
            pl.BlockSpec((FN_GROUPS, n_ctx // FFT_R, FFT_R * FN_GC), lambda i: (0, ctx_blk, 0)),
            pl.BlockSpec((2 * CTX_LEN, CTX_LEN), lambda i: (0, 0)),
            pl.BlockSpec((2 * FN_GC, FN_GC), lambda i: (0, 0)),
            pl.BlockSpec(memory_space=pl.ANY),
        ],
        out_specs=pl.BlockSpec((n_ctx // FFT_R, FFT_R * D_MODEL), lambda i: (ctx_blk, 0)),
        out_shape=jax.ShapeDtypeStruct(yf.shape, BF16),
        input_output_aliases={3: 0},
        compiler_params=_cparams(("arbitrary",)),
        name="fourier_ctx",
    )(fn_view, actx, csm, yf)


def _fourier(fn, tables, need_ctx, n_batch):
    a1, b2, csm, actx = tables
    R = FFT_R
    n_rows = fn.shape[1] * R
    p = _fft1(fn, a1, n_batch)
    yf = _fft2(p.reshape(FN_GROUPS, n_batch, 2, R, R, FN_GC), b2, csm, n_rows, n_batch)
    if need_ctx:
        yf = _fft_ctx(fn, actx, csm, yf, n_batch)
    return yf


def _merge_kernel(hf_ref, hb_ref, o_ref, gm_ref, gd_ref, gf_ref, yd_ref, yf_ref, xl_ref, xc_ref, mod_ref, hg_ref,
                  wml_ref, wda_ref, wfn_ref, wout_ref, out_ref, *, rows_per_batch, n_batch):
    i = pl.program_id(0)
    x = jnp.where(i < rows_per_batch * n_batch, xl_ref[...], xc_ref[...])
    hsum = hf_ref[...].astype(F32) + hb_ref[...].astype(F32)
    hg = hg_ref[...]
    parts = []
    for h in range(ML_HEADS):
        hs = slice(h * ML_DK, (h + 1) * ML_DK)
        parts.append(_rms(hsum[:, hs], hg[:, hs]))
    ym = (jnp.concatenate(parts, axis=1) * jax.nn.sigmoid(o_ref[...].astype(F32))).astype(BF16)
    yf = yf_ref[...].astype(F32).reshape(x.shape).astype(BF16)
    y = (jax.nn.sigmoid(gm_ref[...].astype(F32)) * _dot(ym, wml_ref[...])
         + jax.nn.sigmoid(gd_ref[...].astype(F32)) * _dot(yd_ref[...], wda_ref[...])
         + jax.nn.sigmoid(gf_ref[...].astype(F32)) * _dot(yf, wfn_ref[...]))
    gate = _mod_row(mod_ref, i, rows_per_batch, n_batch, 2)
    out_ref[...] = x + gate * _dot(y.astype(BF16), wout_ref[...])


def _merge(hf, hb, u, yd, yf, x_lat, x_ctx, ctx_blk0, mods, head_g, wml, wda, wfn, wout, layer, need_ctx, n_batch):
    n_rows = n_batch * (SEQ + CTX_LEN)
    tm = _row_tile(n_batch)
    rows_per_batch = SEQ // tm
    lat_blocks = n_batch * rows_per_batch
    ni = (n_rows if need_ctx else n_batch * SEQ) // tm
    kern = functools.partial(_merge_kernel, rows_per_batch=rows_per_batch, n_batch=n_batch)
    row = lambda i: (i, 0)
    full = lambda i: (0, 0)
    wspec = pl.BlockSpec((None, D_MODEL, D_MODEL), lambda i: (layer, 0, 0))
    return pl.pallas_call(
        kern,
        grid=(ni,),
        in_specs=[
            pl.BlockSpec((tm, D_MODEL), row),
            pl.BlockSpec((tm, D_MODEL), row),
            pl.BlockSpec((tm, D_MODEL), lambda i: (i, U_OML)),
            pl.BlockSpec((tm, D_MODEL), lambda i: (i, U_GPRE)),
            pl.BlockSpec((tm, D_MODEL), lambda i: (i, U_GPRE + 1)),
            pl.BlockSpec((tm, D_MODEL), lambda i: (i, U_GPRE + 2)),
            pl.BlockSpec((tm, D_MODEL), row),
            pl.BlockSpec((tm // FFT_R, FFT_R * D_MODEL), row),
            pl.BlockSpec((tm, D_MODEL), lambda i: (jnp.minimum(i, lat_blocks - 1), 0)),
            pl.BlockSpec((tm, D_MODEL), lambda i: (ctx_blk0 + jnp.maximum(i - lat_blocks, 0), 0)),
            pl.BlockSpec((None, 8, N_MOD), lambda i: (layer, 0, 0)),
            pl.BlockSpec((1, D_MODEL), full),
            wspec, wspec, wspec, wspec,
        ],
        out_specs=pl.BlockSpec((tm, D_MODEL), row),
        out_shape=jax.ShapeDtypeStruct((ni * tm, D_MODEL), F32),
        compiler_params=_cparams(("arbitrary",)),
        name="merge_out_proj",
    )(hf, hb, u, u, u, u, yd, yf, x_lat, x_ctx, mods, head_g, wml, wda, wfn, wout)


FFN_CHUNKS = ((0, 1024), (1024, 1024), (2048, 768))


def _ffn_kernel(x_ref, mod_ref, g_ref, win_ref, wout_ref, fg_ref, out_ref, *, rows_per_batch, n_batch, final):
    i = pl.program_id(0)
    x = x_ref[...]
    shift = _mod_row(mod_ref, i, rows_per_batch, n_batch, 3)
    scale = _mod_row(mod_ref, i, rows_per_batch, n_batch, 4)
    gate = _mod_row(mod_ref, i, rows_per_batch, n_batch, 5)
    h = (_rms(x, g_ref[...]) * (1.0 + scale) + shift).astype(BF16)
    acc = None
    for lo, width in FFN_CHUNKS:
        a = _dot(h, win_ref[:, lo:lo + width])
        b = _dot(h, win_ref[:, D_FF + lo:D_FF + lo + width])
        act = (a * jax.nn.sigmoid(a) * b).astype(BF16)
        part = _dot(act, wout_ref[lo:lo + width, :])
        acc = part if acc is None else acc + part
    xn = x + gate * acc
    out_ref[...] = _rms(xn, fg_ref[...]) if final else xn


def _ffn(x, mods, g, w_in, w_out, layer, final_g, final, n_rows_out, n_batch):
    tm = n_batch * CTX_LEN
    rows_per_batch = SEQ // tm
    resident = pl.Buffered(1)
    kern = functools.partial(_ffn_kernel, rows_per_batch=rows_per_batch, n_batch=n_batch, final=final)
    row = lambda i: (i, 0)
    full = lambda i: (0, 0)
    return pl.pallas_call(
        kern,
        grid=(n_rows_out // tm,),
        in_specs=[
            pl.BlockSpec((tm, D_MODEL), row),
            pl.BlockSpec((None, 8, N_MOD), lambda i: (layer, 0, 0)),
            pl.BlockSpec((1, D_MODEL), full),
            pl.BlockSpec((None, D_MODEL, 2 * D_FF), lambda i: (layer, 0, 0), pipeline_mode=resident),
            pl.BlockSpec((None, D_FF, D_MODEL), lambda i: (layer, 0, 0), pipeline_mode=resident),
            pl.BlockSpec((1, D_MODEL), full),
        ],
        out_specs=pl.BlockSpec((tm, D_MODEL), row),
        out_shape=jax.ShapeDtypeStruct((n_rows_out, D_MODEL), F32),
        compiler_params=_cparams(("arbitrary",)),
        name="swiglu_ffn",
    )(x, mods, g, w_in, w_out, final_g)


def _da_col_perm(w):
    half = DA_DH // 2
    lead = w.shape[:-1]
    return jnp.swapaxes(w.reshape(lead + (DA_HEADS, 2, 2, half)), -3, -2).reshape(lead + (DA_HEADS * DA_DV,))


def _rope_tables(pad):
    n_freq = DA_DH // 4
    rows = SEQ // GRID_W
    inv = ROPE_BASE ** (-jnp.arange(n_freq, dtype=F32) / n_freq)
    r = jnp.repeat(jnp.arange(rows, dtype=F32), GRID_W)
    col = jnp.tile(jnp.arange(GRID_W, dtype=F32), rows)
    ang = jnp.concatenate([r[:, None] * inv, col[:, None] * inv], axis=-1)
    cos, sin = jnp.cos(ang), jnp.sin(ang)
    cos_t = jnp.concatenate([cos, cos, cos, cos], axis=-1)
    sin_t = jnp.concatenate([-sin, -sin, sin, sin], axis=-1)
    cos_t = jnp.concatenate([cos_t, jnp.ones((pad, DA_DV), F32)], axis=0)
    sin_t = jnp.concatenate([sin_t, jnp.zeros((pad, DA_DV), F32)], axis=0)
    return cos_t, sin_t


def kernel(x, c, ctx, c_ctx, w_ada, b_ada, norm_g, w_in, ml_gate_b, ml_head_g, da_lam, da_head_g,
           w_br_ml, w_br_da, w_br_fn, w_out, w_ffn_in, w_ffn_out, final_g):
    n_batch = x.shape[0]
    n_lat = n_batch * SEQ
    x_lat = x.reshape(n_lat, D_MODEL)
    x_ctx = ctx.reshape(n_batch * CTX_LEN, D_MODEL)
    cc = jnp.concatenate([c, c_ctx[None, :], jnp.zeros((8 - n_batch - 1, D_MODEL), F32)], axis=0)
    mods = _mods(cc, w_ada, b_ada)
    cos_t, sin_t = _rope_tables(n_batch * CTX_LEN)
    tables = _dft_tables()
    final_g2 = final_g.reshape(1, D_MODEL)
    wb_ml, wb_da, wb_fn, wb_out = _to_bf16(w_br_ml, w_br_da, w_br_fn, w_out)
    (wb_ffn_in,), (wb_ffn_out,) = _to_bf16(w_ffn_in), _to_bf16(w_ffn_out)

    gate_lo = 4 * D_MODEL
    da_lo = gate_lo + N_GATE
    fn_lo = da_lo + 3 * D_MODEL
    w_main = jnp.concatenate([w_in[..., :D_MODEL], w_in[..., 2 * D_MODEL:gate_lo],
                              _da_col_perm(w_in[..., da_lo:da_lo + D_MODEL]),
                              _da_col_perm(w_in[..., da_lo + D_MODEL:da_lo + 2 * D_MODEL]),
                              w_in[..., da_lo + 2 * D_MODEL:fn_lo],
                              w_in[..., fn_lo + D_MODEL:], w_in[..., fn_lo:fn_lo + D_MODEL]], axis=-1).astype(BF16)
    w_kt = jnp.swapaxes(w_in[..., D_MODEL:2 * D_MODEL], 1, 2).astype(BF16)
    w_gate_t = jnp.swapaxes(w_in[..., gate_lo:da_lo], 1, 2).astype(BF16)

    tm_in = n_batch * CTX_LEN
    tm_tok = _row_tile(n_batch)
    xs = None
    for l in range(DEPTH):
        need_ctx = l < DEPTH - 1
        lam_init = 0.8 - 0.6 * math.exp(-0.3 * l)
        xl, xc = (x_lat, x_ctx) if xs is None else (xs, xs)
        u, kt, fn, gates_t = _inproj(xl, xc, 0 if xs is None else n_lat // tm_in, mods,
                                     norm_g[l, 0].reshape(1, D_MODEL), w_main, w_kt, w_gate_t, l,
                                     cos_t, sin_t, n_batch)
        hf, hb = _mlstm(u, kt, gates_t, ml_gate_b[l], n_batch)
        yd = _attn(u, da_lam[l], da_head_g[l].reshape(1, D_MODEL), lam_init, need_ctx, n_batch)
        yf = _fourier(fn, tables, need_ctx, n_batch)
        xs = _merge(hf, hb, u, yd, yf, xl, xc, 0 if xs is None else n_lat // tm_tok, mods,
                    ml_head_g[l].reshape(1, D_MODEL), wb_ml, wb_da, wb_fn, wb_out, l, need_ctx, n_batch)
        final = l == DEPTH - 1
        n_out = n_lat if final else xs.shape[0]
        xs = _ffn(xs, mods, norm_g[l, 1].reshape(1, D_MODEL), wb_ffn_in, wb_ffn_out, l,
                  final_g2, final, n_out, n_batch)
    return xs.reshape(n_batch, SEQ, D_MODEL)
```

```python
import functools
import math

import jax
import jax.numpy as jnp
from jax import lax
from jax.experimental import pallas as pl
from jax.experimental.pallas import tpu as pltpu

D_MODEL = 1024
SEQ = 4096
DEPTH = 4
CTX_LEN = 256
GRID_W = 64
NORM_EPS = 1e-6

ML_HEADS = 4
ML_DK = 256
ML_CHUNK = 256

DA_HEADS = 8
DA_DH = 64
DA_DV = 2 * DA_DH
ROPE_BASE = 10000.0
ATTN_TQ = 512
ATTN_ROW_CHUNKS = 2

FN_GROUPS = 4
FN_GC = 256
FFT_R = 64

D_FF = 2816
N_GATE = 4 * ML_HEADS
N_MOD = 6 * D_MODEL

U_QML, U_VML, U_OML, U_QDA, U_KDA, U_VDA, U_GPRE = 0, 1, 2, 3, 4, 5, 6
U_BLOCKS = 9
W_BLOCKS = U_BLOCKS + 1
STEP_KT = 1
N_COL_STEPS = W_BLOCKS // 2

VMEM_LIMIT_V7X = 56 * 1024 * 1024
LANES_V7X = 128
BF16_SUBLANES_V7X = 16
CAST_STEP_BYTES = 8 * 1024 * 1024

BF16 = jnp.bfloat16
F32 = jnp.float32


def _cparams(sem):
    return pltpu.CompilerParams(dimension_semantics=sem, vmem_limit_bytes=VMEM_LIMIT_V7X)


def _dot(a, b):
    return jnp.dot(a, b, preferred_element_type=F32)


def _dot_nt(a, b):
    return lax.dot_general(a, b, (((1,), (1,)), ((), ())), preferred_element_type=F32)


def _mod_row(mod_ref, i, rows_per_batch, n_batch, col):
    r = jnp.minimum(i // rows_per_batch, n_batch)
    return mod_ref[pl.ds(r, 1), col * D_MODEL:(col + 1) * D_MODEL]


def _row_tile(n_batch):
    return min(512, n_batch * CTX_LEN)


def _rms(x, g):
    return x * lax.rsqrt(jnp.mean(x * x, axis=-1, keepdims=True) + NORM_EPS) * g


def _cast_kernel(*refs):
    n = len(refs) // 2
    for w_ref, o_ref in zip(refs[:n], refs[n:]):
        o_ref[...] = w_ref[...].astype(o_ref.dtype)


def _to_bf16(*ws):
    n_l, rows, cols = ws[0].shape
    assert all(w.shape == ws[0].shape and w.dtype == F32 for w in ws)
    step_bytes = cols * 4 * len(ws)
    tr = max(t for t in range(BF16_SUBLANES_V7X, rows + 1, BF16_SUBLANES_V7X)
             if rows % t == 0 and t * step_bytes <= CAST_STEP_BYTES)
    spec = pl.BlockSpec((1, tr, cols), lambda l, i: (l, i, 0))
    return pl.pallas_call(
        _cast_kernel,
        grid=(n_l, rows // tr),
        in_specs=[spec] * len(ws),
        out_specs=[spec] * len(ws),
        out_shape=[jax.ShapeDtypeStruct(w.shape, BF16) for w in ws],
        compiler_params=_cparams(("arbitrary", "arbitrary")),
        name="cast_bf16",
    )(*ws)


def _mods_kernel(c_ref, w_ref, b_ref, o_ref):
    c = c_ref[...]
    s = (c * jax.nn.sigmoid(c)).astype(BF16)
    o_ref[0] = _dot(s, w_ref[0].astype(BF16)) + b_ref[0]


def _mods(cc, w_ada, b_ada):
    tn = 1536
    return pl.pallas_call(
        _mods_kernel,
        grid=(DEPTH, N_MOD // tn),
        in_specs=[
            pl.BlockSpec((8, D_MODEL), lambda l, j: (0, 0)),
            pl.BlockSpec((1, D_MODEL, tn), lambda l, j: (l, 0, j)),
            pl.BlockSpec((1, 1, tn), lambda l, j: (l, 0, j)),
        ],
        out_specs=pl.BlockSpec((1, 8, tn), lambda l, j: (l, 0, j)),
        out_shape=jax.ShapeDtypeStruct((DEPTH, 8, N_MOD), F32),
        compiler_params=_cparams(("arbitrary", "arbitrary")),
        name="adaln_mods",
    )(cc, w_ada, b_ada.reshape(DEPTH, 1, N_MOD))


def _inproj_kernel(xl_ref, xc_ref, mod_ref, g_ref, w_ref, wkt_ref, wgt_ref, cos_ref, sin_ref,
                   u_ref, kt_ref, fn_ref, gate_ref, xn_ref, *, rows_per_batch, n_batch):
    i = pl.program_id(0)
    j = pl.program_id(1)

    def normalise(x_ref):
        y = _rms(x_ref[...], g_ref[...])
        shift = _mod_row(mod_ref, i, rows_per_batch, n_batch, 0)
        scale = _mod_row(mod_ref, i, rows_per_batch, n_batch, 1)
        xn_ref[...] = (y * (1.0 + scale) + shift).astype(BF16)
        gate_ref[...] = _dot_nt(wgt_ref[...], xn_ref[...])

    is_lat = i < rows_per_batch * n_batch

    @pl.when(jnp.logical_and(j == 0, is_lat))
    def _():
        normalise(xl_ref)

    @pl.when(jnp.logical_and(j == 0, jnp.logical_not(is_lat)))
    def _():
        normalise(xc_ref)

    def product(h):
        return _dot(xn_ref[...], w_ref[:, h * D_MODEL:(h + 1) * D_MODEL])

    def plain(h):
        u_ref[:, h * D_MODEL:(h + 1) * D_MODEL] = product(h).astype(BF16)

    def rope(h):
        acc = product(h)
        cos = cos_ref[...]
        sin = sin_ref[...]
        for t in range(acc.shape[1] // DA_DV):
            x = acc[:, t * DA_DV:(t + 1) * DA_DV]
            lo = h * D_MODEL + t * DA_DV
            u_ref[:, lo:lo + DA_DV] = (x * cos + pltpu.roll(x, DA_DV // 2, 1) * sin).astype(BF16)

    def fourier_in(h):
        acc = product(h)
        for g in range(FN_GROUPS):
            z = acc[:, g * FN_GC:(g + 1) * FN_GC]
            fn_ref[g] = z.reshape(z.shape[0] // FFT_R, FFT_R * FN_GC).astype(BF16)

    def keys_t():
        kt_ref[...] = (_dot_nt(wkt_ref[...], xn_ref[...]) * (ML_DK ** -0.5)).astype(BF16)

    epilogues = {U_QML: plain, U_VML: plain, U_OML: plain, U_QDA: rope, U_KDA: rope, U_VDA: plain,
                 U_GPRE: plain, U_GPRE + 1: plain, U_GPRE + 2: plain, U_BLOCKS: fourier_in}
    for step in range(N_COL_STEPS):
        @pl.when(j == step)
        def _(step=step):
            for h in range(2):
                epilogues[2 * step + h](h)
            if step == STEP_KT:
                keys_t()


def _inproj(x_lat, x_ctx, ctx_blk, mods, g, w_main, w_kt, w_gate_t, layer, cos_t, sin_t, n_batch):
    tm = n_batch * CTX_LEN
    assert SEQ % tm == 0 and cos_t.shape[0] == SEQ + tm
    n_rows = n_batch * (SEQ + CTX_LEN)
    ni = n_rows // tm
    rows_per_batch = SEQ // tm
    lat_blocks = n_batch * rows_per_batch

    def tab_idx(i, j):
        return (jnp.where(i < lat_blocks, i % rows_per_batch, rows_per_batch), 0)

    kern = functools.partial(_inproj_kernel, rows_per_batch=rows_per_batch, n_batch=n_batch)
    return pl.pallas_call(
        kern,
        grid=(ni, N_COL_STEPS),
        in_specs=[
            pl.BlockSpec((tm, D_MODEL), lambda i, j: (jnp.minimum(i, lat_blocks - 1), 0)),
            pl.BlockSpec((tm, D_MODEL), lambda i, j: (ctx_blk, 0)),
            pl.BlockSpec((None, 8, N_MOD), lambda i, j: (layer, 0, 0)),
            pl.BlockSpec((1, D_MODEL), lambda i, j: (0, 0)),
            pl.BlockSpec((None, D_MODEL, 2 * D_MODEL), lambda i, j: (layer, 0, j)),
            pl.BlockSpec((None, D_MODEL, D_MODEL), lambda i, j: (layer, 0, 0)),
            pl.BlockSpec((None, N_GATE, D_MODEL), lambda i, j: (layer, 0, 0)),
            pl.BlockSpec((tm, DA_DV), tab_idx),
            pl.BlockSpec((tm, DA_DV), tab_idx),
        ],
        out_specs=[
            pl.BlockSpec((tm, 2 * D_MODEL), lambda i, j: (i, j)),
            pl.BlockSpec((D_MODEL, tm), lambda i, j: (0, i)),
            pl.BlockSpec((FN_GROUPS, tm // FFT_R, FFT_R * FN_GC), lambda i, j: (0, i, 0)),
            pl.BlockSpec((N_GATE, tm), lambda i, j: (0, i)),
        ],
        out_shape=[
            jax.ShapeDtypeStruct((n_rows, U_BLOCKS * D_MODEL), BF16),
            jax.ShapeDtypeStruct((D_MODEL, n_rows), BF16),
            jax.ShapeDtypeStruct((FN_GROUPS, n_rows // FFT_R, FFT_R * FN_GC), BF16),
            jax.ShapeDtypeStruct((N_GATE, n_rows), F32),
        ],
        scratch_shapes=[pltpu.VMEM((tm, D_MODEL), BF16)],
        compiler_params=_cparams(("arbitrary", "arbitrary")),
        name="in_proj",
    )(x_lat, x_ctx, mods, g, w_main, w_kt, w_gate_t, cos_t, sin_t)


def _split3(x):
    hi = x.astype(BF16).astype(F32)
    mid = (x - hi).astype(BF16).astype(F32)
    lo = (x - hi - mid).astype(BF16).astype(F32)
    return hi, mid, lo


def _mlstm_kernel(qf_ref, ktf_ref, vf_ref, gtf_ref, gtfn_ref, qb_ref, ktb_ref, vb_ref, gtb_ref, gtbn_ref, bias_ref,
                  hf_ref, hb_ref, *scratch):
    n_st = 2 * ML_HEADS
    cx_refs, bw_refs, pmw_refs = scratch[:n_st], scratch[n_st:2 * n_st], scratch[2 * n_st:3 * n_st]
    m_refs, c_refs, bend_refs, g_refs = (scratch[3 * n_st + 2 * k:3 * n_st + 2 * k + 2] for k in range(4))
    state_refs = cx_refs + m_refs
    s = pl.program_id(1)
    L = ML_CHUNK
    H = ML_HEADS
    W = LANES_V7X

    @pl.when(s == 0)
    def _():
        for ref in state_refs:
            ref[...] = jnp.zeros_like(ref)

    t_idx = lax.broadcasted_iota(jnp.int32, (L, L), 0)
    s_idx = lax.broadcasted_iota(jnp.int32, (L, L), 1)
    sub8 = lax.broadcasted_iota(jnp.int32, (8, W), 0)
    ones_w = jnp.ones((L, W), BF16)
    er = lax.broadcasted_iota(jnp.int32, (4 * H, 2 * W), 0)
    ec = lax.broadcasted_iota(jnp.int32, (4 * H, 2 * W), 1)

    def running_max_rows(x, d):
        n_tiles = L // 8
        out = [None] * n_tiles
        carry = None
        for j in (range(n_tiles) if d == 0 else range(n_tiles - 1, -1, -1)):
            r = x[8 * j:8 * (j + 1)]
            k = 1
            while k < 8:
                if d == 0:
                    r = jnp.maximum(r, jnp.where(sub8 >= k, pltpu.roll(r, k, 0), -jnp.inf))
                else:
                    r = jnp.maximum(r, jnp.where(sub8 < 8 - k, pltpu.roll(r, 8 - k, 0), -jnp.inf))
                k *= 2
            if carry is not None:
                r = jnp.maximum(r, carry)
            carry = jnp.broadcast_to(r[7:8] if d == 0 else r[0:1], (8, W))
            out[j] = r
        return jnp.concatenate(out, axis=0)

    def gate_part(d, gt_ref):
        before = (t_idx <= s_idx) if d == 0 else (t_idx >= s_idx)
        gt = gt_ref[2 * H * d:2 * H * (d + 1), :] + bias_ref[2 * H * d:2 * H * (d + 1), :]
        i4, f4 = gt[0:H], gt[H:2 * H]
        lf4 = jnp.minimum(f4, 0.0) - jnp.log1p(jnp.exp(-jnp.abs(f4)))
        lf_terms = jnp.concatenate(list(_split3(lf4)) + [jnp.zeros((H, L), F32)], axis=0).astype(BF16)
        cum_rhs = jnp.concatenate([jnp.where(before, 1.0, 0.0).astype(BF16), ones_w], axis=1)
        r = _dot(lf_terms, cum_rhs)
        bx = r[0:H] + r[H:2 * H] + r[2 * H:3 * H]
        b4, bend4 = bx[:, 0:L], bx[:, L:L + W]
        c4 = i4 - b4
        c_refs[d][...] = c4
        bend_refs[d][...] = bend4
        g_refs[d][...] = bend4[:, 0:1] - b4 + i4
        cols = jnp.concatenate(_split3(b4) + (c4.astype(BF16).astype(F32),), axis=0).T.astype(BF16)
        for h in range(H):
            pick = jnp.where((er % H == h) & ((er < 3 * H) == (ec < W)), 1.0, 0.0).astype(BF16)
            wide = _dot(cols, pick)
            bw_refs[d * H + h][...] = wide[:, 0:W]
            pmw_refs[d * H + h][...] = running_max_rows(wide[:, W:2 * W], d)

    dirs = ((0, qf_ref, ktf_ref, vf_ref, gtf_ref, gtfn_ref, hf_ref),
            (1, qb_ref, ktb_ref, vb_ref, gtb_ref, gtbn_ref, hb_ref))

    @pl.when(s == 0)
    def _():
        for d, _, _, _, gt_ref, _, _ in dirs:
            gate_part(d, gt_ref)

    for d, q_ref, kt_ref, v_ref, _, _, h_ref in dirs:
        causal = (s_idx <= t_idx) if d == 0 else (s_idx >= t_idx)
        c4, bend4, g4 = c_refs[d][...], bend_refs[d][...], g_refs[d][...]
        m_prev4 = m_refs[d][...]
        m_new4 = jnp.maximum(bend4 + m_prev4, jnp.max(g4, axis=1, keepdims=True))
        m_refs[d][...] = m_new4
        decay4 = jnp.exp(bend4 + m_prev4 - m_new4)
        w4 = jnp.exp(g4 - m_new4[:, 0:1])
        for h in range(H):
            st = d * H + h
            hs = slice(h * ML_DK, (h + 1) * ML_DK)
            b_w = bw_refs[st][...]
            m_w = jnp.maximum(pmw_refs[st][...], m_prev4[h:h + 1, :])

            q = q_ref[:, hs]
            kt = kt_ref[hs, :]
            vx = jnp.concatenate([v_ref[:, hs], ones_w], axis=1)
            m_ll = jnp.concatenate([m_w] * (L // W), axis=1)
            a = (jnp.where(causal, jnp.exp(c4[h:h + 1, :] - m_ll), 0.0) * _dot(q, kt)).astype(BF16)
            cx_prev = cx_refs[st][...]
            qc = _dot(q, cx_prev.astype(BF16))
            av = _dot(a, vx)
            sc_w = jnp.exp(m_prev4[h:h + 1, :] - m_w)
            den = sc_w * qc[:, ML_DK:] + av[:, ML_DK:]
            inv = 1.0 / jnp.maximum(jnp.abs(den), jnp.exp(-(b_w + m_w)))
            for t in range(ML_DK // W):
                ts = slice(t * W, (t + 1) * W)
                h_ref[:, h * ML_DK + t * W:h * ML_DK + (t + 1) * W] = (
                    (sc_w * qc[:, ts] + av[:, ts]) * inv).astype(h_ref.dtype)

            kw = (kt.astype(F32) * w4[h:h + 1, :]).astype(BF16)
            dec = jnp.concatenate([decay4[h:h + 1, :]] * (ML_DK // W + 1), axis=1)
            cx_refs[st][...] = dec * cx_prev + _dot(kw, vx)

    for d, _, _, _, _, gtn_ref, _ in dirs:
        gate_part(d, gtn_ref)


def _mlstm(u, kt, gates_t, gate_b, n_batch):
    n_rows = u.shape[0]
    L = ML_CHUNK
    lat_chunks = SEQ // L
    ctx_chunks = CTX_LEN // L
    n_steps = ctx_chunks + lat_chunks
    ctx_base = n_batch * lat_chunks

    def rowblk(d):
        def f(b, s):
            in_ctx = s < ctx_chunks
            if d == 0:
                c = jnp.where(in_ctx, s, s - ctx_chunks)
            else:
                c = jnp.where(in_ctx, ctx_chunks - 1 - s, lat_chunks - 1 - (s - ctx_chunks))
            return jnp.where(in_ctx, ctx_base + ctx_chunks * b, lat_chunks * b) + c
        return f

    def dir_specs(d):
        rb = rowblk(d)
        return [
            pl.BlockSpec((L, D_MODEL), lambda b, s: (rb(b, s), U_QML)),
            pl.BlockSpec((D_MODEL, L), lambda b, s: (0, rb(b, s))),
            pl.BlockSpec((L, D_MODEL), lambda b, s: (rb(b, s), U_VML)),
            pl.BlockSpec((N_GATE, L), lambda b, s: (0, rb(b, s))),
            pl.BlockSpec((N_GATE, L), lambda b, s: (0, rb(b, jnp.minimum(s + 1, n_steps - 1)))),
        ]

    def out_spec(d):
        rb = rowblk(d)
        return pl.BlockSpec((L, D_MODEL), lambda b, s: (rb(b, s), 0))

    n_st = 2 * ML_HEADS
    return pl.pallas_call(
        _mlstm_kernel,
        grid=(n_batch, n_steps),
        in_specs=dir_specs(0) + dir_specs(1) + [pl.BlockSpec((N_GATE, 1), lambda b, s: (0, 0))],
        out_specs=[out_spec(0), out_spec(1)],
        out_shape=[jax.ShapeDtypeStruct((n_rows, D_MODEL), BF16)] * 2,
        scratch_shapes=(
            [pltpu.VMEM((ML_DK, ML_DK + LANES_V7X), F32)] * n_st
            + [pltpu.VMEM((L, LANES_V7X), F32)] * (2 * n_st)
            + [pltpu.VMEM((ML_HEADS, LANES_V7X), F32)] * 2
            + [pltpu.VMEM((ML_HEADS, L), F32)] * 2
            + [pltpu.VMEM((ML_HEADS, LANES_V7X), F32)] * 2
            + [pltpu.VMEM((ML_HEADS, L), F32)] * 2),
        compiler_params=_cparams(("arbitrary", "arbitrary")),
        name="mlstm_scan",
    )(u, kt, u, gates_t, gates_t, u, kt, u, gates_t, gates_t, gate_b.reshape(N_GATE, 1))


def _attn_lambda(lam_ref, lam_init):
    lq = lam_ref[...]
    return (jnp.exp(jnp.sum(lq[0:1] * lq[1:2], axis=1, keepdims=True))
            - jnp.exp(jnp.sum(lq[2:3] * lq[3:4], axis=1, keepdims=True)) + lam_init)


def _attn_queries(q_ref):
    q = q_ref[...]
    lane = lax.broadcasted_iota(jnp.int32, (1, DA_DV), 1)
    zero = jnp.zeros_like(q)
    is_map0 = (lane % DA_DH) < (DA_DH // 2)
    q2 = jnp.concatenate([jnp.where(is_map0, q, zero), jnp.where(is_map0, zero, q)], axis=0)
    return q2 * (DA_DH ** -0.5)


def _attn_sums(s_chunks, m_chunks, vx):
    return jnp.concatenate([_dot(jnp.exp(s - m).astype(BF16), vx) for s, m in zip(s_chunks, m_chunks)], axis=0)


def _attn_finish(acc, lam, g, lam_init, tq):
    o0 = acc[0:tq, 0:DA_DV] * (1.0 / acc[0:tq, DA_DV:DA_DV + 1])
    o1 = acc[tq:, 0:DA_DV] * (1.0 / acc[tq:, DA_DV:DA_DV + 1])
    return _rms(o0 - lam * o1, g) * (1.0 - lam_init)


def _attn_kernel(qa_ref, kla_ref, kca_ref, qb_ref, klb_ref, kcb_ref, vl_ref, vc_ref, lam_ref, g_ref, o_ref,
                 vx_ref, s_ref, m_ref, *, lam_init, q_blocks):
    t = pl.program_id(0)
    tq = qa_ref.shape[0]
    rows = 2 * tq // ATTN_ROW_CHUNKS

    @pl.when(jnp.logical_and(t > 0, (2 * t - 2) % q_blocks == 0))
    def _():
        vx_ref[0:SEQ, 0:DA_DV] = vl_ref[...]
        vx_ref[SEQ:, 0:DA_DV] = vc_ref[...]
        lane_v = lax.broadcasted_iota(jnp.int32, (SEQ + CTX_LEN, DA_DV), 1)
        vx_ref[:, DA_DV:] = jnp.where(lane_v == 0, 1.0, 0.0).astype(BF16)

    def score(slot, q_ref, kl_ref, kc_ref):
        q2 = _attn_queries(q_ref)
        s_lat = _dot_nt(q2, kl_ref[...])
        s_ctx = _dot_nt(q2, kc_ref[...])
        m_ref[slot] = jnp.maximum(jnp.max(s_lat, axis=1, keepdims=True), jnp.max(s_ctx, axis=1, keepdims=True))
        s_ref[slot, :, 0:SEQ] = s_lat
        s_ref[slot, :, SEQ:] = s_ctx

    def drain(slot):
        s_chunks = [s_ref[slot, c * rows:(c + 1) * rows, :] for c in range(ATTN_ROW_CHUNKS)]
        m_chunks = [m_ref[slot, c * rows:(c + 1) * rows, :] for c in range(ATTN_ROW_CHUNKS)]
        acc = _attn_sums(s_chunks, m_chunks, vx_ref[...])
        o = _attn_finish(acc, _attn_lambda(lam_ref, lam_init), g_ref[...], lam_init, tq)
        o_ref[slot * tq:(slot + 1) * tq, :] = o.astype(o_ref.dtype)

    @pl.when(t > 0)
    def _():
        drain(0)
        score(1, qa_ref, kla_ref, kca_ref)

    @pl.when(t == 0)
    def _():
        score(0, qb_ref, klb_ref, kcb_ref)

    @pl.when(t > 0)
    def _():
        drain(1)
        score(0, qb_ref, klb_ref, kcb_ref)


def _attn_ctx_kernel(q_ref, kc_ref, vc_ref, lam_ref, g_ref, yin_ref, o_ref, *, lam_init):
    del yin_ref
    tq = q_ref.shape[0]
    s = _dot_nt(_attn_queries(q_ref), kc_ref[...])
    lane_v = lax.broadcasted_iota(jnp.int32, (CTX_LEN, DA_DV), 1)
    vx = jnp.concatenate([vc_ref[...], jnp.where(lane_v == 0, 1.0, 0.0).astype(BF16)], axis=1)
    acc = _attn_sums([s], [jnp.max(s, axis=1, keepdims=True)], vx)
    o = _attn_finish(acc, _attn_lambda(lam_ref, lam_init), g_ref[...], lam_init, tq)
    o_ref[...] = o.astype(o_ref.dtype)


def _attn(u, da_lam, head_g, lam_init, need_ctx, n_batch):
    n_rows = u.shape[0]
    tq = ATTN_TQ
    lat_qblocks = SEQ // tq
    ctx_base = n_batch * (SEQ // CTX_LEN)
    cpb = D_MODEL // DA_DV
    n_keys = SEQ + CTX_LEN

    n_items = n_batch * DA_HEADS * lat_qblocks
    assert lat_qblocks % 2 == 0

    def item(k):
        k = jnp.clip(k, 0, n_items - 1)
        return k // (DA_HEADS * lat_qblocks), (k // lat_qblocks) % DA_HEADS, k % lat_qblocks

    def of_item(offset, f):
        return lambda t: f(*item(2 * t + offset))

    def scored_specs(offset):
        return [
            pl.BlockSpec((tq, DA_DV), of_item(offset, lambda b, h, qi: (b * lat_qblocks + qi, U_QDA * cpb + h))),
            pl.BlockSpec((SEQ, DA_DV), of_item(offset, lambda b, h, qi: (b, U_KDA * cpb + h))),
            pl.BlockSpec((CTX_LEN, DA_DV), of_item(offset, lambda b, h, qi: (ctx_base + b, U_KDA * cpb + h))),
        ]

    kern = functools.partial(_attn_kernel, lam_init=lam_init, q_blocks=lat_qblocks)
    yd = pl.pallas_call(
        kern,
        grid=(n_items // 2 + 1,),
        in_specs=scored_specs(-1) + scored_specs(0) + [
            pl.BlockSpec((SEQ, DA_DV), of_item(-2, lambda b, h, qi: (b, U_VDA * cpb + h))),
            pl.BlockSpec((CTX_LEN, DA_DV), of_item(-2, lambda b, h, qi: (ctx_base + b, U_VDA * cpb + h))),
            pl.BlockSpec((4, DA_DH), lambda t: (0, 0)),
            pl.BlockSpec((1, DA_DV), of_item(-2, lambda b, h, qi: (0, h))),
        ],
        out_specs=pl.BlockSpec((2 * tq, DA_DV),
                               of_item(-2, lambda b, h, qi: ((b * lat_qblocks + qi) // 2, h))),
        out_shape=jax.ShapeDtypeStruct((n_rows, D_MODEL), BF16),
        scratch_shapes=[
            pltpu.VMEM((n_keys, 2 * DA_DV), BF16),
            pltpu.VMEM((2, 2 * tq, n_keys), F32),
            pltpu.VMEM((2, 2 * tq, 1), F32),
        ],
        compiler_params=_cparams(("arbitrary",)),
        name="diff_attn",
    )(u, u, u, u, u, u, u, u, da_lam, head_g)
    if not need_ctx:
        return yd
    kern_ctx = functools.partial(_attn_ctx_kernel, lam_init=lam_init)
    return pl.pallas_call(
        kern_ctx,
        grid=(n_batch, DA_HEADS),
        in_specs=[
            pl.BlockSpec((CTX_LEN, DA_DV), lambda b, h: (ctx_base + b, U_QDA * cpb + h)),
            pl.BlockSpec((CTX_LEN, DA_DV), lambda b, h: (ctx_base + b, U_KDA * cpb + h)),
            pl.BlockSpec((CTX_LEN, DA_DV), lambda b, h: (ctx_base + b, U_VDA * cpb + h)),
            pl.BlockSpec((4, DA_DH), lambda b, h: (0, 0)),
            pl.BlockSpec((1, DA_DV), lambda b, h: (0, h)),
            pl.BlockSpec(memory_space=pl.ANY),
        ],
        out_specs=pl.BlockSpec((CTX_LEN, DA_DV), lambda b, h: (ctx_base + b, h)),
        out_shape=jax.ShapeDtypeStruct((n_rows, D_MODEL), BF16),
        input_output_aliases={5: 0},
        compiler_params=_cparams(("arbitrary", "arbitrary")),
        name="diff_attn_ctx",
    )(u, u, u, da_lam, head_g, yd)


def _dft_tables():
    R = FFT_R

    def cs(num, period):
        ang = (num % period).astype(F32) * (2.0 * math.pi / period)
        return jnp.cos(ang), jnp.sin(ang)

    idx = jnp.arange(R, dtype=jnp.int32)
    c1, s1 = cs(idx[:, None] * idx[None, :], R)
    a1 = jnp.concatenate([c1, -s1], axis=0).astype(BF16)
    f2 = idx[:, None, None]
    f1 = idx[None, :, None]
    t1 = idx[None, None, :]
    mc, ms = cs(t1 * (R * f1 + f2), SEQ)
    b2 = jnp.concatenate([jnp.concatenate([mc, ms], axis=2),
                          jnp.concatenate([-ms, mc], axis=2)], axis=1).astype(BF16)
    ch = jnp.arange(FN_GC, dtype=jnp.int32)
    cc, sc = cs(ch[:, None] * ch[None, :], FN_GC)
    csm = jnp.concatenate([cc, sc], axis=0).astype(BF16)
    actx = jnp.concatenate([cc, -sc], axis=0).astype(BF16)
    return a1, b2, csm, actx


def _fft1_kernel(a_ref, z_ref, p_ref):
    p_ref[0, 0] = _dot(a_ref[...], z_ref[0]).astype(p_ref.dtype)


def _fft1(fn_view, a1, n_batch):
    R = FFT_R
    lanes = R * FN_GC
    lc = lanes
    return pl.pallas_call(
        _fft1_kernel,
        grid=(FN_GROUPS, n_batch, lanes // lc),
        in_specs=[
            pl.BlockSpec((2 * R, R), lambda g, b, c: (0, 0)),
            pl.BlockSpec((1, R, lc), lambda g, b, c: (g, b, c)),
        ],
        out_specs=pl.BlockSpec((1, 1, 2 * R, lc), lambda g, b, c: (g, b, 0, c)),
        out_shape=jax.ShapeDtypeStruct((FN_GROUPS, n_batch, 2 * R, lanes), BF16),
        compiler_params=_cparams(("arbitrary", "arbitrary", "arbitrary")),
        name="fourier_stage1",
    )(a1, fn_view)


def _fft2_kernel(p_ref, b_ref, cs_ref, o_ref, *, f2b):
    R = FFT_R
    cc = cs_ref[0:FN_GC, :]
    sc = cs_ref[FN_GC:2 * FN_GC, :]
    norm = 1.0 / math.sqrt(SEQ * FN_GC)
    for g in range(FN_GROUPS):
        xr, xi = [], []
        for jj in range(f2b):
            stacked = jnp.concatenate([p_ref[g, 0, 0, jj], p_ref[g, 0, 1, jj]], axis=0)
            x = _dot(b_ref[jj], stacked)
            xr.append(x[0:R])
            xi.append(x[R:2 * R])
        y = (_dot(jnp.concatenate(xr, axis=0).astype(BF16), cc)
             + _dot(jnp.concatenate(xi, axis=0).astype(BF16), sc)) * norm
        for jj in range(f2b):
            lo = jj * D_MODEL + g * FN_GC
            o_ref[:, lo:lo + FN_GC] = y[jj * R:(jj + 1) * R].astype(o_ref.dtype)


def _fft2(p6, b2, csm, n_rows, n_batch):
    R = FFT_R
    f2b = 16
    kern = functools.partial(_fft2_kernel, f2b=f2b)
    return pl.pallas_call(
        kern,
        grid=(n_batch, R // f2b),
        in_specs=[
            pl.BlockSpec((FN_GROUPS, 1, 2, f2b, R, FN_GC), lambda b, f: (0, b, 0, f, 0, 0)),
            pl.BlockSpec((f2b, 2 * R, 2 * R), lambda b, f: (f, 0, 0)),
            pl.BlockSpec((2 * FN_GC, FN_GC), lambda b, f: (0, 0)),
        ],
        out_specs=pl.BlockSpec((R, f2b * D_MODEL), lambda b, f: (b, f)),
        out_shape=jax.ShapeDtypeStruct((n_rows // R, R * D_MODEL), BF16),
        compiler_params=_cparams(("arbitrary", "arbitrary")),
        name="fourier_stage2",
    )(p6, b2, csm)


def _fft_ctx_kernel(z_ref, a_ref, cs_ref, yin_ref, o_ref, *, n_batch):
    del yin_ref
    cc = cs_ref[0:FN_GC, :]
    sc = cs_ref[FN_GC:2 * FN_GC, :]
    norm = 1.0 / math.sqrt(CTX_LEN * FN_GC)
    n_ctx = n_batch * CTX_LEN
    groups = []
    for g in range(FN_GROUPS):
        z_all = z_ref[g].astype(F32).reshape(n_ctx, FN_GC)
        ys = []
        for b in range(n_batch):
            p = _dot(a_ref[...], z_all[b * CTX_LEN:(b + 1) * CTX_LEN].astype(BF16))
            ys.append(_dot(p[0:CTX_LEN].astype(BF16), cc) + _dot(p[CTX_LEN:2 * CTX_LEN].astype(BF16), sc))
        groups.append(jnp.concatenate(ys, axis=0))
    y = jnp.concatenate(groups, axis=1) * norm
    o_ref[...] = y.reshape(n_ctx // FFT_R, FFT_R * D_MODEL).astype(o_ref.dtype)


def _fft_ctx(fn_view, actx, csm, yf, n_batch):
    n_ctx = n_batch * CTX_LEN
    ctx_blk = n_batch * SEQ // n_ctx
    kern = functools.partial(_fft_ctx_kernel, n_batch=n_batch)
    return pl.pallas_call(
        kern,
        grid=(1,),
        in_specs=[
            pl.BlockSpec((FN_GROUPS, n_ctx // FFT_R, FFT_R * FN_GC), lambda i: (0, ctx_blk, 0)),
            pl.BlockSpec((2 * CTX_LEN, CTX_LEN), lambda i: (0, 0)),
            pl.BlockSpec((2 * FN_GC, FN_GC), lambda i: (0, 0)),
            pl.BlockSpec(memory_space=pl.ANY),
        ],
        out_specs=pl.BlockSpec((n_ctx // FFT_R, FFT_R * D_MODEL), lambda i: (ctx_blk, 0)),
        out_shape=jax.ShapeDtypeStruct(yf.shape, BF16),
        input_output_aliases={3: 0},
        compiler_params=_cparams(("arbitrary",)),
        name="fourier_ctx",
    )(fn_view, actx, csm, yf)


def _fourier(fn, tables, need_ctx, n_batch):
    a1, b2, csm, actx = tables
    R = FFT_R
    n_rows = fn.shape[1] * R
    p = _fft1(fn, a1, n_batch)
    yf = _fft2(p.reshape(FN_GROUPS, n_batch, 2, R, R, FN_GC), b2, csm, n_rows, n_batch)
    if need_ctx:
        yf = _fft_ctx(fn, actx, csm, yf, n_batch)
    return yf


def _merge_kernel(hf_ref, hb_ref, o_ref, gm_ref, gd_ref, gf_ref, yd_ref, yf_ref, xl_ref, xc_ref, mod_ref, hg_ref,
                  wml_ref, wda_ref, wfn_ref, wout_ref, out_ref, *, rows_per_batch, n_batch):
    i = pl.program_id(0)
    x = jnp.where(i < rows_per_batch * n_batch, xl_ref[...], xc_ref[...])
    hsum = hf_ref[...].astype(F32) + hb_ref[...].astype(F32)
    hg = hg_ref[...]
    parts = []
    for h in range(ML_HEADS):
        hs = slice(h * ML_DK, (h + 1) * ML_DK)
        parts.append(_rms(hsum[:, hs], hg[:, hs]))
    ym = (jnp.concatenate(parts, axis=1) * jax.nn.sigmoid(o_ref[...].astype(F32))).astype(BF16)
    yf = yf_ref[...].astype(F32).reshape(x.shape).astype(BF16)
    y = (jax.nn.sigmoid(gm_ref[...].astype(F32)) * _dot(ym, wml_ref[...])
         + jax.nn.sigmoid(gd_ref[...].astype(F32)) * _dot(yd_ref[...], wda_ref[...])
         + jax.nn.sigmoid(gf_ref[...].astype(F32)) * _dot(yf, wfn_ref[...]))
    gate = _mod_row(mod_ref, i, rows_per_batch, n_batch, 2)
    out_ref[...] = x + gate * _dot(y.astype(BF16), wout_ref[...])


def _merge(hf, hb, u, yd, yf, x_lat, x_ctx, ctx_blk0, mods, head_g, wml, wda, wfn, wout, layer, need_ctx, n_batch):
    n_rows = n_batch * (SEQ + CTX_LEN)
    tm = _row_tile(n_batch)
    rows_per_batch = SEQ // tm
    lat_blocks = n_batch * rows_per_batch
    ni = (n_rows if need_ctx else n_batch * SEQ) // tm
    kern = functools.partial(_merge_kernel, rows_per_batch=rows_per_batch, n_batch=n_batch)
    row = lambda i: (i, 0)
    full = lambda i: (0, 0)
    wspec = pl.BlockSpec((None, D_MODEL, D_MODEL), lambda i: (layer, 0, 0))
    return pl.pallas_call(
        kern,
        grid=(ni,),
        in_specs=[
            pl.BlockSpec((tm, D_MODEL), row),
            pl.BlockSpec((tm, D_MODEL), row),
            pl.BlockSpec((tm, D_MODEL), lambda i: (i, U_OML)),
            pl.BlockSpec((tm, D_MODEL), lambda i: (i, U_GPRE)),
            pl.BlockSpec((tm, D_MODEL), lambda i: (i, U_GPRE + 1)),
            pl.BlockSpec((tm, D_MODEL), lambda i: (i, U_GPRE + 2)),
            pl.BlockSpec((tm, D_MODEL), row),
            pl.BlockSpec((tm // FFT_R, FFT_R * D_MODEL), row),
            pl.BlockSpec((tm, D_MODEL), lambda i: (jnp.minimum(i, lat_blocks - 1), 0)),
            pl.BlockSpec((tm, D_MODEL), lambda i: (ctx_blk0 + jnp.maximum(i - lat_blocks, 0), 0)),
            pl.BlockSpec((None, 8, N_MOD), lambda i: (layer, 0, 0)),
            pl.BlockSpec((1, D_MODEL), full),
            wspec, wspec, wspec, wspec,
        ],
        out_specs=pl.BlockSpec((tm, D_MODEL), row),
        out_shape=jax.ShapeDtypeStruct((ni * tm, D_MODEL), F32),
        compiler_params=_cparams(("arbitrary",)),
        name="merge_out_proj",
    )(hf, hb, u, u, u, u, yd, yf, x_lat, x_ctx, mods, head_g, wml, wda, wfn, wout)


FFN_CHUNKS = ((0, 1024), (1024, 1024), (2048, 768))


def _ffn_kernel(x_ref, mod_ref, g_ref, win_ref, wout_ref, fg_ref, out_ref, *, rows_per_batch, n_batch, final):
    i = pl.program_id(0)
    x = x_ref[...]
    shift = _mod_row(mod_ref, i, rows_per_batch, n_batch, 3)
    scale = _mod_row(mod_ref, i, rows_per_batch, n_batch, 4)
    gate = _mod_row(mod_ref, i, rows_per_batch, n_batch, 5)
    h = (_rms(x, g_ref[...]) * (1.0 + scale) + shift).astype(BF16)
    acc = None
    for lo, width in FFN_CHUNKS:
        a = _dot(h, win_ref[:, lo:lo + width])
        b = _dot(h, win_ref[:, D_FF + lo:D_FF + lo + width])
        act = (a * jax.nn.sigmoid(a) * b).astype(BF16)
        part = _dot(act, wout_ref[lo:lo + width, :])
        acc = part if acc is None else acc + part
    xn = x + gate * acc
    out_ref[...] = _rms(xn, fg_ref[...]) if final else xn


def _ffn(x, mods, g, w_in, w_out, layer, final_g, final, n_rows_out, n_batch):
    tm = n_batch * CTX_LEN
    rows_per_batch = SEQ // tm
    resident = pl.Buffered(1)
    kern = functools.partial(_ffn_kernel, rows_per_batch=rows_per_batch, n_batch=n_batch, final=final)
    row = lambda i: (i, 0)
    full = lambda i: (0, 0)
    return pl.pallas_call(
        kern,
        grid=(n_rows_out // tm,),
        in_specs=[
            pl.BlockSpec((tm, D_MODEL), row),
            pl.BlockSpec((None, 8, N_MOD), lambda i: (layer, 0, 0)),
            pl.BlockSpec((1, D_MODEL), full),
            pl.BlockSpec((None, D_MODEL, 2 * D_FF), lambda i: (layer, 0, 0), pipeline_mode=resident),
            pl.BlockSpec((None, D_FF, D_MODEL), lambda i: (layer, 0, 0), pipeline_mode=resident),
            pl.BlockSpec((1, D_MODEL), full),
        ],
        out_specs=pl.BlockSpec((tm, D_MODEL), row),
        out_shape=jax.ShapeDtypeStruct((n_rows_out, D_MODEL), F32),
        compiler_params=_cparams(("arbitrary",)),
        name="swiglu_ffn",
    )(x, mods, g, w_in, w_out, final_g)


def _da_col_perm(w):
    half = DA_DH // 2
    lead = w.shape[:-1]
    return jnp.swapaxes(w.reshape(lead + (DA_HEADS, 2, 2, half)), -3, -2).reshape(lead + (DA_HEADS * DA_DV,))


def _rope_tables(pad):
    n_freq = DA_DH // 4
    rows = SEQ // GRID_W
    inv = ROPE_BASE ** (-jnp.arange(n_freq, dtype=F32) / n_freq)
    r = jnp.repeat(jnp.arange(rows, dtype=F32), GRID_W)
    col = jnp.tile(jnp.arange(GRID_W, dtype=F32), rows)
    ang = jnp.concatenate([r[:, None] * inv, col[:, None] * inv], axis=-1)
    cos, sin = jnp.cos(ang), jnp.sin(ang)
    cos_t = jnp.concatenate([cos, cos, cos, cos], axis=-1)
    sin_t = jnp.concatenate([-sin, -sin, sin, sin], axis=-1)
    cos_t = jnp.concatenate([cos_t, jnp.ones((pad, DA_DV), F32)], axis=0)
    sin_t = jnp.concatenate([sin_t, jnp.zeros((pad, DA_DV), F32)], axis=0)
    return cos_t, sin_t


def kernel(x, c, ctx, c_ctx, w_ada, b_ada, norm_g, w_in, ml_gate_b, ml_head_g, da_lam, da_head_g,
           w_br_ml, w_br_da, w_br_fn, w_out, w_ffn_in, w_ffn_out, final_g):
    n_batch = x.shape[0]
    n_lat = n_batch * SEQ
    x_lat = x.reshape(n_lat, D_MODEL)
    x_ctx = ctx.reshape(n_batch * CTX_LEN, D_MODEL)
    cc = jnp.concatenate([c, c_ctx[None, :], jnp.zeros((8 - n_batch - 1, D_MODEL), F32)], axis=0)
    mods = _mods(cc, w_ada, b_ada)
    cos_t, sin_t = _rope_tables(n_batch * CTX_LEN)
    tables = _dft_tables()
    final_g2 = final_g.reshape(1, D_MODEL)
    wb_ml, wb_da, wb_fn, wb_out = _to_bf16(w_br_ml, w_br_da, w_br_fn, w_out)
    (wb_ffn_in,), (wb_ffn_out,) = _to_bf16(w_ffn_in), _to_bf16(w_ffn_out)

    gate_lo = 4 * D_MODEL
    da_lo = gate_lo + N_GATE
    fn_lo = da_lo + 3 * D_MODEL
    w_main = jnp.concatenate([w_in[..., :D_MODEL], w_in[..., 2 * D_MODEL:gate_lo],
                              _da_col_perm(w_in[..., da_lo:da_lo + D_MODEL]),
                              _da_col_perm(w_in[..., da_lo + D_MODEL:da_lo + 2 * D_MODEL]),
                              w_in[..., da_lo + 2 * D_MODEL:fn_lo],
                              w_in[..., fn_lo + D_MODEL:], w_in[..., fn_lo:fn_lo + D_MODEL]], axis=-1).astype(BF16)
    w_kt = jnp.swapaxes(w_in[..., D_MODEL:2 * D_MODEL], 1, 2).astype(BF16)
    w_gate_t = jnp.swapaxes(w_in[..., gate_lo:da_lo], 1, 2).astype(BF16)

    tm_in = n_batch * CTX_LEN
    tm_tok = _row_tile(n_batch)
    xs = None
    for l in range(DEPTH):
        need_ctx = l < DEPTH - 1
        lam_init = 0.8 - 0.6 * math.exp(-0.3 * l)
        xl, xc = (x_lat, x_ctx) if xs is None else (xs, xs)
        u, kt, fn, gates_t = _inproj(xl, xc, 0 if xs is None else n_lat // tm_in, mods,
                                     norm_g[l, 0].reshape(1, D_MODEL), w_main, w_kt, w_gate_t, l,
                                     cos_t, sin_t, n_batch)
        hf, hb = _mlstm(u, kt, gates_t, ml_gate_b[l], n_batch)
        yd = _attn(u, da_lam[l], da_head_g[l].reshape(1, D_MODEL), lam_init, need_ctx, n_batch)
        yf = _fourier(fn, tables, need_ctx, n_batch)
        xs = _merge(hf, hb, u, yd, yf, xl, xc, 0 if xs is None else n_lat // tm_tok, mods,
                    ml_head_g[l].reshape(1, D_MODEL), wb_ml, wb_da, wb_fn, wb_out, l, need_ctx, n_batch)
        final = l == DEPTH - 1
        n_out = n_lat if final else xs.shape[0]
        xs = _ffn(xs, mods, norm_g[l, 1].reshape(1, D_MODEL), wb_ffn_in, wb_ffn_out, l,
                  final_g2, final, n_out, n_batch)
    return xs.reshape(n_batch, SEQ, D_MODEL)
```

```python
import functools
import math

import jax
import jax.numpy as jnp
import numpy as np
from jax import lax
from jax.experimental import pallas as pl
from jax.experimental.pallas import tpu as pltpu

D_MODEL = 1024
SEQ = 4096
DEPTH = 4
CTX_LEN = 256
GRID_W = 64
NORM_EPS = 1e-6

ML_HEADS = 4
ML_DK = 256
ML_CHUNK = 256

DA_HEADS = 8
DA_DH = 64
DA_DV = 2 * DA_DH
ROPE_BASE = 10000.0
ATTN_TQ = 512
ATTN_ROW_CHUNKS = 2

FN_GROUPS = 4
FN_GC = 256
FFT_R = 64

D_FF = 2816
N_GATE = 4 * ML_HEADS
N_MOD = 6 * D_MODEL

U_QML, U_VML, U_OML, U_QDA, U_KDA, U_VDA, U_GPRE = 0, 1, 2, 3, 4, 5, 6
U_BLOCKS = 9
W_BLOCKS = U_BLOCKS + 1
STEP_KT = 1
N_COL_STEPS = W_BLOCKS // 2

VMEM_LIMIT_V7X = 56 * 1024 * 1024
LANES_V7X = 128
BF16_SUBLANES_V7X = 16
CAST_STEP_BYTES = 8 * 1024 * 1024

BF16 = jnp.bfloat16
F32 = jnp.float32


def _cparams(sem):
    return pltpu.CompilerParams(dimension_semantics=sem, vmem_limit_bytes=VMEM_LIMIT_V7X)


def _dot(a, b):
    return jnp.dot(a, b, preferred_element_type=F32)


def _dot_nt(a, b):
    return lax.dot_general(a, b, (((1,), (1,)), ((), ())), preferred_element_type=F32)


def _mod_row(mod_ref, i, rows_per_batch, n_batch, col):
    r = jnp.minimum(i // rows_per_batch, n_batch)
    return mod_ref[pl.ds(r, 1), col * D_MODEL:(col + 1) * D_MODEL]


def _row_tile(n_batch):
    return min(512, n_batch * CTX_LEN)


def _rms(x, g):
    return x * lax.rsqrt(jnp.mean(x * x, axis=-1, keepdims=True) + NORM_EPS) * g


def _cast_kernel(*refs):
    n = len(refs) // 2
    for w_ref, o_ref in zip(refs[:n], refs[n:]):
        o_ref[...] = w_ref[...].astype(o_ref.dtype)


def _to_bf16(*ws):
    n_l, rows, cols = ws[0].shape
    assert all(w.shape == ws[0].shape and w.dtype == F32 for w in ws)
    step_bytes = cols * 4 * len(ws)
    tr = max(t for t in range(BF16_SUBLANES_V7X, rows + 1, BF16_SUBLANES_V7X)
             if rows % t == 0 and t * step_bytes <= CAST_STEP_BYTES)
    spec = pl.BlockSpec((1, tr, cols), lambda l, i: (l, i, 0))
    return pl.pallas_call(
        _cast_kernel,
        grid=(n_l, rows // tr),
        in_specs=[spec] * len(ws),
        out_specs=[spec] * len(ws),
        out_shape=[jax.ShapeDtypeStruct(w.shape, BF16) for w in ws],
        compiler_params=_cparams(("arbitrary", "arbitrary")),
        name="cast_bf16",
    )(*ws)


def _mods_kernel(c_ref, w_ref, b_ref, o_ref):
    c = c_ref[...]
    s = (c * jax.nn.sigmoid(c)).astype(BF16)
    o_ref[0] = _dot(s, w_ref[0].astype(BF16)) + b_ref[0]


def _mods(cc, w_ada, b_ada):
    tn = 1536
    return pl.pallas_call(
        _mods_kernel,
        grid=(DEPTH, N_MOD // tn),
        in_specs=[
            pl.BlockSpec((8, D_MODEL), lambda l, j: (0, 0)),
            pl.BlockSpec((1, D_MODEL, tn), lambda l, j: (l, 0, j)),
            pl.BlockSpec((1, 1, tn), lambda l, j: (l, 0, j)),
        ],
        out_specs=pl.BlockSpec((1, 8, tn), lambda l, j: (l, 0, j)),
        out_shape=jax.ShapeDtypeStruct((DEPTH, 8, N_MOD), F32),
        compiler_params=_cparams(("arbitrary", "arbitrary")),
        name="adaln_mods",
    )(cc, w_ada, b_ada.reshape(DEPTH, 1, N_MOD))


def _inproj_kernel(xl_ref, xc_ref, mod_ref, g_ref, w_ref, wkt_ref, wgt_ref, cos_ref, sin_ref,
                   u_ref, kt_ref, fn_ref, gate_ref, xn_ref, *, rows_per_batch, n_batch):
    i = pl.program_id(0)
    j = pl.program_id(1)

    def normalise(x_ref):
        y = _rms(x_ref[...], g_ref[...])
        shift = _mod_row(mod_ref, i, rows_per_batch, n_batch, 0)
        scale = _mod_row(mod_ref, i, rows_per_batch, n_batch, 1)
        xn_ref[...] = (y * (1.0 + scale) + shift).astype(BF16)
        gate_ref[...] = _dot_nt(wgt_ref[...], xn_ref[...])

    is_lat = i < rows_per_batch * n_batch

    @pl.when(jnp.logical_and(j == 0, is_lat))
    def _():
        normalise(xl_ref)

    @pl.when(jnp.logical_and(j == 0, jnp.logical_not(is_lat)))
    def _():
        normalise(xc_ref)

    def product(h):
        return _dot(xn_ref[...], w_ref[:, h * D_MODEL:(h + 1) * D_MODEL])

    def plain(h):
        u_ref[:, h * D_MODEL:(h + 1) * D_MODEL] = product(h).astype(BF16)

    def rope(h):
        acc = product(h)
        cos = cos_ref[...]
        sin = sin_ref[...]
        for t in range(acc.shape[1] // DA_DV):
            x = acc[:, t * DA_DV:(t + 1) * DA_DV]
            lo = h * D_MODEL + t * DA_DV
            u_ref[:, lo:lo + DA_DV] = (x * cos + pltpu.roll(x, DA_DV // 2, 1) * sin).astype(BF16)

    def fourier_in(h):
        acc = product(h)
        for g in range(FN_GROUPS):
            z = acc[:, g * FN_GC:(g + 1) * FN_GC]
            fn_ref[g] = z.reshape(z.shape[0] // FFT_R, FFT_R * FN_GC).astype(BF16)

    def keys_t():
        kt_ref[...] = (_dot_nt(wkt_ref[...], xn_ref[...]) * (ML_DK ** -0.5)).astype(BF16)

    epilogues = {U_QML: plain, U_VML: plain, U_OML: plain, U_QDA: rope, U_KDA: rope, U_VDA: plain,
                 U_GPRE: plain, U_GPRE + 1: plain, U_GPRE + 2: plain, U_BLOCKS: fourier_in}
    for step in range(N_COL_STEPS):
        @pl.when(j == step)
        def _(step=step):
            for h in range(2):
                epilogues[2 * step + h](h)
            if step == STEP_KT:
                keys_t()


def _inproj(x_lat, x_ctx, ctx_blk, mods, g, w_main, w_kt, w_gate_t, layer, cos_t, sin_t, n_batch):
    tm = n_batch * CTX_LEN
    assert SEQ % tm == 0 and cos_t.shape[0] == SEQ + tm
    n_rows = n_batch * (SEQ + CTX_LEN)
    ni = n_rows // tm
    rows_per_batch = SEQ // tm
    lat_blocks = n_batch * rows_per_batch

    def tab_idx(i, j):
        return (jnp.where(i < lat_blocks, i % rows_per_batch, rows_per_batch), 0)

    kern = functools.partial(_inproj_kernel, rows_per_batch=rows_per_batch, n_batch=n_batch)
    return pl.pallas_call(
        kern,
        grid=(ni, N_COL_STEPS),
        in_specs=[
            pl.BlockSpec((tm, D_MODEL), lambda i, j: (jnp.minimum(i, lat_blocks - 1), 0)),
            pl.BlockSpec((tm, D_MODEL), lambda i, j: (ctx_blk, 0)),
            pl.BlockSpec((None, 8, N_MOD), lambda i, j: (layer, 0, 0)),
            pl.BlockSpec((1, D_MODEL), lambda i, j: (0, 0)),
            pl.BlockSpec((None, D_MODEL, 2 * D_MODEL), lambda i, j: (layer, 0, j)),
            pl.BlockSpec((None, D_MODEL, D_MODEL), lambda i, j: (layer, 0, 0)),
            pl.BlockSpec((None, N_GATE, D_MODEL), lambda i, j: (layer, 0, 0)),
            pl.BlockSpec((tm, DA_DV), tab_idx),
            pl.BlockSpec((tm, DA_DV), tab_idx),
        ],
        out_specs=[
            pl.BlockSpec((tm, 2 * D_MODEL), lambda i, j: (i, j)),
            pl.BlockSpec((D_MODEL, tm), lambda i, j: (0, i)),
            pl.BlockSpec((FN_GROUPS, tm // FFT_R, FFT_R * FN_GC), lambda i, j: (0, i, 0)),
            pl.BlockSpec((N_GATE, tm), lambda i, j: (0, i)),
        ],
        out_shape=[
            jax.ShapeDtypeStruct((n_rows, U_BLOCKS * D_MODEL), BF16),
            jax.ShapeDtypeStruct((D_MODEL, n_rows), BF16),
            jax.ShapeDtypeStruct((FN_GROUPS, n_rows // FFT_R, FFT_R * FN_GC), BF16),
            jax.ShapeDtypeStruct((N_GATE, n_rows), F32),
        ],
        scratch_shapes=[pltpu.VMEM((tm, D_MODEL), BF16)],
        compiler_params=_cparams(("arbitrary", "arbitrary")),
        name="in_proj",
    )(x_lat, x_ctx, mods, g, w_main, w_kt, w_gate_t, cos_t, sin_t)


def _split3(x):
    hi = x.astype(BF16).astype(F32)
    mid = (x - hi).astype(BF16).astype(F32)
    lo = (x - hi - mid).astype(BF16).astype(F32)
    return hi, mid, lo


def _mlstm_kernel(qf_ref, ktf_ref, vf_ref, gtf_ref, gtfn_ref, qb_ref, ktb_ref, vb_ref, gtb_ref, gtbn_ref, bias_ref,
                  hf_ref, hb_ref, *scratch):
    n_st = 2 * ML_HEADS
    cx_refs, bw_refs, pmw_refs = scratch[:n_st], scratch[n_st:2 * n_st], scratch[2 * n_st:3 * n_st]
    m_refs, c_refs, bend_refs, g_refs = (scratch[3 * n_st + 2 * k:3 * n_st + 2 * k + 2] for k in range(4))
    state_refs = cx_refs + m_refs
    s = pl.program_id(1)
    L = ML_CHUNK
    H = ML_HEADS
    W = LANES_V7X

    @pl.when(s == 0)
    def _():
        for ref in state_refs:
            ref[...] = jnp.zeros_like(ref)

    t_idx = lax.broadcasted_iota(jnp.int32, (L, L), 0)
    s_idx = lax.broadcasted_iota(jnp.int32, (L, L), 1)
    sub8 = lax.broadcasted_iota(jnp.int32, (8, W), 0)
    ones_w = jnp.ones((L, W), BF16)
    er = lax.broadcasted_iota(jnp.int32, (4 * H, 2 * W), 0)
    ec = lax.broadcasted_iota(jnp.int32, (4 * H, 2 * W), 1)

    def running_max_rows(x, d):
        n_tiles = L // 8
        out = [None] * n_tiles
        carry = None
        for j in (range(n_tiles) if d == 0 else range(n_tiles - 1, -1, -1)):
            r = x[8 * j:8 * (j + 1)]
            k = 1
            while k < 8:
                if d == 0:
                    r = jnp.maximum(r, jnp.where(sub8 >= k, pltpu.roll(r, k, 0), -jnp.inf))
                else:
                    r = jnp.maximum(r, jnp.where(sub8 < 8 - k, pltpu.roll(r, 8 - k, 0), -jnp.inf))
                k *= 2
            if carry is not None:
                r = jnp.maximum(r, carry)
            carry = jnp.broadcast_to(r[7:8] if d == 0 else r[0:1], (8, W))
            out[j] = r
        return jnp.concatenate(out, axis=0)

    def gate_part(d, gt_ref):
        before = (t_idx <= s_idx) if d == 0 else (t_idx >= s_idx)
        gt = gt_ref[2 * H * d:2 * H * (d + 1), :] + bias_ref[2 * H * d:2 * H * (d + 1), :]
        i4, f4 = gt[0:H], gt[H:2 * H]
        lf4 = jnp.minimum(f4, 0.0) - jnp.log1p(jnp.exp(-jnp.abs(f4)))
        lf_terms = jnp.concatenate(list(_split3(lf4)) + [jnp.zeros((H, L), F32)], axis=0).astype(BF16)
        cum_rhs = jnp.concatenate([jnp.where(before, 1.0, 0.0).astype(BF16), ones_w], axis=1)
        r = _dot(lf_terms, cum_rhs)
        bx = r[0:H] + r[H:2 * H] + r[2 * H:3 * H]
        b4, bend4 = bx[:, 0:L], bx[:, L:L + W]
        c4 = i4 - b4
        c_refs[d][...] = c4
        bend_refs[d][...] = bend4
        g_refs[d][...] = bend4[:, 0:1] - b4 + i4
        cols = jnp.concatenate(_split3(b4) + (c4.astype(BF16).astype(F32),), axis=0).T.astype(BF16)
        for h in range(H):
            pick = jnp.where((er % H == h) & ((er < 3 * H) == (ec < W)), 1.0, 0.0).astype(BF16)
            wide = _dot(cols, pick)
            bw_refs[d * H + h][...] = wide[:, 0:W]
            pmw_refs[d * H + h][...] = running_max_rows(wide[:, W:2 * W], d)

    dirs = ((0, qf_ref, ktf_ref, vf_ref, gtf_ref, gtfn_ref, hf_ref),
            (1, qb_ref, ktb_ref, vb_ref, gtb_ref, gtbn_ref, hb_ref))

    @pl.when(s == 0)
    def _():
        for d, _, _, _, gt_ref, _, _ in dirs:
            gate_part(d, gt_ref)

    for d, q_ref, kt_ref, v_ref, _, _, h_ref in dirs:
        causal = (s_idx <= t_idx) if d == 0 else (s_idx >= t_idx)
        c4, bend4, g4 = c_refs[d][...], bend_refs[d][...], g_refs[d][...]
        m_prev4 = m_refs[d][...]
        m_new4 = jnp.maximum(bend4 + m_prev4, jnp.max(g4, axis=1, keepdims=True))
        m_refs[d][...] = m_new4
        decay4 = jnp.exp(bend4 + m_prev4 - m_new4)
        w4 = jnp.exp(g4 - m_new4[:, 0:1])
        for h in range(H):
            st = d * H + h
            hs = slice(h * ML_DK, (h + 1) * ML_DK)
            b_w = bw_refs[st][...]
            m_w = jnp.maximum(pmw_refs[st][...], m_prev4[h:h + 1, :])

            q = q_ref[:, hs]
            kt = kt_ref[hs, :]
            vx = jnp.concatenate([v_ref[:, hs], ones_w], axis=1)
            m_ll = jnp.concatenate([m_w] * (L // W), axis=1)
            a = (jnp.where(causal, jnp.exp(c4[h:h + 1, :] - m_ll), 0.0) * _dot(q, kt)).astype(BF16)
            cx_prev = cx_refs[st][...]
            qc = _dot(q, cx_prev.astype(BF16))
            av = _dot(a, vx)
            sc_w = jnp.exp(m_prev4[h:h + 1, :] - m_w)
            den = sc_w * qc[:, ML_DK:] + av[:, ML_DK:]
            inv = 1.0 / jnp.maximum(jnp.abs(den), jnp.exp(-(b_w + m_w)))
            for t in range(ML_DK // W):
                ts = slice(t * W, (t + 1) * W)
                h_ref[:, h * ML_DK + t * W:h * ML_DK + (t + 1) * W] = (
                    (sc_w * qc[:, ts] + av[:, ts]) * inv).astype(h_ref.dtype)

            kw = (kt.astype(F32) * w4[h:h + 1, :]).astype(BF16)
            dec = jnp.concatenate([decay4[h:h + 1, :]] * (ML_DK // W + 1), axis=1)
            cx_refs[st][...] = dec * cx_prev + _dot(kw, vx)

    for d, _, _, _, _, gtn_ref, _ in dirs:
        gate_part(d, gtn_ref)


def _mlstm(u, kt, gates_t, gate_b, n_batch):
    n_rows = u.shape[0]
    L = ML_CHUNK
    lat_chunks = SEQ // L
    ctx_chunks = CTX_LEN // L
    n_steps = ctx_chunks + lat_chunks
    ctx_base = n_batch * lat_chunks

    def rowblk(d):
        def f(b, s):
            in_ctx = s < ctx_chunks
            if d == 0:
                c = jnp.where(in_ctx, s, s - ctx_chunks)
            else:
                c = jnp.where(in_ctx, ctx_chunks - 1 - s, lat_chunks - 1 - (s - ctx_chunks))
            return jnp.where(in_ctx, ctx_base + ctx_chunks * b, lat_chunks * b) + c
        return f

    def dir_specs(d):
        rb = rowblk(d)
        return [
            pl.BlockSpec((L, D_MODEL), lambda b, s: (rb(b, s), U_QML)),
            pl.BlockSpec((D_MODEL, L), lambda b, s: (0, rb(b, s))),
            pl.BlockSpec((L, D_MODEL), lambda b, s: (rb(b, s), U_VML)),
            pl.BlockSpec((N_GATE, L), lambda b, s: (0, rb(b, s))),
            pl.BlockSpec((N_GATE, L), lambda b, s: (0, rb(b, jnp.minimum(s + 1, n_steps - 1)))),
        ]

    def out_spec(d):
        rb = rowblk(d)
        return pl.BlockSpec((L, D_MODEL), lambda b, s: (rb(b, s), 0))

    n_st = 2 * ML_HEADS
    return pl.pallas_call(
        _mlstm_kernel,
        grid=(n_batch, n_steps),
        in_specs=dir_specs(0) + dir_specs(1) + [pl.BlockSpec((N_GATE, 1), lambda b, s: (0, 0))],
        out_specs=[out_spec(0), out_spec(1)],
        out_shape=[jax.ShapeDtypeStruct((n_rows, D_MODEL), BF16)] * 2,
        scratch_shapes=(
            [pltpu.VMEM((ML_DK, ML_DK + LANES_V7X), F32)] * n_st
            + [pltpu.VMEM((L, LANES_V7X), F32)] * (2 * n_st)
            + [pltpu.VMEM((ML_HEADS, LANES_V7X), F32)] * 2
            + [pltpu.VMEM((ML_HEADS, L), F32)] * 2
            + [pltpu.VMEM((ML_HEADS, LANES_V7X), F32)] * 2
            + [pltpu.VMEM((ML_HEADS, L), F32)] * 2),
        compiler_params=_cparams(("arbitrary", "arbitrary")),
        name="mlstm_scan",
    )(u, kt, u, gates_t, gates_t, u, kt, u, gates_t, gates_t, gate_b.reshape(N_GATE, 1))


def _attn_lambda(lam_ref, lam_init):
    lq = lam_ref[...]
    return (jnp.exp(jnp.sum(lq[0:1] * lq[1:2], axis=1, keepdims=True))
            - jnp.exp(jnp.sum(lq[2:3] * lq[3:4], axis=1, keepdims=True)) + lam_init)


def _attn_queries(q_ref):
    q = q_ref[...]
    lane = lax.broadcasted_iota(jnp.int32, (1, DA_DV), 1)
    zero = jnp.zeros_like(q)
    is_map0 = (lane % DA_DH) < (DA_DH // 2)
    q2 = jnp.concatenate([jnp.where(is_map0, q, zero), jnp.where(is_map0, zero, q)], axis=0)
    return q2 * (DA_DH ** -0.5)


def _attn_sums(s_chunks, m_chunks, vx):
    return jnp.concatenate([_dot(jnp.exp(s - m).astype(BF16), vx) for s, m in zip(s_chunks, m_chunks)], axis=0)


def _attn_finish(acc, lam, g, lam_init, tq):
    o0 = acc[0:tq, 0:DA_DV] * (1.0 / acc[0:tq, DA_DV:DA_DV + 1])
    o1 = acc[tq:, 0:DA_DV] * (1.0 / acc[tq:, DA_DV:DA_DV + 1])
    return _rms(o0 - lam * o1, g) * (1.0 - lam_init)


def _attn_kernel(qa_ref, kla_ref, kca_ref, qb_ref, klb_ref, kcb_ref, vl_ref, vc_ref, lam_ref, g_ref, o_ref,
                 vx_ref, s_ref, m_ref, *, lam_init, q_blocks):
    t = pl.program_id(0)
    tq = qa_ref.shape[0]
    rows = 2 * tq // ATTN_ROW_CHUNKS

    @pl.when(jnp.logical_and(t > 0, (2 * t - 2) % q_blocks == 0))
    def _():
        vx_ref[0:SEQ, 0:DA_DV] = vl_ref[...]
        vx_ref[SEQ:, 0:DA_DV] = vc_ref[...]
        lane_v = lax.broadcasted_iota(jnp.int32, (SEQ + CTX_LEN, DA_DV), 1)
        vx_ref[:, DA_DV:] = jnp.where(lane_v == 0, 1.0, 0.0).astype(BF16)

    def score(slot, q_ref, kl_ref, kc_ref):
        q2 = _attn_queries(q_ref)
        s_lat = _dot_nt(q2, kl_ref[...])
        s_ctx = _dot_nt(q2, kc_ref[...])
        m_ref[slot] = jnp.maximum(jnp.max(s_lat, axis=1, keepdims=True), jnp.max(s_ctx, axis=1, keepdims=True))
        s_ref[slot, :, 0:SEQ] = s_lat
        s_ref[slot, :, SEQ:] = s_ctx

    def drain(slot):
        s_chunks = [s_ref[slot, c * rows:(c + 1) * rows, :] for c in range(ATTN_ROW_CHUNKS)]
        m_chunks = [m_ref[slot, c * rows:(c + 1) * rows, :] for c in range(ATTN_ROW_CHUNKS)]
        acc = _attn_sums(s_chunks, m_chunks, vx_ref[...])
        o = _attn_finish(acc, _attn_lambda(lam_ref, lam_init), g_ref[...], lam_init, tq)
        o_ref[slot * tq:(slot + 1) * tq, :] = o.astype(o_ref.dtype)

    @pl.when(t > 0)
    def _():
        drain(0)
        score(1, qa_ref, kla_ref, kca_ref)

    @pl.when(t == 0)
    def _():
        score(0, qb_ref, klb_ref, kcb_ref)

    @pl.when(t > 0)
    def _():
        drain(1)
        score(0, qb_ref, klb_ref, kcb_ref)


def _attn_ctx_kernel(q_ref, kc_ref, vc_ref, lam_ref, g_ref, yin_ref, o_ref, *, lam_init):
    del yin_ref
    tq = q_ref.shape[0]
    s = _dot_nt(_attn_queries(q_ref), kc_ref[...])
    lane_v = lax.broadcasted_iota(jnp.int32, (CTX_LEN, DA_DV), 1)
    vx = jnp.concatenate([vc_ref[...], jnp.where(lane_v == 0, 1.0, 0.0).astype(BF16)], axis=1)
    acc = _attn_sums([s], [jnp.max(s, axis=1, keepdims=True)], vx)
    o = _attn_finish(acc, _attn_lambda(lam_ref, lam_init), g_ref[...], lam_init, tq)
    o_ref[...] = o.astype(o_ref.dtype)


def _attn(u, da_lam, head_g, lam_init, need_ctx, n_batch):
    n_rows = u.shape[0]
    tq = ATTN_TQ
    lat_qblocks = SEQ // tq
    ctx_base = n_batch * (SEQ // CTX_LEN)
    cpb = D_MODEL // DA_DV
    n_keys = SEQ + CTX_LEN

    n_items = n_batch * DA_HEADS * lat_qblocks
    assert lat_qblocks % 2 == 0

    def item(k):
        k = jnp.clip(k, 0, n_items - 1)
        return k // (DA_HEADS * lat_qblocks), (k // lat_qblocks) % DA_HEADS, k % lat_qblocks

    def of_item(offset, f):
        return lambda t: f(*item(2 * t + offset))

    def scored_specs(offset):
        return [
            pl.BlockSpec((tq, DA_DV), of_item(offset, lambda b, h, qi: (b * lat_qblocks + qi, U_QDA * cpb + h))),
            pl.BlockSpec((SEQ, DA_DV), of_item(offset, lambda b, h, qi: (b, U_KDA * cpb + h))),
            pl.BlockSpec((CTX_LEN, DA_DV), of_item(offset, lambda b, h, qi: (ctx_base + b, U_KDA * cpb + h))),
        ]

    kern = functools.partial(_attn_kernel, lam_init=lam_init, q_blocks=lat_qblocks)
    yd = pl.pallas_call(
        kern,
        grid=(n_items // 2 + 1,),
        in_specs=scored_specs(-1) + scored_specs(0) + [
            pl.BlockSpec((SEQ, DA_DV), of_item(-2, lambda b, h, qi: (b, U_VDA * cpb + h))),
            pl.BlockSpec((CTX_LEN, DA_DV), of_item(-2, lambda b, h, qi: (ctx_base + b, U_VDA * cpb + h))),
            pl.BlockSpec((4, DA_DH), lambda t: (0, 0)),
            pl.BlockSpec((1, DA_DV), of_item(-2, lambda b, h, qi: (0, h))),
        ],
        out_specs=pl.BlockSpec((2 * tq, DA_DV),
                               of_item(-2, lambda b, h, qi: ((b * lat_qblocks + qi) // 2, h))),
        out_shape=jax.ShapeDtypeStruct((n_rows, D_MODEL), BF16),
        scratch_shapes=[
            pltpu.VMEM((n_keys, 2 * DA_DV), BF16),
            pltpu.VMEM((2, 2 * tq, n_keys), F32),
            pltpu.VMEM((2, 2 * tq, 1), F32),
        ],
        compiler_params=_cparams(("arbitrary",)),
        name="diff_attn",
    )(u, u, u, u, u, u, u, u, da_lam, head_g)
    if not need_ctx:
        return yd
    kern_ctx = functools.partial(_attn_ctx_kernel, lam_init=lam_init)
    return pl.pallas_call(
        kern_ctx,
        grid=(n_batch, DA_HEADS),
        in_specs=[
            pl.BlockSpec((CTX_LEN, DA_DV), lambda b, h: (ctx_base + b, U_QDA * cpb + h)),
            pl.BlockSpec((CTX_LEN, DA_DV), lambda b, h: (ctx_base + b, U_KDA * cpb + h)),
            pl.BlockSpec((CTX_LEN, DA_DV), lambda b, h: (ctx_base + b, U_VDA * cpb + h)),
            pl.BlockSpec((4, DA_DH), lambda b, h: (0, 0)),
            pl.BlockSpec((1, DA_DV), lambda b, h: (0, h)),
            pl.BlockSpec(memory_space=pl.ANY),
        ],
        out_specs=pl.BlockSpec((CTX_LEN, DA_DV), lambda b, h: (ctx_base + b, h)),
        out_shape=jax.ShapeDtypeStruct((n_rows, D_MODEL), BF16),
        input_output_aliases={5: 0},
        compiler_params=_cparams(("arbitrary", "arbitrary")),
        name="diff_attn_ctx",
    )(u, u, u, da_lam, head_g, yd)


def _dft_tables():
    R = FFT_R

    def cs(num, period):
        ang = (num % period).astype(np.float32) * np.float32(2.0 * math.pi / period)
        return np.cos(ang), np.sin(ang)

    def bf16(a):
        return jnp.asarray(a.astype(BF16))

    idx = np.arange(R, dtype=np.int32)
    c1, s1 = cs(idx[:, None] * idx[None, :], R)
    a1 = bf16(np.concatenate([c1, -s1], axis=0))
    f2 = idx[:, None, None]
    f1 = idx[None, :, None]
    t1 = idx[None, None, :]
    mc, ms = cs(t1 * (R * f1 + f2), SEQ)
    b2 = bf16(np.concatenate([np.concatenate([mc, ms], axis=2),
                              np.concatenate([-ms, mc], axis=2)], axis=1))
    ch = np.arange(FN_GC, dtype=np.int32)
    cc, sc = cs(ch[:, None] * ch[None, :], FN_GC)
    csm = bf16(np.concatenate([cc, sc], axis=0))
    actx = bf16(np.concatenate([cc, -sc], axis=0))
    return a1, b2, csm, actx


def _fft1_kernel(a_ref, z_ref, p_ref):
    p_ref[0, 0] = _dot(a_ref[...], z_ref[0]).astype(p_ref.dtype)


def _fft1(fn_view, a1, n_batch):
    R = FFT_R
    lanes = R * FN_GC
    lc = lanes
    return pl.pallas_call(
        _fft1_kernel,
        grid=(FN_GROUPS, n_batch, lanes // lc),
        in_specs=[
            pl.BlockSpec((2 * R, R), lambda g, b, c: (0, 0)),
            pl.BlockSpec((1, R, lc), lambda g, b, c: (g, b, c)),
        ],
        out_specs=pl.BlockSpec((1, 1, 2 * R, lc), lambda g, b, c: (g, b, 0, c)),
        out_shape=jax.ShapeDtypeStruct((FN_GROUPS, n_batch, 2 * R, lanes), BF16),
        compiler_params=_cparams(("arbitrary", "arbitrary", "arbitrary")),
        name="fourier_stage1",
    )(a1, fn_view)


def _fft2_kernel(p_ref, b_ref, cs_ref, o_ref, *, f2b):
    R = FFT_R
    cc = cs_ref[0:FN_GC, :]
    sc = cs_ref[FN_GC:2 * FN_GC, :]
    norm = 1.0 / math.sqrt(SEQ * FN_GC)
    for g in range(FN_GROUPS):
        xr, xi = [], []
        for jj in range(f2b):
            stacked = jnp.concatenate([p_ref[g, 0, 0, jj], p_ref[g, 0, 1, jj]], axis=0)
            x = _dot(b_ref[jj], stacked)
            xr.append(x[0:R])
            xi.append(x[R:2 * R])
        y = (_dot(jnp.concatenate(xr, axis=0).astype(BF16), cc)
             + _dot(jnp.concatenate(xi, axis=0).astype(BF16), sc)) * norm
        for jj in range(f2b):
            lo = jj * D_MODEL + g * FN_GC
            o_ref[:, lo:lo + FN_GC] = y[jj * R:(jj + 1) * R].astype(o_ref.dtype)


def _fft2(p6, b2, csm, n_rows, n_batch):
    R = FFT_R
    f2b = 16
    kern = functools.partial(_fft2_kernel, f2b=f2b)
    return pl.pallas_call(
        kern,
        grid=(n_batch, R // f2b),
        in_specs=[
            pl.BlockSpec((FN_GROUPS, 1, 2, f2b, R, FN_GC), lambda b, f: (0, b, 0, f, 0, 0)),
            pl.BlockSpec((f2b, 2 * R, 2 * R), lambda b, f: (f, 0, 0)),
            pl.BlockSpec((2 * FN_GC, FN_GC), lambda b, f: (0, 0)),
        ],
        out_specs=pl.BlockSpec((R, f2b * D_MODEL), lambda b, f: (b, f)),
        out_shape=jax.ShapeDtypeStruct((n_rows // R, R * D_MODEL), BF16),
        compiler_params=_cparams(("arbitrary", "arbitrary")),
        name="fourier_stage2",
    )(p6, b2, csm)


def _fft_ctx_kernel(z_ref, a_ref, cs_ref, yin_ref, o_ref, *, n_batch):
    del yin_ref
    cc = cs_ref[0:FN_GC, :]
    sc = cs_ref[FN_GC:2 * FN_GC, :]
    norm = 1.0 / math.sqrt(CTX_LEN * FN_GC)
    n_ctx = n_batch * CTX_LEN
    groups = []
    for g in range(FN_GROUPS):
        z_all = z_ref[g].astype(F32).reshape(n_ctx, FN_GC)
        ys = []
        for b in range(n_batch):
            p = _dot(a_ref[...], z_all[b * CTX_LEN:(b + 1) * CTX_LEN].astype(BF16))
            ys.append(_dot(p[0:CTX_LEN].astype(BF16), cc) + _dot(p[CTX_LEN:2 * CTX_LEN].astype(BF16), sc))
        groups.append(jnp.concatenate(ys, axis=0))
    y = jnp.concatenate(groups, axis=1) * norm
    o_ref[...] = y.reshape(n_ctx // FFT_R, FFT_R * D_MODEL).astype(o_ref.dtype)


def _fft_ctx(fn_view, actx, csm, yf, n_batch):
    n_ctx = n_batch * CTX_LEN
    ctx_blk = n_batch * SEQ // n_ctx
    kern = functools.partial(_fft_ctx_kernel, n_batch=n_batch)
    return pl.pallas_call(
        kern,
        grid=(1,),
        in_specs=[
            pl.BlockSpec((FN_GROUPS, n_ctx // FFT_R, FFT_R * FN_GC), lambda i: (0, ctx_blk, 0)),
            pl.BlockSpec((2 * CTX_LEN, CTX_LEN), lambda i: (0, 0)),
            pl.BlockSpec((2 * FN_GC, FN_GC), lambda i: (0, 0)),
            pl.BlockSpec(memory_space=pl.ANY),
        ],
        out_specs=pl.BlockSpec((n_ctx // FFT_R, FFT_R * D_MODEL), lambda i: (ctx_blk, 0)),
        out_shape=jax.ShapeDtypeStruct(yf.shape, BF16),
        input_output_aliases={3: 0},
        compiler_params=_cparams(("arbitrary",)),
        name="fourier_ctx",
    )(fn_view, actx, csm, yf)


def _fourier(fn, tables, need_ctx, n_batch):
    a1, b2, csm, actx = tables
    R = FFT_R
    n_rows = fn.shape[1] * R
    p = _fft1(fn, a1, n_batch)
    yf = _fft2(p.reshape(FN_GROUPS, n_batch, 2, R, R, FN_GC), b2, csm, n_rows, n_batch)
    if need_ctx:
        yf = _fft_ctx(fn, actx, csm, yf, n_batch)
    return yf


def _merge_kernel(hf_ref, hb_ref, o_ref, gm_ref, gd_ref, gf_ref, yd_ref, yf_ref, xl_ref, xc_ref, mod_ref, hg_ref,
                  wml_ref, wda_ref, wfn_ref, wout_ref, out_ref, *, rows_per_batch, n_batch):
    i = pl.program_id(0)
    x = jnp.where(i < rows_per_batch * n_batch, xl_ref[...], xc_ref[...])
    hsum = hf_ref[...].astype(F32) + hb_ref[...].astype(F32)
    hg = hg_ref[...]
    parts = []
    for h in range(ML_HEADS):
        hs = slice(h * ML_DK, (h + 1) * ML_DK)
        parts.append(_rms(hsum[:, hs], hg[:, hs]))
    ym = (jnp.concatenate(parts, axis=1) * jax.nn.sigmoid(o_ref[...].astype(F32))).astype(BF16)
    yf = yf_ref[...].astype(F32).reshape(x.shape).astype(BF16)
    y = (jax.nn.sigmoid(gm_ref[...].astype(F32)) * _dot(ym, wml_ref[...])
         + jax.nn.sigmoid(gd_ref[...].astype(F32)) * _dot(yd_ref[...], wda_ref[...])
         + jax.nn.sigmoid(gf_ref[...].astype(F32)) * _dot(yf, wfn_ref[...]))
    gate = _mod_row(mod_ref, i, rows_per_batch, n_batch, 2)
    out_ref[...] = x + gate * _dot(y.astype(BF16), wout_ref[...])


def _merge(hf, hb, u, yd, yf, x_lat, x_ctx, ctx_blk0, mods, head_g, wml, wda, wfn, wout, layer, need_ctx, n_batch):
    n_rows = n_batch * (SEQ + CTX_LEN)
    tm = _row_tile(n_batch)
    rows_per_batch = SEQ // tm
    lat_blocks = n_batch * rows_per_batch
    ni = (n_rows if need_ctx else n_batch * SEQ) // tm
    kern = functools.partial(_merge_kernel, rows_per_batch=rows_per_batch, n_batch=n_batch)
    row = lambda i: (i, 0)
    full = lambda i: (0, 0)
    wspec = pl.BlockSpec((None, D_MODEL, D_MODEL), lambda i: (layer, 0, 0))
    return pl.pallas_call(
        kern,
        grid=(ni,),
        in_specs=[
            pl.BlockSpec((tm, D_MODEL), row),
            pl.BlockSpec((tm, D_MODEL), row),
            pl.BlockSpec((tm, D_MODEL), lambda i: (i, U_OML)),
            pl.BlockSpec((tm, D_MODEL), lambda i: (i, U_GPRE)),
            pl.BlockSpec((tm, D_MODEL), lambda i: (i, U_GPRE + 1)),
            pl.BlockSpec((tm, D_MODEL), lambda i: (i, U_GPRE + 2)),
            pl.BlockSpec((tm, D_MODEL), row),
            pl.BlockSpec((tm // FFT_R, FFT_R * D_MODEL), row),
            pl.BlockSpec((tm, D_MODEL), lambda i: (jnp.minimum(i, lat_blocks - 1), 0)),
            pl.BlockSpec((tm, D_MODEL), lambda i: (ctx_blk0 + jnp.maximum(i - lat_blocks, 0), 0)),
            pl.BlockSpec((None, 8, N_MOD), lambda i: (layer, 0, 0)),
            pl.BlockSpec((1, D_MODEL), full),
            wspec, wspec, wspec, wspec,
        ],
        out_specs=pl.BlockSpec((tm, D_MODEL), row),
        out_shape=jax.ShapeDtypeStruct((ni * tm, D_MODEL), F32),
        compiler_params=_cparams(("arbitrary",)),
        name="merge_out_proj",
    )(hf, hb, u, u, u, u, yd, yf, x_lat, x_ctx, mods, head_g, wml, wda, wfn, wout)


FFN_CHUNKS = ((0, 1024), (1024, 1024), (2048, 768))


def _ffn_kernel(x_ref, mod_ref, g_ref, win_ref, wout_ref, fg_ref, out_ref, *, rows_per_batch, n_batch, final):
    i = pl.program_id(0)
    x = x_ref[...]
    shift = _mod_row(mod_ref, i, rows_per_batch, n_batch, 3)
    scale = _mod_row(mod_ref, i, rows_per_batch, n_batch, 4)
    gate = _mod_row(mod_ref, i, rows_per_batch, n_batch, 5)
    h = (_rms(x, g_ref[...]) * (1.0 + scale) + shift).astype(BF16)
    acc = None
    for lo, width in FFN_CHUNKS:
        a = _dot(h, win_ref[:, lo:lo + width])
        b = _dot(h, win_ref[:, D_FF + lo:D_FF + lo + width])
        act = (a * jax.nn.sigmoid(a) * b).astype(BF16)
        part = _dot(act, wout_ref[lo:lo + width, :])
        acc = part if acc is None else acc + part
    xn = x + gate * acc
    out_ref[...] = _rms(xn, fg_ref[...]) if final else xn


def _ffn(x, mods, g, w_in, w_out, layer, final_g, final, n_rows_out, n_batch):
    tm = n_batch * CTX_LEN
    rows_per_batch = SEQ // tm
    resident = pl.Buffered(1)
    kern = functools.partial(_ffn_kernel, rows_per_batch=rows_per_batch, n_batch=n_batch, final=final)
    row = lambda i: (i, 0)
    full = lambda i: (0, 0)
    return pl.pallas_call(
        kern,
        grid=(n_rows_out // tm,),
        in_specs=[
            pl.BlockSpec((tm, D_MODEL), row),
            pl.BlockSpec((None, 8, N_MOD), lambda i: (layer, 0, 0)),
            pl.BlockSpec((1, D_MODEL), full),
            pl.BlockSpec((None, D_MODEL, 2 * D_FF), lambda i: (layer, 0, 0), pipeline_mode=resident),
            pl.BlockSpec((None, D_FF, D_MODEL), lambda i: (layer, 0, 0), pipeline_mode=resident),
            pl.BlockSpec((1, D_MODEL), full),
        ],
        out_specs=pl.BlockSpec((tm, D_MODEL), row),
        out_shape=jax.ShapeDtypeStruct((n_rows_out, D_MODEL), F32),
        compiler_params=_cparams(("arbitrary",)),
        name="swiglu_ffn",
    )(x, mods, g, w_in, w_out, final_g)


def _da_col_perm(w):
    half = DA_DH // 2
    lead = w.shape[:-1]
    return jnp.swapaxes(w.reshape(lead + (DA_HEADS, 2, 2, half)), -3, -2).reshape(lead + (DA_HEADS * DA_DV,))


def _rope_tables(pad):
    f32 = np.float32
    n_freq = DA_DH // 4
    rows = SEQ // GRID_W
    inv = f32(ROPE_BASE) ** (-np.arange(n_freq, dtype=f32) / f32(n_freq))
    r = np.repeat(np.arange(rows, dtype=f32), GRID_W)
    col = np.tile(np.arange(GRID_W, dtype=f32), rows)
    ang = np.concatenate([r[:, None] * inv, col[:, None] * inv], axis=-1).astype(f32)
    cos, sin = np.cos(ang), np.sin(ang)
    cos_t = np.concatenate([cos, cos, cos, cos], axis=-1)
    sin_t = np.concatenate([-sin, -sin, sin, sin], axis=-1)
    cos_t = np.concatenate([cos_t, np.ones((pad, DA_DV), f32)], axis=0)
    sin_t = np.concatenate([sin_t, np.zeros((pad, DA_DV), f32)], axis=0)
    return jnp.asarray(cos_t, F32), jnp.asarray(sin_t, F32)


def kernel(x, c, ctx, c_ctx, w_ada, b_ada, norm_g, w_in, ml_gate_b, ml_head_g, da_lam, da_head_g,
           w_br_ml, w_br_da, w_br_fn, w_out, w_ffn_in, w_ffn_out, final_g):
    n_batch = x.shape[0]
    n_lat = n_batch * SEQ
    x_lat = x.reshape(n_lat, D_MODEL)
    x_ctx = ctx.reshape(n_batch * CTX_LEN, D_MODEL)
    cc = jnp.concatenate([c, c_ctx[None, :], jnp.zeros((8 - n_batch - 1, D_MODEL), F32)], axis=0)
    mods = _mods(cc, w_ada, b_ada)
    cos_t, sin_t = _rope_tables(n_batch * CTX_LEN)
    tables = _dft_tables()
    final_g2 = final_g.reshape(1, D_MODEL)
    wb_ml, wb_da, wb_fn, wb_out = _to_bf16(w_br_ml, w_br_da, w_br_fn, w_out)
    (wb_ffn_in,), (wb_ffn_out,) = _to_bf16(w_ffn_in), _to_bf16(w_ffn_out)

    gate_lo = 4 * D_MODEL
    da_lo = gate_lo + N_GATE
    fn_lo = da_lo + 3 * D_MODEL
    w_main = jnp.concatenate([w_in[..., :D_MODEL], w_in[..., 2 * D_MODEL:gate_lo],
                              _da_col_perm(w_in[..., da_lo:da_lo + D_MODEL]),
                              _da_col_perm(w_in[..., da_lo + D_MODEL:da_lo + 2 * D_MODEL]),
                              w_in[..., da_lo + 2 * D_MODEL:fn_lo],
                              w_in[..., fn_lo + D_MODEL:], w_in[..., fn_lo:fn_lo + D_MODEL]], axis=-1).astype(BF16)
    w_kt = jnp.swapaxes(w_in[..., D_MODEL:2 * D_MODEL], 1, 2).astype(BF16)
    w_gate_t = jnp.swapaxes(w_in[..., gate_lo:da_lo], 1, 2).astype(BF16)

    tm_in = n_batch * CTX_LEN
    tm_tok = _row_tile(n_batch)
    xs = None
    for l in range(DEPTH):
        need_ctx = l < DEPTH - 1
        lam_init = 0.8 - 0.6 * math.exp(-0.3 * l)
        xl, xc = (x_lat, x_ctx) if xs is None else (xs, xs)
        u, kt, fn, gates_t = _inproj(xl, xc, 0 if xs is None else n_lat // tm_in, mods,
                                     norm_g[l, 0].reshape(1, D_MODEL), w_main, w_kt, w_gate_t, l,
                                     cos_t, sin_t, n_batch)
        hf, hb = _mlstm(u, kt, gates_t, ml_gate_b[l], n_batch)
        yd = _attn(u, da_lam[l], da_head_g[l].reshape(1, D_MODEL), lam_init, need_ctx, n_batch)
        yf = _fourier(fn, tables, need_ctx, n_batch)
        xs = _merge(hf, hb, u, yd, yf, xl, xc, 0 if xs is None else n_lat // tm_tok, mods,
                    ml_head_g[l].reshape(1, D_MODEL), wb_ml, wb_da, wb_fn, wb_out, l, need_ctx, n_batch)
        final = l == DEPTH - 1
        n_out = n_lat if final else xs.shape[0]
        xs = _ffn(xs, mods, norm_g[l, 1].reshape(1, D_MODEL), wb_ffn_in, wb_ffn_out, l,
                  final_g2, final, n_out, n_batch)
    return xs.reshape(n_batch, SEQ, D_MODEL)
```

```python
import functools
import math

import jax
import jax.numpy as jnp
import numpy as np
from jax import lax
from jax.experimental import pallas as pl
from jax.experimental.pallas import tpu as pltpu

D_MODEL = 1024
SEQ = 4096
DEPTH = 4
CTX_LEN = 256
GRID_W = 64
NORM_EPS = 1e-6

ML_HEADS = 4
ML_DK = 256
ML_CHUNK = 256

DA_HEADS = 8
DA_DH = 64
DA_DV = 2 * DA_DH
ROPE_BASE = 10000.0
ATTN_TQ = 512
ATTN_ROW_CHUNKS = 2

FN_GROUPS = 4
FN_GC = 256
FFT_R = 64

D_FF = 2816
N_GATE = 4 * ML_HEADS
N_MOD = 6 * D_MODEL

U_QML, U_VML, U_OML, U_QDA, U_KDA, U_VDA, U_GPRE = 0, 1, 2, 3, 4, 5, 6
U_BLOCKS = 9
W_BLOCKS = U_BLOCKS + 1
STEP_KT = 1
N_COL_STEPS = W_BLOCKS // 2

VMEM_LIMIT_V7X = 56 * 1024 * 1024
LANES_V7X = 128
BF16_SUBLANES_V7X = 16
CAST_STEP_BYTES = 8 * 1024 * 1024

BF16 = jnp.bfloat16
F32 = jnp.float32


def _cparams(sem):
    return pltpu.CompilerParams(dimension_semantics=sem, vmem_limit_bytes=VMEM_LIMIT_V7X)


def _dot(a, b):
    return jnp.dot(a, b, preferred_element_type=F32)


def _dot_nt(a, b):
    return lax.dot_general(a, b, (((1,), (1,)), ((), ())), preferred_element_type=F32)


def _mod_row(mod_ref, i, rows_per_batch, n_batch, col):
    r = jnp.minimum(i // rows_per_batch, n_batch)
    return mod_ref[pl.ds(r, 1), col * D_MODEL:(col + 1) * D_MODEL]


def _row_tile(n_batch):
    return min(512, n_batch * CTX_LEN)


def _rms(x, g):
    return x * lax.rsqrt(jnp.mean(x * x, axis=-1, keepdims=True) + NORM_EPS) * g


def _cast_kernel(*refs):
    n = len(refs) // 2
    for w_ref, o_ref in zip(refs[:n], refs[n:]):
        o_ref[...] = w_ref[...].astype(o_ref.dtype)


def _to_bf16(*ws):
    n_l, rows, cols = ws[0].shape
    assert all(w.shape == ws[0].shape and w.dtype == F32 for w in ws)
    step_bytes = cols * 4 * len(ws)
    tr = max(t for t in range(BF16_SUBLANES_V7X, rows + 1, BF16_SUBLANES_V7X)
             if rows % t == 0 and t * step_bytes <= CAST_STEP_BYTES)
    spec = pl.BlockSpec((1, tr, cols), lambda l, i: (l, i, 0))
    return pl.pallas_call(
        _cast_kernel,
        grid=(n_l, rows // tr),
        in_specs=[spec] * len(ws),
        out_specs=[spec] * len(ws),
        out_shape=[jax.ShapeDtypeStruct(w.shape, BF16) for w in ws],
        compiler_params=_cparams(("arbitrary", "arbitrary")),
        name="cast_bf16",
    )(*ws)


def _mods_kernel(c_ref, w_ref, b_ref, o_ref):
    c = c_ref[...]
    s = (c * jax.nn.sigmoid(c)).astype(BF16)
    o_ref[0] = _dot(s, w_ref[0].astype(BF16)) + b_ref[0]


def _mods(cc, w_ada, b_ada):
    tn = 1536
    return pl.pallas_call(
        _mods_kernel,
        grid=(DEPTH, N_MOD // tn),
        in_specs=[
            pl.BlockSpec((8, D_MODEL), lambda l, j: (0, 0)),
            pl.BlockSpec((1, D_MODEL, tn), lambda l, j: (l, 0, j)),
            pl.BlockSpec((1, 1, tn), lambda l, j: (l, 0, j)),
        ],
        out_specs=pl.BlockSpec((1, 8, tn), lambda l, j: (l, 0, j)),
        out_shape=jax.ShapeDtypeStruct((DEPTH, 8, N_MOD), F32),
        compiler_params=_cparams(("arbitrary", "arbitrary")),
        name="adaln_mods",
    )(cc, w_ada, b_ada.reshape(DEPTH, 1, N_MOD))


def _inproj_kernel(xl_ref, xc_ref, mod_ref, g_ref, w_ref, wkt_ref, wgt_ref, cos_ref, sin_ref,
                   u_ref, kt_ref, fn_ref, gate_ref, xn_ref, *, rows_per_batch, n_batch):
    i = pl.program_id(0)
    j = pl.program_id(1)

    def normalise(x_ref):
        y = _rms(x_ref[...], g_ref[...])
        shift = _mod_row(mod_ref, i, rows_per_batch, n_batch, 0)
        scale = _mod_row(mod_ref, i, rows_per_batch, n_batch, 1)
        xn_ref[...] = (y * (1.0 + scale) + shift).astype(BF16)
        gate_ref[...] = _dot_nt(wgt_ref[...], xn_ref[...])

    is_lat = i < rows_per_batch * n_batch

    @pl.when(jnp.logical_and(j == 0, is_lat))
    def _():
        normalise(xl_ref)

    @pl.when(jnp.logical_and(j == 0, jnp.logical_not(is_lat)))
    def _():
        normalise(xc_ref)

    def product(h):
        return _dot(xn_ref[...], w_ref[:, h * D_MODEL:(h + 1) * D_MODEL])

    def plain(h):
        u_ref[:, h * D_MODEL:(h + 1) * D_MODEL] = product(h).astype(BF16)

    def rope(h):
        acc = product(h)
        cos = cos_ref[...]
        sin = sin_ref[...]
        for t in range(acc.shape[1] // DA_DV):
            x = acc[:, t * DA_DV:(t + 1) * DA_DV]
            lo = h * D_MODEL + t * DA_DV
            u_ref[:, lo:lo + DA_DV] = (x * cos + pltpu.roll(x, DA_DV // 2, 1) * sin).astype(BF16)

    def fourier_in(h):
        acc = product(h)
        for g in range(FN_GROUPS):
            z = acc[:, g * FN_GC:(g + 1) * FN_GC]
            fn_ref[g] = z.reshape(z.shape[0] // FFT_R, FFT_R * FN_GC).astype(BF16)

    def keys_t():
        kt_ref[...] = (_dot_nt(wkt_ref[...], xn_ref[...]) * (ML_DK ** -0.5)).astype(BF16)

    epilogues = {U_QML: plain, U_VML: plain, U_OML: plain, U_QDA: rope, U_KDA: rope, U_VDA: plain,
                 U_GPRE: plain, U_GPRE + 1: plain, U_GPRE + 2: plain, U_BLOCKS: fourier_in}
    for step in range(N_COL_STEPS):
        @pl.when(j == step)
        def _(step=step):
            for h in range(2):
                epilogues[2 * step + h](h)
            if step == STEP_KT:
                keys_t()


def _inproj(x_lat, x_ctx, ctx_blk, mods, g, w_main, w_kt, w_gate_t, layer, cos_t, sin_t, n_batch):
    tm = n_batch * CTX_LEN
    assert SEQ % tm == 0 and cos_t.shape[0] == SEQ + tm
    n_rows = n_batch * (SEQ + CTX_LEN)
    ni = n_rows // tm
    rows_per_batch = SEQ // tm
    lat_blocks = n_batch * rows_per_batch

    def tab_idx(i, j):
        return (jnp.where(i < lat_blocks, i % rows_per_batch, rows_per_batch), 0)

    kern = functools.partial(_inproj_kernel, rows_per_batch=rows_per_batch, n_batch=n_batch)
    return pl.pallas_call(
        kern,
        grid=(ni, N_COL_STEPS),
        in_specs=[
            pl.BlockSpec((tm, D_MODEL), lambda i, j: (jnp.minimum(i, lat_blocks - 1), 0)),
            pl.BlockSpec((tm, D_MODEL), lambda i, j: (ctx_blk, 0)),
            pl.BlockSpec((None, 8, N_MOD), lambda i, j: (layer, 0, 0)),
            pl.BlockSpec((1, D_MODEL), lambda i, j: (0, 0)),
            pl.BlockSpec((None, D_MODEL, 2 * D_MODEL), lambda i, j: (layer, 0, j)),
            pl.BlockSpec((None, D_MODEL, D_MODEL), lambda i, j: (layer, 0, 0)),
            pl.BlockSpec((None, N_GATE, D_MODEL), lambda i, j: (layer, 0, 0)),
            pl.BlockSpec((tm, DA_DV), tab_idx),
            pl.BlockSpec((tm, DA_DV), tab_idx),
        ],
        out_specs=[
            pl.BlockSpec((tm, 2 * D_MODEL), lambda i, j: (i, j)),
            pl.BlockSpec((D_MODEL, tm), lambda i, j: (0, i)),
            pl.BlockSpec((FN_GROUPS, tm // FFT_R, FFT_R * FN_GC), lambda i, j: (0, i, 0)),
            pl.BlockSpec((N_GATE, tm), lambda i, j: (0, i)),
        ],
        out_shape=[
            jax.ShapeDtypeStruct((n_rows, U_BLOCKS * D_MODEL), BF16),
            jax.ShapeDtypeStruct((D_MODEL, n_rows), BF16),
            jax.ShapeDtypeStruct((FN_GROUPS, n_rows // FFT_R, FFT_R * FN_GC), BF16),
            jax.ShapeDtypeStruct((N_GATE, n_rows), F32),
        ],
        scratch_shapes=[pltpu.VMEM((tm, D_MODEL), BF16)],
        compiler_params=_cparams(("arbitrary", "arbitrary")),
        name="in_proj",
    )(x_lat, x_ctx, mods, g, w_main, w_kt, w_gate_t, cos_t, sin_t)


def _split3(x):
    hi = x.astype(BF16).astype(F32)
    mid = (x - hi).astype(BF16).astype(F32)
    lo = (x - hi - mid).astype(BF16).astype(F32)
    return hi, mid, lo


def _mlstm_kernel(qf_ref, ktf_ref, vf_ref, gtf_ref, gtfn_ref, qb_ref, ktb_ref, vb_ref, gtb_ref, gtbn_ref, bias_ref,
                  hf_ref, hb_ref, *scratch):
    n_st = 2 * ML_HEADS
    cx_refs, bw_refs, pmw_refs = scratch[:n_st], scratch[n_st:2 * n_st], scratch[2 * n_st:3 * n_st]
    m_refs, c_refs, bend_refs, g_refs = (scratch[3 * n_st + 2 * k:3 * n_st + 2 * k + 2] for k in range(4))
    state_refs = cx_refs + m_refs
    s = pl.program_id(1)
    L = ML_CHUNK
    H = ML_HEADS
    W = LANES_V7X

    @pl.when(s == 0)
    def _():
        for ref in state_refs:
            ref[...] = jnp.zeros_like(ref)

    t_idx = lax.broadcasted_iota(jnp.int32, (L, L), 0)
    s_idx = lax.broadcasted_iota(jnp.int32, (L, L), 1)
    sub8 = lax.broadcasted_iota(jnp.int32, (8, W), 0)
    ones_w = jnp.ones((L, W), BF16)
    er = lax.broadcasted_iota(jnp.int32, (4 * H, 2 * W), 0)
    ec = lax.broadcasted_iota(jnp.int32, (4 * H, 2 * W), 1)

    def running_max_rows(x, d):
        n_tiles = L // 8
        out = [None] * n_tiles
        carry = None
        for j in (range(n_tiles) if d == 0 else range(n_tiles - 1, -1, -1)):
            r = x[8 * j:8 * (j + 1)]
            k = 1
            while k < 8:
                if d == 0:
                    r = jnp.maximum(r, jnp.where(sub8 >= k, pltpu.roll(r, k, 0), -jnp.inf))
                else:
                    r = jnp.maximum(r, jnp.where(sub8 < 8 - k, pltpu.roll(r, 8 - k, 0), -jnp.inf))
                k *= 2
            if carry is not None:
                r = jnp.maximum(r, carry)
            carry = jnp.broadcast_to(r[7:8] if d == 0 else r[0:1], (8, W))
            out[j] = r
        return jnp.concatenate(out, axis=0)

    def gate_part(d, gt_ref):
        before = (t_idx <= s_idx) if d == 0 else (t_idx >= s_idx)
        gt = gt_ref[2 * H * d:2 * H * (d + 1), :] + bias_ref[2 * H * d:2 * H * (d + 1), :]
        i4, f4 = gt[0:H], gt[H:2 * H]
        lf4 = jnp.minimum(f4, 0.0) - jnp.log1p(jnp.exp(-jnp.abs(f4)))
        lf_terms = jnp.concatenate(list(_split3(lf4)) + [jnp.zeros((H, L), F32)], axis=0).astype(BF16)
        cum_rhs = jnp.concatenate([jnp.where(before, 1.0, 0.0).astype(BF16), ones_w], axis=1)
        r = _dot(lf_terms, cum_rhs)
        bx = r[0:H] + r[H:2 * H] + r[2 * H:3 * H]
        b4, bend4 = bx[:, 0:L], bx[:, L:L + W]
        c4 = i4 - b4
        c_refs[d][...] = c4
        bend_refs[d][...] = bend4
        g_refs[d][...] = bend4[:, 0:1] - b4 + i4
        cols = jnp.concatenate(_split3(b4) + (c4.astype(BF16).astype(F32),), axis=0).T.astype(BF16)
        for h in range(H):
            pick = jnp.where((er % H == h) & ((er < 3 * H) == (ec < W)), 1.0, 0.0).astype(BF16)
            wide = _dot(cols, pick)
            bw_refs[d * H + h][...] = wide[:, 0:W]
            pmw_refs[d * H + h][...] = running_max_rows(wide[:, W:2 * W], d)

    dirs = ((0, qf_ref, ktf_ref, vf_ref, gtf_ref, gtfn_ref, hf_ref),
            (1, qb_ref, ktb_ref, vb_ref, gtb_ref, gtbn_ref, hb_ref))

    @pl.when(s == 0)
    def _():
        for d, _, _, _, gt_ref, _, _ in dirs:
            gate_part(d, gt_ref)

    for d, q_ref, kt_ref, v_ref, _, _, h_ref in dirs:
        causal = (s_idx <= t_idx) if d == 0 else (s_idx >= t_idx)
        c4, bend4, g4 = c_refs[d][...], bend_refs[d][...], g_refs[d][...]
        m_prev4 = m_refs[d][...]
        m_new4 = jnp.maximum(bend4 + m_prev4, jnp.max(g4, axis=1, keepdims=True))
        m_refs[d][...] = m_new4
        decay4 = jnp.exp(bend4 + m_prev4 - m_new4)
        w4 = jnp.exp(g4 - m_new4[:, 0:1])
        for h in range(H):
            st = d * H + h
            hs = slice(h * ML_DK, (h + 1) * ML_DK)
            b_w = bw_refs[st][...]
            m_w = jnp.maximum(pmw_refs[st][...], m_prev4[h:h + 1, :])

            q = q_ref[:, hs]
            kt = kt_ref[hs, :]
            vx = jnp.concatenate([v_ref[:, hs], ones_w], axis=1)
            m_ll = jnp.concatenate([m_w] * (L // W), axis=1)
            a = (jnp.where(causal, jnp.exp(c4[h:h + 1, :] - m_ll), 0.0) * _dot(q, kt)).astype(BF16)
            cx_prev = cx_refs[st][...]
            qc = _dot(q, cx_prev.astype(BF16))
            av = _dot(a, vx)
            sc_w = jnp.exp(m_prev4[h:h + 1, :] - m_w)
            den = sc_w * qc[:, ML_DK:] + av[:, ML_DK:]
            inv = 1.0 / jnp.maximum(jnp.abs(den), jnp.exp(-(b_w + m_w)))
            for t in range(ML_DK // W):
                ts = slice(t * W, (t + 1) * W)
                h_ref[:, h * ML_DK + t * W:h * ML_DK + (t + 1) * W] = (
                    (sc_w * qc[:, ts] + av[:, ts]) * inv).astype(h_ref.dtype)

            kw = (kt.astype(F32) * w4[h:h + 1, :]).astype(BF16)
            dec = jnp.concatenate([decay4[h:h + 1, :]] * (ML_DK // W + 1), axis=1)
            cx_refs[st][...] = dec * cx_prev + _dot(kw, vx)

    for d, _, _, _, _, gtn_ref, _ in dirs:
        gate_part(d, gtn_ref)


def _mlstm(u, kt, gates_t, gate_b, n_batch):
    n_rows = u.shape[0]
    L = ML_CHUNK
    lat_chunks = SEQ // L
    ctx_chunks = CTX_LEN // L
    n_steps = ctx_chunks + lat_chunks
    ctx_base = n_batch * lat_chunks

    def rowblk(d):
        def f(b, s):
            in_ctx = s < ctx_chunks
            if d == 0:
                c = jnp.where(in_ctx, s, s - ctx_chunks)
            else:
                c = jnp.where(in_ctx, ctx_chunks - 1 - s, lat_chunks - 1 - (s - ctx_chunks))
            return jnp.where(in_ctx, ctx_base + ctx_chunks * b, lat_chunks * b) + c
        return f

    def dir_specs(d):
        rb = rowblk(d)
        return [
            pl.BlockSpec((L, D_MODEL), lambda b, s: (rb(b, s), U_QML)),
            pl.BlockSpec((D_MODEL, L), lambda b, s: (0, rb(b, s))),
            pl.BlockSpec((L, D_MODEL), lambda b, s: (rb(b, s), U_VML)),
            pl.BlockSpec((N_GATE, L), lambda b, s: (0, rb(b, s))),
            pl.BlockSpec((N_GATE, L), lambda b, s: (0, rb(b, jnp.minimum(s + 1, n_steps - 1)))),
        ]

    def out_spec(d):
        rb = rowblk(d)
        return pl.BlockSpec((L, D_MODEL), lambda b, s: (rb(b, s), 0))

    n_st = 2 * ML_HEADS
    return pl.pallas_call(
        _mlstm_kernel,
        grid=(n_batch, n_steps),
        in_specs=dir_specs(0) + dir_specs(1) + [pl.BlockSpec((N_GATE, 1), lambda b, s: (0, 0))],
        out_specs=[out_spec(0), out_spec(1)],
        out_shape=[jax.ShapeDtypeStruct((n_rows, D_MODEL), BF16)] * 2,
        scratch_shapes=(
            [pltpu.VMEM((ML_DK, ML_DK + LANES_V7X), F32)] * n_st
            + [pltpu.VMEM((L, LANES_V7X), F32)] * (2 * n_st)
            + [pltpu.VMEM((ML_HEADS, LANES_V7X), F32)] * 2
            + [pltpu.VMEM((ML_HEADS, L), F32)] * 2
            + [pltpu.VMEM((ML_HEADS, LANES_V7X), F32)] * 2
            + [pltpu.VMEM((ML_HEADS, L), F32)] * 2),
        compiler_params=_cparams(("arbitrary", "arbitrary")),
        name="mlstm_scan",
    )(u, kt, u, gates_t, gates_t, u, kt, u, gates_t, gates_t, gate_b.reshape(N_GATE, 1))


def _attn_lambda(lam_ref, lam_init):
    lq = lam_ref[...]
    return (jnp.exp(jnp.sum(lq[0:1] * lq[1:2], axis=1, keepdims=True))
            - jnp.exp(jnp.sum(lq[2:3] * lq[3:4], axis=1, keepdims=True)) + lam_init)


def _attn_queries(q_ref):
    q = q_ref[...]
    lane = lax.broadcasted_iota(jnp.int32, (1, DA_DV), 1)
    zero = jnp.zeros_like(q)
    is_map0 = (lane % DA_DH) < (DA_DH // 2)
    q2 = jnp.concatenate([jnp.where(is_map0, q, zero), jnp.where(is_map0, zero, q)], axis=0)
    return q2 * (DA_DH ** -0.5)


def _attn_sums(s_chunks, m_chunks, vx):
    return jnp.concatenate([_dot(jnp.exp(s - m).astype(BF16), vx) for s, m in zip(s_chunks, m_chunks)], axis=0)


def _attn_finish(acc, lam, g, lam_init, tq):
    o0 = acc[0:tq, 0:DA_DV] * (1.0 / acc[0:tq, DA_DV:DA_DV + 1])
    o1 = acc[tq:, 0:DA_DV] * (1.0 / acc[tq:, DA_DV:DA_DV + 1])
    return _rms(o0 - lam * o1, g) * (1.0 - lam_init)


def _attn_kernel(qa_ref, kla_ref, kca_ref, qb_ref, klb_ref, kcb_ref, vl_ref, vc_ref, lam_ref, g_ref, o_ref,
                 vx_ref, s_ref, m_ref, *, lam_init, q_blocks):
    t = pl.program_id(0)
    tq = qa_ref.shape[0]
    rows = 2 * tq // ATTN_ROW_CHUNKS

    @pl.when(jnp.logical_and(t > 0, (2 * t - 2) % q_blocks == 0))
    def _():
        vx_ref[0:SEQ, 0:DA_DV] = vl_ref[...]
        vx_ref[SEQ:, 0:DA_DV] = vc_ref[...]

    @pl.when(t == 1)
    def _():
        lane_v = lax.broadcasted_iota(jnp.int32, (SEQ + CTX_LEN, DA_DV), 1)
        vx_ref[:, DA_DV:] = jnp.where(lane_v == 0, 1.0, 0.0).astype(BF16)

    def score(slot, q_ref, kl_ref, kc_ref):
        q2 = _attn_queries(q_ref)
        s_lat = _dot_nt(q2, kl_ref[...])
        s_ctx = _dot_nt(q2, kc_ref[...])
        m_ref[slot] = jnp.maximum(jnp.max(s_lat, axis=1, keepdims=True), jnp.max(s_ctx, axis=1, keepdims=True))
        s_ref[slot, :, 0:SEQ] = s_lat
        s_ref[slot, :, SEQ:] = s_ctx

    def drain(slot):
        s_chunks = [s_ref[slot, c * rows:(c + 1) * rows, :] for c in range(ATTN_ROW_CHUNKS)]
        m_chunks = [m_ref[slot, c * rows:(c + 1) * rows, :] for c in range(ATTN_ROW_CHUNKS)]
        acc = _attn_sums(s_chunks, m_chunks, vx_ref[...])
        o = _attn_finish(acc, _attn_lambda(lam_ref, lam_init), g_ref[...], lam_init, tq)
        o_ref[slot * tq:(slot + 1) * tq, :] = o.astype(o_ref.dtype)

    @pl.when(t > 0)
    def _():
        drain(0)
        score(1, qa_ref, kla_ref, kca_ref)

    @pl.when(t == 0)
    def _():
        score(0, qb_ref, klb_ref, kcb_ref)

    @pl.when(t > 0)
    def _():
        drain(1)
        score(0, qb_ref, klb_ref, kcb_ref)


def _attn_ctx_kernel(q_ref, kc_ref, vc_ref, lam_ref, g_ref, yin_ref, o_ref, *, lam_init):
    del yin_ref
    tq = q_ref.shape[0]
    s = _dot_nt(_attn_queries(q_ref), kc_ref[...])
    lane_v = lax.broadcasted_iota(jnp.int32, (CTX_LEN, DA_DV), 1)
    vx = jnp.concatenate([vc_ref[...], jnp.where(lane_v == 0, 1.0, 0.0).astype(BF16)], axis=1)
    acc = _attn_sums([s], [jnp.max(s, axis=1, keepdims=True)], vx)
    o = _attn_finish(acc, _attn_lambda(lam_ref, lam_init), g_ref[...], lam_init, tq)
    o_ref[...] = o.astype(o_ref.dtype)


def _attn(u, da_lam, head_g, lam_init, need_ctx, n_batch):
    n_rows = u.shape[0]
    tq = ATTN_TQ
    lat_qblocks = SEQ // tq
    ctx_base = n_batch * (SEQ // CTX_LEN)
    cpb = D_MODEL // DA_DV
    n_keys = SEQ + CTX_LEN

    n_items = n_batch * DA_HEADS * lat_qblocks
    assert lat_qblocks % 2 == 0

    def item(k):
        k = jnp.clip(k, 0, n_items - 1)
        return k // (DA_HEADS * lat_qblocks), (k // lat_qblocks) % DA_HEADS, k % lat_qblocks

    def of_item(offset, f):
        return lambda t: f(*item(2 * t + offset))

    def scored_specs(offset):
        return [
            pl.BlockSpec((tq, DA_DV), of_item(offset, lambda b, h, qi: (b * lat_qblocks + qi, U_QDA * cpb + h))),
            pl.BlockSpec((SEQ, DA_DV), of_item(offset, lambda b, h, qi: (b, U_KDA * cpb + h))),
            pl.BlockSpec((CTX_LEN, DA_DV), of_item(offset, lambda b, h, qi: (ctx_base + b, U_KDA * cpb + h))),
        ]

    kern = functools.partial(_attn_kernel, lam_init=lam_init, q_blocks=lat_qblocks)
    yd = pl.pallas_call(
        kern,
        grid=(n_items // 2 + 1,),
        in_specs=scored_specs(-1) + scored_specs(0) + [
            pl.BlockSpec((SEQ, DA_DV), of_item(-2, lambda b, h, qi: (b, U_VDA * cpb + h))),
            pl.BlockSpec((CTX_LEN, DA_DV), of_item(-2, lambda b, h, qi: (ctx_base + b, U_VDA * cpb + h))),
            pl.BlockSpec((4, DA_DH), lambda t: (0, 0)),
            pl.BlockSpec((1, DA_DV), of_item(-2, lambda b, h, qi: (0, h))),
        ],
        out_specs=pl.BlockSpec((2 * tq, DA_DV),
                               of_item(-2, lambda b, h, qi: ((b * lat_qblocks + qi) // 2, h))),
        out_shape=jax.ShapeDtypeStruct((n_rows, D_MODEL), BF16),
        scratch_shapes=[
            pltpu.VMEM((n_keys, 2 * DA_DV), BF16),
            pltpu.VMEM((2, 2 * tq, n_keys), F32),
            pltpu.VMEM((2, 2 * tq, 1), F32),
        ],
        compiler_params=_cparams(("arbitrary",)),
        name="diff_attn",
    )(u, u, u, u, u, u, u, u, da_lam, head_g)
    if not need_ctx:
        return yd
    kern_ctx = functools.partial(_attn_ctx_kernel, lam_init=lam_init)
    return pl.pallas_call(
        kern_ctx,
        grid=(n_batch, DA_HEADS),
        in_specs=[
            pl.BlockSpec((CTX_LEN, DA_DV), lambda b, h: (ctx_base + b, U_QDA * cpb + h)),
            pl.BlockSpec((CTX_LEN, DA_DV), lambda b, h: (ctx_base + b, U_KDA * cpb + h)),
            pl.BlockSpec((CTX_LEN, DA_DV), lambda b, h: (ctx_base + b, U_VDA * cpb + h)),
            pl.BlockSpec((4, DA_DH), lambda b, h: (0, 0)),
            pl.BlockSpec((1, DA_DV), lambda b, h: (0, h)),
            pl.BlockSpec(memory_space=pl.ANY),
        ],
        out_specs=pl.BlockSpec((CTX_LEN, DA_DV), lambda b, h: (ctx_base + b, h)),
        out_shape=jax.ShapeDtypeStruct((n_rows, D_MODEL), BF16),
        input_output_aliases={5: 0},
        compiler_params=_cparams(("arbitrary", "arbitrary")),
        name="diff_attn_ctx",
    )(u, u, u, da_lam, head_g, yd)


def _dft_tables():
    R = FFT_R

    def cs(num, period):
        ang = (num % period).astype(np.float32) * np.float32(2.0 * math.pi / period)
        return np.cos(ang), np.sin(ang)

    def bf16(a):
        return jnp.asarray(a.astype(BF16))

    idx = np.arange(R, dtype=np.int32)
    c1, s1 = cs(idx[:, None] * idx[None, :], R)
    a1 = bf16(np.concatenate([c1, -s1], axis=0))
    f2 = idx[:, None, None]
    f1 = idx[None, :, None]
    t1 = idx[None, None, :]
    mc, ms = cs(t1 * (R * f1 + f2), SEQ)
    b2 = bf16(np.concatenate([np.concatenate([mc, ms], axis=2),
                              np.concatenate([-ms, mc], axis=2)], axis=1))
    ch = np.arange(FN_GC, dtype=np.int32)
    cc, sc = cs(ch[:, None] * ch[None, :], FN_GC)
    csm = bf16(np.concatenate([cc, sc], axis=0))
    actx = bf16(np.concatenate([cc, -sc], axis=0))
    return a1, b2, csm, actx


def _fft1_kernel(a_ref, z_ref, p_ref):
    p_ref[0, 0] = _dot(a_ref[...], z_ref[0]).astype(p_ref.dtype)


def _fft1(fn_view, a1, n_batch):
    R = FFT_R
    lanes = R * FN_GC
    lc = lanes
    return pl.pallas_call(
        _fft1_kernel,
        grid=(FN_GROUPS, n_batch, lanes // lc),
        in_specs=[
            pl.BlockSpec((2 * R, R), lambda g, b, c: (0, 0)),
            pl.BlockSpec((1, R, lc), lambda g, b, c: (g, b, c)),
        ],
        out_specs=pl.BlockSpec((1, 1, 2 * R, lc), lambda g, b, c: (g, b, 0, c)),
        out_shape=jax.ShapeDtypeStruct((FN_GROUPS, n_batch, 2 * R, lanes), BF16),
        compiler_params=_cparams(("arbitrary", "arbitrary", "arbitrary")),
        name="fourier_stage1",
    )(a1, fn_view)


def _fft2_kernel(p_ref, b_ref, cs_ref, o_ref, *, f2b):
    R = FFT_R
    cc = cs_ref[0:FN_GC, :]
    sc = cs_ref[FN_GC:2 * FN_GC, :]
    norm = 1.0 / math.sqrt(SEQ * FN_GC)
    for g in range(FN_GROUPS):
        xr, xi = [], []
        for jj in range(f2b):
            stacked = jnp.concatenate([p_ref[g, 0, 0, jj], p_ref[g, 0, 1, jj]], axis=0)
            x = _dot(b_ref[jj], stacked)
            xr.append(x[0:R])
            xi.append(x[R:2 * R])
        y = (_dot(jnp.concatenate(xr, axis=0).astype(BF16), cc)
             + _dot(jnp.concatenate(xi, axis=0).astype(BF16), sc)) * norm
        for jj in range(f2b):
            lo = jj * D_MODEL + g * FN_GC
            o_ref[:, lo:lo + FN_GC] = y[jj * R:(jj + 1) * R].astype(o_ref.dtype)


def _fft2(p6, b2, csm, n_rows, n_batch):
    R = FFT_R
    f2b = 16
    kern = functools.partial(_fft2_kernel, f2b=f2b)
    return pl.pallas_call(
        kern,
        grid=(n_batch, R // f2b),
        in_specs=[
            pl.BlockSpec((FN_GROUPS, 1, 2, f2b, R, FN_GC), lambda b, f: (0, b, 0, f, 0, 0)),
            pl.BlockSpec((f2b, 2 * R, 2 * R), lambda b, f: (f, 0, 0)),
            pl.BlockSpec((2 * FN_GC, FN_GC), lambda b, f: (0, 0)),
        ],
        out_specs=pl.BlockSpec((R, f2b * D_MODEL), lambda b, f: (b, f)),
        out_shape=jax.ShapeDtypeStruct((n_rows // R, R * D_MODEL), BF16),
        compiler_params=_cparams(("arbitrary", "arbitrary")),
        name="fourier_stage2",
    )(p6, b2, csm)


def _fft_ctx_kernel(z_ref, a_ref, cs_ref, yin_ref, o_ref, *, n_batch):
    del yin_ref
    cc = cs_ref[0:FN_GC, :]
    sc = cs_ref[FN_GC:2 * FN_GC, :]
    norm = 1.0 / math.sqrt(CTX_LEN * FN_GC)
    n_ctx = n_batch * CTX_LEN
    groups = []
    for g in range(FN_GROUPS):
        z_all = z_ref[g].astype(F32).reshape(n_ctx, FN_GC)
        ys = []
        for b in range(n_batch):
            p = _dot(a_ref[...], z_all[b * CTX_LEN:(b + 1) * CTX_LEN].astype(BF16))
            ys.append(_dot(p[0:CTX_LEN].astype(BF16), cc) + _dot(p[CTX_LEN:2 * CTX_LEN].astype(BF16), sc))
        groups.append(jnp.concatenate(ys, axis=0))
    y = jnp.concatenate(groups, axis=1) * norm
    o_ref[...] = y.reshape(n_ctx // FFT_R, FFT_R * D_MODEL).astype(o_ref.dtype)


def _fft_ctx(fn_view, actx, csm, yf, n_batch):
    n_ctx = n_batch * CTX_LEN
    ctx_blk = n_batch * SEQ // n_ctx
    kern = functools.partial(_fft_ctx_kernel, n_batch=n_batch)
    return pl.pallas_call(
        kern,
        grid=(1,),
        in_specs=[
            pl.BlockSpec((FN_GROUPS, n_ctx // FFT_R, FFT_R * FN_GC), lambda i: (0, ctx_blk, 0)),
            pl.BlockSpec((2 * CTX_LEN, CTX_LEN), lambda i: (0, 0)),
            pl.BlockSpec((2 * FN_GC, FN_GC), lambda i: (0, 0)),
            pl.BlockSpec(memory_space=pl.ANY),
        ],
        out_specs=pl.BlockSpec((n_ctx // FFT_R, FFT_R * D_MODEL), lambda i: (ctx_blk, 0)),
        out_shape=jax.ShapeDtypeStruct(yf.shape, BF16),
        input_output_aliases={3: 0},
        compiler_params=_cparams(("arbitrary",)),
        name="fourier_ctx",
    )(fn_view, actx, csm, yf)


def _fourier(fn, tables, need_ctx, n_batch):
    a1, b2, csm, actx = tables
    R = FFT_R
    n_rows = fn.shape[1] * R
    p = _fft1(fn, a1, n_batch)
    yf = _fft2(p.reshape(FN_GROUPS, n_batch, 2, R, R, FN_GC), b2, csm, n_rows, n_batch)
    if need_ctx:
        yf = _fft_ctx(fn, actx, csm, yf, n_batch)
    return yf


def _merge_kernel(hf_ref, hb_ref, o_ref, gm_ref, gd_ref, gf_ref, yd_ref, yf_ref, xl_ref, xc_ref, mod_ref, hg_ref,
                  wml_ref, wda_ref, wfn_ref, wout_ref, out_ref, *, rows_per_batch, n_batch):
    i = pl.program_id(0)
    x = jnp.where(i < rows_per_batch * n_batch, xl_ref[...], xc_ref[...])
    hsum = hf_ref[...].astype(F32) + hb_ref[...].astype(F32)
    hg = hg_ref[...]
    parts = []
    for h in range(ML_HEADS):
        hs = slice(h * ML_DK, (h + 1) * ML_DK)
        parts.append(_rms(hsum[:, hs], hg[:, hs]))
    ym = (jnp.concatenate(parts, axis=1) * jax.nn.sigmoid(o_ref[...].astype(F32))).astype(BF16)
    yf = yf_ref[...].astype(F32).reshape(x.shape).astype(BF16)
    y = (jax.nn.sigmoid(gm_ref[...].astype(F32)) * _dot(ym, wml_ref[...])
         + jax.nn.sigmoid(gd_ref[...].astype(F32)) * _dot(yd_ref[...], wda_ref[...])
         + jax.nn.sigmoid(gf_ref[...].astype(F32)) * _dot(yf, wfn_ref[...]))
    gate = _mod_row(mod_ref, i, rows_per_batch, n_batch, 2)
    out_ref[...] = x + gate * _dot(y.astype(BF16), wout_ref[...])


def _merge(hf, hb, u, yd, yf, x_lat, x_ctx, ctx_blk0, mods, head_g, wml, wda, wfn, wout, layer, need_ctx, n_batch):
    n_rows = n_batch * (SEQ + CTX_LEN)
    tm = _row_tile(n_batch)
    rows_per_batch = SEQ // tm
    lat_blocks = n_batch * rows_per_batch
    ni = (n_rows if need_ctx else n_batch * SEQ) // tm
    kern = functools.partial(_merge_kernel, rows_per_batch=rows_per_batch, n_batch=n_batch)
    row = lambda i: (i, 0)
    full = lambda i: (0, 0)
    wspec = pl.BlockSpec((None, D_MODEL, D_MODEL), lambda i: (layer, 0, 0))
    return pl.pallas_call(
        kern,
        grid=(ni,),
        in_specs=[
            pl.BlockSpec((tm, D_MODEL), row),
            pl.BlockSpec((tm, D_MODEL), row),
            pl.BlockSpec((tm, D_MODEL), lambda i: (i, U_OML)),
            pl.BlockSpec((tm, D_MODEL), lambda i: (i, U_GPRE)),
            pl.BlockSpec((tm, D_MODEL), lambda i: (i, U_GPRE + 1)),
            pl.BlockSpec((tm, D_MODEL), lambda i: (i, U_GPRE + 2)),
            pl.BlockSpec((tm, D_MODEL), row),
            pl.BlockSpec((tm // FFT_R, FFT_R * D_MODEL), row),
            pl.BlockSpec((tm, D_MODEL), lambda i: (jnp.minimum(i, lat_blocks - 1), 0)),
            pl.BlockSpec((tm, D_MODEL), lambda i: (ctx_blk0 + jnp.maximum(i - lat_blocks, 0), 0)),
            pl.BlockSpec((None, 8, N_MOD), lambda i: (layer, 0, 0)),
            pl.BlockSpec((1, D_MODEL), full),
            wspec, wspec, wspec, wspec,
        ],
        out_specs=pl.BlockSpec((tm, D_MODEL), row),
        out_shape=jax.ShapeDtypeStruct((ni * tm, D_MODEL), F32),
        compiler_params=_cparams(("arbitrary",)),
        name="merge_out_proj",
    )(hf, hb, u, u, u, u, yd, yf, x_lat, x_ctx, mods, head_g, wml, wda, wfn, wout)


FFN_CHUNKS = ((0, 1024), (1024, 1024), (2048, 768))


def _ffn_kernel(x_ref, mod_ref, g_ref, win_ref, wout_ref, fg_ref, out_ref, *, rows_per_batch, n_batch, final):
    i = pl.program_id(0)
    x = x_ref[...]
    shift = _mod_row(mod_ref, i, rows_per_batch, n_batch, 3)
    scale = _mod_row(mod_ref, i, rows_per_batch, n_batch, 4)
    gate = _mod_row(mod_ref, i, rows_per_batch, n_batch, 5)
    h = (_rms(x, g_ref[...]) * (1.0 + scale) + shift).astype(BF16)
    acc = None
    for lo, width in FFN_CHUNKS:
        a = _dot(h, win_ref[:, lo:lo + width])
        b = _dot(h, win_ref[:, D_FF + lo:D_FF + lo + width])
        act = (a * jax.nn.sigmoid(a) * b).astype(BF16)
        part = _dot(act, wout_ref[lo:lo + width, :])
        acc = part if acc is None else acc + part
    xn = x + gate * acc
    out_ref[...] = _rms(xn, fg_ref[...]) if final else xn


def _ffn(x, mods, g, w_in, w_out, layer, final_g, final, n_rows_out, n_batch):
    tm = n_batch * CTX_LEN
    rows_per_batch = SEQ // tm
    resident = pl.Buffered(1)
    kern = functools.partial(_ffn_kernel, rows_per_batch=rows_per_batch, n_batch=n_batch, final=final)
    row = lambda i: (i, 0)
    full = lambda i: (0, 0)
    return pl.pallas_call(
        kern,
        grid=(n_rows_out // tm,),
        in_specs=[
            pl.BlockSpec((tm, D_MODEL), row),
            pl.BlockSpec((None, 8, N_MOD), lambda i: (layer, 0, 0)),
            pl.BlockSpec((1, D_MODEL), full),
            pl.BlockSpec((None, D_MODEL, 2 * D_FF), lambda i: (layer, 0, 0), pipeline_mode=resident),
            pl.BlockSpec((None, D_FF, D_MODEL), lambda i: (layer, 0, 0), pipeline_mode=resident),
            pl.BlockSpec((1, D_MODEL), full),
        ],
        out_specs=pl.BlockSpec((tm, D_MODEL), row),
        out_shape=jax.ShapeDtypeStruct((n_rows_out, D_MODEL), F32),
        compiler_params=_cparams(("arbitrary",)),
        name="swiglu_ffn",
    )(x, mods, g, w_in, w_out, final_g)


def _da_col_perm(w):
    half = DA_DH // 2
    lead = w.shape[:-1]
    return jnp.swapaxes(w.reshape(lead + (DA_HEADS, 2, 2, half)), -3, -2).reshape(lead + (DA_HEADS * DA_DV,))


def _rope_tables(pad):
    f32 = np.float32
    n_freq = DA_DH // 4
    rows = SEQ // GRID_W
    inv = f32(ROPE_BASE) ** (-np.arange(n_freq, dtype=f32) / f32(n_freq))
    r = np.repeat(np.arange(rows, dtype=f32), GRID_W)
    col = np.tile(np.arange(GRID_W, dtype=f32), rows)
    ang = np.concatenate([r[:, None] * inv, col[:, None] * inv], axis=-1).astype(f32)
    cos, sin = np.cos(ang), np.sin(ang)
    cos_t = np.concatenate([cos, cos, cos, cos], axis=-1)
    sin_t = np.concatenate([-sin, -sin, sin, sin], axis=-1)
    cos_t = np.concatenate([cos_t, np.ones((pad, DA_DV), f32)], axis=0)
    sin_t = np.concatenate([sin_t, np.zeros((pad, DA_DV), f32)], axis=0)
    return jnp.asarray(cos_t, F32), jnp.asarray(sin_t, F32)


def kernel(x, c, ctx, c_ctx, w_ada, b_ada, norm_g, w_in, ml_gate_b, ml_head_g, da_lam, da_head_g,
           w_br_ml, w_br_da, w_br_fn, w_out, w_ffn_in, w_ffn_out, final_g):
    n_batch = x.shape[0]
    n_lat = n_batch * SEQ
    x_lat = x.reshape(n_lat, D_MODEL)
    x_ctx = ctx.reshape(n_batch * CTX_LEN, D_MODEL)
    cc = jnp.concatenate([c, c_ctx[None, :], jnp.zeros((8 - n_batch - 1, D_MODEL), F32)], axis=0)
    mods = _mods(cc, w_ada, b_ada)
    cos_t, sin_t = _rope_tables(n_batch * CTX_LEN)
    tables = _dft_tables()
    final_g2 = final_g.reshape(1, D_MODEL)
    wb_ml, wb_da, wb_fn, wb_out = _to_bf16(w_br_ml, w_br_da, w_br_fn, w_out)
    (wb_ffn_in,), (wb_ffn_out,) = _to_bf16(w_ffn_in), _to_bf16(w_ffn_out)

    gate_lo = 4 * D_MODEL
    da_lo = gate_lo + N_GATE
    fn_lo = da_lo + 3 * D_MODEL
    w_main = jnp.concatenate([w_in[..., :D_MODEL], w_in[..., 2 * D_MODEL:gate_lo],
                              _da_col_perm(w_in[..., da_lo:da_lo + D_MODEL]),
                              _da_col_perm(w_in[..., da_lo + D_MODEL:da_lo + 2 * D_MODEL]),
                              w_in[..., da_lo + 2 * D_MODEL:fn_lo],
                              w_in[..., fn_lo + D_MODEL:], w_in[..., fn_lo:fn_lo + D_MODEL]], axis=-1).astype(BF16)
    w_kt = jnp.swapaxes(w_in[..., D_MODEL:2 * D_MODEL], 1, 2).astype(BF16)
    w_gate_t = jnp.swapaxes(w_in[..., gate_lo:da_lo], 1, 2).astype(BF16)

    tm_in = n_batch * CTX_LEN
    tm_tok = _row_tile(n_batch)
    xs = None
    for l in range(DEPTH):
        need_ctx = l < DEPTH - 1
        lam_init = 0.8 - 0.6 * math.exp(-0.3 * l)
        xl, xc = (x_lat, x_ctx) if xs is None else (xs, xs)
        u, kt, fn, gates_t = _inproj(xl, xc, 0 if xs is None else n_lat // tm_in, mods,
                                     norm_g[l, 0].reshape(1, D_MODEL), w_main, w_kt, w_gate_t, l,
                                     cos_t, sin_t, n_batch)
        hf, hb = _mlstm(u, kt, gates_t, ml_gate_b[l], n_batch)
        yd = _attn(u, da_lam[l], da_head_g[l].reshape(1, D_MODEL), lam_init, need_ctx, n_batch)
        yf = _fourier(fn, tables, need_ctx, n_batch)
        xs = _merge(hf, hb, u, yd, yf, xl, xc, 0 if xs is None else n_lat // tm_tok, mods,
                    ml_head_g[l].reshape(1, D_MODEL), wb_ml, wb_da, wb_fn, wb_out, l, need_ctx, n_batch)
        final = l == DEPTH - 1
        n_out = n_lat if final else xs.shape[0]
        xs = _ffn(xs, mods, norm_g[l, 1].reshape(1, D_MODEL), wb_ffn_in, wb_ffn_out, l,
                  final_g2, final, n_out, n_batch)
    return xs.reshape(n_batch, SEQ, D_MODEL)
```
